```python
import math
import jax
import jax.numpy as jnp
from jax import lax
import numpy as np

D_MODEL = 1024
BATCH = 2
SEQ = 8192
DEPTH = 4

GRID_W = 64
CTX_LEN = 256
HEAD_DIM = 64
N_HEADS = D_MODEL // 128
N_KV_HEADS = N_HEADS // 4
GQA_GROUP = N_HEADS // N_KV_HEADS
ATTN_WIDTH = N_HEADS * HEAD_DIM
KV_WIDTH = N_KV_HEADS * HEAD_DIM
WINDOW = 128
BLOCK = 128
ROPE_THETA = 10000.0
HYENA_ORDER = 2
HYENA_WIDTH = D_MODEL // 4
SHORT_CONV = 3
FILTER_BANDS = 16
FILTER_EMB = 2 * FILTER_BANDS + 1
FILTER_HIDDEN = 64
DECAY_TARGET = 1e-2
FAST_DECAY_PCT = 0.3
SLOW_DECAY_PCT = 1.5
FNET_WIDTH = D_MODEL // 4
FNET_GROUPS = 4
FNET_GROUP_DIM = FNET_WIDTH // FNET_GROUPS
N_BRANCHES = 3
Q_END = ATTN_WIDTH
K_END = Q_END + KV_WIDTH
V_END = K_END + KV_WIDTH
HY_END = V_END + (HYENA_ORDER + 1) * HYENA_WIDTH
FN_END = HY_END + FNET_WIDTH
IN_WIDTH = FN_END + N_BRANCHES * D_MODEL
D_FF = 256 * ((8 * D_MODEL // 3 + 255) // 256)
N_EXPERTS = 8
TOP_K = 2
EPS = 1e-6

kernel_name = "hybrid_dit_swa_hyena_fnet_moe"


def rms_norm(x, g):
    xf = x.astype(jnp.float32)
    y = xf * lax.rsqrt(jnp.mean(jnp.square(xf), axis=-1, keepdims=True) + EPS)
    return (y * g.astype(jnp.float32)).astype(x.dtype)


def modulate(x, g, shift, scale):
    return rms_norm(x, g) * (1 + scale[:, None, :]) + shift[:, None, :]


def adaln(cond, w, b, i):
    lo, hi = 3 * i * D_MODEL, 3 * (i + 1) * D_MODEL
    mod = jax.nn.silu(cond) @ w[:, lo:hi] + b[lo:hi]
    return jnp.split(mod, 3, axis=-1)


def axial_rope_tables(L):
    rows = L // GRID_W
    r, col = jnp.meshgrid(jnp.arange(rows), jnp.arange(GRID_W), indexing="ij")
    pos = jnp.stack([r.reshape(-1), col.reshape(-1)], axis=-1).astype(jnp.float32)
    n_freq = HEAD_DIM // 4
    inv = ROPE_THETA ** (-jnp.arange(n_freq, dtype=jnp.float32) / n_freq)
    ang = pos[:, :, None] * inv
    return jnp.cos(ang), jnp.sin(ang)


def apply_axial_rope(x, cos, sin):
    B, L, H, _ = x.shape
    xf = x.astype(jnp.float32).reshape(B, L, H, 2, 2, HEAD_DIM // 4)
    x1, x2 = xf[..., 0, :], xf[..., 1, :]
    c, s = cos[None, :, None], sin[None, :, None]
    out = jnp.stack([x1 * c - x2 * s, x1 * s + x2 * c], axis=-2)
    return out.reshape(B, L, H, HEAD_DIM).astype(x.dtype)


def q_heads(pr, q_gain):
    B, L, _ = pr.shape
    return rms_norm(pr[..., :Q_END].reshape(B, L, N_HEADS, HEAD_DIM), q_gain)


def kv_heads(pr_kv, k_gain):
    B, L, _ = pr_kv.shape
    k = rms_norm(pr_kv[..., :KV_WIDTH].reshape(B, L, N_KV_HEADS, HEAD_DIM), k_gain)
    v = pr_kv[..., KV_WIDTH:].reshape(B, L, N_KV_HEADS, HEAD_DIM)
    return k, v


def softmax_with_sink(s, sink):
    m = jnp.maximum(jnp.max(s, axis=-1, keepdims=True), sink)
    p = jnp.exp(s - m)
    return p / (jnp.sum(p, axis=-1, keepdims=True) + jnp.exp(sink - m))


def latent_window_attention(q, k, v, k_ctx, v_ctx, sink):
    B, L, _, _ = q.shape
    nb = L // BLOCK
    qb = q.reshape(B, nb, BLOCK, N_KV_HEADS, GQA_GROUP, HEAD_DIM)

    def band(t):
        tp = jnp.pad(t, ((0, 0), (BLOCK, BLOCK), (0, 0), (0, 0)))
        tp = tp.reshape(B, nb + 2, BLOCK, N_KV_HEADS, HEAD_DIM)
        return jnp.concatenate([tp[:, :-2], tp[:, 1:-1], tp[:, 2:]], axis=2)

    kb, vb = band(k), band(v)
    scale = HEAD_DIM ** -0.5
    s_loc = jnp.einsum("bnqhgd,bnjhd->bnhgqj", qb, kb,
                       preferred_element_type=jnp.float32) * scale
    qpos = jnp.arange(nb)[:, None, None] * BLOCK + jnp.arange(BLOCK)[None, :, None]
    kpos = (jnp.arange(nb)[:, None, None] - 1) * BLOCK + jnp.arange(3 * BLOCK)[None, None, :]
    mask = (jnp.abs(qpos - kpos) <= WINDOW) & (kpos >= 0) & (kpos < L)
    s_loc = jnp.where(mask[None, :, None, None], s_loc, -jnp.inf)
    s_ctx = jnp.einsum("bnqhgd,bjhd->bnhgqj", qb, k_ctx,
                       preferred_element_type=jnp.float32) * scale
    s = jnp.concatenate([s_loc, s_ctx], axis=-1)
    sk = sink.astype(jnp.float32).reshape(N_KV_HEADS, GQA_GROUP)[None, None, :, :, None, None]
    p = softmax_with_sink(s, sk).astype(v.dtype)
    n_loc = 3 * BLOCK
    o = (jnp.einsum("bnhgqj,bnjhd->bnqhgd", p[..., :n_loc], vb)
         + jnp.einsum("bnhgqj,bjhd->bnqhgd", p[..., n_loc:], v_ctx))
    return o.reshape(B, L, ATTN_WIDTH)


def context_attention(q, k, v, sink):
    B, C, _, _ = q.shape
    qg = q.reshape(B, C, N_KV_HEADS, GQA_GROUP, HEAD_DIM)
    s = jnp.einsum("bqhgd,bjhd->bhgqj", qg, k,
                   preferred_element_type=jnp.float32) * HEAD_DIM ** -0.5
    sk = sink.astype(jnp.float32).reshape(N_KV_HEADS, GQA_GROUP)[None, :, :, None, None]
    p = softmax_with_sink(s, sk).astype(v.dtype)
    o = jnp.einsum("bhgqj,bjhd->bqhgd", p, v)
    return o.reshape(B, C, ATTN_WIDTH)


def hyena_filter_spectrum(L, w1, b1, fr1, w2, b2, fr2, w3):
    f32 = jnp.float32
    t = jnp.linspace(0.0, 1.0, L, dtype=f32)[:, None]
    w = 2.0 * math.pi * jnp.arange(L, dtype=f32)[:, None] / L
    fb = jnp.linspace(1e-4, FILTER_BANDS - 1, FILTER_BANDS, dtype=f32)
    z = jnp.concatenate([t, jnp.cos(fb * w), -jnp.sin(fb * w)], axis=-1)
    h = jnp.sin(fr1.astype(f32) * (z @ w1.astype(f32) + b1.astype(f32)))
    h = jnp.sin(fr2.astype(f32) * (h @ w2.astype(f32) + b2.astype(f32)))
    h = (h @ w3.astype(f32)).reshape(L, 2, HYENA_ORDER, HYENA_WIDTH)
    deltas = jnp.abs(jnp.linspace(math.log(DECAY_TARGET) / SLOW_DECAY_PCT,
                                  math.log(DECAY_TARGET) / FAST_DECAY_PCT,
                                  HYENA_WIDTH, dtype=f32))
    h = h * jnp.exp(-t * deltas)[:, None, None, :]
    k = jnp.concatenate([h[:, 0], jnp.zeros_like(h[:1, 0]), h[:0:-1, 1]], axis=0)
    k = k / jnp.sum(jnp.abs(k), axis=0, keepdims=True)
    return jnp.fft.rfft(k, axis=0)


def hyena_mixer(u, conv_w, conv_b, k_spec, bias):
    L = u.shape[1]
    uc = lax.conv_general_dilated(u, conv_w, window_strides=(1,),
                                  padding=((SHORT_CONV // 2, SHORT_CONV // 2),),
                                  dimension_numbers=("NWC", "WIO", "NWC"),
                                  feature_group_count=u.shape[-1]) + conv_b
    *gates, z = jnp.split(uc.astype(jnp.float32), HYENA_ORDER + 1, axis=-1)
    for o, gate in enumerate(gates):
        zc = jnp.fft.irfft(jnp.fft.rfft(z, n=2 * L, axis=1) * k_spec[None, :, o],
                           n=2 * L, axis=1)[:, :L]
        z = gate * (zc + bias[o].astype(jnp.float32) * z)
    return z.astype(u.dtype)


def fnet_mixer(u):
    B, L, _ = u.shape
    ug = u.astype(jnp.float32).reshape(B, L, FNET_GROUPS, FNET_GROUP_DIM).transpose(0, 2, 1, 3)
    y = jnp.fft.fft2(ug, norm="ortho").real
    return y.transpose(0, 2, 1, 3).reshape(B, L, FNET_WIDTH).astype(u.dtype)


def merge_branches(pr, attn_o, hy_o, fn_o, w_pa, w_ph, w_pf, w_o):
    B, L, _ = pr.shape
    g = jax.nn.sigmoid(pr[..., FN_END:].reshape(B, L, N_BRANCHES, D_MODEL))
    m = (g[:, :, 0] * (attn_o @ w_pa) + g[:, :, 1] * (hy_o @ w_ph) + g[:, :, 2] * (fn_o @ w_pf))
    return m @ w_o


def swiglu(h, wg, wu, wd):
    return (jax.nn.silu(h @ wg) * (h @ wu)) @ wd


def moe_swiglu(h, w_router, wg, wu, wd):
    logits = jnp.einsum("bld,de->ble", h, w_router).astype(jnp.float32)
    top_val, top_idx = lax.top_k(logits, TOP_K)
    top_w = jax.nn.softmax(top_val, axis=-1)
    combine = jnp.einsum("blk,blke->ble", top_w,
                         jax.nn.one_hot(top_idx, N_EXPERTS, dtype=jnp.float32)).astype(h.dtype)
    out = jnp.zeros_like(h)
    for e in range(N_EXPERTS):
        out = out + combine[..., e:e + 1] * swiglu(h, wg[e], wu[e], wd[e])
    return out


def setup_inputs(seed: int = 0) -> dict:
    key = jax.random.key(seed)
    ks = iter(jax.random.split(key, 40))

    def nrm(shape, scale=1.0):
        return jax.random.normal(next(ks), shape, jnp.float32) * scale

    D, W, E, F = D_MODEL, HYENA_WIDTH, N_EXPERTS, D_FF
    n_dense, n_moe = (DEPTH + 1) // 2, DEPTH // 2
    return {
        "x": nrm((BATCH, SEQ, D)),
        "c": nrm((BATCH, D)),
        "ctx": nrm((BATCH, CTX_LEN, D)),
        "c_ctx": nrm((D,)),
        "w_ada": nrm((DEPTH, D, 6 * D), 0.5 * D ** -0.5),
        "b_ada": nrm((DEPTH, 6 * D), 0.01),
        "norm1_g": 1.0 + nrm((DEPTH, D), 0.02),
        "norm2_g": 1.0 + nrm((DEPTH, D), 0.02),
        "w_in": nrm((DEPTH, D, IN_WIDTH), D ** -0.5),
        "q_norm_g": 1.0 + nrm((DEPTH, HEAD_DIM), 0.02),
        "k_norm_g": 1.0 + nrm((DEPTH, HEAD_DIM), 0.02),
        "attn_sink": nrm((DEPTH, N_HEADS), 0.5),
        "hy_conv_w": nrm((DEPTH, SHORT_CONV, 1, (HYENA_ORDER + 1) * W), SHORT_CONV ** -0.5),
        "hy_conv_b": nrm((DEPTH, (HYENA_ORDER + 1) * W), 0.02),
        "hy_filt_w1": nrm((DEPTH, FILTER_EMB, FILTER_HIDDEN), FILTER_EMB ** -0.5),
        "hy_filt_b1": nrm((DEPTH, FILTER_HIDDEN), 0.1),
        "hy_filt_freq1": 1.0 + nrm((DEPTH, FILTER_HIDDEN), 0.02),
        "hy_filt_w2": nrm((DEPTH, FILTER_HIDDEN, FILTER_HIDDEN), FILTER_HIDDEN ** -0.5),
        "hy_filt_b2": nrm((DEPTH, FILTER_HIDDEN), 0.1),
        "hy_filt_freq2": 1.0 + nrm((DEPTH, FILTER_HIDDEN), 0.02),
        "hy_filt_w3": nrm((DEPTH, FILTER_HIDDEN, 2 * HYENA_ORDER * W), FILTER_HIDDEN ** -0.5),
        "hy_bias": nrm((DEPTH, HYENA_ORDER, W)),
        "w_proj_attn": nrm((DEPTH, ATTN_WIDTH, D), ATTN_WIDTH ** -0.5),
        "w_proj_hyena": nrm((DEPTH, W, D), W ** -0.5),
        "w_proj_fnet": nrm((DEPTH, FNET_WIDTH, D), FNET_WIDTH ** -0.5),
        "w_out": nrm((DEPTH, D, D), D ** -0.5),
        "ffn_w_gate": nrm((n_dense, D, F), D ** -0.5),
        "ffn_w_up": nrm((n_dense, D, F), D ** -0.5),
        "ffn_w_down": nrm((n_dense, F, D), F ** -0.5),
        "moe_router": nrm((n_moe, D, E), D ** -0.5),
        "moe_w_gate": nrm((n_moe, E, D, F), D ** -0.5),
        "moe_w_up": nrm((n_moe, E, D, F), D ** -0.5),
        "moe_w_down": nrm((n_moe, E, F, D), F ** -0.5),
    }


def reference(x, c, ctx, c_ctx, w_ada, b_ada, norm1_g, norm2_g, w_in, q_norm_g, k_norm_g,
              attn_sink, hy_conv_w, hy_conv_b, hy_filt_w1, hy_filt_b1, hy_filt_freq1,
              hy_filt_w2, hy_filt_b2, hy_filt_freq2, hy_filt_w3, hy_bias, w_proj_attn,
              w_proj_hyena, w_proj_fnet, w_out, ffn_w_gate, ffn_w_up, ffn_w_down,
              moe_router, moe_w_gate, moe_w_up, moe_w_down):
    L = x.shape[1]
    C = ctx.shape[1]
    cos, sin = axial_rope_tables(L)
    cond_ctx = c_ctx[None, :]
    for l in range(DEPTH):
        last = l == DEPTH - 1
        filt = (hy_filt_w1[l], hy_filt_b1[l], hy_filt_freq1[l], hy_filt_w2[l], hy_filt_b2[l],
                hy_filt_freq2[l], hy_filt_w3[l])
        proj = (w_proj_attn[l], w_proj_hyena[l], w_proj_fnet[l], w_out[l])

        sh, sc, gt = adaln(c, w_ada[l], b_ada[l], 0)
        sh_c, sc_c, gt_c = adaln(cond_ctx, w_ada[l], b_ada[l], 0)
        h = modulate(x, norm1_g[l], sh, sc)
        h_c = modulate(ctx, norm1_g[l], sh_c, sc_c)

        pr = h @ w_in[l]
        q = apply_axial_rope(q_heads(pr, q_norm_g[l]), cos, sin)
        k, v = kv_heads(pr[..., Q_END:V_END], k_norm_g[l])
        k = apply_axial_rope(k, cos, sin)
        if last:
            k_c, v_c = kv_heads(h_c @ w_in[l][:, Q_END:V_END], k_norm_g[l])
        else:
            pr_c = h_c @ w_in[l]
            q_c = q_heads(pr_c, q_norm_g[l])
            k_c, v_c = kv_heads(pr_c[..., Q_END:V_END], k_norm_g[l])

        attn_o = latent_window_attention(q, k, v, k_c, v_c, attn_sink[l])
        hy_o = hyena_mixer(pr[..., V_END:HY_END], hy_conv_w[l], hy_conv_b[l],
                           hyena_filter_spectrum(L, *filt), hy_bias[l])
        fn_o = fnet_mixer(pr[..., HY_END:FN_END])
        y = merge_branches(pr, attn_o, hy_o, fn_o, *proj)
        x = x + gt[:, None, :] * y

        if not last:
            attn_c = context_attention(q_c, k_c, v_c, attn_sink[l])
            hy_c = hyena_mixer(pr_c[..., V_END:HY_END], hy_conv_w[l], hy_conv_b[l],
                               hyena_filter_spectrum(C, *filt), hy_bias[l])
            fn_c = fnet_mixer(pr_c[..., HY_END:FN_END])
            ctx = ctx + gt_c[:, None, :] * merge_branches(pr_c, attn_c, hy_c, fn_c, *proj)

        sh2, sc2, gt2 = adaln(c, w_ada[l], b_ada[l], 1)
        if l % 2 == 0:
            i = l // 2
            ffn = functools_partial_dense = (lambda t, i=i: swiglu(t, ffn_w_gate[i], ffn_w_up[i], ffn_w_down[i]))
        else:
            i = l // 2
            ffn = (lambda t, i=i: moe_swiglu(t, moe_router[i], moe_w_gate[i], moe_w_up[i],
                                            moe_w_down[i]))
        x = x + gt2[:, None, :] * ffn(modulate(x, norm2_g[l], sh2, sc2))
        if not last:
            sh2_c, sc2_c, gt2_c = adaln(cond_ctx, w_ada[l], b_ada[l], 1)
            ctx = ctx + gt2_c[:, None, :] * ffn(modulate(ctx, norm2_g[l], sh2_c, sc2_c))
    return x
```

```python
import functools
import math

import numpy as np
import jax
import jax.numpy as jnp
from jax import lax
from jax.experimental import pallas as pl
from jax.experimental.pallas import tpu as pltpu

F32 = jnp.float32
BF = jnp.bfloat16

D_MODEL = 1024
DEPTH = 4
GRID_W = 64
HEAD_DIM = 64
N_HEADS = 8
N_KV_HEADS = 2
ATTN_WIDTH = N_HEADS * HEAD_DIM
KV_WIDTH = N_KV_HEADS * HEAD_DIM
WINDOW = 128
QBLK = 128
ROPE_THETA = 10000.0
HYENA_ORDER = 2
HYENA_WIDTH = 256
FILTER_BANDS = 16
FILTER_HIDDEN = 64
DECAY_TARGET = 1e-2
FAST_DECAY_PCT = 0.3
SLOW_DECAY_PCT = 1.5
FNET_WIDTH = 256
FNET_GROUP_DIM = 64
Q_END = ATTN_WIDTH
K_END = Q_END + KV_WIDTH
V_END = K_END + KV_WIDTH
HY_END = V_END + (HYENA_ORDER + 1) * HYENA_WIDTH
FN_END = HY_END + FNET_WIDTH
IN_WIDTH = FN_END + 3 * D_MODEL
D_FF = 2816
N_EXPERTS = 8
EPS = 1e-6
LANES = 128
NEG = -1e30

VMEM_LIMIT = 56 * 1024 * 1024


def _cparams(*sem):
    return pltpu.CompilerParams(dimension_semantics=sem, vmem_limit_bytes=VMEM_LIMIT)


def _dot(a, b):
    return jnp.dot(a, b, preferred_element_type=F32)


def _dot_nt(a, b):
    return lax.dot_general(a, b, (((1,), (1,)), ((), ())), preferred_element_type=F32)


def _split(a):
    hi = a.astype(BF)
    lo = (a - hi.astype(F32)).astype(BF)
    return hi, lo


def _dot3(a, b):
    ah, al = _split(a)
    bh, bl = _split(b)
    return _dot(ah, bh) + (_dot(ah, bl) + _dot(al, bh))


def _dot2(a, b_bf16):
    ah, al = _split(a)
    return _dot(ah, b_bf16) + _dot(al, b_bf16)


def _silu(v):
    return v * jax.nn.sigmoid(v)


def _adaln_kernel(c_ref, w_ref, b_ref, o_ref):
    o_ref[0] = _dot3(_silu(c_ref[...]), w_ref[0]) + b_ref[0]


def _adaln(cond8, w_ada, b_ada):
    depth, d, n6 = w_ada.shape
    tn = 1024
    return pl.pallas_call(
        _adaln_kernel,
        grid=(depth, n6 // tn),
        in_specs=[
            pl.BlockSpec((8, d), lambda l, j: (0, 0)),
            pl.BlockSpec((1, d, tn), lambda l, j: (l, 0, j)),
            pl.BlockSpec((1, 1, tn), lambda l, j: (l, 0, j)),
        ],
        out_specs=pl.BlockSpec((1, 8, tn), lambda l, j: (l, 0, j)),
        out_shape=jax.ShapeDtypeStruct((depth, 8, n6), F32),
        compiler_params=_cparams("parallel", "parallel"),
        name="adaln",
    )(cond8, w_ada, b_ada.reshape(depth, 1, n6))


def _modulated_norm(x, g, sc, sh):
    ms = jnp.mean(x * x, axis=-1, keepdims=True)
    h = (x * lax.rsqrt(ms + EPS)) * g
    return h * (1.0 + sc) + sh


def _phase_a_kernel(x_ref, sh_ref, sc_ref, g_ref, w_ref, cos_ref, sin_ref, qg_ref, kg_ref,
                    gsum_ref, mfn_ref, q_ref, kd_ref, vd_ref, u_ref, pq_ref, gate_ref):
    tm = x_ref.shape[0]
    hb = _modulated_norm(x_ref[...], g_ref[...], sc_ref[0], sh_ref[0]).astype(BF)
    cos = cos_ref[...]
    sin = sin_ref[...]

    def headnorm(t, gain, gs):
        ss = _dot2(t * t, gs)
        return t * lax.rsqrt(ss * (1.0 / HEAD_DIM) + EPS) * gain

    def rope(t, cosw, sinw):
        w = t.shape[1]
        nxt = pltpu.roll(t, w - 16, axis=1)
        prv = pltpu.roll(t, 16, axis=1)
        lw = lax.broadcasted_iota(jnp.int32, t.shape, 1)
        return t * cosw + jnp.where((lw % 32) < 16, nxt, prv) * sinw

    def dup_halves(t):
        lane = lax.broadcasted_iota(jnp.int32, t.shape, 1)
        sw = pltpu.roll(t, 64, axis=1)
        lo = lane < 64
        return jnp.concatenate([jnp.where(lo, t, sw), jnp.where(lo, sw, t)], axis=1)

    pq = _dot(hb, w_ref[:, 0:Q_END])
    qn = headnorm(pq, qg_ref[...], gsum_ref[...])
    cos4 = jnp.concatenate([cos] * 4, axis=1)
    sin4 = jnp.concatenate([sin] * 4, axis=1)
    q_ref[...] = (rope(qn, cos4, sin4) * (HEAD_DIM ** -0.5)).astype(BF)

    pk = _dot(hb, w_ref[:, Q_END:K_END])
    kn = headnorm(pk, kg_ref[...], gsum_ref[0:KV_WIDTH, 0:KV_WIDTH])
    kd_ref[...] = dup_halves(rope(kn, cos, sin)).astype(BF)
    vd_ref[...] = dup_halves(_dot(hb, w_ref[:, K_END:V_END])).astype(BF)

    u_ref[...] = _dot(hb, w_ref[:, V_END:HY_END])
    f = _dot(hb, w_ref[:, HY_END:FN_END])
    pq_ref[...] = _dot(f.astype(BF), mfn_ref[...])
    for i in range(3):
        lo = FN_END + i * D_MODEL
        gate_ref[:, i * D_MODEL:(i + 1) * D_MODEL] = jax.nn.sigmoid(
            _dot(hb, w_ref[:, lo:lo + D_MODEL])).astype(BF)


def _phase_a(x2, sh, sc, g, w_in_bf, cos_t, sin_t, qg, kg, gsum, mfn, *, tm, mod_row, tab_row):
    t, d = x2.shape
    row3 = lambda i: (mod_row(i), 0, 0)
    full = lambda i: (0, 0)
    tok = lambda i: (i, 0)
    outs = [
        ((t, ATTN_WIDTH), BF), ((t, 2 * KV_WIDTH), BF), ((t, 2 * KV_WIDTH), BF),
        ((t, 3 * HYENA_WIDTH), F32), ((t, 2 * FNET_WIDTH), F32), ((t, 3 * D_MODEL), BF),
    ]
    return pl.pallas_call(
        _phase_a_kernel,
        grid=(t // tm,),
        in_specs=[
            pl.BlockSpec((tm, d), tok),
            pl.BlockSpec((1, 1, d), row3),
            pl.BlockSpec((1, 1, d), row3),
            pl.BlockSpec((1, d), full),
            pl.BlockSpec((d, IN_WIDTH), full),
            pl.BlockSpec((tm, LANES), lambda i: (tab_row(i), 0)),
            pl.BlockSpec((tm, LANES), lambda i: (tab_row(i), 0)),
            pl.BlockSpec((1, ATTN_WIDTH), full),
            pl.BlockSpec((1, KV_WIDTH), full),
            pl.BlockSpec((ATTN_WIDTH, ATTN_WIDTH), full),
            pl.BlockSpec((FNET_WIDTH, 2 * FNET_WIDTH), full),
        ],
        out_specs=[pl.BlockSpec((tm, s[1]), tok) for s, _ in outs],
        out_shape=[jax.ShapeDtypeStruct(s, dt) for s, dt in outs],
        compiler_params=_cparams("parallel"),
        name="phase_a",
    )(x2, sh, sc, g, w_in_bf, cos_t, sin_t, qg, kg, gsum, mfn)


def _attn_kernel(sink_ref, q_ref, kd_ref, vd_ref, kc_ref, vc_ref, o_ref, *, local, seq_len):
    tq = q_ref.shape[1]
    nblk = tq // QBLK
    gq = N_HEADS // N_KV_HEADS
    rows = gq * QBLK
    lane = lax.broadcasted_iota(jnp.int32, (QBLK, LANES), 1)
    lo_half = lane < 64
    hrow = lax.broadcasted_iota(jnp.int32, (rows, 1), 0) // QBLK
    nband = 3 * QBLK
    for blk in range(nblk):
        r0 = blk * QBLK
        qb = q_ref[0, r0:r0 + QBLK, :]
        if local:
            n = pl.program_id(1) * nblk + blk
            start = pl.multiple_of(jnp.clip((n - 1) * QBLK, 0, seq_len - nband), QBLK)
            qpos = n * QBLK + lax.broadcasted_iota(jnp.int32, (rows, nband), 0) % QBLK
            kpos = start + lax.broadcasted_iota(jnp.int32, (rows, nband), 1)
            valid = jnp.abs(qpos - kpos) <= WINDOW
        for g in range(N_KV_HEADS):
            parts = []
            for hh in range(gq):
                h = gq * g + hh
                qc = qb[:, (h // 2) * LANES:(h // 2 + 1) * LANES]
                keep = lo_half if h % 2 == 0 else jnp.logical_not(lo_half)
                parts.append(jnp.where(keep, qc, jnp.zeros_like(qc)))
            q4 = jnp.concatenate(parts, axis=0)
            sk = jnp.full((rows, 1), sink_ref[gq * g + gq - 1], F32)
            for hh in range(gq - 2, -1, -1):
                sk = jnp.where(hrow == hh, sink_ref[gq * g + hh], sk)
            gl = slice(g * LANES, (g + 1) * LANES)
            s_ctx = _dot_nt(q4, kc_ref[0, :, gl])
            m = jnp.maximum(jnp.max(s_ctx, axis=1, keepdims=True), sk)
            if local:
                s_loc = _dot_nt(q4, kd_ref[0, pl.ds(start, nband), gl])
                s_loc = jnp.where(valid, s_loc, NEG)
                m = jnp.maximum(m, jnp.max(s_loc, axis=1, keepdims=True))
            p_ctx = jnp.exp(s_ctx - m)
            den = jnp.sum(p_ctx, axis=1, keepdims=True) + jnp.exp(sk - m)
            o = _dot(p_ctx.astype(BF), vc_ref[0, :, gl])
            if local:
                p_loc = jnp.exp(s_loc - m)
                den = den + jnp.sum(p_loc, axis=1, keepdims=True)
                o = o + _dot(p_loc.astype(BF), vd_ref[0, pl.ds(start, nband), gl])
            o = o / den
            for cc in range(gq // 2):
                col = (gq // 2) * g + cc
                oa = o[(2 * cc) * QBLK:(2 * cc + 1) * QBLK]
                ob = o[(2 * cc + 1) * QBLK:(2 * cc + 2) * QBLK]
                o_ref[0, r0:r0 + QBLK, col * LANES:(col + 1) * LANES] = (
                    jnp.where(lo_half, oa, ob).astype(BF))


def _attention(sink, q, kd, vd, kc, vc, *, local, tq):
    b, lq, _ = q.shape
    lk = kd.shape[1]
    c = kc.shape[1]
    kern = functools.partial(_attn_kernel, local=local, seq_len=lk)
    return pl.pallas_call(
        kern,
        grid=(b, lq // tq),
        in_specs=[
            pl.BlockSpec(memory_space=pltpu.SMEM),
            pl.BlockSpec((1, tq, ATTN_WIDTH), lambda bi, i: (bi, i, 0)),
            pl.BlockSpec((1, lk, 2 * KV_WIDTH), lambda bi, i: (bi, 0, 0)),
            pl.BlockSpec((1, lk, 2 * KV_WIDTH), lambda bi, i: (bi, 0, 0)),
            pl.BlockSpec((1, c, 2 * KV_WIDTH), lambda bi, i: (bi, 0, 0)),
            pl.BlockSpec((1, c, 2 * KV_WIDTH), lambda bi, i: (bi, 0, 0)),
        ],
        out_specs=pl.BlockSpec((1, tq, ATTN_WIDTH), lambda bi, i: (bi, i, 0)),
        out_shape=jax.ShapeDtypeStruct((b, lq, ATTN_WIDTH), BF),
        compiler_params=_cparams("parallel", "parallel"),
        name="attn_local" if local else "attn_ctx",
    )(sink, q, kd, vd, kc, vc)


def _hy_prep_kernel(u_ref, w_ref, b_ref, o_ref):
    u = u_ref[0]
    n = u.shape[0]
    row = lax.broadcasted_iota(jnp.int32, u.shape, 0)
    prv = jnp.where(row == 0, 0.0, pltpu.roll(u, 1, axis=0))
    nxt = jnp.where(row == n - 1, 0.0, pltpu.roll(u, n - 1, axis=0))
    o_ref[0] = prv * w_ref[0:1, :] + u * w_ref[1:2, :] + nxt * w_ref[2:3, :] + b_ref[...]


def _hy_prep(u, conv_w, conv_b):
    b, n, w = u.shape
    return pl.pallas_call(
        _hy_prep_kernel,
        grid=(b, w // LANES),
        in_specs=[
            pl.BlockSpec((1, n, LANES), lambda bi, j: (bi, 0, j)),
            pl.BlockSpec((3, LANES), lambda bi, j: (0, j)),
            pl.BlockSpec((1, LANES), lambda bi, j: (0, j)),
        ],
        out_specs=pl.BlockSpec((1, n, LANES), lambda bi, j: (bi, 0, j)),
        out_shape=jax.ShapeDtypeStruct((b, n, w), F32),
        compiler_params=_cparams("parallel", "parallel"),
        name="hy_prep",
    )(u, conv_w, conv_b)


def _hy_filter_kernel(z_ref, w1_ref, b1_ref, f1_ref, w2_ref, b2_ref, f2_ref, w3_ref, dl_ref,
                      h_ref, nrm_ref, *, n_total):
    i = pl.program_id(0)
    tm = z_ref.shape[0]
    h = jnp.sin(f1_ref[...] * (_dot3(z_ref[...], w1_ref[...]) + b1_ref[...]))
    h = jnp.sin(f2_ref[...] * (_dot3(h, w2_ref[...]) + b2_ref[...]))
    h = _dot3(h, w3_ref[...])
    row = i * tm + lax.broadcasted_iota(jnp.int32, (tm, 1), 0)
    t = row.astype(F32) * (1.0 / (n_total - 1))
    h = h * jnp.exp(-t * dl_ref[...])
    col = lax.broadcasted_iota(jnp.int32, h.shape, 1)
    h = jnp.where((col >= h.shape[1] // 2) & (row == 0), 0.0, h)
    h_ref[...] = h

    @pl.when(i == 0)
    def _():
        nrm_ref[...] = jnp.zeros_like(nrm_ref)

    nrm_ref[...] += jnp.sum(jnp.abs(h), axis=0, keepdims=True)


def _hy_filter(zfeat, w1p, b1, f1, w2, b2, f2, w3, deltas4, *, tm):
    n = zfeat.shape[0]
    wout = w3.shape[1]
    full = lambda i: (0, 0)
    kern = functools.partial(_hy_filter_kernel, n_total=n)
    return pl.pallas_call(
        kern,
        grid=(n // tm,),
        in_specs=[
            pl.BlockSpec((tm, zfeat.shape[1]), lambda i: (i, 0)),
            pl.BlockSpec(w1p.shape, full), pl.BlockSpec(b1.shape, full), pl.BlockSpec(f1.shape, full),
            pl.BlockSpec(w2.shape, full), pl.BlockSpec(b2.shape, full), pl.BlockSpec(f2.shape, full),
            pl.BlockSpec(w3.shape, full), pl.BlockSpec(deltas4.shape, full),
        ],
        out_specs=[pl.BlockSpec((tm, wout), lambda i: (i, 0)), pl.BlockSpec((1, wout), full)],
        out_shape=[jax.ShapeDtypeStruct((n, wout), F32), jax.ShapeDtypeStruct((1, wout), F32)],
        compiler_params=_cparams("arbitrary"),
        name="hy_filter",
    )(zfeat, w1p, b1, f1, w2, b2, f2, w3, deltas4)


def _hy_gate_kernel(g_ref, z_ref, c_ref, b_ref, o_ref):
    o_ref[...] = (g_ref[...] * (c_ref[...] + b_ref[...] * z_ref[...])).astype(o_ref.dtype)


def _hy_gate(uc2, gate_blk, zprev, zprev_blk, zc2, bias, out_dtype, *, tm):
    t = zc2.shape[0]
    w = HYENA_WIDTH
    return pl.pallas_call(
        _hy_gate_kernel,
        grid=(t // tm,),
        in_specs=[
            pl.BlockSpec((tm, w), lambda i: (i, gate_blk)),
            pl.BlockSpec((tm, w), lambda i: (i, zprev_blk)),
            pl.BlockSpec((tm, w), lambda i: (i, 0)),
            pl.BlockSpec((1, w), lambda i: (0, 0)),
        ],
        out_specs=pl.BlockSpec((tm, w), lambda i: (i, 0)),
        out_shape=jax.ShapeDtypeStruct((t, w), out_dtype),
        compiler_params=_cparams("parallel"),
        name="hy_gate",
    )(uc2, zprev, zc2, bias)


def _dft_mats(k_out, r_in, period, sign, scale, n_in, real_out):
    k = np.arange(k_out)[:, None]
    r = np.arange(r_in)[None, :]
    ang = 2.0 * np.pi * ((k * r) % period) / period
    fr = np.cos(ang) * scale
    fi = sign * np.sin(ang) * scale
    if real_out:
        mats = [fr, -fi]
    else:
        mats = [np.concatenate([fr, fi], 0), np.concatenate([-fi, fr], 0)]
    return jnp.asarray(np.stack(mats[:n_in], 0), dtype=F32).astype(BF)


def _twiddle(s_n, k_n, n, sign):
    s = lax.broadcasted_iota(jnp.int32, (s_n, k_n, LANES), 0)
    k = lax.broadcasted_iota(jnp.int32, (s_n, k_n, LANES), 1)
    ang = (s * k).astype(F32) * (2.0 * math.pi / n)
    return jnp.cos(ang), sign * jnp.sin(ang)


def _stage_kernel(*refs, n_in, k_out, sbk, tw, spec, real_out, transposed_out, flat):
    it = iter(refs)
    x_refs = [next(it) for _ in range(n_in)]
    g_ref = next(it)
    tw_refs = [next(it), next(it)] if tw else None
    spec_refs = [next(it) for _ in range(6)] if spec else None
    out_refs = [next(it)] if real_out else [next(it), next(it)]
    cb = out_refs[0].shape[-1]
    if spec:
        hfr, hfi, hbr, hbi, nf, nb = spec_refs
        inv = 1.0 / (nf[...] + nb[...])
    for j in range(sbk):
        acc = None
        for xi, x_ref in enumerate(x_refs):
            xv = x_ref[...] if flat else x_ref[0, :, j, :]
            d = _dot(g_ref[xi], xv.astype(BF))
            acc = d if acc is None else acc + d
        if real_out:
            ys = [acc]
        else:
            yr, yi = acc[:k_out], acc[k_out:]
            if tw:
                tr = jnp.concatenate([tw_refs[0][j]] * (cb // LANES), axis=1)
                ti = jnp.concatenate([tw_refs[1][j]] * (cb // LANES), axis=1)
                yr, yi = yr * tr - yi * ti, yr * ti + yi * tr
            if spec:
                sr = (hfr[0, :, j, :] + hbr[0, :, j, :]) * inv
                si = (hfi[0, :, j, :] - hbi[0, :, j, :]) * inv
                yr, yi = yr * sr - yi * si, yr * si + yi * sr
            ys = [yr, yi]
        for o_ref, y in zip(out_refs, ys):
            if flat:
                o_ref[...] = y.astype(o_ref.dtype)
            elif transposed_out:
                o_ref[0, j] = y.astype(o_ref.dtype)
            else:
                o_ref[0, :, j, :] = y.astype(o_ref.dtype)


def _fft_stage(xs, x_sel, gmat, *, r_in, s_n, k_out, n_groups, n_cblk, cb, transposed_out,
               real_out, out_dtype, tw=None, spec=None, spec_sel=None, sbk=8, name="fft_stage"):
    n_in = len(xs)
    flat = s_n == 1
    sbk = 1 if flat else sbk
    in_specs, args = [], []
    for x, sel in zip(xs, x_sel):
        if flat:
            in_specs.append(pl.BlockSpec((None, r_in, cb), lambda s, g, c, sel=sel: (sel(g, c)[0], 0, sel(g, c)[1])))
            args.append(x)
        else:
            xv = x.reshape(x.shape[0], x.shape[1] // s_n, s_n, x.shape[2])
            in_specs.append(pl.BlockSpec((1, r_in, sbk, cb),
                                         lambda s, g, c, sel=sel: (sel(g, c)[0], 0, s, sel(g, c)[1])))
            args.append(xv)
    in_specs.append(pl.BlockSpec(gmat.shape, lambda s, g, c: (0, 0, 0)))
    args.append(gmat)
    if tw is not None:
        for tarr in tw:
            in_specs.append(pl.BlockSpec((sbk, k_out, LANES), lambda s, g, c: (s, 0, 0)))
            args.append(tarr)
    if spec is not None:
        hr, hi, nrm = spec
        hr4 = hr.reshape(1, k_out, s_n, hr.shape[-1])
        hi4 = hi.reshape(1, k_out, s_n, hi.shape[-1])
        for arr, which in ((hr4, 0), (hi4, 0), (hr4, 1), (hi4, 1)):
            in_specs.append(pl.BlockSpec((1, k_out, sbk, cb),
                                         lambda s, g, c, which=which: (0, 0, s, spec_sel(g, c, which))))
            args.append(arr)
        for which in (0, 1):
            in_specs.append(pl.BlockSpec((1, cb), lambda s, g, c, which=which: (0, spec_sel(g, c, which))))
            args.append(nrm)
    ctot = n_cblk * cb
    if flat:
        oshape = (n_groups, k_out, ctot)
        ospec = pl.BlockSpec((None, k_out, cb), lambda s, g, c: (g, 0, c))
    elif transposed_out:
        oshape = (n_groups, s_n, k_out, ctot)
        ospec = pl.BlockSpec((1, sbk, k_out, cb), lambda s, g, c: (g, s, 0, c))
    else:
        oshape = (n_groups, k_out, s_n, ctot)
        ospec = pl.BlockSpec((1, k_out, sbk, cb), lambda s, g, c: (g, 0, s, c))
    n_out = 1 if real_out else 2
    kern = functools.partial(_stage_kernel, n_in=n_in, k_out=k_out, sbk=sbk, tw=tw is not None,
                             spec=spec is not None, real_out=real_out, transposed_out=transposed_out,
                             flat=flat)
    outs = pl.pallas_call(
        kern,
        grid=(s_n // sbk, n_groups, n_cblk),
        in_specs=in_specs,
        out_specs=[ospec] * n_out,
        out_shape=[jax.ShapeDtypeStruct(oshape, out_dtype)] * n_out,
        compiler_params=_cparams("parallel", "parallel", "parallel"),
        name=name,
    )(*args)
    return [o.reshape(n_groups, -1, ctot) for o in outs]


def _split_len(n):
    if n <= 1024:
        return n, 1
    s = 128
    return n // s, s


def _fft_forward(xs, x_sel, n, n_rows, *, n_groups, n_cblk, cb, spec=None, spec_sel=None,
                 out_dtype=F32, name="fwd"):
    n1, s = _split_len(n)
    n_in = len(xs)
    if s == 1:
        g = _dft_mats(n, n_rows, n, -1.0, 1.0, n_in, False)
        return _fft_stage(xs, x_sel, g, r_in=n_rows, s_n=1, k_out=n, n_groups=n_groups, n_cblk=n_cblk,
                          cb=cb, transposed_out=False, real_out=False, out_dtype=out_dtype,
                          spec=spec, spec_sel=spec_sel, name=name + "_direct")
    r1 = n_rows // s
    g1 = _dft_mats(n1, r1, n1, -1.0, 1.0, n_in, False)
    tw = _twiddle(s, n1, n, -1.0)
    ar, ai = _fft_stage(xs, x_sel, g1, r_in=r1, s_n=s, k_out=n1, n_groups=n_groups, n_cblk=n_cblk, cb=cb,
                        transposed_out=True, real_out=False, out_dtype=BF, tw=tw, name=name + "_s1")
    g2 = _dft_mats(s, s, s, -1.0, 1.0, 2, False)
    ident = lambda g, c: (g, c)
    return _fft_stage([ar, ai], [ident, ident], g2, r_in=s, s_n=n1, k_out=s, n_groups=n_groups,
                      n_cblk=n_cblk, cb=cb, transposed_out=False, real_out=False, out_dtype=out_dtype,
                      spec=spec, spec_sel=spec_sel, name=name + "_s2")


def _fft_inverse(pr, pi, n, n_keep, *, n_groups, n_cblk, cb, name="inv"):
    ident = lambda g, c: (g, c)
    n1, s = _split_len(n)
    if s == 1:
        g = _dft_mats(n_keep, n, n, 1.0, 1.0 / n, 2, False)
        return _fft_stage([pr, pi], [ident, ident], g, r_in=n, s_n=1, k_out=n_keep, n_groups=n_groups,
                          n_cblk=n_cblk, cb=cb, transposed_out=False, real_out=False, out_dtype=F32,
                          name=name + "_direct")
    g1 = _dft_mats(s, s, s, 1.0, 1.0, 2, False)
    tw = _twiddle(n1, s, n, 1.0)
    qr, qi = _fft_stage([pr, pi], [ident, ident], g1, r_in=s, s_n=n1, k_out=s, n_groups=n_groups,
                        n_cblk=n_cblk, cb=cb, transposed_out=True, real_out=False, out_dtype=BF, tw=tw,
                        name=name + "_s1")
    k2 = n_keep // s
    g2 = _dft_mats(k2, n1, n1, 1.0, 1.0 / n, 2, False)
    return _fft_stage([qr, qi], [ident, ident], g2, r_in=n1, s_n=s, k_out=k2, n_groups=n_groups,
                      n_cblk=n_cblk, cb=cb, transposed_out=False, real_out=False, out_dtype=F32,
                      name=name + "_s2")


def _hyena(uc, h_filt, nrm, hy_bias, out_rows):
    b, n, _ = uc.shape
    w = HYENA_WIDTH
    nfft = 2 * n
    hr, hi = _fft_forward([h_filt[None]], [lambda g, c: (0, c)], nfft, n, n_groups=1, n_cblk=4, cb=w,
                          name="hy_filt_fft")
    uc2 = uc.reshape(b * n, 3 * w)
    z, zblk = uc, 2
    z2 = uc2
    for o in range(HYENA_ORDER):
        sel_r = lambda g, c, zblk=zblk: (0, zblk)
        sel_i = lambda g, c, zblk=zblk: (1, zblk)
        spec_sel = lambda g, c, which, o=o: which * HYENA_ORDER + o
        pr, pi = _fft_forward([z, z], [sel_r, sel_i], nfft, n, n_groups=1, n_cblk=1, cb=w,
                              spec=(hr[0], hi[0], nrm), spec_sel=spec_sel, out_dtype=BF, name="hy_fwd")
        cr, ci = _fft_inverse(pr, pi, nfft, n, n_groups=1, n_cblk=1, cb=w, name="hy_inv")
        zc2 = jnp.concatenate([cr, ci], axis=0).reshape(b * n, w)
        last = o == HYENA_ORDER - 1
        znext = _hy_gate(uc2, o, z2, zblk, zc2, hy_bias[o:o + 1], BF if last else F32, tm=out_rows)
        z, zblk, z2 = znext.reshape(b, n, w), 0, znext
    return z2


def _fnet(pq):
    b, n, _ = pq.shape
    w = FNET_WIDTH
    scale = 1.0 / math.sqrt(n * FNET_GROUP_DIM)
    sel_r = lambda g, c: (g, 0)
    sel_i = lambda g, c: (g, 1)
    ident = lambda g, c: (g, c)
    n1, s = _split_len(n)
    if s == 1:
        g = _dft_mats(n, n, n, -1.0, scale, 2, True)
        (y,) = _fft_stage([pq, pq], [sel_r, sel_i], g, r_in=n, s_n=1, k_out=n, n_groups=b, n_cblk=1, cb=w,
                          transposed_out=False, real_out=True, out_dtype=BF, name="fnet_direct")
        return y.reshape(b * n, w)
    g1 = _dft_mats(n1, n1, n1, -1.0, 1.0, 2, False)
    tw = _twiddle(s, n1, n, -1.0)
    ar, ai = _fft_stage([pq, pq], [sel_r, sel_i], g1, r_in=n1, s_n=s, k_out=n1, n_groups=b, n_cblk=1, cb=w,
                        transposed_out=True, real_out=False, out_dtype=BF, tw=tw, name="fnet_s1")
    g2 = _dft_mats(s, s, s, -1.0, scale, 2, True)
    (y,) = _fft_stage([ar, ai], [ident, ident], g2, r_in=s, s_n=n1, k_out=s, n_groups=b, n_cblk=1, cb=w,
                      transposed_out=False, real_out=True, out_dtype=BF, name="fnet_s2")
    return y.reshape(b * n, w)


def _merge_kernel(x_ref, gt_ref, a_ref, h_ref, f_ref, gate_ref, wa_ref, wh_ref, wf_ref, wo_ref, o_ref):
    d = D_MODEL
    m = gate_ref[:, 0:d].astype(F32) * _dot(a_ref[...], wa_ref[...])
    m = m + gate_ref[:, d:2 * d].astype(F32) * _dot(h_ref[...], wh_ref[...])
    m = m + gate_ref[:, 2 * d:3 * d].astype(F32) * _dot(f_ref[...], wf_ref[...])
    y = _dot(m.astype(BF), wo_ref[...])
    o_ref[...] = x_ref[...] + gt_ref[0] * y


def _merge(x2, gt, attn_o, hy_o, fn_o, gates, wa, wh, wf, wo, *, tm, mod_row):
    t, d = x2.shape
    tok = lambda i: (i, 0)
    full = lambda i: (0, 0)
    return pl.pallas_call(
        _merge_kernel,
        grid=(t // tm,),
        in_specs=[
            pl.BlockSpec((tm, d), tok),
            pl.BlockSpec((1, 1, d), lambda i: (mod_row(i), 0, 0)),
            pl.BlockSpec((tm, ATTN_WIDTH), tok),
            pl.BlockSpec((tm, HYENA_WIDTH), tok),
            pl.BlockSpec((tm, FNET_WIDTH), tok),
            pl.BlockSpec((tm, 3 * d), tok),
            pl.BlockSpec(wa.shape, full), pl.BlockSpec(wh.shape, full),
            pl.BlockSpec(wf.shape, full), pl.BlockSpec(wo.shape, full),
        ],
        out_specs=pl.BlockSpec((tm, d), tok),
        out_shape=jax.ShapeDtypeStruct((t, d), F32),
        compiler_params=_cparams("parallel"),
        name="merge",
    )(x2, gt, attn_o, hy_o, fn_o, gates, wa, wh, wf, wo)


def _ffn_kernel(x_ref, sh_ref, sc_ref, gt_ref, g_ref, wg_ref, wu_ref, wd_ref, o_ref, h_scr, acc_scr):
    f = pl.program_id(1)

    @pl.when(f == 0)
    def _():
        h_scr[...] = _modulated_norm(x_ref[...], g_ref[...], sc_ref[0], sh_ref[0]).astype(BF)
        acc_scr[...] = jnp.zeros_like(acc_scr)

    hb = h_scr[...]
    act = _silu(_dot(hb, wg_ref[...])) * _dot(hb, wu_ref[...])
    acc_scr[...] += _dot(act.astype(BF), wd_ref[...])

    @pl.when(f == pl.num_programs(1) - 1)
    def _():
        o_ref[...] = x_ref[...] + gt_ref[0] * acc_scr[...]


def _ffn(x2, sh, sc, gt, g, wg, wu, wd, *, tm, tf, mod_row):
    t, d = x2.shape
    ff = wg.shape[1]
    row3 = lambda i, f: (mod_row(i), 0, 0)
    return pl.pallas_call(
        _ffn_kernel,
        grid=(t // tm, ff // tf),
        in_specs=[
            pl.BlockSpec((tm, d), lambda i, f: (i, 0)),
            pl.BlockSpec((1, 1, d), row3), pl.BlockSpec((1, 1, d), row3), pl.BlockSpec((1, 1, d), row3),
            pl.BlockSpec((1, d), lambda i, f: (0, 0)),
            pl.BlockSpec((d, tf), lambda i, f: (0, f)),
            pl.BlockSpec((d, tf), lambda i, f: (0, f)),
            pl.BlockSpec((tf, d), lambda i, f: (f, 0)),
        ],
        out_specs=pl.BlockSpec((tm, d), lambda i, f: (i, 0)),
        out_shape=jax.ShapeDtypeStruct((t, d), F32),
        scratch_shapes=[pltpu.VMEM((tm, d), BF), pltpu.VMEM((tm, d), F32)],
        compiler_params=_cparams("parallel", "arbitrary"),
        name="ffn_dense",
    )(x2, sh, sc, gt, g, wg, wu, wd)


def _top2_combine(logits):
    lane = lax.broadcasted_iota(jnp.int32, logits.shape, 1)
    lg = jnp.where(lane < N_EXPERTS, logits, -jnp.inf)
    m1 = jnp.max(lg, axis=1, keepdims=True)
    i1 = jnp.min(jnp.where(lg == m1, lane, LANES), axis=1, keepdims=True)
    lg2 = jnp.where(lane == i1, -jnp.inf, lg)
    m2 = jnp.max(lg2, axis=1, keepdims=True)
    i2 = jnp.min(jnp.where(lg2 == m2, lane, LANES), axis=1, keepdims=True)
    e2 = jnp.exp(m2 - m1)
    w1 = 1.0 / (1.0 + e2)
    w2 = e2 * w1
    return jnp.where(lane == i1, w1, 0.0) + jnp.where(lane == i2, w2, 0.0)


def _moe_kernel(x_ref, sh_ref, sc_ref, gt_ref, g_ref, wr_ref, wg_ref, wu_ref, wd_ref, o_ref,
                h_scr, comb_scr, acc_scr):
    e = pl.program_id(1)
    f = pl.program_id(2)

    @pl.when((e == 0) & (f == 0))
    def _():
        h = _modulated_norm(x_ref[...], g_ref[...], sc_ref[0], sh_ref[0])
        h_scr[...] = h.astype(BF)
        comb_scr[...] = _top2_combine(_dot3(h, wr_ref[...]))
        acc_scr[...] = jnp.zeros_like(acc_scr)

    hb = h_scr[...]
    comb = comb_scr[...]
    lane = lax.broadcasted_iota(jnp.int32, comb.shape, 1)
    ce = jnp.sum(jnp.where(lane == e, comb, 0.0), axis=1, keepdims=True)
    act = _silu(_dot(hb, wg_ref[0])) * _dot(hb, wu_ref[0])
    acc_scr[...] += ce * _dot(act.astype(BF), wd_ref[0])

    @pl.when((e == pl.num_programs(1) - 1) & (f == pl.num_programs(2) - 1))
    def _():
        o_ref[...] = x_ref[...] + gt_ref[0] * acc_scr[...]


def _moe(x2, sh, sc, gt, g, wr_pad, wg, wu, wd, *, tm, tf, mod_row):
    t, d = x2.shape
    ne, _, ff = wg.shape
    row3 = lambda i, e, f: (mod_row(i), 0, 0)
    return pl.pallas_call(
        _moe_kernel,
        grid=(t // tm, ne, ff // tf),
        in_specs=[
            pl.BlockSpec((tm, d), lambda i, e, f: (i, 0)),
            pl.BlockSpec((1, 1, d), row3), pl.BlockSpec((1, 1, d), row3), pl.BlockSpec((1, 1, d), row3),
            pl.BlockSpec((1, d), lambda i, e, f: (0, 0)),
            pl.BlockSpec((d, LANES), lambda i, e, f: (0, 0)),
            pl.BlockSpec((1, d, tf), lambda i, e, f: (e, 0, f)),
            pl.BlockSpec((1, d, tf), lambda i, e, f: (e, 0, f)),
            pl.BlockSpec((1, tf, d), lambda i, e, f: (e, f, 0)),
        ],
        out_specs=pl.BlockSpec((tm, d), lambda i, e, f: (i, 0)),
        out_shape=jax.ShapeDtypeStruct((t, d), F32),
        scratch_shapes=[pltpu.VMEM((tm, d), BF), pltpu.VMEM((tm, LANES), F32), pltpu.VMEM((tm, d), F32)],
        compiler_params=_cparams("parallel", "arbitrary", "arbitrary"),
        name="moe_dense",
    )(x2, sh, sc, gt, g, wr_pad, wg, wu, wd)


def _rope_tables(seq_len):
    pos = np.arange(seq_len)
    prow = (pos // GRID_W).astype(np.float32)
    pcol = (pos % GRID_W).astype(np.float32)
    n_freq = HEAD_DIM // 4
    inv = (np.float32(ROPE_THETA) ** (-np.arange(n_freq, dtype=np.float32) / n_freq)).astype(np.float32)
    ar = jnp.asarray(prow)[:, None] * jnp.asarray(inv)[None, :]
    ac = jnp.asarray(pcol)[:, None] * jnp.asarray(inv)[None, :]
    cos = jnp.concatenate([jnp.cos(ar)] * 2 + [jnp.cos(ac)] * 2, axis=1)
    sin = jnp.concatenate([-jnp.sin(ar), jnp.sin(ar), -jnp.sin(ac), jnp.sin(ac)], axis=1)
    return jnp.concatenate([cos, cos], axis=1), jnp.concatenate([sin, sin], axis=1)


def _head_sum_matrix():
    c = np.arange(ATTN_WIDTH)
    return jnp.asarray((c[:, None] // HEAD_DIM) == (c[None, :] // HEAD_DIM), dtype=F32).astype(BF)


def _fnet_channel_matrix():
    c = np.arange(FNET_WIDTH)
    same = (c[:, None] // FNET_GROUP_DIM) == (c[None, :] // FNET_GROUP_DIM)
    ang = 2.0 * np.pi * (((c[:, None] % FNET_GROUP_DIM) * (c[None, :] % FNET_GROUP_DIM)) % FNET_GROUP_DIM) / FNET_GROUP_DIM
    cb = np.where(same, np.cos(ang), 0.0)
    sb = np.where(same, np.sin(ang), 0.0)
    return jnp.asarray(np.concatenate([cb, -sb], axis=1), dtype=F32).astype(BF)


def _filter_features(n):
    t = jnp.linspace(0.0, 1.0, n, dtype=F32)[:, None]
    w = 2.0 * math.pi * jnp.arange(n, dtype=F32)[:, None] / n
    fb = jnp.linspace(1e-4, FILTER_BANDS - 1, FILTER_BANDS, dtype=F32)
    z = jnp.concatenate([t, jnp.cos(fb * w), -jnp.sin(fb * w)], axis=-1)
    return jnp.pad(z, ((0, 0), (0, 64 - z.shape[1])))


def _decay_rates():
    d = jnp.abs(jnp.linspace(math.log(DECAY_TARGET) / SLOW_DECAY_PCT, math.log(DECAY_TARGET) / FAST_DECAY_PCT,
                             HYENA_WIDTH, dtype=F32))
    return jnp.concatenate([d] * (2 * HYENA_ORDER))[None, :]


def kernel(x, c, ctx, c_ctx, w_ada, b_ada, norm1_g, norm2_g, w_in, q_norm_g, k_norm_g, attn_sink,
           hy_conv_w, hy_conv_b, hy_filt_w1, hy_filt_b1, hy_filt_freq1, hy_filt_w2, hy_filt_b2,
           hy_filt_freq2, hy_filt_w3, hy_bias, w_proj_attn, w_proj_hyena, w_proj_fnet, w_out,
           ffn_w_gate, ffn_w_up, ffn_w_down, moe_router, moe_w_gate, moe_w_up, moe_w_down):
    b, seq, d = x.shape
    n_ctx = ctx.shape[1]
    depth = w_ada.shape[0]
    tm = 256
    tiles_per_seq = seq // tm

    cond8 = jnp.concatenate([c, c_ctx[None, :], jnp.zeros((8 - b - 1, d), F32)], axis=0)
    mods = _adaln(cond8, w_ada, b_ada)

    cos_l, sin_l = _rope_tables(seq)
    cos_c = jnp.ones((n_ctx, LANES), F32)
    sin_c = jnp.zeros((n_ctx, LANES), F32)
    gsum = _head_sum_matrix()
    mfn = _fnet_channel_matrix()
    deltas4 = _decay_rates()
    zfeat_l = _filter_features(seq)
    zfeat_c = _filter_features(n_ctx)

    lat_row = lambda i: i // tiles_per_seq
    ctx_row = lambda i: b
    lat_tab = lambda i: i % tiles_per_seq
    ctx_tab = lambda i: 0
    tm_c = min(tm, n_ctx)

    xs = x.reshape(b * seq, d)
    cs = ctx.reshape(b * n_ctx, d)
    for l in range(depth):
        last = l == depth - 1
        mod = lambda j: mods[l, :, j * d:(j + 1) * d].reshape(8, 1, d)
        w_in_bf = w_in[l].astype(BF)
        qg = jnp.tile(q_norm_g[l], N_HEADS)[None, :]
        kg = jnp.tile(k_norm_g[l], N_KV_HEADS)[None, :]
        g1 = norm1_g[l][None, :]
        wa, wh, wf, wo = (w_proj_attn[l].astype(BF), w_proj_hyena[l].astype(BF),
                          w_proj_fnet[l].astype(BF), w_out[l].astype(BF))
        conv_w = hy_conv_w[l].reshape(3, -1)
        conv_b = hy_conv_b[l][None, :]
        w1p = jnp.pad(hy_filt_w1[l], ((0, 64 - hy_filt_w1.shape[1]), (0, 0)))
        filt = (w1p, hy_filt_b1[l][None, :], hy_filt_freq1[l][None, :], hy_filt_w2[l],
                hy_filt_b2[l][None, :], hy_filt_freq2[l][None, :], hy_filt_w3[l], deltas4)

        q_c, kd_c, vd_c, u_c, pq_c, gates_c = _phase_a(
            cs, mod(0), mod(1), g1, w_in_bf, cos_c, sin_c, qg, kg, gsum, mfn,
            tm=tm_c, mod_row=ctx_row, tab_row=ctx_tab)
        kd_c3 = kd_c.reshape(b, n_ctx, -1)
        vd_c3 = vd_c.reshape(b, n_ctx, -1)

        q_l, kd_l, vd_l, u_l, pq_l, gates_l = _phase_a(
            xs, mod(0), mod(1), g1, w_in_bf, cos_l, sin_l, qg, kg, gsum, mfn,
            tm=tm, mod_row=lat_row, tab_row=lat_tab)
        attn_l = _attention(attn_sink[l], q_l.reshape(b, seq, -1), kd_l.reshape(b, seq, -1),
                            vd_l.reshape(b, seq, -1), kd_c3, vd_c3, local=True, tq=512)
        h_l, nrm_l = _hy_filter(zfeat_l, *filt, tm=512)
        uc_l = _hy_prep(u_l.reshape(b, seq, -1), conv_w, conv_b)
        hy_l = _hyena(uc_l, h_l, nrm_l, hy_bias[l], out_rows=512)
        fn_l = _fnet(pq_l.reshape(b, seq, -1))
        xs = _merge(xs, mod(2), attn_l.reshape(b * seq, -1), hy_l, fn_l, gates_l, wa, wh, wf, wo,
                    tm=tm, mod_row=lat_row)

        if not last:
            attn_c = _attention(attn_sink[l], q_c.reshape(b, n_ctx, -1), kd_c3, vd_c3, kd_c3, vd_c3,
                                local=False, tq=n_ctx)
            h_c, nrm_c = _hy_filter(zfeat_c, *filt, tm=n_ctx)
            uc_c = _hy_prep(u_c.reshape(b, n_ctx, -1), conv_w, conv_b)
            hy_c = _hyena(uc_c, h_c, nrm_c, hy_bias[l], out_rows=n_ctx)
            fn_c = _fnet(pq_c.reshape(b, n_ctx, -1))
            cs = _merge(cs, mod(2), attn_c.reshape(b * n_ctx, -1), hy_c, fn_c, gates_c, wa, wh, wf, wo,
                        tm=tm_c, mod_row=ctx_row)

        g2 = norm2_g[l][None, :]
        i = l // 2
        if l % 2 == 0:
            wg, wu, wd = ffn_w_gate[i].astype(BF), ffn_w_up[i].astype(BF), ffn_w_down[i].astype(BF)
            run = lambda t2, rows, tmm: _ffn(t2, mod(3), mod(4), mod(5), g2, wg, wu, wd,
                                             tm=tmm, tf=D_FF // 2, mod_row=rows)
        else:
            wr = jnp.pad(moe_router[i], ((0, 0), (0, LANES - N_EXPERTS)))
            wg, wu, wd = moe_w_gate[i].astype(BF), moe_w_up[i].astype(BF), moe_w_down[i].astype(BF)
            run = lambda t2, rows, tmm: _moe(t2, mod(3), mod(4), mod(5), g2, wr, wg, wu, wd,
                                             tm=tmm, tf=D_FF // 2, mod_row=rows)
        tm_ffn = 512
        xs = run(xs, lambda t: t // (seq // tm_ffn), tm_ffn)
        if not last:
            cs = run(cs, ctx_row, tm_ffn)
    return xs.reshape(b, seq, d)
```

```python
import functools
import math

import numpy as np
import jax
import jax.numpy as jnp
from jax import lax
from jax.experimental import pallas as pl
from jax.experimental.pallas import tpu as pltpu

F32 = jnp.float32
BF = jnp.bfloat16

D_MODEL = 1024
DEPTH = 4
GRID_W = 64
HEAD_DIM = 64
N_HEADS = 8
N_KV_HEADS = 2
ATTN_WIDTH = N_HEADS * HEAD_DIM
KV_WIDTH = N_KV_HEADS * HEAD_DIM
WINDOW = 128
QBLK = 128
ROPE_THETA = 10000.0
HYENA_ORDER = 2
HYENA_WIDTH = 256
FILTER_BANDS = 16
FILTER_HIDDEN = 64
DECAY_TARGET = 1e-2
FAST_DECAY_PCT = 0.3
SLOW_DECAY_PCT = 1.5
FNET_WIDTH = 256
FNET_GROUP_DIM = 64
Q_END = ATTN_WIDTH
K_END = Q_END + KV_WIDTH
V_END = K_END + KV_WIDTH
HY_END = V_END + (HYENA_ORDER + 1) * HYENA_WIDTH
FN_END = HY_END + FNET_WIDTH
IN_WIDTH = FN_END + 3 * D_MODEL
D_FF = 2816
N_EXPERTS = 8
EPS = 1e-6
LANES = 128
NEG = -1e30

VMEM_LIMIT = 56 * 1024 * 1024


def _cparams(*sem):
    return pltpu.CompilerParams(dimension_semantics=sem, vmem_limit_bytes=VMEM_LIMIT)


def _dot(a, b):
    return jnp.dot(a, b, preferred_element_type=F32)


def _dot_nt(a, b):
    return lax.dot_general(a, b, (((1,), (1,)), ((), ())), preferred_element_type=F32)


def _split(a):
    hi = a.astype(BF)
    lo = (a - hi.astype(F32)).astype(BF)
    return hi, lo


def _dot3(a, b):
    ah, al = _split(a)
    bh, bl = _split(b)
    return _dot(ah, bh) + (_dot(ah, bl) + _dot(al, bh))


def _dot2(a, b_bf16):
    ah, al = _split(a)
    return _dot(ah, b_bf16) + _dot(al, b_bf16)


def _silu(v):
    return v * jax.nn.sigmoid(v)


def _adaln_kernel(c_ref, w_ref, b_ref, o_ref):
    o_ref[0] = _dot3(_silu(c_ref[...]), w_ref[0]) + b_ref[0]


def _adaln(cond8, w_ada, b_ada):
    depth, d, n6 = w_ada.shape
    tn = 1024
    return pl.pallas_call(
        _adaln_kernel,
        grid=(depth, n6 // tn),
        in_specs=[
            pl.BlockSpec((8, d), lambda l, j: (0, 0)),
            pl.BlockSpec((1, d, tn), lambda l, j: (l, 0, j)),
            pl.BlockSpec((1, 1, tn), lambda l, j: (l, 0, j)),
        ],
        out_specs=pl.BlockSpec((1, 8, tn), lambda l, j: (l, 0, j)),
        out_shape=jax.ShapeDtypeStruct((depth, 8, n6), F32),
        compiler_params=_cparams("parallel", "parallel"),
        name="adaln",
    )(cond8, w_ada, b_ada.reshape(depth, 1, n6))


def _modulated_norm(x, g, sc, sh):
    ms = jnp.mean(x * x, axis=-1, keepdims=True)
    h = (x * lax.rsqrt(ms + EPS)) * g
    return h * (1.0 + sc) + sh


def _phase_a_kernel(x_ref, sh_ref, sc_ref, g_ref, w_ref, cos_ref, sin_ref, qg_ref, kg_ref,
                    gsum_ref, mfn_ref, q_ref, kd_ref, vd_ref, u_ref, pq_ref, gate_ref):
    tm = x_ref.shape[0]
    hb = _modulated_norm(x_ref[...], g_ref[...], sc_ref[0], sh_ref[0]).astype(BF)
    cos = cos_ref[...]
    sin = sin_ref[...]

    def headnorm(t, gain, gs):
        ss = _dot2(t * t, gs)
        return t * lax.rsqrt(ss * (1.0 / HEAD_DIM) + EPS) * gain

    def rope(t, cosw, sinw):
        w = t.shape[1]
        nxt = pltpu.roll(t, w - 16, axis=1)
        prv = pltpu.roll(t, 16, axis=1)
        lw = lax.broadcasted_iota(jnp.int32, t.shape, 1)
        return t * cosw + jnp.where((lw % 32) < 16, nxt, prv) * sinw

    def dup_halves(t):
        lane = lax.broadcasted_iota(jnp.int32, t.shape, 1)
        sw = pltpu.roll(t, 64, axis=1)
        lo = lane < 64
        return jnp.concatenate([jnp.where(lo, t, sw), jnp.where(lo, sw, t)], axis=1)

    pq = _dot(hb, w_ref[:, 0:Q_END])
    qn = headnorm(pq, qg_ref[...], gsum_ref[...])
    cos4 = jnp.concatenate([cos] * 4, axis=1)
    sin4 = jnp.concatenate([sin] * 4, axis=1)
    q_ref[...] = (rope(qn, cos4, sin4) * (HEAD_DIM ** -0.5)).astype(BF)

    pk = _dot(hb, w_ref[:, Q_END:K_END])
    kn = headnorm(pk, kg_ref[...], gsum_ref[0:KV_WIDTH, 0:KV_WIDTH])
    kd_ref[...] = dup_halves(rope(kn, cos, sin)).astype(BF)
    vd_ref[...] = dup_halves(_dot(hb, w_ref[:, K_END:V_END])).astype(BF)

    u_ref[...] = _dot(hb, w_ref[:, V_END:HY_END])
    f = _dot(hb, w_ref[:, HY_END:FN_END])
    pq_ref[...] = _dot(f.astype(BF), mfn_ref[...])
    for i in range(3):
        lo = FN_END + i * D_MODEL
        gate_ref[:, i * D_MODEL:(i + 1) * D_MODEL] = jax.nn.sigmoid(
            _dot(hb, w_ref[:, lo:lo + D_MODEL])).astype(BF)


def _phase_a(x2, sh, sc, g, w_in_bf, cos_t, sin_t, qg, kg, gsum, mfn, *, tm, mod_row, tab_row):
    t, d = x2.shape
    row3 = lambda i: (mod_row(i), 0, 0)
    full = lambda i: (0, 0)
    tok = lambda i: (i, 0)
    outs = [
        ((t, ATTN_WIDTH), BF), ((t, 2 * KV_WIDTH), BF), ((t, 2 * KV_WIDTH), BF),
        ((t, 3 * HYENA_WIDTH), F32), ((t, 2 * FNET_WIDTH), F32), ((t, 3 * D_MODEL), BF),
    ]
    return pl.pallas_call(
        _phase_a_kernel,
        grid=(t // tm,),
        in_specs=[
            pl.BlockSpec((tm, d), tok),
            pl.BlockSpec((1, 1, d), row3),
            pl.BlockSpec((1, 1, d), row3),
            pl.BlockSpec((1, d), full),
            pl.BlockSpec((d, IN_WIDTH), full),
            pl.BlockSpec((tm, LANES), lambda i: (tab_row(i), 0)),
            pl.BlockSpec((tm, LANES), lambda i: (tab_row(i), 0)),
            pl.BlockSpec((1, ATTN_WIDTH), full),
            pl.BlockSpec((1, KV_WIDTH), full),
            pl.BlockSpec((ATTN_WIDTH, ATTN_WIDTH), full),
            pl.BlockSpec((FNET_WIDTH, 2 * FNET_WIDTH), full),
        ],
        out_specs=[pl.BlockSpec((tm, s[1]), tok) for s, _ in outs],
        out_shape=[jax.ShapeDtypeStruct(s, dt) for s, dt in outs],
        compiler_params=_cparams("parallel"),
        name="phase_a",
    )(x2, sh, sc, g, w_in_bf, cos_t, sin_t, qg, kg, gsum, mfn)


def _attn_kernel(sink_ref, q_ref, kd_ref, vd_ref, kc_ref, vc_ref, o_ref, *, local, seq_len):
    tq = q_ref.shape[1]
    nblk = tq // QBLK
    gq = N_HEADS // N_KV_HEADS
    rows = gq * QBLK
    lane = lax.broadcasted_iota(jnp.int32, (QBLK, LANES), 1)
    lo_half = lane < 64
    hrow = lax.broadcasted_iota(jnp.int32, (rows, 1), 0) // QBLK
    nband = 3 * QBLK
    for blk in range(nblk):
        r0 = blk * QBLK
        qb = q_ref[0, r0:r0 + QBLK, :]
        if local:
            n = pl.program_id(1) * nblk + blk
            start = pl.multiple_of(jnp.clip((n - 1) * QBLK, 0, seq_len - nband), QBLK)
            qpos = n * QBLK + lax.broadcasted_iota(jnp.int32, (rows, nband), 0) % QBLK
            kpos = start + lax.broadcasted_iota(jnp.int32, (rows, nband), 1)
            valid = jnp.abs(qpos - kpos) <= WINDOW
        for g in range(N_KV_HEADS):
            parts = []
            for hh in range(gq):
                h = gq * g + hh
                qc = qb[:, (h // 2) * LANES:(h // 2 + 1) * LANES]
                keep = lo_half if h % 2 == 0 else jnp.logical_not(lo_half)
                parts.append(jnp.where(keep, qc, jnp.zeros_like(qc)))
            q4 = jnp.concatenate(parts, axis=0)
            sk = jnp.full((rows, 1), sink_ref[gq * g + gq - 1], F32)
            for hh in range(gq - 2, -1, -1):
                sk = jnp.where(hrow == hh, sink_ref[gq * g + hh], sk)
            gl = slice(g * LANES, (g + 1) * LANES)
            s_ctx = _dot_nt(q4, kc_ref[0, :, gl])
            m = jnp.maximum(jnp.max(s_ctx, axis=1, keepdims=True), sk)
            if local:
                s_loc = _dot_nt(q4, kd_ref[0, pl.ds(start, nband), gl])
                s_loc = jnp.where(valid, s_loc, NEG)
                m = jnp.maximum(m, jnp.max(s_loc, axis=1, keepdims=True))
            p_ctx = jnp.exp(s_ctx - m)
            den = jnp.sum(p_ctx, axis=1, keepdims=True) + jnp.exp(sk - m)
            o = _dot(p_ctx.astype(BF), vc_ref[0, :, gl])
            if local:
                p_loc = jnp.exp(s_loc - m)
                den = den + jnp.sum(p_loc, axis=1, keepdims=True)
                o = o + _dot(p_loc.astype(BF), vd_ref[0, pl.ds(start, nband), gl])
            o = o / den
            for cc in range(gq // 2):
                col = (gq // 2) * g + cc
                oa = o[(2 * cc) * QBLK:(2 * cc + 1) * QBLK]
                ob = o[(2 * cc + 1) * QBLK:(2 * cc + 2) * QBLK]
                o_ref[0, r0:r0 + QBLK, col * LANES:(col + 1) * LANES] = (
                    jnp.where(lo_half, oa, ob).astype(BF))


def _attention(sink, q, kd, vd, kc, vc, *, local, tq):
    b, lq, _ = q.shape
    lk = kd.shape[1]
    c = kc.shape[1]
    kern = functools.partial(_attn_kernel, local=local, seq_len=lk)
    return pl.pallas_call(
        kern,
        grid=(b, lq // tq),
        in_specs=[
            pl.BlockSpec(memory_space=pltpu.SMEM),
            pl.BlockSpec((1, tq, ATTN_WIDTH), lambda bi, i: (bi, i, 0)),
            pl.BlockSpec((1, lk, 2 * KV_WIDTH), lambda bi, i: (bi, 0, 0)),
            pl.BlockSpec((1, lk, 2 * KV_WIDTH), lambda bi, i: (bi, 0, 0)),
            pl.BlockSpec((1, c, 2 * KV_WIDTH), lambda bi, i: (bi, 0, 0)),
            pl.BlockSpec((1, c, 2 * KV_WIDTH), lambda bi, i: (bi, 0, 0)),
        ],
        out_specs=pl.BlockSpec((1, tq, ATTN_WIDTH), lambda bi, i: (bi, i, 0)),
        out_shape=jax.ShapeDtypeStruct((b, lq, ATTN_WIDTH), BF),
        compiler_params=_cparams("parallel", "parallel"),
        name="attn_local" if local else "attn_ctx",
    )(sink, q, kd, vd, kc, vc)


def _hy_prep_kernel(u_ref, w_ref, b_ref, o_ref):
    u = u_ref[0]
    n = u.shape[0]
    row = lax.broadcasted_iota(jnp.int32, u.shape, 0)
    prv = jnp.where(row == 0, 0.0, pltpu.roll(u, 1, axis=0))
    nxt = jnp.where(row == n - 1, 0.0, pltpu.roll(u, n - 1, axis=0))
    o_ref[0] = prv * w_ref[0:1, :] + u * w_ref[1:2, :] + nxt * w_ref[2:3, :] + b_ref[...]


def _hy_prep(u, conv_w, conv_b):
    b, n, w = u.shape
    return pl.pallas_call(
        _hy_prep_kernel,
        grid=(b, w // LANES),
        in_specs=[
            pl.BlockSpec((1, n, LANES), lambda bi, j: (bi, 0, j)),
            pl.BlockSpec((3, LANES), lambda bi, j: (0, j)),
            pl.BlockSpec((1, LANES), lambda bi, j: (0, j)),
        ],
        out_specs=pl.BlockSpec((1, n, LANES), lambda bi, j: (bi, 0, j)),
        out_shape=jax.ShapeDtypeStruct((b, n, w), F32),
        compiler_params=_cparams("parallel", "parallel"),
        name="hy_prep",
    )(u, conv_w, conv_b)


def _hy_filter_kernel(z_ref, w1_ref, b1_ref, f1_ref, w2_ref, b2_ref, f2_ref, w3_ref, dl_ref,
                      h_ref, nrm_ref, *, n_total):
    i = pl.program_id(0)
    tm = z_ref.shape[0]
    h = jnp.sin(f1_ref[...] * (_dot3(z_ref[...], w1_ref[...]) + b1_ref[...]))
    h = jnp.sin(f2_ref[...] * (_dot3(h, w2_ref[...]) + b2_ref[...]))
    h = _dot3(h, w3_ref[...])
    row = i * tm + lax.broadcasted_iota(jnp.int32, (tm, 1), 0)
    t = row.astype(F32) * (1.0 / (n_total - 1))
    h = h * jnp.exp(-t * dl_ref[...])
    col = lax.broadcasted_iota(jnp.int32, h.shape, 1)
    h = jnp.where((col >= h.shape[1] // 2) & (row == 0), 0.0, h)
    h_ref[...] = h

    @pl.when(i == 0)
    def _():
        nrm_ref[...] = jnp.zeros_like(nrm_ref)

    nrm_ref[...] += jnp.sum(jnp.abs(h), axis=0, keepdims=True)


def _hy_filter(zfeat, w1p, b1, f1, w2, b2, f2, w3, deltas4, *, tm):
    n = zfeat.shape[0]
    wout = w3.shape[1]
    full = lambda i: (0, 0)
    kern = functools.partial(_hy_filter_kernel, n_total=n)
    return pl.pallas_call(
        kern,
        grid=(n // tm,),
        in_specs=[
            pl.BlockSpec((tm, zfeat.shape[1]), lambda i: (i, 0)),
            pl.BlockSpec(w1p.shape, full), pl.BlockSpec(b1.shape, full), pl.BlockSpec(f1.shape, full),
            pl.BlockSpec(w2.shape, full), pl.BlockSpec(b2.shape, full), pl.BlockSpec(f2.shape, full),
            pl.BlockSpec(w3.shape, full), pl.BlockSpec(deltas4.shape, full),
        ],
        out_specs=[pl.BlockSpec((tm, wout), lambda i: (i, 0)), pl.BlockSpec((1, wout), full)],
        out_shape=[jax.ShapeDtypeStruct((n, wout), F32), jax.ShapeDtypeStruct((1, wout), F32)],
        compiler_params=_cparams("arbitrary"),
        name="hy_filter",
    )(zfeat, w1p, b1, f1, w2, b2, f2, w3, deltas4)


def _hy_gate_kernel(g_ref, z_ref, c_ref, b_ref, o_ref):
    o_ref[...] = (g_ref[...] * (c_ref[...] + b_ref[...] * z_ref[...])).astype(o_ref.dtype)


def _hy_gate(uc2, gate_blk, zprev, zprev_blk, zc2, bias, out_dtype, *, tm):
    t = zc2.shape[0]
    w = HYENA_WIDTH
    return pl.pallas_call(
        _hy_gate_kernel,
        grid=(t // tm,),
        in_specs=[
            pl.BlockSpec((tm, w), lambda i: (i, gate_blk)),
            pl.BlockSpec((tm, w), lambda i: (i, zprev_blk)),
            pl.BlockSpec((tm, w), lambda i: (i, 0)),
            pl.BlockSpec((1, w), lambda i: (0, 0)),
        ],
        out_specs=pl.BlockSpec((tm, w), lambda i: (i, 0)),
        out_shape=jax.ShapeDtypeStruct((t, w), out_dtype),
        compiler_params=_cparams("parallel"),
        name="hy_gate",
    )(uc2, zprev, zc2, bias)


def _dft_mats(k_out, r_in, period, sign, scale, n_in, real_out):
    k = np.arange(k_out)[:, None]
    r = np.arange(r_in)[None, :]
    ang = 2.0 * np.pi * ((k * r) % period) / period
    fr = np.cos(ang) * scale
    fi = sign * np.sin(ang) * scale
    if real_out:
        mats = [fr, -fi]
    else:
        mats = [np.concatenate([fr, fi], 0), np.concatenate([-fi, fr], 0)]
    return jnp.asarray(np.stack(mats[:n_in], 0), dtype=F32).astype(BF)


def _twiddle(s_n, k_n, n, sign):
    s = lax.broadcasted_iota(jnp.int32, (s_n, k_n, LANES), 0)
    k = lax.broadcasted_iota(jnp.int32, (s_n, k_n, LANES), 1)
    ang = (s * k).astype(F32) * (2.0 * math.pi / n)
    return jnp.cos(ang), sign * jnp.sin(ang)


def _stage_kernel(*refs, n_in, k_out, sbk, tw, spec, real_out, transposed_out, flat):
    it = iter(refs)
    x_refs = [next(it) for _ in range(n_in)]
    g_ref = next(it)
    tw_refs = [next(it), next(it)] if tw else None
    spec_refs = [next(it) for _ in range(6)] if spec else None
    out_refs = [next(it)] if real_out else [next(it), next(it)]
    cb = out_refs[0].shape[-1]
    if spec:
        hfr, hfi, hbr, hbi, nf, nb = spec_refs
        inv = 1.0 / (nf[...] + nb[...])
    for j in range(sbk):
        acc = None
        for xi, x_ref in enumerate(x_refs):
            xv = x_ref[...] if flat else x_ref[0, :, j, :]
            d = _dot(g_ref[xi], xv.astype(BF))
            acc = d if acc is None else acc + d
        if real_out:
            ys = [acc]
        else:
            yr, yi = acc[:k_out], acc[k_out:]
            if tw:
                tr = jnp.concatenate([tw_refs[0][j]] * (cb // LANES), axis=1)
                ti = jnp.concatenate([tw_refs[1][j]] * (cb // LANES), axis=1)
                yr, yi = yr * tr - yi * ti, yr * ti + yi * tr
            if spec:
                sr = (hfr[0, :, j, :] + hbr[0, :, j, :]) * inv
                si = (hfi[0, :, j, :] - hbi[0, :, j, :]) * inv
                yr, yi = yr * sr - yi * si, yr * si + yi * sr
            ys = [yr, yi]
        for o_ref, y in zip(out_refs, ys):
            if flat:
                o_ref[...] = y.astype(o_ref.dtype)
            elif transposed_out:
                o_ref[0, j] = y.astype(o_ref.dtype)
            else:
                o_ref[0, :, j, :] = y.astype(o_ref.dtype)


def _fft_stage(xs, x_sel, gmat, *, r_in, s_n, k_out, n_groups, n_cblk, cb, transposed_out,
               real_out, out_dtype, tw=None, spec=None, spec_sel=None, sbk=8, name="fft_stage"):
    n_in = len(xs)
    flat = s_n == 1
    sbk = 1 if flat else sbk
    in_specs, args = [], []
    for x, sel in zip(xs, x_sel):
        if flat:
            in_specs.append(pl.BlockSpec((None, r_in, cb), lambda s, g, c, sel=sel: (sel(g, c)[0], 0, sel(g, c)[1])))
            args.append(x)
        else:
            xv = x.reshape(x.shape[0], x.shape[1] // s_n, s_n, x.shape[2])
            in_specs.append(pl.BlockSpec((1, r_in, sbk, cb),
                                         lambda s, g, c, sel=sel: (sel(g, c)[0], 0, s, sel(g, c)[1])))
            args.append(xv)
    in_specs.append(pl.BlockSpec(gmat.shape, lambda s, g, c: (0, 0, 0)))
    args.append(gmat)
    if tw is not None:
        for tarr in tw:
            in_specs.append(pl.BlockSpec((sbk, k_out, LANES), lambda s, g, c: (s, 0, 0)))
            args.append(tarr)
    if spec is not None:
        hr, hi, nrm = spec
        hr4 = hr.reshape(1, k_out, s_n, hr.shape[-1])
        hi4 = hi.reshape(1, k_out, s_n, hi.shape[-1])
        for arr, which in ((hr4, 0), (hi4, 0), (hr4, 1), (hi4, 1)):
            in_specs.append(pl.BlockSpec((1, k_out, sbk, cb),
                                         lambda s, g, c, which=which: (0, 0, s, spec_sel(g, c, which))))
            args.append(arr)
        for which in (0, 1):
            in_specs.append(pl.BlockSpec((1, cb), lambda s, g, c, which=which: (0, spec_sel(g, c, which))))
            args.append(nrm)
    ctot = n_cblk * cb
    if flat:
        oshape = (n_groups, k_out, ctot)
        ospec = pl.BlockSpec((None, k_out, cb), lambda s, g, c: (g, 0, c))
    elif transposed_out:
        oshape = (n_groups, s_n, k_out, ctot)
        ospec = pl.BlockSpec((1, sbk, k_out, cb), lambda s, g, c: (g, s, 0, c))
    else:
        oshape = (n_groups, k_out, s_n, ctot)
        ospec = pl.BlockSpec((1, k_out, sbk, cb), lambda s, g, c: (g, 0, s, c))
    n_out = 1 if real_out else 2
    kern = functools.partial(_stage_kernel, n_in=n_in, k_out=k_out, sbk=sbk, tw=tw is not None,
                             spec=spec is not None, real_out=real_out, transposed_out=transposed_out,
                             flat=flat)
    outs = pl.pallas_call(
        kern,
        grid=(s_n // sbk, n_groups, n_cblk),
        in_specs=in_specs,
        out_specs=[ospec] * n_out,
        out_shape=[jax.ShapeDtypeStruct(oshape, out_dtype)] * n_out,
        compiler_params=_cparams("parallel", "parallel", "parallel"),
        name=name,
    )(*args)
    return [o.reshape(n_groups, -1, ctot) for o in outs]


def _split_len(n):
    if n <= 1024:
        return n, 1
    s = 128
    return n // s, s


def _fft_forward(xs, x_sel, n, n_rows, *, n_groups, n_cblk, cb, spec=None, spec_sel=None,
                 out_dtype=F32, name="fwd"):
    n1, s = _split_len(n)
    n_in = len(xs)
    if s == 1:
        g = _dft_mats(n, n_rows, n, -1.0, 1.0, n_in, False)
        return _fft_stage(xs, x_sel, g, r_in=n_rows, s_n=1, k_out=n, n_groups=n_groups, n_cblk=n_cblk,
                          cb=cb, transposed_out=False, real_out=False, out_dtype=out_dtype,
                          spec=spec, spec_sel=spec_sel, name=name + "_direct")
    r1 = n_rows // s
    g1 = _dft_mats(n1, r1, n1, -1.0, 1.0, n_in, False)
    tw = _twiddle(s, n1, n, -1.0)
    ar, ai = _fft_stage(xs, x_sel, g1, r_in=r1, s_n=s, k_out=n1, n_groups=n_groups, n_cblk=n_cblk, cb=cb,
                        transposed_out=True, real_out=False, out_dtype=BF, tw=tw, name=name + "_s1")
    g2 = _dft_mats(s, s, s, -1.0, 1.0, 2, False)
    ident = lambda g, c: (g, c)
    return _fft_stage([ar, ai], [ident, ident], g2, r_in=s, s_n=n1, k_out=s, n_groups=n_groups,
                      n_cblk=n_cblk, cb=cb, transposed_out=False, real_out=False, out_dtype=out_dtype,
                      spec=spec, spec_sel=spec_sel, name=name + "_s2")


def _fft_inverse(pr, pi, n, n_keep, *, n_groups, n_cblk, cb, name="inv"):
    ident = lambda g, c: (g, c)
    n1, s = _split_len(n)
    if s == 1:
        g = _dft_mats(n_keep, n, n, 1.0, 1.0 / n, 2, False)
        return _fft_stage([pr, pi], [ident, ident], g, r_in=n, s_n=1, k_out=n_keep, n_groups=n_groups,
                          n_cblk=n_cblk, cb=cb, transposed_out=False, real_out=False, out_dtype=F32,
                          name=name + "_direct")
    g1 = _dft_mats(s, s, s, 1.0, 1.0, 2, False)
    tw = _twiddle(n1, s, n, 1.0)
    qr, qi = _fft_stage([pr, pi], [ident, ident], g1, r_in=s, s_n=n1, k_out=s, n_groups=n_groups,
                        n_cblk=n_cblk, cb=cb, transposed_out=True, real_out=False, out_dtype=BF, tw=tw,
                        name=name + "_s1")
    k2 = n_keep // s
    g2 = _dft_mats(k2, n1, n1, 1.0, 1.0 / n, 2, False)
    return _fft_stage([qr, qi], [ident, ident], g2, r_in=n1, s_n=s, k_out=k2, n_groups=n_groups,
                      n_cblk=n_cblk, cb=cb, transposed_out=False, real_out=False, out_dtype=F32,
                      name=name + "_s2")


def _hyena(uc, h_filt, nrm, hy_bias, out_rows):
    b, n, _ = uc.shape
    w = HYENA_WIDTH
    nfft = 2 * n
    hr, hi = _fft_forward([h_filt[None]], [lambda g, c: (0, c)], nfft, n, n_groups=1, n_cblk=4, cb=w,
                          name="hy_filt_fft")
    uc2 = uc.reshape(b * n, 3 * w)
    z, zblk = uc, 2
    z2 = uc2
    for o in range(HYENA_ORDER):
        sel_r = lambda g, c, zblk=zblk: (0, zblk)
        sel_i = lambda g, c, zblk=zblk: (1, zblk)
        spec_sel = lambda g, c, which, o=o: which * HYENA_ORDER + o
        pr, pi = _fft_forward([z, z], [sel_r, sel_i], nfft, n, n_groups=1, n_cblk=1, cb=w,
                              spec=(hr[0], hi[0], nrm), spec_sel=spec_sel, out_dtype=BF, name="hy_fwd")
        cr, ci = _fft_inverse(pr, pi, nfft, n, n_groups=1, n_cblk=1, cb=w, name="hy_inv")
        zc2 = jnp.concatenate([cr, ci], axis=0).reshape(b * n, w)
        last = o == HYENA_ORDER - 1
        znext = _hy_gate(uc2, o, z2, zblk, zc2, hy_bias[o:o + 1], BF if last else F32, tm=out_rows)
        z, zblk, z2 = znext.reshape(b, n, w), 0, znext
    return z2


def _fnet(pq):
    b, n, _ = pq.shape
    w = FNET_WIDTH
    scale = 1.0 / math.sqrt(n * FNET_GROUP_DIM)
    sel_r = lambda g, c: (g, 0)
    sel_i = lambda g, c: (g, 1)
    ident = lambda g, c: (g, c)
    n1, s = _split_len(n)
    if s == 1:
        g = _dft_mats(n, n, n, -1.0, scale, 2, True)
        (y,) = _fft_stage([pq, pq], [sel_r, sel_i], g, r_in=n, s_n=1, k_out=n, n_groups=b, n_cblk=1, cb=w,
                          transposed_out=False, real_out=True, out_dtype=BF, name="fnet_direct")
        return y.reshape(b * n, w)
    g1 = _dft_mats(n1, n1, n1, -1.0, 1.0, 2, False)
    tw = _twiddle(s, n1, n, -1.0)
    ar, ai = _fft_stage([pq, pq], [sel_r, sel_i], g1, r_in=n1, s_n=s, k_out=n1, n_groups=b, n_cblk=1, cb=w,
                        transposed_out=True, real_out=False, out_dtype=BF, tw=tw, name="fnet_s1")
    g2 = _dft_mats(s, s, s, -1.0, scale, 2, True)
    (y,) = _fft_stage([ar, ai], [ident, ident], g2, r_in=s, s_n=n1, k_out=s, n_groups=b, n_cblk=1, cb=w,
                      transposed_out=False, real_out=True, out_dtype=BF, name="fnet_s2")
    return y.reshape(b * n, w)


def _merge_kernel(x_ref, gt_ref, a_ref, h_ref, f_ref, gate_ref, wa_ref, wh_ref, wf_ref, wo_ref, o_ref):
    d = D_MODEL
    m = gate_ref[:, 0:d].astype(F32) * _dot(a_ref[...], wa_ref[...])
    m = m + gate_ref[:, d:2 * d].astype(F32) * _dot(h_ref[...], wh_ref[...])
    m = m + gate_ref[:, 2 * d:3 * d].astype(F32) * _dot(f_ref[...], wf_ref[...])
    y = _dot(m.astype(BF), wo_ref[...])
    o_ref[...] = x_ref[...] + gt_ref[0] * y


def _merge(x2, gt, attn_o, hy_o, fn_o, gates, wa, wh, wf, wo, *, tm, mod_row):
    t, d = x2.shape
    tok = lambda i: (i, 0)
    full = lambda i: (0, 0)
    return pl.pallas_call(
        _merge_kernel,
        grid=(t // tm,),
        in_specs=[
            pl.BlockSpec((tm, d), tok),
            pl.BlockSpec((1, 1, d), lambda i: (mod_row(i), 0, 0)),
            pl.BlockSpec((tm, ATTN_WIDTH), tok),
            pl.BlockSpec((tm, HYENA_WIDTH), tok),
            pl.BlockSpec((tm, FNET_WIDTH), tok),
            pl.BlockSpec((tm, 3 * d), tok),
            pl.BlockSpec(wa.shape, full), pl.BlockSpec(wh.shape, full),
            pl.BlockSpec(wf.shape, full), pl.BlockSpec(wo.shape, full),
        ],
        out_specs=pl.BlockSpec((tm, d), tok),
        out_shape=jax.ShapeDtypeStruct((t, d), F32),
        compiler_params=_cparams("parallel"),
        name="merge",
    )(x2, gt, attn_o, hy_o, fn_o, gates, wa, wh, wf, wo)


def _ffn_kernel(x_ref, sh_ref, sc_ref, gt_ref, g_ref, wg_ref, wu_ref, wd_ref, o_ref, h_scr, acc_scr):
    f = pl.program_id(1)

    @pl.when(f == 0)
    def _():
        h_scr[...] = _modulated_norm(x_ref[...], g_ref[...], sc_ref[0], sh_ref[0]).astype(BF)
        acc_scr[...] = jnp.zeros_like(acc_scr)

    hb = h_scr[...]
    act = _silu(_dot(hb, wg_ref[...])) * _dot(hb, wu_ref[...])
    acc_scr[...] += _dot(act.astype(BF), wd_ref[...])

    @pl.when(f == pl.num_programs(1) - 1)
    def _():
        o_ref[...] = x_ref[...] + gt_ref[0] * acc_scr[...]


def _ffn(x2, sh, sc, gt, g, wg, wu, wd, *, tm, tf, mod_row):
    t, d = x2.shape
    ff = wg.shape[1]
    row3 = lambda i, f: (mod_row(i), 0, 0)
    return pl.pallas_call(
        _ffn_kernel,
        grid=(t // tm, ff // tf),
        in_specs=[
            pl.BlockSpec((tm, d), lambda i, f: (i, 0)),
            pl.BlockSpec((1, 1, d), row3), pl.BlockSpec((1, 1, d), row3), pl.BlockSpec((1, 1, d), row3),
            pl.BlockSpec((1, d), lambda i, f: (0, 0)),
            pl.BlockSpec((d, tf), lambda i, f: (0, f)),
            pl.BlockSpec((d, tf), lambda i, f: (0, f)),
            pl.BlockSpec((tf, d), lambda i, f: (f, 0)),
        ],
        out_specs=pl.BlockSpec((tm, d), lambda i, f: (i, 0)),
        out_shape=jax.ShapeDtypeStruct((t, d), F32),
        scratch_shapes=[pltpu.VMEM((tm, d), BF), pltpu.VMEM((tm, d), F32)],
        compiler_params=_cparams("parallel", "arbitrary"),
        name="ffn_dense",
    )(x2, sh, sc, gt, g, wg, wu, wd)


def _top2(logits):
    lane = lax.broadcasted_iota(jnp.int32, logits.shape, 1)
    lg = jnp.where(lane < N_EXPERTS, logits, -jnp.inf)
    m1 = jnp.max(lg, axis=1, keepdims=True)
    i1 = jnp.min(jnp.where(lg == m1, lane, LANES), axis=1, keepdims=True)
    lg2 = jnp.where(lane == i1, -jnp.inf, lg)
    m2 = jnp.max(lg2, axis=1, keepdims=True)
    i2 = jnp.min(jnp.where(lg2 == m2, lane, LANES), axis=1, keepdims=True)
    e2 = jnp.exp(m2 - m1)
    w1 = 1.0 / (1.0 + e2)
    return i1, i2, w1, e2 * w1


GROUP_TILE = 256
SLOT_RADIX = 64.0


def _moe_group_kernel(x_ref, sh_ref, sc_ref, g_ref, wr_ref, xg_ref, ws_ref, slot_ref, cnt_ref,
                      h_scr, rows_scr, wm_scr):
    j = pl.program_id(1)
    tb = x_ref.shape[0]
    gt_rows = xg_ref.shape[1]

    @pl.when(j == 0)
    def _():
        h = _modulated_norm(x_ref[...], g_ref[...], sc_ref[0], sh_ref[0])
        h_scr[...] = h.astype(BF)
        i1, i2, w1, w2 = _top2(_dot3(h, wr_ref[...]))
        lane = lax.broadcasted_iota(jnp.int32, (tb, LANES), 1)
        oh0 = jnp.where(lane == i1, 1.0, 0.0)
        oh1 = jnp.where(lane == i2, 1.0, 0.0)
        c0 = jnp.sum(oh0, axis=0, keepdims=True)
        cnt = c0 + jnp.sum(oh1, axis=0, keepdims=True)
        tri = jnp.where(lax.broadcasted_iota(jnp.int32, (tb, tb), 1)
                        < lax.broadcasted_iota(jnp.int32, (tb, tb), 0), 1.0, 0.0).astype(BF)
        pre0 = _dot(tri, oh0.astype(BF))
        pre1 = _dot(tri, oh1.astype(BF)) + c0
        tiles = jnp.ceil(cnt * (1.0 / gt_rows))
        upper = jnp.where(lax.broadcasted_iota(jnp.int32, (LANES, LANES), 0)
                          < lax.broadcasted_iota(jnp.int32, (LANES, LANES), 1), 1.0, 0.0).astype(BF)
        off = _dot(jnp.broadcast_to(tiles, (8, LANES)).astype(BF), upper)[0:1] * float(gt_rows)
        slot0 = jnp.sum(oh0 * (off + pre0), axis=1, keepdims=True)
        slot1 = jnp.sum(oh1 * (off + pre1), axis=1, keepdims=True)
        slot_ref[0] = jnp.where(lane == 0, slot0, jnp.where(lane == 1, slot1, 0.0))
        cnt_ref[0] = jnp.broadcast_to(cnt, (8, LANES))
        hi0 = jnp.floor(slot0 * (1.0 / SLOT_RADIX))
        hi1 = jnp.floor(slot1 * (1.0 / SLOT_RADIX))
        digits = jnp.where(lane == 0, hi0, jnp.where(lane == 1, slot0 - SLOT_RADIX * hi0,
                           jnp.where(lane == 2, hi1, jnp.where(lane == 3, slot1 - SLOT_RADIX * hi1, 0.0))))
        sel = jnp.where(lax.broadcasted_iota(jnp.int32, (8, LANES), 0)
                        == lax.broadcasted_iota(jnp.int32, (8, LANES), 1), 1.0, 0.0).astype(BF)
        rows_scr[...] = _dot_nt(sel, digits.astype(BF))
        w1h, w1l = _split(w1)
        w1m, w1l = _split(w1 - w1h.astype(F32))
        w2h, w2l = _split(w2)
        w2m, w2l = _split(w2 - w2h.astype(F32))
        cols = [w1h, w1m, w1l, w2h, w2m, w2l]
        wm = jnp.zeros((tb, LANES), F32)
        for li, col in enumerate(cols):
            wm = jnp.where(lane == li, col.astype(F32), wm)
        wm_scr[...] = wm.astype(BF)

    rows = rows_scr[...]
    s0 = rows[0:1] * SLOT_RADIX + rows[1:2]
    s1 = rows[2:3] * SLOT_RADIX + rows[3:4]
    pos = (lax.broadcasted_iota(jnp.int32, (gt_rows, tb), 0) + j * gt_rows).astype(F32)
    g0 = jnp.where(pos == s0, 1.0, 0.0).astype(BF)
    g1 = jnp.where(pos == s1, 1.0, 0.0).astype(BF)
    xg_ref[0] = _dot(g0 + g1, h_scr[...]).astype(BF)
    lane_w = lax.broadcasted_iota(jnp.int32, (gt_rows, LANES), 1)
    wsum = (jnp.where(lane_w < 3, _dot(g0, wm_scr[...]), 0.0)
            + jnp.where((lane_w >= 3) & (lane_w < 6), _dot(g1, wm_scr[...]), 0.0))
    ws_ref[0] = jnp.broadcast_to(jnp.sum(wsum, axis=1, keepdims=True), (gt_rows, LANES))


def _moe_group(x2, sh, sc, g, wr_pad, *, tb, nt, mod_row):
    t, d = x2.shape
    nb = t // tb
    row3 = lambda b, j: (mod_row(b), 0, 0)
    return pl.pallas_call(
        _moe_group_kernel,
        grid=(nb, nt),
        in_specs=[
            pl.BlockSpec((tb, d), lambda b, j: (b, 0)),
            pl.BlockSpec((1, 1, d), row3), pl.BlockSpec((1, 1, d), row3),
            pl.BlockSpec((1, d), lambda b, j: (0, 0)),
            pl.BlockSpec((d, LANES), lambda b, j: (0, 0)),
        ],
        out_specs=[
            pl.BlockSpec((1, GROUP_TILE, d), lambda b, j: (b * nt + j, 0, 0)),
            pl.BlockSpec((1, GROUP_TILE, LANES), lambda b, j: (b * nt + j, 0, 0)),
            pl.BlockSpec((1, tb, LANES), lambda b, j: (b, 0, 0)),
            pl.BlockSpec((1, 8, LANES), lambda b, j: (b, 0, 0)),
        ],
        out_shape=[
            jax.ShapeDtypeStruct((nb * nt, GROUP_TILE, d), BF),
            jax.ShapeDtypeStruct((nb * nt, GROUP_TILE, LANES), F32),
            jax.ShapeDtypeStruct((nb, tb, LANES), F32),
            jax.ShapeDtypeStruct((nb, 8, LANES), F32),
        ],
        scratch_shapes=[pltpu.VMEM((tb, d), BF), pltpu.VMEM((8, tb), F32), pltpu.VMEM((tb, LANES), BF)],
        compiler_params=_cparams("parallel", "arbitrary"),
        name="moe_group",
    )(x2, sh, sc, g, wr_pad)


def _moe_schedule(cnt, nt):
    tiles = (cnt + GROUP_TILE - 1) // GROUP_TILE
    nb, ne = tiles.shape
    tot_e = tiles.sum(0)
    cum_e = jnp.cumsum(tot_e)
    n_used = cum_e[-1]
    pos = jnp.minimum(jnp.arange(nb * nt, dtype=jnp.int32), n_used - 1)
    e = jnp.sum(pos[:, None] >= cum_e[None, :], axis=1).astype(jnp.int32)
    r = pos - (cum_e - tot_e)[e]
    cum_b = jnp.cumsum(tiles, axis=0)
    cum_b_e = cum_b.T[e]
    blk = jnp.sum(r[:, None] >= cum_b_e, axis=1).astype(jnp.int32)
    first_in_blk = (jnp.cumsum(tiles, axis=1) - tiles)[blk, e]
    j = first_in_blk + r - (cum_b[blk, e] - tiles[blk, e])
    return ((blk * nt + j).astype(jnp.int32), e, n_used.astype(jnp.int32).reshape(1),
            tiles.sum(1).astype(jnp.int32))


def _moe_expert_kernel(tile_ref, exp_ref, nused_ref, xg_ref, ws_ref, wg_ref, wu_ref, wd_ref, y_ref):
    @pl.when(pl.program_id(0) < nused_ref[0])
    def _():
        x = xg_ref[0]
        act = _silu(_dot(x, wg_ref[0])) * _dot(x, wu_ref[0])
        y = _dot(act.astype(BF), wd_ref[0])
        w = ws_ref[0]
        y_ref[0] = (y * jnp.concatenate([w] * (y.shape[1] // LANES), axis=1)).astype(BF)


def _moe_experts(sched_tile, sched_exp, n_used, xg, ws, wg, wu, wd):
    n_sched, gt_rows, d = xg.shape
    ff = wg.shape[2]
    tile3 = lambda i, st, se, nu: (st[i], 0, 0)
    exp3 = lambda i, st, se, nu: (se[i], 0, 0)
    return pl.pallas_call(
        _moe_expert_kernel,
        grid_spec=pltpu.PrefetchScalarGridSpec(
            num_scalar_prefetch=3,
            grid=(n_sched,),
            in_specs=[
                pl.BlockSpec((1, gt_rows, d), tile3),
                pl.BlockSpec((1, gt_rows, LANES), tile3),
                pl.BlockSpec((1, d, ff), exp3),
                pl.BlockSpec((1, d, ff), exp3),
                pl.BlockSpec((1, ff, d), exp3),
            ],
            out_specs=pl.BlockSpec((1, gt_rows, d), tile3),
        ),
        out_shape=jax.ShapeDtypeStruct((n_sched, gt_rows, d), BF),
        compiler_params=_cparams("arbitrary"),
        name="moe_experts",
    )(sched_tile, sched_exp, n_used, xg, ws, wg, wu, wd)


def _moe_combine_kernel(nt_ref, x_ref, gt_ref, slot_ref, y_ref, o_ref, acc_scr):
    b = pl.program_id(0)
    j = pl.program_id(1)
    tb = x_ref.shape[0]
    gt_rows = y_ref.shape[1]

    @pl.when(j == 0)
    def _():
        acc_scr[...] = jnp.zeros_like(acc_scr)

    @pl.when(j < nt_ref[b])
    def _():
        sl = slot_ref[0]
        pos = (lax.broadcasted_iota(jnp.int32, (tb, gt_rows), 1) + j * gt_rows).astype(F32)
        p = jnp.where((pos == sl[:, 0:1]) | (pos == sl[:, 1:2]), 1.0, 0.0).astype(BF)
        acc_scr[...] += _dot(p, y_ref[0])

    @pl.when(j == pl.num_programs(1) - 1)
    def _():
        o_ref[...] = x_ref[...] + gt_ref[0] * acc_scr[...]


def _moe_combine(ntiles_b, x2, gt, slots, yg, *, tb, nt, mod_row):
    t, d = x2.shape
    nb = t // tb
    return pl.pallas_call(
        _moe_combine_kernel,
        grid_spec=pltpu.PrefetchScalarGridSpec(
            num_scalar_prefetch=1,
            grid=(nb, nt),
            in_specs=[
                pl.BlockSpec((tb, d), lambda b, j, n: (b, 0)),
                pl.BlockSpec((1, 1, d), lambda b, j, n: (mod_row(b), 0, 0)),
                pl.BlockSpec((1, tb, LANES), lambda b, j, n: (b, 0, 0)),
                pl.BlockSpec((1, GROUP_TILE, d), lambda b, j, n: (b * nt + jnp.minimum(j, n[b] - 1), 0, 0)),
            ],
            out_specs=pl.BlockSpec((tb, d), lambda b, j, n: (b, 0)),
            scratch_shapes=[pltpu.VMEM((tb, d), F32)],
        ),
        out_shape=jax.ShapeDtypeStruct((t, d), F32),
        compiler_params=_cparams("parallel", "arbitrary"),
        name="moe_combine",
    )(ntiles_b, x2, gt, slots, yg)


def _moe(x2, sh, sc, gt, g, wr_pad, wg, wu, wd, *, tb, mod_row):
    nt = 2 * tb // GROUP_TILE + N_EXPERTS - 1
    xg, ws, slots, cnt = _moe_group(x2, sh, sc, g, wr_pad, tb=tb, nt=nt, mod_row=mod_row)
    counts = cnt[:, 0, :N_EXPERTS].astype(jnp.int32)
    sched_tile, sched_exp, n_used, ntiles_b = _moe_schedule(counts, nt)
    yg = _moe_experts(sched_tile, sched_exp, n_used, xg, ws, wg, wu, wd)
    return _moe_combine(ntiles_b, x2, gt, slots, yg, tb=tb, nt=nt, mod_row=mod_row)


def _rope_tables(seq_len):
    pos = np.arange(seq_len)
    prow = (pos // GRID_W).astype(np.float32)
    pcol = (pos % GRID_W).astype(np.float32)
    n_freq = HEAD_DIM // 4
    inv = (np.float32(ROPE_THETA) ** (-np.arange(n_freq, dtype=np.float32) / n_freq)).astype(np.float32)
    ar = jnp.asarray(prow)[:, None] * jnp.asarray(inv)[None, :]
    ac = jnp.asarray(pcol)[:, None] * jnp.asarray(inv)[None, :]
    cos = jnp.concatenate([jnp.cos(ar)] * 2 + [jnp.cos(ac)] * 2, axis=1)
    sin = jnp.concatenate([-jnp.sin(ar), jnp.sin(ar), -jnp.sin(ac), jnp.sin(ac)], axis=1)
    return jnp.concatenate([cos, cos], axis=1), jnp.concatenate([sin, sin], axis=1)


def _head_sum_matrix():
    c = np.arange(ATTN_WIDTH)
    return jnp.asarray((c[:, None] // HEAD_DIM) == (c[None, :] // HEAD_DIM), dtype=F32).astype(BF)


def _fnet_channel_matrix():
    c = np.arange(FNET_WIDTH)
    same = (c[:, None] // FNET_GROUP_DIM) == (c[None, :] // FNET_GROUP_DIM)
    ang = 2.0 * np.pi * (((c[:, None] % FNET_GROUP_DIM) * (c[None, :] % FNET_GROUP_DIM)) % FNET_GROUP_DIM) / FNET_GROUP_DIM
    cb = np.where(same, np.cos(ang), 0.0)
    sb = np.where(same, np.sin(ang), 0.0)
    return jnp.asarray(np.concatenate([cb, -sb], axis=1), dtype=F32).astype(BF)


def _filter_features(n):
    t = jnp.linspace(0.0, 1.0, n, dtype=F32)[:, None]
    w = 2.0 * math.pi * jnp.arange(n, dtype=F32)[:, None] / n
    fb = jnp.linspace(1e-4, FILTER_BANDS - 1, FILTER_BANDS, dtype=F32)
    z = jnp.concatenate([t, jnp.cos(fb * w), -jnp.sin(fb * w)], axis=-1)
    return jnp.pad(z, ((0, 0), (0, 64 - z.shape[1])))


def _decay_rates():
    d = jnp.abs(jnp.linspace(math.log(DECAY_TARGET) / SLOW_DECAY_PCT, math.log(DECAY_TARGET) / FAST_DECAY_PCT,
                             HYENA_WIDTH, dtype=F32))
    return jnp.concatenate([d] * (2 * HYENA_ORDER))[None, :]


def kernel(x, c, ctx, c_ctx, w_ada, b_ada, norm1_g, norm2_g, w_in, q_norm_g, k_norm_g, attn_sink,
           hy_conv_w, hy_conv_b, hy_filt_w1, hy_filt_b1, hy_filt_freq1, hy_filt_w2, hy_filt_b2,
           hy_filt_freq2, hy_filt_w3, hy_bias, w_proj_attn, w_proj_hyena, w_proj_fnet, w_out,
           ffn_w_gate, ffn_w_up, ffn_w_down, moe_router, moe_w_gate, moe_w_up, moe_w_down):
    b, seq, d = x.shape
    n_ctx = ctx.shape[1]
    depth = w_ada.shape[0]
    tm = 256
    tiles_per_seq = seq // tm

    cond8 = jnp.concatenate([c, c_ctx[None, :], jnp.zeros((8 - b - 1, d), F32)], axis=0)
    mods = _adaln(cond8, w_ada, b_ada)

    cos_l, sin_l = _rope_tables(seq)
    cos_c = jnp.ones((n_ctx, LANES), F32)
    sin_c = jnp.zeros((n_ctx, LANES), F32)
    gsum = _head_sum_matrix()
    mfn = _fnet_channel_matrix()
    deltas4 = _decay_rates()
    zfeat_l = _filter_features(seq)
    zfeat_c = _filter_features(n_ctx)

    lat_row = lambda i: i // tiles_per_seq
    ctx_row = lambda i: b
    lat_tab = lambda i: i % tiles_per_seq
    ctx_tab = lambda i: 0
    tm_c = min(tm, n_ctx)

    xs = x.reshape(b * seq, d)
    cs = ctx.reshape(b * n_ctx, d)
    for l in range(depth):
        last = l == depth - 1
        mod = lambda j: mods[l, :, j * d:(j + 1) * d].reshape(8, 1, d)
        w_in_bf = w_in[l].astype(BF)
        qg = jnp.tile(q_norm_g[l], N_HEADS)[None, :]
        kg = jnp.tile(k_norm_g[l], N_KV_HEADS)[None, :]
        g1 = norm1_g[l][None, :]
        wa, wh, wf, wo = (w_proj_attn[l].astype(BF), w_proj_hyena[l].astype(BF),
                          w_proj_fnet[l].astype(BF), w_out[l].astype(BF))
        conv_w = hy_conv_w[l].reshape(3, -1)
        conv_b = hy_conv_b[l][None, :]
        w1p = jnp.pad(hy_filt_w1[l], ((0, 64 - hy_filt_w1.shape[1]), (0, 0)))
        filt = (w1p, hy_filt_b1[l][None, :], hy_filt_freq1[l][None, :], hy_filt_w2[l],
                hy_filt_b2[l][None, :], hy_filt_freq2[l][None, :], hy_filt_w3[l], deltas4)

        q_c, kd_c, vd_c, u_c, pq_c, gates_c = _phase_a(
            cs, mod(0), mod(1), g1, w_in_bf, cos_c, sin_c, qg, kg, gsum, mfn,
            tm=tm_c, mod_row=ctx_row, tab_row=ctx_tab)
        kd_c3 = kd_c.reshape(b, n_ctx, -1)
        vd_c3 = vd_c.reshape(b, n_ctx, -1)

        q_l, kd_l, vd_l, u_l, pq_l, gates_l = _phase_a(
            xs, mod(0), mod(1), g1, w_in_bf, cos_l, sin_l, qg, kg, gsum, mfn,
            tm=tm, mod_row=lat_row, tab_row=lat_tab)
        attn_l = _attention(attn_sink[l], q_l.reshape(b, seq, -1), kd_l.reshape(b, seq, -1),
                            vd_l.reshape(b, seq, -1), kd_c3, vd_c3, local=True, tq=512)
        h_l, nrm_l = _hy_filter(zfeat_l, *filt, tm=512)
        uc_l = _hy_prep(u_l.reshape(b, seq, -1), conv_w, conv_b)
        hy_l = _hyena(uc_l, h_l, nrm_l, hy_bias[l], out_rows=512)
        fn_l = _fnet(pq_l.reshape(b, seq, -1))
        xs = _merge(xs, mod(2), attn_l.reshape(b * seq, -1), hy_l, fn_l, gates_l, wa, wh, wf, wo,
                    tm=tm, mod_row=lat_row)

        if not last:
            attn_c = _attention(attn_sink[l], q_c.reshape(b, n_ctx, -1), kd_c3, vd_c3, kd_c3, vd_c3,
                                local=False, tq=n_ctx)
            h_c, nrm_c = _hy_filter(zfeat_c, *filt, tm=n_ctx)
            uc_c = _hy_prep(u_c.reshape(b, n_ctx, -1), conv_w, conv_b)
            hy_c = _hyena(uc_c, h_c, nrm_c, hy_bias[l], out_rows=n_ctx)
            fn_c = _fnet(pq_c.reshape(b, n_ctx, -1))
            cs = _merge(cs, mod(2), attn_c.reshape(b * n_ctx, -1), hy_c, fn_c, gates_c, wa, wh, wf, wo,
                        tm=tm_c, mod_row=ctx_row)

        g2 = norm2_g[l][None, :]
        i = l // 2
        if l % 2 == 0:
            wg, wu, wd = ffn_w_gate[i].astype(BF), ffn_w_up[i].astype(BF), ffn_w_down[i].astype(BF)
            run = lambda t2, rows, tmm: _ffn(t2, mod(3), mod(4), mod(5), g2, wg, wu, wd,
                                             tm=tmm, tf=D_FF // 2, mod_row=rows)
        else:
            wr = jnp.pad(moe_router[i], ((0, 0), (0, LANES - N_EXPERTS)))
            wg, wu, wd = moe_w_gate[i].astype(BF), moe_w_up[i].astype(BF), moe_w_down[i].astype(BF)
            run = lambda t2, rows, tmm: _moe(t2, mod(3), mod(4), mod(5), g2, wr, wg, wu, wd,
                                             tb=tmm, mod_row=rows)
        tm_ffn = 512 if l % 2 == 0 else 1024
        xs = run(xs, lambda t: t // (seq // tm_ffn), tm_ffn)
        if not last:
            cs = run(cs, ctx_row, min(tm_ffn, b * n_ctx))
    return xs.reshape(b, seq, d)
```

```python
import functools
import math

import numpy as np
import jax
import jax.numpy as jnp
from jax import lax
from jax.experimental import pallas as pl
from jax.experimental.pallas import tpu as pltpu

F32 = jnp.float32
BF = jnp.bfloat16

D_MODEL = 1024
DEPTH = 4
GRID_W = 64
HEAD_DIM = 64
N_HEADS = 8
N_KV_HEADS = 2
ATTN_WIDTH = N_HEADS * HEAD_DIM
KV_WIDTH = N_KV_HEADS * HEAD_DIM
WINDOW = 128
QBLK = 128
ROPE_THETA = 10000.0
HYENA_ORDER = 2
HYENA_WIDTH = 256
FILTER_BANDS = 16
FILTER_HIDDEN = 64
DECAY_TARGET = 1e-2
FAST_DECAY_PCT = 0.3
SLOW_DECAY_PCT = 1.5
FNET_WIDTH = 256
FNET_GROUP_DIM = 64
Q_END = ATTN_WIDTH
K_END = Q_END + KV_WIDTH
V_END = K_END + KV_WIDTH
HY_END = V_END + (HYENA_ORDER + 1) * HYENA_WIDTH
FN_END = HY_END + FNET_WIDTH
IN_WIDTH = FN_END + 3 * D_MODEL
D_FF = 2816
N_EXPERTS = 8
EPS = 1e-6
LANES = 128
NEG = -1e30

VMEM_LIMIT = 56 * 1024 * 1024


def _cparams(*sem):
    return pltpu.CompilerParams(dimension_semantics=sem, vmem_limit_bytes=VMEM_LIMIT)


def _dot(a, b):
    return jnp.dot(a, b, preferred_element_type=F32)


def _dot_nt(a, b):
    return lax.dot_general(a, b, (((1,), (1,)), ((), ())), preferred_element_type=F32)


def _split(a):
    hi = a.astype(BF)
    lo = (a - hi.astype(F32)).astype(BF)
    return hi, lo


def _dot3(a, b):
    ah, al = _split(a)
    bh, bl = _split(b)
    return _dot(ah, bh) + (_dot(ah, bl) + _dot(al, bh))


def _dot2(a, b_bf16):
    ah, al = _split(a)
    return _dot(ah, b_bf16) + _dot(al, b_bf16)


def _silu(v):
    return v * jax.nn.sigmoid(v)


def _adaln_kernel(c_ref, w_ref, b_ref, o_ref):
    o_ref[0] = _dot3(_silu(c_ref[...]), w_ref[0]) + b_ref[0]


def _adaln(cond8, w_ada, b_ada):
    depth, d, n6 = w_ada.shape
    tn = 1024
    return pl.pallas_call(
        _adaln_kernel,
        grid=(depth, n6 // tn),
        in_specs=[
            pl.BlockSpec((8, d), lambda l, j: (0, 0)),
            pl.BlockSpec((1, d, tn), lambda l, j: (l, 0, j)),
            pl.BlockSpec((1, 1, tn), lambda l, j: (l, 0, j)),
        ],
        out_specs=pl.BlockSpec((1, 8, tn), lambda l, j: (l, 0, j)),
        out_shape=jax.ShapeDtypeStruct((depth, 8, n6), F32),
        compiler_params=_cparams("parallel", "parallel"),
        name="adaln",
    )(cond8, w_ada, b_ada.reshape(depth, 1, n6))


def _modulated_norm(x, g, sc, sh):
    ms = jnp.mean(x * x, axis=-1, keepdims=True)
    h = (x * lax.rsqrt(ms + EPS)) * g
    return h * (1.0 + sc) + sh


def _phase_a_kernel(x_ref, sh_ref, sc_ref, g_ref, w_ref, cos_ref, sin_ref, qg_ref, kg_ref,
                    gsum_ref, mfn_ref, q_ref, kd_ref, vd_ref, u_ref, pq_ref, gate_ref):
    tm = x_ref.shape[0]
    hb = _modulated_norm(x_ref[...], g_ref[...], sc_ref[0], sh_ref[0]).astype(BF)
    cos = cos_ref[...]
    sin = sin_ref[...]

    def headnorm(t, gain, gs):
        ss = _dot2(t * t, gs)
        return t * lax.rsqrt(ss * (1.0 / HEAD_DIM) + EPS) * gain

    def rope(t, cosw, sinw):
        w = t.shape[1]
        nxt = pltpu.roll(t, w - 16, axis=1)
        prv = pltpu.roll(t, 16, axis=1)
        lw = lax.broadcasted_iota(jnp.int32, t.shape, 1)
        return t * cosw + jnp.where((lw % 32) < 16, nxt, prv) * sinw

    def dup_halves(t):
        lane = lax.broadcasted_iota(jnp.int32, t.shape, 1)
        sw = pltpu.roll(t, 64, axis=1)
        lo = lane < 64
        return jnp.concatenate([jnp.where(lo, t, sw), jnp.where(lo, sw, t)], axis=1)

    pq = _dot(hb, w_ref[:, 0:Q_END])
    qn = headnorm(pq, qg_ref[...], gsum_ref[...])
    cos4 = jnp.concatenate([cos] * 4, axis=1)
    sin4 = jnp.concatenate([sin] * 4, axis=1)
    q_ref[...] = (rope(qn, cos4, sin4) * (HEAD_DIM ** -0.5)).astype(BF)

    pk = _dot(hb, w_ref[:, Q_END:K_END])
    kn = headnorm(pk, kg_ref[...], gsum_ref[0:KV_WIDTH, 0:KV_WIDTH])
    kd_ref[...] = dup_halves(rope(kn, cos, sin)).astype(BF)
    vd_ref[...] = dup_halves(_dot(hb, w_ref[:, K_END:V_END])).astype(BF)

    u_ref[...] = _dot(hb, w_ref[:, V_END:HY_END])
    f = _dot(hb, w_ref[:, HY_END:FN_END])
    pq_ref[...] = _dot(f.astype(BF), mfn_ref[...])
    for i in range(3):
        lo = FN_END + i * D_MODEL
        gate_ref[:, i * D_MODEL:(i + 1) * D_MODEL] = jax.nn.sigmoid(
            _dot(hb, w_ref[:, lo:lo + D_MODEL])).astype(BF)


def _phase_a(x2, sh, sc, g, w_in_bf, cos_t, sin_t, qg, kg, gsum, mfn, *, tm, mod_row, tab_row):
    t, d = x2.shape
    row3 = lambda i: (mod_row(i), 0, 0)
    full = lambda i: (0, 0)
    tok = lambda i: (i, 0)
    outs = [
        ((t, ATTN_WIDTH), BF), ((t, 2 * KV_WIDTH), BF), ((t, 2 * KV_WIDTH), BF),
        ((t, 3 * HYENA_WIDTH), F32), ((t, 2 * FNET_WIDTH), F32), ((t, 3 * D_MODEL), BF),
    ]
    return pl.pallas_call(
        _phase_a_kernel,
        grid=(t // tm,),
        in_specs=[
            pl.BlockSpec((tm, d), tok),
            pl.BlockSpec((1, 1, d), row3),
            pl.BlockSpec((1, 1, d), row3),
            pl.BlockSpec((1, d), full),
            pl.BlockSpec((d, IN_WIDTH), full),
            pl.BlockSpec((tm, LANES), lambda i: (tab_row(i), 0)),
            pl.BlockSpec((tm, LANES), lambda i: (tab_row(i), 0)),
            pl.BlockSpec((1, ATTN_WIDTH), full),
            pl.BlockSpec((1, KV_WIDTH), full),
            pl.BlockSpec((ATTN_WIDTH, ATTN_WIDTH), full),
            pl.BlockSpec((FNET_WIDTH, 2 * FNET_WIDTH), full),
        ],
        out_specs=[pl.BlockSpec((tm, s[1]), tok) for s, _ in outs],
        out_shape=[jax.ShapeDtypeStruct(s, dt) for s, dt in outs],
        compiler_params=_cparams("parallel"),
        name="phase_a",
    )(x2, sh, sc, g, w_in_bf, cos_t, sin_t, qg, kg, gsum, mfn)


def _attn_kernel(sink_ref, q_ref, kd_ref, vd_ref, kc_ref, vc_ref, o_ref, *, local, seq_len):
    tq = q_ref.shape[1]
    nblk = tq // QBLK
    gq = N_HEADS // N_KV_HEADS
    rows = gq * QBLK
    lane = lax.broadcasted_iota(jnp.int32, (QBLK, LANES), 1)
    lo_half = lane < 64
    hrow = lax.broadcasted_iota(jnp.int32, (rows, 1), 0) // QBLK
    nband = 3 * QBLK
    for blk in range(nblk):
        r0 = blk * QBLK
        qb = q_ref[0, r0:r0 + QBLK, :]
        if local:
            n = pl.program_id(1) * nblk + blk
            start = pl.multiple_of(jnp.clip((n - 1) * QBLK, 0, seq_len - nband), QBLK)
            qpos = n * QBLK + lax.broadcasted_iota(jnp.int32, (rows, nband), 0) % QBLK
            kpos = start + lax.broadcasted_iota(jnp.int32, (rows, nband), 1)
            valid = jnp.abs(qpos - kpos) <= WINDOW
        for g in range(N_KV_HEADS):
            parts = []
            for hh in range(gq):
                h = gq * g + hh
                qc = qb[:, (h // 2) * LANES:(h // 2 + 1) * LANES]
                keep = lo_half if h % 2 == 0 else jnp.logical_not(lo_half)
                parts.append(jnp.where(keep, qc, jnp.zeros_like(qc)))
            q4 = jnp.concatenate(parts, axis=0)
            sk = jnp.full((rows, 1), sink_ref[gq * g + gq - 1], F32)
            for hh in range(gq - 2, -1, -1):
                sk = jnp.where(hrow == hh, sink_ref[gq * g + hh], sk)
            gl = slice(g * LANES, (g + 1) * LANES)
            s_ctx = _dot_nt(q4, kc_ref[0, :, gl])
            m = jnp.maximum(jnp.max(s_ctx, axis=1, keepdims=True), sk)
            if local:
                s_loc = _dot_nt(q4, kd_ref[0, pl.ds(start, nband), gl])
                s_loc = jnp.where(valid, s_loc, NEG)
                m = jnp.maximum(m, jnp.max(s_loc, axis=1, keepdims=True))
            p_ctx = jnp.exp(s_ctx - m)
            den = jnp.sum(p_ctx, axis=1, keepdims=True) + jnp.exp(sk - m)
            o = _dot(p_ctx.astype(BF), vc_ref[0, :, gl])
            if local:
                p_loc = jnp.exp(s_loc - m)
                den = den + jnp.sum(p_loc, axis=1, keepdims=True)
                o = o + _dot(p_loc.astype(BF), vd_ref[0, pl.ds(start, nband), gl])
            o = o / den
            for cc in range(gq // 2):
                col = (gq // 2) * g + cc
                oa = o[(2 * cc) * QBLK:(2 * cc + 1) * QBLK]
                ob = o[(2 * cc + 1) * QBLK:(2 * cc + 2) * QBLK]
                o_ref[0, r0:r0 + QBLK, col * LANES:(col + 1) * LANES] = (
                    jnp.where(lo_half, oa, ob).astype(BF))


def _attention(sink, q, kd, vd, kc, vc, *, local, tq):
    b, lq, _ = q.shape
    lk = kd.shape[1]
    c = kc.shape[1]
    kern = functools.partial(_attn_kernel, local=local, seq_len=lk)
    return pl.pallas_call(
        kern,
        grid=(b, lq // tq),
        in_specs=[
            pl.BlockSpec(memory_space=pltpu.SMEM),
            pl.BlockSpec((1, tq, ATTN_WIDTH), lambda bi, i: (bi, i, 0)),
            pl.BlockSpec((1, lk, 2 * KV_WIDTH), lambda bi, i: (bi, 0, 0)),
            pl.BlockSpec((1, lk, 2 * KV_WIDTH), lambda bi, i: (bi, 0, 0)),
            pl.BlockSpec((1, c, 2 * KV_WIDTH), lambda bi, i: (bi, 0, 0)),
            pl.BlockSpec((1, c, 2 * KV_WIDTH), lambda bi, i: (bi, 0, 0)),
        ],
        out_specs=pl.BlockSpec((1, tq, ATTN_WIDTH), lambda bi, i: (bi, i, 0)),
        out_shape=jax.ShapeDtypeStruct((b, lq, ATTN_WIDTH), BF),
        compiler_params=_cparams("parallel", "parallel"),
        name="attn_local" if local else "attn_ctx",
    )(sink, q, kd, vd, kc, vc)


def _hy_prep_kernel(u_ref, w_ref, b_ref, o_ref):
    u = u_ref[0]
    n = u.shape[0]
    row = lax.broadcasted_iota(jnp.int32, u.shape, 0)
    prv = jnp.where(row == 0, 0.0, pltpu.roll(u, 1, axis=0))
    nxt = jnp.where(row == n - 1, 0.0, pltpu.roll(u, n - 1, axis=0))
    o_ref[0] = prv * w_ref[0:1, :] + u * w_ref[1:2, :] + nxt * w_ref[2:3, :] + b_ref[...]


def _hy_prep(u, conv_w, conv_b):
    b, n, w = u.shape
    return pl.pallas_call(
        _hy_prep_kernel,
        grid=(b, w // LANES),
        in_specs=[
            pl.BlockSpec((1, n, LANES), lambda bi, j: (bi, 0, j)),
            pl.BlockSpec((3, LANES), lambda bi, j: (0, j)),
            pl.BlockSpec((1, LANES), lambda bi, j: (0, j)),
        ],
        out_specs=pl.BlockSpec((1, n, LANES), lambda bi, j: (bi, 0, j)),
        out_shape=jax.ShapeDtypeStruct((b, n, w), F32),
        compiler_params=_cparams("parallel", "parallel"),
        name="hy_prep",
    )(u, conv_w, conv_b)


def _hy_filter_kernel(z_ref, w1_ref, b1_ref, f1_ref, w2_ref, b2_ref, f2_ref, w3_ref, dl_ref,
                      k_ref, nrm_ref, *, n_half):
    i = pl.program_id(0)
    tm = z_ref.shape[0]
    z = z_ref[...]
    h = jnp.sin(f1_ref[...] * (_dot3(z, w1_ref[...]) + b1_ref[...]))
    h = jnp.sin(f2_ref[...] * (_dot3(h, w2_ref[...]) + b2_ref[...]))
    h = _dot3(h, w3_ref[...])
    h = h * jnp.exp(-z[:, 0:1] * dl_ref[...])
    row = i * tm + lax.broadcasted_iota(jnp.int32, (tm, 1), 0)
    h = jnp.where(row == n_half, 0.0, h)
    k_ref[...] = h

    @pl.when(i == 0)
    def _():
        nrm_ref[...] = jnp.zeros_like(nrm_ref)

    nrm_ref[...] += jnp.sum(jnp.abs(h), axis=0, keepdims=True)


def _hy_filter(zfeat2, w1p, b1, f1, w2, b2, f2, w3, deltas2, *, tm):
    n2 = zfeat2.shape[0]
    n_half = n2 // 2
    wout = w3.shape[1] // 2
    full = lambda i: (0, 0)
    kern = functools.partial(_hy_filter_kernel, n_half=n_half)
    return pl.pallas_call(
        kern,
        grid=(n2 // tm,),
        in_specs=[
            pl.BlockSpec((tm, zfeat2.shape[1]), lambda i: (i, 0)),
            pl.BlockSpec(w1p.shape, full), pl.BlockSpec(b1.shape, full), pl.BlockSpec(f1.shape, full),
            pl.BlockSpec(w2.shape, full), pl.BlockSpec(b2.shape, full), pl.BlockSpec(f2.shape, full),
            pl.BlockSpec((w3.shape[0], wout), lambda i: (0, (i * tm) // n_half)),
            pl.BlockSpec(deltas2.shape, full),
        ],
        out_specs=[pl.BlockSpec((tm, wout), lambda i: (i, 0)), pl.BlockSpec((1, wout), full)],
        out_shape=[jax.ShapeDtypeStruct((n2, wout), F32), jax.ShapeDtypeStruct((1, wout), F32)],
        compiler_params=_cparams("arbitrary"),
        name="hy_filter",
    )(zfeat2, w1p, b1, f1, w2, b2, f2, w3, deltas2)


def _hy_gate_kernel(g_ref, z_ref, c_ref, b_ref, o_ref):
    o_ref[...] = (g_ref[...] * (c_ref[...] + b_ref[...] * z_ref[...])).astype(o_ref.dtype)


def _hy_gate(uc2, gate_blk, zprev, zprev_blk, zc2, bias, out_dtype, *, tm):
    t = zc2.shape[0]
    w = HYENA_WIDTH
    return pl.pallas_call(
        _hy_gate_kernel,
        grid=(t // tm,),
        in_specs=[
            pl.BlockSpec((tm, w), lambda i: (i, gate_blk)),
            pl.BlockSpec((tm, w), lambda i: (i, zprev_blk)),
            pl.BlockSpec((tm, w), lambda i: (i, 0)),
            pl.BlockSpec((1, w), lambda i: (0, 0)),
        ],
        out_specs=pl.BlockSpec((tm, w), lambda i: (i, 0)),
        out_shape=jax.ShapeDtypeStruct((t, w), out_dtype),
        compiler_params=_cparams("parallel"),
        name="hy_gate",
    )(uc2, zprev, zc2, bias)


def _dft_mats(k_out, r_in, period, sign, scale, n_in, real_out):
    k = np.arange(k_out)[:, None]
    r = np.arange(r_in)[None, :]
    ang = 2.0 * np.pi * ((k * r) % period) / period
    fr = np.cos(ang) * scale
    fi = sign * np.sin(ang) * scale
    if real_out:
        mats = [fr, -fi]
    else:
        mats = [np.concatenate([fr, fi], 0), np.concatenate([-fi, fr], 0)]
    return jnp.asarray(np.stack(mats[:n_in], 0), dtype=F32).astype(BF)


def _twiddle(s_n, k_n, n, sign):
    s = lax.broadcasted_iota(jnp.int32, (s_n, k_n, LANES), 0)
    k = lax.broadcasted_iota(jnp.int32, (s_n, k_n, LANES), 1)
    ang = (s * k).astype(F32) * (2.0 * math.pi / n)
    return jnp.cos(ang), sign * jnp.sin(ang)


def _stage_kernel(*refs, n_in, r_in, k_out, sbk, tw, spec, real_out, transposed_out, flat):
    it = iter(refs)
    x_refs = [next(it) for _ in range(n_in)]
    g_ref = next(it)
    tw_refs = [next(it), next(it)] if tw else None
    spec_refs = [next(it) for _ in range(3)] if spec else None
    out_refs = [next(it)] if real_out else [next(it), next(it)]
    if not flat:
        x_refs = [r.reshape(r_in * sbk, LANES) for r in x_refs]
        if spec:
            spec_refs = [r.reshape(k_out * sbk, LANES) for r in spec_refs[:2]] + spec_refs[2:]
        if not transposed_out:
            out_refs = [r.reshape(k_out * sbk, LANES) for r in out_refs]
    if spec:
        inv = 1.0 / spec_refs[2][...]
    for j in range(sbk):
        acc = None
        for xi, x_ref in enumerate(x_refs):
            xv = x_ref[...] if flat else x_ref[pl.ds(j, r_in, stride=sbk), :]
            d = _dot(g_ref[xi], xv.astype(BF))
            acc = d if acc is None else acc + d
        if real_out:
            ys = [acc]
        else:
            yr, yi = acc[:k_out], acc[k_out:]
            if tw:
                tr, ti = tw_refs[0][j], tw_refs[1][j]
                yr, yi = yr * tr - yi * ti, yr * ti + yi * tr
            if spec:
                rows = slice(None) if flat else pl.ds(j, k_out, stride=sbk)
                sr = spec_refs[0][rows, :] * inv
                si = spec_refs[1][rows, :] * inv
                yr, yi = yr * sr - yi * si, yr * si + yi * sr
            ys = [yr, yi]
        for o_ref, y in zip(out_refs, ys):
            if flat:
                o_ref[...] = y.astype(o_ref.dtype)
            elif transposed_out:
                o_ref[0, j] = y.astype(o_ref.dtype)
            else:
                o_ref[pl.ds(j, k_out, stride=sbk), :] = y.astype(o_ref.dtype)


def _fft_stage(xs, x_sel, gmat, *, r_in, s_n, k_out, n_groups, n_cblk, transposed_out, real_out,
               out_dtype=F32, tw=None, spec=None, spec_sel=None, sbk=16, name="fft_stage"):
    n_in = len(xs)
    flat = s_n == 1
    sbk = 1 if flat else min(sbk, s_n)
    cb = LANES
    in_specs, args = [], []
    for x, sel in zip(xs, x_sel):
        if flat:
            in_specs.append(pl.BlockSpec((None, r_in, cb), lambda s, g, c, sel=sel: (sel(g, c)[0], 0, sel(g, c)[1])))
            args.append(x)
        else:
            xv = x.reshape(x.shape[0], x.shape[1] // s_n, s_n, x.shape[2])
            in_specs.append(pl.BlockSpec((1, r_in, sbk, cb),
                                         lambda s, g, c, sel=sel: (sel(g, c)[0], 0, s, sel(g, c)[1])))
            args.append(xv)
    in_specs.append(pl.BlockSpec(gmat.shape, lambda s, g, c: (0, 0, 0)))
    args.append(gmat)
    if tw is not None:
        for tarr in tw:
            in_specs.append(pl.BlockSpec((sbk, k_out, LANES), lambda s, g, c: (s, 0, 0)))
            args.append(tarr)
    if spec is not None:
        kr, ki, nrm = spec
        for arr in (kr, ki):
            if flat:
                in_specs.append(pl.BlockSpec((k_out, cb), lambda s, g, c: (0, spec_sel(g, c))))
                args.append(arr)
            else:
                in_specs.append(pl.BlockSpec((1, k_out, sbk, cb), lambda s, g, c: (0, 0, s, spec_sel(g, c))))
                args.append(arr.reshape(1, k_out, s_n, arr.shape[-1]))
        in_specs.append(pl.BlockSpec((1, cb), lambda s, g, c: (0, spec_sel(g, c))))
        args.append(nrm)
    ctot = n_cblk * cb
    if flat:
        oshape = (n_groups, k_out, ctot)
        ospec = pl.BlockSpec((None, k_out, cb), lambda s, g, c: (g, 0, c))
    elif transposed_out:
        oshape = (n_groups, s_n, k_out, ctot)
        ospec = pl.BlockSpec((1, sbk, k_out, cb), lambda s, g, c: (g, s, 0, c))
    else:
        oshape = (n_groups, k_out, s_n, ctot)
        ospec = pl.BlockSpec((1, k_out, sbk, cb), lambda s, g, c: (g, 0, s, c))
    n_out = 1 if real_out else 2
    kern = functools.partial(_stage_kernel, n_in=n_in, r_in=r_in, k_out=k_out, sbk=sbk, tw=tw is not None,
                             spec=spec is not None, real_out=real_out, transposed_out=transposed_out,
                             flat=flat)
    outs = pl.pallas_call(
        kern,
        grid=(s_n // sbk, n_groups, n_cblk),
        in_specs=in_specs,
        out_specs=[ospec] * n_out,
        out_shape=[jax.ShapeDtypeStruct(oshape, out_dtype)] * n_out,
        compiler_params=_cparams("parallel", "parallel", "parallel"),
        name=name,
    )(*args)
    return [o.reshape(n_groups, -1, ctot) for o in outs]


def _split_len(n):
    if n <= 1024:
        return n, 1
    s = 128
    return n // s, s


def _fft_forward(xs, x_sel, n, n_rows, *, n_groups, n_cblk, spec=None, spec_sel=None, name="fwd"):
    n1, s = _split_len(n)
    n_in = len(xs)
    if s == 1:
        g = _dft_mats(n, n_rows, n, -1.0, 1.0, n_in, False)
        return _fft_stage(xs, x_sel, g, r_in=n_rows, s_n=1, k_out=n, n_groups=n_groups, n_cblk=n_cblk,
                          transposed_out=False, real_out=False, spec=spec, spec_sel=spec_sel,
                          name=name + "_direct")
    r1 = n_rows // s
    g1 = _dft_mats(n1, r1, n1, -1.0, 1.0, n_in, False)
    tw = _twiddle(s, n1, n, -1.0)
    ar, ai = _fft_stage(xs, x_sel, g1, r_in=r1, s_n=s, k_out=n1, n_groups=n_groups, n_cblk=n_cblk,
                        transposed_out=True, real_out=False, tw=tw, name=name + "_s1")
    g2 = _dft_mats(s, s, s, -1.0, 1.0, 2, False)
    ident = lambda g, c: (g, c)
    return _fft_stage([ar, ai], [ident, ident], g2, r_in=s, s_n=n1, k_out=s, n_groups=n_groups,
                      n_cblk=n_cblk, transposed_out=False, real_out=False, spec=spec, spec_sel=spec_sel,
                      name=name + "_s2")


def _fft_inverse(pr, pi, n, n_keep, *, n_groups, n_cblk, name="inv"):
    ident = lambda g, c: (g, c)
    n1, s = _split_len(n)
    if s == 1:
        g = _dft_mats(n_keep, n, n, 1.0, 1.0 / n, 2, False)
        return _fft_stage([pr, pi], [ident, ident], g, r_in=n, s_n=1, k_out=n_keep, n_groups=n_groups,
                          n_cblk=n_cblk, transposed_out=False, real_out=False, name=name + "_direct")
    g1 = _dft_mats(s, s, s, 1.0, 1.0, 2, False)
    tw = _twiddle(n1, s, n, 1.0)
    qr, qi = _fft_stage([pr, pi], [ident, ident], g1, r_in=s, s_n=n1, k_out=s, n_groups=n_groups,
                        n_cblk=n_cblk, transposed_out=True, real_out=False, tw=tw, name=name + "_s1")
    k2 = n_keep // s
    g2 = _dft_mats(k2, n1, n1, 1.0, 1.0 / n, 2, False)
    return _fft_stage([qr, qi], [ident, ident], g2, r_in=n1, s_n=s, k_out=k2, n_groups=n_groups,
                      n_cblk=n_cblk, transposed_out=False, real_out=False, name=name + "_s2")


def _hyena(uc, k_filt, nrm, hy_bias, out_rows):
    b, n, _ = uc.shape
    w = HYENA_WIDTH
    wblk = w // LANES
    nfft = 2 * n
    kr, ki = _fft_forward([k_filt[None]], [lambda g, c: (0, c)], nfft, nfft, n_groups=1,
                          n_cblk=HYENA_ORDER * wblk, name="hy_filt_fft")
    uc2 = uc.reshape(b * n, 3 * w)
    z, zblk = uc, 2 * wblk
    z2, z2blk = uc2, 2
    for o in range(HYENA_ORDER):
        sel_r = lambda g, c, zblk=zblk: (0, zblk + c)
        sel_i = lambda g, c, zblk=zblk: (1, zblk + c)
        spec_sel = lambda g, c, o=o: o * wblk + c
        pr, pi = _fft_forward([z, z], [sel_r, sel_i], nfft, n, n_groups=1, n_cblk=wblk,
                              spec=(kr[0], ki[0], nrm), spec_sel=spec_sel, name="hy_fwd")
        cr, ci = _fft_inverse(pr, pi, nfft, n, n_groups=1, n_cblk=wblk, name="hy_inv")
        zc2 = jnp.concatenate([cr, ci], axis=0).reshape(b * n, w)
        last = o == HYENA_ORDER - 1
        znext = _hy_gate(uc2, o, z2, z2blk, zc2, hy_bias[o:o + 1], BF if last else F32, tm=out_rows)
        z, zblk, z2, z2blk = znext.reshape(b, n, w), 0, znext, 0
    return z2


def _fnet(pq):
    b, n, _ = pq.shape
    w = FNET_WIDTH
    wblk = w // LANES
    scale = 1.0 / math.sqrt(n * FNET_GROUP_DIM)
    sel_r = lambda g, c: (g, c)
    sel_i = lambda g, c: (g, wblk + c)
    ident = lambda g, c: (g, c)
    n1, s = _split_len(n)
    if s == 1:
        g = _dft_mats(n, n, n, -1.0, scale, 2, True)
        (y,) = _fft_stage([pq, pq], [sel_r, sel_i], g, r_in=n, s_n=1, k_out=n, n_groups=b, n_cblk=wblk,
                          transposed_out=False, real_out=True, name="fnet_direct")
        return y.reshape(b * n, w)
    g1 = _dft_mats(n1, n1, n1, -1.0, 1.0, 2, False)
    tw = _twiddle(s, n1, n, -1.0)
    ar, ai = _fft_stage([pq, pq], [sel_r, sel_i], g1, r_in=n1, s_n=s, k_out=n1, n_groups=b, n_cblk=wblk,
                        transposed_out=True, real_out=False, tw=tw, name="fnet_s1")
    g2 = _dft_mats(s, s, s, -1.0, scale, 2, True)
    (y,) = _fft_stage([ar, ai], [ident, ident], g2, r_in=s, s_n=n1, k_out=s, n_groups=b, n_cblk=wblk,
                      transposed_out=False, real_out=True, name="fnet_s2")
    return y.reshape(b * n, w)


def _merge_kernel(x_ref, gt_ref, a_ref, h_ref, f_ref, gate_ref, wa_ref, wh_ref, wf_ref, wo_ref, o_ref):
    d = D_MODEL
    m = gate_ref[:, 0:d].astype(F32) * _dot(a_ref[...], wa_ref[...])
    m = m + gate_ref[:, d:2 * d].astype(F32) * _dot(h_ref[...], wh_ref[...])
    m = m + gate_ref[:, 2 * d:3 * d].astype(F32) * _dot(f_ref[...].astype(BF), wf_ref[...])
    y = _dot(m.astype(BF), wo_ref[...])
    o_ref[...] = x_ref[...] + gt_ref[0] * y


def _merge(x2, gt, attn_o, hy_o, fn_o, gates, wa, wh, wf, wo, *, tm, mod_row):
    t, d = x2.shape
    tok = lambda i: (i, 0)
    full = lambda i: (0, 0)
    return pl.pallas_call(
        _merge_kernel,
        grid=(t // tm,),
        in_specs=[
            pl.BlockSpec((tm, d), tok),
            pl.BlockSpec((1, 1, d), lambda i: (mod_row(i), 0, 0)),
            pl.BlockSpec((tm, ATTN_WIDTH), tok),
            pl.BlockSpec((tm, HYENA_WIDTH), tok),
            pl.BlockSpec((tm, FNET_WIDTH), tok),
            pl.BlockSpec((tm, 3 * d), tok),
            pl.BlockSpec(wa.shape, full), pl.BlockSpec(wh.shape, full),
            pl.BlockSpec(wf.shape, full), pl.BlockSpec(wo.shape, full),
        ],
        out_specs=pl.BlockSpec((tm, d), tok),
        out_shape=jax.ShapeDtypeStruct((t, d), F32),
        compiler_params=_cparams("parallel"),
        name="merge",
    )(x2, gt, attn_o, hy_o, fn_o, gates, wa, wh, wf, wo)


def _ffn_kernel(x_ref, sh_ref, sc_ref, gt_ref, g_ref, wg_ref, wu_ref, wd_ref, o_ref, h_scr, acc_scr):
    f = pl.program_id(1)

    @pl.when(f == 0)
    def _():
        h_scr[...] = _modulated_norm(x_ref[...], g_ref[...], sc_ref[0], sh_ref[0]).astype(BF)
        acc_scr[...] = jnp.zeros_like(acc_scr)

    hb = h_scr[...]
    act = _silu(_dot(hb, wg_ref[...])) * _dot(hb, wu_ref[...])
    acc_scr[...] += _dot(act.astype(BF), wd_ref[...])

    @pl.when(f == pl.num_programs(1) - 1)
    def _():
        o_ref[...] = x_ref[...] + gt_ref[0] * acc_scr[...]


def _ffn(x2, sh, sc, gt, g, wg, wu, wd, *, tm, tf, mod_row):
    t, d = x2.shape
    ff = wg.shape[1]
    row3 = lambda i, f: (mod_row(i), 0, 0)
    return pl.pallas_call(
        _ffn_kernel,
        grid=(t // tm, ff // tf),
        in_specs=[
            pl.BlockSpec((tm, d), lambda i, f: (i, 0)),
            pl.BlockSpec((1, 1, d), row3), pl.BlockSpec((1, 1, d), row3), pl.BlockSpec((1, 1, d), row3),
            pl.BlockSpec((1, d), lambda i, f: (0, 0)),
            pl.BlockSpec((d, tf), lambda i, f: (0, f)),
            pl.BlockSpec((d, tf), lambda i, f: (0, f)),
            pl.BlockSpec((tf, d), lambda i, f: (f, 0)),
        ],
        out_specs=pl.BlockSpec((tm, d), lambda i, f: (i, 0)),
        out_shape=jax.ShapeDtypeStruct((t, d), F32),
        scratch_shapes=[pltpu.VMEM((tm, d), BF), pltpu.VMEM((tm, d), F32)],
        compiler_params=_cparams("parallel", "arbitrary"),
        name="ffn_dense",
    )(x2, sh, sc, gt, g, wg, wu, wd)


def _top2(logits):
    lane = lax.broadcasted_iota(jnp.int32, logits.shape, 1)
    lg = jnp.where(lane < N_EXPERTS, logits, -jnp.inf)
    m1 = jnp.max(lg, axis=1, keepdims=True)
    i1 = jnp.min(jnp.where(lg == m1, lane, LANES), axis=1, keepdims=True)
    lg2 = jnp.where(lane == i1, -jnp.inf, lg)
    m2 = jnp.max(lg2, axis=1, keepdims=True)
    i2 = jnp.min(jnp.where(lg2 == m2, lane, LANES), axis=1, keepdims=True)
    e2 = jnp.exp(m2 - m1)
    w1 = 1.0 / (1.0 + e2)
    return i1, i2, w1, e2 * w1


GROUP_TILE = 256
SLOT_RADIX = 64.0


def _moe_group_kernel(x_ref, sh_ref, sc_ref, g_ref, wr_ref, xg_ref, ws_ref, slot_ref, cnt_ref,
                      h_scr, rows_scr, wm_scr):
    j = pl.program_id(1)
    tb = x_ref.shape[0]
    gt_rows = xg_ref.shape[1]

    @pl.when(j == 0)
    def _():
        h = _modulated_norm(x_ref[...], g_ref[...], sc_ref[0], sh_ref[0])
        h_scr[...] = h.astype(BF)
        i1, i2, w1, w2 = _top2(_dot3(h, wr_ref[...]))
        lane = lax.broadcasted_iota(jnp.int32, (tb, LANES), 1)
        oh0 = jnp.where(lane == i1, 1.0, 0.0)
        oh1 = jnp.where(lane == i2, 1.0, 0.0)
        c0 = jnp.sum(oh0, axis=0, keepdims=True)
        cnt = c0 + jnp.sum(oh1, axis=0, keepdims=True)
        tri = jnp.where(lax.broadcasted_iota(jnp.int32, (tb, tb), 1)
                        < lax.broadcasted_iota(jnp.int32, (tb, tb), 0), 1.0, 0.0).astype(BF)
        pre0 = _dot(tri, oh0.astype(BF))
        pre1 = _dot(tri, oh1.astype(BF)) + c0
        tiles = jnp.ceil(cnt * (1.0 / gt_rows))
        upper = jnp.where(lax.broadcasted_iota(jnp.int32, (LANES, LANES), 0)
                          < lax.broadcasted_iota(jnp.int32, (LANES, LANES), 1), 1.0, 0.0).astype(BF)
        off = _dot(jnp.broadcast_to(tiles, (8, LANES)).astype(BF), upper)[0:1] * float(gt_rows)
        slot0 = jnp.sum(oh0 * (off + pre0), axis=1, keepdims=True)
        slot1 = jnp.sum(oh1 * (off + pre1), axis=1, keepdims=True)
        slot_ref[0] = jnp.where(lane == 0, slot0, jnp.where(lane == 1, slot1, 0.0))
        cnt_ref[0] = jnp.broadcast_to(cnt, (8, LANES))
        hi0 = jnp.floor(slot0 * (1.0 / SLOT_RADIX))
        hi1 = jnp.floor(slot1 * (1.0 / SLOT_RADIX))
        digits = jnp.where(lane == 0, hi0, jnp.where(lane == 1, slot0 - SLOT_RADIX * hi0,
                           jnp.where(lane == 2, hi1, jnp.where(lane == 3, slot1 - SLOT_RADIX * hi1, 0.0))))
        sel = jnp.where(lax.broadcasted_iota(jnp.int32, (8, LANES), 0)
                        == lax.broadcasted_iota(jnp.int32, (8, LANES), 1), 1.0, 0.0).astype(BF)
        rows_scr[...] = _dot_nt(sel, digits.astype(BF))
        w1h, w1l = _split(w1)
        w1m, w1l = _split(w1 - w1h.astype(F32))
        w2h, w2l = _split(w2)
        w2m, w2l = _split(w2 - w2h.astype(F32))
        cols = [w1h, w1m, w1l, w2h, w2m, w2l]
        wm = jnp.zeros((tb, LANES), F32)
        for li, col in enumerate(cols):
            wm = jnp.where(lane == li, col.astype(F32), wm)
        wm_scr[...] = wm.astype(BF)

    rows = rows_scr[...]
    s0 = rows[0:1] * SLOT_RADIX + rows[1:2]
    s1 = rows[2:3] * SLOT_RADIX + rows[3:4]
    pos = (lax.broadcasted_iota(jnp.int32, (gt_rows, tb), 0) + j * gt_rows).astype(F32)
    g0 = jnp.where(pos == s0, 1.0, 0.0).astype(BF)
    g1 = jnp.where(pos == s1, 1.0, 0.0).astype(BF)
    xg_ref[0] = _dot(g0 + g1, h_scr[...]).astype(BF)
    lane_w = lax.broadcasted_iota(jnp.int32, (gt_rows, LANES), 1)
    wsum = (jnp.where(lane_w < 3, _dot(g0, wm_scr[...]), 0.0)
            + jnp.where((lane_w >= 3) & (lane_w < 6), _dot(g1, wm_scr[...]), 0.0))
    ws_ref[0] = jnp.broadcast_to(jnp.sum(wsum, axis=1, keepdims=True), (gt_rows, LANES))


def _moe_group(x2, sh, sc, g, wr_pad, *, tb, nt, mod_row):
    t, d = x2.shape
    nb = t // tb
    row3 = lambda b, j: (mod_row(b), 0, 0)
    return pl.pallas_call(
        _moe_group_kernel,
        grid=(nb, nt),
        in_specs=[
            pl.BlockSpec((tb, d), lambda b, j: (b, 0)),
            pl.BlockSpec((1, 1, d), row3), pl.BlockSpec((1, 1, d), row3),
            pl.BlockSpec((1, d), lambda b, j: (0, 0)),
            pl.BlockSpec((d, LANES), lambda b, j: (0, 0)),
        ],
        out_specs=[
            pl.BlockSpec((1, GROUP_TILE, d), lambda b, j: (b * nt + j, 0, 0)),
            pl.BlockSpec((1, GROUP_TILE, LANES), lambda b, j: (b * nt + j, 0, 0)),
            pl.BlockSpec((1, tb, LANES), lambda b, j: (b, 0, 0)),
            pl.BlockSpec((1, 8, LANES), lambda b, j: (b, 0, 0)),
        ],
        out_shape=[
            jax.ShapeDtypeStruct((nb * nt, GROUP_TILE, d), BF),
            jax.ShapeDtypeStruct((nb * nt, GROUP_TILE, LANES), F32),
            jax.ShapeDtypeStruct((nb, tb, LANES), F32),
            jax.ShapeDtypeStruct((nb, 8, LANES), F32),
        ],
        scratch_shapes=[pltpu.VMEM((tb, d), BF), pltpu.VMEM((8, tb), F32), pltpu.VMEM((tb, LANES), BF)],
        compiler_params=_cparams("parallel", "arbitrary"),
        name="moe_group",
    )(x2, sh, sc, g, wr_pad)


def _moe_schedule(cnt, nt):
    tiles = (cnt + GROUP_TILE - 1) // GROUP_TILE
    nb, ne = tiles.shape
    tot_e = tiles.sum(0)
    cum_e = jnp.cumsum(tot_e)
    n_used = cum_e[-1]
    pos = jnp.minimum(jnp.arange(nb * nt, dtype=jnp.int32), n_used - 1)
    e = jnp.sum(pos[:, None] >= cum_e[None, :], axis=1).astype(jnp.int32)
    r = pos - (cum_e - tot_e)[e]
    cum_b = jnp.cumsum(tiles, axis=0)
    cum_b_e = cum_b.T[e]
    blk = jnp.sum(r[:, None] >= cum_b_e, axis=1).astype(jnp.int32)
    first_in_blk = (jnp.cumsum(tiles, axis=1) - tiles)[blk, e]
    j = first_in_blk + r - (cum_b[blk, e] - tiles[blk, e])
    return ((blk * nt + j).astype(jnp.int32), e, n_used.astype(jnp.int32).reshape(1),
            tiles.sum(1).astype(jnp.int32))


def _moe_expert_kernel(tile_ref, exp_ref, nused_ref, xg_ref, ws_ref, wg_ref, wu_ref, wd_ref, y_ref):
    @pl.when(pl.program_id(0) < nused_ref[0])
    def _():
        x = xg_ref[0]
        act = _silu(_dot(x, wg_ref[0])) * _dot(x, wu_ref[0])
        y = _dot(act.astype(BF), wd_ref[0])
        w = ws_ref[0]
        y_ref[0] = (y * jnp.concatenate([w] * (y.shape[1] // LANES), axis=1)).astype(BF)


def _moe_experts(sched_tile, sched_exp, n_used, xg, ws, wg, wu, wd):
    n_sched, gt_rows, d = xg.shape
    ff = wg.shape[2]
    tile3 = lambda i, st, se, nu: (st[i], 0, 0)
    exp3 = lambda i, st, se, nu: (se[i], 0, 0)
    return pl.pallas_call(
        _moe_expert_kernel,
        grid_spec=pltpu.PrefetchScalarGridSpec(
            num_scalar_prefetch=3,
            grid=(n_sched,),
            in_specs=[
                pl.BlockSpec((1, gt_rows, d), tile3),
                pl.BlockSpec((1, gt_rows, LANES), tile3),
                pl.BlockSpec((1, d, ff), exp3),
                pl.BlockSpec((1, d, ff), exp3),
                pl.BlockSpec((1, ff, d), exp3),
            ],
            out_specs=pl.BlockSpec((1, gt_rows, d), tile3),
        ),
        out_shape=jax.ShapeDtypeStruct((n_sched, gt_rows, d), BF),
        compiler_params=_cparams("arbitrary"),
        name="moe_experts",
    )(sched_tile, sched_exp, n_used, xg, ws, wg, wu, wd)


def _moe_combine_kernel(nt_ref, x_ref, gt_ref, slot_ref, y_ref, o_ref, acc_scr):
    b = pl.program_id(0)
    j = pl.program_id(1)
    tb = x_ref.shape[0]
    gt_rows = y_ref.shape[1]

    @pl.when(j == 0)
    def _():
        acc_scr[...] = jnp.zeros_like(acc_scr)

    @pl.when(j < nt_ref[b])
    def _():
        sl = slot_ref[0]
        pos = (lax.broadcasted_iota(jnp.int32, (tb, gt_rows), 1) + j * gt_rows).astype(F32)
        p = jnp.where((pos == sl[:, 0:1]) | (pos == sl[:, 1:2]), 1.0, 0.0).astype(BF)
        acc_scr[...] += _dot(p, y_ref[0])

    @pl.when(j == pl.num_programs(1) - 1)
    def _():
        o_ref[...] = x_ref[...] + gt_ref[0] * acc_scr[...]


def _moe_combine(ntiles_b, x2, gt, slots, yg, *, tb, nt, mod_row):
    t, d = x2.shape
    nb = t // tb
    return pl.pallas_call(
        _moe_combine_kernel,
        grid_spec=pltpu.PrefetchScalarGridSpec(
            num_scalar_prefetch=1,
            grid=(nb, nt),
            in_specs=[
                pl.BlockSpec((tb, d), lambda b, j, n: (b, 0)),
                pl.BlockSpec((1, 1, d), lambda b, j, n: (mod_row(b), 0, 0)),
                pl.BlockSpec((1, tb, LANES), lambda b, j, n: (b, 0, 0)),
                pl.BlockSpec((1, GROUP_TILE, d), lambda b, j, n: (b * nt + jnp.minimum(j, n[b] - 1), 0, 0)),
            ],
            out_specs=pl.BlockSpec((tb, d), lambda b, j, n: (b, 0)),
            scratch_shapes=[pltpu.VMEM((tb, d), F32)],
        ),
        out_shape=jax.ShapeDtypeStruct((t, d), F32),
        compiler_params=_cparams("parallel", "arbitrary"),
        name="moe_combine",
    )(ntiles_b, x2, gt, slots, yg)


def _moe(x2, sh, sc, gt, g, wr_pad, wg, wu, wd, *, tb, mod_row):
    nt = 2 * tb // GROUP_TILE + N_EXPERTS - 1
    xg, ws, slots, cnt = _moe_group(x2, sh, sc, g, wr_pad, tb=tb, nt=nt, mod_row=mod_row)
    counts = cnt[:, 0, :N_EXPERTS].astype(jnp.int32)
    sched_tile, sched_exp, n_used, ntiles_b = _moe_schedule(counts, nt)
    yg = _moe_experts(sched_tile, sched_exp, n_used, xg, ws, wg, wu, wd)
    return _moe_combine(ntiles_b, x2, gt, slots, yg, tb=tb, nt=nt, mod_row=mod_row)


def _rope_tables(seq_len):
    pos = np.arange(seq_len)
    prow = (pos // GRID_W).astype(np.float32)
    pcol = (pos % GRID_W).astype(np.float32)
    n_freq = HEAD_DIM // 4
    inv = (np.float32(ROPE_THETA) ** (-np.arange(n_freq, dtype=np.float32) / n_freq)).astype(np.float32)
    ar = jnp.asarray(prow)[:, None] * jnp.asarray(inv)[None, :]
    ac = jnp.asarray(pcol)[:, None] * jnp.asarray(inv)[None, :]
    cos = jnp.concatenate([jnp.cos(ar)] * 2 + [jnp.cos(ac)] * 2, axis=1)
    sin = jnp.concatenate([-jnp.sin(ar), jnp.sin(ar), -jnp.sin(ac), jnp.sin(ac)], axis=1)
    return jnp.concatenate([cos, cos], axis=1), jnp.concatenate([sin, sin], axis=1)


def _head_sum_matrix():
    c = np.arange(ATTN_WIDTH)
    return jnp.asarray((c[:, None] // HEAD_DIM) == (c[None, :] // HEAD_DIM), dtype=F32).astype(BF)


def _fnet_channel_matrix():
    c = np.arange(FNET_WIDTH)
    same = (c[:, None] // FNET_GROUP_DIM) == (c[None, :] // FNET_GROUP_DIM)
    ang = 2.0 * np.pi * (((c[:, None] % FNET_GROUP_DIM) * (c[None, :] % FNET_GROUP_DIM)) % FNET_GROUP_DIM) / FNET_GROUP_DIM
    cb = np.where(same, np.cos(ang), 0.0)
    sb = np.where(same, np.sin(ang), 0.0)
    return jnp.asarray(np.concatenate([cb, -sb], axis=1), dtype=F32).astype(BF)


def _filter_features(n):
    t = jnp.linspace(0.0, 1.0, n, dtype=F32)[:, None]
    w = 2.0 * math.pi * jnp.arange(n, dtype=F32)[:, None] / n
    fb = jnp.linspace(1e-4, FILTER_BANDS - 1, FILTER_BANDS, dtype=F32)
    z = jnp.concatenate([t, jnp.cos(fb * w), -jnp.sin(fb * w)], axis=-1)
    z = jnp.pad(z, ((0, 0), (0, 64 - z.shape[1])))
    return jnp.concatenate([z, jnp.zeros((1, z.shape[1]), F32), z[:0:-1]], axis=0)


def _decay_rates():
    d = jnp.abs(jnp.linspace(math.log(DECAY_TARGET) / SLOW_DECAY_PCT, math.log(DECAY_TARGET) / FAST_DECAY_PCT,
                             HYENA_WIDTH, dtype=F32))
    return jnp.concatenate([d] * HYENA_ORDER)[None, :]


def kernel(x, c, ctx, c_ctx, w_ada, b_ada, norm1_g, norm2_g, w_in, q_norm_g, k_norm_g, attn_sink,
           hy_conv_w, hy_conv_b, hy_filt_w1, hy_filt_b1, hy_filt_freq1, hy_filt_w2, hy_filt_b2,
           hy_filt_freq2, hy_filt_w3, hy_bias, w_proj_attn, w_proj_hyena, w_proj_fnet, w_out,
           ffn_w_gate, ffn_w_up, ffn_w_down, moe_router, moe_w_gate, moe_w_up, moe_w_down):
    b, seq, d = x.shape
    n_ctx = ctx.shape[1]
    depth = w_ada.shape[0]
    tm = 256
    tiles_per_seq = seq // tm

    cond8 = jnp.concatenate([c, c_ctx[None, :], jnp.zeros((8 - b - 1, d), F32)], axis=0)
    mods = _adaln(cond8, w_ada, b_ada)

    cos_l, sin_l = _rope_tables(seq)
    cos_c = jnp.ones((n_ctx, LANES), F32)
    sin_c = jnp.zeros((n_ctx, LANES), F32)
    gsum = _head_sum_matrix()
    mfn = _fnet_channel_matrix()
    deltas = _decay_rates()
    zfeat_l = _filter_features(seq)
    zfeat_c = _filter_features(n_ctx)

    lat_row = lambda i: i // tiles_per_seq
    ctx_row = lambda i: b
    lat_tab = lambda i: i % tiles_per_seq
    ctx_tab = lambda i: 0
    tm_c = min(tm, n_ctx)

    xs = x.reshape(b * seq, d)
    cs = ctx.reshape(b * n_ctx, d)
    for l in range(depth):
        last = l == depth - 1
        mod = lambda j: mods[l, :, j * d:(j + 1) * d].reshape(8, 1, d)
        w_in_bf = w_in[l].astype(BF)
        qg = jnp.tile(q_norm_g[l], N_HEADS)[None, :]
        kg = jnp.tile(k_norm_g[l], N_KV_HEADS)[None, :]
        g1 = norm1_g[l][None, :]
        wa, wh, wf, wo = (w_proj_attn[l].astype(BF), w_proj_hyena[l].astype(BF),
                          w_proj_fnet[l].astype(BF), w_out[l].astype(BF))
        conv_w = hy_conv_w[l].reshape(3, -1)
        conv_b = hy_conv_b[l][None, :]
        w1p = jnp.pad(hy_filt_w1[l], ((0, 64 - hy_filt_w1.shape[1]), (0, 0)))
        filt = (w1p, hy_filt_b1[l][None, :], hy_filt_freq1[l][None, :], hy_filt_w2[l],
                hy_filt_b2[l][None, :], hy_filt_freq2[l][None, :], hy_filt_w3[l], deltas)

        q_c, kd_c, vd_c, u_c, pq_c, gates_c = _phase_a(
            cs, mod(0), mod(1), g1, w_in_bf, cos_c, sin_c, qg, kg, gsum, mfn,
            tm=tm_c, mod_row=ctx_row, tab_row=ctx_tab)
        kd_c3 = kd_c.reshape(b, n_ctx, -1)
        vd_c3 = vd_c.reshape(b, n_ctx, -1)

        q_l, kd_l, vd_l, u_l, pq_l, gates_l = _phase_a(
            xs, mod(0), mod(1), g1, w_in_bf, cos_l, sin_l, qg, kg, gsum, mfn,
            tm=tm, mod_row=lat_row, tab_row=lat_tab)
        attn_l = _attention(attn_sink[l], q_l.reshape(b, seq, -1), kd_l.reshape(b, seq, -1),
                            vd_l.reshape(b, seq, -1), kd_c3, vd_c3, local=True, tq=512)
        h_l, nrm_l = _hy_filter(zfeat_l, *filt, tm=512)
        uc_l = _hy_prep(u_l.reshape(b, seq, -1), conv_w, conv_b)
        hy_l = _hyena(uc_l, h_l, nrm_l, hy_bias[l], out_rows=512)
        fn_l = _fnet(pq_l.reshape(b, seq, -1))
        xs = _merge(xs, mod(2), attn_l.reshape(b * seq, -1), hy_l, fn_l, gates_l, wa, wh, wf, wo,
                    tm=tm, mod_row=lat_row)

        if not last:
            attn_c = _attention(attn_sink[l], q_c.reshape(b, n_ctx, -1), kd_c3, vd_c3, kd_c3, vd_c3,
                                local=False, tq=n_ctx)
            h_c, nrm_c = _hy_filter(zfeat_c, *filt, tm=n_ctx)
            uc_c = _hy_prep(u_c.reshape(b, n_ctx, -1), conv_w, conv_b)
            hy_c = _hyena(uc_c, h_c, nrm_c, hy_bias[l], out_rows=n_ctx)
            fn_c = _fnet(pq_c.reshape(b, n_ctx, -1))
            cs = _merge(cs, mod(2), attn_c.reshape(b * n_ctx, -1), hy_c, fn_c, gates_c, wa, wh, wf, wo,
                        tm=tm_c, mod_row=ctx_row)

        g2 = norm2_g[l][None, :]
        i = l // 2
        if l % 2 == 0:
            wg, wu, wd = ffn_w_gate[i].astype(BF), ffn_w_up[i].astype(BF), ffn_w_down[i].astype(BF)
            run = lambda t2, rows, tmm: _ffn(t2, mod(3), mod(4), mod(5), g2, wg, wu, wd,
                                             tm=tmm, tf=D_FF // 2, mod_row=rows)
        else:
            wr = jnp.pad(moe_router[i], ((0, 0), (0, LANES - N_EXPERTS)))
            wg, wu, wd = moe_w_gate[i].astype(BF), moe_w_up[i].astype(BF), moe_w_down[i].astype(BF)
            run = lambda t2, rows, tmm: _moe(t2, mod(3), mod(4), mod(5), g2, wr, wg, wu, wd,
                                             tb=tmm, mod_row=rows)
        tm_ffn = 512 if l % 2 == 0 else 1024
        xs = run(xs, lambda t: t // (seq // tm_ffn), tm_ffn)
        if not last:
            cs = run(cs, ctx_row, min(tm_ffn, b * n_ctx))
    return xs.reshape(b, seq, d)
```

```python
import functools
import math

import numpy as np
import jax
import jax.numpy as jnp
from jax import lax
from jax.experimental import pallas as pl
from jax.experimental.pallas import tpu as pltpu

F32 = jnp.float32
BF = jnp.bfloat16

D_MODEL = 1024
DEPTH = 4
GRID_W = 64
HEAD_DIM = 64
N_HEADS = 8
N_KV_HEADS = 2
ATTN_WIDTH = N_HEADS * HEAD_DIM
KV_WIDTH = N_KV_HEADS * HEAD_DIM
WINDOW = 128
QBLK = 128
ROPE_THETA = 10000.0
HYENA_ORDER = 2
HYENA_WIDTH = 256
FILTER_BANDS = 16
FILTER_HIDDEN = 64
DECAY_TARGET = 1e-2
FAST_DECAY_PCT = 0.3
SLOW_DECAY_PCT = 1.5
FNET_WIDTH = 256
FNET_GROUP_DIM = 64
Q_END = ATTN_WIDTH
K_END = Q_END + KV_WIDTH
V_END = K_END + KV_WIDTH
HY_END = V_END + (HYENA_ORDER + 1) * HYENA_WIDTH
FN_END = HY_END + FNET_WIDTH
IN_WIDTH = FN_END + 3 * D_MODEL
D_FF = 2816
N_EXPERTS = 8
EPS = 1e-6
LANES = 128
NEG = -1e30
STAGE_ROWS = 16

VMEM_LIMIT = 56 * 1024 * 1024


def _cparams(*sem):
    return pltpu.CompilerParams(dimension_semantics=sem, vmem_limit_bytes=VMEM_LIMIT)


def _dot(a, b):
    return jnp.dot(a, b, preferred_element_type=F32)


def _dot_nt(a, b):
    return lax.dot_general(a, b, (((1,), (1,)), ((), ())), preferred_element_type=F32)


def _split(a):
    hi = a.astype(BF)
    lo = (a - hi.astype(F32)).astype(BF)
    return hi, lo


def _dot3(a, b):
    ah, al = _split(a)
    bh, bl = _split(b)
    return _dot(ah, bh) + (_dot(ah, bl) + _dot(al, bh))


def _dot2(a, b_bf16):
    ah, al = _split(a)
    return _dot(ah, b_bf16) + _dot(al, b_bf16)


def _silu(v):
    return v * jax.nn.sigmoid(v)


def _adaln_kernel(c_ref, w_ref, b_ref, o_ref):
    o_ref[0] = _dot3(_silu(c_ref[...]), w_ref[0]) + b_ref[0]


def _adaln(cond8, w_ada, b_ada):
    depth, d, n6 = w_ada.shape
    tn = 1024
    return pl.pallas_call(
        _adaln_kernel,
        grid=(depth, n6 // tn),
        in_specs=[
            pl.BlockSpec((8, d), lambda l, j: (0, 0)),
            pl.BlockSpec((1, d, tn), lambda l, j: (l, 0, j)),
            pl.BlockSpec((1, 1, tn), lambda l, j: (l, 0, j)),
        ],
        out_specs=pl.BlockSpec((1, 8, tn), lambda l, j: (l, 0, j)),
        out_shape=jax.ShapeDtypeStruct((depth, 8, n6), F32),
        compiler_params=_cparams("parallel", "parallel"),
        name="adaln",
    )(cond8, w_ada, b_ada.reshape(depth, 1, n6))


def _modulated_norm(x, g, sc, sh):
    ms = jnp.mean(x * x, axis=-1, keepdims=True)
    h = (x * lax.rsqrt(ms + EPS)) * g
    return h * (1.0 + sc) + sh


def _phase_a_kernel(x_ref, sh_ref, sc_ref, g_ref, w_ref, cos_ref, sin_ref, qg_ref, kg_ref,
                    gsum_ref, mfn_ref, q_ref, kd_ref, vd_ref, u_ref, pq_ref, gate_ref):
    tm = x_ref.shape[0]
    hb = _modulated_norm(x_ref[...], g_ref[...], sc_ref[0], sh_ref[0]).astype(BF)
    cos = cos_ref[...]
    sin = sin_ref[...]

    def headnorm(t, gain, gs):
        ss = _dot2(t * t, gs)
        return t * lax.rsqrt(ss * (1.0 / HEAD_DIM) + EPS) * gain

    def rope(t, cosw, sinw):
        w = t.shape[1]
        nxt = pltpu.roll(t, w - 16, axis=1)
        prv = pltpu.roll(t, 16, axis=1)
        lw = lax.broadcasted_iota(jnp.int32, t.shape, 1)
        return t * cosw + jnp.where((lw % 32) < 16, nxt, prv) * sinw

    def dup_halves(t):
        lane = lax.broadcasted_iota(jnp.int32, t.shape, 1)
        sw = pltpu.roll(t, 64, axis=1)
        lo = lane < 64
        return jnp.concatenate([jnp.where(lo, t, sw), jnp.where(lo, sw, t)], axis=1)

    pq = _dot(hb, w_ref[:, 0:Q_END])
    qn = headnorm(pq, qg_ref[...], gsum_ref[...])
    cos4 = jnp.concatenate([cos] * 4, axis=1)
    sin4 = jnp.concatenate([sin] * 4, axis=1)
    q_ref[...] = (rope(qn, cos4, sin4) * (HEAD_DIM ** -0.5)).astype(BF)

    pk = _dot(hb, w_ref[:, Q_END:K_END])
    kn = headnorm(pk, kg_ref[...], gsum_ref[0:KV_WIDTH, 0:KV_WIDTH])
    kd_ref[...] = dup_halves(rope(kn, cos, sin)).astype(BF)
    vd_ref[...] = dup_halves(_dot(hb, w_ref[:, K_END:V_END])).astype(BF)

    u_ref[...] = _dot(hb, w_ref[:, V_END:HY_END])
    f = _dot(hb, w_ref[:, HY_END:FN_END])
    pq_ref[...] = _dot(f.astype(BF), mfn_ref[...])
    for i in range(3):
        lo = FN_END + i * D_MODEL
        gate_ref[:, i * D_MODEL:(i + 1) * D_MODEL] = jax.nn.sigmoid(
            _dot(hb, w_ref[:, lo:lo + D_MODEL])).astype(BF)


def _phase_a(x2, sh, sc, g, w_in_bf, cos_t, sin_t, qg, kg, gsum, mfn, *, tm, mod_row, tab_row):
    t, d = x2.shape
    row3 = lambda i: (mod_row(i), 0, 0)
    full = lambda i: (0, 0)
    tok = lambda i: (i, 0)
    outs = [
        ((t, ATTN_WIDTH), BF), ((t, 2 * KV_WIDTH), BF), ((t, 2 * KV_WIDTH), BF),
        ((t, 3 * HYENA_WIDTH), F32), ((t, 2 * FNET_WIDTH), F32), ((t, 3 * D_MODEL), BF),
    ]
    return pl.pallas_call(
        _phase_a_kernel,
        grid=(t // tm,),
        in_specs=[
            pl.BlockSpec((tm, d), tok),
            pl.BlockSpec((1, 1, d), row3),
            pl.BlockSpec((1, 1, d), row3),
            pl.BlockSpec((1, d), full),
            pl.BlockSpec((d, IN_WIDTH), full),
            pl.BlockSpec((tm, LANES), lambda i: (tab_row(i), 0)),
            pl.BlockSpec((tm, LANES), lambda i: (tab_row(i), 0)),
            pl.BlockSpec((1, ATTN_WIDTH), full),
            pl.BlockSpec((1, KV_WIDTH), full),
            pl.BlockSpec((ATTN_WIDTH, ATTN_WIDTH), full),
            pl.BlockSpec((FNET_WIDTH, 2 * FNET_WIDTH), full),
        ],
        out_specs=[pl.BlockSpec((tm, s[1]), tok) for s, _ in outs],
        out_shape=[jax.ShapeDtypeStruct(s, dt) for s, dt in outs],
        compiler_params=_cparams("parallel"),
        name="phase_a",
    )(x2, sh, sc, g, w_in_bf, cos_t, sin_t, qg, kg, gsum, mfn)


def _attn_kernel(sink_ref, q_ref, kd_ref, vd_ref, kc_ref, vc_ref, o_ref, *, local, seq_len):
    tq = q_ref.shape[1]
    nblk = tq // QBLK
    gq = N_HEADS // N_KV_HEADS
    rows = gq * QBLK
    lane = lax.broadcasted_iota(jnp.int32, (QBLK, LANES), 1)
    lo_half = lane < 64
    hrow = lax.broadcasted_iota(jnp.int32, (rows, 1), 0) // QBLK
    nband = 3 * QBLK
    for blk in range(nblk):
        r0 = blk * QBLK
        qb = q_ref[0, r0:r0 + QBLK, :]
        if local:
            n = pl.program_id(1) * nblk + blk
            start = pl.multiple_of(jnp.clip((n - 1) * QBLK, 0, seq_len - nband), QBLK)
            qpos = n * QBLK + lax.broadcasted_iota(jnp.int32, (rows, nband), 0) % QBLK
            kpos = start + lax.broadcasted_iota(jnp.int32, (rows, nband), 1)
            valid = jnp.abs(qpos - kpos) <= WINDOW
        for g in range(N_KV_HEADS):
            parts = []
            for hh in range(gq):
                h = gq * g + hh
                qc = qb[:, (h // 2) * LANES:(h // 2 + 1) * LANES]
                keep = lo_half if h % 2 == 0 else jnp.logical_not(lo_half)
                parts.append(jnp.where(keep, qc, jnp.zeros_like(qc)))
            q4 = jnp.concatenate(parts, axis=0)
            sk = jnp.full((rows, 1), sink_ref[gq * g + gq - 1], F32)
            for hh in range(gq - 2, -1, -1):
                sk = jnp.where(hrow == hh, sink_ref[gq * g + hh], sk)
            gl = slice(g * LANES, (g + 1) * LANES)
            s_ctx = _dot_nt(q4, kc_ref[0, :, gl])
            m = jnp.maximum(jnp.max(s_ctx, axis=1, keepdims=True), sk)
            if local:
                s_loc = _dot_nt(q4, kd_ref[0, pl.ds(start, nband), gl])
                s_loc = jnp.where(valid, s_loc, NEG)
                m = jnp.maximum(m, jnp.max(s_loc, axis=1, keepdims=True))
            p_ctx = jnp.exp(s_ctx - m)
            den = jnp.sum(p_ctx, axis=1, keepdims=True) + jnp.exp(sk - m)
            o = _dot(p_ctx.astype(BF), vc_ref[0, :, gl])
            if local:
                p_loc = jnp.exp(s_loc - m)
                den = den + jnp.sum(p_loc, axis=1, keepdims=True)
                o = o + _dot(p_loc.astype(BF), vd_ref[0, pl.ds(start, nband), gl])
            o = o / den
            for cc in range(gq // 2):
                col = (gq // 2) * g + cc
                oa = o[(2 * cc) * QBLK:(2 * cc + 1) * QBLK]
                ob = o[(2 * cc + 1) * QBLK:(2 * cc + 2) * QBLK]
                o_ref[0, r0:r0 + QBLK, col * LANES:(col + 1) * LANES] = (
                    jnp.where(lo_half, oa, ob).astype(BF))


def _attention(sink, q, kd, vd, kc, vc, *, local, tq):
    b, lq, _ = q.shape
    lk = kd.shape[1]
    c = kc.shape[1]
    kern = functools.partial(_attn_kernel, local=local, seq_len=lk)
    return pl.pallas_call(
        kern,
        grid=(b, lq // tq),
        in_specs=[
            pl.BlockSpec(memory_space=pltpu.SMEM),
            pl.BlockSpec((1, tq, ATTN_WIDTH), lambda bi, i: (bi, i, 0)),
            pl.BlockSpec((1, lk, 2 * KV_WIDTH), lambda bi, i: (bi, 0, 0)),
            pl.BlockSpec((1, lk, 2 * KV_WIDTH), lambda bi, i: (bi, 0, 0)),
            pl.BlockSpec((1, c, 2 * KV_WIDTH), lambda bi, i: (bi, 0, 0)),
            pl.BlockSpec((1, c, 2 * KV_WIDTH), lambda bi, i: (bi, 0, 0)),
        ],
        out_specs=pl.BlockSpec((1, tq, ATTN_WIDTH), lambda bi, i: (bi, i, 0)),
        out_shape=jax.ShapeDtypeStruct((b, lq, ATTN_WIDTH), BF),
        compiler_params=_cparams("parallel", "parallel"),
        name="attn_local" if local else "attn_ctx",
    )(sink, q, kd, vd, kc, vc)


def _hy_prep_kernel(u_ref, w_ref, b_ref, o_ref):
    u = u_ref[0]
    n = u.shape[0]
    row = lax.broadcasted_iota(jnp.int32, u.shape, 0)
    prv = jnp.where(row == 0, 0.0, pltpu.roll(u, 1, axis=0))
    nxt = jnp.where(row == n - 1, 0.0, pltpu.roll(u, n - 1, axis=0))
    o_ref[0] = prv * w_ref[0:1, :] + u * w_ref[1:2, :] + nxt * w_ref[2:3, :] + b_ref[...]


def _hy_prep(u, conv_w, conv_b):
    b, n, w = u.shape
    return pl.pallas_call(
        _hy_prep_kernel,
        grid=(b, w // LANES),
        in_specs=[
            pl.BlockSpec((1, n, LANES), lambda bi, j: (bi, 0, j)),
            pl.BlockSpec((3, LANES), lambda bi, j: (0, j)),
            pl.BlockSpec((1, LANES), lambda bi, j: (0, j)),
        ],
        out_specs=pl.BlockSpec((1, n, LANES), lambda bi, j: (bi, 0, j)),
        out_shape=jax.ShapeDtypeStruct((b, n, w), F32),
        compiler_params=_cparams("parallel", "parallel"),
        name="hy_prep",
    )(u, conv_w, conv_b)


def _hy_filter_kernel(z_ref, w1_ref, b1_ref, f1_ref, w2_ref, b2_ref, f2_ref, w3_ref, dl_ref,
                      k_ref, nrm_ref, *, n_half):
    i = pl.program_id(0)
    tm = z_ref.shape[0]
    z = z_ref[...]
    h = jnp.sin(f1_ref[...] * (_dot3(z, w1_ref[...]) + b1_ref[...]))
    h = jnp.sin(f2_ref[...] * (_dot3(h, w2_ref[...]) + b2_ref[...]))
    h = _dot3(h, w3_ref[...])
    h = h * jnp.exp(-z[:, 0:1] * dl_ref[...])
    row = i * tm + lax.broadcasted_iota(jnp.int32, (tm, 1), 0)
    h = jnp.where(row == n_half, 0.0, h)
    k_ref[...] = h

    @pl.when(i == 0)
    def _():
        nrm_ref[...] = jnp.zeros_like(nrm_ref)

    nrm_ref[...] += jnp.sum(jnp.abs(h), axis=0, keepdims=True)


def _hy_filter(zfeat2, w1p, b1, f1, w2, b2, f2, w3, deltas2, *, tm):
    n2 = zfeat2.shape[0]
    n_half = n2 // 2
    wout = w3.shape[1] // 2
    full = lambda i: (0, 0)
    kern = functools.partial(_hy_filter_kernel, n_half=n_half)
    return pl.pallas_call(
        kern,
        grid=(n2 // tm,),
        in_specs=[
            pl.BlockSpec((tm, zfeat2.shape[1]), lambda i: (i, 0)),
            pl.BlockSpec(w1p.shape, full), pl.BlockSpec(b1.shape, full), pl.BlockSpec(f1.shape, full),
            pl.BlockSpec(w2.shape, full), pl.BlockSpec(b2.shape, full), pl.BlockSpec(f2.shape, full),
            pl.BlockSpec((w3.shape[0], wout), lambda i: (0, (i * tm) // n_half)),
            pl.BlockSpec(deltas2.shape, full),
        ],
        out_specs=[pl.BlockSpec((tm, wout), lambda i: (i, 0)), pl.BlockSpec((1, wout), full)],
        out_shape=[jax.ShapeDtypeStruct((n2, wout), F32), jax.ShapeDtypeStruct((1, wout), F32)],
        compiler_params=_cparams("arbitrary"),
        name="hy_filter",
    )(zfeat2, w1p, b1, f1, w2, b2, f2, w3, deltas2)


def _dft_mats(k_out, r_in, period, sign, scale, n_in, real_out):
    k = np.arange(k_out)[:, None]
    r = np.arange(r_in)[None, :]
    ang = 2.0 * np.pi * ((k * r) % period) / period
    fr = np.cos(ang) * scale
    fi = sign * np.sin(ang) * scale
    if real_out:
        mats = [fr, -fi]
    else:
        mats = [np.concatenate([fr, fi], 0), np.concatenate([-fi, fr], 0)]
    return jnp.asarray(np.stack(mats[:n_in], 0), dtype=F32).astype(BF)


def _twiddle(s_n, k_n, n, sign, sbk):
    s0 = lax.broadcasted_iota(jnp.int32, (s_n // sbk, k_n, LANES), 0) * sbk
    k = lax.broadcasted_iota(jnp.int32, (s_n // sbk, k_n, LANES), 1)
    ang = (s0 * k).astype(F32) * (2.0 * math.pi / n)
    ang1 = lax.broadcasted_iota(jnp.int32, (k_n, LANES), 0).astype(F32) * (2.0 * math.pi / n)
    return jnp.cos(ang), sign * jnp.sin(ang), jnp.cos(ang1), sign * jnp.sin(ang1)


def _stage_kernel(*refs, n_in, r_in, k_mid, k_out, sbk, tw, spec, second, gate, real_out,
                  transposed_out, flat):
    it = iter(refs)
    x_refs = [next(it) for _ in range(n_in)]
    g_ref = next(it)
    g2_ref = next(it) if second else None
    tw_refs = [next(it) for _ in range(4)] if tw else None
    spec_refs = [next(it) for _ in range(3)] if spec else None
    gate_refs = [next(it) for _ in range(5)] if gate else None
    out_refs = [next(it)] if (real_out or gate) else [next(it), next(it)]
    if not flat:
        x_refs = [r.reshape(r_in * sbk, LANES) for r in x_refs]
        if spec:
            spec_refs = [r.reshape(k_mid * sbk, LANES) for r in spec_refs[:2]] + spec_refs[2:]
        if gate:
            gate_refs = [r.reshape(k_out * sbk, LANES) for r in gate_refs[:4]] + gate_refs[4:]
            out_refs = [out_refs[0].reshape(2 * k_out * sbk, LANES)]
        elif not transposed_out:
            out_refs = [r.reshape(k_out * sbk, LANES) for r in out_refs]
    if spec:
        inv = 1.0 / spec_refs[2][...]
    if tw:
        tr, ti = tw_refs[0][0], tw_refs[1][0]
        wr, wi = tw_refs[2][...], tw_refs[3][...]
    for j in range(sbk):
        acc = None
        for xi, x_ref in enumerate(x_refs):
            xv = x_ref[...] if flat else x_ref[pl.ds(j, r_in, stride=sbk), :]
            d = _dot(g_ref[xi], xv.astype(BF))
            acc = d if acc is None else acc + d
        if real_out:
            ys = [acc]
        else:
            yr, yi = acc[:k_mid], acc[k_mid:]
            if spec:
                rows = slice(None) if flat else pl.ds(j, k_mid, stride=sbk)
                sr = spec_refs[0][rows, :] * inv
                si = spec_refs[1][rows, :] * inv
                yr, yi = yr * sr - yi * si, yr * si + yi * sr
            if second:
                acc = _dot(g2_ref[0], yr.astype(BF)) + _dot(g2_ref[1], yi.astype(BF))
                yr, yi = acc[:k_out], acc[k_out:]
            if tw:
                yr, yi = yr * tr - yi * ti, yr * ti + yi * tr
                if j + 1 < sbk:
                    tr, ti = tr * wr - ti * wi, tr * wi + ti * wr
            ys = [yr, yi]
        if gate:
            o_ref = out_refs[0]
            for part, y in enumerate(ys):
                rows = slice(None) if flat else pl.ds(j, k_out, stride=sbk)
                val = gate_refs[part][rows, :] * (y + gate_refs[4][...] * gate_refs[2 + part][rows, :])
                if flat:
                    o_ref[part] = val.astype(o_ref.dtype)
                else:
                    o_ref[pl.ds(part * k_out * sbk + j, k_out, stride=sbk), :] = val.astype(o_ref.dtype)
            continue
        for o_ref, y in zip(out_refs, ys):
            if flat:
                o_ref[...] = y.astype(o_ref.dtype)
            elif transposed_out:
                o_ref[0, j] = y.astype(o_ref.dtype)
            else:
                o_ref[pl.ds(j, k_out, stride=sbk), :] = y.astype(o_ref.dtype)


def _fft_stage(xs, x_sel, gmat, *, r_in, s_n, k_out, n_groups, n_cblk, transposed_out, real_out,
               out_dtype=F32, g2mat=None, tw=None, spec=None, spec_sel=None, gate=None, sbk=STAGE_ROWS,
               name="fft_stage"):
    n_in = len(xs)
    flat = s_n == 1
    sbk = 1 if flat else min(sbk, s_n)
    cb = LANES
    in_specs, args = [], []
    for x, sel in zip(xs, x_sel):
        if flat:
            in_specs.append(pl.BlockSpec((None, r_in, cb), lambda s, g, c, sel=sel: (sel(g, c)[0], 0, sel(g, c)[1])))
            args.append(x)
        else:
            xv = x.reshape(x.shape[0], x.shape[1] // s_n, s_n, x.shape[2])
            in_specs.append(pl.BlockSpec((1, r_in, sbk, cb),
                                         lambda s, g, c, sel=sel: (sel(g, c)[0], 0, s, sel(g, c)[1])))
            args.append(xv)
    in_specs.append(pl.BlockSpec(gmat.shape, lambda s, g, c: (0, 0, 0)))
    args.append(gmat)
    k_mid = gmat.shape[1] // (1 if real_out else 2)
    if g2mat is not None:
        in_specs.append(pl.BlockSpec(g2mat.shape, lambda s, g, c: (0, 0, 0)))
        args.append(g2mat)
    if tw is not None:
        for tarr in tw[:2]:
            in_specs.append(pl.BlockSpec((1, k_out, LANES), lambda s, g, c: (s, 0, 0)))
            args.append(tarr)
        for tarr in tw[2:]:
            in_specs.append(pl.BlockSpec((k_out, LANES), lambda s, g, c: (0, 0)))
            args.append(tarr)
    if spec is not None:
        kr, ki, nrm = spec
        for arr in (kr, ki):
            if flat:
                in_specs.append(pl.BlockSpec((k_mid, cb), lambda s, g, c: (0, spec_sel(g, c))))
                args.append(arr)
            else:
                in_specs.append(pl.BlockSpec((1, k_mid, sbk, cb), lambda s, g, c: (0, 0, s, spec_sel(g, c))))
                args.append(arr.reshape(1, k_mid, s_n, arr.shape[-1]))
        in_specs.append(pl.BlockSpec((1, cb), lambda s, g, c: (0, spec_sel(g, c))))
        args.append(nrm)
    if gate is not None:
        (ga, gblk), (za, zblk), bias = gate
        for arr, blk in ((ga, gblk), (za, zblk)):
            for bi in (0, 1):
                if flat:
                    in_specs.append(pl.BlockSpec((None, k_out, cb), lambda s, g, c, bi=bi, blk=blk: (bi, 0, blk + c)))
                    args.append(arr)
                else:
                    in_specs.append(pl.BlockSpec((1, k_out, sbk, cb),
                                                 lambda s, g, c, bi=bi, blk=blk: (bi, 0, s, blk + c)))
                    args.append(arr.reshape(arr.shape[0], k_out, s_n, arr.shape[-1]))
        in_specs.append(pl.BlockSpec((1, cb), lambda s, g, c: (0, c)))
        args.append(bias)
    ctot = n_cblk * cb
    if gate is not None:
        n_groups = 2
        if flat:
            oshape = (2, k_out, ctot)
            ospec = pl.BlockSpec((2, k_out, cb), lambda s, g, c: (0, 0, c))
        else:
            oshape = (2, k_out, s_n, ctot)
            ospec = pl.BlockSpec((2, k_out, sbk, cb), lambda s, g, c: (0, 0, s, c))
    elif flat:
        oshape = (n_groups, k_out, ctot)
        ospec = pl.BlockSpec((None, k_out, cb), lambda s, g, c: (g, 0, c))
    elif transposed_out:
        oshape = (n_groups, s_n, k_out, ctot)
        ospec = pl.BlockSpec((1, sbk, k_out, cb), lambda s, g, c: (g, s, 0, c))
    else:
        oshape = (n_groups, k_out, s_n, ctot)
        ospec = pl.BlockSpec((1, k_out, sbk, cb), lambda s, g, c: (g, 0, s, c))
    n_out = 1 if (real_out or gate is not None) else 2
    kern = functools.partial(_stage_kernel, n_in=n_in, r_in=r_in, k_mid=k_mid, k_out=k_out, sbk=sbk,
                             tw=tw is not None, spec=spec is not None, second=g2mat is not None,
                             gate=gate is not None, real_out=real_out, transposed_out=transposed_out,
                             flat=flat)
    n_grid_groups = 1 if gate is not None else n_groups
    outs = pl.pallas_call(
        kern,
        grid=(s_n // sbk, n_grid_groups, n_cblk),
        in_specs=in_specs,
        out_specs=[ospec] * n_out,
        out_shape=[jax.ShapeDtypeStruct(oshape, out_dtype)] * n_out,
        compiler_params=_cparams("parallel", "parallel", "parallel"),
        name=name,
    )(*args)
    return [o.reshape(n_groups, -1, ctot) for o in outs]


def _split_len(n):
    if n <= 1024:
        return n, 1
    s = 128
    return n // s, s


def _fft_forward(xs, x_sel, n, n_rows, *, n_groups, n_cblk, name="fwd"):
    n1, s = _split_len(n)
    n_in = len(xs)
    if s == 1:
        g = _dft_mats(n, n_rows, n, -1.0, 1.0, n_in, False)
        return _fft_stage(xs, x_sel, g, r_in=n_rows, s_n=1, k_out=n, n_groups=n_groups, n_cblk=n_cblk,
                          transposed_out=False, real_out=False, name=name + "_direct")
    r1 = n_rows // s
    g1 = _dft_mats(n1, r1, n1, -1.0, 1.0, n_in, False)
    tw = _twiddle(s, n1, n, -1.0, STAGE_ROWS)
    ar, ai = _fft_stage(xs, x_sel, g1, r_in=r1, s_n=s, k_out=n1, n_groups=n_groups, n_cblk=n_cblk,
                        transposed_out=True, real_out=False, tw=tw, name=name + "_s1")
    g2 = _dft_mats(s, s, s, -1.0, 1.0, 2, False)
    ident = lambda g, c: (g, c)
    return _fft_stage([ar, ai], [ident, ident], g2, r_in=s, s_n=n1, k_out=s, n_groups=n_groups,
                      n_cblk=n_cblk, transposed_out=False, real_out=False, name=name + "_s2")


def _hyena(uc, k_filt, nrm, hy_bias):
    b, n, _ = uc.shape
    w = HYENA_WIDTH
    wblk = w // LANES
    nfft = 2 * n
    n1, s = _split_len(nfft)
    ident = lambda g, c: (g, c)
    kr, ki = _fft_forward([k_filt[None]], [lambda g, c: (0, c)], nfft, nfft, n_groups=1,
                          n_cblk=HYENA_ORDER * wblk, name="hy_filt_fft")
    z, zblk = uc, 2 * wblk
    for o in range(HYENA_ORDER):
        sel_r = lambda g, c, zblk=zblk: (0, zblk + c)
        sel_i = lambda g, c, zblk=zblk: (1, zblk + c)
        spec = (kr[0], ki[0], nrm)
        spec_sel = lambda g, c, o=o: o * wblk + c
        gate = ((uc, o * wblk), (z, zblk), hy_bias[o:o + 1])
        common = dict(n_groups=1, n_cblk=wblk, real_out=False)
        if s == 1:
            gf = _dft_mats(nfft, n, nfft, -1.0, 1.0, 2, False)
            gi = _dft_mats(n, nfft, nfft, 1.0, 1.0 / nfft, 2, False)
            (z,) = _fft_stage([z, z], [sel_r, sel_i], gf, r_in=n, s_n=1, k_out=n, transposed_out=False,
                              g2mat=gi, spec=spec, spec_sel=spec_sel, gate=gate, name="hy_direct", **common)
        else:
            r1 = n // s
            g1 = _dft_mats(n1, r1, n1, -1.0, 1.0, 2, False)
            ar, ai = _fft_stage([z, z], [sel_r, sel_i], g1, r_in=r1, s_n=s, k_out=n1, transposed_out=True,
                                tw=_twiddle(s, n1, nfft, -1.0, STAGE_ROWS), name="hy_s1", **common)
            g2 = _dft_mats(s, s, s, -1.0, 1.0, 2, False)
            g3 = _dft_mats(s, s, s, 1.0, 1.0, 2, False)
            qr, qi = _fft_stage([ar, ai], [ident, ident], g2, r_in=s, s_n=n1, k_out=s, transposed_out=True,
                                g2mat=g3, tw=_twiddle(n1, s, nfft, 1.0, STAGE_ROWS), spec=spec,
                                spec_sel=spec_sel, name="hy_mid", **common)
            g4 = _dft_mats(n // s, n1, n1, 1.0, 1.0 / nfft, 2, False)
            (z,) = _fft_stage([qr, qi], [ident, ident], g4, r_in=n1, s_n=s, k_out=n // s,
                              transposed_out=False, gate=gate, name="hy_last", **common)
        zblk = 0
    return z.reshape(b * n, w)


def _fnet(pq):
    b, n, _ = pq.shape
    w = FNET_WIDTH
    wblk = w // LANES
    scale = 1.0 / math.sqrt(n * FNET_GROUP_DIM)
    sel_r = lambda g, c: (g, c)
    sel_i = lambda g, c: (g, wblk + c)
    ident = lambda g, c: (g, c)
    n1, s = _split_len(n)
    if s == 1:
        g = _dft_mats(n, n, n, -1.0, scale, 2, True)
        (y,) = _fft_stage([pq, pq], [sel_r, sel_i], g, r_in=n, s_n=1, k_out=n, n_groups=b, n_cblk=wblk,
                          transposed_out=False, real_out=True, name="fnet_direct")
        return y.reshape(b * n, w)
    g1 = _dft_mats(n1, n1, n1, -1.0, 1.0, 2, False)
    tw = _twiddle(s, n1, n, -1.0, STAGE_ROWS)
    ar, ai = _fft_stage([pq, pq], [sel_r, sel_i], g1, r_in=n1, s_n=s, k_out=n1, n_groups=b, n_cblk=wblk,
                        transposed_out=True, real_out=False, tw=tw, name="fnet_s1")
    g2 = _dft_mats(s, s, s, -1.0, scale, 2, True)
    (y,) = _fft_stage([ar, ai], [ident, ident], g2, r_in=s, s_n=n1, k_out=s, n_groups=b, n_cblk=wblk,
                      transposed_out=False, real_out=True, name="fnet_s2")
    return y.reshape(b * n, w)


def _merge_kernel(x_ref, gt_ref, a_ref, h_ref, f_ref, gate_ref, wa_ref, wh_ref, wf_ref, wo_ref, o_ref):
    d = D_MODEL
    m = gate_ref[:, 0:d].astype(F32) * _dot(a_ref[...], wa_ref[...])
    m = m + gate_ref[:, d:2 * d].astype(F32) * _dot(h_ref[...].astype(BF), wh_ref[...])
    m = m + gate_ref[:, 2 * d:3 * d].astype(F32) * _dot(f_ref[...].astype(BF), wf_ref[...])
    y = _dot(m.astype(BF), wo_ref[...])
    o_ref[...] = x_ref[...] + gt_ref[0] * y


def _merge(x2, gt, attn_o, hy_o, fn_o, gates, wa, wh, wf, wo, *, tm, mod_row):
    t, d = x2.shape
    tok = lambda i: (i, 0)
    full = lambda i: (0, 0)
    return pl.pallas_call(
        _merge_kernel,
        grid=(t // tm,),
        in_specs=[
            pl.BlockSpec((tm, d), tok),
            pl.BlockSpec((1, 1, d), lambda i: (mod_row(i), 0, 0)),
            pl.BlockSpec((tm, ATTN_WIDTH), tok),
            pl.BlockSpec((tm, HYENA_WIDTH), tok),
            pl.BlockSpec((tm, FNET_WIDTH), tok),
            pl.BlockSpec((tm, 3 * d), tok),
            pl.BlockSpec(wa.shape, full), pl.BlockSpec(wh.shape, full),
            pl.BlockSpec(wf.shape, full), pl.BlockSpec(wo.shape, full),
        ],
        out_specs=pl.BlockSpec((tm, d), tok),
        out_shape=jax.ShapeDtypeStruct((t, d), F32),
        compiler_params=_cparams("parallel"),
        name="merge",
    )(x2, gt, attn_o, hy_o, fn_o, gates, wa, wh, wf, wo)


def _ffn_kernel(x_ref, sh_ref, sc_ref, gt_ref, g_ref, wg_ref, wu_ref, wd_ref, o_ref, h_scr, acc_scr):
    f = pl.program_id(1)

    @pl.when(f == 0)
    def _():
        h_scr[...] = _modulated_norm(x_ref[...], g_ref[...], sc_ref[0], sh_ref[0]).astype(BF)
        acc_scr[...] = jnp.zeros_like(acc_scr)

    hb = h_scr[...]
    act = _silu(_dot(hb, wg_ref[...])) * _dot(hb, wu_ref[...])
    acc_scr[...] += _dot(act.astype(BF), wd_ref[...])

    @pl.when(f == pl.num_programs(1) - 1)
    def _():
        o_ref[...] = x_ref[...] + gt_ref[0] * acc_scr[...]


def _ffn(x2, sh, sc, gt, g, wg, wu, wd, *, tm, tf, mod_row):
    t, d = x2.shape
    ff = wg.shape[1]
    row3 = lambda i, f: (mod_row(i), 0, 0)
    return pl.pallas_call(
        _ffn_kernel,
        grid=(t // tm, ff // tf),
        in_specs=[
            pl.BlockSpec((tm, d), lambda i, f: (i, 0)),
            pl.BlockSpec((1, 1, d), row3), pl.BlockSpec((1, 1, d), row3), pl.BlockSpec((1, 1, d), row3),
            pl.BlockSpec((1, d), lambda i, f: (0, 0)),
            pl.BlockSpec((d, tf), lambda i, f: (0, f)),
            pl.BlockSpec((d, tf), lambda i, f: (0, f)),
            pl.BlockSpec((tf, d), lambda i, f: (f, 0)),
        ],
        out_specs=pl.BlockSpec((tm, d), lambda i, f: (i, 0)),
        out_shape=jax.ShapeDtypeStruct((t, d), F32),
        scratch_shapes=[pltpu.VMEM((tm, d), BF), pltpu.VMEM((tm, d), F32)],
        compiler_params=_cparams("parallel", "arbitrary"),
        name="ffn_dense",
    )(x2, sh, sc, gt, g, wg, wu, wd)


def _top2(logits):
    lane = lax.broadcasted_iota(jnp.int32, logits.shape, 1)
    lg = jnp.where(lane < N_EXPERTS, logits, -jnp.inf)
    m1 = jnp.max(lg, axis=1, keepdims=True)
    i1 = jnp.min(jnp.where(lg == m1, lane, LANES), axis=1, keepdims=True)
    lg2 = jnp.where(lane == i1, -jnp.inf, lg)
    m2 = jnp.max(lg2, axis=1, keepdims=True)
    i2 = jnp.min(jnp.where(lg2 == m2, lane, LANES), axis=1, keepdims=True)
    e2 = jnp.exp(m2 - m1)
    w1 = 1.0 / (1.0 + e2)
    return i1, i2, w1, e2 * w1


GROUP_TILE = 128
SLOT_RADIX = 64.0


def _moe_group_kernel(x_ref, sh_ref, sc_ref, g_ref, wr_ref, xg_ref, ws_ref, slot_ref, cnt_ref,
                      h_scr, rows_scr, wm_scr):
    j = pl.program_id(1)
    tb = x_ref.shape[0]
    gt_rows = xg_ref.shape[1]

    @pl.when(j == 0)
    def _():
        h = _modulated_norm(x_ref[...], g_ref[...], sc_ref[0], sh_ref[0])
        h_scr[...] = h.astype(BF)
        i1, i2, w1, w2 = _top2(_dot3(h, wr_ref[...]))
        lane = lax.broadcasted_iota(jnp.int32, (tb, LANES), 1)
        oh0 = jnp.where(lane == i1, 1.0, 0.0)
        oh1 = jnp.where(lane == i2, 1.0, 0.0)
        c0 = jnp.sum(oh0, axis=0, keepdims=True)
        cnt = c0 + jnp.sum(oh1, axis=0, keepdims=True)
        tri = jnp.where(lax.broadcasted_iota(jnp.int32, (tb, tb), 1)
                        < lax.broadcasted_iota(jnp.int32, (tb, tb), 0), 1.0, 0.0).astype(BF)
        pre0 = _dot(tri, oh0.astype(BF))
        pre1 = _dot(tri, oh1.astype(BF)) + c0
        tiles = jnp.ceil(cnt * (1.0 / gt_rows))
        upper = jnp.where(lax.broadcasted_iota(jnp.int32, (LANES, LANES), 0)
                          < lax.broadcasted_iota(jnp.int32, (LANES, LANES), 1), 1.0, 0.0).astype(BF)
        off = _dot(jnp.broadcast_to(tiles, (8, LANES)).astype(BF), upper)[0:1] * float(gt_rows)
        slot0 = jnp.sum(oh0 * (off + pre0), axis=1, keepdims=True)
        slot1 = jnp.sum(oh1 * (off + pre1), axis=1, keepdims=True)
        slot_ref[0] = jnp.where(lane == 0, slot0, jnp.where(lane == 1, slot1, 0.0))
        cnt_ref[0] = jnp.broadcast_to(cnt, (8, LANES))
        hi0 = jnp.floor(slot0 * (1.0 / SLOT_RADIX))
        hi1 = jnp.floor(slot1 * (1.0 / SLOT_RADIX))
        digits = jnp.where(lane == 0, hi0, jnp.where(lane == 1, slot0 - SLOT_RADIX * hi0,
                           jnp.where(lane == 2, hi1, jnp.where(lane == 3, slot1 - SLOT_RADIX * hi1, 0.0))))
        sel = jnp.where(lax.broadcasted_iota(jnp.int32, (8, LANES), 0)
                        == lax.broadcasted_iota(jnp.int32, (8, LANES), 1), 1.0, 0.0).astype(BF)
        rows_scr[...] = _dot_nt(sel, digits.astype(BF))
        w1h, w1l = _split(w1)
        w1m, w1l = _split(w1 - w1h.astype(F32))
        w2h, w2l = _split(w2)
        w2m, w2l = _split(w2 - w2h.astype(F32))
        cols = [w1h, w1m, w1l, w2h, w2m, w2l]
        wm = jnp.zeros((tb, LANES), F32)
        for li, col in enumerate(cols):
            wm = jnp.where(lane == li, col.astype(F32), wm)
        wm_scr[...] = wm.astype(BF)

    rows = rows_scr[...]
    s0 = rows[0:1] * SLOT_RADIX + rows[1:2]
    s1 = rows[2:3] * SLOT_RADIX + rows[3:4]
    pos = (lax.broadcasted_iota(jnp.int32, (gt_rows, tb), 0) + j * gt_rows).astype(F32)
    g0 = jnp.where(pos == s0, 1.0, 0.0).astype(BF)
    g1 = jnp.where(pos == s1, 1.0, 0.0).astype(BF)
    xg_ref[0] = _dot(g0 + g1, h_scr[...]).astype(BF)
    lane_w = lax.broadcasted_iota(jnp.int32, (gt_rows, LANES), 1)
    wsum = (jnp.where(lane_w < 3, _dot(g0, wm_scr[...]), 0.0)
            + jnp.where((lane_w >= 3) & (lane_w < 6), _dot(g1, wm_scr[...]), 0.0))
    ws_ref[0] = jnp.broadcast_to(jnp.sum(wsum, axis=1, keepdims=True), (gt_rows, LANES))


def _moe_group(x2, sh, sc, g, wr_pad, *, tb, nt, mod_row):
    t, d = x2.shape
    nb = t // tb
    row3 = lambda b, j: (mod_row(b), 0, 0)
    return pl.pallas_call(
        _moe_group_kernel,
        grid=(nb, nt),
        in_specs=[
            pl.BlockSpec((tb, d), lambda b, j: (b, 0)),
            pl.BlockSpec((1, 1, d), row3), pl.BlockSpec((1, 1, d), row3),
            pl.BlockSpec((1, d), lambda b, j: (0, 0)),
            pl.BlockSpec((d, LANES), lambda b, j: (0, 0)),
        ],
        out_specs=[
            pl.BlockSpec((1, GROUP_TILE, d), lambda b, j: (b * nt + j, 0, 0)),
            pl.BlockSpec((1, GROUP_TILE, LANES), lambda b, j: (b * nt + j, 0, 0)),
            pl.BlockSpec((1, tb, LANES), lambda b, j: (b, 0, 0)),
            pl.BlockSpec((1, 8, LANES), lambda b, j: (b, 0, 0)),
        ],
        out_shape=[
            jax.ShapeDtypeStruct((nb * nt, GROUP_TILE, d), BF),
            jax.ShapeDtypeStruct((nb * nt, GROUP_TILE, LANES), F32),
            jax.ShapeDtypeStruct((nb, tb, LANES), F32),
            jax.ShapeDtypeStruct((nb, 8, LANES), F32),
        ],
        scratch_shapes=[pltpu.VMEM((tb, d), BF), pltpu.VMEM((8, tb), F32), pltpu.VMEM((tb, LANES), BF)],
        compiler_params=_cparams("parallel", "arbitrary"),
        name="moe_group",
    )(x2, sh, sc, g, wr_pad)


def _moe_schedule(cnt, nt):
    tiles = (cnt + GROUP_TILE - 1) // GROUP_TILE
    nb, ne = tiles.shape
    tot_e = tiles.sum(0)
    cum_e = jnp.cumsum(tot_e)
    n_used = cum_e[-1]
    pos = jnp.minimum(jnp.arange(nb * nt, dtype=jnp.int32), n_used - 1)
    e = jnp.sum(pos[:, None] >= cum_e[None, :], axis=1).astype(jnp.int32)
    r = pos - (cum_e - tot_e)[e]
    cum_b = jnp.cumsum(tiles, axis=0)
    cum_b_e = cum_b.T[e]
    blk = jnp.sum(r[:, None] >= cum_b_e, axis=1).astype(jnp.int32)
    first_in_blk = (jnp.cumsum(tiles, axis=1) - tiles)[blk, e]
    j = first_in_blk + r - (cum_b[blk, e] - tiles[blk, e])
    return ((blk * nt + j).astype(jnp.int32), e, n_used.astype(jnp.int32).reshape(1),
            tiles.sum(1).astype(jnp.int32))


def _moe_expert_kernel(tile_ref, exp_ref, nused_ref, xg_ref, ws_ref, wg_ref, wu_ref, wd_ref, y_ref):
    @pl.when(pl.program_id(0) < nused_ref[0])
    def _():
        x = xg_ref[0]
        act = _silu(_dot(x, wg_ref[0])) * _dot(x, wu_ref[0])
        y = _dot(act.astype(BF), wd_ref[0])
        w = ws_ref[0]
        y_ref[0] = (y * jnp.concatenate([w] * (y.shape[1] // LANES), axis=1)).astype(BF)


def _moe_experts(sched_tile, sched_exp, n_used, xg, ws, wg, wu, wd):
    n_sched, gt_rows, d = xg.shape
    ff = wg.shape[2]
    tile3 = lambda i, st, se, nu: (st[i], 0, 0)
    exp3 = lambda i, st, se, nu: (se[i], 0, 0)
    return pl.pallas_call(
        _moe_expert_kernel,
        grid_spec=pltpu.PrefetchScalarGridSpec(
            num_scalar_prefetch=3,
            grid=(n_sched,),
            in_specs=[
                pl.BlockSpec((1, gt_rows, d), tile3),
                pl.BlockSpec((1, gt_rows, LANES), tile3),
                pl.BlockSpec((1, d, ff), exp3),
                pl.BlockSpec((1, d, ff), exp3),
                pl.BlockSpec((1, ff, d), exp3),
            ],
            out_specs=pl.BlockSpec((1, gt_rows, d), tile3),
        ),
        out_shape=jax.ShapeDtypeStruct((n_sched, gt_rows, d), BF),
        compiler_params=_cparams("arbitrary"),
        name="moe_experts",
    )(sched_tile, sched_exp, n_used, xg, ws, wg, wu, wd)


def _moe_combine_kernel(nt_ref, x_ref, gt_ref, slot_ref, y_ref, o_ref, acc_scr):
    b = pl.program_id(0)
    j = pl.program_id(1)
    tb = x_ref.shape[0]
    gt_rows = y_ref.shape[1]

    @pl.when(j == 0)
    def _():
        acc_scr[...] = jnp.zeros_like(acc_scr)

    @pl.when(j < nt_ref[b])
    def _():
        sl = slot_ref[0]
        pos = (lax.broadcasted_iota(jnp.int32, (tb, gt_rows), 1) + j * gt_rows).astype(F32)
        p = jnp.where((pos == sl[:, 0:1]) | (pos == sl[:, 1:2]), 1.0, 0.0).astype(BF)
        acc_scr[...] += _dot(p, y_ref[0])

    @pl.when(j == pl.num_programs(1) - 1)
    def _():
        o_ref[...] = x_ref[...] + gt_ref[0] * acc_scr[...]


def _moe_combine(ntiles_b, x2, gt, slots, yg, *, tb, nt, mod_row):
    t, d = x2.shape
    nb = t // tb
    return pl.pallas_call(
        _moe_combine_kernel,
        grid_spec=pltpu.PrefetchScalarGridSpec(
            num_scalar_prefetch=1,
            grid=(nb, nt),
            in_specs=[
                pl.BlockSpec((tb, d), lambda b, j, n: (b, 0)),
                pl.BlockSpec((1, 1, d), lambda b, j, n: (mod_row(b), 0, 0)),
                pl.BlockSpec((1, tb, LANES), lambda b, j, n: (b, 0, 0)),
                pl.BlockSpec((1, GROUP_TILE, d), lambda b, j, n: (b * nt + jnp.minimum(j, n[b] - 1), 0, 0)),
            ],
            out_specs=pl.BlockSpec((tb, d), lambda b, j, n: (b, 0)),
            scratch_shapes=[pltpu.VMEM((tb, d), F32)],
        ),
        out_shape=jax.ShapeDtypeStruct((t, d), F32),
        compiler_params=_cparams("parallel", "arbitrary"),
        name="moe_combine",
    )(ntiles_b, x2, gt, slots, yg)


def _moe(x2, sh, sc, gt, g, wr_pad, wg, wu, wd, *, tb, mod_row):
    nt = 2 * tb // GROUP_TILE + N_EXPERTS - 1
    xg, ws, slots, cnt = _moe_group(x2, sh, sc, g, wr_pad, tb=tb, nt=nt, mod_row=mod_row)
    counts = cnt[:, 0, :N_EXPERTS].astype(jnp.int32)
    sched_tile, sched_exp, n_used, ntiles_b = _moe_schedule(counts, nt)
    yg = _moe_experts(sched_tile, sched_exp, n_used, xg, ws, wg, wu, wd)
    return _moe_combine(ntiles_b, x2, gt, slots, yg, tb=tb, nt=nt, mod_row=mod_row)


def _rope_tables(seq_len):
    pos = np.arange(seq_len)
    prow = (pos // GRID_W).astype(np.float32)
    pcol = (pos % GRID_W).astype(np.float32)
    n_freq = HEAD_DIM // 4
    inv = (np.float32(ROPE_THETA) ** (-np.arange(n_freq, dtype=np.float32) / n_freq)).astype(np.float32)
    ar = jnp.asarray(prow)[:, None] * jnp.asarray(inv)[None, :]
    ac = jnp.asarray(pcol)[:, None] * jnp.asarray(inv)[None, :]
    cos = jnp.concatenate([jnp.cos(ar)] * 2 + [jnp.cos(ac)] * 2, axis=1)
    sin = jnp.concatenate([-jnp.sin(ar), jnp.sin(ar), -jnp.sin(ac), jnp.sin(ac)], axis=1)
    return jnp.concatenate([cos, cos], axis=1), jnp.concatenate([sin, sin], axis=1)


def _head_sum_matrix():
    c = np.arange(ATTN_WIDTH)
    return jnp.asarray((c[:, None] // HEAD_DIM) == (c[None, :] // HEAD_DIM), dtype=F32).astype(BF)


def _fnet_channel_matrix():
    c = np.arange(FNET_WIDTH)
    same = (c[:, None] // FNET_GROUP_DIM) == (c[None, :] // FNET_GROUP_DIM)
    ang = 2.0 * np.pi * (((c[:, None] % FNET_GROUP_DIM) * (c[None, :] % FNET_GROUP_DIM)) % FNET_GROUP_DIM) / FNET_GROUP_DIM
    cb = np.where(same, np.cos(ang), 0.0)
    sb = np.where(same, np.sin(ang), 0.0)
    return jnp.asarray(np.concatenate([cb, -sb], axis=1), dtype=F32).astype(BF)


def _filter_features(n):
    t = jnp.linspace(0.0, 1.0, n, dtype=F32)[:, None]
    w = 2.0 * math.pi * jnp.arange(n, dtype=F32)[:, None] / n
    fb = jnp.linspace(1e-4, FILTER_BANDS - 1, FILTER_BANDS, dtype=F32)
    z = jnp.concatenate([t, jnp.cos(fb * w), -jnp.sin(fb * w)], axis=-1)
    z = jnp.pad(z, ((0, 0), (0, 64 - z.shape[1])))
    return jnp.concatenate([z, jnp.zeros((1, z.shape[1]), F32), z[:0:-1]], axis=0)


def _decay_rates():
    d = jnp.abs(jnp.linspace(math.log(DECAY_TARGET) / SLOW_DECAY_PCT, math.log(DECAY_TARGET) / FAST_DECAY_PCT,
                             HYENA_WIDTH, dtype=F32))
    return jnp.concatenate([d] * HYENA_ORDER)[None, :]


def kernel(x, c, ctx, c_ctx, w_ada, b_ada, norm1_g, norm2_g, w_in, q_norm_g, k_norm_g, attn_sink,
           hy_conv_w, hy_conv_b, hy_filt_w1, hy_filt_b1, hy_filt_freq1, hy_filt_w2, hy_filt_b2,
           hy_filt_freq2, hy_filt_w3, hy_bias, w_proj_attn, w_proj_hyena, w_proj_fnet, w_out,
           ffn_w_gate, ffn_w_up, ffn_w_down, moe_router, moe_w_gate, moe_w_up, moe_w_down):
    b, seq, d = x.shape
    n_ctx = ctx.shape[1]
    depth = w_ada.shape[0]
    tm = 256
    tiles_per_seq = seq // tm

    cond8 = jnp.concatenate([c, c_ctx[None, :], jnp.zeros((8 - b - 1, d), F32)], axis=0)
    mods = _adaln(cond8, w_ada, b_ada)

    cos_l, sin_l = _rope_tables(seq)
    cos_c = jnp.ones((n_ctx, LANES), F32)
    sin_c = jnp.zeros((n_ctx, LANES), F32)
    gsum = _head_sum_matrix()
    mfn = _fnet_channel_matrix()
    deltas = _decay_rates()
    zfeat_l = _filter_features(seq)
    zfeat_c = _filter_features(n_ctx)

    lat_row = lambda i: i // tiles_per_seq
    ctx_row = lambda i: b
    lat_tab = lambda i: i % tiles_per_seq
    ctx_tab = lambda i: 0
    tm_c = min(tm, n_ctx)

    xs = x.reshape(b * seq, d)
    cs = ctx.reshape(b * n_ctx, d)
    for l in range(depth):
        last = l == depth - 1
        mod = lambda j: mods[l, :, j * d:(j + 1) * d].reshape(8, 1, d)
        w_in_bf = w_in[l].astype(BF)
        qg = jnp.tile(q_norm_g[l], N_HEADS)[None, :]
        kg = jnp.tile(k_norm_g[l], N_KV_HEADS)[None, :]
        g1 = norm1_g[l][None, :]
        wa, wh, wf, wo = (w_proj_attn[l].astype(BF), w_proj_hyena[l].astype(BF),
                          w_proj_fnet[l].astype(BF), w_out[l].astype(BF))
        conv_w = hy_conv_w[l].reshape(3, -1)
        conv_b = hy_conv_b[l][None, :]
        w1p = jnp.pad(hy_filt_w1[l], ((0, 64 - hy_filt_w1.shape[1]), (0, 0)))
        filt = (w1p, hy_filt_b1[l][None, :], hy_filt_freq1[l][None, :], hy_filt_w2[l],
                hy_filt_b2[l][None, :], hy_filt_freq2[l][None, :], hy_filt_w3[l], deltas)

        q_c, kd_c, vd_c, u_c, pq_c, gates_c = _phase_a(
            cs, mod(0), mod(1), g1, w_in_bf, cos_c, sin_c, qg, kg, gsum, mfn,
            tm=tm_c, mod_row=ctx_row, tab_row=ctx_tab)
        kd_c3 = kd_c.reshape(b, n_ctx, -1)
        vd_c3 = vd_c.reshape(b, n_ctx, -1)

        q_l, kd_l, vd_l, u_l, pq_l, gates_l = _phase_a(
            xs, mod(0), mod(1), g1, w_in_bf, cos_l, sin_l, qg, kg, gsum, mfn,
            tm=tm, mod_row=lat_row, tab_row=lat_tab)
        attn_l = _attention(attn_sink[l], q_l.reshape(b, seq, -1), kd_l.reshape(b, seq, -1),
                            vd_l.reshape(b, seq, -1), kd_c3, vd_c3, local=True, tq=512)
        h_l, nrm_l = _hy_filter(zfeat_l, *filt, tm=512)
        uc_l = _hy_prep(u_l.reshape(b, seq, -1), conv_w, conv_b)
        hy_l = _hyena(uc_l, h_l, nrm_l, hy_bias[l])
        fn_l = _fnet(pq_l.reshape(b, seq, -1))
        xs = _merge(xs, mod(2), attn_l.reshape(b * seq, -1), hy_l, fn_l, gates_l, wa, wh, wf, wo,
                    tm=tm, mod_row=lat_row)

        if not last:
            attn_c = _attention(attn_sink[l], q_c.reshape(b, n_ctx, -1), kd_c3, vd_c3, kd_c3, vd_c3,
                                local=False, tq=n_ctx)
            h_c, nrm_c = _hy_filter(zfeat_c, *filt, tm=n_ctx)
            uc_c = _hy_prep(u_c.reshape(b, n_ctx, -1), conv_w, conv_b)
            hy_c = _hyena(uc_c, h_c, nrm_c, hy_bias[l])
            fn_c = _fnet(pq_c.reshape(b, n_ctx, -1))
            cs = _merge(cs, mod(2), attn_c.reshape(b * n_ctx, -1), hy_c, fn_c, gates_c, wa, wh, wf, wo,
                        tm=tm_c, mod_row=ctx_row)

        g2 = norm2_g[l][None, :]
        i = l // 2
        if l % 2 == 0:
            wg, wu, wd = ffn_w_gate[i].astype(BF), ffn_w_up[i].astype(BF), ffn_w_down[i].astype(BF)
            run = lambda t2, rows, tmm: _ffn(t2, mod(3), mod(4), mod(5), g2, wg, wu, wd,
                                             tm=tmm, tf=D_FF // 2, mod_row=rows)
        else:
            wr = jnp.pad(moe_router[i], ((0, 0), (0, LANES - N_EXPERTS)))
            wg, wu, wd = moe_w_gate[i].astype(BF), moe_w_up[i].astype(BF), moe_w_down[i].astype(BF)
            run = lambda t2, rows, tmm: _moe(t2, mod(3), mod(4), mod(5), g2, wr, wg, wu, wd,
                                             tb=tmm, mod_row=rows)
        tm_ffn = 512 if l % 2 == 0 else 1024
        xs = run(xs, lambda t: t // (seq // tm_ffn), tm_ffn)
        if not last:
            cs = run(cs, ctx_row, min(tm_ffn, b * n_ctx))
    return xs.reshape(b, seq, d)
```

```python
import functools
import math

import numpy as np
import jax
import jax.numpy as jnp
from jax import lax
from jax.experimental import pallas as pl
from jax.experimental.pallas import tpu as pltpu

F32 = jnp.float32
BF = jnp.bfloat16

D_MODEL = 1024
DEPTH = 4
GRID_W = 64
HEAD_DIM = 64
N_HEADS = 8
N_KV_HEADS = 2
ATTN_WIDTH = N_HEADS * HEAD_DIM
KV_WIDTH = N_KV_HEADS * HEAD_DIM
WINDOW = 128
QBLK = 128
ROPE_THETA = 10000.0
HYENA_ORDER = 2
HYENA_WIDTH = 256
FILTER_BANDS = 16
FILTER_HIDDEN = 64
DECAY_TARGET = 1e-2
FAST_DECAY_PCT = 0.3
SLOW_DECAY_PCT = 1.5
FNET_WIDTH = 256
FNET_GROUP_DIM = 64
Q_END = ATTN_WIDTH
K_END = Q_END + KV_WIDTH
V_END = K_END + KV_WIDTH
HY_END = V_END + (HYENA_ORDER + 1) * HYENA_WIDTH
FN_END = HY_END + FNET_WIDTH
IN_WIDTH = FN_END + 3 * D_MODEL
D_FF = 2816
N_EXPERTS = 8
EPS = 1e-6
LANES = 128
NEG = -1e30
STAGE_ROWS = 16

VMEM_LIMIT = 56 * 1024 * 1024


def _cparams(*sem):
    return pltpu.CompilerParams(dimension_semantics=sem, vmem_limit_bytes=VMEM_LIMIT)


def _dot(a, b):
    return jnp.dot(a, b, preferred_element_type=F32)


def _dot_nt(a, b):
    return lax.dot_general(a, b, (((1,), (1,)), ((), ())), preferred_element_type=F32)


def _split(a):
    hi = a.astype(BF)
    lo = (a - hi.astype(F32)).astype(BF)
    return hi, lo


def _dot3(a, b):
    ah, al = _split(a)
    bh, bl = _split(b)
    return _dot(ah, bh) + (_dot(ah, bl) + _dot(al, bh))


def _dot2(a, b_bf16):
    ah, al = _split(a)
    return _dot(ah, b_bf16) + _dot(al, b_bf16)


def _silu(v):
    return v * jax.nn.sigmoid(v)


def _adaln_kernel(c_ref, w_ref, b_ref, o_ref):
    o_ref[0] = _dot3(_silu(c_ref[...]), w_ref[0]) + b_ref[0]


def _adaln(cond8, w_ada, b_ada):
    depth, d, n6 = w_ada.shape
    tn = 1024
    return pl.pallas_call(
        _adaln_kernel,
        grid=(depth, n6 // tn),
        in_specs=[
            pl.BlockSpec((8, d), lambda l, j: (0, 0)),
            pl.BlockSpec((1, d, tn), lambda l, j: (l, 0, j)),
            pl.BlockSpec((1, 1, tn), lambda l, j: (l, 0, j)),
        ],
        out_specs=pl.BlockSpec((1, 8, tn), lambda l, j: (l, 0, j)),
        out_shape=jax.ShapeDtypeStruct((depth, 8, n6), F32),
        compiler_params=_cparams("parallel", "parallel"),
        name="adaln",
    )(cond8, w_ada, b_ada.reshape(depth, 1, n6))


def _modulated_norm(x, g, sc, sh):
    ms = jnp.mean(x * x, axis=-1, keepdims=True)
    h = (x * lax.rsqrt(ms + EPS)) * g
    return h * (1.0 + sc) + sh


def _phase_a_kernel(x_ref, sh_ref, sc_ref, g_ref, w_ref, cos_ref, sin_ref, qg_ref, kg_ref,
                    gsum_ref, mfn_ref, q_ref, kd_ref, vd_ref, u_ref, pq_ref, gate_ref):
    tm = x_ref.shape[0]
    hb = _modulated_norm(x_ref[...], g_ref[...], sc_ref[0], sh_ref[0]).astype(BF)
    cos = cos_ref[...]
    sin = sin_ref[...]

    def headnorm(t, gain, gs):
        ss = _dot2(t * t, gs)
        return t * lax.rsqrt(ss * (1.0 / HEAD_DIM) + EPS) * gain

    def rope(t, cosw, sinw):
        w = t.shape[1]
        nxt = pltpu.roll(t, w - 16, axis=1)
        prv = pltpu.roll(t, 16, axis=1)
        lw = lax.broadcasted_iota(jnp.int32, t.shape, 1)
        return t * cosw + jnp.where((lw % 32) < 16, nxt, prv) * sinw

    def dup_halves(t):
        lane = lax.broadcasted_iota(jnp.int32, t.shape, 1)
        sw = pltpu.roll(t, 64, axis=1)
        lo = lane < 64
        return jnp.concatenate([jnp.where(lo, t, sw), jnp.where(lo, sw, t)], axis=1)

    pq = _dot(hb, w_ref[:, 0:Q_END])
    qn = headnorm(pq, qg_ref[...], gsum_ref[...])
    cos4 = jnp.concatenate([cos] * 4, axis=1)
    sin4 = jnp.concatenate([sin] * 4, axis=1)
    q_ref[...] = (rope(qn, cos4, sin4) * (HEAD_DIM ** -0.5)).astype(BF)

    pk = _dot(hb, w_ref[:, Q_END:K_END])
    kn = headnorm(pk, kg_ref[...], gsum_ref[0:KV_WIDTH, 0:KV_WIDTH])
    kd_ref[...] = dup_halves(rope(kn, cos, sin)).astype(BF)
    vd_ref[...] = dup_halves(_dot(hb, w_ref[:, K_END:V_END])).astype(BF)

    u_ref[...] = _dot(hb, w_ref[:, V_END:HY_END])
    f = _dot(hb, w_ref[:, HY_END:FN_END])
    pq_ref[...] = _dot(f.astype(BF), mfn_ref[...])
    for i in range(3):
        lo = FN_END + i * D_MODEL
        gate_ref[:, i * D_MODEL:(i + 1) * D_MODEL] = jax.nn.sigmoid(
            _dot(hb, w_ref[:, lo:lo + D_MODEL])).astype(BF)


def _phase_a(x2, sh, sc, g, w_in_bf, cos_t, sin_t, qg, kg, gsum, mfn, *, tm, mod_row, tab_row):
    t, d = x2.shape
    row3 = lambda i: (mod_row(i), 0, 0)
    full = lambda i: (0, 0)
    tok = lambda i: (i, 0)
    outs = [
        ((t, ATTN_WIDTH), BF), ((t, 2 * KV_WIDTH), BF), ((t, 2 * KV_WIDTH), BF),
        ((t, 3 * HYENA_WIDTH), F32), ((t, 2 * FNET_WIDTH), F32), ((t, 3 * D_MODEL), BF),
    ]
    return pl.pallas_call(
        _phase_a_kernel,
        grid=(t // tm,),
        in_specs=[
            pl.BlockSpec((tm, d), tok),
            pl.BlockSpec((1, 1, d), row3),
            pl.BlockSpec((1, 1, d), row3),
            pl.BlockSpec((1, d), full),
            pl.BlockSpec((d, IN_WIDTH), full, pipeline_mode=pl.Buffered(1)),
            pl.BlockSpec((tm, LANES), lambda i: (tab_row(i), 0)),
            pl.BlockSpec((tm, LANES), lambda i: (tab_row(i), 0)),
            pl.BlockSpec((1, ATTN_WIDTH), full),
            pl.BlockSpec((1, KV_WIDTH), full),
            pl.BlockSpec((ATTN_WIDTH, ATTN_WIDTH), full),
            pl.BlockSpec((FNET_WIDTH, 2 * FNET_WIDTH), full),
        ],
        out_specs=[pl.BlockSpec((tm, s[1]), tok) for s, _ in outs],
        out_shape=[jax.ShapeDtypeStruct(s, dt) for s, dt in outs],
        compiler_params=_cparams("parallel"),
        name="phase_a",
    )(x2, sh, sc, g, w_in_bf, cos_t, sin_t, qg, kg, gsum, mfn)


def _attn_kernel(sink_ref, q_ref, kd_ref, vd_ref, kc_ref, vc_ref, o_ref, *, local, seq_len):
    tq = q_ref.shape[1]
    nblk = tq // QBLK
    gq = N_HEADS // N_KV_HEADS
    rows = gq * QBLK
    lane = lax.broadcasted_iota(jnp.int32, (QBLK, LANES), 1)
    lo_half = lane < 64
    hrow = lax.broadcasted_iota(jnp.int32, (rows, 1), 0) // QBLK
    nband = 3 * QBLK
    for blk in range(nblk):
        r0 = blk * QBLK
        qb = q_ref[0, r0:r0 + QBLK, :]
        if local:
            n = pl.program_id(1) * nblk + blk
            start = pl.multiple_of(jnp.clip((n - 1) * QBLK, 0, seq_len - nband), QBLK)
            qpos = n * QBLK + lax.broadcasted_iota(jnp.int32, (rows, nband), 0) % QBLK
            kpos = start + lax.broadcasted_iota(jnp.int32, (rows, nband), 1)
            valid = jnp.abs(qpos - kpos) <= WINDOW
        for g in range(N_KV_HEADS):
            parts = []
            for hh in range(gq):
                h = gq * g + hh
                qc = qb[:, (h // 2) * LANES:(h // 2 + 1) * LANES]
                keep = lo_half if h % 2 == 0 else jnp.logical_not(lo_half)
                parts.append(jnp.where(keep, qc, jnp.zeros_like(qc)))
            q4 = jnp.concatenate(parts, axis=0)
            sk = jnp.full((rows, 1), sink_ref[gq * g + gq - 1], F32)
            for hh in range(gq - 2, -1, -1):
                sk = jnp.where(hrow == hh, sink_ref[gq * g + hh], sk)
            gl = slice(g * LANES, (g + 1) * LANES)
            s_ctx = _dot_nt(q4, kc_ref[0, :, gl])
            m = jnp.maximum(jnp.max(s_ctx, axis=1, keepdims=True), sk)
            if local:
                s_loc = _dot_nt(q4, kd_ref[0, pl.ds(start, nband), gl])
                s_loc = jnp.where(valid, s_loc, NEG)
                m = jnp.maximum(m, jnp.max(s_loc, axis=1, keepdims=True))
            p_ctx = jnp.exp(s_ctx - m)
            den = jnp.sum(p_ctx, axis=1, keepdims=True) + jnp.exp(sk - m)
            o = _dot(p_ctx.astype(BF), vc_ref[0, :, gl])
            if local:
                p_loc = jnp.exp(s_loc - m)
                den = den + jnp.sum(p_loc, axis=1, keepdims=True)
                o = o + _dot(p_loc.astype(BF), vd_ref[0, pl.ds(start, nband), gl])
            o = o / den
            for cc in range(gq // 2):
                col = (gq // 2) * g + cc
                oa = o[(2 * cc) * QBLK:(2 * cc + 1) * QBLK]
                ob = o[(2 * cc + 1) * QBLK:(2 * cc + 2) * QBLK]
                o_ref[0, r0:r0 + QBLK, col * LANES:(col + 1) * LANES] = (
                    jnp.where(lo_half, oa, ob).astype(BF))


def _attention(sink, q, kd, vd, kc, vc, *, local, tq):
    b, lq, _ = q.shape
    lk = kd.shape[1]
    c = kc.shape[1]
    kern = functools.partial(_attn_kernel, local=local, seq_len=lk)
    return pl.pallas_call(
        kern,
        grid=(b, lq // tq),
        in_specs=[
            pl.BlockSpec(memory_space=pltpu.SMEM),
            pl.BlockSpec((1, tq, ATTN_WIDTH), lambda bi, i: (bi, i, 0)),
            pl.BlockSpec((1, lk, 2 * KV_WIDTH), lambda bi, i: (bi, 0, 0)),
            pl.BlockSpec((1, lk, 2 * KV_WIDTH), lambda bi, i: (bi, 0, 0)),
            pl.BlockSpec((1, c, 2 * KV_WIDTH), lambda bi, i: (bi, 0, 0)),
            pl.BlockSpec((1, c, 2 * KV_WIDTH), lambda bi, i: (bi, 0, 0)),
        ],
        out_specs=pl.BlockSpec((1, tq, ATTN_WIDTH), lambda bi, i: (bi, i, 0)),
        out_shape=jax.ShapeDtypeStruct((b, lq, ATTN_WIDTH), BF),
        compiler_params=_cparams("parallel", "parallel"),
        name="attn_local" if local else "attn_ctx",
    )(sink, q, kd, vd, kc, vc)


def _hy_prep_kernel(u_ref, w_ref, b_ref, o_ref):
    u = u_ref[0]
    n = u.shape[0]
    row = lax.broadcasted_iota(jnp.int32, u.shape, 0)
    prv = jnp.where(row == 0, 0.0, pltpu.roll(u, 1, axis=0))
    nxt = jnp.where(row == n - 1, 0.0, pltpu.roll(u, n - 1, axis=0))
    o_ref[0] = prv * w_ref[0:1, :] + u * w_ref[1:2, :] + nxt * w_ref[2:3, :] + b_ref[...]


def _hy_prep(u, conv_w, conv_b):
    b, n, w = u.shape
    return pl.pallas_call(
        _hy_prep_kernel,
        grid=(b, w // LANES),
        in_specs=[
            pl.BlockSpec((1, n, LANES), lambda bi, j: (bi, 0, j)),
            pl.BlockSpec((3, LANES), lambda bi, j: (0, j)),
            pl.BlockSpec((1, LANES), lambda bi, j: (0, j)),
        ],
        out_specs=pl.BlockSpec((1, n, LANES), lambda bi, j: (bi, 0, j)),
        out_shape=jax.ShapeDtypeStruct((b, n, w), F32),
        compiler_params=_cparams("parallel", "parallel"),
        name="hy_prep",
    )(u, conv_w, conv_b)


def _hy_hidden_kernel(z_ref, w1_ref, b1_ref, f1_ref, w2_ref, b2_ref, f2_ref, h_ref):
    h = jnp.sin(f1_ref[...] * (_dot3(z_ref[...], w1_ref[...]) + b1_ref[...]))
    h_ref[...] = jnp.sin(f2_ref[...] * (_dot3(h, w2_ref[...]) + b2_ref[...]))


def _hy_taps_kernel(h_ref, w3_ref, dl_ref, k_ref, nrm_ref, *, n_half):
    i = pl.program_id(0)
    tm = h_ref.shape[0]
    hid = h_ref[...]
    k = _dot3(hid, w3_ref[...]) * jnp.exp(-hid[:, FILTER_HIDDEN:FILTER_HIDDEN + 1] * dl_ref[...])
    row = i * tm + lax.broadcasted_iota(jnp.int32, (tm, 1), 0)
    k = jnp.where(row == n_half, 0.0, k)
    k_ref[...] = k

    @pl.when(i == 0)
    def _():
        nrm_ref[...] = jnp.zeros_like(nrm_ref)

    nrm_ref[...] += jnp.sum(jnp.abs(k), axis=0, keepdims=True)


def _hy_filter(zfeat, w1p, b1, f1, w2, b2, f2, w3, deltas2, *, tm):
    n = zfeat.shape[0]
    full = lambda i: (0, 0)
    hidden = pl.pallas_call(
        _hy_hidden_kernel,
        grid=(n // tm,),
        in_specs=[
            pl.BlockSpec((tm, zfeat.shape[1]), lambda i: (i, 0)),
            pl.BlockSpec(w1p.shape, full), pl.BlockSpec(b1.shape, full), pl.BlockSpec(f1.shape, full),
            pl.BlockSpec(w2.shape, full), pl.BlockSpec(b2.shape, full), pl.BlockSpec(f2.shape, full),
        ],
        out_specs=pl.BlockSpec((tm, FILTER_HIDDEN), lambda i: (i, 0)),
        out_shape=jax.ShapeDtypeStruct((n, FILTER_HIDDEN), F32),
        compiler_params=_cparams("parallel"),
        name="hy_hidden",
    )(zfeat, w1p, b1, f1, w2, b2, f2)
    hid_t = jnp.concatenate([hidden, zfeat[:, 0:1]], axis=1)
    hid_t = jnp.concatenate([hid_t, jnp.zeros((1, hid_t.shape[1]), F32), hid_t[:0:-1]], axis=0)
    hid_t = jnp.pad(hid_t, ((0, 0), (0, LANES - hid_t.shape[1])))
    w3p = jnp.pad(w3, ((0, LANES - w3.shape[0]), (0, 0)))
    wout = w3.shape[1] // 2
    kern = functools.partial(_hy_taps_kernel, n_half=n)
    return pl.pallas_call(
        kern,
        grid=(2 * n // tm,),
        in_specs=[
            pl.BlockSpec((tm, LANES), lambda i: (i, 0)),
            pl.BlockSpec((LANES, wout), lambda i: (0, (i * tm) // n)),
            pl.BlockSpec(deltas2.shape, full),
        ],
        out_specs=[pl.BlockSpec((tm, wout), lambda i: (i, 0)), pl.BlockSpec((1, wout), full)],
        out_shape=[jax.ShapeDtypeStruct((2 * n, wout), F32), jax.ShapeDtypeStruct((1, wout), F32)],
        compiler_params=_cparams("arbitrary"),
        name="hy_taps",
    )(hid_t, w3p, deltas2)


def _dft_mats(k_out, r_in, period, sign, scale, n_in, real_out):
    k = np.arange(k_out)[:, None]
    r = np.arange(r_in)[None, :]
    ang = 2.0 * np.pi * ((k * r) % period) / period
    fr = np.cos(ang) * scale
    fi = sign * np.sin(ang) * scale
    if real_out:
        mats = [fr, -fi]
    else:
        mats = [np.concatenate([fr, fi], 0), np.concatenate([-fi, fr], 0)]
    return jnp.asarray(np.stack(mats[:n_in], 0), dtype=F32).astype(BF)


def _twiddle(s_n, k_n, n, sign, sbk):
    s0 = lax.broadcasted_iota(jnp.int32, (s_n // sbk, k_n, LANES), 0) * sbk
    k = lax.broadcasted_iota(jnp.int32, (s_n // sbk, k_n, LANES), 1)
    ang = (s0 * k).astype(F32) * (2.0 * math.pi / n)
    ang1 = lax.broadcasted_iota(jnp.int32, (k_n, LANES), 0).astype(F32) * (2.0 * math.pi / n)
    return jnp.cos(ang), sign * jnp.sin(ang), jnp.cos(ang1), sign * jnp.sin(ang1)


def _stage_kernel(*refs, n_in, r_in, k_mid, k_out, sbk, tw, spec, second, gate, real_out,
                  transposed_out, flat):
    it = iter(refs)
    x_refs = [next(it) for _ in range(n_in)]
    g_ref = next(it)
    g2_ref = next(it) if second else None
    tw_refs = [next(it) for _ in range(4)] if tw else None
    spec_refs = [next(it) for _ in range(3)] if spec else None
    gate_refs = [next(it) for _ in range(5)] if gate else None
    out_refs = [next(it)] if (real_out or gate) else [next(it), next(it)]
    if not flat:
        x_refs = [r.reshape(r_in * sbk, LANES) for r in x_refs]
        if spec:
            spec_refs = [r.reshape(k_mid * sbk, LANES) for r in spec_refs[:2]] + spec_refs[2:]
        if gate:
            gate_refs = [r.reshape(k_out * sbk, LANES) for r in gate_refs[:4]] + gate_refs[4:]
            out_refs = [out_refs[0].reshape(2 * k_out * sbk, LANES)]
        elif not transposed_out:
            out_refs = [r.reshape(k_out * sbk, LANES) for r in out_refs]
    if spec:
        inv = 1.0 / spec_refs[2][...]
    if tw:
        tr, ti = tw_refs[0][0], tw_refs[1][0]
        wr, wi = tw_refs[2][...], tw_refs[3][...]
    for j in range(sbk):
        acc = None
        for xi, x_ref in enumerate(x_refs):
            xv = x_ref[...] if flat else x_ref[pl.ds(j, r_in, stride=sbk), :]
            d = _dot(g_ref[xi], xv.astype(BF))
            acc = d if acc is None else acc + d
        if real_out:
            ys = [acc]
        else:
            yr, yi = acc[:k_mid], acc[k_mid:]
            if spec:
                rows = slice(None) if flat else pl.ds(j, k_mid, stride=sbk)
                sr = spec_refs[0][rows, :] * inv
                si = spec_refs[1][rows, :] * inv
                yr, yi = yr * sr - yi * si, yr * si + yi * sr
            if second:
                acc = _dot(g2_ref[0], yr.astype(BF)) + _dot(g2_ref[1], yi.astype(BF))
                yr, yi = acc[:k_out], acc[k_out:]
            if tw:
                yr, yi = yr * tr - yi * ti, yr * ti + yi * tr
                if j + 1 < sbk:
                    tr, ti = tr * wr - ti * wi, tr * wi + ti * wr
            ys = [yr, yi]
        if gate:
            o_ref = out_refs[0]
            for part, y in enumerate(ys):
                rows = slice(None) if flat else pl.ds(j, k_out, stride=sbk)
                val = gate_refs[part][rows, :] * (y + gate_refs[4][...] * gate_refs[2 + part][rows, :])
                if flat:
                    o_ref[part] = val.astype(o_ref.dtype)
                else:
                    o_ref[pl.ds(part * k_out * sbk + j, k_out, stride=sbk), :] = val.astype(o_ref.dtype)
            continue
        for o_ref, y in zip(out_refs, ys):
            if flat:
                o_ref[...] = y.astype(o_ref.dtype)
            elif transposed_out:
                o_ref[0, j] = y.astype(o_ref.dtype)
            else:
                o_ref[pl.ds(j, k_out, stride=sbk), :] = y.astype(o_ref.dtype)


def _fft_stage(xs, x_sel, gmat, *, r_in, s_n, k_out, n_groups, n_cblk, transposed_out, real_out,
               out_dtype=F32, g2mat=None, tw=None, spec=None, spec_sel=None, gate=None, sbk=STAGE_ROWS,
               name="fft_stage"):
    n_in = len(xs)
    flat = s_n == 1
    sbk = 1 if flat else min(sbk, s_n)
    cb = LANES
    in_specs, args = [], []
    for x, sel in zip(xs, x_sel):
        if flat:
            in_specs.append(pl.BlockSpec((None, r_in, cb), lambda s, g, c, sel=sel: (sel(g, c)[0], 0, sel(g, c)[1])))
            args.append(x)
        else:
            xv = x.reshape(x.shape[0], x.shape[1] // s_n, s_n, x.shape[2])
            in_specs.append(pl.BlockSpec((1, r_in, sbk, cb),
                                         lambda s, g, c, sel=sel: (sel(g, c)[0], 0, s, sel(g, c)[1])))
            args.append(xv)
    in_specs.append(pl.BlockSpec(gmat.shape, lambda s, g, c: (0, 0, 0)))
    args.append(gmat)
    k_mid = gmat.shape[1] // (1 if real_out else 2)
    if g2mat is not None:
        in_specs.append(pl.BlockSpec(g2mat.shape, lambda s, g, c: (0, 0, 0)))
        args.append(g2mat)
    if tw is not None:
        for tarr in tw[:2]:
            in_specs.append(pl.BlockSpec((1, k_out, LANES), lambda s, g, c: (s, 0, 0)))
            args.append(tarr)
        for tarr in tw[2:]:
            in_specs.append(pl.BlockSpec((k_out, LANES), lambda s, g, c: (0, 0)))
            args.append(tarr)
    if spec is not None:
        kr, ki, nrm = spec
        for arr in (kr, ki):
            if flat:
                in_specs.append(pl.BlockSpec((k_mid, cb), lambda s, g, c: (0, spec_sel(g, c))))
                args.append(arr)
            else:
                in_specs.append(pl.BlockSpec((1, k_mid, sbk, cb), lambda s, g, c: (0, 0, s, spec_sel(g, c))))
                args.append(arr.reshape(1, k_mid, s_n, arr.shape[-1]))
        in_specs.append(pl.BlockSpec((1, cb), lambda s, g, c: (0, spec_sel(g, c))))
        args.append(nrm)
    if gate is not None:
        (ga, gblk), (za, zblk), bias = gate
        for arr, blk in ((ga, gblk), (za, zblk)):
            for bi in (0, 1):
                if flat:
                    in_specs.append(pl.BlockSpec((None, k_out, cb), lambda s, g, c, bi=bi, blk=blk: (bi, 0, blk + c)))
                    args.append(arr)
                else:
                    in_specs.append(pl.BlockSpec((1, k_out, sbk, cb),
                                                 lambda s, g, c, bi=bi, blk=blk: (bi, 0, s, blk + c)))
                    args.append(arr.reshape(arr.shape[0], k_out, s_n, arr.shape[-1]))
        in_specs.append(pl.BlockSpec((1, cb), lambda s, g, c: (0, c)))
        args.append(bias)
    ctot = n_cblk * cb
    if gate is not None:
        n_groups = 2
        if flat:
            oshape = (2, k_out, ctot)
            ospec = pl.BlockSpec((2, k_out, cb), lambda s, g, c: (0, 0, c))
        else:
            oshape = (2, k_out, s_n, ctot)
            ospec = pl.BlockSpec((2, k_out, sbk, cb), lambda s, g, c: (0, 0, s, c))
    elif flat:
        oshape = (n_groups, k_out, ctot)
        ospec = pl.BlockSpec((None, k_out, cb), lambda s, g, c: (g, 0, c))
    elif transposed_out:
        oshape = (n_groups, s_n, k_out, ctot)
        ospec = pl.BlockSpec((1, sbk, k_out, cb), lambda s, g, c: (g, s, 0, c))
    else:
        oshape = (n_groups, k_out, s_n, ctot)
        ospec = pl.BlockSpec((1, k_out, sbk, cb), lambda s, g, c: (g, 0, s, c))
    n_out = 1 if (real_out or gate is not None) else 2
    kern = functools.partial(_stage_kernel, n_in=n_in, r_in=r_in, k_mid=k_mid, k_out=k_out, sbk=sbk,
                             tw=tw is not None, spec=spec is not None, second=g2mat is not None,
                             gate=gate is not None, real_out=real_out, transposed_out=transposed_out,
                             flat=flat)
    n_grid_groups = 1 if gate is not None else n_groups
    outs = pl.pallas_call(
        kern,
        grid=(s_n // sbk, n_grid_groups, n_cblk),
        in_specs=in_specs,
        out_specs=[ospec] * n_out,
        out_shape=[jax.ShapeDtypeStruct(oshape, out_dtype)] * n_out,
        compiler_params=_cparams("parallel", "parallel", "parallel"),
        name=name,
    )(*args)
    return [o.reshape(n_groups, -1, ctot) for o in outs]


def _split_len(n):
    if n <= 1024:
        return n, 1
    s = 128
    return n // s, s


def _fft_forward(xs, x_sel, n, n_rows, *, n_groups, n_cblk, name="fwd"):
    n1, s = _split_len(n)
    n_in = len(xs)
    if s == 1:
        g = _dft_mats(n, n_rows, n, -1.0, 1.0, n_in, False)
        return _fft_stage(xs, x_sel, g, r_in=n_rows, s_n=1, k_out=n, n_groups=n_groups, n_cblk=n_cblk,
                          transposed_out=False, real_out=False, name=name + "_direct")
    r1 = n_rows // s
    g1 = _dft_mats(n1, r1, n1, -1.0, 1.0, n_in, False)
    tw = _twiddle(s, n1, n, -1.0, STAGE_ROWS)
    ar, ai = _fft_stage(xs, x_sel, g1, r_in=r1, s_n=s, k_out=n1, n_groups=n_groups, n_cblk=n_cblk,
                        transposed_out=True, real_out=False, tw=tw, name=name + "_s1")
    g2 = _dft_mats(s, s, s, -1.0, 1.0, 2, False)
    ident = lambda g, c: (g, c)
    return _fft_stage([ar, ai], [ident, ident], g2, r_in=s, s_n=n1, k_out=s, n_groups=n_groups,
                      n_cblk=n_cblk, transposed_out=False, real_out=False, name=name + "_s2")


def _hyena(uc, k_filt, nrm, hy_bias):
    b, n, _ = uc.shape
    w = HYENA_WIDTH
    wblk = w // LANES
    nfft = 2 * n
    n1, s = _split_len(nfft)
    ident = lambda g, c: (g, c)
    kr, ki = _fft_forward([k_filt[None]], [lambda g, c: (0, c)], nfft, nfft, n_groups=1,
                          n_cblk=HYENA_ORDER * wblk, name="hy_filt_fft")
    z, zblk = uc, 2 * wblk
    for o in range(HYENA_ORDER):
        sel_r = lambda g, c, zblk=zblk: (0, zblk + c)
        sel_i = lambda g, c, zblk=zblk: (1, zblk + c)
        spec = (kr[0], ki[0], nrm)
        spec_sel = lambda g, c, o=o: o * wblk + c
        gate = ((uc, o * wblk), (z, zblk), hy_bias[o:o + 1])
        common = dict(n_groups=1, n_cblk=wblk, real_out=False)
        if s == 1:
            gf = _dft_mats(nfft, n, nfft, -1.0, 1.0, 2, False)
            gi = _dft_mats(n, nfft, nfft, 1.0, 1.0 / nfft, 2, False)
            (z,) = _fft_stage([z, z], [sel_r, sel_i], gf, r_in=n, s_n=1, k_out=n, transposed_out=False,
                              g2mat=gi, spec=spec, spec_sel=spec_sel, gate=gate, name="hy_direct", **common)
        else:
            r1 = n // s
            g1 = _dft_mats(n1, r1, n1, -1.0, 1.0, 2, False)
            ar, ai = _fft_stage([z, z], [sel_r, sel_i], g1, r_in=r1, s_n=s, k_out=n1, transposed_out=True,
                                tw=_twiddle(s, n1, nfft, -1.0, STAGE_ROWS), name="hy_s1", **common)
            g2 = _dft_mats(s, s, s, -1.0, 1.0, 2, False)
            g3 = _dft_mats(s, s, s, 1.0, 1.0, 2, False)
            qr, qi = _fft_stage([ar, ai], [ident, ident], g2, r_in=s, s_n=n1, k_out=s, transposed_out=True,
                                g2mat=g3, tw=_twiddle(n1, s, nfft, 1.0, STAGE_ROWS), spec=spec,
                                spec_sel=spec_sel, name="hy_mid", **common)
            g4 = _dft_mats(n // s, n1, n1, 1.0, 1.0 / nfft, 2, False)
            (z,) = _fft_stage([qr, qi], [ident, ident], g4, r_in=n1, s_n=s, k_out=n // s,
                              transposed_out=False, gate=gate, name="hy_last", **common)
        zblk = 0
    return z.reshape(b * n, w)


def _fnet(pq):
    b, n, _ = pq.shape
    w = FNET_WIDTH
    wblk = w // LANES
    scale = 1.0 / math.sqrt(n * FNET_GROUP_DIM)
    sel_r = lambda g, c: (g, c)
    sel_i = lambda g, c: (g, wblk + c)
    ident = lambda g, c: (g, c)
    n1, s = _split_len(n)
    if s == 1:
        g = _dft_mats(n, n, n, -1.0, scale, 2, True)
        (y,) = _fft_stage([pq, pq], [sel_r, sel_i], g, r_in=n, s_n=1, k_out=n, n_groups=b, n_cblk=wblk,
                          transposed_out=False, real_out=True, name="fnet_direct")
        return y.reshape(b * n, w)
    g1 = _dft_mats(n1, n1, n1, -1.0, 1.0, 2, False)
    tw = _twiddle(s, n1, n, -1.0, STAGE_ROWS)
    ar, ai = _fft_stage([pq, pq], [sel_r, sel_i], g1, r_in=n1, s_n=s, k_out=n1, n_groups=b, n_cblk=wblk,
                        transposed_out=True, real_out=False, tw=tw, name="fnet_s1")
    g2 = _dft_mats(s, s, s, -1.0, scale, 2, True)
    (y,) = _fft_stage([ar, ai], [ident, ident], g2, r_in=s, s_n=n1, k_out=s, n_groups=b, n_cblk=wblk,
                      transposed_out=False, real_out=True, name="fnet_s2")
    return y.reshape(b * n, w)


def _merge_kernel(x_ref, gt_ref, a_ref, h_ref, f_ref, gate_ref, wa_ref, wh_ref, wf_ref, wo_ref, o_ref):
    d = D_MODEL
    m = gate_ref[:, 0:d].astype(F32) * _dot(a_ref[...], wa_ref[...])
    m = m + gate_ref[:, d:2 * d].astype(F32) * _dot(h_ref[...].astype(BF), wh_ref[...])
    m = m + gate_ref[:, 2 * d:3 * d].astype(F32) * _dot(f_ref[...].astype(BF), wf_ref[...])
    y = _dot(m.astype(BF), wo_ref[...])
    o_ref[...] = x_ref[...] + gt_ref[0] * y


def _merge(x2, gt, attn_o, hy_o, fn_o, gates, wa, wh, wf, wo, *, tm, mod_row):
    t, d = x2.shape
    tok = lambda i: (i, 0)
    full = lambda i: (0, 0)
    return pl.pallas_call(
        _merge_kernel,
        grid=(t // tm,),
        in_specs=[
            pl.BlockSpec((tm, d), tok),
            pl.BlockSpec((1, 1, d), lambda i: (mod_row(i), 0, 0)),
            pl.BlockSpec((tm, ATTN_WIDTH), tok),
            pl.BlockSpec((tm, HYENA_WIDTH), tok),
            pl.BlockSpec((tm, FNET_WIDTH), tok),
            pl.BlockSpec((tm, 3 * d), tok),
            pl.BlockSpec(wa.shape, full, pipeline_mode=pl.Buffered(1)),
            pl.BlockSpec(wh.shape, full, pipeline_mode=pl.Buffered(1)),
            pl.BlockSpec(wf.shape, full, pipeline_mode=pl.Buffered(1)),
            pl.BlockSpec(wo.shape, full, pipeline_mode=pl.Buffered(1)),
        ],
        out_specs=pl.BlockSpec((tm, d), tok),
        out_shape=jax.ShapeDtypeStruct((t, d), F32),
        compiler_params=_cparams("parallel"),
        name="merge",
    )(x2, gt, attn_o, hy_o, fn_o, gates, wa, wh, wf, wo)


def _ffn_kernel(x_ref, sh_ref, sc_ref, gt_ref, g_ref, wg_ref, wu_ref, wd_ref, o_ref, h_scr, acc_scr):
    f = pl.program_id(1)

    @pl.when(f == 0)
    def _():
        h_scr[...] = _modulated_norm(x_ref[...], g_ref[...], sc_ref[0], sh_ref[0]).astype(BF)
        acc_scr[...] = jnp.zeros_like(acc_scr)

    hb = h_scr[...]
    act = _silu(_dot(hb, wg_ref[...])) * _dot(hb, wu_ref[...])
    acc_scr[...] += _dot(act.astype(BF), wd_ref[...])

    @pl.when(f == pl.num_programs(1) - 1)
    def _():
        o_ref[...] = x_ref[...] + gt_ref[0] * acc_scr[...]


def _ffn(x2, sh, sc, gt, g, wg, wu, wd, *, tm, tf, mod_row):
    t, d = x2.shape
    ff = wg.shape[1]
    row3 = lambda i, f: (mod_row(i), 0, 0)
    return pl.pallas_call(
        _ffn_kernel,
        grid=(t // tm, ff // tf),
        in_specs=[
            pl.BlockSpec((tm, d), lambda i, f: (i, 0)),
            pl.BlockSpec((1, 1, d), row3), pl.BlockSpec((1, 1, d), row3), pl.BlockSpec((1, 1, d), row3),
            pl.BlockSpec((1, d), lambda i, f: (0, 0)),
            pl.BlockSpec((d, tf), lambda i, f: (0, f)),
            pl.BlockSpec((d, tf), lambda i, f: (0, f)),
            pl.BlockSpec((tf, d), lambda i, f: (f, 0)),
        ],
        out_specs=pl.BlockSpec((tm, d), lambda i, f: (i, 0)),
        out_shape=jax.ShapeDtypeStruct((t, d), F32),
        scratch_shapes=[pltpu.VMEM((tm, d), BF), pltpu.VMEM((tm, d), F32)],
        compiler_params=_cparams("parallel", "arbitrary"),
        name="ffn_dense",
    )(x2, sh, sc, gt, g, wg, wu, wd)


def _top2(logits):
    lane = lax.broadcasted_iota(jnp.int32, logits.shape, 1)
    lg = jnp.where(lane < N_EXPERTS, logits, -jnp.inf)
    m1 = jnp.max(lg, axis=1, keepdims=True)
    i1 = jnp.min(jnp.where(lg == m1, lane, LANES), axis=1, keepdims=True)
    lg2 = jnp.where(lane == i1, -jnp.inf, lg)
    m2 = jnp.max(lg2, axis=1, keepdims=True)
    i2 = jnp.min(jnp.where(lg2 == m2, lane, LANES), axis=1, keepdims=True)
    e2 = jnp.exp(m2 - m1)
    w1 = 1.0 / (1.0 + e2)
    return i1, i2, w1, e2 * w1


GROUP_TILE = 256
GROUP_PAD = 128
SLOT_RADIX = 64.0


def _moe_group_kernel(x_ref, sh_ref, sc_ref, g_ref, wr_ref, xg_ref, ws_ref, slot_ref, cnt_ref,
                      h_scr, rows_scr, wm_scr):
    j = pl.program_id(1)
    tb = x_ref.shape[0]
    gt_rows = xg_ref.shape[1]

    @pl.when(j == 0)
    def _():
        h = _modulated_norm(x_ref[...], g_ref[...], sc_ref[0], sh_ref[0])
        h_scr[...] = h.astype(BF)
        i1, i2, w1, w2 = _top2(_dot3(h, wr_ref[...]))
        lane = lax.broadcasted_iota(jnp.int32, (tb, LANES), 1)
        oh0 = jnp.where(lane == i1, 1.0, 0.0)
        oh1 = jnp.where(lane == i2, 1.0, 0.0)
        c0 = jnp.sum(oh0, axis=0, keepdims=True)
        cnt = c0 + jnp.sum(oh1, axis=0, keepdims=True)
        tri = jnp.where(lax.broadcasted_iota(jnp.int32, (tb, tb), 1)
                        < lax.broadcasted_iota(jnp.int32, (tb, tb), 0), 1.0, 0.0).astype(BF)
        pre0 = _dot(tri, oh0.astype(BF))
        pre1 = _dot(tri, oh1.astype(BF)) + c0
        tiles = jnp.ceil(cnt * (1.0 / GROUP_PAD))
        upper = jnp.where(lax.broadcasted_iota(jnp.int32, (LANES, LANES), 0)
                          < lax.broadcasted_iota(jnp.int32, (LANES, LANES), 1), 1.0, 0.0).astype(BF)
        off = _dot(jnp.broadcast_to(tiles, (8, LANES)).astype(BF), upper)[0:1] * float(GROUP_PAD)
        slot0 = jnp.sum(oh0 * (off + pre0), axis=1, keepdims=True)
        slot1 = jnp.sum(oh1 * (off + pre1), axis=1, keepdims=True)
        slot_ref[0] = jnp.where(lane == 0, slot0, jnp.where(lane == 1, slot1, 0.0))
        cnt_ref[0] = jnp.broadcast_to(cnt, (8, LANES))
        hi0 = jnp.floor(slot0 * (1.0 / SLOT_RADIX))
        hi1 = jnp.floor(slot1 * (1.0 / SLOT_RADIX))
        digits = jnp.where(lane == 0, hi0, jnp.where(lane == 1, slot0 - SLOT_RADIX * hi0,
                           jnp.where(lane == 2, hi1, jnp.where(lane == 3, slot1 - SLOT_RADIX * hi1, 0.0))))
        sel = jnp.where(lax.broadcasted_iota(jnp.int32, (8, LANES), 0)
                        == lax.broadcasted_iota(jnp.int32, (8, LANES), 1), 1.0, 0.0).astype(BF)
        rows_scr[...] = _dot_nt(sel, digits.astype(BF))
        w1h, w1l = _split(w1)
        w1m, w1l = _split(w1 - w1h.astype(F32))
        w2h, w2l = _split(w2)
        w2m, w2l = _split(w2 - w2h.astype(F32))
        cols = [w1h, w1m, w1l, w2h, w2m, w2l]
        wm = jnp.zeros((tb, LANES), F32)
        for li, col in enumerate(cols):
            wm = jnp.where(lane == li, col.astype(F32), wm)
        wm_scr[...] = wm.astype(BF)

    rows = rows_scr[...]
    s0 = rows[0:1] * SLOT_RADIX + rows[1:2]
    s1 = rows[2:3] * SLOT_RADIX + rows[3:4]
    pos = (lax.broadcasted_iota(jnp.int32, (gt_rows, tb), 0) + j * gt_rows).astype(F32)
    g0 = jnp.where(pos == s0, 1.0, 0.0).astype(BF)
    g1 = jnp.where(pos == s1, 1.0, 0.0).astype(BF)
    xg_ref[0] = _dot(g0 + g1, h_scr[...]).astype(BF)
    lane_w = lax.broadcasted_iota(jnp.int32, (gt_rows, LANES), 1)
    wsum = (jnp.where(lane_w < 3, _dot(g0, wm_scr[...]), 0.0)
            + jnp.where((lane_w >= 3) & (lane_w < 6), _dot(g1, wm_scr[...]), 0.0))
    ws_ref[0] = jnp.broadcast_to(jnp.sum(wsum, axis=1, keepdims=True), (gt_rows, LANES))


def _moe_group(x2, sh, sc, g, wr_pad, *, tb, nt, mod_row):
    t, d = x2.shape
    nb = t // tb
    row3 = lambda b, j: (mod_row(b), 0, 0)
    return pl.pallas_call(
        _moe_group_kernel,
        grid=(nb, nt),
        in_specs=[
            pl.BlockSpec((tb, d), lambda b, j: (b, 0)),
            pl.BlockSpec((1, 1, d), row3), pl.BlockSpec((1, 1, d), row3),
            pl.BlockSpec((1, d), lambda b, j: (0, 0)),
            pl.BlockSpec((d, LANES), lambda b, j: (0, 0)),
        ],
        out_specs=[
            pl.BlockSpec((1, GROUP_TILE, d), lambda b, j: (b * nt + j, 0, 0)),
            pl.BlockSpec((1, GROUP_TILE, LANES), lambda b, j: (b * nt + j, 0, 0)),
            pl.BlockSpec((1, tb, LANES), lambda b, j: (b, 0, 0)),
            pl.BlockSpec((1, 8, LANES), lambda b, j: (b, 0, 0)),
        ],
        out_shape=[
            jax.ShapeDtypeStruct((nb * nt, GROUP_TILE, d), BF),
            jax.ShapeDtypeStruct((nb * nt, GROUP_TILE, LANES), F32),
            jax.ShapeDtypeStruct((nb, tb, LANES), F32),
            jax.ShapeDtypeStruct((nb, 8, LANES), F32),
        ],
        scratch_shapes=[pltpu.VMEM((tb, d), BF), pltpu.VMEM((8, tb), F32), pltpu.VMEM((tb, LANES), BF)],
        compiler_params=_cparams("parallel", "arbitrary"),
        name="moe_group",
    )(x2, sh, sc, g, wr_pad)


def _moe_schedule(cnt, nh):
    h = (cnt + GROUP_PAD - 1) // GROUP_PAD
    nb, ne = h.shape
    tot = h.sum(0)
    pairs = (tot + 1) // 2
    cum_p = jnp.cumsum(pairs)
    start_p = cum_p - pairs
    n_used = cum_p[-1]
    n_steps = (nb * nh + ne) // 2
    q = jnp.minimum(jnp.arange(n_steps, dtype=jnp.int32), n_used - 1)
    e = jnp.sum(q[:, None] >= cum_p[None, :], axis=1).astype(jnp.int32)
    r = q - start_p[e]
    cum_b = jnp.cumsum(h, axis=0)
    first = jnp.cumsum(h, axis=1) - h

    def piece(idx):
        idx = jnp.minimum(idx, tot[e] - 1)
        blk = jnp.sum(idx[:, None] >= cum_b.T[e], axis=1).astype(jnp.int32)
        return blk * nh + first[blk, e] + idx - (cum_b[blk, e] - h[blk, e])

    x = jnp.arange(nh, dtype=jnp.int32)[None, :]
    ex = jnp.sum(x[:, :, None] >= jnp.cumsum(h, axis=1)[:, None, :], axis=2).astype(jnp.int32)
    exc = jnp.minimum(ex, ne - 1)
    g = jnp.take_along_axis(cum_b - h, exc, axis=1) + x - jnp.take_along_axis(first, exc, axis=1)
    loc = 2 * (start_p[exc] + g // 2) + g % 2
    loc = jnp.where(ex < ne, loc, loc[:, 0:1])
    return (piece(2 * r).astype(jnp.int32), piece(2 * r + 1).astype(jnp.int32), e,
            n_used.astype(jnp.int32).reshape(1), loc.reshape(-1).astype(jnp.int32),
            ((h.sum(1) + 1) // 2).astype(jnp.int32))


def _moe_expert_kernel(pa_ref, pb_ref, exp_ref, nused_ref, xa_ref, xb_ref, wa_ref, wb_ref,
                       wg_ref, wu_ref, wd_ref, y_ref):
    @pl.when(pl.program_id(0) < nused_ref[0])
    def _():
        x = jnp.concatenate([xa_ref[0], xb_ref[0]], axis=0)
        act = _silu(_dot(x, wg_ref[0])) * _dot(x, wu_ref[0])
        y = _dot(act.astype(BF), wd_ref[0])
        w = jnp.concatenate([wa_ref[0], wb_ref[0]], axis=0)
        y_ref[0] = (y * jnp.concatenate([w] * (y.shape[1] // LANES), axis=1)).astype(BF)


def _moe_experts(piece_a, piece_b, step_exp, n_used, xg, ws, wg, wu, wd):
    d = xg.shape[-1]
    ff = wg.shape[2]
    n_steps = piece_a.shape[0]
    xh = xg.reshape(-1, GROUP_PAD, d)
    wh = ws.reshape(-1, GROUP_PAD, LANES)
    pa3 = lambda i, pa, pb, se, nu: (pa[i], 0, 0)
    pb3 = lambda i, pa, pb, se, nu: (pb[i], 0, 0)
    exp3 = lambda i, pa, pb, se, nu: (se[i], 0, 0)
    return pl.pallas_call(
        _moe_expert_kernel,
        grid_spec=pltpu.PrefetchScalarGridSpec(
            num_scalar_prefetch=4,
            grid=(n_steps,),
            in_specs=[
                pl.BlockSpec((1, GROUP_PAD, d), pa3),
                pl.BlockSpec((1, GROUP_PAD, d), pb3),
                pl.BlockSpec((1, GROUP_PAD, LANES), pa3),
                pl.BlockSpec((1, GROUP_PAD, LANES), pb3),
                pl.BlockSpec((1, d, ff), exp3),
                pl.BlockSpec((1, d, ff), exp3),
                pl.BlockSpec((1, ff, d), exp3),
            ],
            out_specs=pl.BlockSpec((1, 2 * GROUP_PAD, d),
                                   lambda i, pa, pb, se, nu: (jnp.minimum(i, nu[0] - 1), 0, 0)),
        ),
        out_shape=jax.ShapeDtypeStruct((n_steps, 2 * GROUP_PAD, d), BF),
        compiler_params=_cparams("arbitrary"),
        name="moe_experts",
    )(piece_a, piece_b, step_exp, n_used, xh, xh, wh, wh, wg, wu, wd)


def _moe_combine_kernel(nt_ref, loc_ref, x_ref, gt_ref, slot_ref, ya_ref, yb_ref, o_ref, acc_scr):
    b = pl.program_id(0)
    j = pl.program_id(1)
    tb = x_ref.shape[0]
    gt_rows = 2 * ya_ref.shape[1]

    @pl.when(j == 0)
    def _():
        acc_scr[...] = jnp.zeros_like(acc_scr)

    @pl.when(j < nt_ref[b])
    def _():
        sl = slot_ref[0]
        pos = (lax.broadcasted_iota(jnp.int32, (tb, gt_rows), 1) + j * gt_rows).astype(F32)
        p = jnp.where((pos == sl[:, 0:1]) | (pos == sl[:, 1:2]), 1.0, 0.0).astype(BF)
        acc_scr[...] += _dot(p, jnp.concatenate([ya_ref[0], yb_ref[0]], axis=0))

    @pl.when(j == pl.num_programs(1) - 1)
    def _():
        o_ref[...] = x_ref[...] + gt_ref[0] * acc_scr[...]


def _moe_combine(ntiles_b, loc, x2, gt, slots, yg, *, tb, nt, mod_row):
    t, d = x2.shape
    nb = t // tb
    yh = yg.reshape(-1, GROUP_PAD, d)

    def piece3(half):
        return lambda b, j, n, lc: (lc[(b * nt + jnp.minimum(j, n[b] - 1)) * 2 + half], 0, 0)

    return pl.pallas_call(
        _moe_combine_kernel,
        grid_spec=pltpu.PrefetchScalarGridSpec(
            num_scalar_prefetch=2,
            grid=(nb, nt),
            in_specs=[
                pl.BlockSpec((tb, d), lambda b, j, n, lc: (b, 0)),
                pl.BlockSpec((1, 1, d), lambda b, j, n, lc: (mod_row(b), 0, 0)),
                pl.BlockSpec((1, tb, LANES), lambda b, j, n, lc: (b, 0, 0)),
                pl.BlockSpec((1, GROUP_PAD, d), piece3(0)),
                pl.BlockSpec((1, GROUP_PAD, d), piece3(1)),
            ],
            out_specs=pl.BlockSpec((tb, d), lambda b, j, n, lc: (b, 0)),
            scratch_shapes=[pltpu.VMEM((tb, d), F32)],
        ),
        out_shape=jax.ShapeDtypeStruct((t, d), F32),
        compiler_params=_cparams("parallel", "arbitrary"),
        name="moe_combine",
    )(ntiles_b, loc, x2, gt, slots, yh, yh)


def _moe(x2, sh, sc, gt, g, wr_pad, wg, wu, wd, *, tb, mod_row):
    nt = -(-(2 * tb + N_EXPERTS * (GROUP_PAD - 1)) // GROUP_TILE)
    xg, ws, slots, cnt = _moe_group(x2, sh, sc, g, wr_pad, tb=tb, nt=nt, mod_row=mod_row)
    counts = cnt[:, 0, :N_EXPERTS].astype(jnp.int32)
    piece_a, piece_b, step_exp, n_used, loc, ntiles_b = _moe_schedule(counts, nt * (GROUP_TILE // GROUP_PAD))
    yg = _moe_experts(piece_a, piece_b, step_exp, n_used, xg, ws, wg, wu, wd)
    return _moe_combine(ntiles_b, loc, x2, gt, slots, yg, tb=tb, nt=nt, mod_row=mod_row)


def _rope_tables(seq_len):
    pos = np.arange(seq_len)
    prow = (pos // GRID_W).astype(np.float32)
    pcol = (pos % GRID_W).astype(np.float32)
    n_freq = HEAD_DIM // 4
    inv = (np.float32(ROPE_THETA) ** (-np.arange(n_freq, dtype=np.float32) / n_freq)).astype(np.float32)
    ar = jnp.asarray(prow)[:, None] * jnp.asarray(inv)[None, :]
    ac = jnp.asarray(pcol)[:, None] * jnp.asarray(inv)[None, :]
    cos = jnp.concatenate([jnp.cos(ar)] * 2 + [jnp.cos(ac)] * 2, axis=1)
    sin = jnp.concatenate([-jnp.sin(ar), jnp.sin(ar), -jnp.sin(ac), jnp.sin(ac)], axis=1)
    return jnp.concatenate([cos, cos], axis=1), jnp.concatenate([sin, sin], axis=1)


def _head_sum_matrix():
    c = np.arange(ATTN_WIDTH)
    return jnp.asarray((c[:, None] // HEAD_DIM) == (c[None, :] // HEAD_DIM), dtype=F32).astype(BF)


def _fnet_channel_matrix():
    c = np.arange(FNET_WIDTH)
    same = (c[:, None] // FNET_GROUP_DIM) == (c[None, :] // FNET_GROUP_DIM)
    ang = 2.0 * np.pi * (((c[:, None] % FNET_GROUP_DIM) * (c[None, :] % FNET_GROUP_DIM)) % FNET_GROUP_DIM) / FNET_GROUP_DIM
    cb = np.where(same, np.cos(ang), 0.0)
    sb = np.where(same, np.sin(ang), 0.0)
    return jnp.asarray(np.concatenate([cb, -sb], axis=1), dtype=F32).astype(BF)


def _filter_features(n):
    t = jnp.linspace(0.0, 1.0, n, dtype=F32)[:, None]
    w = 2.0 * math.pi * jnp.arange(n, dtype=F32)[:, None] / n
    fb = jnp.linspace(1e-4, FILTER_BANDS - 1, FILTER_BANDS, dtype=F32)
    z = jnp.concatenate([t, jnp.cos(fb * w), -jnp.sin(fb * w)], axis=-1)
    return jnp.pad(z, ((0, 0), (0, 64 - z.shape[1])))


def _decay_rates():
    d = jnp.abs(jnp.linspace(math.log(DECAY_TARGET) / SLOW_DECAY_PCT, math.log(DECAY_TARGET) / FAST_DECAY_PCT,
                             HYENA_WIDTH, dtype=F32))
    return jnp.concatenate([d] * HYENA_ORDER)[None, :]


def kernel(x, c, ctx, c_ctx, w_ada, b_ada, norm1_g, norm2_g, w_in, q_norm_g, k_norm_g, attn_sink,
           hy_conv_w, hy_conv_b, hy_filt_w1, hy_filt_b1, hy_filt_freq1, hy_filt_w2, hy_filt_b2,
           hy_filt_freq2, hy_filt_w3, hy_bias, w_proj_attn, w_proj_hyena, w_proj_fnet, w_out,
           ffn_w_gate, ffn_w_up, ffn_w_down, moe_router, moe_w_gate, moe_w_up, moe_w_down):
    b, seq, d = x.shape
    n_ctx = ctx.shape[1]
    depth = w_ada.shape[0]
    tm = 512
    tiles_per_seq = seq // tm

    cond8 = jnp.concatenate([c, c_ctx[None, :], jnp.zeros((8 - b - 1, d), F32)], axis=0)
    mods = _adaln(cond8, w_ada, b_ada)

    cos_l, sin_l = _rope_tables(seq)
    cos_c = jnp.ones((n_ctx, LANES), F32)
    sin_c = jnp.zeros((n_ctx, LANES), F32)
    gsum = _head_sum_matrix()
    mfn = _fnet_channel_matrix()
    deltas = _decay_rates()
    zfeat_l = _filter_features(seq)
    zfeat_c = _filter_features(n_ctx)

    lat_row = lambda i: i // tiles_per_seq
    ctx_row = lambda i: b
    lat_tab = lambda i: i % tiles_per_seq
    ctx_tab = lambda i: 0
    tm_c = min(tm, n_ctx)

    xs = x.reshape(b * seq, d)
    cs = ctx.reshape(b * n_ctx, d)
    for l in range(depth):
        last = l == depth - 1
        mod = lambda j: mods[l, :, j * d:(j + 1) * d].reshape(8, 1, d)
        w_in_bf = w_in[l].astype(BF)
        qg = jnp.tile(q_norm_g[l], N_HEADS)[None, :]
        kg = jnp.tile(k_norm_g[l], N_KV_HEADS)[None, :]
        g1 = norm1_g[l][None, :]
        wa, wh, wf, wo = (w_proj_attn[l].astype(BF), w_proj_hyena[l].astype(BF),
                          w_proj_fnet[l].astype(BF), w_out[l].astype(BF))
        conv_w = hy_conv_w[l].reshape(3, -1)
        conv_b = hy_conv_b[l][None, :]
        w1p = jnp.pad(hy_filt_w1[l], ((0, 64 - hy_filt_w1.shape[1]), (0, 0)))
        filt = (w1p, hy_filt_b1[l][None, :], hy_filt_freq1[l][None, :], hy_filt_w2[l],
                hy_filt_b2[l][None, :], hy_filt_freq2[l][None, :], hy_filt_w3[l], deltas)

        q_c, kd_c, vd_c, u_c, pq_c, gates_c = _phase_a(
            cs, mod(0), mod(1), g1, w_in_bf, cos_c, sin_c, qg, kg, gsum, mfn,
            tm=tm_c, mod_row=ctx_row, tab_row=ctx_tab)
        kd_c3 = kd_c.reshape(b, n_ctx, -1)
        vd_c3 = vd_c.reshape(b, n_ctx, -1)

        q_l, kd_l, vd_l, u_l, pq_l, gates_l = _phase_a(
            xs, mod(0), mod(1), g1, w_in_bf, cos_l, sin_l, qg, kg, gsum, mfn,
            tm=tm, mod_row=lat_row, tab_row=lat_tab)
        attn_l = _attention(attn_sink[l], q_l.reshape(b, seq, -1), kd_l.reshape(b, seq, -1),
                            vd_l.reshape(b, seq, -1), kd_c3, vd_c3, local=True, tq=512)
        h_l, nrm_l = _hy_filter(zfeat_l, *filt, tm=512)
        uc_l = _hy_prep(u_l.reshape(b, seq, -1), conv_w, conv_b)
        hy_l = _hyena(uc_l, h_l, nrm_l, hy_bias[l])
        fn_l = _fnet(pq_l.reshape(b, seq, -1))
        xs = _merge(xs, mod(2), attn_l.reshape(b * seq, -1), hy_l, fn_l, gates_l, wa, wh, wf, wo,
                    tm=tm, mod_row=lat_row)

        if not last:
            attn_c = _attention(attn_sink[l], q_c.reshape(b, n_ctx, -1), kd_c3, vd_c3, kd_c3, vd_c3,
                                local=False, tq=n_ctx)
            h_c, nrm_c = _hy_filter(zfeat_c, *filt, tm=n_ctx)
            uc_c = _hy_prep(u_c.reshape(b, n_ctx, -1), conv_w, conv_b)
            hy_c = _hyena(uc_c, h_c, nrm_c, hy_bias[l])
            fn_c = _fnet(pq_c.reshape(b, n_ctx, -1))
            cs = _merge(cs, mod(2), attn_c.reshape(b * n_ctx, -1), hy_c, fn_c, gates_c, wa, wh, wf, wo,
                        tm=tm_c, mod_row=ctx_row)

        g2 = norm2_g[l][None, :]
        i = l // 2
        if l % 2 == 0:
            wg, wu, wd = ffn_w_gate[i].astype(BF), ffn_w_up[i].astype(BF), ffn_w_down[i].astype(BF)
            run = lambda t2, rows, tmm: _ffn(t2, mod(3), mod(4), mod(5), g2, wg, wu, wd,
                                             tm=tmm, tf=D_FF // 2, mod_row=rows)
        else:
            wr = jnp.pad(moe_router[i], ((0, 0), (0, LANES - N_EXPERTS)))
            wg, wu, wd = moe_w_gate[i].astype(BF), moe_w_up[i].astype(BF), moe_w_down[i].astype(BF)
            run = lambda t2, rows, tmm: _moe(t2, mod(3), mod(4), mod(5), g2, wr, wg, wu, wd,
                                             tb=tmm, mod_row=rows)
        tm_ffn = 512 if l % 2 == 0 else 1024
        xs = run(xs, lambda t: t // (seq // tm_ffn), tm_ffn)
        if not last:
            cs = run(cs, ctx_row, min(tm_ffn, b * n_ctx))
    return xs.reshape(b, seq, d)
```

```python
import functools
import math

import numpy as np
import jax
import jax.numpy as jnp
from jax import lax
from jax.experimental import pallas as pl
from jax.experimental.pallas import tpu as pltpu

F32 = jnp.float32
BF = jnp.bfloat16

D_MODEL = 1024
DEPTH = 4
GRID_W = 64
HEAD_DIM = 64
N_HEADS = 8
N_KV_HEADS = 2
ATTN_WIDTH = N_HEADS * HEAD_DIM
KV_WIDTH = N_KV_HEADS * HEAD_DIM
WINDOW = 128
QBLK = 128
ROPE_THETA = 10000.0
HYENA_ORDER = 2
HYENA_WIDTH = 256
FILTER_BANDS = 16
FILTER_HIDDEN = 64
DECAY_TARGET = 1e-2
FAST_DECAY_PCT = 0.3
SLOW_DECAY_PCT = 1.5
FNET_WIDTH = 256
FNET_GROUP_DIM = 64
Q_END = ATTN_WIDTH
K_END = Q_END + KV_WIDTH
V_END = K_END + KV_WIDTH
HY_END = V_END + (HYENA_ORDER + 1) * HYENA_WIDTH
FN_END = HY_END + FNET_WIDTH
IN_WIDTH = FN_END + 3 * D_MODEL
D_FF = 2816
N_EXPERTS = 8
EPS = 1e-6
LANES = 128
NEG = -1e30
STAGE_ROWS = 16

VMEM_LIMIT = 56 * 1024 * 1024


def _cparams(*sem):
    return pltpu.CompilerParams(dimension_semantics=sem, vmem_limit_bytes=VMEM_LIMIT)


def _dot(a, b):
    return jnp.dot(a, b, preferred_element_type=F32)


def _dot_nt(a, b):
    return lax.dot_general(a, b, (((1,), (1,)), ((), ())), preferred_element_type=F32)


def _split(a):
    hi = a.astype(BF)
    lo = (a - hi.astype(F32)).astype(BF)
    return hi, lo


def _dot3(a, b):
    ah, al = _split(a)
    bh, bl = _split(b)
    return _dot(ah, bh) + (_dot(ah, bl) + _dot(al, bh))


def _dot2(a, b_bf16):
    ah, al = _split(a)
    return _dot(ah, b_bf16) + _dot(al, b_bf16)


def _sigmoid(v):
    return 0.5 * jnp.tanh(0.5 * v) + 0.5


def _silu(v):
    return v * _sigmoid(v)


def _adaln_kernel(c_ref, w_ref, b_ref, o_ref):
    o_ref[0] = _dot3(_silu(c_ref[...]), w_ref[0]) + b_ref[0]


def _adaln(cond8, w_ada, b_ada):
    depth, d, n6 = w_ada.shape
    tn = 1024
    return pl.pallas_call(
        _adaln_kernel,
        grid=(depth, n6 // tn),
        in_specs=[
            pl.BlockSpec((8, d), lambda l, j: (0, 0)),
            pl.BlockSpec((1, d, tn), lambda l, j: (l, 0, j)),
            pl.BlockSpec((1, 1, tn), lambda l, j: (l, 0, j)),
        ],
        out_specs=pl.BlockSpec((1, 8, tn), lambda l, j: (l, 0, j)),
        out_shape=jax.ShapeDtypeStruct((depth, 8, n6), F32),
        compiler_params=_cparams("parallel", "parallel"),
        name="adaln",
    )(cond8, w_ada, b_ada.reshape(depth, 1, n6))


def _modulated_norm(x, g, sc, sh):
    ms = jnp.mean(x * x, axis=-1, keepdims=True)
    h = (x * lax.rsqrt(ms + EPS)) * g
    return h * (1.0 + sc) + sh


def _phase_a_kernel(x_ref, sh_ref, sc_ref, g_ref, w_ref, cos_ref, sin_ref, qg_ref, kg_ref,
                    gsum_ref, mfn_ref, q_ref, kd_ref, vd_ref, u_ref, pq_ref, gate_ref):
    tm = x_ref.shape[0]
    hb = _modulated_norm(x_ref[...], g_ref[...], sc_ref[0], sh_ref[0]).astype(BF)
    cos = cos_ref[...]
    sin = sin_ref[...]

    def headnorm(t, gain, gs):
        ss = _dot2(t * t, gs)
        return t * lax.rsqrt(ss * (1.0 / HEAD_DIM) + EPS) * gain

    def rope(t, cosw, sinw):
        w = t.shape[1]
        nxt = pltpu.roll(t, w - 16, axis=1)
        prv = pltpu.roll(t, 16, axis=1)
        lw = lax.broadcasted_iota(jnp.int32, t.shape, 1)
        return t * cosw + jnp.where((lw % 32) < 16, nxt, prv) * sinw

    def dup_halves(t):
        lane = lax.broadcasted_iota(jnp.int32, t.shape, 1)
        sw = pltpu.roll(t, 64, axis=1)
        lo = lane < 64
        return jnp.concatenate([jnp.where(lo, t, sw), jnp.where(lo, sw, t)], axis=1)

    pq = _dot(hb, w_ref[:, 0:Q_END])
    qn = headnorm(pq, qg_ref[...], gsum_ref[...])
    cos4 = jnp.concatenate([cos] * 4, axis=1)
    sin4 = jnp.concatenate([sin] * 4, axis=1)
    q_ref[...] = (rope(qn, cos4, sin4) * (HEAD_DIM ** -0.5)).astype(BF)

    pk = _dot(hb, w_ref[:, Q_END:K_END])
    kn = headnorm(pk, kg_ref[...], gsum_ref[0:KV_WIDTH, 0:KV_WIDTH])
    kd_ref[...] = dup_halves(rope(kn, cos, sin)).astype(BF)
    vd_ref[...] = dup_halves(_dot(hb, w_ref[:, K_END:V_END])).astype(BF)

    u_ref[...] = _dot(hb, w_ref[:, V_END:HY_END])
    f = _dot(hb, w_ref[:, HY_END:FN_END])
    pq_ref[...] = _dot(f.astype(BF), mfn_ref[...])
    for i in range(3):
        lo = FN_END + i * D_MODEL
        gate_ref[:, i * D_MODEL:(i + 1) * D_MODEL] = _sigmoid(
            _dot(hb, w_ref[:, lo:lo + D_MODEL])).astype(BF)


def _phase_a(x2, sh, sc, g, w_in_bf, cos_t, sin_t, qg, kg, gsum, mfn, *, tm, mod_row, tab_row):
    t, d = x2.shape
    row3 = lambda i: (mod_row(i), 0, 0)
    full = lambda i: (0, 0)
    tok = lambda i: (i, 0)
    outs = [
        ((t, ATTN_WIDTH), BF), ((t, 2 * KV_WIDTH), BF), ((t, 2 * KV_WIDTH), BF),
        ((t, 3 * HYENA_WIDTH), F32), ((t, 2 * FNET_WIDTH), F32), ((t, 3 * D_MODEL), BF),
    ]
    return pl.pallas_call(
        _phase_a_kernel,
        grid=(t // tm,),
        in_specs=[
            pl.BlockSpec((tm, d), tok),
            pl.BlockSpec((1, 1, d), row3),
            pl.BlockSpec((1, 1, d), row3),
            pl.BlockSpec((1, d), full),
            pl.BlockSpec((d, IN_WIDTH), full, pipeline_mode=pl.Buffered(1)),
            pl.BlockSpec((tm, LANES), lambda i: (tab_row(i), 0)),
            pl.BlockSpec((tm, LANES), lambda i: (tab_row(i), 0)),
            pl.BlockSpec((1, ATTN_WIDTH), full),
            pl.BlockSpec((1, KV_WIDTH), full),
            pl.BlockSpec((ATTN_WIDTH, ATTN_WIDTH), full),
            pl.BlockSpec((FNET_WIDTH, 2 * FNET_WIDTH), full),
        ],
        out_specs=[pl.BlockSpec((tm, s[1]), tok) for s, _ in outs],
        out_shape=[jax.ShapeDtypeStruct(s, dt) for s, dt in outs],
        compiler_params=_cparams("parallel"),
        name="phase_a",
    )(x2, sh, sc, g, w_in_bf, cos_t, sin_t, qg, kg, gsum, mfn)


def _attn_kernel(sink_ref, q_ref, kd_ref, vd_ref, kc_ref, vc_ref, o_ref, *, local, seq_len):
    tq = q_ref.shape[1]
    nblk = tq // QBLK
    gq = N_HEADS // N_KV_HEADS
    rows = gq * QBLK
    lane = lax.broadcasted_iota(jnp.int32, (QBLK, LANES), 1)
    lo_half = lane < 64
    hrow = lax.broadcasted_iota(jnp.int32, (rows, 1), 0) // QBLK
    nband = 3 * QBLK
    for blk in range(nblk):
        r0 = blk * QBLK
        qb = q_ref[0, r0:r0 + QBLK, :]
        if local:
            n = pl.program_id(1) * nblk + blk
            start = pl.multiple_of(jnp.clip((n - 1) * QBLK, 0, seq_len - nband), QBLK)
            qpos = n * QBLK + lax.broadcasted_iota(jnp.int32, (rows, nband), 0) % QBLK
            kpos = start + lax.broadcasted_iota(jnp.int32, (rows, nband), 1)
            valid = jnp.abs(qpos - kpos) <= WINDOW
        for g in range(N_KV_HEADS):
            parts = []
            for hh in range(gq):
                h = gq * g + hh
                qc = qb[:, (h // 2) * LANES:(h // 2 + 1) * LANES]
                keep = lo_half if h % 2 == 0 else jnp.logical_not(lo_half)
                parts.append(jnp.where(keep, qc, jnp.zeros_like(qc)))
            q4 = jnp.concatenate(parts, axis=0)
            sk = jnp.full((rows, 1), sink_ref[gq * g + gq - 1], F32)
            for hh in range(gq - 2, -1, -1):
                sk = jnp.where(hrow == hh, sink_ref[gq * g + hh], sk)
            gl = slice(g * LANES, (g + 1) * LANES)
            s_ctx = _dot_nt(q4, kc_ref[0, :, gl])
            m = jnp.maximum(jnp.max(s_ctx, axis=1, keepdims=True), sk)
            if local:
                s_loc = _dot_nt(q4, kd_ref[0, pl.ds(start, nband), gl])
                s_loc = jnp.where(valid, s_loc, NEG)
                m = jnp.maximum(m, jnp.max(s_loc, axis=1, keepdims=True))
            p_ctx = jnp.exp(s_ctx - m)
            den = jnp.sum(p_ctx, axis=1, keepdims=True) + jnp.exp(sk - m)
            o = _dot(p_ctx.astype(BF), vc_ref[0, :, gl])
            if local:
                p_loc = jnp.exp(s_loc - m)
                den = den + jnp.sum(p_loc, axis=1, keepdims=True)
                o = o + _dot(p_loc.astype(BF), vd_ref[0, pl.ds(start, nband), gl])
            o = o / den
            for cc in range(gq // 2):
                col = (gq // 2) * g + cc
                oa = o[(2 * cc) * QBLK:(2 * cc + 1) * QBLK]
                ob = o[(2 * cc + 1) * QBLK:(2 * cc + 2) * QBLK]
                o_ref[0, r0:r0 + QBLK, col * LANES:(col + 1) * LANES] = (
                    jnp.where(lo_half, oa, ob).astype(BF))


def _attention(sink, q, kd, vd, kc, vc, *, local, tq):
    b, lq, _ = q.shape
    lk = kd.shape[1]
    c = kc.shape[1]
    kern = functools.partial(_attn_kernel, local=local, seq_len=lk)
    return pl.pallas_call(
        kern,
        grid=(b, lq // tq),
        in_specs=[
            pl.BlockSpec(memory_space=pltpu.SMEM),
            pl.BlockSpec((1, tq, ATTN_WIDTH), lambda bi, i: (bi, i, 0)),
            pl.BlockSpec((1, lk, 2 * KV_WIDTH), lambda bi, i: (bi, 0, 0)),
            pl.BlockSpec((1, lk, 2 * KV_WIDTH), lambda bi, i: (bi, 0, 0)),
            pl.BlockSpec((1, c, 2 * KV_WIDTH), lambda bi, i: (bi, 0, 0)),
            pl.BlockSpec((1, c, 2 * KV_WIDTH), lambda bi, i: (bi, 0, 0)),
        ],
        out_specs=pl.BlockSpec((1, tq, ATTN_WIDTH), lambda bi, i: (bi, i, 0)),
        out_shape=jax.ShapeDtypeStruct((b, lq, ATTN_WIDTH), BF),
        compiler_params=_cparams("parallel", "parallel"),
        name="attn_local" if local else "attn_ctx",
    )(sink, q, kd, vd, kc, vc)


def _hy_prep_kernel(u_ref, w_ref, b_ref, o_ref):
    u = u_ref[0]
    n = u.shape[0]
    row = lax.broadcasted_iota(jnp.int32, u.shape, 0)
    prv = jnp.where(row == 0, 0.0, pltpu.roll(u, 1, axis=0))
    nxt = jnp.where(row == n - 1, 0.0, pltpu.roll(u, n - 1, axis=0))
    o_ref[0] = prv * w_ref[0:1, :] + u * w_ref[1:2, :] + nxt * w_ref[2:3, :] + b_ref[...]


def _hy_prep(u, conv_w, conv_b):
    b, n, w = u.shape
    return pl.pallas_call(
        _hy_prep_kernel,
        grid=(b, w // LANES),
        in_specs=[
            pl.BlockSpec((1, n, LANES), lambda bi, j: (bi, 0, j)),
            pl.BlockSpec((3, LANES), lambda bi, j: (0, j)),
            pl.BlockSpec((1, LANES), lambda bi, j: (0, j)),
        ],
        out_specs=pl.BlockSpec((1, n, LANES), lambda bi, j: (bi, 0, j)),
        out_shape=jax.ShapeDtypeStruct((b, n, w), F32),
        compiler_params=_cparams("parallel", "parallel"),
        name="hy_prep",
    )(u, conv_w, conv_b)


def _hy_filter_kernel(z_ref, w1_ref, b1_ref, f1_ref, w2_ref, b2_ref, f2_ref, w3_ref, dl_ref,
                      k_ref, nrm_ref, *, n_half):
    i = pl.program_id(0)
    tm = z_ref.shape[0]
    z = z_ref[...]
    h = jnp.sin(f1_ref[...] * (_dot3(z, w1_ref[...]) + b1_ref[...]))
    h = jnp.sin(f2_ref[...] * (_dot3(h, w2_ref[...]) + b2_ref[...]))
    h = _dot3(h, w3_ref[...])
    h = h * jnp.exp(-z[:, 0:1] * dl_ref[...])
    row = i * tm + lax.broadcasted_iota(jnp.int32, (tm, 1), 0)
    h = jnp.where(row == n_half, 0.0, h)
    k_ref[...] = h

    @pl.when(i == 0)
    def _():
        nrm_ref[...] = jnp.zeros_like(nrm_ref)

    nrm_ref[...] += jnp.sum(jnp.abs(h), axis=0, keepdims=True)


def _hy_filter(zfeat2, w1p, b1, f1, w2, b2, f2, w3, deltas2, *, tm):
    n2 = zfeat2.shape[0]
    n_half = n2 // 2
    wout = w3.shape[1] // 2
    full = lambda i: (0, 0)
    kern = functools.partial(_hy_filter_kernel, n_half=n_half)
    return pl.pallas_call(
        kern,
        grid=(n2 // tm,),
        in_specs=[
            pl.BlockSpec((tm, zfeat2.shape[1]), lambda i: (i, 0)),
            pl.BlockSpec(w1p.shape, full), pl.BlockSpec(b1.shape, full), pl.BlockSpec(f1.shape, full),
            pl.BlockSpec(w2.shape, full), pl.BlockSpec(b2.shape, full), pl.BlockSpec(f2.shape, full),
            pl.BlockSpec((w3.shape[0], wout), lambda i: (0, (i * tm) // n_half)),
            pl.BlockSpec(deltas2.shape, full),
        ],
        out_specs=[pl.BlockSpec((tm, wout), lambda i: (i, 0)), pl.BlockSpec((1, wout), full)],
        out_shape=[jax.ShapeDtypeStruct((n2, wout), F32), jax.ShapeDtypeStruct((1, wout), F32)],
        compiler_params=_cparams("arbitrary"),
        name="hy_filter",
    )(zfeat2, w1p, b1, f1, w2, b2, f2, w3, deltas2)


def _dft_mats(k_out, r_in, period, sign, scale, n_in, real_out):
    k = np.arange(k_out)[:, None]
    r = np.arange(r_in)[None, :]
    ang = 2.0 * np.pi * ((k * r) % period) / period
    fr = np.cos(ang) * scale
    fi = sign * np.sin(ang) * scale
    if real_out:
        mats = [fr, -fi]
    else:
        mats = [np.concatenate([fr, fi], 0), np.concatenate([-fi, fr], 0)]
    return jnp.asarray(np.stack(mats[:n_in], 0), dtype=F32).astype(BF)


def _twiddle(s_n, k_n, n, sign, sbk):
    s0 = lax.broadcasted_iota(jnp.int32, (s_n // sbk, k_n, LANES), 0) * sbk
    k = lax.broadcasted_iota(jnp.int32, (s_n // sbk, k_n, LANES), 1)
    ang = (s0 * k).astype(F32) * (2.0 * math.pi / n)
    ang1 = lax.broadcasted_iota(jnp.int32, (k_n, LANES), 0).astype(F32) * (2.0 * math.pi / n)
    return jnp.cos(ang), sign * jnp.sin(ang), jnp.cos(ang1), sign * jnp.sin(ang1)


def _unpack_pair(p):
    return [pltpu.unpack_elementwise(p, index=i, packed_dtype=BF, unpacked_dtype=F32) for i in (0, 1)]


def _stage_kernel(*refs, n_in, r_in, k_mid, k_out, sbk, tw, spec, second, gate, real_out,
                  transposed_out, flat, packed_in, packed_spec, packed_out):
    it = iter(refs)
    x_refs = [next(it) for _ in range(n_in)]
    g_ref = next(it)
    g2_ref = next(it) if second else None
    tw_refs = [next(it) for _ in range(4)] if tw else None
    n_spec = 1 if packed_spec else 2
    spec_refs = [next(it) for _ in range(n_spec + 1)] if spec else None
    gate_refs = [next(it) for _ in range(5)] if gate else None
    out_refs = [next(it)] if (real_out or gate or packed_out) else [next(it), next(it)]
    if not flat:
        x_refs = [r.reshape(r_in * sbk, LANES) for r in x_refs]
        if spec:
            spec_refs = [r.reshape(k_mid * sbk, LANES) for r in spec_refs[:n_spec]] + spec_refs[n_spec:]
        if gate:
            gate_refs = [r.reshape(k_out * sbk, LANES) for r in gate_refs[:4]] + gate_refs[4:]
            out_refs = [out_refs[0].reshape(2 * k_out * sbk, LANES)]
        elif not transposed_out:
            out_refs = [r.reshape(k_out * sbk, LANES) for r in out_refs]
    if spec:
        inv = 1.0 / spec_refs[n_spec][...]
    if tw:
        tr, ti = tw_refs[0][0], tw_refs[1][0]
        wr, wi = tw_refs[2][...], tw_refs[3][...]
    for j in range(sbk):
        parts = [x_ref[...] if flat else x_ref[pl.ds(j, r_in, stride=sbk), :] for x_ref in x_refs]
        if packed_in:
            parts = _unpack_pair(parts[0])
        acc = None
        for xi, xv in enumerate(parts):
            d = _dot(g_ref[xi], xv.astype(BF))
            acc = d if acc is None else acc + d
        if real_out:
            ys = [acc]
        else:
            yr, yi = acc[:k_mid], acc[k_mid:]
            if spec:
                rows = slice(None) if flat else pl.ds(j, k_mid, stride=sbk)
                if packed_spec:
                    sr, si = _unpack_pair(spec_refs[0][rows, :])
                else:
                    sr, si = spec_refs[0][rows, :], spec_refs[1][rows, :]
                sr, si = sr * inv, si * inv
                yr, yi = yr * sr - yi * si, yr * si + yi * sr
            if second:
                acc = _dot(g2_ref[0], yr.astype(BF)) + _dot(g2_ref[1], yi.astype(BF))
                yr, yi = acc[:k_out], acc[k_out:]
            if tw:
                yr, yi = yr * tr - yi * ti, yr * ti + yi * tr
                if j + 1 < sbk:
                    tr, ti = tr * wr - ti * wi, tr * wi + ti * wr
            ys = [yr, yi]
        if gate:
            o_ref = out_refs[0]
            for part, y in enumerate(ys):
                rows = slice(None) if flat else pl.ds(j, k_out, stride=sbk)
                val = gate_refs[part][rows, :] * (y + gate_refs[4][...] * gate_refs[2 + part][rows, :])
                if flat:
                    o_ref[part] = val.astype(o_ref.dtype)
                else:
                    o_ref[pl.ds(part * k_out * sbk + j, k_out, stride=sbk), :] = val.astype(o_ref.dtype)
            continue
        if packed_out:
            ys = [pltpu.pack_elementwise(ys, packed_dtype=BF)]
        for o_ref, y in zip(out_refs, ys):
            if flat:
                o_ref[...] = y.astype(o_ref.dtype)
            elif transposed_out:
                o_ref[0, j] = y.astype(o_ref.dtype)
            else:
                o_ref[pl.ds(j, k_out, stride=sbk), :] = y.astype(o_ref.dtype)


def _fft_stage(xs, x_sel, gmat, *, r_in, s_n, k_out, n_groups, n_cblk, transposed_out, real_out,
               out_dtype=F32, g2mat=None, tw=None, spec=None, spec_sel=None, gate=None, sbk=STAGE_ROWS,
               packed_in=False, packed_out=False, name="fft_stage"):
    n_in = len(xs)
    flat = s_n == 1
    sbk = 1 if flat else min(sbk, s_n)
    cb = LANES
    in_specs, args = [], []
    for x, sel in zip(xs, x_sel):
        if flat:
            in_specs.append(pl.BlockSpec((None, r_in, cb), lambda s, g, c, sel=sel: (sel(g, c)[0], 0, sel(g, c)[1])))
            args.append(x)
        else:
            xv = x.reshape(x.shape[0], x.shape[1] // s_n, s_n, x.shape[2])
            in_specs.append(pl.BlockSpec((1, r_in, sbk, cb),
                                         lambda s, g, c, sel=sel: (sel(g, c)[0], 0, s, sel(g, c)[1])))
            args.append(xv)
    in_specs.append(pl.BlockSpec(gmat.shape, lambda s, g, c: (0, 0, 0)))
    args.append(gmat)
    k_mid = gmat.shape[1] // (1 if real_out else 2)
    if g2mat is not None:
        in_specs.append(pl.BlockSpec(g2mat.shape, lambda s, g, c: (0, 0, 0)))
        args.append(g2mat)
    if tw is not None:
        for tarr in tw[:2]:
            in_specs.append(pl.BlockSpec((1, k_out, LANES), lambda s, g, c: (s, 0, 0)))
            args.append(tarr)
        for tarr in tw[2:]:
            in_specs.append(pl.BlockSpec((k_out, LANES), lambda s, g, c: (0, 0)))
            args.append(tarr)
    if spec is not None:
        *planes, nrm = spec
        for arr in planes:
            if flat:
                in_specs.append(pl.BlockSpec((k_mid, cb), lambda s, g, c: (0, spec_sel(g, c))))
                args.append(arr)
            else:
                in_specs.append(pl.BlockSpec((1, k_mid, sbk, cb), lambda s, g, c: (0, 0, s, spec_sel(g, c))))
                args.append(arr.reshape(1, k_mid, s_n, arr.shape[-1]))
        in_specs.append(pl.BlockSpec((1, cb), lambda s, g, c: (0, spec_sel(g, c))))
        args.append(nrm)
    if gate is not None:
        (ga, gblk), (za, zblk), bias = gate
        for arr, blk in ((ga, gblk), (za, zblk)):
            for bi in (0, 1):
                if flat:
                    in_specs.append(pl.BlockSpec((None, k_out, cb), lambda s, g, c, bi=bi, blk=blk: (bi, 0, blk + c)))
                    args.append(arr)
                else:
                    in_specs.append(pl.BlockSpec((1, k_out, sbk, cb),
                                                 lambda s, g, c, bi=bi, blk=blk: (bi, 0, s, blk + c)))
                    args.append(arr.reshape(arr.shape[0], k_out, s_n, arr.shape[-1]))
        in_specs.append(pl.BlockSpec((1, cb), lambda s, g, c: (0, c)))
        args.append(bias)
    ctot = n_cblk * cb
    if gate is not None:
        n_groups = 2
        if flat:
            oshape = (2, k_out, ctot)
            ospec = pl.BlockSpec((2, k_out, cb), lambda s, g, c: (0, 0, c))
        else:
            oshape = (2, k_out, s_n, ctot)
            ospec = pl.BlockSpec((2, k_out, sbk, cb), lambda s, g, c: (0, 0, s, c))
    elif flat:
        oshape = (n_groups, k_out, ctot)
        ospec = pl.BlockSpec((None, k_out, cb), lambda s, g, c: (g, 0, c))
    elif transposed_out:
        oshape = (n_groups, s_n, k_out, ctot)
        ospec = pl.BlockSpec((1, sbk, k_out, cb), lambda s, g, c: (g, s, 0, c))
    else:
        oshape = (n_groups, k_out, s_n, ctot)
        ospec = pl.BlockSpec((1, k_out, sbk, cb), lambda s, g, c: (g, 0, s, c))
    n_out = 1 if (real_out or gate is not None or packed_out) else 2
    if packed_out:
        out_dtype = jnp.int32
    kern = functools.partial(_stage_kernel, n_in=n_in, r_in=r_in, k_mid=k_mid, k_out=k_out, sbk=sbk,
                             tw=tw is not None, spec=spec is not None, second=g2mat is not None,
                             gate=gate is not None, real_out=real_out, transposed_out=transposed_out,
                             flat=flat, packed_in=packed_in, packed_spec=spec is not None and len(spec) == 2,
                             packed_out=packed_out)
    n_grid_groups = 1 if gate is not None else n_groups
    outs = pl.pallas_call(
        kern,
        grid=(s_n // sbk, n_grid_groups, n_cblk),
        in_specs=in_specs,
        out_specs=[ospec] * n_out,
        out_shape=[jax.ShapeDtypeStruct(oshape, out_dtype)] * n_out,
        compiler_params=_cparams("parallel", "parallel", "parallel"),
        name=name,
    )(*args)
    return [o.reshape(n_groups, -1, ctot) for o in outs]


def _split_len(n):
    if n <= 1024:
        return n, 1
    s = 128
    return n // s, s


def _fft_forward(xs, x_sel, n, n_rows, *, n_groups, n_cblk, name="fwd"):
    n1, s = _split_len(n)
    n_in = len(xs)
    if s == 1:
        g = _dft_mats(n, n_rows, n, -1.0, 1.0, n_in, False)
        return _fft_stage(xs, x_sel, g, r_in=n_rows, s_n=1, k_out=n, n_groups=n_groups, n_cblk=n_cblk,
                          transposed_out=False, real_out=False, name=name + "_direct")
    r1 = n_rows // s
    g1 = _dft_mats(n1, r1, n1, -1.0, 1.0, n_in, False)
    tw = _twiddle(s, n1, n, -1.0, STAGE_ROWS)
    (a,) = _fft_stage(xs, x_sel, g1, r_in=r1, s_n=s, k_out=n1, n_groups=n_groups, n_cblk=n_cblk,
                      transposed_out=True, real_out=False, tw=tw, packed_out=True, name=name + "_s1")
    g2 = _dft_mats(s, s, s, -1.0, 1.0, 2, False)
    return _fft_stage([a], [lambda g, c: (g, c)], g2, r_in=s, s_n=n1, k_out=s, n_groups=n_groups,
                      n_cblk=n_cblk, transposed_out=False, real_out=False, packed_in=True, packed_out=True,
                      name=name + "_s2")


def _hyena(uc, k_filt, nrm, hy_bias):
    b, n, _ = uc.shape
    w = HYENA_WIDTH
    wblk = w // LANES
    nfft = 2 * n
    n1, s = _split_len(nfft)
    ident = lambda g, c: (g, c)
    k_spec = _fft_forward([k_filt[None]], [lambda g, c: (0, c)], nfft, nfft, n_groups=1,
                          n_cblk=HYENA_ORDER * wblk, name="hy_filt_fft")
    z, zblk = uc, 2 * wblk
    for o in range(HYENA_ORDER):
        sel_r = lambda g, c, zblk=zblk: (0, zblk + c)
        sel_i = lambda g, c, zblk=zblk: (1, zblk + c)
        spec = tuple(p[0] for p in k_spec) + (nrm,)
        spec_sel = lambda g, c, o=o: o * wblk + c
        gate = ((uc, o * wblk), (z, zblk), hy_bias[o:o + 1])
        common = dict(n_groups=1, n_cblk=wblk, real_out=False)
        if s == 1:
            gf = _dft_mats(nfft, n, nfft, -1.0, 1.0, 2, False)
            gi = _dft_mats(n, nfft, nfft, 1.0, 1.0 / nfft, 2, False)
            (z,) = _fft_stage([z, z], [sel_r, sel_i], gf, r_in=n, s_n=1, k_out=n, transposed_out=False,
                              g2mat=gi, spec=spec, spec_sel=spec_sel, gate=gate, name="hy_direct", **common)
        else:
            r1 = n // s
            g1 = _dft_mats(n1, r1, n1, -1.0, 1.0, 2, False)
            (a,) = _fft_stage([z, z], [sel_r, sel_i], g1, r_in=r1, s_n=s, k_out=n1, transposed_out=True,
                              tw=_twiddle(s, n1, nfft, -1.0, STAGE_ROWS), packed_out=True, name="hy_s1",
                              **common)
            g2 = _dft_mats(s, s, s, -1.0, 1.0, 2, False)
            g3 = _dft_mats(s, s, s, 1.0, 1.0, 2, False)
            (q,) = _fft_stage([a], [ident], g2, r_in=s, s_n=n1, k_out=s, transposed_out=True,
                              g2mat=g3, tw=_twiddle(n1, s, nfft, 1.0, STAGE_ROWS), spec=spec,
                              spec_sel=spec_sel, packed_in=True, packed_out=True, name="hy_mid", **common)
            g4 = _dft_mats(n // s, n1, n1, 1.0, 1.0 / nfft, 2, False)
            (z,) = _fft_stage([q], [ident], g4, r_in=n1, s_n=s, k_out=n // s, transposed_out=False,
                              gate=gate, packed_in=True, name="hy_last", **common)
        zblk = 0
    return z.reshape(b * n, w)


def _fnet(pq):
    b, n, _ = pq.shape
    w = FNET_WIDTH
    wblk = w // LANES
    scale = 1.0 / math.sqrt(n * FNET_GROUP_DIM)
    sel_r = lambda g, c: (g, c)
    sel_i = lambda g, c: (g, wblk + c)
    ident = lambda g, c: (g, c)
    n1, s = _split_len(n)
    if s == 1:
        g = _dft_mats(n, n, n, -1.0, scale, 2, True)
        (y,) = _fft_stage([pq, pq], [sel_r, sel_i], g, r_in=n, s_n=1, k_out=n, n_groups=b, n_cblk=wblk,
                          transposed_out=False, real_out=True, name="fnet_direct")
        return y.reshape(b * n, w)
    g1 = _dft_mats(n1, n1, n1, -1.0, 1.0, 2, False)
    tw = _twiddle(s, n1, n, -1.0, STAGE_ROWS)
    (a,) = _fft_stage([pq, pq], [sel_r, sel_i], g1, r_in=n1, s_n=s, k_out=n1, n_groups=b, n_cblk=wblk,
                      transposed_out=True, real_out=False, tw=tw, packed_out=True, name="fnet_s1")
    g2 = _dft_mats(s, s, s, -1.0, scale, 2, True)
    (y,) = _fft_stage([a], [ident], g2, r_in=s, s_n=n1, k_out=s, n_groups=b, n_cblk=wblk,
                      transposed_out=False, real_out=True, packed_in=True, name="fnet_s2")
    return y.reshape(b * n, w)


def _merge_kernel(x_ref, gt_ref, a_ref, h_ref, f_ref, gate_ref, wa_ref, wh_ref, wf_ref, wo_ref, o_ref):
    d = D_MODEL
    m = gate_ref[:, 0:d].astype(F32) * _dot(a_ref[...], wa_ref[...])
    m = m + gate_ref[:, d:2 * d].astype(F32) * _dot(h_ref[...].astype(BF), wh_ref[...])
    m = m + gate_ref[:, 2 * d:3 * d].astype(F32) * _dot(f_ref[...].astype(BF), wf_ref[...])
    y = _dot(m.astype(BF), wo_ref[...])
    o_ref[...] = x_ref[...] + gt_ref[0] * y


def _merge(x2, gt, attn_o, hy_o, fn_o, gates, wa, wh, wf, wo, *, tm, mod_row):
    t, d = x2.shape
    tok = lambda i: (i, 0)
    full = lambda i: (0, 0)
    return pl.pallas_call(
        _merge_kernel,
        grid=(t // tm,),
        in_specs=[
            pl.BlockSpec((tm, d), tok),
            pl.BlockSpec((1, 1, d), lambda i: (mod_row(i), 0, 0)),
            pl.BlockSpec((tm, ATTN_WIDTH), tok),
            pl.BlockSpec((tm, HYENA_WIDTH), tok),
            pl.BlockSpec((tm, FNET_WIDTH), tok),
            pl.BlockSpec((tm, 3 * d), tok),
            pl.BlockSpec(wa.shape, full, pipeline_mode=pl.Buffered(1)),
            pl.BlockSpec(wh.shape, full, pipeline_mode=pl.Buffered(1)),
            pl.BlockSpec(wf.shape, full, pipeline_mode=pl.Buffered(1)),
            pl.BlockSpec(wo.shape, full, pipeline_mode=pl.Buffered(1)),
        ],
        out_specs=pl.BlockSpec((tm, d), tok),
        out_shape=jax.ShapeDtypeStruct((t, d), F32),
        compiler_params=_cparams("parallel"),
        name="merge",
    )(x2, gt, attn_o, hy_o, fn_o, gates, wa, wh, wf, wo)


def _ffn_kernel(x_ref, sh_ref, sc_ref, gt_ref, g_ref, wg_ref, wu_ref, wd_ref, o_ref, h_scr, acc_scr):
    f = pl.program_id(1)

    @pl.when(f == 0)
    def _():
        h_scr[...] = _modulated_norm(x_ref[...], g_ref[...], sc_ref[0], sh_ref[0]).astype(BF)
        acc_scr[...] = jnp.zeros_like(acc_scr)

    hb = h_scr[...]
    act = _silu(_dot(hb, wg_ref[...])) * _dot(hb, wu_ref[...])
    acc_scr[...] += _dot(act.astype(BF), wd_ref[...])

    @pl.when(f == pl.num_programs(1) - 1)
    def _():
        o_ref[...] = x_ref[...] + gt_ref[0] * acc_scr[...]


def _ffn(x2, sh, sc, gt, g, wg, wu, wd, *, tm, tf, mod_row):
    t, d = x2.shape
    ff = wg.shape[1]
    row3 = lambda i, f: (mod_row(i), 0, 0)
    return pl.pallas_call(
        _ffn_kernel,
        grid=(t // tm, ff // tf),
        in_specs=[
            pl.BlockSpec((tm, d), lambda i, f: (i, 0)),
            pl.BlockSpec((1, 1, d), row3), pl.BlockSpec((1, 1, d), row3), pl.BlockSpec((1, 1, d), row3),
            pl.BlockSpec((1, d), lambda i, f: (0, 0)),
            pl.BlockSpec((d, tf), lambda i, f: (0, f)),
            pl.BlockSpec((d, tf), lambda i, f: (0, f)),
            pl.BlockSpec((tf, d), lambda i, f: (f, 0)),
        ],
        out_specs=pl.BlockSpec((tm, d), lambda i, f: (i, 0)),
        out_shape=jax.ShapeDtypeStruct((t, d), F32),
        scratch_shapes=[pltpu.VMEM((tm, d), BF), pltpu.VMEM((tm, d), F32)],
        compiler_params=_cparams("parallel", "arbitrary"),
        name="ffn_dense",
    )(x2, sh, sc, gt, g, wg, wu, wd)


def _top2(logits):
    lane = lax.broadcasted_iota(jnp.int32, logits.shape, 1)
    lg = jnp.where(lane < N_EXPERTS, logits, -jnp.inf)
    m1 = jnp.max(lg, axis=1, keepdims=True)
    i1 = jnp.min(jnp.where(lg == m1, lane, LANES), axis=1, keepdims=True)
    lg2 = jnp.where(lane == i1, -jnp.inf, lg)
    m2 = jnp.max(lg2, axis=1, keepdims=True)
    i2 = jnp.min(jnp.where(lg2 == m2, lane, LANES), axis=1, keepdims=True)
    e2 = jnp.exp(m2 - m1)
    w1 = 1.0 / (1.0 + e2)
    return i1, i2, w1, e2 * w1


GROUP_TILE = 256
GROUP_PAD = 128
SLOT_RADIX = 64.0


def _moe_group_kernel(x_ref, sh_ref, sc_ref, g_ref, wr_ref, xg_ref, ws_ref, slot_ref, cnt_ref,
                      h_scr, rows_scr, wm_scr):
    j = pl.program_id(1)
    tb = x_ref.shape[0]
    gt_rows = xg_ref.shape[1]

    @pl.when(j == 0)
    def _():
        h = _modulated_norm(x_ref[...], g_ref[...], sc_ref[0], sh_ref[0])
        h_scr[...] = h.astype(BF)
        i1, i2, w1, w2 = _top2(_dot3(h, wr_ref[...]))
        lane = lax.broadcasted_iota(jnp.int32, (tb, LANES), 1)
        oh0 = jnp.where(lane == i1, 1.0, 0.0)
        oh1 = jnp.where(lane == i2, 1.0, 0.0)
        c0 = jnp.sum(oh0, axis=0, keepdims=True)
        cnt = c0 + jnp.sum(oh1, axis=0, keepdims=True)
        tri = jnp.where(lax.broadcasted_iota(jnp.int32, (tb, tb), 1)
                        < lax.broadcasted_iota(jnp.int32, (tb, tb), 0), 1.0, 0.0).astype(BF)
        pre0 = _dot(tri, oh0.astype(BF))
        pre1 = _dot(tri, oh1.astype(BF)) + c0
        tiles = jnp.ceil(cnt * (1.0 / GROUP_PAD))
        upper = jnp.where(lax.broadcasted_iota(jnp.int32, (LANES, LANES), 0)
                          < lax.broadcasted_iota(jnp.int32, (LANES, LANES), 1), 1.0, 0.0).astype(BF)
        off = _dot(jnp.broadcast_to(tiles, (8, LANES)).astype(BF), upper)[0:1] * float(GROUP_PAD)
        slot0 = jnp.sum(oh0 * (off + pre0), axis=1, keepdims=True)
        slot1 = jnp.sum(oh1 * (off + pre1), axis=1, keepdims=True)
        slot_ref[0] = jnp.where(lane == 0, slot0, jnp.where(lane == 1, slot1, 0.0))
        cnt_ref[0] = jnp.broadcast_to(cnt, (8, LANES))
        hi0 = jnp.floor(slot0 * (1.0 / SLOT_RADIX))
        hi1 = jnp.floor(slot1 * (1.0 / SLOT_RADIX))
        digits = jnp.where(lane == 0, hi0, jnp.where(lane == 1, slot0 - SLOT_RADIX * hi0,
                           jnp.where(lane == 2, hi1, jnp.where(lane == 3, slot1 - SLOT_RADIX * hi1, 0.0))))
        sel = jnp.where(lax.broadcasted_iota(jnp.int32, (8, LANES), 0)
                        == lax.broadcasted_iota(jnp.int32, (8, LANES), 1), 1.0, 0.0).astype(BF)
        rows_scr[...] = _dot_nt(sel, digits.astype(BF))
        w1h, w1l = _split(w1)
        w1m, w1l = _split(w1 - w1h.astype(F32))
        w2h, w2l = _split(w2)
        w2m, w2l = _split(w2 - w2h.astype(F32))
        cols = [w1h, w1m, w1l, w2h, w2m, w2l]
        wm = jnp.zeros((tb, LANES), F32)
        for li, col in enumerate(cols):
            wm = jnp.where(lane == li, col.astype(F32), wm)
        wm_scr[...] = wm.astype(BF)

    rows = rows_scr[...]
    s0 = rows[0:1] * SLOT_RADIX + rows[1:2]
    s1 = rows[2:3] * SLOT_RADIX + rows[3:4]
    pos = (lax.broadcasted_iota(jnp.int32, (gt_rows, tb), 0) + j * gt_rows).astype(F32)
    g0 = jnp.where(pos == s0, 1.0, 0.0).astype(BF)
    g1 = jnp.where(pos == s1, 1.0, 0.0).astype(BF)
    xg_ref[0] = _dot(g0 + g1, h_scr[...]).astype(BF)
    lane_w = lax.broadcasted_iota(jnp.int32, (gt_rows, LANES), 1)
    wsum = (jnp.where(lane_w < 3, _dot(g0, wm_scr[...]), 0.0)
            + jnp.where((lane_w >= 3) & (lane_w < 6), _dot(g1, wm_scr[...]), 0.0))
    ws_ref[0] = jnp.broadcast_to(jnp.sum(wsum, axis=1, keepdims=True), (gt_rows, LANES))


def _moe_group(x2, sh, sc, g, wr_pad, *, tb, nt, mod_row):
    t, d = x2.shape
    nb = t // tb
    row3 = lambda b, j: (mod_row(b), 0, 0)
    return pl.pallas_call(
        _moe_group_kernel,
        grid=(nb, nt),
        in_specs=[
            pl.BlockSpec((tb, d), lambda b, j: (b, 0)),
            pl.BlockSpec((1, 1, d), row3), pl.BlockSpec((1, 1, d), row3),
            pl.BlockSpec((1, d), lambda b, j: (0, 0)),
            pl.BlockSpec((d, LANES), lambda b, j: (0, 0)),
        ],
        out_specs=[
            pl.BlockSpec((1, GROUP_TILE, d), lambda b, j: (b * nt + j, 0, 0)),
            pl.BlockSpec((1, GROUP_TILE, LANES), lambda b, j: (b * nt + j, 0, 0)),
            pl.BlockSpec((1, tb, LANES), lambda b, j: (b, 0, 0)),
            pl.BlockSpec((1, 8, LANES), lambda b, j: (b, 0, 0)),
        ],
        out_shape=[
            jax.ShapeDtypeStruct((nb * nt, GROUP_TILE, d), BF),
            jax.ShapeDtypeStruct((nb * nt, GROUP_TILE, LANES), F32),
            jax.ShapeDtypeStruct((nb, tb, LANES), F32),
            jax.ShapeDtypeStruct((nb, 8, LANES), F32),
        ],
        scratch_shapes=[pltpu.VMEM((tb, d), BF), pltpu.VMEM((8, tb), F32), pltpu.VMEM((tb, LANES), BF)],
        compiler_params=_cparams("parallel", "arbitrary"),
        name="moe_group",
    )(x2, sh, sc, g, wr_pad)


def _moe_schedule(cnt, nh):
    h = (cnt + GROUP_PAD - 1) // GROUP_PAD
    nb, ne = h.shape
    tot = h.sum(0)
    pairs = (tot + 1) // 2
    cum_p = jnp.cumsum(pairs)
    start_p = cum_p - pairs
    n_used = cum_p[-1]
    n_steps = (nb * nh + ne) // 2
    q = jnp.minimum(jnp.arange(n_steps, dtype=jnp.int32), n_used - 1)
    e = jnp.sum(q[:, None] >= cum_p[None, :], axis=1).astype(jnp.int32)
    r = q - start_p[e]
    cum_b = jnp.cumsum(h, axis=0)
    first = jnp.cumsum(h, axis=1) - h

    def piece(idx):
        idx = jnp.minimum(idx, tot[e] - 1)
        blk = jnp.sum(idx[:, None] >= cum_b.T[e], axis=1).astype(jnp.int32)
        return blk * nh + first[blk, e] + idx - (cum_b[blk, e] - h[blk, e])

    x = jnp.arange(nh, dtype=jnp.int32)[None, :]
    ex = jnp.sum(x[:, :, None] >= jnp.cumsum(h, axis=1)[:, None, :], axis=2).astype(jnp.int32)
    exc = jnp.minimum(ex, ne - 1)
    g = jnp.take_along_axis(cum_b - h, exc, axis=1) + x - jnp.take_along_axis(first, exc, axis=1)
    loc = 2 * (start_p[exc] + g // 2) + g % 2
    loc = jnp.where(ex < ne, loc, loc[:, 0:1])
    return (piece(2 * r).astype(jnp.int32), piece(2 * r + 1).astype(jnp.int32), e,
            n_used.astype(jnp.int32).reshape(1), loc.reshape(-1).astype(jnp.int32),
            ((h.sum(1) + 1) // 2).astype(jnp.int32))


def _moe_expert_kernel(pa_ref, pb_ref, exp_ref, nused_ref, xa_ref, xb_ref, wa_ref, wb_ref,
                       wg_ref, wu_ref, wd_ref, y_ref):
    @pl.when(pl.program_id(0) < nused_ref[0])
    def _():
        x = jnp.concatenate([xa_ref[0], xb_ref[0]], axis=0)
        act = _silu(_dot(x, wg_ref[0])) * _dot(x, wu_ref[0])
        y = _dot(act.astype(BF), wd_ref[0])
        w = jnp.concatenate([wa_ref[0], wb_ref[0]], axis=0)
        y_ref[0] = (y * jnp.concatenate([w] * (y.shape[1] // LANES), axis=1)).astype(BF)


def _moe_experts(piece_a, piece_b, step_exp, n_used, xg, ws, wg, wu, wd):
    d = xg.shape[-1]
    ff = wg.shape[2]
    n_steps = piece_a.shape[0]
    xh = xg.reshape(-1, GROUP_PAD, d)
    wh = ws.reshape(-1, GROUP_PAD, LANES)
    pa3 = lambda i, pa, pb, se, nu: (pa[i], 0, 0)
    pb3 = lambda i, pa, pb, se, nu: (pb[i], 0, 0)
    exp3 = lambda i, pa, pb, se, nu: (se[i], 0, 0)
    return pl.pallas_call(
        _moe_expert_kernel,
        grid_spec=pltpu.PrefetchScalarGridSpec(
            num_scalar_prefetch=4,
            grid=(n_steps,),
            in_specs=[
                pl.BlockSpec((1, GROUP_PAD, d), pa3),
                pl.BlockSpec((1, GROUP_PAD, d), pb3),
                pl.BlockSpec((1, GROUP_PAD, LANES), pa3),
                pl.BlockSpec((1, GROUP_PAD, LANES), pb3),
                pl.BlockSpec((1, d, ff), exp3),
                pl.BlockSpec((1, d, ff), exp3),
                pl.BlockSpec((1, ff, d), exp3),
            ],
            out_specs=pl.BlockSpec((1, 2 * GROUP_PAD, d),
                                   lambda i, pa, pb, se, nu: (jnp.minimum(i, nu[0] - 1), 0, 0)),
        ),
        out_shape=jax.ShapeDtypeStruct((n_steps, 2 * GROUP_PAD, d), BF),
        compiler_params=_cparams("arbitrary"),
        name="moe_experts",
    )(piece_a, piece_b, step_exp, n_used, xh, xh, wh, wh, wg, wu, wd)


def _moe_combine_kernel(nt_ref, loc_ref, x_ref, gt_ref, slot_ref, ya_ref, yb_ref, o_ref, acc_scr):
    b = pl.program_id(0)
    j = pl.program_id(1)
    tb = x_ref.shape[0]
    gt_rows = 2 * ya_ref.shape[1]

    @pl.when(j == 0)
    def _():
        acc_scr[...] = jnp.zeros_like(acc_scr)

    @pl.when(j < nt_ref[b])
    def _():
        sl = slot_ref[0]
        pos = (lax.broadcasted_iota(jnp.int32, (tb, gt_rows), 1) + j * gt_rows).astype(F32)
        p = jnp.where((pos == sl[:, 0:1]) | (pos == sl[:, 1:2]), 1.0, 0.0).astype(BF)
        acc_scr[...] += _dot(p, jnp.concatenate([ya_ref[0], yb_ref[0]], axis=0))

    @pl.when(j == pl.num_programs(1) - 1)
    def _():
        o_ref[...] = x_ref[...] + gt_ref[0] * acc_scr[...]


def _moe_combine(ntiles_b, loc, x2, gt, slots, yg, *, tb, nt, mod_row):
    t, d = x2.shape
    nb = t // tb
    yh = yg.reshape(-1, GROUP_PAD, d)

    def piece3(half):
        return lambda b, j, n, lc: (lc[(b * nt + jnp.minimum(j, n[b] - 1)) * 2 + half], 0, 0)

    return pl.pallas_call(
        _moe_combine_kernel,
        grid_spec=pltpu.PrefetchScalarGridSpec(
            num_scalar_prefetch=2,
            grid=(nb, nt),
            in_specs=[
                pl.BlockSpec((tb, d), lambda b, j, n, lc: (b, 0)),
                pl.BlockSpec((1, 1, d), lambda b, j, n, lc: (mod_row(b), 0, 0)),
                pl.BlockSpec((1, tb, LANES), lambda b, j, n, lc: (b, 0, 0)),
                pl.BlockSpec((1, GROUP_PAD, d), piece3(0)),
                pl.BlockSpec((1, GROUP_PAD, d), piece3(1)),
            ],
            out_specs=pl.BlockSpec((tb, d), lambda b, j, n, lc: (b, 0)),
            scratch_shapes=[pltpu.VMEM((tb, d), F32)],
        ),
        out_shape=jax.ShapeDtypeStruct((t, d), F32),
        compiler_params=_cparams("parallel", "arbitrary"),
        name="moe_combine",
    )(ntiles_b, loc, x2, gt, slots, yh, yh)


def _moe(x2, sh, sc, gt, g, wr_pad, wg, wu, wd, *, tb, mod_row):
    nt = -(-(2 * tb + N_EXPERTS * (GROUP_PAD - 1)) // GROUP_TILE)
    xg, ws, slots, cnt = _moe_group(x2, sh, sc, g, wr_pad, tb=tb, nt=nt, mod_row=mod_row)
    counts = cnt[:, 0, :N_EXPERTS].astype(jnp.int32)
    piece_a, piece_b, step_exp, n_used, loc, ntiles_b = _moe_schedule(counts, nt * (GROUP_TILE // GROUP_PAD))
    yg = _moe_experts(piece_a, piece_b, step_exp, n_used, xg, ws, wg, wu, wd)
    return _moe_combine(ntiles_b, loc, x2, gt, slots, yg, tb=tb, nt=nt, mod_row=mod_row)


def _rope_tables(seq_len):
    pos = np.arange(seq_len)
    prow = (pos // GRID_W).astype(np.float32)
    pcol = (pos % GRID_W).astype(np.float32)
    n_freq = HEAD_DIM // 4
    inv = (np.float32(ROPE_THETA) ** (-np.arange(n_freq, dtype=np.float32) / n_freq)).astype(np.float32)
    ar = jnp.asarray(prow)[:, None] * jnp.asarray(inv)[None, :]
    ac = jnp.asarray(pcol)[:, None] * jnp.asarray(inv)[None, :]
    cos = jnp.concatenate([jnp.cos(ar)] * 2 + [jnp.cos(ac)] * 2, axis=1)
    sin = jnp.concatenate([-jnp.sin(ar), jnp.sin(ar), -jnp.sin(ac), jnp.sin(ac)], axis=1)
    return jnp.concatenate([cos, cos], axis=1), jnp.concatenate([sin, sin], axis=1)


def _head_sum_matrix():
    c = np.arange(ATTN_WIDTH)
    return jnp.asarray((c[:, None] // HEAD_DIM) == (c[None, :] // HEAD_DIM), dtype=F32).astype(BF)


def _fnet_channel_matrix():
    c = np.arange(FNET_WIDTH)
    same = (c[:, None] // FNET_GROUP_DIM) == (c[None, :] // FNET_GROUP_DIM)
    ang = 2.0 * np.pi * (((c[:, None] % FNET_GROUP_DIM) * (c[None, :] % FNET_GROUP_DIM)) % FNET_GROUP_DIM) / FNET_GROUP_DIM
    cb = np.where(same, np.cos(ang), 0.0)
    sb = np.where(same, np.sin(ang), 0.0)
    return jnp.asarray(np.concatenate([cb, -sb], axis=1), dtype=F32).astype(BF)


def _filter_features(n):
    t = np.linspace(0.0, 1.0, n)[:, None]
    w = 2.0 * np.pi * np.arange(n)[:, None] / n
    fb = np.linspace(1e-4, FILTER_BANDS - 1, FILTER_BANDS)
    z = np.concatenate([t, np.cos(fb * w), -np.sin(fb * w)], axis=-1)
    z = np.pad(z, ((0, 0), (0, 64 - z.shape[1])))
    return jnp.asarray(np.concatenate([z, np.zeros((1, z.shape[1])), z[:0:-1]], axis=0), dtype=F32)


def _decay_rates():
    d = jnp.abs(jnp.linspace(math.log(DECAY_TARGET) / SLOW_DECAY_PCT, math.log(DECAY_TARGET) / FAST_DECAY_PCT,
                             HYENA_WIDTH, dtype=F32))
    return jnp.concatenate([d] * HYENA_ORDER)[None, :]


def kernel(x, c, ctx, c_ctx, w_ada, b_ada, norm1_g, norm2_g, w_in, q_norm_g, k_norm_g, attn_sink,
           hy_conv_w, hy_conv_b, hy_filt_w1, hy_filt_b1, hy_filt_freq1, hy_filt_w2, hy_filt_b2,
           hy_filt_freq2, hy_filt_w3, hy_bias, w_proj_attn, w_proj_hyena, w_proj_fnet, w_out,
           ffn_w_gate, ffn_w_up, ffn_w_down, moe_router, moe_w_gate, moe_w_up, moe_w_down):
    b, seq, d = x.shape
    n_ctx = ctx.shape[1]
    depth = w_ada.shape[0]
    tm = 512
    tiles_per_seq = seq // tm

    cond8 = jnp.concatenate([c, c_ctx[None, :], jnp.zeros((8 - b - 1, d), F32)], axis=0)
    mods = _adaln(cond8, w_ada, b_ada)

    cos_l, sin_l = _rope_tables(seq)
    cos_c = jnp.ones((n_ctx, LANES), F32)
    sin_c = jnp.zeros((n_ctx, LANES), F32)
    gsum = _head_sum_matrix()
    mfn = _fnet_channel_matrix()
    deltas = _decay_rates()
    zfeat_l = _filter_features(seq)
    zfeat_c = _filter_features(n_ctx)

    lat_row = lambda i: i // tiles_per_seq
    ctx_row = lambda i: b
    lat_tab = lambda i: i % tiles_per_seq
    ctx_tab = lambda i: 0
    tm_c = min(tm, n_ctx)

    xs = x.reshape(b * seq, d)
    cs = ctx.reshape(b * n_ctx, d)
    for l in range(depth):
        last = l == depth - 1
        mod = lambda j: mods[l, :, j * d:(j + 1) * d].reshape(8, 1, d)
        w_in_bf = w_in[l].astype(BF)
        qg = jnp.tile(q_norm_g[l], N_HEADS)[None, :]
        kg = jnp.tile(k_norm_g[l], N_KV_HEADS)[None, :]
        g1 = norm1_g[l][None, :]
        wa, wh, wf, wo = (w_proj_attn[l].astype(BF), w_proj_hyena[l].astype(BF),
                          w_proj_fnet[l].astype(BF), w_out[l].astype(BF))
        conv_w = hy_conv_w[l].reshape(3, -1)
        conv_b = hy_conv_b[l][None, :]
        w1p = jnp.pad(hy_filt_w1[l], ((0, 64 - hy_filt_w1.shape[1]), (0, 0)))
        filt = (w1p, hy_filt_b1[l][None, :], hy_filt_freq1[l][None, :], hy_filt_w2[l],
                hy_filt_b2[l][None, :], hy_filt_freq2[l][None, :], hy_filt_w3[l], deltas)

        q_c, kd_c, vd_c, u_c, pq_c, gates_c = _phase_a(
            cs, mod(0), mod(1), g1, w_in_bf, cos_c, sin_c, qg, kg, gsum, mfn,
            tm=tm_c, mod_row=ctx_row, tab_row=ctx_tab)
        kd_c3 = kd_c.reshape(b, n_ctx, -1)
        vd_c3 = vd_c.reshape(b, n_ctx, -1)

        q_l, kd_l, vd_l, u_l, pq_l, gates_l = _phase_a(
            xs, mod(0), mod(1), g1, w_in_bf, cos_l, sin_l, qg, kg, gsum, mfn,
            tm=tm, mod_row=lat_row, tab_row=lat_tab)
        attn_l = _attention(attn_sink[l], q_l.reshape(b, seq, -1), kd_l.reshape(b, seq, -1),
                            vd_l.reshape(b, seq, -1), kd_c3, vd_c3, local=True, tq=512)
        h_l, nrm_l = _hy_filter(zfeat_l, *filt, tm=512)
        uc_l = _hy_prep(u_l.reshape(b, seq, -1), conv_w, conv_b)
        hy_l = _hyena(uc_l, h_l, nrm_l, hy_bias[l])
        fn_l = _fnet(pq_l.reshape(b, seq, -1))
        xs = _merge(xs, mod(2), attn_l.reshape(b * seq, -1), hy_l, fn_l, gates_l, wa, wh, wf, wo,
                    tm=tm, mod_row=lat_row)

        if not last:
            attn_c = _attention(attn_sink[l], q_c.reshape(b, n_ctx, -1), kd_c3, vd_c3, kd_c3, vd_c3,
                                local=False, tq=n_ctx)
            h_c, nrm_c = _hy_filter(zfeat_c, *filt, tm=n_ctx)
            uc_c = _hy_prep(u_c.reshape(b, n_ctx, -1), conv_w, conv_b)
            hy_c = _hyena(uc_c, h_c, nrm_c, hy_bias[l])
            fn_c = _fnet(pq_c.reshape(b, n_ctx, -1))
            cs = _merge(cs, mod(2), attn_c.reshape(b * n_ctx, -1), hy_c, fn_c, gates_c, wa, wh, wf, wo,
                        tm=tm_c, mod_row=ctx_row)

        g2 = norm2_g[l][None, :]
        i = l // 2
        if l % 2 == 0:
            wg, wu, wd = ffn_w_gate[i].astype(BF), ffn_w_up[i].astype(BF), ffn_w_down[i].astype(BF)
            run = lambda t2, rows, tmm: _ffn(t2, mod(3), mod(4), mod(5), g2, wg, wu, wd,
                                             tm=tmm, tf=D_FF // 2, mod_row=rows)
        else:
            wr = jnp.pad(moe_router[i], ((0, 0), (0, LANES - N_EXPERTS)))
            wg, wu, wd = moe_w_gate[i].astype(BF), moe_w_up[i].astype(BF), moe_w_down[i].astype(BF)
            run = lambda t2, rows, tmm: _moe(t2, mod(3), mod(4), mod(5), g2, wr, wg, wu, wd,
                                             tb=tmm, mod_row=rows)
        tm_ffn = 512 if l % 2 == 0 else 1024
        xs = run(xs, lambda t: t // (seq // tm_ffn), tm_ffn)
        if not last:
            cs = run(cs, ctx_row, min(tm_ffn, b * n_ctx))
    return xs.reshape(b, seq, d)
```

```python
import functools
import math

import numpy as np
import jax
import jax.numpy as jnp
from jax import lax
from jax.experimental import pallas as pl
from jax.experimental.pallas import tpu as pltpu

F32 = jnp.float32
BF = jnp.bfloat16

D_MODEL = 1024
DEPTH = 4
GRID_W = 64
HEAD_DIM = 64
N_HEADS = 8
N_KV_HEADS = 2
ATTN_WIDTH = N_HEADS * HEAD_DIM
KV_WIDTH = N_KV_HEADS * HEAD_DIM
WINDOW = 128
QBLK = 128
ROPE_THETA = 10000.0
HYENA_ORDER = 2
HYENA_WIDTH = 256
FILTER_BANDS = 16
FILTER_HIDDEN = 64
DECAY_TARGET = 1e-2
FAST_DECAY_PCT = 0.3
SLOW_DECAY_PCT = 1.5
FNET_WIDTH = 256
FNET_GROUP_DIM = 64
Q_END = ATTN_WIDTH
K_END = Q_END + KV_WIDTH
V_END = K_END + KV_WIDTH
HY_END = V_END + (HYENA_ORDER + 1) * HYENA_WIDTH
FN_END = HY_END + FNET_WIDTH
IN_WIDTH = FN_END + 3 * D_MODEL
D_FF = 2816
N_EXPERTS = 8
EPS = 1e-6
LANES = 128
NEG = -1e30
STAGE_ROWS = 16

VMEM_LIMIT = 56 * 1024 * 1024


def _cparams(*sem):
    return pltpu.CompilerParams(dimension_semantics=sem, vmem_limit_bytes=VMEM_LIMIT)


def _dot(a, b):
    return jnp.dot(a, b, preferred_element_type=F32)


def _dot_nt(a, b):
    return lax.dot_general(a, b, (((1,), (1,)), ((), ())), preferred_element_type=F32)


def _split(a):
    hi = a.astype(BF)
    lo = (a - hi.astype(F32)).astype(BF)
    return hi, lo


def _dot3(a, b):
    ah, al = _split(a)
    bh, bl = _split(b)
    return _dot(ah, bh) + (_dot(ah, bl) + _dot(al, bh))


def _dot2(a, b_bf16):
    ah, al = _split(a)
    return _dot(ah, b_bf16) + _dot(al, b_bf16)


def _sigmoid(v):
    return 0.5 * jnp.tanh(0.5 * v) + 0.5


def _silu(v):
    return v * _sigmoid(v)


def _adaln_kernel(c_ref, w_ref, b_ref, o_ref):
    o_ref[0] = _dot3(_silu(c_ref[...]), w_ref[0]) + b_ref[0]


def _adaln(cond8, w_ada, b_ada):
    depth, d, n6 = w_ada.shape
    tn = 1024
    return pl.pallas_call(
        _adaln_kernel,
        grid=(depth, n6 // tn),
        in_specs=[
            pl.BlockSpec((8, d), lambda l, j: (0, 0)),
            pl.BlockSpec((1, d, tn), lambda l, j: (l, 0, j)),
            pl.BlockSpec((1, 1, tn), lambda l, j: (l, 0, j)),
        ],
        out_specs=pl.BlockSpec((1, 8, tn), lambda l, j: (l, 0, j)),
        out_shape=jax.ShapeDtypeStruct((depth, 8, n6), F32),
        compiler_params=_cparams("parallel", "parallel"),
        name="adaln",
    )(cond8, w_ada, b_ada.reshape(depth, 1, n6))


def _modulated_norm(x, g, sc, sh):
    ms = jnp.mean(x * x, axis=-1, keepdims=True)
    h = (x * lax.rsqrt(ms + EPS)) * g
    return h * (1.0 + sc) + sh


def _phase_a_kernel(x_ref, sh_ref, sc_ref, g_ref, w_ref, cos_ref, sin_ref, qg_ref, kg_ref,
                    gsum_ref, mfn_ref, q_ref, kd_ref, vd_ref, u_ref, pq_ref, gate_ref):
    tm = x_ref.shape[0]
    hb = _modulated_norm(x_ref[...], g_ref[...], sc_ref[0], sh_ref[0]).astype(BF)
    cos = cos_ref[...]
    sin = sin_ref[...]

    def headnorm(t, gain, gs):
        ss = _dot2(t * t, gs)
        return t * lax.rsqrt(ss * (1.0 / HEAD_DIM) + EPS) * gain

    def rope(t, cosw, sinw):
        w = t.shape[1]
        nxt = pltpu.roll(t, w - 16, axis=1)
        prv = pltpu.roll(t, 16, axis=1)
        lw = lax.broadcasted_iota(jnp.int32, t.shape, 1)
        return t * cosw + jnp.where((lw % 32) < 16, nxt, prv) * sinw

    def dup_halves(t):
        lane = lax.broadcasted_iota(jnp.int32, t.shape, 1)
        sw = pltpu.roll(t, 64, axis=1)
        lo = lane < 64
        return jnp.concatenate([jnp.where(lo, t, sw), jnp.where(lo, sw, t)], axis=1)

    pq = _dot(hb, w_ref[:, 0:Q_END])
    qn = headnorm(pq, qg_ref[...], gsum_ref[...])
    cos4 = jnp.concatenate([cos] * 4, axis=1)
    sin4 = jnp.concatenate([sin] * 4, axis=1)
    q_ref[...] = (rope(qn, cos4, sin4) * (HEAD_DIM ** -0.5)).astype(BF)

    pk = _dot(hb, w_ref[:, Q_END:K_END])
    kn = headnorm(pk, kg_ref[...], gsum_ref[0:KV_WIDTH, 0:KV_WIDTH])
    kd_ref[...] = dup_halves(rope(kn, cos, sin)).astype(BF)
    vd_ref[...] = dup_halves(_dot(hb, w_ref[:, K_END:V_END])).astype(BF)

    u_ref[...] = _dot(hb, w_ref[:, V_END:HY_END])
    f = _dot(hb, w_ref[:, HY_END:FN_END])
    pq_ref[...] = _dot(f.astype(BF), mfn_ref[...])
    for i in range(3):
        lo = FN_END + i * D_MODEL
        gate_ref[:, i * D_MODEL:(i + 1) * D_MODEL] = _sigmoid(
            _dot(hb, w_ref[:, lo:lo + D_MODEL])).astype(BF)


def _phase_a(x2, sh, sc, g, w_in_bf, cos_t, sin_t, qg, kg, gsum, mfn, *, tm, mod_row, tab_row):
    t, d = x2.shape
    row3 = lambda i: (mod_row(i), 0, 0)
    full = lambda i: (0, 0)
    tok = lambda i: (i, 0)
    outs = [
        ((t, ATTN_WIDTH), BF), ((t, 2 * KV_WIDTH), BF), ((t, 2 * KV_WIDTH), BF),
        ((t, 3 * HYENA_WIDTH), F32), ((t, 2 * FNET_WIDTH), F32), ((t, 3 * D_MODEL), BF),
    ]
    return pl.pallas_call(
        _phase_a_kernel,
        grid=(t // tm,),
        in_specs=[
            pl.BlockSpec((tm, d), tok),
            pl.BlockSpec((1, 1, d), row3),
            pl.BlockSpec((1, 1, d), row3),
            pl.BlockSpec((1, d), full),
            pl.BlockSpec((d, IN_WIDTH), full, pipeline_mode=pl.Buffered(1)),
            pl.BlockSpec((tm, LANES), lambda i: (tab_row(i), 0)),
            pl.BlockSpec((tm, LANES), lambda i: (tab_row(i), 0)),
            pl.BlockSpec((1, ATTN_WIDTH), full),
            pl.BlockSpec((1, KV_WIDTH), full),
            pl.BlockSpec((ATTN_WIDTH, ATTN_WIDTH), full),
            pl.BlockSpec((FNET_WIDTH, 2 * FNET_WIDTH), full),
        ],
        out_specs=[pl.BlockSpec((tm, s[1]), tok) for s, _ in outs],
        out_shape=[jax.ShapeDtypeStruct(s, dt) for s, dt in outs],
        compiler_params=_cparams("parallel"),
        name="phase_a",
    )(x2, sh, sc, g, w_in_bf, cos_t, sin_t, qg, kg, gsum, mfn)


def _attn_kernel(sink_ref, q_ref, kd_ref, vd_ref, kc_ref, vc_ref, o_ref, *, local, seq_len):
    tq = q_ref.shape[1]
    nblk = tq // QBLK
    gq = N_HEADS // N_KV_HEADS
    rows = gq * QBLK
    lane = lax.broadcasted_iota(jnp.int32, (QBLK, LANES), 1)
    lo_half = lane < 64
    hrow = lax.broadcasted_iota(jnp.int32, (rows, 1), 0) // QBLK
    nband = 3 * QBLK
    if local:
        qk_off = (lax.broadcasted_iota(jnp.int32, (rows, nband), 0) % QBLK
                  - lax.broadcasted_iota(jnp.int32, (rows, nband), 1))
    for blk in range(nblk):
        r0 = blk * QBLK
        qb = q_ref[0, r0:r0 + QBLK, :]
        if local:
            n = pl.program_id(1) * nblk + blk
            start = pl.multiple_of(jnp.clip((n - 1) * QBLK, 0, seq_len - nband), QBLK)
            valid = jnp.abs(qk_off + (n * QBLK - start)) <= WINDOW
        for g in range(N_KV_HEADS):
            parts = []
            for hh in range(gq):
                h = gq * g + hh
                qc = qb[:, (h // 2) * LANES:(h // 2 + 1) * LANES]
                keep = lo_half if h % 2 == 0 else jnp.logical_not(lo_half)
                parts.append(jnp.where(keep, qc, jnp.zeros_like(qc)))
            q4 = jnp.concatenate(parts, axis=0)
            sk = jnp.full((rows, 1), sink_ref[gq * g + gq - 1], F32)
            for hh in range(gq - 2, -1, -1):
                sk = jnp.where(hrow == hh, sink_ref[gq * g + hh], sk)
            gl = slice(g * LANES, (g + 1) * LANES)
            s_ctx = _dot_nt(q4, kc_ref[0, :, gl])
            m = jnp.maximum(jnp.max(s_ctx, axis=1, keepdims=True), sk)
            if local:
                s_loc = _dot_nt(q4, kd_ref[0, pl.ds(start, nband), gl])
                s_loc = jnp.where(valid, s_loc, NEG)
                m = jnp.maximum(m, jnp.max(s_loc, axis=1, keepdims=True))
            p_ctx = jnp.exp(s_ctx - m)
            den = jnp.sum(p_ctx, axis=1, keepdims=True) + jnp.exp(sk - m)
            o = _dot(p_ctx.astype(BF), vc_ref[0, :, gl])
            if local:
                p_loc = jnp.exp(s_loc - m)
                den = den + jnp.sum(p_loc, axis=1, keepdims=True)
                o = o + _dot(p_loc.astype(BF), vd_ref[0, pl.ds(start, nband), gl])
            o = o / den
            for cc in range(gq // 2):
                col = (gq // 2) * g + cc
                oa = o[(2 * cc) * QBLK:(2 * cc + 1) * QBLK]
                ob = o[(2 * cc + 1) * QBLK:(2 * cc + 2) * QBLK]
                o_ref[0, r0:r0 + QBLK, col * LANES:(col + 1) * LANES] = (
                    jnp.where(lo_half, oa, ob).astype(BF))


def _attention(sink, q, kd, vd, kc, vc, *, local, tq):
    b, lq, _ = q.shape
    lk = kd.shape[1]
    c = kc.shape[1]
    kern = functools.partial(_attn_kernel, local=local, seq_len=lk)
    return pl.pallas_call(
        kern,
        grid=(b, lq // tq),
        in_specs=[
            pl.BlockSpec(memory_space=pltpu.SMEM),
            pl.BlockSpec((1, tq, ATTN_WIDTH), lambda bi, i: (bi, i, 0)),
            pl.BlockSpec((1, lk, 2 * KV_WIDTH), lambda bi, i: (bi, 0, 0)),
            pl.BlockSpec((1, lk, 2 * KV_WIDTH), lambda bi, i: (bi, 0, 0)),
            pl.BlockSpec((1, c, 2 * KV_WIDTH), lambda bi, i: (bi, 0, 0)),
            pl.BlockSpec((1, c, 2 * KV_WIDTH), lambda bi, i: (bi, 0, 0)),
        ],
        out_specs=pl.BlockSpec((1, tq, ATTN_WIDTH), lambda bi, i: (bi, i, 0)),
        out_shape=jax.ShapeDtypeStruct((b, lq, ATTN_WIDTH), BF),
        compiler_params=_cparams("parallel", "parallel"),
        name="attn_local" if local else "attn_ctx",
    )(sink, q, kd, vd, kc, vc)


def _hy_prep_kernel(u_ref, w_ref, b_ref, o_ref):
    u = u_ref[0]
    n = u.shape[0]
    row = lax.broadcasted_iota(jnp.int32, u.shape, 0)
    prv = jnp.where(row == 0, 0.0, pltpu.roll(u, 1, axis=0))
    nxt = jnp.where(row == n - 1, 0.0, pltpu.roll(u, n - 1, axis=0))
    o_ref[0] = prv * w_ref[0:1, :] + u * w_ref[1:2, :] + nxt * w_ref[2:3, :] + b_ref[...]


def _hy_prep(u, conv_w, conv_b):
    b, n, w = u.shape
    return pl.pallas_call(
        _hy_prep_kernel,
        grid=(b, w // LANES),
        in_specs=[
            pl.BlockSpec((1, n, LANES), lambda bi, j: (bi, 0, j)),
            pl.BlockSpec((3, LANES), lambda bi, j: (0, j)),
            pl.BlockSpec((1, LANES), lambda bi, j: (0, j)),
        ],
        out_specs=pl.BlockSpec((1, n, LANES), lambda bi, j: (bi, 0, j)),
        out_shape=jax.ShapeDtypeStruct((b, n, w), F32),
        compiler_params=_cparams("parallel", "parallel"),
        name="hy_prep",
    )(u, conv_w, conv_b)


def _hy_filter_kernel(z_ref, w1_ref, b1_ref, f1_ref, w2_ref, b2_ref, f2_ref, w3_ref, dl_ref,
                      k_ref, nrm_ref, *, n_half):
    i = pl.program_id(0)
    tm = z_ref.shape[0]
    z = z_ref[...]
    h = jnp.sin(f1_ref[...] * (_dot3(z, w1_ref[...]) + b1_ref[...]))
    h = jnp.sin(f2_ref[...] * (_dot3(h, w2_ref[...]) + b2_ref[...]))
    h = _dot3(h, w3_ref[...])
    h = h * jnp.exp(-z[:, 0:1] * dl_ref[...])
    row = i * tm + lax.broadcasted_iota(jnp.int32, (tm, 1), 0)
    h = jnp.where(row == n_half, 0.0, h)
    k_ref[...] = h

    @pl.when(i == 0)
    def _():
        nrm_ref[...] = jnp.zeros_like(nrm_ref)

    nrm_ref[...] += jnp.sum(jnp.abs(h), axis=0, keepdims=True)


def _hy_filter(zfeat2, w1p, b1, f1, w2, b2, f2, w3, deltas2, *, tm):
    n2 = zfeat2.shape[0]
    n_half = n2 // 2
    wout = w3.shape[1] // 2
    full = lambda i: (0, 0)
    kern = functools.partial(_hy_filter_kernel, n_half=n_half)
    return pl.pallas_call(
        kern,
        grid=(n2 // tm,),
        in_specs=[
            pl.BlockSpec((tm, zfeat2.shape[1]), lambda i: (i, 0)),
            pl.BlockSpec(w1p.shape, full), pl.BlockSpec(b1.shape, full), pl.BlockSpec(f1.shape, full),
            pl.BlockSpec(w2.shape, full), pl.BlockSpec(b2.shape, full), pl.BlockSpec(f2.shape, full),
            pl.BlockSpec((w3.shape[0], wout), lambda i: (0, (i * tm) // n_half)),
            pl.BlockSpec(deltas2.shape, full),
        ],
        out_specs=[pl.BlockSpec((tm, wout), lambda i: (i, 0)), pl.BlockSpec((1, wout), full)],
        out_shape=[jax.ShapeDtypeStruct((n2, wout), F32), jax.ShapeDtypeStruct((1, wout), F32)],
        compiler_params=_cparams("arbitrary"),
        name="hy_filter",
    )(zfeat2, w1p, b1, f1, w2, b2, f2, w3, deltas2)


def _dft_mats(k_out, r_in, period, sign, scale, n_in, real_out):
    k = np.arange(k_out)[:, None]
    r = np.arange(r_in)[None, :]
    ang = 2.0 * np.pi * ((k * r) % period) / period
    fr = np.cos(ang) * scale
    fi = sign * np.sin(ang) * scale
    if real_out:
        mats = [fr, -fi]
    else:
        mats = [np.concatenate([fr, fi], 0), np.concatenate([-fi, fr], 0)]
    return jnp.asarray(np.stack(mats[:n_in], 0), dtype=F32).astype(BF)


def _twiddle(s_n, k_n, n, sign, sbk):
    s0 = lax.broadcasted_iota(jnp.int32, (s_n // sbk, k_n, LANES), 0) * sbk
    k = lax.broadcasted_iota(jnp.int32, (s_n // sbk, k_n, LANES), 1)
    ang = (s0 * k).astype(F32) * (2.0 * math.pi / n)
    ang1 = lax.broadcasted_iota(jnp.int32, (k_n, LANES), 0).astype(F32) * (2.0 * math.pi / n)
    return jnp.cos(ang), sign * jnp.sin(ang), jnp.cos(ang1), sign * jnp.sin(ang1)


def _unpack_pair(p):
    return [pltpu.unpack_elementwise(p, index=i, packed_dtype=BF, unpacked_dtype=F32) for i in (0, 1)]


def _stage_kernel(*refs, n_in, r_in, k_mid, k_out, sbk, tw, spec, second, gate, real_out,
                  transposed_out, flat, packed_in, packed_spec, packed_out):
    it = iter(refs)
    x_refs = [next(it) for _ in range(n_in)]
    g_ref = next(it)
    g2_ref = next(it) if second else None
    tw_refs = [next(it) for _ in range(4)] if tw else None
    n_spec = 1 if packed_spec else 2
    spec_refs = [next(it) for _ in range(n_spec + 1)] if spec else None
    gate_refs = [next(it) for _ in range(5)] if gate else None
    out_refs = [next(it)] if (real_out or gate or packed_out) else [next(it), next(it)]
    if not flat:
        x_refs = [r.reshape(r_in * sbk, LANES) for r in x_refs]
        if spec:
            spec_refs = [r.reshape(k_mid * sbk, LANES) for r in spec_refs[:n_spec]] + spec_refs[n_spec:]
        if gate:
            gate_refs = [r.reshape(k_out * sbk, LANES) for r in gate_refs[:4]] + gate_refs[4:]
            out_refs = [out_refs[0].reshape(2 * k_out * sbk, LANES)]
        elif not transposed_out:
            out_refs = [r.reshape(k_out * sbk, LANES) for r in out_refs]
    if spec:
        inv = 1.0 / spec_refs[n_spec][...]
    if tw:
        tr, ti = tw_refs[0][0], tw_refs[1][0]
        wr, wi = tw_refs[2][...], tw_refs[3][...]
    for j in range(sbk):
        parts = [x_ref[...] if flat else x_ref[pl.ds(j, r_in, stride=sbk), :] for x_ref in x_refs]
        if packed_in:
            parts = _unpack_pair(parts[0])
        acc = None
        for xi, xv in enumerate(parts):
            d = _dot(g_ref[xi], xv.astype(BF))
            acc = d if acc is None else acc + d
        if real_out:
            ys = [acc]
        else:
            yr, yi = acc[:k_mid], acc[k_mid:]
            if spec:
                rows = slice(None) if flat else pl.ds(j, k_mid, stride=sbk)
                if packed_spec:
                    sr, si = _unpack_pair(spec_refs[0][rows, :])
                else:
                    sr, si = spec_refs[0][rows, :], spec_refs[1][rows, :]
                sr, si = sr * inv, si * inv
                yr, yi = yr * sr - yi * si, yr * si + yi * sr
            if second:
                acc = _dot(g2_ref[0], yr.astype(BF)) + _dot(g2_ref[1], yi.astype(BF))
                yr, yi = acc[:k_out], acc[k_out:]
            if tw:
                yr, yi = yr * tr - yi * ti, yr * ti + yi * tr
                if j + 1 < sbk:
                    tr, ti = tr * wr - ti * wi, tr * wi + ti * wr
            ys = [yr, yi]
        if gate:
            o_ref = out_refs[0]
            for part, y in enumerate(ys):
                rows = slice(None) if flat else pl.ds(j, k_out, stride=sbk)
                val = gate_refs[part][rows, :] * (y + gate_refs[4][...] * gate_refs[2 + part][rows, :])
                if flat:
                    o_ref[part] = val.astype(o_ref.dtype)
                else:
                    o_ref[pl.ds(part * k_out * sbk + j, k_out, stride=sbk), :] = val.astype(o_ref.dtype)
            continue
        if packed_out:
            ys = [pltpu.pack_elementwise(ys, packed_dtype=BF)]
        for o_ref, y in zip(out_refs, ys):
            if flat:
                o_ref[...] = y.astype(o_ref.dtype)
            elif transposed_out:
                o_ref[0, j] = y.astype(o_ref.dtype)
            else:
                o_ref[pl.ds(j, k_out, stride=sbk), :] = y.astype(o_ref.dtype)


def _fft_stage(xs, x_sel, gmat, *, r_in, s_n, k_out, n_groups, n_cblk, transposed_out, real_out,
               out_dtype=F32, g2mat=None, tw=None, spec=None, spec_sel=None, gate=None, sbk=STAGE_ROWS,
               packed_in=False, packed_out=False, name="fft_stage"):
    n_in = len(xs)
    flat = s_n == 1
    sbk = 1 if flat else min(sbk, s_n)
    cb = LANES
    in_specs, args = [], []
    for x, sel in zip(xs, x_sel):
        if flat:
            in_specs.append(pl.BlockSpec((None, r_in, cb), lambda s, g, c, sel=sel: (sel(g, c)[0], 0, sel(g, c)[1])))
            args.append(x)
        else:
            xv = x.reshape(x.shape[0], x.shape[1] // s_n, s_n, x.shape[2])
            in_specs.append(pl.BlockSpec((1, r_in, sbk, cb),
                                         lambda s, g, c, sel=sel: (sel(g, c)[0], 0, s, sel(g, c)[1])))
            args.append(xv)
    in_specs.append(pl.BlockSpec(gmat.shape, lambda s, g, c: (0, 0, 0)))
    args.append(gmat)
    k_mid = gmat.shape[1] // (1 if real_out else 2)
    if g2mat is not None:
        in_specs.append(pl.BlockSpec(g2mat.shape, lambda s, g, c: (0, 0, 0)))
        args.append(g2mat)
    if tw is not None:
        for tarr in tw[:2]:
            in_specs.append(pl.BlockSpec((1, k_out, LANES), lambda s, g, c: (s, 0, 0)))
            args.append(tarr)
        for tarr in tw[2:]:
            in_specs.append(pl.BlockSpec((k_out, LANES), lambda s, g, c: (0, 0)))
            args.append(tarr)
    if spec is not None:
        *planes, nrm = spec
        for arr in planes:
            if flat:
                in_specs.append(pl.BlockSpec((k_mid, cb), lambda s, g, c: (0, spec_sel(g, c))))
                args.append(arr)
            else:
                in_specs.append(pl.BlockSpec((1, k_mid, sbk, cb), lambda s, g, c: (0, 0, s, spec_sel(g, c))))
                args.append(arr.reshape(1, k_mid, s_n, arr.shape[-1]))
        in_specs.append(pl.BlockSpec((1, cb), lambda s, g, c: (0, spec_sel(g, c))))
        args.append(nrm)
    if gate is not None:
        (ga, gblk), (za, zblk), bias = gate
        for arr, blk in ((ga, gblk), (za, zblk)):
            for bi in (0, 1):
                if flat:
                    in_specs.append(pl.BlockSpec((None, k_out, cb), lambda s, g, c, bi=bi, blk=blk: (bi, 0, blk + c)))
                    args.append(arr)
                else:
                    in_specs.append(pl.BlockSpec((1, k_out, sbk, cb),
                                                 lambda s, g, c, bi=bi, blk=blk: (bi, 0, s, blk + c)))
                    args.append(arr.reshape(arr.shape[0], k_out, s_n, arr.shape[-1]))
        in_specs.append(pl.BlockSpec((1, cb), lambda s, g, c: (0, c)))
        args.append(bias)
    ctot = n_cblk * cb
    if gate is not None:
        n_groups = 2
        if flat:
            oshape = (2, k_out, ctot)
            ospec = pl.BlockSpec((2, k_out, cb), lambda s, g, c: (0, 0, c))
        else:
            oshape = (2, k_out, s_n, ctot)
            ospec = pl.BlockSpec((2, k_out, sbk, cb), lambda s, g, c: (0, 0, s, c))
    elif flat:
        oshape = (n_groups, k_out, ctot)
        ospec = pl.BlockSpec((None, k_out, cb), lambda s, g, c: (g, 0, c))
    elif transposed_out:
        oshape = (n_groups, s_n, k_out, ctot)
        ospec = pl.BlockSpec((1, sbk, k_out, cb), lambda s, g, c: (g, s, 0, c))
    else:
        oshape = (n_groups, k_out, s_n, ctot)
        ospec = pl.BlockSpec((1, k_out, sbk, cb), lambda s, g, c: (g, 0, s, c))
    n_out = 1 if (real_out or gate is not None or packed_out) else 2
    if packed_out:
        out_dtype = jnp.int32
    kern = functools.partial(_stage_kernel, n_in=n_in, r_in=r_in, k_mid=k_mid, k_out=k_out, sbk=sbk,
                             tw=tw is not None, spec=spec is not None, second=g2mat is not None,
                             gate=gate is not None, real_out=real_out, transposed_out=transposed_out,
                             flat=flat, packed_in=packed_in, packed_spec=spec is not None and len(spec) == 2,
                             packed_out=packed_out)
    n_grid_groups = 1 if gate is not None else n_groups
    outs = pl.pallas_call(
        kern,
        grid=(s_n // sbk, n_grid_groups, n_cblk),
        in_specs=in_specs,
        out_specs=[ospec] * n_out,
        out_shape=[jax.ShapeDtypeStruct(oshape, out_dtype)] * n_out,
        compiler_params=_cparams("parallel", "parallel", "parallel"),
        name=name,
    )(*args)
    return [o.reshape(n_groups, -1, ctot) for o in outs]


def _split_len(n):
    if n <= 1024:
        return n, 1
    s = 128
    return n // s, s


def _fft_forward(xs, x_sel, n, n_rows, *, n_groups, n_cblk, name="fwd"):
    n1, s = _split_len(n)
    n_in = len(xs)
    if s == 1:
        g = _dft_mats(n, n_rows, n, -1.0, 1.0, n_in, False)
        return _fft_stage(xs, x_sel, g, r_in=n_rows, s_n=1, k_out=n, n_groups=n_groups, n_cblk=n_cblk,
                          transposed_out=False, real_out=False, name=name + "_direct")
    r1 = n_rows // s
    g1 = _dft_mats(n1, r1, n1, -1.0, 1.0, n_in, False)
    tw = _twiddle(s, n1, n, -1.0, STAGE_ROWS)
    (a,) = _fft_stage(xs, x_sel, g1, r_in=r1, s_n=s, k_out=n1, n_groups=n_groups, n_cblk=n_cblk,
                      transposed_out=True, real_out=False, tw=tw, packed_out=True, name=name + "_s1")
    g2 = _dft_mats(s, s, s, -1.0, 1.0, 2, False)
    return _fft_stage([a], [lambda g, c: (g, c)], g2, r_in=s, s_n=n1, k_out=s, n_groups=n_groups,
                      n_cblk=n_cblk, transposed_out=False, real_out=False, packed_in=True, packed_out=True,
                      name=name + "_s2")


def _hyena(uc, k_filt, nrm, hy_bias):
    b, n, _ = uc.shape
    w = HYENA_WIDTH
    wblk = w // LANES
    nfft = 2 * n
    n1, s = _split_len(nfft)
    ident = lambda g, c: (g, c)
    k_spec = _fft_forward([k_filt[None]], [lambda g, c: (0, c)], nfft, nfft, n_groups=1,
                          n_cblk=HYENA_ORDER * wblk, name="hy_filt_fft")
    z, zblk = uc, 2 * wblk
    for o in range(HYENA_ORDER):
        sel_r = lambda g, c, zblk=zblk: (0, zblk + c)
        sel_i = lambda g, c, zblk=zblk: (1, zblk + c)
        spec = tuple(p[0] for p in k_spec) + (nrm,)
        spec_sel = lambda g, c, o=o: o * wblk + c
        gate = ((uc, o * wblk), (z, zblk), hy_bias[o:o + 1])
        common = dict(n_groups=1, n_cblk=wblk, real_out=False)
        if s == 1:
            gf = _dft_mats(nfft, n, nfft, -1.0, 1.0, 2, False)
            gi = _dft_mats(n, nfft, nfft, 1.0, 1.0 / nfft, 2, False)
            (z,) = _fft_stage([z, z], [sel_r, sel_i], gf, r_in=n, s_n=1, k_out=n, transposed_out=False,
                              g2mat=gi, spec=spec, spec_sel=spec_sel, gate=gate, name="hy_direct", **common)
        else:
            r1 = n // s
            g1 = _dft_mats(n1, r1, n1, -1.0, 1.0, 2, False)
            (a,) = _fft_stage([z, z], [sel_r, sel_i], g1, r_in=r1, s_n=s, k_out=n1, transposed_out=True,
                              tw=_twiddle(s, n1, nfft, -1.0, STAGE_ROWS), packed_out=True, name="hy_s1",
                              **common)
            g2 = _dft_mats(s, s, s, -1.0, 1.0, 2, False)
            g3 = _dft_mats(s, s, s, 1.0, 1.0, 2, False)
            (q,) = _fft_stage([a], [ident], g2, r_in=s, s_n=n1, k_out=s, transposed_out=True,
                              g2mat=g3, tw=_twiddle(n1, s, nfft, 1.0, STAGE_ROWS), spec=spec,
                              spec_sel=spec_sel, packed_in=True, packed_out=True, name="hy_mid", **common)
            g4 = _dft_mats(n // s, n1, n1, 1.0, 1.0 / nfft, 2, False)
            (z,) = _fft_stage([q], [ident], g4, r_in=n1, s_n=s, k_out=n // s, transposed_out=False,
                              gate=gate, packed_in=True, name="hy_last", **common)
        zblk = 0
    return z.reshape(b * n, w)


def _fnet(pq):
    b, n, _ = pq.shape
    w = FNET_WIDTH
    wblk = w // LANES
    scale = 1.0 / math.sqrt(n * FNET_GROUP_DIM)
    sel_r = lambda g, c: (g, c)
    sel_i = lambda g, c: (g, wblk + c)
    ident = lambda g, c: (g, c)
    n1, s = _split_len(n)
    if s == 1:
        g = _dft_mats(n, n, n, -1.0, scale, 2, True)
        (y,) = _fft_stage([pq, pq], [sel_r, sel_i], g, r_in=n, s_n=1, k_out=n, n_groups=b, n_cblk=wblk,
                          transposed_out=False, real_out=True, name="fnet_direct")
        return y.reshape(b * n, w)
    g1 = _dft_mats(n1, n1, n1, -1.0, 1.0, 2, False)
    tw = _twiddle(s, n1, n, -1.0, STAGE_ROWS)
    (a,) = _fft_stage([pq, pq], [sel_r, sel_i], g1, r_in=n1, s_n=s, k_out=n1, n_groups=b, n_cblk=wblk,
                      transposed_out=True, real_out=False, tw=tw, packed_out=True, name="fnet_s1")
    g2 = _dft_mats(s, s, s, -1.0, scale, 2, True)
    (y,) = _fft_stage([a], [ident], g2, r_in=s, s_n=n1, k_out=s, n_groups=b, n_cblk=wblk,
                      transposed_out=False, real_out=True, packed_in=True, name="fnet_s2")
    return y.reshape(b * n, w)


def _merge_kernel(x_ref, gt_ref, a_ref, h_ref, f_ref, gate_ref, wa_ref, wh_ref, wf_ref, wo_ref, o_ref):
    d = D_MODEL
    m = gate_ref[:, 0:d].astype(F32) * _dot(a_ref[...], wa_ref[...])
    m = m + gate_ref[:, d:2 * d].astype(F32) * _dot(h_ref[...].astype(BF), wh_ref[...])
    m = m + gate_ref[:, 2 * d:3 * d].astype(F32) * _dot(f_ref[...].astype(BF), wf_ref[...])
    y = _dot(m.astype(BF), wo_ref[...])
    o_ref[...] = x_ref[...] + gt_ref[0] * y


def _merge(x2, gt, attn_o, hy_o, fn_o, gates, wa, wh, wf, wo, *, tm, mod_row):
    t, d = x2.shape
    tok = lambda i: (i, 0)
    full = lambda i: (0, 0)
    return pl.pallas_call(
        _merge_kernel,
        grid=(t // tm,),
        in_specs=[
            pl.BlockSpec((tm, d), tok),
            pl.BlockSpec((1, 1, d), lambda i: (mod_row(i), 0, 0)),
            pl.BlockSpec((tm, ATTN_WIDTH), tok),
            pl.BlockSpec((tm, HYENA_WIDTH), tok),
            pl.BlockSpec((tm, FNET_WIDTH), tok),
            pl.BlockSpec((tm, 3 * d), tok),
            pl.BlockSpec(wa.shape, full, pipeline_mode=pl.Buffered(1)),
            pl.BlockSpec(wh.shape, full, pipeline_mode=pl.Buffered(1)),
            pl.BlockSpec(wf.shape, full, pipeline_mode=pl.Buffered(1)),
            pl.BlockSpec(wo.shape, full, pipeline_mode=pl.Buffered(1)),
        ],
        out_specs=pl.BlockSpec((tm, d), tok),
        out_shape=jax.ShapeDtypeStruct((t, d), F32),
        compiler_params=_cparams("parallel"),
        name="merge",
    )(x2, gt, attn_o, hy_o, fn_o, gates, wa, wh, wf, wo)


def _ffn_kernel(x_ref, sh_ref, sc_ref, gt_ref, g_ref, wg_ref, wu_ref, wd_ref, o_ref, h_scr, acc_scr):
    f = pl.program_id(1)

    @pl.when(f == 0)
    def _():
        h_scr[...] = _modulated_norm(x_ref[...], g_ref[...], sc_ref[0], sh_ref[0]).astype(BF)
        acc_scr[...] = jnp.zeros_like(acc_scr)

    hb = h_scr[...]
    act = _silu(_dot(hb, wg_ref[...])) * _dot(hb, wu_ref[...])
    acc_scr[...] += _dot(act.astype(BF), wd_ref[...])

    @pl.when(f == pl.num_programs(1) - 1)
    def _():
        o_ref[...] = x_ref[...] + gt_ref[0] * acc_scr[...]


def _ffn(x2, sh, sc, gt, g, wg, wu, wd, *, tm, tf, mod_row):
    t, d = x2.shape
    ff = wg.shape[1]
    row3 = lambda i, f: (mod_row(i), 0, 0)
    wmode = dict(pipeline_mode=pl.Buffered(1)) if tf == ff else {}
    return pl.pallas_call(
        _ffn_kernel,
        grid=(t // tm, ff // tf),
        in_specs=[
            pl.BlockSpec((tm, d), lambda i, f: (i, 0)),
            pl.BlockSpec((1, 1, d), row3), pl.BlockSpec((1, 1, d), row3), pl.BlockSpec((1, 1, d), row3),
            pl.BlockSpec((1, d), lambda i, f: (0, 0)),
            pl.BlockSpec((d, tf), lambda i, f: (0, f), **wmode),
            pl.BlockSpec((d, tf), lambda i, f: (0, f), **wmode),
            pl.BlockSpec((tf, d), lambda i, f: (f, 0), **wmode),
        ],
        out_specs=pl.BlockSpec((tm, d), lambda i, f: (i, 0)),
        out_shape=jax.ShapeDtypeStruct((t, d), F32),
        scratch_shapes=[pltpu.VMEM((tm, d), BF), pltpu.VMEM((tm, d), F32)],
        compiler_params=_cparams("parallel", "arbitrary"),
        name="ffn_dense",
    )(x2, sh, sc, gt, g, wg, wu, wd)


def _top2(logits):
    lane = lax.broadcasted_iota(jnp.int32, logits.shape, 1)
    lg = jnp.where(lane < N_EXPERTS, logits, -jnp.inf)
    m1 = jnp.max(lg, axis=1, keepdims=True)
    i1 = jnp.min(jnp.where(lg == m1, lane, LANES), axis=1, keepdims=True)
    lg2 = jnp.where(lane == i1, -jnp.inf, lg)
    m2 = jnp.max(lg2, axis=1, keepdims=True)
    i2 = jnp.min(jnp.where(lg2 == m2, lane, LANES), axis=1, keepdims=True)
    e2 = jnp.exp(m2 - m1)
    w1 = 1.0 / (1.0 + e2)
    return i1, i2, w1, e2 * w1


GROUP_TILE = 256
GROUP_PAD = 64
GROUP_PIECES = GROUP_TILE // GROUP_PAD
SLOT_RADIX = 64.0


def _moe_group_kernel(x_ref, sh_ref, sc_ref, g_ref, wr_ref, xg_ref, ws_ref, slot_ref, cnt_ref,
                      h_scr, rows_scr, wm_scr):
    j = pl.program_id(1)
    tb = x_ref.shape[0]
    gt_rows = xg_ref.shape[1]

    @pl.when(j == 0)
    def _():
        h = _modulated_norm(x_ref[...], g_ref[...], sc_ref[0], sh_ref[0])
        h_scr[...] = h.astype(BF)
        i1, i2, w1, w2 = _top2(_dot3(h, wr_ref[...]))
        lane = lax.broadcasted_iota(jnp.int32, (tb, LANES), 1)
        oh0 = jnp.where(lane == i1, 1.0, 0.0)
        oh1 = jnp.where(lane == i2, 1.0, 0.0)
        c0 = jnp.sum(oh0, axis=0, keepdims=True)
        cnt = c0 + jnp.sum(oh1, axis=0, keepdims=True)
        tri = jnp.where(lax.broadcasted_iota(jnp.int32, (tb, tb), 1)
                        < lax.broadcasted_iota(jnp.int32, (tb, tb), 0), 1.0, 0.0).astype(BF)
        pre0 = _dot(tri, oh0.astype(BF))
        pre1 = _dot(tri, oh1.astype(BF)) + c0
        tiles = jnp.ceil(cnt * (1.0 / GROUP_PAD))
        upper = jnp.where(lax.broadcasted_iota(jnp.int32, (LANES, LANES), 0)
                          < lax.broadcasted_iota(jnp.int32, (LANES, LANES), 1), 1.0, 0.0).astype(BF)
        off = _dot(jnp.broadcast_to(tiles, (8, LANES)).astype(BF), upper)[0:1] * float(GROUP_PAD)
        slot0 = jnp.sum(oh0 * (off + pre0), axis=1, keepdims=True)
        slot1 = jnp.sum(oh1 * (off + pre1), axis=1, keepdims=True)
        slot_ref[0] = jnp.where(lane == 0, slot0, jnp.where(lane == 1, slot1, 0.0))
        cnt_ref[0] = jnp.broadcast_to(cnt, (8, LANES))
        hi0 = jnp.floor(slot0 * (1.0 / SLOT_RADIX))
        hi1 = jnp.floor(slot1 * (1.0 / SLOT_RADIX))
        digits = jnp.where(lane == 0, hi0, jnp.where(lane == 1, slot0 - SLOT_RADIX * hi0,
                           jnp.where(lane == 2, hi1, jnp.where(lane == 3, slot1 - SLOT_RADIX * hi1, 0.0))))
        sel = jnp.where(lax.broadcasted_iota(jnp.int32, (8, LANES), 0)
                        == lax.broadcasted_iota(jnp.int32, (8, LANES), 1), 1.0, 0.0).astype(BF)
        rows_scr[...] = _dot_nt(sel, digits.astype(BF))
        w1h, w1l = _split(w1)
        w1m, w1l = _split(w1 - w1h.astype(F32))
        w2h, w2l = _split(w2)
        w2m, w2l = _split(w2 - w2h.astype(F32))
        cols = [w1h, w1m, w1l, w2h, w2m, w2l]
        wm = jnp.zeros((tb, LANES), F32)
        for li, col in enumerate(cols):
            wm = jnp.where(lane == li, col.astype(F32), wm)
        wm_scr[...] = wm.astype(BF)

    rows = rows_scr[...]
    s0 = rows[0:1] * SLOT_RADIX + rows[1:2]
    s1 = rows[2:3] * SLOT_RADIX + rows[3:4]
    pos = (lax.broadcasted_iota(jnp.int32, (gt_rows, tb), 0) + j * gt_rows).astype(F32)
    g0 = jnp.where(pos == s0, 1.0, 0.0).astype(BF)
    g1 = jnp.where(pos == s1, 1.0, 0.0).astype(BF)
    xg_ref[0] = _dot(g0 + g1, h_scr[...]).astype(BF)
    lane_w = lax.broadcasted_iota(jnp.int32, (gt_rows, LANES), 1)
    wsum = (jnp.where(lane_w < 3, _dot(g0, wm_scr[...]), 0.0)
            + jnp.where((lane_w >= 3) & (lane_w < 6), _dot(g1, wm_scr[...]), 0.0))
    ws_ref[0] = jnp.broadcast_to(jnp.sum(wsum, axis=1, keepdims=True), (gt_rows, LANES))


def _moe_group(x2, sh, sc, g, wr_pad, *, tb, nt, mod_row):
    t, d = x2.shape
    nb = t // tb
    row3 = lambda b, j: (mod_row(b), 0, 0)
    return pl.pallas_call(
        _moe_group_kernel,
        grid=(nb, nt),
        in_specs=[
            pl.BlockSpec((tb, d), lambda b, j: (b, 0)),
            pl.BlockSpec((1, 1, d), row3), pl.BlockSpec((1, 1, d), row3),
            pl.BlockSpec((1, d), lambda b, j: (0, 0)),
            pl.BlockSpec((d, LANES), lambda b, j: (0, 0)),
        ],
        out_specs=[
            pl.BlockSpec((1, GROUP_TILE, d), lambda b, j: (b * nt + j, 0, 0)),
            pl.BlockSpec((1, GROUP_TILE, LANES), lambda b, j: (b * nt + j, 0, 0)),
            pl.BlockSpec((1, tb, LANES), lambda b, j: (b, 0, 0)),
            pl.BlockSpec((1, 8, LANES), lambda b, j: (b, 0, 0)),
        ],
        out_shape=[
            jax.ShapeDtypeStruct((nb * nt, GROUP_TILE, d), BF),
            jax.ShapeDtypeStruct((nb * nt, GROUP_TILE, LANES), F32),
            jax.ShapeDtypeStruct((nb, tb, LANES), F32),
            jax.ShapeDtypeStruct((nb, 8, LANES), F32),
        ],
        scratch_shapes=[pltpu.VMEM((tb, d), BF), pltpu.VMEM((8, tb), F32), pltpu.VMEM((tb, LANES), BF)],
        compiler_params=_cparams("parallel", "arbitrary"),
        name="moe_group",
    )(x2, sh, sc, g, wr_pad)


def _moe_schedule(cnt, nh):
    np_ = GROUP_PIECES
    h = (cnt + GROUP_PAD - 1) // GROUP_PAD
    nb, ne = h.shape
    tot = h.sum(0)
    pairs = (tot + np_ - 1) // np_
    cum_p = jnp.cumsum(pairs)
    start_p = cum_p - pairs
    n_used = cum_p[-1]
    n_steps = (nb * nh + ne * (np_ - 1)) // np_
    q = jnp.minimum(jnp.arange(n_steps, dtype=jnp.int32), n_used - 1)
    e = jnp.sum(q[:, None] >= cum_p[None, :], axis=1).astype(jnp.int32)
    r = q - start_p[e]
    cum_b = jnp.cumsum(h, axis=0)
    first = jnp.cumsum(h, axis=1) - h

    def piece(idx):
        idx = jnp.minimum(idx, tot[e] - 1)
        blk = jnp.sum(idx[:, None] >= cum_b.T[e], axis=1).astype(jnp.int32)
        return blk * nh + first[blk, e] + idx - (cum_b[blk, e] - h[blk, e])

    x = jnp.arange(nh, dtype=jnp.int32)[None, :]
    ex = jnp.sum(x[:, :, None] >= jnp.cumsum(h, axis=1)[:, None, :], axis=2).astype(jnp.int32)
    exc = jnp.minimum(ex, ne - 1)
    g = jnp.take_along_axis(cum_b - h, exc, axis=1) + x - jnp.take_along_axis(first, exc, axis=1)
    loc = np_ * (start_p[exc] + g // np_) + g % np_
    loc = jnp.where(ex < ne, loc, loc[:, 0:1])
    pieces = jnp.concatenate([piece(np_ * r + k) for k in range(np_)]).astype(jnp.int32)
    return (pieces, e, n_used.astype(jnp.int32).reshape(1), loc.reshape(-1).astype(jnp.int32),
            ((h.sum(1) + np_ - 1) // np_).astype(jnp.int32))


def _moe_expert_kernel(pc_ref, exp_ref, nused_ref, *refs):
    np_ = GROUP_PIECES
    x_refs, w_refs = refs[:np_], refs[np_:2 * np_]
    wg_ref, wu_ref, wd_ref, y_ref = refs[2 * np_:]

    @pl.when(pl.program_id(0) < nused_ref[0])
    def _():
        x = jnp.concatenate([r[0] for r in x_refs], axis=0)
        act = _silu(_dot(x, wg_ref[0])) * _dot(x, wu_ref[0])
        y = _dot(act.astype(BF), wd_ref[0])
        w = jnp.concatenate([r[0] for r in w_refs], axis=0)
        y_ref[0] = (y * jnp.concatenate([w] * (y.shape[1] // LANES), axis=1)).astype(BF)


def _moe_experts(pieces, step_exp, n_used, xg, ws, wg, wu, wd):
    d = xg.shape[-1]
    ff = wg.shape[2]
    np_ = GROUP_PIECES
    n_steps = step_exp.shape[0]
    xh = xg.reshape(-1, GROUP_PAD, d)
    wh = ws.reshape(-1, GROUP_PAD, LANES)
    pc3 = lambda k: (lambda i, pc, se, nu: (pc[k * n_steps + i], 0, 0))
    exp3 = lambda i, pc, se, nu: (se[i], 0, 0)
    return pl.pallas_call(
        _moe_expert_kernel,
        grid_spec=pltpu.PrefetchScalarGridSpec(
            num_scalar_prefetch=3,
            grid=(n_steps,),
            in_specs=(
                [pl.BlockSpec((1, GROUP_PAD, d), pc3(k)) for k in range(np_)]
                + [pl.BlockSpec((1, GROUP_PAD, LANES), pc3(k)) for k in range(np_)]
                + [pl.BlockSpec((1, d, ff), exp3), pl.BlockSpec((1, d, ff), exp3), pl.BlockSpec((1, ff, d), exp3)]
            ),
            out_specs=pl.BlockSpec((1, GROUP_TILE, d), lambda i, pc, se, nu: (jnp.minimum(i, nu[0] - 1), 0, 0)),
        ),
        out_shape=jax.ShapeDtypeStruct((n_steps, GROUP_TILE, d), BF),
        compiler_params=_cparams("arbitrary"),
        name="moe_experts",
    )(pieces, step_exp, n_used, *([xh] * np_), *([wh] * np_), wg, wu, wd)


def _moe_combine_kernel(nt_ref, loc_ref, x_ref, gt_ref, slot_ref, *refs):
    y_refs, o_ref, acc_scr = refs[:GROUP_PIECES], refs[GROUP_PIECES], refs[GROUP_PIECES + 1]
    b = pl.program_id(0)
    j = pl.program_id(1)
    tb = x_ref.shape[0]

    @pl.when(j == 0)
    def _():
        acc_scr[...] = jnp.zeros_like(acc_scr)

    @pl.when(j < nt_ref[b])
    def _():
        sl = slot_ref[0]
        pos = (lax.broadcasted_iota(jnp.int32, (tb, GROUP_TILE), 1) + j * GROUP_TILE).astype(F32)
        p = jnp.where((pos == sl[:, 0:1]) | (pos == sl[:, 1:2]), 1.0, 0.0).astype(BF)
        acc_scr[...] += _dot(p, jnp.concatenate([r[0] for r in y_refs], axis=0))

    @pl.when(j == pl.num_programs(1) - 1)
    def _():
        o_ref[...] = x_ref[...] + gt_ref[0] * acc_scr[...]


def _moe_combine(ntiles_b, loc, x2, gt, slots, yg, *, tb, nt, mod_row):
    t, d = x2.shape
    nb = t // tb
    np_ = GROUP_PIECES
    yh = yg.reshape(-1, GROUP_PAD, d)

    def piece3(k):
        return lambda b, j, n, lc: (lc[(b * nt + jnp.minimum(j, n[b] - 1)) * np_ + k], 0, 0)

    return pl.pallas_call(
        _moe_combine_kernel,
        grid_spec=pltpu.PrefetchScalarGridSpec(
            num_scalar_prefetch=2,
            grid=(nb, nt),
            in_specs=[
                pl.BlockSpec((tb, d), lambda b, j, n, lc: (b, 0)),
                pl.BlockSpec((1, 1, d), lambda b, j, n, lc: (mod_row(b), 0, 0)),
                pl.BlockSpec((1, tb, LANES), lambda b, j, n, lc: (b, 0, 0)),
            ] + [pl.BlockSpec((1, GROUP_PAD, d), piece3(k)) for k in range(np_)],
            out_specs=pl.BlockSpec((tb, d), lambda b, j, n, lc: (b, 0)),
            scratch_shapes=[pltpu.VMEM((tb, d), F32)],
        ),
        out_shape=jax.ShapeDtypeStruct((t, d), F32),
        compiler_params=_cparams("parallel", "arbitrary"),
        name="moe_combine",
    )(ntiles_b, loc, x2, gt, slots, *([yh] * np_))


def _moe(x2, sh, sc, gt, g, wr_pad, wg, wu, wd, *, tb, mod_row):
    nt = -(-(2 * tb + N_EXPERTS * (GROUP_PAD - 1)) // GROUP_TILE)
    xg, ws, slots, cnt = _moe_group(x2, sh, sc, g, wr_pad, tb=tb, nt=nt, mod_row=mod_row)
    counts = cnt[:, 0, :N_EXPERTS].astype(jnp.int32)
    pieces, step_exp, n_used, loc, ntiles_b = _moe_schedule(counts, nt * GROUP_PIECES)
    yg = _moe_experts(pieces, step_exp, n_used, xg, ws, wg, wu, wd)
    return _moe_combine(ntiles_b, loc, x2, gt, slots, yg, tb=tb, nt=nt, mod_row=mod_row)


def _rope_tables(seq_len):
    pos = np.arange(seq_len)
    prow = (pos // GRID_W).astype(np.float32)
    pcol = (pos % GRID_W).astype(np.float32)
    n_freq = HEAD_DIM // 4
    inv = (np.float32(ROPE_THETA) ** (-np.arange(n_freq, dtype=np.float32) / n_freq)).astype(np.float32)
    ar = (prow[:, None] * inv[None, :]).astype(np.float64)
    ac = (pcol[:, None] * inv[None, :]).astype(np.float64)
    cos = np.concatenate([np.cos(ar)] * 2 + [np.cos(ac)] * 2, axis=1)
    sin = np.concatenate([-np.sin(ar), np.sin(ar), -np.sin(ac), np.sin(ac)], axis=1)
    return (jnp.asarray(np.concatenate([cos, cos], axis=1), dtype=F32),
            jnp.asarray(np.concatenate([sin, sin], axis=1), dtype=F32))


def _head_sum_matrix():
    c = np.arange(ATTN_WIDTH)
    return jnp.asarray((c[:, None] // HEAD_DIM) == (c[None, :] // HEAD_DIM), dtype=F32).astype(BF)


def _fnet_channel_matrix():
    c = np.arange(FNET_WIDTH)
    same = (c[:, None] // FNET_GROUP_DIM) == (c[None, :] // FNET_GROUP_DIM)
    ang = 2.0 * np.pi * (((c[:, None] % FNET_GROUP_DIM) * (c[None, :] % FNET_GROUP_DIM)) % FNET_GROUP_DIM) / FNET_GROUP_DIM
    cb = np.where(same, np.cos(ang), 0.0)
    sb = np.where(same, np.sin(ang), 0.0)
    return jnp.asarray(np.concatenate([cb, -sb], axis=1), dtype=F32).astype(BF)


def _filter_features(n):
    t = np.linspace(0.0, 1.0, n)[:, None]
    w = 2.0 * np.pi * np.arange(n)[:, None] / n
    fb = np.linspace(1e-4, FILTER_BANDS - 1, FILTER_BANDS)
    z = np.concatenate([t, np.cos(fb * w), -np.sin(fb * w)], axis=-1)
    z = np.pad(z, ((0, 0), (0, 64 - z.shape[1])))
    return jnp.asarray(np.concatenate([z, np.zeros((1, z.shape[1])), z[:0:-1]], axis=0), dtype=F32)


def _decay_rates():
    d = jnp.abs(jnp.linspace(math.log(DECAY_TARGET) / SLOW_DECAY_PCT, math.log(DECAY_TARGET) / FAST_DECAY_PCT,
                             HYENA_WIDTH, dtype=F32))
    return jnp.concatenate([d] * HYENA_ORDER)[None, :]


def kernel(x, c, ctx, c_ctx, w_ada, b_ada, norm1_g, norm2_g, w_in, q_norm_g, k_norm_g, attn_sink,
           hy_conv_w, hy_conv_b, hy_filt_w1, hy_filt_b1, hy_filt_freq1, hy_filt_w2, hy_filt_b2,
           hy_filt_freq2, hy_filt_w3, hy_bias, w_proj_attn, w_proj_hyena, w_proj_fnet, w_out,
           ffn_w_gate, ffn_w_up, ffn_w_down, moe_router, moe_w_gate, moe_w_up, moe_w_down):
    b, seq, d = x.shape
    n_ctx = ctx.shape[1]
    depth = w_ada.shape[0]
    tm = 512
    tiles_per_seq = seq // tm

    cond8 = jnp.concatenate([c, c_ctx[None, :], jnp.zeros((8 - b - 1, d), F32)], axis=0)
    mods = _adaln(cond8, w_ada, b_ada)

    cos_l, sin_l = _rope_tables(seq)
    cos_c = jnp.ones((n_ctx, LANES), F32)
    sin_c = jnp.zeros((n_ctx, LANES), F32)
    gsum = _head_sum_matrix()
    mfn = _fnet_channel_matrix()
    deltas = _decay_rates()
    zfeat_l = _filter_features(seq)
    zfeat_c = _filter_features(n_ctx)

    lat_row = lambda i: i // tiles_per_seq
    ctx_row = lambda i: b
    lat_tab = lambda i: i % tiles_per_seq
    ctx_tab = lambda i: 0
    tm_c = min(tm, n_ctx)

    xs = x.reshape(b * seq, d)
    cs = ctx.reshape(b * n_ctx, d)
    for l in range(depth):
        last = l == depth - 1
        mod = lambda j: mods[l, :, j * d:(j + 1) * d].reshape(8, 1, d)
        w_in_bf = w_in[l].astype(BF)
        qg = jnp.tile(q_norm_g[l], N_HEADS)[None, :]
        kg = jnp.tile(k_norm_g[l], N_KV_HEADS)[None, :]
        g1 = norm1_g[l][None, :]
        wa, wh, wf, wo = (w_proj_attn[l].astype(BF), w_proj_hyena[l].astype(BF),
                          w_proj_fnet[l].astype(BF), w_out[l].astype(BF))
        conv_w = hy_conv_w[l].reshape(3, -1)
        conv_b = hy_conv_b[l][None, :]
        w1p = jnp.pad(hy_filt_w1[l], ((0, 64 - hy_filt_w1.shape[1]), (0, 0)))
        filt = (w1p, hy_filt_b1[l][None, :], hy_filt_freq1[l][None, :], hy_filt_w2[l],
                hy_filt_b2[l][None, :], hy_filt_freq2[l][None, :], hy_filt_w3[l], deltas)

        q_c, kd_c, vd_c, u_c, pq_c, gates_c = _phase_a(
            cs, mod(0), mod(1), g1, w_in_bf, cos_c, sin_c, qg, kg, gsum, mfn,
            tm=tm_c, mod_row=ctx_row, tab_row=ctx_tab)
        kd_c3 = kd_c.reshape(b, n_ctx, -1)
        vd_c3 = vd_c.reshape(b, n_ctx, -1)

        q_l, kd_l, vd_l, u_l, pq_l, gates_l = _phase_a(
            xs, mod(0), mod(1), g1, w_in_bf, cos_l, sin_l, qg, kg, gsum, mfn,
            tm=tm, mod_row=lat_row, tab_row=lat_tab)
        attn_l = _attention(attn_sink[l], q_l.reshape(b, seq, -1), kd_l.reshape(b, seq, -1),
                            vd_l.reshape(b, seq, -1), kd_c3, vd_c3, local=True, tq=512)
        h_l, nrm_l = _hy_filter(zfeat_l, *filt, tm=512)
        uc_l = _hy_prep(u_l.reshape(b, seq, -1), conv_w, conv_b)
        hy_l = _hyena(uc_l, h_l, nrm_l, hy_bias[l])
        fn_l = _fnet(pq_l.reshape(b, seq, -1))
        xs = _merge(xs, mod(2), attn_l.reshape(b * seq, -1), hy_l, fn_l, gates_l, wa, wh, wf, wo,
                    tm=tm, mod_row=lat_row)

        if not last:
            attn_c = _attention(attn_sink[l], q_c.reshape(b, n_ctx, -1), kd_c3, vd_c3, kd_c3, vd_c3,
                                local=False, tq=n_ctx)
            h_c, nrm_c = _hy_filter(zfeat_c, *filt, tm=n_ctx)
            uc_c = _hy_prep(u_c.reshape(b, n_ctx, -1), conv_w, conv_b)
            hy_c = _hyena(uc_c, h_c, nrm_c, hy_bias[l])
            fn_c = _fnet(pq_c.reshape(b, n_ctx, -1))
            cs = _merge(cs, mod(2), attn_c.reshape(b * n_ctx, -1), hy_c, fn_c, gates_c, wa, wh, wf, wo,
                        tm=tm_c, mod_row=ctx_row)

        g2 = norm2_g[l][None, :]
        i = l // 2
        if l % 2 == 0:
            wg, wu, wd = ffn_w_gate[i].astype(BF), ffn_w_up[i].astype(BF), ffn_w_down[i].astype(BF)
            run = lambda t2, rows, tmm: _ffn(t2, mod(3), mod(4), mod(5), g2, wg, wu, wd,
                                             tm=tmm, tf=D_FF, mod_row=rows)
        else:
            wr = jnp.pad(moe_router[i], ((0, 0), (0, LANES - N_EXPERTS)))
            wg, wu, wd = moe_w_gate[i].astype(BF), moe_w_up[i].astype(BF), moe_w_down[i].astype(BF)
            run = lambda t2, rows, tmm: _moe(t2, mod(3), mod(4), mod(5), g2, wr, wg, wu, wd,
                                             tb=tmm, mod_row=rows)
        tm_ffn = 512 if l % 2 == 0 else 1024
        xs = run(xs, lambda t: t // (seq // tm_ffn), tm_ffn)
        if not last:
            cs = run(cs, ctx_row, min(tm_ffn, b * n_ctx))
    return xs.reshape(b, seq, d)
```

```python
import functools
import math

import numpy as np
import jax
import jax.numpy as jnp
from jax import lax
from jax.experimental import pallas as pl
from jax.experimental.pallas import tpu as pltpu

F32 = jnp.float32
BF = jnp.bfloat16

D_MODEL = 1024
DEPTH = 4
GRID_W = 64
HEAD_DIM = 64
N_HEADS = 8
N_KV_HEADS = 2
ATTN_WIDTH = N_HEADS * HEAD_DIM
KV_WIDTH = N_KV_HEADS * HEAD_DIM
WINDOW = 128
QBLK = 128
ROPE_THETA = 10000.0
HYENA_ORDER = 2
HYENA_WIDTH = 256
FILTER_BANDS = 16
FILTER_HIDDEN = 64
DECAY_TARGET = 1e-2
FAST_DECAY_PCT = 0.3
SLOW_DECAY_PCT = 1.5
FNET_WIDTH = 256
FNET_GROUP_DIM = 64
Q_END = ATTN_WIDTH
K_END = Q_END + KV_WIDTH
V_END = K_END + KV_WIDTH
HY_END = V_END + (HYENA_ORDER + 1) * HYENA_WIDTH
FN_END = HY_END + FNET_WIDTH
IN_WIDTH = FN_END + 3 * D_MODEL
D_FF = 2816
N_EXPERTS = 8
EPS = 1e-6
LANES = 128
NEG = -1e30
STAGE_ROWS = 16

VMEM_LIMIT = 56 * 1024 * 1024


def _cparams(*sem):
    return pltpu.CompilerParams(dimension_semantics=sem, vmem_limit_bytes=VMEM_LIMIT)


def _dot(a, b):
    return jnp.dot(a, b, preferred_element_type=F32)


def _dot_nt(a, b):
    return lax.dot_general(a, b, (((1,), (1,)), ((), ())), preferred_element_type=F32)


def _split(a):
    hi = a.astype(BF)
    lo = (a - hi.astype(F32)).astype(BF)
    return hi, lo


def _dot3(a, b):
    ah, al = _split(a)
    bh, bl = _split(b)
    return _dot(ah, bh) + (_dot(ah, bl) + _dot(al, bh))


def _dot2(a, b_bf16):
    ah, al = _split(a)
    return _dot(ah, b_bf16) + _dot(al, b_bf16)


def _sigmoid(v):
    return 0.5 * jnp.tanh(0.5 * v) + 0.5


def _silu(v):
    return v * _sigmoid(v)


def _adaln_kernel(c_ref, w_ref, b_ref, o_ref):
    o_ref[0] = _dot3(_silu(c_ref[...]), w_ref[0]) + b_ref[0]


def _adaln(cond8, w_ada, b_ada):
    depth, d, n6 = w_ada.shape
    tn = 1024
    return pl.pallas_call(
        _adaln_kernel,
        grid=(depth, n6 // tn),
        in_specs=[
            pl.BlockSpec((8, d), lambda l, j: (0, 0)),
            pl.BlockSpec((1, d, tn), lambda l, j: (l, 0, j)),
            pl.BlockSpec((1, 1, tn), lambda l, j: (l, 0, j)),
        ],
        out_specs=pl.BlockSpec((1, 8, tn), lambda l, j: (l, 0, j)),
        out_shape=jax.ShapeDtypeStruct((depth, 8, n6), F32),
        compiler_params=_cparams("parallel", "parallel"),
        name="adaln",
    )(cond8, w_ada, b_ada.reshape(depth, 1, n6))


def _modulated_norm(x, g, sc, sh):
    ms = jnp.mean(x * x, axis=-1, keepdims=True)
    h = (x * lax.rsqrt(ms + EPS)) * g
    return h * (1.0 + sc) + sh


HALO = 16


def _phase_a_kernel(x_ref, xp_ref, xn_ref, sh_ref, sc_ref, g_ref, w_ref, cos_ref, sin_ref, qg_ref, kg_ref,
                    gsum_ref, mfn_ref, cw_ref, cb_ref, qkv_ref, upq_ref, gate_ref, *, tiles_per_seq):
    tm = x_ref.shape[0]
    norm = lambda xv: _modulated_norm(xv, g_ref[...], sc_ref[0], sh_ref[0]).astype(BF)
    hb = norm(x_ref[...])
    cos = cos_ref[...]
    sin = sin_ref[...]

    def headnorm(t, gain, gs):
        ss = _dot2(t * t, gs)
        return t * lax.rsqrt(ss * (1.0 / HEAD_DIM) + EPS) * gain

    def rope(t, cosw, sinw):
        w = t.shape[1]
        nxt = pltpu.roll(t, w - 16, axis=1)
        prv = pltpu.roll(t, 16, axis=1)
        lw = lax.broadcasted_iota(jnp.int32, t.shape, 1)
        return t * cosw + jnp.where((lw % 32) < 16, nxt, prv) * sinw

    def dup_halves(t):
        lane = lax.broadcasted_iota(jnp.int32, t.shape, 1)
        sw = pltpu.roll(t, 64, axis=1)
        lo = lane < 64
        return jnp.concatenate([jnp.where(lo, t, sw), jnp.where(lo, sw, t)], axis=1)

    pq = _dot(hb, w_ref[:, 0:Q_END])
    qn = headnorm(pq, qg_ref[...], gsum_ref[...])
    cos4 = jnp.concatenate([cos] * 4, axis=1)
    sin4 = jnp.concatenate([sin] * 4, axis=1)
    qkv_ref[:, 0:Q_END] = (rope(qn, cos4, sin4) * (HEAD_DIM ** -0.5)).astype(BF)

    pk = _dot(hb, w_ref[:, Q_END:K_END])
    kn = headnorm(pk, kg_ref[...], gsum_ref[0:KV_WIDTH, 0:KV_WIDTH])
    qkv_ref[:, Q_END:Q_END + 2 * KV_WIDTH] = dup_halves(rope(kn, cos, sin)).astype(BF)
    qkv_ref[:, Q_END + 2 * KV_WIDTH:] = dup_halves(_dot(hb, w_ref[:, K_END:V_END])).astype(BF)

    w_hy = w_ref[:, V_END:HY_END]
    u = _dot(hb, w_hy)
    tile = pl.program_id(0) % tiles_per_seq
    u_before = jnp.where(tile == 0, 0.0, _dot(norm(xp_ref[...]), w_hy)[HALO - 1:HALO])
    u_after = jnp.where(tile == tiles_per_seq - 1, 0.0, _dot(norm(xn_ref[...]), w_hy)[0:1])
    row = lax.broadcasted_iota(jnp.int32, (tm, 1), 0)
    prv = jnp.where(row == 0, u_before, pltpu.roll(u, 1, axis=0))
    nxt = jnp.where(row == tm - 1, u_after, pltpu.roll(u, tm - 1, axis=0))
    n_hy = HY_END - V_END
    upq_ref[:, 0:n_hy] = prv * cw_ref[0:1, :] + u * cw_ref[1:2, :] + nxt * cw_ref[2:3, :] + cb_ref[...]

    f = _dot(hb, w_ref[:, HY_END:FN_END])
    upq_ref[:, n_hy:] = _dot(f.astype(BF), mfn_ref[...])
    for i in range(3):
        lo = FN_END + i * D_MODEL
        gate_ref[:, i * D_MODEL:(i + 1) * D_MODEL] = _sigmoid(
            _dot(hb, w_ref[:, lo:lo + D_MODEL])).astype(BF)


def _phase_a(x2, sh, sc, g, w_in_bf, cos_t, sin_t, qg, kg, gsum, mfn, conv_w, conv_b, *, tm, tiles_per_seq,
             mod_row, tab_row):
    t, d = x2.shape
    row3 = lambda i: (mod_row(i), 0, 0)
    full = lambda i: (0, 0)
    tok = lambda i: (i, 0)
    n_halo = t // HALO
    outs = [
        ((t, ATTN_WIDTH + 4 * KV_WIDTH), BF), ((t, 3 * HYENA_WIDTH + 2 * FNET_WIDTH), F32),
        ((t, 3 * D_MODEL), BF),
    ]
    kern = functools.partial(_phase_a_kernel, tiles_per_seq=tiles_per_seq)
    return pl.pallas_call(
        kern,
        grid=(t // tm,),
        in_specs=[
            pl.BlockSpec((tm, d), tok),
            pl.BlockSpec((HALO, d), lambda i: (jnp.maximum(i * (tm // HALO) - 1, 0), 0)),
            pl.BlockSpec((HALO, d), lambda i: (jnp.minimum((i + 1) * (tm // HALO), n_halo - 1), 0)),
            pl.BlockSpec((1, 1, d), row3),
            pl.BlockSpec((1, 1, d), row3),
            pl.BlockSpec((1, d), full),
            pl.BlockSpec((d, IN_WIDTH), full, pipeline_mode=pl.Buffered(1)),
            pl.BlockSpec((tm, LANES), lambda i: (tab_row(i), 0)),
            pl.BlockSpec((tm, LANES), lambda i: (tab_row(i), 0)),
            pl.BlockSpec((1, ATTN_WIDTH), full),
            pl.BlockSpec((1, KV_WIDTH), full),
            pl.BlockSpec((ATTN_WIDTH, ATTN_WIDTH), full),
            pl.BlockSpec((FNET_WIDTH, 2 * FNET_WIDTH), full),
            pl.BlockSpec(conv_w.shape, full),
            pl.BlockSpec(conv_b.shape, full),
        ],
        out_specs=[pl.BlockSpec((tm, s[1]), tok) for s, _ in outs],
        out_shape=[jax.ShapeDtypeStruct(s, dt) for s, dt in outs],
        compiler_params=_cparams("parallel"),
        name="phase_a",
    )(x2, x2, x2, sh, sc, g, w_in_bf, cos_t, sin_t, qg, kg, gsum, mfn, conv_w, conv_b)


def _attn_kernel(sink_ref, q_ref, kd_ref, vd_ref, kc_ref, vc_ref, o_ref, *, local, seq_len):
    tq = q_ref.shape[1]
    nblk = tq // QBLK
    gq = N_HEADS // N_KV_HEADS
    rows = gq * QBLK
    lane = lax.broadcasted_iota(jnp.int32, (QBLK, LANES), 1)
    lo_half = lane < 64
    hrow = lax.broadcasted_iota(jnp.int32, (rows, 1), 0) // QBLK
    nband = 3 * QBLK
    if local:
        qk_off = (lax.broadcasted_iota(jnp.int32, (rows, nband), 0) % QBLK
                  - lax.broadcasted_iota(jnp.int32, (rows, nband), 1))
    for blk in range(nblk):
        r0 = blk * QBLK
        qb = q_ref[0, r0:r0 + QBLK, :]
        if local:
            n = pl.program_id(1) * nblk + blk
            start = pl.multiple_of(jnp.clip((n - 1) * QBLK, 0, seq_len - nband), QBLK)
            valid = jnp.abs(qk_off + (n * QBLK - start)) <= WINDOW
        for g in range(N_KV_HEADS):
            parts = []
            for hh in range(gq):
                h = gq * g + hh
                qc = qb[:, (h // 2) * LANES:(h // 2 + 1) * LANES]
                keep = lo_half if h % 2 == 0 else jnp.logical_not(lo_half)
                parts.append(jnp.where(keep, qc, jnp.zeros_like(qc)))
            q4 = jnp.concatenate(parts, axis=0)
            sk = jnp.full((rows, 1), sink_ref[gq * g + gq - 1], F32)
            for hh in range(gq - 2, -1, -1):
                sk = jnp.where(hrow == hh, sink_ref[gq * g + hh], sk)
            gl = slice(g * LANES, (g + 1) * LANES)
            s_ctx = _dot_nt(q4, kc_ref[0, :, gl])
            m = jnp.maximum(jnp.max(s_ctx, axis=1, keepdims=True), sk)
            if local:
                s_loc = _dot_nt(q4, kd_ref[0, pl.ds(start, nband), gl])
                s_loc = jnp.where(valid, s_loc, NEG)
                m = jnp.maximum(m, jnp.max(s_loc, axis=1, keepdims=True))
            p_ctx = jnp.exp(s_ctx - m)
            den = jnp.sum(p_ctx, axis=1, keepdims=True) + jnp.exp(sk - m)
            o = _dot(p_ctx.astype(BF), vc_ref[0, :, gl])
            if local:
                p_loc = jnp.exp(s_loc - m)
                den = den + jnp.sum(p_loc, axis=1, keepdims=True)
                o = o + _dot(p_loc.astype(BF), vd_ref[0, pl.ds(start, nband), gl])
            o = o / den
            for cc in range(gq // 2):
                col = (gq // 2) * g + cc
                oa = o[(2 * cc) * QBLK:(2 * cc + 1) * QBLK]
                ob = o[(2 * cc + 1) * QBLK:(2 * cc + 2) * QBLK]
                o_ref[0, r0:r0 + QBLK, col * LANES:(col + 1) * LANES] = (
                    jnp.where(lo_half, oa, ob).astype(BF))


def _attention(sink, qkv, qkv_ctx, *, local, tq):
    b, lq, _ = qkv.shape
    c = qkv_ctx.shape[1]
    kw = 2 * KV_WIDTH
    k_blk, v_blk = ATTN_WIDTH // kw, ATTN_WIDTH // kw + 1
    kern = functools.partial(_attn_kernel, local=local, seq_len=lq)
    return pl.pallas_call(
        kern,
        grid=(b, lq // tq),
        in_specs=[
            pl.BlockSpec(memory_space=pltpu.SMEM),
            pl.BlockSpec((1, tq, ATTN_WIDTH), lambda bi, i: (bi, i, 0)),
            pl.BlockSpec((1, lq, kw), lambda bi, i: (bi, 0, k_blk)),
            pl.BlockSpec((1, lq, kw), lambda bi, i: (bi, 0, v_blk)),
            pl.BlockSpec((1, c, kw), lambda bi, i: (bi, 0, k_blk)),
            pl.BlockSpec((1, c, kw), lambda bi, i: (bi, 0, v_blk)),
        ],
        out_specs=pl.BlockSpec((1, tq, ATTN_WIDTH), lambda bi, i: (bi, i, 0)),
        out_shape=jax.ShapeDtypeStruct((b, lq, ATTN_WIDTH), BF),
        compiler_params=_cparams("parallel", "parallel"),
        name="attn_local" if local else "attn_ctx",
    )(sink, qkv, qkv, qkv, qkv_ctx, qkv_ctx)


def _hy_filter_kernel(z_ref, w1_ref, b1_ref, f1_ref, w2_ref, b2_ref, f2_ref, w3_ref, dl_ref,
                      k_ref, nrm_ref, *, n_half):
    i = pl.program_id(0)
    tm = z_ref.shape[0]
    z = z_ref[...]
    h = jnp.sin(f1_ref[...] * (_dot3(z, w1_ref[...]) + b1_ref[...]))
    h = jnp.sin(f2_ref[...] * (_dot3(h, w2_ref[...]) + b2_ref[...]))
    h = _dot3(h, w3_ref[...])
    h = h * jnp.exp(-z[:, 0:1] * dl_ref[...])
    row = i * tm + lax.broadcasted_iota(jnp.int32, (tm, 1), 0)
    h = jnp.where(row == n_half, 0.0, h)
    k_ref[...] = h

    @pl.when(i == 0)
    def _():
        nrm_ref[...] = jnp.zeros_like(nrm_ref)

    nrm_ref[...] += jnp.sum(jnp.abs(h), axis=0, keepdims=True)


def _hy_filter(zfeat2, w1p, b1, f1, w2, b2, f2, w3, deltas2, *, tm):
    n2 = zfeat2.shape[0]
    n_half = n2 // 2
    wout = w3.shape[1] // 2
    full = lambda i: (0, 0)
    kern = functools.partial(_hy_filter_kernel, n_half=n_half)
    return pl.pallas_call(
        kern,
        grid=(n2 // tm,),
        in_specs=[
            pl.BlockSpec((tm, zfeat2.shape[1]), lambda i: (i, 0)),
            pl.BlockSpec(w1p.shape, full), pl.BlockSpec(b1.shape, full), pl.BlockSpec(f1.shape, full),
            pl.BlockSpec(w2.shape, full), pl.BlockSpec(b2.shape, full), pl.BlockSpec(f2.shape, full),
            pl.BlockSpec((w3.shape[0], wout), lambda i: (0, (i * tm) // n_half)),
            pl.BlockSpec(deltas2.shape, full),
        ],
        out_specs=[pl.BlockSpec((tm, wout), lambda i: (i, 0)), pl.BlockSpec((1, wout), full)],
        out_shape=[jax.ShapeDtypeStruct((n2, wout), F32), jax.ShapeDtypeStruct((1, wout), F32)],
        compiler_params=_cparams("arbitrary"),
        name="hy_filter",
    )(zfeat2, w1p, b1, f1, w2, b2, f2, w3, deltas2)


def _dft_mats(k_out, r_in, period, sign, scale, n_in, real_out):
    k = np.arange(k_out)[:, None]
    r = np.arange(r_in)[None, :]
    ang = 2.0 * np.pi * ((k * r) % period) / period
    fr = np.cos(ang) * scale
    fi = sign * np.sin(ang) * scale
    if real_out:
        mats = [fr, -fi]
    else:
        mats = [np.concatenate([fr, fi], 0), np.concatenate([-fi, fr], 0)]
    return jnp.asarray(np.stack(mats[:n_in], 0), dtype=F32).astype(BF)


def _twiddle(s_n, k_n, n, sign, sbk):
    s0 = lax.broadcasted_iota(jnp.int32, (s_n // sbk, k_n, LANES), 0) * sbk
    k = lax.broadcasted_iota(jnp.int32, (s_n // sbk, k_n, LANES), 1)
    ang = (s0 * k).astype(F32) * (2.0 * math.pi / n)
    ang1 = lax.broadcasted_iota(jnp.int32, (k_n, LANES), 0).astype(F32) * (2.0 * math.pi / n)
    return jnp.cos(ang), sign * jnp.sin(ang), jnp.cos(ang1), sign * jnp.sin(ang1)


def _unpack_pair(p):
    return [pltpu.unpack_elementwise(p, index=i, packed_dtype=BF, unpacked_dtype=F32) for i in (0, 1)]


def _stage_kernel(*refs, n_in, r_in, k_mid, k_out, sbk, tw, spec, second, gate, real_out,
                  transposed_out, flat, packed_in, packed_spec, packed_out):
    it = iter(refs)
    x_refs = [next(it) for _ in range(n_in)]
    g_ref = next(it)
    g2_ref = next(it) if second else None
    tw_refs = [next(it) for _ in range(4)] if tw else None
    n_spec = 1 if packed_spec else 2
    spec_refs = [next(it) for _ in range(n_spec + 1)] if spec else None
    gate_refs = [next(it) for _ in range(5)] if gate else None
    out_refs = [next(it)] if (real_out or gate or packed_out) else [next(it), next(it)]
    if not flat:
        x_refs = [r.reshape(r_in * sbk, LANES) for r in x_refs]
        if spec:
            spec_refs = [r.reshape(k_mid * sbk, LANES) for r in spec_refs[:n_spec]] + spec_refs[n_spec:]
        if gate:
            gate_refs = [r.reshape(k_out * sbk, LANES) for r in gate_refs[:4]] + gate_refs[4:]
            out_refs = [out_refs[0].reshape(2 * k_out * sbk, LANES)]
        elif not transposed_out:
            out_refs = [r.reshape(k_out * sbk, LANES) for r in out_refs]
    if spec:
        inv = 1.0 / spec_refs[n_spec][...]
    if tw:
        tr, ti = tw_refs[0][0], tw_refs[1][0]
        wr, wi = tw_refs[2][...], tw_refs[3][...]
    for j in range(sbk):
        parts = [x_ref[...] if flat else x_ref[pl.ds(j, r_in, stride=sbk), :] for x_ref in x_refs]
        if packed_in:
            parts = _unpack_pair(parts[0])
        acc = None
        for xi, xv in enumerate(parts):
            d = _dot(g_ref[xi], xv.astype(BF))
            acc = d if acc is None else acc + d
        if real_out:
            ys = [acc]
        else:
            yr, yi = acc[:k_mid], acc[k_mid:]
            if spec:
                rows = slice(None) if flat else pl.ds(j, k_mid, stride=sbk)
                if packed_spec:
                    sr, si = _unpack_pair(spec_refs[0][rows, :])
                else:
                    sr, si = spec_refs[0][rows, :], spec_refs[1][rows, :]
                sr, si = sr * inv, si * inv
                yr, yi = yr * sr - yi * si, yr * si + yi * sr
            if second:
                acc = _dot(g2_ref[0], yr.astype(BF)) + _dot(g2_ref[1], yi.astype(BF))
                yr, yi = acc[:k_out], acc[k_out:]
            if tw:
                yr, yi = yr * tr - yi * ti, yr * ti + yi * tr
                if j + 1 < sbk:
                    tr, ti = tr * wr - ti * wi, tr * wi + ti * wr
            ys = [yr, yi]
        if gate:
            o_ref = out_refs[0]
            for part, y in enumerate(ys):
                rows = slice(None) if flat else pl.ds(j, k_out, stride=sbk)
                val = gate_refs[part][rows, :] * (y + gate_refs[4][...] * gate_refs[2 + part][rows, :])
                if flat:
                    o_ref[part] = val.astype(o_ref.dtype)
                else:
                    o_ref[pl.ds(part * k_out * sbk + j, k_out, stride=sbk), :] = val.astype(o_ref.dtype)
            continue
        if packed_out:
            ys = [pltpu.pack_elementwise(ys, packed_dtype=BF)]
        for o_ref, y in zip(out_refs, ys):
            if flat:
                o_ref[...] = y.astype(o_ref.dtype)
            elif transposed_out:
                o_ref[0, j] = y.astype(o_ref.dtype)
            else:
                o_ref[pl.ds(j, k_out, stride=sbk), :] = y.astype(o_ref.dtype)


def _fft_stage(xs, x_sel, gmat, *, r_in, s_n, k_out, n_groups, n_cblk, transposed_out, real_out,
               out_dtype=F32, g2mat=None, tw=None, spec=None, spec_sel=None, gate=None, sbk=STAGE_ROWS,
               packed_in=False, packed_out=False, name="fft_stage"):
    n_in = len(xs)
    flat = s_n == 1
    sbk = 1 if flat else min(sbk, s_n)
    cb = LANES
    in_specs, args = [], []
    for x, sel in zip(xs, x_sel):
        if flat:
            in_specs.append(pl.BlockSpec((None, r_in, cb), lambda s, g, c, sel=sel: (sel(g, c)[0], 0, sel(g, c)[1])))
            args.append(x)
        else:
            xv = x.reshape(x.shape[0], x.shape[1] // s_n, s_n, x.shape[2])
            in_specs.append(pl.BlockSpec((1, r_in, sbk, cb),
                                         lambda s, g, c, sel=sel: (sel(g, c)[0], 0, s, sel(g, c)[1])))
            args.append(xv)
    in_specs.append(pl.BlockSpec(gmat.shape, lambda s, g, c: (0, 0, 0)))
    args.append(gmat)
    k_mid = gmat.shape[1] // (1 if real_out else 2)
    if g2mat is not None:
        in_specs.append(pl.BlockSpec(g2mat.shape, lambda s, g, c: (0, 0, 0)))
        args.append(g2mat)
    if tw is not None:
        for tarr in tw[:2]:
            in_specs.append(pl.BlockSpec((1, k_out, LANES), lambda s, g, c: (s, 0, 0)))
            args.append(tarr)
        for tarr in tw[2:]:
            in_specs.append(pl.BlockSpec((k_out, LANES), lambda s, g, c: (0, 0)))
            args.append(tarr)
    if spec is not None:
        *planes, nrm = spec
        for arr in planes:
            if flat:
                in_specs.append(pl.BlockSpec((k_mid, cb), lambda s, g, c: (0, spec_sel(g, c))))
                args.append(arr)
            else:
                in_specs.append(pl.BlockSpec((1, k_mid, sbk, cb), lambda s, g, c: (0, 0, s, spec_sel(g, c))))
                args.append(arr.reshape(1, k_mid, s_n, arr.shape[-1]))
        in_specs.append(pl.BlockSpec((1, cb), lambda s, g, c: (0, spec_sel(g, c))))
        args.append(nrm)
    if gate is not None:
        (ga, gblk), (za, zblk), bias = gate
        for arr, blk in ((ga, gblk), (za, zblk)):
            for bi in (0, 1):
                if flat:
                    in_specs.append(pl.BlockSpec((None, k_out, cb), lambda s, g, c, bi=bi, blk=blk: (bi, 0, blk + c)))
                    args.append(arr)
                else:
                    in_specs.append(pl.BlockSpec((1, k_out, sbk, cb),
                                                 lambda s, g, c, bi=bi, blk=blk: (bi, 0, s, blk + c)))
                    args.append(arr.reshape(arr.shape[0], k_out, s_n, arr.shape[-1]))
        in_specs.append(pl.BlockSpec((1, cb), lambda s, g, c: (0, c)))
        args.append(bias)
    ctot = n_cblk * cb
    if gate is not None:
        n_groups = 2
        if flat:
            oshape = (2, k_out, ctot)
            ospec = pl.BlockSpec((2, k_out, cb), lambda s, g, c: (0, 0, c))
        else:
            oshape = (2, k_out, s_n, ctot)
            ospec = pl.BlockSpec((2, k_out, sbk, cb), lambda s, g, c: (0, 0, s, c))
    elif flat:
        oshape = (n_groups, k_out, ctot)
        ospec = pl.BlockSpec((None, k_out, cb), lambda s, g, c: (g, 0, c))
    elif transposed_out:
        oshape = (n_groups, s_n, k_out, ctot)
        ospec = pl.BlockSpec((1, sbk, k_out, cb), lambda s, g, c: (g, s, 0, c))
    else:
        oshape = (n_groups, k_out, s_n, ctot)
        ospec = pl.BlockSpec((1, k_out, sbk, cb), lambda s, g, c: (g, 0, s, c))
    n_out = 1 if (real_out or gate is not None or packed_out) else 2
    if packed_out:
        out_dtype = jnp.int32
    kern = functools.partial(_stage_kernel, n_in=n_in, r_in=r_in, k_mid=k_mid, k_out=k_out, sbk=sbk,
                             tw=tw is not None, spec=spec is not None, second=g2mat is not None,
                             gate=gate is not None, real_out=real_out, transposed_out=transposed_out,
                             flat=flat, packed_in=packed_in, packed_spec=spec is not None and len(spec) == 2,
                             packed_out=packed_out)
    n_grid_groups = 1 if gate is not None else n_groups
    outs = pl.pallas_call(
        kern,
        grid=(s_n // sbk, n_grid_groups, n_cblk),
        in_specs=in_specs,
        out_specs=[ospec] * n_out,
        out_shape=[jax.ShapeDtypeStruct(oshape, out_dtype)] * n_out,
        compiler_params=_cparams("parallel", "parallel", "parallel"),
        name=name,
    )(*args)
    return [o.reshape(n_groups, -1, ctot) for o in outs]


def _split_len(n):
    if n <= 1024:
        return n, 1
    s = 128
    return n // s, s


def _fft_forward(xs, x_sel, n, n_rows, *, n_groups, n_cblk, name="fwd"):
    n1, s = _split_len(n)
    n_in = len(xs)
    if s == 1:
        g = _dft_mats(n, n_rows, n, -1.0, 1.0, n_in, False)
        return _fft_stage(xs, x_sel, g, r_in=n_rows, s_n=1, k_out=n, n_groups=n_groups, n_cblk=n_cblk,
                          transposed_out=False, real_out=False, name=name + "_direct")
    r1 = n_rows // s
    g1 = _dft_mats(n1, r1, n1, -1.0, 1.0, n_in, False)
    tw = _twiddle(s, n1, n, -1.0, STAGE_ROWS)
    (a,) = _fft_stage(xs, x_sel, g1, r_in=r1, s_n=s, k_out=n1, n_groups=n_groups, n_cblk=n_cblk,
                      transposed_out=True, real_out=False, tw=tw, packed_out=True, name=name + "_s1")
    g2 = _dft_mats(s, s, s, -1.0, 1.0, 2, False)
    return _fft_stage([a], [lambda g, c: (g, c)], g2, r_in=s, s_n=n1, k_out=s, n_groups=n_groups,
                      n_cblk=n_cblk, transposed_out=False, real_out=False, packed_in=True, packed_out=True,
                      name=name + "_s2")


def _hyena(uc, k_filt, nrm, hy_bias):
    b, n, _ = uc.shape
    w = HYENA_WIDTH
    wblk = w // LANES
    nfft = 2 * n
    n1, s = _split_len(nfft)
    ident = lambda g, c: (g, c)
    k_spec = _fft_forward([k_filt[None]], [lambda g, c: (0, c)], nfft, nfft, n_groups=1,
                          n_cblk=HYENA_ORDER * wblk, name="hy_filt_fft")
    z, zblk = uc, 2 * wblk
    for o in range(HYENA_ORDER):
        sel_r = lambda g, c, zblk=zblk: (0, zblk + c)
        sel_i = lambda g, c, zblk=zblk: (1, zblk + c)
        spec = tuple(p[0] for p in k_spec) + (nrm,)
        spec_sel = lambda g, c, o=o: o * wblk + c
        gate = ((uc, o * wblk), (z, zblk), hy_bias[o:o + 1])
        common = dict(n_groups=1, n_cblk=wblk, real_out=False)
        if s == 1:
            gf = _dft_mats(nfft, n, nfft, -1.0, 1.0, 2, False)
            gi = _dft_mats(n, nfft, nfft, 1.0, 1.0 / nfft, 2, False)
            (z,) = _fft_stage([z, z], [sel_r, sel_i], gf, r_in=n, s_n=1, k_out=n, transposed_out=False,
                              g2mat=gi, spec=spec, spec_sel=spec_sel, gate=gate, name="hy_direct", **common)
        else:
            r1 = n // s
            g1 = _dft_mats(n1, r1, n1, -1.0, 1.0, 2, False)
            (a,) = _fft_stage([z, z], [sel_r, sel_i], g1, r_in=r1, s_n=s, k_out=n1, transposed_out=True,
                              tw=_twiddle(s, n1, nfft, -1.0, STAGE_ROWS), packed_out=True, name="hy_s1",
                              **common)
            g2 = _dft_mats(s, s, s, -1.0, 1.0, 2, False)
            g3 = _dft_mats(s, s, s, 1.0, 1.0, 2, False)
            (q,) = _fft_stage([a], [ident], g2, r_in=s, s_n=n1, k_out=s, transposed_out=True,
                              g2mat=g3, tw=_twiddle(n1, s, nfft, 1.0, STAGE_ROWS), spec=spec,
                              spec_sel=spec_sel, packed_in=True, packed_out=True, name="hy_mid", **common)
            g4 = _dft_mats(n // s, n1, n1, 1.0, 1.0 / nfft, 2, False)
            (z,) = _fft_stage([q], [ident], g4, r_in=n1, s_n=s, k_out=n // s, transposed_out=False,
                              gate=gate, packed_in=True, name="hy_last", **common)
        zblk = 0
    return z.reshape(b * n, w)


def _fnet(pq):
    b, n, _ = pq.shape
    w = FNET_WIDTH
    wblk = w // LANES
    first = 3 * HYENA_WIDTH // LANES
    scale = 1.0 / math.sqrt(n * FNET_GROUP_DIM)
    sel_r = lambda g, c: (g, first + c)
    sel_i = lambda g, c: (g, first + wblk + c)
    ident = lambda g, c: (g, c)
    n1, s = _split_len(n)
    if s == 1:
        g = _dft_mats(n, n, n, -1.0, scale, 2, True)
        (y,) = _fft_stage([pq, pq], [sel_r, sel_i], g, r_in=n, s_n=1, k_out=n, n_groups=b, n_cblk=wblk,
                          transposed_out=False, real_out=True, name="fnet_direct")
        return y.reshape(b * n, w)
    g1 = _dft_mats(n1, n1, n1, -1.0, 1.0, 2, False)
    tw = _twiddle(s, n1, n, -1.0, STAGE_ROWS)
    (a,) = _fft_stage([pq, pq], [sel_r, sel_i], g1, r_in=n1, s_n=s, k_out=n1, n_groups=b, n_cblk=wblk,
                      transposed_out=True, real_out=False, tw=tw, packed_out=True, name="fnet_s1")
    g2 = _dft_mats(s, s, s, -1.0, scale, 2, True)
    (y,) = _fft_stage([a], [ident], g2, r_in=s, s_n=n1, k_out=s, n_groups=b, n_cblk=wblk,
                      transposed_out=False, real_out=True, packed_in=True, name="fnet_s2")
    return y.reshape(b * n, w)


def _merge_kernel(x_ref, gt_ref, a_ref, h_ref, f_ref, gate_ref, wa_ref, wh_ref, wf_ref, wo_ref, o_ref):
    d = D_MODEL
    m = gate_ref[:, 0:d].astype(F32) * _dot(a_ref[...], wa_ref[...])
    m = m + gate_ref[:, d:2 * d].astype(F32) * _dot(h_ref[...].astype(BF), wh_ref[...])
    m = m + gate_ref[:, 2 * d:3 * d].astype(F32) * _dot(f_ref[...].astype(BF), wf_ref[...])
    y = _dot(m.astype(BF), wo_ref[...])
    o_ref[...] = x_ref[...] + gt_ref[0] * y


def _merge(x2, gt, attn_o, hy_o, fn_o, gates, wa, wh, wf, wo, *, tm, mod_row):
    t, d = x2.shape
    tok = lambda i: (i, 0)
    full = lambda i: (0, 0)
    return pl.pallas_call(
        _merge_kernel,
        grid=(t // tm,),
        in_specs=[
            pl.BlockSpec((tm, d), tok),
            pl.BlockSpec((1, 1, d), lambda i: (mod_row(i), 0, 0)),
            pl.BlockSpec((tm, ATTN_WIDTH), tok),
            pl.BlockSpec((tm, HYENA_WIDTH), tok),
            pl.BlockSpec((tm, FNET_WIDTH), tok),
            pl.BlockSpec((tm, 3 * d), tok),
            pl.BlockSpec(wa.shape, full, pipeline_mode=pl.Buffered(1)),
            pl.BlockSpec(wh.shape, full, pipeline_mode=pl.Buffered(1)),
            pl.BlockSpec(wf.shape, full, pipeline_mode=pl.Buffered(1)),
            pl.BlockSpec(wo.shape, full, pipeline_mode=pl.Buffered(1)),
        ],
        out_specs=pl.BlockSpec((tm, d), tok),
        out_shape=jax.ShapeDtypeStruct((t, d), F32),
        compiler_params=_cparams("parallel"),
        name="merge",
    )(x2, gt, attn_o, hy_o, fn_o, gates, wa, wh, wf, wo)


def _ffn_kernel(x_ref, sh_ref, sc_ref, gt_ref, g_ref, wg_ref, wu_ref, wd_ref, o_ref, h_scr, acc_scr):
    f = pl.program_id(1)

    @pl.when(f == 0)
    def _():
        h_scr[...] = _modulated_norm(x_ref[...], g_ref[...], sc_ref[0], sh_ref[0]).astype(BF)
        acc_scr[...] = jnp.zeros_like(acc_scr)

    hb = h_scr[...]
    act = _silu(_dot(hb, wg_ref[...])) * _dot(hb, wu_ref[...])
    acc_scr[...] += _dot(act.astype(BF), wd_ref[...])

    @pl.when(f == pl.num_programs(1) - 1)
    def _():
        o_ref[...] = x_ref[...] + gt_ref[0] * acc_scr[...]


def _ffn(x2, sh, sc, gt, g, wg, wu, wd, *, tm, tf, mod_row):
    t, d = x2.shape
    ff = wg.shape[1]
    row3 = lambda i, f: (mod_row(i), 0, 0)
    wmode = dict(pipeline_mode=pl.Buffered(1)) if tf == ff else {}
    return pl.pallas_call(
        _ffn_kernel,
        grid=(t // tm, ff // tf),
        in_specs=[
            pl.BlockSpec((tm, d), lambda i, f: (i, 0)),
            pl.BlockSpec((1, 1, d), row3), pl.BlockSpec((1, 1, d), row3), pl.BlockSpec((1, 1, d), row3),
            pl.BlockSpec((1, d), lambda i, f: (0, 0)),
            pl.BlockSpec((d, tf), lambda i, f: (0, f), **wmode),
            pl.BlockSpec((d, tf), lambda i, f: (0, f), **wmode),
            pl.BlockSpec((tf, d), lambda i, f: (f, 0), **wmode),
        ],
        out_specs=pl.BlockSpec((tm, d), lambda i, f: (i, 0)),
        out_shape=jax.ShapeDtypeStruct((t, d), F32),
        scratch_shapes=[pltpu.VMEM((tm, d), BF), pltpu.VMEM((tm, d), F32)],
        compiler_params=_cparams("parallel", "arbitrary"),
        name="ffn_dense",
    )(x2, sh, sc, gt, g, wg, wu, wd)


def _top2(logits):
    lane = lax.broadcasted_iota(jnp.int32, logits.shape, 1)
    lg = jnp.where(lane < N_EXPERTS, logits, -jnp.inf)
    m1 = jnp.max(lg, axis=1, keepdims=True)
    i1 = jnp.min(jnp.where(lg == m1, lane, LANES), axis=1, keepdims=True)
    lg2 = jnp.where(lane == i1, -jnp.inf, lg)
    m2 = jnp.max(lg2, axis=1, keepdims=True)
    i2 = jnp.min(jnp.where(lg2 == m2, lane, LANES), axis=1, keepdims=True)
    e2 = jnp.exp(m2 - m1)
    w1 = 1.0 / (1.0 + e2)
    return i1, i2, w1, e2 * w1


GROUP_TILE = 256
GROUP_PAD = 64
GROUP_PIECES = GROUP_TILE // GROUP_PAD
SLOT_RADIX = 64.0


def _moe_group_kernel(x_ref, sh_ref, sc_ref, g_ref, wr_ref, xg_ref, ws_ref, slot_ref, cnt_ref,
                      h_scr, rows_scr, wm_scr):
    j = pl.program_id(1)
    tb = x_ref.shape[0]
    gt_rows = xg_ref.shape[1]

    @pl.when(j == 0)
    def _():
        h = _modulated_norm(x_ref[...], g_ref[...], sc_ref[0], sh_ref[0])
        h_scr[...] = h.astype(BF)
        i1, i2, w1, w2 = _top2(_dot3(h, wr_ref[...]))
        lane = lax.broadcasted_iota(jnp.int32, (tb, LANES), 1)
        oh0 = jnp.where(lane == i1, 1.0, 0.0)
        oh1 = jnp.where(lane == i2, 1.0, 0.0)
        c0 = jnp.sum(oh0, axis=0, keepdims=True)
        cnt = c0 + jnp.sum(oh1, axis=0, keepdims=True)
        tri = jnp.where(lax.broadcasted_iota(jnp.int32, (tb, tb), 1)
                        < lax.broadcasted_iota(jnp.int32, (tb, tb), 0), 1.0, 0.0).astype(BF)
        pre0 = _dot(tri, oh0.astype(BF))
        pre1 = _dot(tri, oh1.astype(BF)) + c0
        tiles = jnp.ceil(cnt * (1.0 / GROUP_PAD))
        upper = jnp.where(lax.broadcasted_iota(jnp.int32, (LANES, LANES), 0)
                          < lax.broadcasted_iota(jnp.int32, (LANES, LANES), 1), 1.0, 0.0).astype(BF)
        off = _dot(jnp.broadcast_to(tiles, (8, LANES)).astype(BF), upper)[0:1] * float(GROUP_PAD)
        slot0 = jnp.sum(oh0 * (off + pre0), axis=1, keepdims=True)
        slot1 = jnp.sum(oh1 * (off + pre1), axis=1, keepdims=True)
        slot_ref[0] = jnp.where(lane == 0, slot0, jnp.where(lane == 1, slot1, 0.0))
        cnt_ref[0] = jnp.broadcast_to(cnt, (8, LANES))
        hi0 = jnp.floor(slot0 * (1.0 / SLOT_RADIX))
        hi1 = jnp.floor(slot1 * (1.0 / SLOT_RADIX))
        digits = jnp.where(lane == 0, hi0, jnp.where(lane == 1, slot0 - SLOT_RADIX * hi0,
                           jnp.where(lane == 2, hi1, jnp.where(lane == 3, slot1 - SLOT_RADIX * hi1, 0.0))))
        sel = jnp.where(lax.broadcasted_iota(jnp.int32, (8, LANES), 0)
                        == lax.broadcasted_iota(jnp.int32, (8, LANES), 1), 1.0, 0.0).astype(BF)
        rows_scr[...] = _dot_nt(sel, digits.astype(BF))
        w1h, w1l = _split(w1)
        w1m, w1l = _split(w1 - w1h.astype(F32))
        w2h, w2l = _split(w2)
        w2m, w2l = _split(w2 - w2h.astype(F32))
        cols = [w1h, w1m, w1l, w2h, w2m, w2l]
        wm = jnp.zeros((tb, LANES), F32)
        for li, col in enumerate(cols):
            wm = jnp.where(lane == li, col.astype(F32), wm)
        wm_scr[...] = wm.astype(BF)

    rows = rows_scr[...]
    s0 = rows[0:1] * SLOT_RADIX + rows[1:2]
    s1 = rows[2:3] * SLOT_RADIX + rows[3:4]
    pos = (lax.broadcasted_iota(jnp.int32, (gt_rows, tb), 0) + j * gt_rows).astype(F32)
    g0 = jnp.where(pos == s0, 1.0, 0.0).astype(BF)
    g1 = jnp.where(pos == s1, 1.0, 0.0).astype(BF)
    xg_ref[0] = _dot(g0 + g1, h_scr[...]).astype(BF)
    lane_w = lax.broadcasted_iota(jnp.int32, (gt_rows, LANES), 1)
    wsum = (jnp.where(lane_w < 3, _dot(g0, wm_scr[...]), 0.0)
            + jnp.where((lane_w >= 3) & (lane_w < 6), _dot(g1, wm_scr[...]), 0.0))
    ws_ref[0] = jnp.broadcast_to(jnp.sum(wsum, axis=1, keepdims=True), (gt_rows, LANES))


def _moe_group(x2, sh, sc, g, wr_pad, *, tb, nt, mod_row):
    t, d = x2.shape
    nb = t // tb
    row3 = lambda b, j: (mod_row(b), 0, 0)
    return pl.pallas_call(
        _moe_group_kernel,
        grid=(nb, nt),
        in_specs=[
            pl.BlockSpec((tb, d), lambda b, j: (b, 0)),
            pl.BlockSpec((1, 1, d), row3), pl.BlockSpec((1, 1, d), row3),
            pl.BlockSpec((1, d), lambda b, j: (0, 0)),
            pl.BlockSpec((d, LANES), lambda b, j: (0, 0)),
        ],
        out_specs=[
            pl.BlockSpec((1, GROUP_TILE, d), lambda b, j: (b * nt + j, 0, 0)),
            pl.BlockSpec((1, GROUP_TILE, LANES), lambda b, j: (b * nt + j, 0, 0)),
            pl.BlockSpec((1, tb, LANES), lambda b, j: (b, 0, 0)),
            pl.BlockSpec((1, 8, LANES), lambda b, j: (b, 0, 0)),
        ],
        out_shape=[
            jax.ShapeDtypeStruct((nb * nt, GROUP_TILE, d), BF),
            jax.ShapeDtypeStruct((nb * nt, GROUP_TILE, LANES), F32),
            jax.ShapeDtypeStruct((nb, tb, LANES), F32),
            jax.ShapeDtypeStruct((nb, 8, LANES), F32),
        ],
        scratch_shapes=[pltpu.VMEM((tb, d), BF), pltpu.VMEM((8, tb), F32), pltpu.VMEM((tb, LANES), BF)],
        compiler_params=_cparams("parallel", "arbitrary"),
        name="moe_group",
    )(x2, sh, sc, g, wr_pad)


def _moe_schedule(cnt, nh):
    np_ = GROUP_PIECES
    h = (cnt + GROUP_PAD - 1) // GROUP_PAD
    nb, ne = h.shape
    tot = h.sum(0)
    pairs = (tot + np_ - 1) // np_
    cum_p = jnp.cumsum(pairs)
    start_p = cum_p - pairs
    n_used = cum_p[-1]
    n_steps = (nb * nh + ne * (np_ - 1)) // np_
    q = jnp.minimum(jnp.arange(n_steps, dtype=jnp.int32), n_used - 1)
    e = jnp.sum(q[:, None] >= cum_p[None, :], axis=1).astype(jnp.int32)
    r = q - start_p[e]
    cum_b = jnp.cumsum(h, axis=0)
    first = jnp.cumsum(h, axis=1) - h

    def piece(idx):
        idx = jnp.minimum(idx, tot[e] - 1)
        blk = jnp.sum(idx[:, None] >= cum_b.T[e], axis=1).astype(jnp.int32)
        return blk * nh + first[blk, e] + idx - (cum_b[blk, e] - h[blk, e])

    x = jnp.arange(nh, dtype=jnp.int32)[None, :]
    ex = jnp.sum(x[:, :, None] >= jnp.cumsum(h, axis=1)[:, None, :], axis=2).astype(jnp.int32)
    exc = jnp.minimum(ex, ne - 1)
    g = jnp.take_along_axis(cum_b - h, exc, axis=1) + x - jnp.take_along_axis(first, exc, axis=1)
    loc = np_ * (start_p[exc] + g // np_) + g % np_
    loc = jnp.where(ex < ne, loc, loc[:, 0:1])
    pieces = jnp.concatenate([piece(np_ * r + k) for k in range(np_)]).astype(jnp.int32)
    return (pieces, e, n_used.astype(jnp.int32).reshape(1), loc.reshape(-1).astype(jnp.int32),
            ((h.sum(1) + np_ - 1) // np_).astype(jnp.int32))


def _moe_expert_kernel(pc_ref, exp_ref, nused_ref, *refs):
    np_ = GROUP_PIECES
    x_refs, w_refs = refs[:np_], refs[np_:2 * np_]
    wg_ref, wu_ref, wd_ref, y_ref = refs[2 * np_:]

    @pl.when(pl.program_id(0) < nused_ref[0])
    def _():
        x = jnp.concatenate([r[0] for r in x_refs], axis=0)
        act = _silu(_dot(x, wg_ref[0])) * _dot(x, wu_ref[0])
        y = _dot(act.astype(BF), wd_ref[0])
        w = jnp.concatenate([r[0] for r in w_refs], axis=0)
        y_ref[0] = (y * jnp.concatenate([w] * (y.shape[1] // LANES), axis=1)).astype(BF)


def _moe_experts(pieces, step_exp, n_used, xg, ws, wg, wu, wd):
    d = xg.shape[-1]
    ff = wg.shape[2]
    np_ = GROUP_PIECES
    n_steps = step_exp.shape[0]
    xh = xg.reshape(-1, GROUP_PAD, d)
    wh = ws.reshape(-1, GROUP_PAD, LANES)
    pc3 = lambda k: (lambda i, pc, se, nu: (pc[k * n_steps + i], 0, 0))
    exp3 = lambda i, pc, se, nu: (se[i], 0, 0)
    return pl.pallas_call(
        _moe_expert_kernel,
        grid_spec=pltpu.PrefetchScalarGridSpec(
            num_scalar_prefetch=3,
            grid=(n_steps,),
            in_specs=(
                [pl.BlockSpec((1, GROUP_PAD, d), pc3(k)) for k in range(np_)]
                + [pl.BlockSpec((1, GROUP_PAD, LANES), pc3(k)) for k in range(np_)]
                + [pl.BlockSpec((1, d, ff), exp3), pl.BlockSpec((1, d, ff), exp3), pl.BlockSpec((1, ff, d), exp3)]
            ),
            out_specs=pl.BlockSpec((1, GROUP_TILE, d), lambda i, pc, se, nu: (jnp.minimum(i, nu[0] - 1), 0, 0)),
        ),
        out_shape=jax.ShapeDtypeStruct((n_steps, GROUP_TILE, d), BF),
        compiler_params=_cparams("arbitrary"),
        name="moe_experts",
    )(pieces, step_exp, n_used, *([xh] * np_), *([wh] * np_), wg, wu, wd)


def _moe_combine_kernel(nt_ref, loc_ref, x_ref, gt_ref, slot_ref, *refs):
    y_refs, o_ref, acc_scr = refs[:GROUP_PIECES], refs[GROUP_PIECES], refs[GROUP_PIECES + 1]
    b = pl.program_id(0)
    j = pl.program_id(1)
    tb = x_ref.shape[0]

    @pl.when(j == 0)
    def _():
        acc_scr[...] = jnp.zeros_like(acc_scr)

    @pl.when(j < nt_ref[b])
    def _():
        sl = slot_ref[0]
        pos = (lax.broadcasted_iota(jnp.int32, (tb, GROUP_TILE), 1) + j * GROUP_TILE).astype(F32)
        p = jnp.where((pos == sl[:, 0:1]) | (pos == sl[:, 1:2]), 1.0, 0.0).astype(BF)
        acc_scr[...] += _dot(p, jnp.concatenate([r[0] for r in y_refs], axis=0))

    @pl.when(j == pl.num_programs(1) - 1)
    def _():
        o_ref[...] = x_ref[...] + gt_ref[0] * acc_scr[...]


def _moe_combine(ntiles_b, loc, x2, gt, slots, yg, *, tb, nt, mod_row):
    t, d = x2.shape
    nb = t // tb
    np_ = GROUP_PIECES
    yh = yg.reshape(-1, GROUP_PAD, d)

    def piece3(k):
        return lambda b, j, n, lc: (lc[(b * nt + jnp.minimum(j, n[b] - 1)) * np_ + k], 0, 0)

    return pl.pallas_call(
        _moe_combine_kernel,
        grid_spec=pltpu.PrefetchScalarGridSpec(
            num_scalar_prefetch=2,
            grid=(nb, nt),
            in_specs=[
                pl.BlockSpec((tb, d), lambda b, j, n, lc: (b, 0)),
                pl.BlockSpec((1, 1, d), lambda b, j, n, lc: (mod_row(b), 0, 0)),
                pl.BlockSpec((1, tb, LANES), lambda b, j, n, lc: (b, 0, 0)),
            ] + [pl.BlockSpec((1, GROUP_PAD, d), piece3(k)) for k in range(np_)],
            out_specs=pl.BlockSpec((tb, d), lambda b, j, n, lc: (b, 0)),
            scratch_shapes=[pltpu.VMEM((tb, d), F32)],
        ),
        out_shape=jax.ShapeDtypeStruct((t, d), F32),
        compiler_params=_cparams("parallel", "arbitrary"),
        name="moe_combine",
    )(ntiles_b, loc, x2, gt, slots, *([yh] * np_))


def _moe(x2, sh, sc, gt, g, wr_pad, wg, wu, wd, *, tb, mod_row):
    nt = -(-(2 * tb + N_EXPERTS * (GROUP_PAD - 1)) // GROUP_TILE)
    xg, ws, slots, cnt = _moe_group(x2, sh, sc, g, wr_pad, tb=tb, nt=nt, mod_row=mod_row)
    counts = cnt[:, 0, :N_EXPERTS].astype(jnp.int32)
    pieces, step_exp, n_used, loc, ntiles_b = _moe_schedule(counts, nt * GROUP_PIECES)
    yg = _moe_experts(pieces, step_exp, n_used, xg, ws, wg, wu, wd)
    return _moe_combine(ntiles_b, loc, x2, gt, slots, yg, tb=tb, nt=nt, mod_row=mod_row)


def _rope_tables(seq_len):
    pos = np.arange(seq_len)
    prow = (pos // GRID_W).astype(np.float32)
    pcol = (pos % GRID_W).astype(np.float32)
    n_freq = HEAD_DIM // 4
    inv = (np.float32(ROPE_THETA) ** (-np.arange(n_freq, dtype=np.float32) / n_freq)).astype(np.float32)
    ar = (prow[:, None] * inv[None, :]).astype(np.float64)
    ac = (pcol[:, None] * inv[None, :]).astype(np.float64)
    cos = np.concatenate([np.cos(ar)] * 2 + [np.cos(ac)] * 2, axis=1)
    sin = np.concatenate([-np.sin(ar), np.sin(ar), -np.sin(ac), np.sin(ac)], axis=1)
    return (jnp.asarray(np.concatenate([cos, cos], axis=1), dtype=F32),
            jnp.asarray(np.concatenate([sin, sin], axis=1), dtype=F32))


def _head_sum_matrix():
    c = np.arange(ATTN_WIDTH)
    return jnp.asarray((c[:, None] // HEAD_DIM) == (c[None, :] // HEAD_DIM), dtype=F32).astype(BF)


def _fnet_channel_matrix():
    c = np.arange(FNET_WIDTH)
    same = (c[:, None] // FNET_GROUP_DIM) == (c[None, :] // FNET_GROUP_DIM)
    ang = 2.0 * np.pi * (((c[:, None] % FNET_GROUP_DIM) * (c[None, :] % FNET_GROUP_DIM)) % FNET_GROUP_DIM) / FNET_GROUP_DIM
    cb = np.where(same, np.cos(ang), 0.0)
    sb = np.where(same, np.sin(ang), 0.0)
    return jnp.asarray(np.concatenate([cb, -sb], axis=1), dtype=F32).astype(BF)


def _filter_features(n):
    t = np.linspace(0.0, 1.0, n)[:, None]
    w = 2.0 * np.pi * np.arange(n)[:, None] / n
    fb = np.linspace(1e-4, FILTER_BANDS - 1, FILTER_BANDS)
    z = np.concatenate([t, np.cos(fb * w), -np.sin(fb * w)], axis=-1)
    z = np.pad(z, ((0, 0), (0, 64 - z.shape[1])))
    return jnp.asarray(np.concatenate([z, np.zeros((1, z.shape[1])), z[:0:-1]], axis=0), dtype=F32)


def _decay_rates():
    d = jnp.abs(jnp.linspace(math.log(DECAY_TARGET) / SLOW_DECAY_PCT, math.log(DECAY_TARGET) / FAST_DECAY_PCT,
                             HYENA_WIDTH, dtype=F32))
    return jnp.concatenate([d] * HYENA_ORDER)[None, :]


def kernel(x, c, ctx, c_ctx, w_ada, b_ada, norm1_g, norm2_g, w_in, q_norm_g, k_norm_g, attn_sink,
           hy_conv_w, hy_conv_b, hy_filt_w1, hy_filt_b1, hy_filt_freq1, hy_filt_w2, hy_filt_b2,
           hy_filt_freq2, hy_filt_w3, hy_bias, w_proj_attn, w_proj_hyena, w_proj_fnet, w_out,
           ffn_w_gate, ffn_w_up, ffn_w_down, moe_router, moe_w_gate, moe_w_up, moe_w_down):
    b, seq, d = x.shape
    n_ctx = ctx.shape[1]
    depth = w_ada.shape[0]
    tm = 512
    tiles_per_seq = seq // tm

    cond8 = jnp.concatenate([c, c_ctx[None, :], jnp.zeros((8 - b - 1, d), F32)], axis=0)
    mods = _adaln(cond8, w_ada, b_ada)

    cos_l, sin_l = _rope_tables(seq)
    cos_c = jnp.ones((n_ctx, LANES), F32)
    sin_c = jnp.zeros((n_ctx, LANES), F32)
    gsum = _head_sum_matrix()
    mfn = _fnet_channel_matrix()
    deltas = _decay_rates()
    zfeat_l = _filter_features(seq)
    zfeat_c = _filter_features(n_ctx)

    lat_row = lambda i: i // tiles_per_seq
    ctx_row = lambda i: b
    lat_tab = lambda i: i % tiles_per_seq
    ctx_tab = lambda i: 0
    tm_c = min(tm, n_ctx)

    xs = x.reshape(b * seq, d)
    cs = ctx.reshape(b * n_ctx, d)
    for l in range(depth):
        last = l == depth - 1
        mod = lambda j: mods[l, :, j * d:(j + 1) * d].reshape(8, 1, d)
        w_in_bf = w_in[l].astype(BF)
        qg = jnp.tile(q_norm_g[l], N_HEADS)[None, :]
        kg = jnp.tile(k_norm_g[l], N_KV_HEADS)[None, :]
        g1 = norm1_g[l][None, :]
        wa, wh, wf, wo = (w_proj_attn[l].astype(BF), w_proj_hyena[l].astype(BF),
                          w_proj_fnet[l].astype(BF), w_out[l].astype(BF))
        conv_w = hy_conv_w[l].reshape(3, -1)
        conv_b = hy_conv_b[l][None, :]
        w1p = jnp.pad(hy_filt_w1[l], ((0, 64 - hy_filt_w1.shape[1]), (0, 0)))
        filt = (w1p, hy_filt_b1[l][None, :], hy_filt_freq1[l][None, :], hy_filt_w2[l],
                hy_filt_b2[l][None, :], hy_filt_freq2[l][None, :], hy_filt_w3[l], deltas)

        qkv_c, upq_c, gates_c = _phase_a(
            cs, mod(0), mod(1), g1, w_in_bf, cos_c, sin_c, qg, kg, gsum, mfn, conv_w, conv_b,
            tm=tm_c, tiles_per_seq=n_ctx // tm_c, mod_row=ctx_row, tab_row=ctx_tab)
        qkv_c = qkv_c.reshape(b, n_ctx, -1)
        upq_c = upq_c.reshape(b, n_ctx, -1)

        qkv_l, upq_l, gates_l = _phase_a(
            xs, mod(0), mod(1), g1, w_in_bf, cos_l, sin_l, qg, kg, gsum, mfn, conv_w, conv_b,
            tm=tm, tiles_per_seq=tiles_per_seq, mod_row=lat_row, tab_row=lat_tab)
        upq_l = upq_l.reshape(b, seq, -1)
        attn_l = _attention(attn_sink[l], qkv_l.reshape(b, seq, -1), qkv_c, local=True, tq=512)
        h_l, nrm_l = _hy_filter(zfeat_l, *filt, tm=512)
        hy_l = _hyena(upq_l, h_l, nrm_l, hy_bias[l])
        fn_l = _fnet(upq_l)
        xs = _merge(xs, mod(2), attn_l.reshape(b * seq, -1), hy_l, fn_l, gates_l, wa, wh, wf, wo,
                    tm=tm, mod_row=lat_row)

        if not last:
            attn_c = _attention(attn_sink[l], qkv_c, qkv_c, local=False, tq=n_ctx)
            h_c, nrm_c = _hy_filter(zfeat_c, *filt, tm=n_ctx)
            hy_c = _hyena(upq_c, h_c, nrm_c, hy_bias[l])
            fn_c = _fnet(upq_c)
            cs = _merge(cs, mod(2), attn_c.reshape(b * n_ctx, -1), hy_c, fn_c, gates_c, wa, wh, wf, wo,
                        tm=tm_c, mod_row=ctx_row)

        g2 = norm2_g[l][None, :]
        i = l // 2
        if l % 2 == 0:
            wg, wu, wd = ffn_w_gate[i].astype(BF), ffn_w_up[i].astype(BF), ffn_w_down[i].astype(BF)
            run = lambda t2, rows, tmm: _ffn(t2, mod(3), mod(4), mod(5), g2, wg, wu, wd,
                                             tm=tmm, tf=D_FF, mod_row=rows)
        else:
            wr = jnp.pad(moe_router[i], ((0, 0), (0, LANES - N_EXPERTS)))
            wg, wu, wd = moe_w_gate[i].astype(BF), moe_w_up[i].astype(BF), moe_w_down[i].astype(BF)
            run = lambda t2, rows, tmm: _moe(t2, mod(3), mod(4), mod(5), g2, wr, wg, wu, wd,
                                             tb=tmm, mod_row=rows)
        tm_ffn = 512 if l % 2 == 0 else 1024
        xs = run(xs, lambda t: t // (seq // tm_ffn), tm_ffn)
        if not last:
            cs = run(cs, ctx_row, min(tm_ffn, b * n_ctx))
    return xs.reshape(b, seq, d)
```

```python
import functools
import math

import numpy as np
import jax
import jax.numpy as jnp
from jax import lax
from jax.experimental import pallas as pl
from jax.experimental.pallas import tpu as pltpu

F32 = jnp.float32
BF = jnp.bfloat16

D_MODEL = 1024
DEPTH = 4
GRID_W = 64
HEAD_DIM = 64
N_HEADS = 8
N_KV_HEADS = 2
ATTN_WIDTH = N_HEADS * HEAD_DIM
KV_WIDTH = N_KV_HEADS * HEAD_DIM
WINDOW = 128
QBLK = 128
ROPE_THETA = 10000.0
HYENA_ORDER = 2
HYENA_WIDTH = 256
FILTER_BANDS = 16
FILTER_HIDDEN = 64
DECAY_TARGET = 1e-2
FAST_DECAY_PCT = 0.3
SLOW_DECAY_PCT = 1.5
FNET_WIDTH = 256
FNET_GROUP_DIM = 64
Q_END = ATTN_WIDTH
K_END = Q_END + KV_WIDTH
V_END = K_END + KV_WIDTH
HY_END = V_END + (HYENA_ORDER + 1) * HYENA_WIDTH
FN_END = HY_END + FNET_WIDTH
IN_WIDTH = FN_END + 3 * D_MODEL
D_FF = 2816
N_EXPERTS = 8
EPS = 1e-6
LANES = 128
NEG = -1e30
STAGE_ROWS = 16

VMEM_LIMIT = 56 * 1024 * 1024


def _cparams(*sem):
    return pltpu.CompilerParams(dimension_semantics=sem, vmem_limit_bytes=VMEM_LIMIT)


def _dot(a, b):
    return jnp.dot(a, b, preferred_element_type=F32)


def _dot_nt(a, b):
    return lax.dot_general(a, b, (((1,), (1,)), ((), ())), preferred_element_type=F32)


def _split(a):
    hi = a.astype(BF)
    lo = (a - hi.astype(F32)).astype(BF)
    return hi, lo


def _dot3(a, b):
    ah, al = _split(a)
    bh, bl = _split(b)
    return _dot(ah, bh) + (_dot(ah, bl) + _dot(al, bh))


def _dot2(a, b_bf16):
    ah, al = _split(a)
    return _dot(ah, b_bf16) + _dot(al, b_bf16)


def _sigmoid(v):
    return 0.5 * jnp.tanh(0.5 * v) + 0.5


def _silu(v):
    return v * _sigmoid(v)


def _adaln_kernel(c_ref, w_ref, b_ref, o_ref):
    o_ref[0] = _dot3(_silu(c_ref[...]), w_ref[0]) + b_ref[0]


def _adaln(cond8, w_ada, b_ada):
    depth, d, n6 = w_ada.shape
    tn = 1024
    return pl.pallas_call(
        _adaln_kernel,
        grid=(depth, n6 // tn),
        in_specs=[
            pl.BlockSpec((8, d), lambda l, j: (0, 0)),
            pl.BlockSpec((1, d, tn), lambda l, j: (l, 0, j)),
            pl.BlockSpec((1, 1, tn), lambda l, j: (l, 0, j)),
        ],
        out_specs=pl.BlockSpec((1, 8, tn), lambda l, j: (l, 0, j)),
        out_shape=jax.ShapeDtypeStruct((depth, 8, n6), F32),
        compiler_params=_cparams("parallel", "parallel"),
        name="adaln",
    )(cond8, w_ada, b_ada.reshape(depth, 1, n6))


def _modulated_norm(x, g, sc, sh):
    ms = jnp.mean(x * x, axis=-1, keepdims=True)
    h = (x * lax.rsqrt(ms + EPS)) * g
    return h * (1.0 + sc) + sh


HALO = 16


def _phase_a_kernel(x_ref, xp_ref, xn_ref, sh_ref, sc_ref, g_ref, w_ref, cos_ref, sin_ref, qg_ref, kg_ref,
                    gsum_ref, mfn_ref, cw_ref, cb_ref, qkv_ref, upq_ref, gate_ref, *, tiles_per_seq):
    tm = x_ref.shape[0]
    norm = lambda xv: _modulated_norm(xv, g_ref[...], sc_ref[0], sh_ref[0]).astype(BF)
    hb = norm(x_ref[...])
    cos = cos_ref[...]
    sin = sin_ref[...]

    def headnorm(t, gain, gs):
        ss = _dot2(t * t, gs)
        return t * lax.rsqrt(ss * (1.0 / HEAD_DIM) + EPS) * gain

    def rope(t, cosw, sinw):
        w = t.shape[1]
        nxt = pltpu.roll(t, w - 16, axis=1)
        prv = pltpu.roll(t, 16, axis=1)
        lw = lax.broadcasted_iota(jnp.int32, t.shape, 1)
        return t * cosw + jnp.where((lw % 32) < 16, nxt, prv) * sinw

    def dup_halves(t):
        lane = lax.broadcasted_iota(jnp.int32, t.shape, 1)
        sw = pltpu.roll(t, 64, axis=1)
        lo = lane < 64
        return jnp.concatenate([jnp.where(lo, t, sw), jnp.where(lo, sw, t)], axis=1)

    pr = _dot(hb, w_ref[:, 0:FN_END])
    pq = pr[:, 0:Q_END]
    qn = headnorm(pq, qg_ref[...], gsum_ref[...])
    cos4 = jnp.concatenate([cos] * 4, axis=1)
    sin4 = jnp.concatenate([sin] * 4, axis=1)
    qkv_ref[:, 0:Q_END] = (rope(qn, cos4, sin4) * (HEAD_DIM ** -0.5)).astype(BF)

    kn = headnorm(pr[:, Q_END:K_END], kg_ref[...], gsum_ref[0:KV_WIDTH, 0:KV_WIDTH])
    qkv_ref[:, Q_END:Q_END + 2 * KV_WIDTH] = dup_halves(rope(kn, cos, sin)).astype(BF)
    qkv_ref[:, Q_END + 2 * KV_WIDTH:] = dup_halves(pr[:, K_END:V_END]).astype(BF)

    w_hy = w_ref[:, V_END:HY_END]
    u = pr[:, V_END:HY_END]
    tile = pl.program_id(0) % tiles_per_seq
    u_before = jnp.where(tile == 0, 0.0, _dot(norm(xp_ref[...]), w_hy)[HALO - 1:HALO])
    u_after = jnp.where(tile == tiles_per_seq - 1, 0.0, _dot(norm(xn_ref[...]), w_hy)[0:1])
    row = lax.broadcasted_iota(jnp.int32, (tm, 1), 0)
    prv = jnp.where(row == 0, u_before, pltpu.roll(u, 1, axis=0))
    nxt = jnp.where(row == tm - 1, u_after, pltpu.roll(u, tm - 1, axis=0))
    n_hy = HY_END - V_END
    upq_ref[:, 0:n_hy] = prv * cw_ref[0:1, :] + u * cw_ref[1:2, :] + nxt * cw_ref[2:3, :] + cb_ref[...]

    upq_ref[:, n_hy:] = _dot(pr[:, HY_END:FN_END].astype(BF), mfn_ref[...])
    gate_ref[...] = _sigmoid(_dot(hb, w_ref[:, FN_END:])).astype(BF)


def _phase_a(x2, sh, sc, g, w_in_bf, cos_t, sin_t, qg, kg, gsum, mfn, conv_w, conv_b, *, tm, tiles_per_seq,
             mod_row, tab_row):
    t, d = x2.shape
    row3 = lambda i: (mod_row(i), 0, 0)
    full = lambda i: (0, 0)
    tok = lambda i: (i, 0)
    n_halo = t // HALO
    outs = [
        ((t, ATTN_WIDTH + 4 * KV_WIDTH), BF), ((t, 3 * HYENA_WIDTH + 2 * FNET_WIDTH), F32),
        ((t, 3 * D_MODEL), BF),
    ]
    kern = functools.partial(_phase_a_kernel, tiles_per_seq=tiles_per_seq)
    return pl.pallas_call(
        kern,
        grid=(t // tm,),
        in_specs=[
            pl.BlockSpec((tm, d), tok),
            pl.BlockSpec((HALO, d), lambda i: (jnp.maximum(i * (tm // HALO) - 1, 0), 0)),
            pl.BlockSpec((HALO, d), lambda i: (jnp.minimum((i + 1) * (tm // HALO), n_halo - 1), 0)),
            pl.BlockSpec((1, 1, d), row3),
            pl.BlockSpec((1, 1, d), row3),
            pl.BlockSpec((1, d), full),
            pl.BlockSpec((d, IN_WIDTH), full, pipeline_mode=pl.Buffered(1)),
            pl.BlockSpec((tm, LANES), lambda i: (tab_row(i), 0)),
            pl.BlockSpec((tm, LANES), lambda i: (tab_row(i), 0)),
            pl.BlockSpec((1, ATTN_WIDTH), full),
            pl.BlockSpec((1, KV_WIDTH), full),
            pl.BlockSpec((ATTN_WIDTH, ATTN_WIDTH), full),
            pl.BlockSpec((FNET_WIDTH, 2 * FNET_WIDTH), full),
            pl.BlockSpec(conv_w.shape, full),
            pl.BlockSpec(conv_b.shape, full),
        ],
        out_specs=[pl.BlockSpec((tm, s[1]), tok) for s, _ in outs],
        out_shape=[jax.ShapeDtypeStruct(s, dt) for s, dt in outs],
        compiler_params=_cparams("parallel"),
        name="phase_a",
    )(x2, x2, x2, sh, sc, g, w_in_bf, cos_t, sin_t, qg, kg, gsum, mfn, conv_w, conv_b)


def _attn_kernel(sink_ref, q_ref, kd_ref, vd_ref, kc_ref, vc_ref, o_ref, *, local, seq_len):
    tq = q_ref.shape[1]
    nblk = tq // QBLK
    gq = N_HEADS // N_KV_HEADS
    rows = gq * QBLK
    lane = lax.broadcasted_iota(jnp.int32, (QBLK, LANES), 1)
    lo_half = lane < 64
    hrow = lax.broadcasted_iota(jnp.int32, (rows, 1), 0) // QBLK
    nband = 3 * QBLK
    if local:
        qk_off = (lax.broadcasted_iota(jnp.int32, (rows, nband), 0) % QBLK
                  - lax.broadcasted_iota(jnp.int32, (rows, nband), 1))
    for blk in range(nblk):
        r0 = blk * QBLK
        qb = q_ref[0, r0:r0 + QBLK, :]
        if local:
            n = pl.program_id(1) * nblk + blk
            start = pl.multiple_of(jnp.clip((n - 1) * QBLK, 0, seq_len - nband), QBLK)
            valid = jnp.abs(qk_off + (n * QBLK - start)) <= WINDOW
        for g in range(N_KV_HEADS):
            parts = []
            for hh in range(gq):
                h = gq * g + hh
                qc = qb[:, (h // 2) * LANES:(h // 2 + 1) * LANES]
                keep = lo_half if h % 2 == 0 else jnp.logical_not(lo_half)
                parts.append(jnp.where(keep, qc, jnp.zeros_like(qc)))
            q4 = jnp.concatenate(parts, axis=0)
            sk = jnp.full((rows, 1), sink_ref[gq * g + gq - 1], F32)
            for hh in range(gq - 2, -1, -1):
                sk = jnp.where(hrow == hh, sink_ref[gq * g + hh], sk)
            gl = slice(g * LANES, (g + 1) * LANES)
            s_ctx = _dot_nt(q4, kc_ref[0, :, gl])
            m = jnp.maximum(jnp.max(s_ctx, axis=1, keepdims=True), sk)
            if local:
                s_loc = _dot_nt(q4, kd_ref[0, pl.ds(start, nband), gl])
                s_loc = jnp.where(valid, s_loc, NEG)
                m = jnp.maximum(m, jnp.max(s_loc, axis=1, keepdims=True))
            p_ctx = jnp.exp(s_ctx - m)
            den = jnp.sum(p_ctx, axis=1, keepdims=True) + jnp.exp(sk - m)
            o = _dot(p_ctx.astype(BF), vc_ref[0, :, gl])
            if local:
                p_loc = jnp.exp(s_loc - m)
                den = den + jnp.sum(p_loc, axis=1, keepdims=True)
                o = o + _dot(p_loc.astype(BF), vd_ref[0, pl.ds(start, nband), gl])
            o = o / den
            for cc in range(gq // 2):
                col = (gq // 2) * g + cc
                oa = o[(2 * cc) * QBLK:(2 * cc + 1) * QBLK]
                ob = o[(2 * cc + 1) * QBLK:(2 * cc + 2) * QBLK]
                o_ref[0, r0:r0 + QBLK, col * LANES:(col + 1) * LANES] = (
                    jnp.where(lo_half, oa, ob).astype(BF))


def _attention(sink, qkv, qkv_ctx, *, local, tq):
    b, lq, _ = qkv.shape
    c = qkv_ctx.shape[1]
    kw = 2 * KV_WIDTH
    k_blk, v_blk = ATTN_WIDTH // kw, ATTN_WIDTH // kw + 1
    kern = functools.partial(_attn_kernel, local=local, seq_len=lq)
    return pl.pallas_call(
        kern,
        grid=(b, lq // tq),
        in_specs=[
            pl.BlockSpec(memory_space=pltpu.SMEM),
            pl.BlockSpec((1, tq, ATTN_WIDTH), lambda bi, i: (bi, i, 0)),
            pl.BlockSpec((1, lq, kw), lambda bi, i: (bi, 0, k_blk)),
            pl.BlockSpec((1, lq, kw), lambda bi, i: (bi, 0, v_blk)),
            pl.BlockSpec((1, c, kw), lambda bi, i: (bi, 0, k_blk)),
            pl.BlockSpec((1, c, kw), lambda bi, i: (bi, 0, v_blk)),
        ],
        out_specs=pl.BlockSpec((1, tq, ATTN_WIDTH), lambda bi, i: (bi, i, 0)),
        out_shape=jax.ShapeDtypeStruct((b, lq, ATTN_WIDTH), BF),
        compiler_params=_cparams("parallel", "parallel"),
        name="attn_local" if local else "attn_ctx",
    )(sink, qkv, qkv, qkv, qkv_ctx, qkv_ctx)


def _hy_filter_kernel(z_ref, w1_ref, b1_ref, f1_ref, w2_ref, b2_ref, f2_ref, w3_ref, dl_ref,
                      k_ref, nrm_ref, *, n_half):
    i = pl.program_id(0)
    tm = z_ref.shape[0]
    z = z_ref[...]
    h = jnp.sin(f1_ref[...] * (_dot3(z, w1_ref[...]) + b1_ref[...]))
    h = jnp.sin(f2_ref[...] * (_dot3(h, w2_ref[...]) + b2_ref[...]))
    h = _dot3(h, w3_ref[...])
    h = h * jnp.exp(-z[:, 0:1] * dl_ref[...])
    row = i * tm + lax.broadcasted_iota(jnp.int32, (tm, 1), 0)
    h = jnp.where(row == n_half, 0.0, h)
    k_ref[...] = h

    @pl.when(i == 0)
    def _():
        nrm_ref[...] = jnp.zeros_like(nrm_ref)

    nrm_ref[...] += jnp.sum(jnp.abs(h), axis=0, keepdims=True)


def _hy_filter(zfeat2, w1p, b1, f1, w2, b2, f2, w3, deltas2, *, tm):
    n2 = zfeat2.shape[0]
    n_half = n2 // 2
    wout = w3.shape[1] // 2
    full = lambda i: (0, 0)
    kern = functools.partial(_hy_filter_kernel, n_half=n_half)
    return pl.pallas_call(
        kern,
        grid=(n2 // tm,),
        in_specs=[
            pl.BlockSpec((tm, zfeat2.shape[1]), lambda i: (i, 0)),
            pl.BlockSpec(w1p.shape, full), pl.BlockSpec(b1.shape, full), pl.BlockSpec(f1.shape, full),
            pl.BlockSpec(w2.shape, full), pl.BlockSpec(b2.shape, full), pl.BlockSpec(f2.shape, full),
            pl.BlockSpec((w3.shape[0], wout), lambda i: (0, (i * tm) // n_half)),
            pl.BlockSpec(deltas2.shape, full),
        ],
        out_specs=[pl.BlockSpec((tm, wout), lambda i: (i, 0)), pl.BlockSpec((1, wout), full)],
        out_shape=[jax.ShapeDtypeStruct((n2, wout), F32), jax.ShapeDtypeStruct((1, wout), F32)],
        compiler_params=_cparams("arbitrary"),
        name="hy_filter",
    )(zfeat2, w1p, b1, f1, w2, b2, f2, w3, deltas2)


def _dft_mats(k_out, r_in, period, sign, scale, n_in, real_out):
    k = np.arange(k_out)[:, None]
    r = np.arange(r_in)[None, :]
    ang = 2.0 * np.pi * ((k * r) % period) / period
    fr = np.cos(ang) * scale
    fi = sign * np.sin(ang) * scale
    if real_out:
        mats = [fr, -fi]
    else:
        mats = [np.concatenate([fr, fi], 0), np.concatenate([-fi, fr], 0)]
    return jnp.asarray(np.stack(mats[:n_in], 0), dtype=F32).astype(BF)


def _twiddle(s_n, k_n, n, sign, sbk):
    s0 = lax.broadcasted_iota(jnp.int32, (s_n // sbk, k_n, LANES), 0) * sbk
    k = lax.broadcasted_iota(jnp.int32, (s_n // sbk, k_n, LANES), 1)
    ang = (s0 * k).astype(F32) * (2.0 * math.pi / n)
    ang1 = lax.broadcasted_iota(jnp.int32, (k_n, LANES), 0).astype(F32) * (2.0 * math.pi / n)
    return jnp.cos(ang), sign * jnp.sin(ang), jnp.cos(ang1), sign * jnp.sin(ang1)


def _unpack_pair(p):
    return [pltpu.unpack_elementwise(p, index=i, packed_dtype=BF, unpacked_dtype=F32) for i in (0, 1)]


def _stage_kernel(*refs, n_in, r_in, k_mid, k_out, sbk, tw, spec, second, gate, real_out,
                  transposed_out, flat, packed_in, packed_spec, packed_out):
    it = iter(refs)
    x_refs = [next(it) for _ in range(n_in)]
    g_ref = next(it)
    g2_ref = next(it) if second else None
    tw_refs = [next(it) for _ in range(4)] if tw else None
    n_spec = 1 if packed_spec else 2
    spec_refs = [next(it) for _ in range(n_spec + 1)] if spec else None
    gate_refs = [next(it) for _ in range(5)] if gate else None
    out_refs = [next(it)] if (real_out or gate or packed_out) else [next(it), next(it)]
    if not flat:
        x_refs = [r.reshape(r_in * sbk, LANES) for r in x_refs]
        if spec:
            spec_refs = [r.reshape(k_mid * sbk, LANES) for r in spec_refs[:n_spec]] + spec_refs[n_spec:]
        if gate:
            gate_refs = [r.reshape(k_out * sbk, LANES) for r in gate_refs[:4]] + gate_refs[4:]
            out_refs = [out_refs[0].reshape(2 * k_out * sbk, LANES)]
        elif not transposed_out:
            out_refs = [r.reshape(k_out * sbk, LANES) for r in out_refs]
    if spec:
        inv = 1.0 / spec_refs[n_spec][...]
    if tw:
        tr, ti = tw_refs[0][0], tw_refs[1][0]
        wr, wi = tw_refs[2][...], tw_refs[3][...]
    for j in range(sbk):
        parts = [x_ref[...] if flat else x_ref[pl.ds(j, r_in, stride=sbk), :] for x_ref in x_refs]
        if packed_in:
            parts = _unpack_pair(parts[0])
        acc = None
        for xi, xv in enumerate(parts):
            d = _dot(g_ref[xi], xv.astype(BF))
            acc = d if acc is None else acc + d
        if real_out:
            ys = [acc]
        else:
            yr, yi = acc[:k_mid], acc[k_mid:]
            if spec:
                rows = slice(None) if flat else pl.ds(j, k_mid, stride=sbk)
                if packed_spec:
                    sr, si = _unpack_pair(spec_refs[0][rows, :])
                else:
                    sr, si = spec_refs[0][rows, :], spec_refs[1][rows, :]
                sr, si = sr * inv, si * inv
                yr, yi = yr * sr - yi * si, yr * si + yi * sr
            if second:
                acc = _dot(g2_ref[0], yr.astype(BF)) + _dot(g2_ref[1], yi.astype(BF))
                yr, yi = acc[:k_out], acc[k_out:]
            if tw:
                yr, yi = yr * tr - yi * ti, yr * ti + yi * tr
                if j + 1 < sbk:
                    tr, ti = tr * wr - ti * wi, tr * wi + ti * wr
            ys = [yr, yi]
        if gate:
            o_ref = out_refs[0]
            for part, y in enumerate(ys):
                rows = slice(None) if flat else pl.ds(j, k_out, stride=sbk)
                val = gate_refs[part][rows, :] * (y + gate_refs[4][...] * gate_refs[2 + part][rows, :])
                if flat:
                    o_ref[part] = val.astype(o_ref.dtype)
                else:
                    o_ref[pl.ds(part * k_out * sbk + j, k_out, stride=sbk), :] = val.astype(o_ref.dtype)
            continue
        if packed_out:
            ys = [pltpu.pack_elementwise(ys, packed_dtype=BF)]
        for o_ref, y in zip(out_refs, ys):
            if flat:
                o_ref[...] = y.astype(o_ref.dtype)
            elif transposed_out:
                o_ref[0, j] = y.astype(o_ref.dtype)
            else:
                o_ref[pl.ds(j, k_out, stride=sbk), :] = y.astype(o_ref.dtype)


def _fft_stage(xs, x_sel, gmat, *, r_in, s_n, k_out, n_groups, n_cblk, transposed_out, real_out,
               out_dtype=F32, g2mat=None, tw=None, spec=None, spec_sel=None, gate=None, sbk=STAGE_ROWS,
               packed_in=False, packed_out=False, name="fft_stage"):
    n_in = len(xs)
    flat = s_n == 1
    sbk = 1 if flat else min(sbk, s_n)
    cb = LANES
    in_specs, args = [], []
    for x, sel in zip(xs, x_sel):
        if flat:
            in_specs.append(pl.BlockSpec((None, r_in, cb), lambda s, g, c, sel=sel: (sel(g, c)[0], 0, sel(g, c)[1])))
            args.append(x)
        else:
            xv = x.reshape(x.shape[0], x.shape[1] // s_n, s_n, x.shape[2])
            in_specs.append(pl.BlockSpec((1, r_in, sbk, cb),
                                         lambda s, g, c, sel=sel: (sel(g, c)[0], 0, s, sel(g, c)[1])))
            args.append(xv)
    in_specs.append(pl.BlockSpec(gmat.shape, lambda s, g, c: (0, 0, 0)))
    args.append(gmat)
    k_mid = gmat.shape[1] // (1 if real_out else 2)
    if g2mat is not None:
        in_specs.append(pl.BlockSpec(g2mat.shape, lambda s, g, c: (0, 0, 0)))
        args.append(g2mat)
    if tw is not None:
        for tarr in tw[:2]:
            in_specs.append(pl.BlockSpec((1, k_out, LANES), lambda s, g, c: (s, 0, 0)))
            args.append(tarr)
        for tarr in tw[2:]:
            in_specs.append(pl.BlockSpec((k_out, LANES), lambda s, g, c: (0, 0)))
            args.append(tarr)
    if spec is not None:
        *planes, nrm = spec
        for arr in planes:
            if flat:
                in_specs.append(pl.BlockSpec((k_mid, cb), lambda s, g, c: (0, spec_sel(g, c))))
                args.append(arr)
            else:
                in_specs.append(pl.BlockSpec((1, k_mid, sbk, cb), lambda s, g, c: (0, 0, s, spec_sel(g, c))))
                args.append(arr.reshape(1, k_mid, s_n, arr.shape[-1]))
        in_specs.append(pl.BlockSpec((1, cb), lambda s, g, c: (0, spec_sel(g, c))))
        args.append(nrm)
    if gate is not None:
        (ga, gblk), (za, zblk), bias = gate
        for arr, blk in ((ga, gblk), (za, zblk)):
            for bi in (0, 1):
                if flat:
                    in_specs.append(pl.BlockSpec((None, k_out, cb), lambda s, g, c, bi=bi, blk=blk: (bi, 0, blk + c)))
                    args.append(arr)
                else:
                    in_specs.append(pl.BlockSpec((1, k_out, sbk, cb),
                                                 lambda s, g, c, bi=bi, blk=blk: (bi, 0, s, blk + c)))
                    args.append(arr.reshape(arr.shape[0], k_out, s_n, arr.shape[-1]))
        in_specs.append(pl.BlockSpec((1, cb), lambda s, g, c: (0, c)))
        args.append(bias)
    ctot = n_cblk * cb
    if gate is not None:
        n_groups = 2
        if flat:
            oshape = (2, k_out, ctot)
            ospec = pl.BlockSpec((2, k_out, cb), lambda s, g, c: (0, 0, c))
        else:
            oshape = (2, k_out, s_n, ctot)
            ospec = pl.BlockSpec((2, k_out, sbk, cb), lambda s, g, c: (0, 0, s, c))
    elif flat:
        oshape = (n_groups, k_out, ctot)
        ospec = pl.BlockSpec((None, k_out, cb), lambda s, g, c: (g, 0, c))
    elif transposed_out:
        oshape = (n_groups, s_n, k_out, ctot)
        ospec = pl.BlockSpec((1, sbk, k_out, cb), lambda s, g, c: (g, s, 0, c))
    else:
        oshape = (n_groups, k_out, s_n, ctot)
        ospec = pl.BlockSpec((1, k_out, sbk, cb), lambda s, g, c: (g, 0, s, c))
    n_out = 1 if (real_out or gate is not None or packed_out) else 2
    if packed_out:
        out_dtype = jnp.int32
    kern = functools.partial(_stage_kernel, n_in=n_in, r_in=r_in, k_mid=k_mid, k_out=k_out, sbk=sbk,
                             tw=tw is not None, spec=spec is not None, second=g2mat is not None,
                             gate=gate is not None, real_out=real_out, transposed_out=transposed_out,
                             flat=flat, packed_in=packed_in, packed_spec=spec is not None and len(spec) == 2,
                             packed_out=packed_out)
    n_grid_groups = 1 if gate is not None else n_groups
    outs = pl.pallas_call(
        kern,
        grid=(s_n // sbk, n_grid_groups, n_cblk),
        in_specs=in_specs,
        out_specs=[ospec] * n_out,
        out_shape=[jax.ShapeDtypeStruct(oshape, out_dtype)] * n_out,
        compiler_params=_cparams("parallel", "parallel", "parallel"),
        name=name,
    )(*args)
    return [o.reshape(n_groups, -1, ctot) for o in outs]


def _split_len(n):
    if n <= 1024:
        return n, 1
    s = 128
    return n // s, s


def _fft_forward(xs, x_sel, n, n_rows, *, n_groups, n_cblk, name="fwd"):
    n1, s = _split_len(n)
    n_in = len(xs)
    if s == 1:
        g = _dft_mats(n, n_rows, n, -1.0, 1.0, n_in, False)
        return _fft_stage(xs, x_sel, g, r_in=n_rows, s_n=1, k_out=n, n_groups=n_groups, n_cblk=n_cblk,
                          transposed_out=False, real_out=False, name=name + "_direct")
    r1 = n_rows // s
    g1 = _dft_mats(n1, r1, n1, -1.0, 1.0, n_in, False)
    tw = _twiddle(s, n1, n, -1.0, STAGE_ROWS)
    (a,) = _fft_stage(xs, x_sel, g1, r_in=r1, s_n=s, k_out=n1, n_groups=n_groups, n_cblk=n_cblk,
                      transposed_out=True, real_out=False, tw=tw, packed_out=True, name=name + "_s1")
    g2 = _dft_mats(s, s, s, -1.0, 1.0, 2, False)
    return _fft_stage([a], [lambda g, c: (g, c)], g2, r_in=s, s_n=n1, k_out=s, n_groups=n_groups,
                      n_cblk=n_cblk, transposed_out=False, real_out=False, packed_in=True, packed_out=True,
                      name=name + "_s2")


def _hyena(uc, k_filt, nrm, hy_bias):
    b, n, _ = uc.shape
    w = HYENA_WIDTH
    wblk = w // LANES
    nfft = 2 * n
    n1, s = _split_len(nfft)
    ident = lambda g, c: (g, c)
    k_spec = _fft_forward([k_filt[None]], [lambda g, c: (0, c)], nfft, nfft, n_groups=1,
                          n_cblk=HYENA_ORDER * wblk, name="hy_filt_fft")
    z, zblk = uc, 2 * wblk
    for o in range(HYENA_ORDER):
        sel_r = lambda g, c, zblk=zblk: (0, zblk + c)
        sel_i = lambda g, c, zblk=zblk: (1, zblk + c)
        spec = tuple(p[0] for p in k_spec) + (nrm,)
        spec_sel = lambda g, c, o=o: o * wblk + c
        gate = ((uc, o * wblk), (z, zblk), hy_bias[o:o + 1])
        common = dict(n_groups=1, n_cblk=wblk, real_out=False)
        if s == 1:
            gf = _dft_mats(nfft, n, nfft, -1.0, 1.0, 2, False)
            gi = _dft_mats(n, nfft, nfft, 1.0, 1.0 / nfft, 2, False)
            (z,) = _fft_stage([z, z], [sel_r, sel_i], gf, r_in=n, s_n=1, k_out=n, transposed_out=False,
                              g2mat=gi, spec=spec, spec_sel=spec_sel, gate=gate, name="hy_direct", **common)
        else:
            r1 = n // s
            g1 = _dft_mats(n1, r1, n1, -1.0, 1.0, 2, False)
            (a,) = _fft_stage([z, z], [sel_r, sel_i], g1, r_in=r1, s_n=s, k_out=n1, transposed_out=True,
                              tw=_twiddle(s, n1, nfft, -1.0, STAGE_ROWS), packed_out=True, name="hy_s1",
                              **common)
            g2 = _dft_mats(s, s, s, -1.0, 1.0, 2, False)
            g3 = _dft_mats(s, s, s, 1.0, 1.0, 2, False)
            (q,) = _fft_stage([a], [ident], g2, r_in=s, s_n=n1, k_out=s, transposed_out=True,
                              g2mat=g3, tw=_twiddle(n1, s, nfft, 1.0, STAGE_ROWS), spec=spec,
                              spec_sel=spec_sel, packed_in=True, packed_out=True, name="hy_mid", **common)
            g4 = _dft_mats(n // s, n1, n1, 1.0, 1.0 / nfft, 2, False)
            (z,) = _fft_stage([q], [ident], g4, r_in=n1, s_n=s, k_out=n // s, transposed_out=False,
                              gate=gate, packed_in=True, name="hy_last", **common)
        zblk = 0
    return z.reshape(b * n, w)


def _fnet(pq):
    b, n, _ = pq.shape
    w = FNET_WIDTH
    wblk = w // LANES
    first = 3 * HYENA_WIDTH // LANES
    scale = 1.0 / math.sqrt(n * FNET_GROUP_DIM)
    sel_r = lambda g, c: (g, first + c)
    sel_i = lambda g, c: (g, first + wblk + c)
    ident = lambda g, c: (g, c)
    n1, s = _split_len(n)
    if s == 1:
        g = _dft_mats(n, n, n, -1.0, scale, 2, True)
        (y,) = _fft_stage([pq, pq], [sel_r, sel_i], g, r_in=n, s_n=1, k_out=n, n_groups=b, n_cblk=wblk,
                          transposed_out=False, real_out=True, name="fnet_direct")
        return y.reshape(b * n, w)
    g1 = _dft_mats(n1, n1, n1, -1.0, 1.0, 2, False)
    tw = _twiddle(s, n1, n, -1.0, STAGE_ROWS)
    (a,) = _fft_stage([pq, pq], [sel_r, sel_i], g1, r_in=n1, s_n=s, k_out=n1, n_groups=b, n_cblk=wblk,
                      transposed_out=True, real_out=False, tw=tw, packed_out=True, name="fnet_s1")
    g2 = _dft_mats(s, s, s, -1.0, scale, 2, True)
    (y,) = _fft_stage([a], [ident], g2, r_in=s, s_n=n1, k_out=s, n_groups=b, n_cblk=wblk,
                      transposed_out=False, real_out=True, packed_in=True, name="fnet_s2")
    return y.reshape(b * n, w)


def _merge_kernel(x_ref, gt_ref, a_ref, h_ref, f_ref, gate_ref, wa_ref, wh_ref, wf_ref, wo_ref, o_ref):
    d = D_MODEL
    m = gate_ref[:, 0:d].astype(F32) * _dot(a_ref[...], wa_ref[...])
    m = m + gate_ref[:, d:2 * d].astype(F32) * _dot(h_ref[...].astype(BF), wh_ref[...])
    m = m + gate_ref[:, 2 * d:3 * d].astype(F32) * _dot(f_ref[...].astype(BF), wf_ref[...])
    y = _dot(m.astype(BF), wo_ref[...])
    o_ref[...] = x_ref[...] + gt_ref[0] * y


def _merge(x2, gt, attn_o, hy_o, fn_o, gates, wa, wh, wf, wo, *, tm, mod_row):
    t, d = x2.shape
    tok = lambda i: (i, 0)
    full = lambda i: (0, 0)
    return pl.pallas_call(
        _merge_kernel,
        grid=(t // tm,),
        in_specs=[
            pl.BlockSpec((tm, d), tok),
            pl.BlockSpec((1, 1, d), lambda i: (mod_row(i), 0, 0)),
            pl.BlockSpec((tm, ATTN_WIDTH), tok),
            pl.BlockSpec((tm, HYENA_WIDTH), tok),
            pl.BlockSpec((tm, FNET_WIDTH), tok),
            pl.BlockSpec((tm, 3 * d), tok),
            pl.BlockSpec(wa.shape, full, pipeline_mode=pl.Buffered(1)),
            pl.BlockSpec(wh.shape, full, pipeline_mode=pl.Buffered(1)),
            pl.BlockSpec(wf.shape, full, pipeline_mode=pl.Buffered(1)),
            pl.BlockSpec(wo.shape, full, pipeline_mode=pl.Buffered(1)),
        ],
        out_specs=pl.BlockSpec((tm, d), tok),
        out_shape=jax.ShapeDtypeStruct((t, d), F32),
        compiler_params=_cparams("parallel"),
        name="merge",
    )(x2, gt, attn_o, hy_o, fn_o, gates, wa, wh, wf, wo)


def _ffn_kernel(x_ref, sh_ref, sc_ref, gt_ref, g_ref, wg_ref, wu_ref, wd_ref, o_ref, h_scr, acc_scr):
    f = pl.program_id(1)

    @pl.when(f == 0)
    def _():
        h_scr[...] = _modulated_norm(x_ref[...], g_ref[...], sc_ref[0], sh_ref[0]).astype(BF)
        acc_scr[...] = jnp.zeros_like(acc_scr)

    hb = h_scr[...]
    act = _silu(_dot(hb, wg_ref[...])) * _dot(hb, wu_ref[...])
    acc_scr[...] += _dot(act.astype(BF), wd_ref[...])

    @pl.when(f == pl.num_programs(1) - 1)
    def _():
        o_ref[...] = x_ref[...] + gt_ref[0] * acc_scr[...]


def _ffn(x2, sh, sc, gt, g, wg, wu, wd, *, tm, tf, mod_row):
    t, d = x2.shape
    ff = wg.shape[1]
    row3 = lambda i, f: (mod_row(i), 0, 0)
    wmode = dict(pipeline_mode=pl.Buffered(1)) if tf == ff else {}
    return pl.pallas_call(
        _ffn_kernel,
        grid=(t // tm, ff // tf),
        in_specs=[
            pl.BlockSpec((tm, d), lambda i, f: (i, 0)),
            pl.BlockSpec((1, 1, d), row3), pl.BlockSpec((1, 1, d), row3), pl.BlockSpec((1, 1, d), row3),
            pl.BlockSpec((1, d), lambda i, f: (0, 0)),
            pl.BlockSpec((d, tf), lambda i, f: (0, f), **wmode),
            pl.BlockSpec((d, tf), lambda i, f: (0, f), **wmode),
            pl.BlockSpec((tf, d), lambda i, f: (f, 0), **wmode),
        ],
        out_specs=pl.BlockSpec((tm, d), lambda i, f: (i, 0)),
        out_shape=jax.ShapeDtypeStruct((t, d), F32),
        scratch_shapes=[pltpu.VMEM((tm, d), BF), pltpu.VMEM((tm, d), F32)],
        compiler_params=_cparams("parallel", "arbitrary"),
        name="ffn_dense",
    )(x2, sh, sc, gt, g, wg, wu, wd)


def _top2(logits):
    lane = lax.broadcasted_iota(jnp.int32, logits.shape, 1)
    lg = jnp.where(lane < N_EXPERTS, logits, -jnp.inf)
    m1 = jnp.max(lg, axis=1, keepdims=True)
    i1 = jnp.min(jnp.where(lg == m1, lane, LANES), axis=1, keepdims=True)
    lg2 = jnp.where(lane == i1, -jnp.inf, lg)
    m2 = jnp.max(lg2, axis=1, keepdims=True)
    i2 = jnp.min(jnp.where(lg2 == m2, lane, LANES), axis=1, keepdims=True)
    e2 = jnp.exp(m2 - m1)
    w1 = 1.0 / (1.0 + e2)
    return i1, i2, w1, e2 * w1


GROUP_TILE = 256
GROUP_PAD = 64
GROUP_PIECES = GROUP_TILE // GROUP_PAD
SLOT_RADIX = 64.0


def _moe_group_kernel(x_ref, sh_ref, sc_ref, g_ref, wr_ref, xg_ref, ws_ref, slot_ref, cnt_ref,
                      h_scr, rows_scr, wm_scr):
    j = pl.program_id(1)
    tb = x_ref.shape[0]
    gt_rows = xg_ref.shape[1]

    @pl.when(j == 0)
    def _():
        h = _modulated_norm(x_ref[...], g_ref[...], sc_ref[0], sh_ref[0])
        h_scr[...] = h.astype(BF)
        i1, i2, w1, w2 = _top2(_dot3(h, wr_ref[...]))
        lane = lax.broadcasted_iota(jnp.int32, (tb, LANES), 1)
        oh0 = jnp.where(lane == i1, 1.0, 0.0)
        oh1 = jnp.where(lane == i2, 1.0, 0.0)
        c0 = jnp.sum(oh0, axis=0, keepdims=True)
        cnt = c0 + jnp.sum(oh1, axis=0, keepdims=True)
        tri = jnp.where(lax.broadcasted_iota(jnp.int32, (tb, tb), 1)
                        < lax.broadcasted_iota(jnp.int32, (tb, tb), 0), 1.0, 0.0).astype(BF)
        pre0 = _dot(tri, oh0.astype(BF))
        pre1 = _dot(tri, oh1.astype(BF)) + c0
        tiles = jnp.ceil(cnt * (1.0 / GROUP_PAD))
        upper = jnp.where(lax.broadcasted_iota(jnp.int32, (LANES, LANES), 0)
                          < lax.broadcasted_iota(jnp.int32, (LANES, LANES), 1), 1.0, 0.0).astype(BF)
        off = _dot(jnp.broadcast_to(tiles, (8, LANES)).astype(BF), upper)[0:1] * float(GROUP_PAD)
        slot0 = jnp.sum(oh0 * (off + pre0), axis=1, keepdims=True)
        slot1 = jnp.sum(oh1 * (off + pre1), axis=1, keepdims=True)
        slot_ref[0] = jnp.where(lane == 0, slot0, jnp.where(lane == 1, slot1, 0.0))
        cnt_ref[0] = jnp.broadcast_to(cnt, (8, LANES))
        hi0 = jnp.floor(slot0 * (1.0 / SLOT_RADIX))
        hi1 = jnp.floor(slot1 * (1.0 / SLOT_RADIX))
        digits = jnp.where(lane == 0, hi0, jnp.where(lane == 1, slot0 - SLOT_RADIX * hi0,
                           jnp.where(lane == 2, hi1, jnp.where(lane == 3, slot1 - SLOT_RADIX * hi1, 0.0))))
        sel = jnp.where(lax.broadcasted_iota(jnp.int32, (8, LANES), 0)
                        == lax.broadcasted_iota(jnp.int32, (8, LANES), 1), 1.0, 0.0).astype(BF)
        rows_scr[...] = _dot_nt(sel, digits.astype(BF))
        w1h, w1l = _split(w1)
        w1m, w1l = _split(w1 - w1h.astype(F32))
        w2h, w2l = _split(w2)
        w2m, w2l = _split(w2 - w2h.astype(F32))
        cols = [w1h, w1m, w1l, w2h, w2m, w2l]
        wm = jnp.zeros((tb, LANES), F32)
        for li, col in enumerate(cols):
            wm = jnp.where(lane == li, col.astype(F32), wm)
        wm_scr[...] = wm.astype(BF)

    rows = rows_scr[...]
    s0 = rows[0:1] * SLOT_RADIX + rows[1:2]
    s1 = rows[2:3] * SLOT_RADIX + rows[3:4]
    pos = (lax.broadcasted_iota(jnp.int32, (gt_rows, tb), 0) + j * gt_rows).astype(F32)
    g0 = jnp.where(pos == s0, 1.0, 0.0).astype(BF)
    g1 = jnp.where(pos == s1, 1.0, 0.0).astype(BF)
    xg_ref[0] = _dot(g0 + g1, h_scr[...]).astype(BF)
    lane_w = lax.broadcasted_iota(jnp.int32, (gt_rows, LANES), 1)
    wsum = (jnp.where(lane_w < 3, _dot(g0, wm_scr[...]), 0.0)
            + jnp.where((lane_w >= 3) & (lane_w < 6), _dot(g1, wm_scr[...]), 0.0))
    ws_ref[0] = jnp.broadcast_to(jnp.sum(wsum, axis=1, keepdims=True), (gt_rows, LANES))


def _moe_group(x2, sh, sc, g, wr_pad, *, tb, nt, mod_row):
    t, d = x2.shape
    nb = t // tb
    row3 = lambda b, j: (mod_row(b), 0, 0)
    return pl.pallas_call(
        _moe_group_kernel,
        grid=(nb, nt),
        in_specs=[
            pl.BlockSpec((tb, d), lambda b, j: (b, 0)),
            pl.BlockSpec((1, 1, d), row3), pl.BlockSpec((1, 1, d), row3),
            pl.BlockSpec((1, d), lambda b, j: (0, 0)),
            pl.BlockSpec((d, LANES), lambda b, j: (0, 0)),
        ],
        out_specs=[
            pl.BlockSpec((1, GROUP_TILE, d), lambda b, j: (b * nt + j, 0, 0)),
            pl.BlockSpec((1, GROUP_TILE, LANES), lambda b, j: (b * nt + j, 0, 0)),
            pl.BlockSpec((1, tb, LANES), lambda b, j: (b, 0, 0)),
            pl.BlockSpec((1, 8, LANES), lambda b, j: (b, 0, 0)),
        ],
        out_shape=[
            jax.ShapeDtypeStruct((nb * nt, GROUP_TILE, d), BF),
            jax.ShapeDtypeStruct((nb * nt, GROUP_TILE, LANES), F32),
            jax.ShapeDtypeStruct((nb, tb, LANES), F32),
            jax.ShapeDtypeStruct((nb, 8, LANES), F32),
        ],
        scratch_shapes=[pltpu.VMEM((tb, d), BF), pltpu.VMEM((8, tb), F32), pltpu.VMEM((tb, LANES), BF)],
        compiler_params=_cparams("parallel", "arbitrary"),
        name="moe_group",
    )(x2, sh, sc, g, wr_pad)


def _moe_schedule(cnt, nh):
    np_ = GROUP_PIECES
    h = (cnt + GROUP_PAD - 1) // GROUP_PAD
    nb, ne = h.shape
    tot = h.sum(0)
    pairs = (tot + np_ - 1) // np_
    cum_p = jnp.cumsum(pairs)
    start_p = cum_p - pairs
    n_used = cum_p[-1]
    n_steps = (nb * nh + ne * (np_ - 1)) // np_
    q = jnp.minimum(jnp.arange(n_steps, dtype=jnp.int32), n_used - 1)
    e = jnp.sum(q[:, None] >= cum_p[None, :], axis=1).astype(jnp.int32)
    r = q - start_p[e]
    cum_b = jnp.cumsum(h, axis=0)
    first = jnp.cumsum(h, axis=1) - h

    def piece(idx):
        idx = jnp.minimum(idx, tot[e] - 1)
        blk = jnp.sum(idx[:, None] >= cum_b.T[e], axis=1).astype(jnp.int32)
        return blk * nh + first[blk, e] + idx - (cum_b[blk, e] - h[blk, e])

    x = jnp.arange(nh, dtype=jnp.int32)[None, :]
    ex = jnp.sum(x[:, :, None] >= jnp.cumsum(h, axis=1)[:, None, :], axis=2).astype(jnp.int32)
    exc = jnp.minimum(ex, ne - 1)
    g = jnp.take_along_axis(cum_b - h, exc, axis=1) + x - jnp.take_along_axis(first, exc, axis=1)
    loc = np_ * (start_p[exc] + g // np_) + g % np_
    loc = jnp.where(ex < ne, loc, loc[:, 0:1])
    pieces = jnp.concatenate([piece(np_ * r + k) for k in range(np_)]).astype(jnp.int32)
    return (pieces, e, n_used.astype(jnp.int32).reshape(1), loc.reshape(-1).astype(jnp.int32),
            ((h.sum(1) + np_ - 1) // np_).astype(jnp.int32))


def _moe_expert_kernel(pc_ref, exp_ref, nused_ref, *refs):
    np_ = GROUP_PIECES
    x_refs, w_refs = refs[:np_], refs[np_:2 * np_]
    wg_ref, wu_ref, wd_ref, y_ref = refs[2 * np_:]

    @pl.when(pl.program_id(0) < nused_ref[0])
    def _():
        x = jnp.concatenate([r[0] for r in x_refs], axis=0)
        act = _silu(_dot(x, wg_ref[0])) * _dot(x, wu_ref[0])
        y = _dot(act.astype(BF), wd_ref[0])
        w = jnp.concatenate([r[0] for r in w_refs], axis=0)
        y_ref[0] = (y * jnp.concatenate([w] * (y.shape[1] // LANES), axis=1)).astype(BF)


def _moe_experts(pieces, step_exp, n_used, xg, ws, wg, wu, wd):
    d = xg.shape[-1]
    ff = wg.shape[2]
    np_ = GROUP_PIECES
    n_steps = step_exp.shape[0]
    xh = xg.reshape(-1, GROUP_PAD, d)
    wh = ws.reshape(-1, GROUP_PAD, LANES)
    pc3 = lambda k: (lambda i, pc, se, nu: (pc[k * n_steps + i], 0, 0))
    exp3 = lambda i, pc, se, nu: (se[i], 0, 0)
    return pl.pallas_call(
        _moe_expert_kernel,
        grid_spec=pltpu.PrefetchScalarGridSpec(
            num_scalar_prefetch=3,
            grid=(n_steps,),
            in_specs=(
                [pl.BlockSpec((1, GROUP_PAD, d), pc3(k)) for k in range(np_)]
                + [pl.BlockSpec((1, GROUP_PAD, LANES), pc3(k)) for k in range(np_)]
                + [pl.BlockSpec((1, d, ff), exp3), pl.BlockSpec((1, d, ff), exp3), pl.BlockSpec((1, ff, d), exp3)]
            ),
            out_specs=pl.BlockSpec((1, GROUP_TILE, d), lambda i, pc, se, nu: (jnp.minimum(i, nu[0] - 1), 0, 0)),
        ),
        out_shape=jax.ShapeDtypeStruct((n_steps, GROUP_TILE, d), BF),
        compiler_params=_cparams("arbitrary"),
        name="moe_experts",
    )(pieces, step_exp, n_used, *([xh] * np_), *([wh] * np_), wg, wu, wd)


def _moe_combine_kernel(nt_ref, loc_ref, x_ref, gt_ref, slot_ref, *refs):
    y_refs, o_ref, acc_scr = refs[:GROUP_PIECES], refs[GROUP_PIECES], refs[GROUP_PIECES + 1]
    b = pl.program_id(0)
    j = pl.program_id(1)
    tb = x_ref.shape[0]

    @pl.when(j == 0)
    def _():
        acc_scr[...] = jnp.zeros_like(acc_scr)

    @pl.when(j < nt_ref[b])
    def _():
        sl = slot_ref[0]
        pos = (lax.broadcasted_iota(jnp.int32, (tb, GROUP_TILE), 1) + j * GROUP_TILE).astype(F32)
        p = jnp.where((pos == sl[:, 0:1]) | (pos == sl[:, 1:2]), 1.0, 0.0).astype(BF)
        acc_scr[...] += _dot(p, jnp.concatenate([r[0] for r in y_refs], axis=0))

    @pl.when(j == pl.num_programs(1) - 1)
    def _():
        o_ref[...] = x_ref[...] + gt_ref[0] * acc_scr[...]


def _moe_combine(ntiles_b, loc, x2, gt, slots, yg, *, tb, nt, mod_row):
    t, d = x2.shape
    nb = t // tb
    np_ = GROUP_PIECES
    yh = yg.reshape(-1, GROUP_PAD, d)

    def piece3(k):
        return lambda b, j, n, lc: (lc[(b * nt + jnp.minimum(j, n[b] - 1)) * np_ + k], 0, 0)

    return pl.pallas_call(
        _moe_combine_kernel,
        grid_spec=pltpu.PrefetchScalarGridSpec(
            num_scalar_prefetch=2,
            grid=(nb, nt),
            in_specs=[
                pl.BlockSpec((tb, d), lambda b, j, n, lc: (b, 0)),
                pl.BlockSpec((1, 1, d), lambda b, j, n, lc: (mod_row(b), 0, 0)),
                pl.BlockSpec((1, tb, LANES), lambda b, j, n, lc: (b, 0, 0)),
            ] + [pl.BlockSpec((1, GROUP_PAD, d), piece3(k)) for k in range(np_)],
            out_specs=pl.BlockSpec((tb, d), lambda b, j, n, lc: (b, 0)),
            scratch_shapes=[pltpu.VMEM((tb, d), F32)],
        ),
        out_shape=jax.ShapeDtypeStruct((t, d), F32),
        compiler_params=_cparams("parallel", "arbitrary"),
        name="moe_combine",
    )(ntiles_b, loc, x2, gt, slots, *([yh] * np_))


def _moe(x2, sh, sc, gt, g, wr_pad, wg, wu, wd, *, tb, mod_row):
    nt = -(-(2 * tb + N_EXPERTS * (GROUP_PAD - 1)) // GROUP_TILE)
    xg, ws, slots, cnt = _moe_group(x2, sh, sc, g, wr_pad, tb=tb, nt=nt, mod_row=mod_row)
    counts = cnt[:, 0, :N_EXPERTS].astype(jnp.int32)
    pieces, step_exp, n_used, loc, ntiles_b = _moe_schedule(counts, nt * GROUP_PIECES)
    yg = _moe_experts(pieces, step_exp, n_used, xg, ws, wg, wu, wd)
    return _moe_combine(ntiles_b, loc, x2, gt, slots, yg, tb=tb, nt=nt, mod_row=mod_row)


def _rope_tables(seq_len):
    pos = np.arange(seq_len)
    prow = (pos // GRID_W).astype(np.float32)
    pcol = (pos % GRID_W).astype(np.float32)
    n_freq = HEAD_DIM // 4
    inv = (np.float32(ROPE_THETA) ** (-np.arange(n_freq, dtype=np.float32) / n_freq)).astype(np.float32)
    ar = (prow[:, None] * inv[None, :]).astype(np.float64)
    ac = (pcol[:, None] * inv[None, :]).astype(np.float64)
    cos = np.concatenate([np.cos(ar)] * 2 + [np.cos(ac)] * 2, axis=1)
    sin = np.concatenate([-np.sin(ar), np.sin(ar), -np.sin(ac), np.sin(ac)], axis=1)
    return (jnp.asarray(np.concatenate([cos, cos], axis=1), dtype=F32),
            jnp.asarray(np.concatenate([sin, sin], axis=1), dtype=F32))


def _head_sum_matrix():
    c = np.arange(ATTN_WIDTH)
    return jnp.asarray((c[:, None] // HEAD_DIM) == (c[None, :] // HEAD_DIM), dtype=F32).astype(BF)


def _fnet_channel_matrix():
    c = np.arange(FNET_WIDTH)
    same = (c[:, None] // FNET_GROUP_DIM) == (c[None, :] // FNET_GROUP_DIM)
    ang = 2.0 * np.pi * (((c[:, None] % FNET_GROUP_DIM) * (c[None, :] % FNET_GROUP_DIM)) % FNET_GROUP_DIM) / FNET_GROUP_DIM
    cb = np.where(same, np.cos(ang), 0.0)
    sb = np.where(same, np.sin(ang), 0.0)
    return jnp.asarray(np.concatenate([cb, -sb], axis=1), dtype=F32).astype(BF)


def _filter_features(n):
    t = np.linspace(0.0, 1.0, n)[:, None]
    w = 2.0 * np.pi * np.arange(n)[:, None] / n
    fb = np.linspace(1e-4, FILTER_BANDS - 1, FILTER_BANDS)
    z = np.concatenate([t, np.cos(fb * w), -np.sin(fb * w)], axis=-1)
    z = np.pad(z, ((0, 0), (0, 64 - z.shape[1])))
    return jnp.asarray(np.concatenate([z, np.zeros((1, z.shape[1])), z[:0:-1]], axis=0), dtype=F32)


def _decay_rates():
    d = jnp.abs(jnp.linspace(math.log(DECAY_TARGET) / SLOW_DECAY_PCT, math.log(DECAY_TARGET) / FAST_DECAY_PCT,
                             HYENA_WIDTH, dtype=F32))
    return jnp.concatenate([d] * HYENA_ORDER)[None, :]


def kernel(x, c, ctx, c_ctx, w_ada, b_ada, norm1_g, norm2_g, w_in, q_norm_g, k_norm_g, attn_sink,
           hy_conv_w, hy_conv_b, hy_filt_w1, hy_filt_b1, hy_filt_freq1, hy_filt_w2, hy_filt_b2,
           hy_filt_freq2, hy_filt_w3, hy_bias, w_proj_attn, w_proj_hyena, w_proj_fnet, w_out,
           ffn_w_gate, ffn_w_up, ffn_w_down, moe_router, moe_w_gate, moe_w_up, moe_w_down):
    b, seq, d = x.shape
    n_ctx = ctx.shape[1]
    depth = w_ada.shape[0]
    tm = 512
    tiles_per_seq = seq // tm

    cond8 = jnp.concatenate([c, c_ctx[None, :], jnp.zeros((8 - b - 1, d), F32)], axis=0)
    mods = _adaln(cond8, w_ada, b_ada)

    cos_l, sin_l = _rope_tables(seq)
    cos_c = jnp.ones((n_ctx, LANES), F32)
    sin_c = jnp.zeros((n_ctx, LANES), F32)
    gsum = _head_sum_matrix()
    mfn = _fnet_channel_matrix()
    deltas = _decay_rates()
    zfeat_l = _filter_features(seq)
    zfeat_c = _filter_features(n_ctx)

    lat_row = lambda i: i // tiles_per_seq
    ctx_row = lambda i: b
    lat_tab = lambda i: i % tiles_per_seq
    ctx_tab = lambda i: 0
    tm_c = min(tm, n_ctx)

    xs = x.reshape(b * seq, d)
    cs = ctx.reshape(b * n_ctx, d)
    for l in range(depth):
        last = l == depth - 1
        mod = lambda j: mods[l, :, j * d:(j + 1) * d].reshape(8, 1, d)
        w_in_bf = w_in[l].astype(BF)
        qg = jnp.tile(q_norm_g[l], N_HEADS)[None, :]
        kg = jnp.tile(k_norm_g[l], N_KV_HEADS)[None, :]
        g1 = norm1_g[l][None, :]
        wa, wh, wf, wo = (w_proj_attn[l].astype(BF), w_proj_hyena[l].astype(BF),
                          w_proj_fnet[l].astype(BF), w_out[l].astype(BF))
        conv_w = hy_conv_w[l].reshape(3, -1)
        conv_b = hy_conv_b[l][None, :]
        w1p = jnp.pad(hy_filt_w1[l], ((0, 64 - hy_filt_w1.shape[1]), (0, 0)))
        filt = (w1p, hy_filt_b1[l][None, :], hy_filt_freq1[l][None, :], hy_filt_w2[l],
                hy_filt_b2[l][None, :], hy_filt_freq2[l][None, :], hy_filt_w3[l], deltas)

        qkv_c, upq_c, gates_c = _phase_a(
            cs, mod(0), mod(1), g1, w_in_bf, cos_c, sin_c, qg, kg, gsum, mfn, conv_w, conv_b,
            tm=tm_c, tiles_per_seq=n_ctx // tm_c, mod_row=ctx_row, tab_row=ctx_tab)
        qkv_c = qkv_c.reshape(b, n_ctx, -1)
        upq_c = upq_c.reshape(b, n_ctx, -1)

        qkv_l, upq_l, gates_l = _phase_a(
            xs, mod(0), mod(1), g1, w_in_bf, cos_l, sin_l, qg, kg, gsum, mfn, conv_w, conv_b,
            tm=tm, tiles_per_seq=tiles_per_seq, mod_row=lat_row, tab_row=lat_tab)
        upq_l = upq_l.reshape(b, seq, -1)
        attn_l = _attention(attn_sink[l], qkv_l.reshape(b, seq, -1), qkv_c, local=True, tq=512)
        h_l, nrm_l = _hy_filter(zfeat_l, *filt, tm=512)
        hy_l = _hyena(upq_l, h_l, nrm_l, hy_bias[l])
        fn_l = _fnet(upq_l)
        xs = _merge(xs, mod(2), attn_l.reshape(b * seq, -1), hy_l, fn_l, gates_l, wa, wh, wf, wo,
                    tm=tm, mod_row=lat_row)

        if not last:
            attn_c = _attention(attn_sink[l], qkv_c, qkv_c, local=False, tq=n_ctx)
            h_c, nrm_c = _hy_filter(zfeat_c, *filt, tm=n_ctx)
            hy_c = _hyena(upq_c, h_c, nrm_c, hy_bias[l])
            fn_c = _fnet(upq_c)
            cs = _merge(cs, mod(2), attn_c.reshape(b * n_ctx, -1), hy_c, fn_c, gates_c, wa, wh, wf, wo,
                        tm=tm_c, mod_row=ctx_row)

        g2 = norm2_g[l][None, :]
        i = l // 2
        if l % 2 == 0:
            wg, wu, wd = ffn_w_gate[i].astype(BF), ffn_w_up[i].astype(BF), ffn_w_down[i].astype(BF)
            run = lambda t2, rows, tmm: _ffn(t2, mod(3), mod(4), mod(5), g2, wg, wu, wd,
                                             tm=tmm, tf=D_FF, mod_row=rows)
        else:
            wr = jnp.pad(moe_router[i], ((0, 0), (0, LANES - N_EXPERTS)))
            wg, wu, wd = moe_w_gate[i].astype(BF), moe_w_up[i].astype(BF), moe_w_down[i].astype(BF)
            run = lambda t2, rows, tmm: _moe(t2, mod(3), mod(4), mod(5), g2, wr, wg, wu, wd,
                                             tb=tmm, mod_row=rows)
        tm_ffn = 512 if l % 2 == 0 else 1024
        xs = run(xs, lambda t: t // (seq // tm_ffn), tm_ffn)
        if not last:
            cs = run(cs, ctx_row, min(tm_ffn, b * n_ctx))
    return xs.reshape(b, seq, d)
```

```python
import functools
import math

import numpy as np
import jax
import jax.numpy as jnp
from jax import lax
from jax.experimental import pallas as pl
from jax.experimental.pallas import tpu as pltpu

F32 = jnp.float32
BF = jnp.bfloat16

D_MODEL = 1024
DEPTH = 4
GRID_W = 64
HEAD_DIM = 64
N_HEADS = 8
N_KV_HEADS = 2
ATTN_WIDTH = N_HEADS * HEAD_DIM
KV_WIDTH = N_KV_HEADS * HEAD_DIM
WINDOW = 128
QBLK = 128
ROPE_THETA = 10000.0
HYENA_ORDER = 2
HYENA_WIDTH = 256
FILTER_BANDS = 16
FILTER_HIDDEN = 64
DECAY_TARGET = 1e-2
FAST_DECAY_PCT = 0.3
SLOW_DECAY_PCT = 1.5
FNET_WIDTH = 256
FNET_GROUP_DIM = 64
Q_END = ATTN_WIDTH
K_END = Q_END + KV_WIDTH
V_END = K_END + KV_WIDTH
HY_END = V_END + (HYENA_ORDER + 1) * HYENA_WIDTH
FN_END = HY_END + FNET_WIDTH
IN_WIDTH = FN_END + 3 * D_MODEL
D_FF = 2816
N_EXPERTS = 8
EPS = 1e-6
LANES = 128
NEG = -1e30
STAGE_ROWS = 32
STAGE_JOIN = 2

VMEM_LIMIT = 56 * 1024 * 1024


def _cparams(*sem):
    return pltpu.CompilerParams(dimension_semantics=sem, vmem_limit_bytes=VMEM_LIMIT)


def _dot(a, b):
    return jnp.dot(a, b, preferred_element_type=F32)


def _dot_nt(a, b):
    return lax.dot_general(a, b, (((1,), (1,)), ((), ())), preferred_element_type=F32)


def _split(a):
    hi = a.astype(BF)
    lo = (a - hi.astype(F32)).astype(BF)
    return hi, lo


def _dot3(a, b):
    ah, al = _split(a)
    bh, bl = _split(b)
    return _dot(ah, bh) + (_dot(ah, bl) + _dot(al, bh))


def _dot2(a, b_bf16):
    ah, al = _split(a)
    return _dot(ah, b_bf16) + _dot(al, b_bf16)


def _sigmoid(v):
    return 0.5 * jnp.tanh(0.5 * v) + 0.5


def _silu(v):
    return v * _sigmoid(v)


def _adaln_kernel(c_ref, w_ref, b_ref, o_ref):
    o_ref[0] = _dot3(_silu(c_ref[...]), w_ref[0]) + b_ref[0]


def _adaln(cond8, w_ada, b_ada):
    depth, d, n6 = w_ada.shape
    tn = 1024
    return pl.pallas_call(
        _adaln_kernel,
        grid=(depth, n6 // tn),
        in_specs=[
            pl.BlockSpec((8, d), lambda l, j: (0, 0)),
            pl.BlockSpec((1, d, tn), lambda l, j: (l, 0, j)),
            pl.BlockSpec((1, 1, tn), lambda l, j: (l, 0, j)),
        ],
        out_specs=pl.BlockSpec((1, 8, tn), lambda l, j: (l, 0, j)),
        out_shape=jax.ShapeDtypeStruct((depth, 8, n6), F32),
        compiler_params=_cparams("parallel", "parallel"),
        name="adaln",
    )(cond8, w_ada, b_ada.reshape(depth, 1, n6))


def _modulated_norm(x, g, sc, sh):
    ms = jnp.mean(x * x, axis=-1, keepdims=True)
    h = (x * lax.rsqrt(ms + EPS)) * g
    return h * (1.0 + sc) + sh


HALO = 16


def _phase_a_kernel(x_ref, xp_ref, xn_ref, sh_ref, sc_ref, g_ref, w_ref, cos_ref, sin_ref, qg_ref, kg_ref,
                    gsum_ref, mfn_ref, cw_ref, cb_ref, qkv_ref, upq_ref, gate_ref, *, tiles_per_seq):
    tm = x_ref.shape[0]
    norm = lambda xv: _modulated_norm(xv, g_ref[...], sc_ref[0], sh_ref[0]).astype(BF)
    hb = norm(x_ref[...])
    cos = cos_ref[...]
    sin = sin_ref[...]

    def headnorm(t, gain, gs):
        ss = _dot2(t * t, gs)
        return t * lax.rsqrt(ss * (1.0 / HEAD_DIM) + EPS) * gain

    def rope(t, cosw, sinw):
        w = t.shape[1]
        nxt = pltpu.roll(t, w - 16, axis=1)
        prv = pltpu.roll(t, 16, axis=1)
        lw = lax.broadcasted_iota(jnp.int32, t.shape, 1)
        return t * cosw + jnp.where((lw % 32) < 16, nxt, prv) * sinw

    def dup_halves(t):
        lane = lax.broadcasted_iota(jnp.int32, t.shape, 1)
        sw = pltpu.roll(t, 64, axis=1)
        lo = lane < 64
        return jnp.concatenate([jnp.where(lo, t, sw), jnp.where(lo, sw, t)], axis=1)

    pr = _dot(hb, w_ref[:, 0:FN_END])
    pq = pr[:, 0:Q_END]
    qn = headnorm(pq, qg_ref[...], gsum_ref[...])
    cos4 = jnp.concatenate([cos] * 4, axis=1)
    sin4 = jnp.concatenate([sin] * 4, axis=1)
    qkv_ref[:, 0:Q_END] = (rope(qn, cos4, sin4) * (HEAD_DIM ** -0.5)).astype(BF)

    kn = headnorm(pr[:, Q_END:K_END], kg_ref[...], gsum_ref[0:KV_WIDTH, 0:KV_WIDTH])
    qkv_ref[:, Q_END:Q_END + 2 * KV_WIDTH] = dup_halves(rope(kn, cos, sin)).astype(BF)
    qkv_ref[:, Q_END + 2 * KV_WIDTH:] = dup_halves(pr[:, K_END:V_END]).astype(BF)

    w_hy = w_ref[:, V_END:HY_END]
    u = pr[:, V_END:HY_END]
    tile = pl.program_id(0) % tiles_per_seq
    u_before = jnp.where(tile == 0, 0.0, _dot(norm(xp_ref[...]), w_hy)[HALO - 1:HALO])
    u_after = jnp.where(tile == tiles_per_seq - 1, 0.0, _dot(norm(xn_ref[...]), w_hy)[0:1])
    row = lax.broadcasted_iota(jnp.int32, (tm, 1), 0)
    prv = jnp.where(row == 0, u_before, pltpu.roll(u, 1, axis=0))
    nxt = jnp.where(row == tm - 1, u_after, pltpu.roll(u, tm - 1, axis=0))
    n_hy = HY_END - V_END
    upq_ref[:, 0:n_hy] = prv * cw_ref[0:1, :] + u * cw_ref[1:2, :] + nxt * cw_ref[2:3, :] + cb_ref[...]

    upq_ref[:, n_hy:] = _dot(pr[:, HY_END:FN_END].astype(BF), mfn_ref[...])
    gate_ref[...] = _sigmoid(_dot(hb, w_ref[:, FN_END:])).astype(BF)


def _phase_a(x2, sh, sc, g, w_in_bf, cos_t, sin_t, qg, kg, gsum, mfn, conv_w, conv_b, *, tm, tiles_per_seq,
             mod_row, tab_row):
    t, d = x2.shape
    row3 = lambda i: (mod_row(i), 0, 0)
    full = lambda i: (0, 0)
    tok = lambda i: (i, 0)
    n_halo = t // HALO
    outs = [
        ((t, ATTN_WIDTH + 4 * KV_WIDTH), BF), ((t, 3 * HYENA_WIDTH + 2 * FNET_WIDTH), F32),
        ((t, 3 * D_MODEL), BF),
    ]
    kern = functools.partial(_phase_a_kernel, tiles_per_seq=tiles_per_seq)
    return pl.pallas_call(
        kern,
        grid=(t // tm,),
        in_specs=[
            pl.BlockSpec((tm, d), tok),
            pl.BlockSpec((HALO, d), lambda i: (jnp.maximum(i * (tm // HALO) - 1, 0), 0)),
            pl.BlockSpec((HALO, d), lambda i: (jnp.minimum((i + 1) * (tm // HALO), n_halo - 1), 0)),
            pl.BlockSpec((1, 1, d), row3),
            pl.BlockSpec((1, 1, d), row3),
            pl.BlockSpec((1, d), full),
            pl.BlockSpec((d, IN_WIDTH), full, pipeline_mode=pl.Buffered(1)),
            pl.BlockSpec((tm, LANES), lambda i: (tab_row(i), 0)),
            pl.BlockSpec((tm, LANES), lambda i: (tab_row(i), 0)),
            pl.BlockSpec((1, ATTN_WIDTH), full),
            pl.BlockSpec((1, KV_WIDTH), full),
            pl.BlockSpec((ATTN_WIDTH, ATTN_WIDTH), full),
            pl.BlockSpec((FNET_WIDTH, 2 * FNET_WIDTH), full),
            pl.BlockSpec(conv_w.shape, full),
            pl.BlockSpec(conv_b.shape, full),
        ],
        out_specs=[pl.BlockSpec((tm, s[1]), tok) for s, _ in outs],
        out_shape=[jax.ShapeDtypeStruct(s, dt) for s, dt in outs],
        compiler_params=_cparams("parallel"),
        name="phase_a",
    )(x2, x2, x2, sh, sc, g, w_in_bf, cos_t, sin_t, qg, kg, gsum, mfn, conv_w, conv_b)


def _attn_kernel(sink_ref, q_ref, kd_ref, vd_ref, kc_ref, vc_ref, o_ref, *, local, seq_len):
    tq = q_ref.shape[1]
    nblk = tq // QBLK
    gq = N_HEADS // N_KV_HEADS
    rows = gq * QBLK
    lane = lax.broadcasted_iota(jnp.int32, (QBLK, LANES), 1)
    lo_half = lane < 64
    hrow = lax.broadcasted_iota(jnp.int32, (rows, 1), 0) // QBLK
    nband = 3 * QBLK
    if local:
        qk_off = (lax.broadcasted_iota(jnp.int32, (rows, nband), 0) % QBLK
                  - lax.broadcasted_iota(jnp.int32, (rows, nband), 1))
    for blk in range(nblk):
        r0 = blk * QBLK
        qb = q_ref[0, r0:r0 + QBLK, :]
        if local:
            n = pl.program_id(1) * nblk + blk
            start = pl.multiple_of(jnp.clip((n - 1) * QBLK, 0, seq_len - nband), QBLK)
            valid = jnp.abs(qk_off + (n * QBLK - start)) <= WINDOW
        for g in range(N_KV_HEADS):
            parts = []
            for hh in range(gq):
                h = gq * g + hh
                qc = qb[:, (h // 2) * LANES:(h // 2 + 1) * LANES]
                keep = lo_half if h % 2 == 0 else jnp.logical_not(lo_half)
                parts.append(jnp.where(keep, qc, jnp.zeros_like(qc)))
            q4 = jnp.concatenate(parts, axis=0)
            sk = jnp.full((rows, 1), sink_ref[gq * g + gq - 1], F32)
            for hh in range(gq - 2, -1, -1):
                sk = jnp.where(hrow == hh, sink_ref[gq * g + hh], sk)
            gl = slice(g * LANES, (g + 1) * LANES)
            s_ctx = _dot_nt(q4, kc_ref[0, :, gl])
            m = jnp.maximum(jnp.max(s_ctx, axis=1, keepdims=True), sk)
            if local:
                s_loc = _dot_nt(q4, kd_ref[0, pl.ds(start, nband), gl])
                s_loc = jnp.where(valid, s_loc, NEG)
                m = jnp.maximum(m, jnp.max(s_loc, axis=1, keepdims=True))
            p_ctx = jnp.exp(s_ctx - m)
            den = jnp.sum(p_ctx, axis=1, keepdims=True) + jnp.exp(sk - m)
            o = _dot(p_ctx.astype(BF), vc_ref[0, :, gl])
            if local:
                p_loc = jnp.exp(s_loc - m)
                den = den + jnp.sum(p_loc, axis=1, keepdims=True)
                o = o + _dot(p_loc.astype(BF), vd_ref[0, pl.ds(start, nband), gl])
            o = o / den
            for cc in range(gq // 2):
                col = (gq // 2) * g + cc
                oa = o[(2 * cc) * QBLK:(2 * cc + 1) * QBLK]
                ob = o[(2 * cc + 1) * QBLK:(2 * cc + 2) * QBLK]
                o_ref[0, r0:r0 + QBLK, col * LANES:(col + 1) * LANES] = (
                    jnp.where(lo_half, oa, ob).astype(BF))


def _attention(sink, qkv, qkv_ctx, *, local, tq):
    b, lq, _ = qkv.shape
    c = qkv_ctx.shape[1]
    kw = 2 * KV_WIDTH
    k_blk, v_blk = ATTN_WIDTH // kw, ATTN_WIDTH // kw + 1
    kern = functools.partial(_attn_kernel, local=local, seq_len=lq)
    return pl.pallas_call(
        kern,
        grid=(b, lq // tq),
        in_specs=[
            pl.BlockSpec(memory_space=pltpu.SMEM),
            pl.BlockSpec((1, tq, ATTN_WIDTH), lambda bi, i: (bi, i, 0)),
            pl.BlockSpec((1, lq, kw), lambda bi, i: (bi, 0, k_blk)),
            pl.BlockSpec((1, lq, kw), lambda bi, i: (bi, 0, v_blk)),
            pl.BlockSpec((1, c, kw), lambda bi, i: (bi, 0, k_blk)),
            pl.BlockSpec((1, c, kw), lambda bi, i: (bi, 0, v_blk)),
        ],
        out_specs=pl.BlockSpec((1, tq, ATTN_WIDTH), lambda bi, i: (bi, i, 0)),
        out_shape=jax.ShapeDtypeStruct((b, lq, ATTN_WIDTH), BF),
        compiler_params=_cparams("parallel", "parallel"),
        name="attn_local" if local else "attn_ctx",
    )(sink, qkv, qkv, qkv, qkv_ctx, qkv_ctx)


def _hy_filter_kernel(z_ref, w1_ref, b1_ref, f1_ref, w2_ref, b2_ref, f2_ref, w3_ref, dl_ref,
                      k_ref, nrm_ref, *, n_half):
    i = pl.program_id(0)
    tm = z_ref.shape[0]
    z = z_ref[...]
    h = jnp.sin(f1_ref[...] * (_dot3(z, w1_ref[...]) + b1_ref[...]))
    h = jnp.sin(f2_ref[...] * (_dot3(h, w2_ref[...]) + b2_ref[...]))
    h = _dot3(h, w3_ref[...])
    h = h * jnp.exp(-z[:, 0:1] * dl_ref[...])
    row = i * tm + lax.broadcasted_iota(jnp.int32, (tm, 1), 0)
    h = jnp.where(row == n_half, 0.0, h)
    k_ref[...] = h

    @pl.when(i == 0)
    def _():
        nrm_ref[...] = jnp.zeros_like(nrm_ref)

    nrm_ref[...] += jnp.sum(jnp.abs(h), axis=0, keepdims=True)


def _hy_filter(zfeat2, w1p, b1, f1, w2, b2, f2, w3, deltas2, *, tm):
    n2 = zfeat2.shape[0]
    n_half = n2 // 2
    wout = w3.shape[1] // 2
    full = lambda i: (0, 0)
    kern = functools.partial(_hy_filter_kernel, n_half=n_half)
    return pl.pallas_call(
        kern,
        grid=(n2 // tm,),
        in_specs=[
            pl.BlockSpec((tm, zfeat2.shape[1]), lambda i: (i, 0)),
            pl.BlockSpec(w1p.shape, full), pl.BlockSpec(b1.shape, full), pl.BlockSpec(f1.shape, full),
            pl.BlockSpec(w2.shape, full), pl.BlockSpec(b2.shape, full), pl.BlockSpec(f2.shape, full),
            pl.BlockSpec((w3.shape[0], wout), lambda i: (0, (i * tm) // n_half)),
            pl.BlockSpec(deltas2.shape, full),
        ],
        out_specs=[pl.BlockSpec((tm, wout), lambda i: (i, 0)), pl.BlockSpec((1, wout), full)],
        out_shape=[jax.ShapeDtypeStruct((n2, wout), F32), jax.ShapeDtypeStruct((1, wout), F32)],
        compiler_params=_cparams("arbitrary"),
        name="hy_filter",
    )(zfeat2, w1p, b1, f1, w2, b2, f2, w3, deltas2)


def _dft_mats(k_out, r_in, period, sign, scale, n_in, real_out):
    k = np.arange(k_out)[:, None]
    r = np.arange(r_in)[None, :]
    ang = 2.0 * np.pi * ((k * r) % period) / period
    fr = np.cos(ang) * scale
    fi = sign * np.sin(ang) * scale
    if real_out:
        mats = [fr, -fi]
    else:
        mats = [np.concatenate([fr, fi], 0), np.concatenate([-fi, fr], 0)]
    return jnp.asarray(np.stack(mats[:n_in], 0), dtype=F32).astype(BF)


def _twiddle(s_n, k_n, n, sign, sbk):
    s0 = lax.broadcasted_iota(jnp.int32, (s_n // sbk, k_n, LANES), 0) * sbk
    k = lax.broadcasted_iota(jnp.int32, (s_n // sbk, k_n, LANES), 1)
    ang = (s0 * k).astype(F32) * (2.0 * math.pi / n)
    ang1 = lax.broadcasted_iota(jnp.int32, (k_n, LANES), 0).astype(F32) * (2.0 * math.pi / n)
    return jnp.cos(ang), sign * jnp.sin(ang), jnp.cos(ang1), sign * jnp.sin(ang1)


def _unpack_pair(p):
    return [pltpu.unpack_elementwise(p, index=i, packed_dtype=BF, unpacked_dtype=F32) for i in (0, 1)]


def _stage_kernel(*refs, n_in, r_in, k_mid, k_out, sbk, tw, spec, second, gate, real_out,
                  transposed_out, flat, packed_in, packed_spec, packed_out):
    it = iter(refs)
    x_refs = [next(it) for _ in range(n_in)]
    g_ref = next(it)
    g2_ref = next(it) if second else None
    tw_refs = [next(it) for _ in range(4)] if tw else None
    n_spec = 1 if packed_spec else 2
    spec_refs = [next(it) for _ in range(n_spec + 1)] if spec else None
    gate_refs = [next(it) for _ in range(5)] if gate else None
    out_refs = [next(it)] if (real_out or gate or packed_out) else [next(it), next(it)]
    if not flat:
        x_refs = [r.reshape(r_in * sbk, LANES) for r in x_refs]
        if spec:
            spec_refs = [r.reshape(k_mid * sbk, LANES) for r in spec_refs[:n_spec]] + spec_refs[n_spec:]
        if gate:
            gate_refs = [r.reshape(k_out * sbk, LANES) for r in gate_refs[:4]] + gate_refs[4:]
            out_refs = [out_refs[0].reshape(2 * k_out * sbk, LANES)]
        elif not transposed_out:
            out_refs = [r.reshape(k_out * sbk, LANES) for r in out_refs]
    if spec:
        inv = 1.0 / spec_refs[n_spec][...]
    if tw:
        tr, ti = tw_refs[0][0], tw_refs[1][0]
        wr, wi = tw_refs[2][...], tw_refs[3][...]
    def joined_dots(mats_ref, cols):
        acc = None
        for xi in range(len(cols[0])):
            wide = jnp.concatenate([c[xi].astype(BF) for c in cols], axis=1)
            d = _dot(mats_ref[xi], wide)
            acc = d if acc is None else acc + d
        return [acc[:, k * LANES:(k + 1) * LANES] for k in range(len(cols))]

    join = 1 if flat else min(STAGE_JOIN, sbk)
    results = {}
    for j in range(sbk):
        if j % join == 0:
            cols = []
            for jj in range(j, j + join):
                parts = [x_ref[...] if flat else x_ref[pl.ds(jj, r_in, stride=sbk), :] for x_ref in x_refs]
                cols.append(_unpack_pair(parts[0]) if packed_in else parts)
            accs = joined_dots(g_ref, cols)
            if not real_out:
                mids = []
                for jj, acc in zip(range(j, j + join), accs):
                    yr, yi = acc[:k_mid], acc[k_mid:]
                    if spec:
                        rows = slice(None) if flat else pl.ds(jj, k_mid, stride=sbk)
                        if packed_spec:
                            sr, si = _unpack_pair(spec_refs[0][rows, :])
                        else:
                            sr, si = spec_refs[0][rows, :], spec_refs[1][rows, :]
                        sr, si = sr * inv, si * inv
                        yr, yi = yr * sr - yi * si, yr * si + yi * sr
                    mids.append([yr, yi])
                if second:
                    mids = [[acc[:k_out], acc[k_out:]] for acc in joined_dots(g2_ref, mids)]
                accs = mids
            results = dict(zip(range(j, j + join), accs))
        if real_out:
            ys = [results[j]]
        else:
            yr, yi = results[j]
            if tw:
                yr, yi = yr * tr - yi * ti, yr * ti + yi * tr
                if j + 1 < sbk:
                    tr, ti = tr * wr - ti * wi, tr * wi + ti * wr
            ys = [yr, yi]
        if gate:
            o_ref = out_refs[0]
            for part, y in enumerate(ys):
                rows = slice(None) if flat else pl.ds(j, k_out, stride=sbk)
                val = gate_refs[part][rows, :] * (y + gate_refs[4][...] * gate_refs[2 + part][rows, :])
                if flat:
                    o_ref[part] = val.astype(o_ref.dtype)
                else:
                    o_ref[pl.ds(part * k_out * sbk + j, k_out, stride=sbk), :] = val.astype(o_ref.dtype)
            continue
        if packed_out:
            ys = [pltpu.pack_elementwise(ys, packed_dtype=BF)]
        for o_ref, y in zip(out_refs, ys):
            if flat:
                o_ref[...] = y.astype(o_ref.dtype)
            elif transposed_out:
                o_ref[0, j] = y.astype(o_ref.dtype)
            else:
                o_ref[pl.ds(j, k_out, stride=sbk), :] = y.astype(o_ref.dtype)


def _fft_stage(xs, x_sel, gmat, *, r_in, s_n, k_out, n_groups, n_cblk, transposed_out, real_out,
               out_dtype=F32, g2mat=None, tw=None, spec=None, spec_sel=None, gate=None, sbk=STAGE_ROWS,
               packed_in=False, packed_out=False, name="fft_stage"):
    n_in = len(xs)
    flat = s_n == 1
    sbk = 1 if flat else min(sbk, s_n)
    cb = LANES
    in_specs, args = [], []
    for x, sel in zip(xs, x_sel):
        if flat:
            in_specs.append(pl.BlockSpec((None, r_in, cb), lambda s, g, c, sel=sel: (sel(g, c)[0], 0, sel(g, c)[1])))
            args.append(x)
        else:
            xv = x.reshape(x.shape[0], x.shape[1] // s_n, s_n, x.shape[2])
            in_specs.append(pl.BlockSpec((1, r_in, sbk, cb),
                                         lambda s, g, c, sel=sel: (sel(g, c)[0], 0, s, sel(g, c)[1])))
            args.append(xv)
    in_specs.append(pl.BlockSpec(gmat.shape, lambda s, g, c: (0, 0, 0)))
    args.append(gmat)
    k_mid = gmat.shape[1] // (1 if real_out else 2)
    if g2mat is not None:
        in_specs.append(pl.BlockSpec(g2mat.shape, lambda s, g, c: (0, 0, 0)))
        args.append(g2mat)
    if tw is not None:
        for tarr in tw[:2]:
            in_specs.append(pl.BlockSpec((1, k_out, LANES), lambda s, g, c: (s, 0, 0)))
            args.append(tarr)
        for tarr in tw[2:]:
            in_specs.append(pl.BlockSpec((k_out, LANES), lambda s, g, c: (0, 0)))
            args.append(tarr)
    if spec is not None:
        *planes, nrm = spec
        for arr in planes:
            if flat:
                in_specs.append(pl.BlockSpec((k_mid, cb), lambda s, g, c: (0, spec_sel(g, c))))
                args.append(arr)
            else:
                in_specs.append(pl.BlockSpec((1, k_mid, sbk, cb), lambda s, g, c: (0, 0, s, spec_sel(g, c))))
                args.append(arr.reshape(1, k_mid, s_n, arr.shape[-1]))
        in_specs.append(pl.BlockSpec((1, cb), lambda s, g, c: (0, spec_sel(g, c))))
        args.append(nrm)
    if gate is not None:
        (ga, gblk), (za, zblk), bias = gate
        for arr, blk in ((ga, gblk), (za, zblk)):
            for bi in (0, 1):
                if flat:
                    in_specs.append(pl.BlockSpec((None, k_out, cb), lambda s, g, c, bi=bi, blk=blk: (bi, 0, blk + c)))
                    args.append(arr)
                else:
                    in_specs.append(pl.BlockSpec((1, k_out, sbk, cb),
                                                 lambda s, g, c, bi=bi, blk=blk: (bi, 0, s, blk + c)))
                    args.append(arr.reshape(arr.shape[0], k_out, s_n, arr.shape[-1]))
        in_specs.append(pl.BlockSpec((1, cb), lambda s, g, c: (0, c)))
        args.append(bias)
    ctot = n_cblk * cb
    if gate is not None:
        n_groups = 2
        if flat:
            oshape = (2, k_out, ctot)
            ospec = pl.BlockSpec((2, k_out, cb), lambda s, g, c: (0, 0, c))
        else:
            oshape = (2, k_out, s_n, ctot)
            ospec = pl.BlockSpec((2, k_out, sbk, cb), lambda s, g, c: (0, 0, s, c))
    elif flat:
        oshape = (n_groups, k_out, ctot)
        ospec = pl.BlockSpec((None, k_out, cb), lambda s, g, c: (g, 0, c))
    elif transposed_out:
        oshape = (n_groups, s_n, k_out, ctot)
        ospec = pl.BlockSpec((1, sbk, k_out, cb), lambda s, g, c: (g, s, 0, c))
    else:
        oshape = (n_groups, k_out, s_n, ctot)
        ospec = pl.BlockSpec((1, k_out, sbk, cb), lambda s, g, c: (g, 0, s, c))
    n_out = 1 if (real_out or gate is not None or packed_out) else 2
    if packed_out:
        out_dtype = jnp.int32
    kern = functools.partial(_stage_kernel, n_in=n_in, r_in=r_in, k_mid=k_mid, k_out=k_out, sbk=sbk,
                             tw=tw is not None, spec=spec is not None, second=g2mat is not None,
                             gate=gate is not None, real_out=real_out, transposed_out=transposed_out,
                             flat=flat, packed_in=packed_in, packed_spec=spec is not None and len(spec) == 2,
                             packed_out=packed_out)
    n_grid_groups = 1 if gate is not None else n_groups
    outs = pl.pallas_call(
        kern,
        grid=(s_n // sbk, n_grid_groups, n_cblk),
        in_specs=in_specs,
        out_specs=[ospec] * n_out,
        out_shape=[jax.ShapeDtypeStruct(oshape, out_dtype)] * n_out,
        compiler_params=_cparams("parallel", "parallel", "parallel"),
        name=name,
    )(*args)
    return [o.reshape(n_groups, -1, ctot) for o in outs]


def _split_len(n):
    if n <= 1024:
        return n, 1
    s = 128
    return n // s, s


def _fft_forward(xs, x_sel, n, n_rows, *, n_groups, n_cblk, name="fwd"):
    n1, s = _split_len(n)
    n_in = len(xs)
    if s == 1:
        g = _dft_mats(n, n_rows, n, -1.0, 1.0, n_in, False)
        return _fft_stage(xs, x_sel, g, r_in=n_rows, s_n=1, k_out=n, n_groups=n_groups, n_cblk=n_cblk,
                          transposed_out=False, real_out=False, name=name + "_direct")
    r1 = n_rows // s
    g1 = _dft_mats(n1, r1, n1, -1.0, 1.0, n_in, False)
    tw = _twiddle(s, n1, n, -1.0, STAGE_ROWS)
    (a,) = _fft_stage(xs, x_sel, g1, r_in=r1, s_n=s, k_out=n1, n_groups=n_groups, n_cblk=n_cblk,
                      transposed_out=True, real_out=False, tw=tw, packed_out=True, name=name + "_s1")
    g2 = _dft_mats(s, s, s, -1.0, 1.0, 2, False)
    return _fft_stage([a], [lambda g, c: (g, c)], g2, r_in=s, s_n=n1, k_out=s, n_groups=n_groups,
                      n_cblk=n_cblk, transposed_out=False, real_out=False, packed_in=True, packed_out=True,
                      name=name + "_s2")


def _hyena(uc, k_filt, nrm, hy_bias):
    b, n, _ = uc.shape
    w = HYENA_WIDTH
    wblk = w // LANES
    nfft = 2 * n
    n1, s = _split_len(nfft)
    ident = lambda g, c: (g, c)
    k_spec = _fft_forward([k_filt[None]], [lambda g, c: (0, c)], nfft, nfft, n_groups=1,
                          n_cblk=HYENA_ORDER * wblk, name="hy_filt_fft")
    z, zblk = uc, 2 * wblk
    for o in range(HYENA_ORDER):
        sel_r = lambda g, c, zblk=zblk: (0, zblk + c)
        sel_i = lambda g, c, zblk=zblk: (1, zblk + c)
        spec = tuple(p[0] for p in k_spec) + (nrm,)
        spec_sel = lambda g, c, o=o: o * wblk + c
        gate = ((uc, o * wblk), (z, zblk), hy_bias[o:o + 1])
        common = dict(n_groups=1, n_cblk=wblk, real_out=False)
        if s == 1:
            gf = _dft_mats(nfft, n, nfft, -1.0, 1.0, 2, False)
            gi = _dft_mats(n, nfft, nfft, 1.0, 1.0 / nfft, 2, False)
            (z,) = _fft_stage([z, z], [sel_r, sel_i], gf, r_in=n, s_n=1, k_out=n, transposed_out=False,
                              g2mat=gi, spec=spec, spec_sel=spec_sel, gate=gate, name="hy_direct", **common)
        else:
            r1 = n // s
            g1 = _dft_mats(n1, r1, n1, -1.0, 1.0, 2, False)
            (a,) = _fft_stage([z, z], [sel_r, sel_i], g1, r_in=r1, s_n=s, k_out=n1, transposed_out=True,
                              tw=_twiddle(s, n1, nfft, -1.0, STAGE_ROWS), packed_out=True, name="hy_s1",
                              **common)
            g2 = _dft_mats(s, s, s, -1.0, 1.0, 2, False)
            g3 = _dft_mats(s, s, s, 1.0, 1.0, 2, False)
            (q,) = _fft_stage([a], [ident], g2, r_in=s, s_n=n1, k_out=s, transposed_out=True,
                              g2mat=g3, tw=_twiddle(n1, s, nfft, 1.0, STAGE_ROWS), spec=spec,
                              spec_sel=spec_sel, packed_in=True, packed_out=True, name="hy_mid", **common)
            g4 = _dft_mats(n // s, n1, n1, 1.0, 1.0 / nfft, 2, False)
            (z,) = _fft_stage([q], [ident], g4, r_in=n1, s_n=s, k_out=n // s, transposed_out=False,
                              gate=gate, packed_in=True, name="hy_last", **common)
        zblk = 0
    return z.reshape(b * n, w)


def _fnet(pq):
    b, n, _ = pq.shape
    w = FNET_WIDTH
    wblk = w // LANES
    first = 3 * HYENA_WIDTH // LANES
    scale = 1.0 / math.sqrt(n * FNET_GROUP_DIM)
    sel_r = lambda g, c: (g, first + c)
    sel_i = lambda g, c: (g, first + wblk + c)
    ident = lambda g, c: (g, c)
    n1, s = _split_len(n)
    if s == 1:
        g = _dft_mats(n, n, n, -1.0, scale, 2, True)
        (y,) = _fft_stage([pq, pq], [sel_r, sel_i], g, r_in=n, s_n=1, k_out=n, n_groups=b, n_cblk=wblk,
                          transposed_out=False, real_out=True, name="fnet_direct")
        return y.reshape(b * n, w)
    g1 = _dft_mats(n1, n1, n1, -1.0, 1.0, 2, False)
    tw = _twiddle(s, n1, n, -1.0, STAGE_ROWS)
    (a,) = _fft_stage([pq, pq], [sel_r, sel_i], g1, r_in=n1, s_n=s, k_out=n1, n_groups=b, n_cblk=wblk,
                      transposed_out=True, real_out=False, tw=tw, packed_out=True, name="fnet_s1")
    g2 = _dft_mats(s, s, s, -1.0, scale, 2, True)
    (y,) = _fft_stage([a], [ident], g2, r_in=s, s_n=n1, k_out=s, n_groups=b, n_cblk=wblk,
                      transposed_out=False, real_out=True, packed_in=True, name="fnet_s2")
    return y.reshape(b * n, w)


def _merge_kernel(x_ref, gt_ref, a_ref, h_ref, f_ref, gate_ref, wa_ref, wh_ref, wf_ref, wo_ref, o_ref):
    d = D_MODEL
    m = gate_ref[:, 0:d].astype(F32) * _dot(a_ref[...], wa_ref[...])
    m = m + gate_ref[:, d:2 * d].astype(F32) * _dot(h_ref[...].astype(BF), wh_ref[...])
    m = m + gate_ref[:, 2 * d:3 * d].astype(F32) * _dot(f_ref[...].astype(BF), wf_ref[...])
    y = _dot(m.astype(BF), wo_ref[...])
    o_ref[...] = x_ref[...] + gt_ref[0] * y


def _merge(x2, gt, attn_o, hy_o, fn_o, gates, wa, wh, wf, wo, *, tm, mod_row):
    t, d = x2.shape
    tok = lambda i: (i, 0)
    full = lambda i: (0, 0)
    return pl.pallas_call(
        _merge_kernel,
        grid=(t // tm,),
        in_specs=[
            pl.BlockSpec((tm, d), tok),
            pl.BlockSpec((1, 1, d), lambda i: (mod_row(i), 0, 0)),
            pl.BlockSpec((tm, ATTN_WIDTH), tok),
            pl.BlockSpec((tm, HYENA_WIDTH), tok),
            pl.BlockSpec((tm, FNET_WIDTH), tok),
            pl.BlockSpec((tm, 3 * d), tok),
            pl.BlockSpec(wa.shape, full, pipeline_mode=pl.Buffered(1)),
            pl.BlockSpec(wh.shape, full, pipeline_mode=pl.Buffered(1)),
            pl.BlockSpec(wf.shape, full, pipeline_mode=pl.Buffered(1)),
            pl.BlockSpec(wo.shape, full, pipeline_mode=pl.Buffered(1)),
        ],
        out_specs=pl.BlockSpec((tm, d), tok),
        out_shape=jax.ShapeDtypeStruct((t, d), F32),
        compiler_params=_cparams("parallel"),
        name="merge",
    )(x2, gt, attn_o, hy_o, fn_o, gates, wa, wh, wf, wo)


def _ffn_kernel(x_ref, sh_ref, sc_ref, gt_ref, g_ref, wg_ref, wu_ref, wd_ref, o_ref, h_scr, acc_scr):
    f = pl.program_id(1)

    @pl.when(f == 0)
    def _():
        h_scr[...] = _modulated_norm(x_ref[...], g_ref[...], sc_ref[0], sh_ref[0]).astype(BF)
        acc_scr[...] = jnp.zeros_like(acc_scr)

    hb = h_scr[...]
    act = _silu(_dot(hb, wg_ref[...])) * _dot(hb, wu_ref[...])
    acc_scr[...] += _dot(act.astype(BF), wd_ref[...])

    @pl.when(f == pl.num_programs(1) - 1)
    def _():
        o_ref[...] = x_ref[...] + gt_ref[0] * acc_scr[...]


def _ffn(x2, sh, sc, gt, g, wg, wu, wd, *, tm, tf, mod_row):
    t, d = x2.shape
    ff = wg.shape[1]
    row3 = lambda i, f: (mod_row(i), 0, 0)
    wmode = dict(pipeline_mode=pl.Buffered(1)) if tf == ff else {}
    return pl.pallas_call(
        _ffn_kernel,
        grid=(t // tm, ff // tf),
        in_specs=[
            pl.BlockSpec((tm, d), lambda i, f: (i, 0)),
            pl.BlockSpec((1, 1, d), row3), pl.BlockSpec((1, 1, d), row3), pl.BlockSpec((1, 1, d), row3),
            pl.BlockSpec((1, d), lambda i, f: (0, 0)),
            pl.BlockSpec((d, tf), lambda i, f: (0, f), **wmode),
            pl.BlockSpec((d, tf), lambda i, f: (0, f), **wmode),
            pl.BlockSpec((tf, d), lambda i, f: (f, 0), **wmode),
        ],
        out_specs=pl.BlockSpec((tm, d), lambda i, f: (i, 0)),
        out_shape=jax.ShapeDtypeStruct((t, d), F32),
        scratch_shapes=[pltpu.VMEM((tm, d), BF), pltpu.VMEM((tm, d), F32)],
        compiler_params=_cparams("parallel", "arbitrary"),
        name="ffn_dense",
    )(x2, sh, sc, gt, g, wg, wu, wd)


def _top2(logits):
    lane = lax.broadcasted_iota(jnp.int32, logits.shape, 1)
    lg = jnp.where(lane < N_EXPERTS, logits, -jnp.inf)
    m1 = jnp.max(lg, axis=1, keepdims=True)
    i1 = jnp.min(jnp.where(lg == m1, lane, LANES), axis=1, keepdims=True)
    lg2 = jnp.where(lane == i1, -jnp.inf, lg)
    m2 = jnp.max(lg2, axis=1, keepdims=True)
    i2 = jnp.min(jnp.where(lg2 == m2, lane, LANES), axis=1, keepdims=True)
    e2 = jnp.exp(m2 - m1)
    w1 = 1.0 / (1.0 + e2)
    return i1, i2, w1, e2 * w1


GROUP_TILE = 256
GROUP_PAD = 64
GROUP_PIECES = GROUP_TILE // GROUP_PAD
SLOT_RADIX = 64.0


def _moe_group_kernel(x_ref, sh_ref, sc_ref, g_ref, wr_ref, xg_ref, ws_ref, slot_ref, cnt_ref,
                      h_scr, rows_scr, wm_scr):
    j = pl.program_id(1)
    tb = x_ref.shape[0]
    gt_rows = xg_ref.shape[1]

    @pl.when(j == 0)
    def _():
        h = _modulated_norm(x_ref[...], g_ref[...], sc_ref[0], sh_ref[0])
        h_scr[...] = h.astype(BF)
        i1, i2, w1, w2 = _top2(_dot3(h, wr_ref[...]))
        lane = lax.broadcasted_iota(jnp.int32, (tb, LANES), 1)
        oh0 = jnp.where(lane == i1, 1.0, 0.0)
        oh1 = jnp.where(lane == i2, 1.0, 0.0)
        c0 = jnp.sum(oh0, axis=0, keepdims=True)
        cnt = c0 + jnp.sum(oh1, axis=0, keepdims=True)
        tri = jnp.where(lax.broadcasted_iota(jnp.int32, (tb, tb), 1)
                        < lax.broadcasted_iota(jnp.int32, (tb, tb), 0), 1.0, 0.0).astype(BF)
        pre0 = _dot(tri, oh0.astype(BF))
        pre1 = _dot(tri, oh1.astype(BF)) + c0
        tiles = jnp.ceil(cnt * (1.0 / GROUP_PAD))
        upper = jnp.where(lax.broadcasted_iota(jnp.int32, (LANES, LANES), 0)
                          < lax.broadcasted_iota(jnp.int32, (LANES, LANES), 1), 1.0, 0.0).astype(BF)
        off = _dot(jnp.broadcast_to(tiles, (8, LANES)).astype(BF), upper)[0:1] * float(GROUP_PAD)
        slot0 = jnp.sum(oh0 * (off + pre0), axis=1, keepdims=True)
        slot1 = jnp.sum(oh1 * (off + pre1), axis=1, keepdims=True)
        slot_ref[0] = jnp.where(lane == 0, slot0, jnp.where(lane == 1, slot1, 0.0))
        cnt_ref[0] = jnp.broadcast_to(cnt, (8, LANES))
        hi0 = jnp.floor(slot0 * (1.0 / SLOT_RADIX))
        hi1 = jnp.floor(slot1 * (1.0 / SLOT_RADIX))
        digits = jnp.where(lane == 0, hi0, jnp.where(lane == 1, slot0 - SLOT_RADIX * hi0,
                           jnp.where(lane == 2, hi1, jnp.where(lane == 3, slot1 - SLOT_RADIX * hi1, 0.0))))
        sel = jnp.where(lax.broadcasted_iota(jnp.int32, (8, LANES), 0)
                        == lax.broadcasted_iota(jnp.int32, (8, LANES), 1), 1.0, 0.0).astype(BF)
        rows_scr[...] = _dot_nt(sel, digits.astype(BF))
        w1h, w1l = _split(w1)
        w1m, w1l = _split(w1 - w1h.astype(F32))
        w2h, w2l = _split(w2)
        w2m, w2l = _split(w2 - w2h.astype(F32))
        cols = [w1h, w1m, w1l, w2h, w2m, w2l]
        wm = jnp.zeros((tb, LANES), F32)
        for li, col in enumerate(cols):
            wm = jnp.where(lane == li, col.astype(F32), wm)
        wm_scr[...] = wm.astype(BF)

    rows = rows_scr[...]
    s0 = rows[0:1] * SLOT_RADIX + rows[1:2]
    s1 = rows[2:3] * SLOT_RADIX + rows[3:4]
    pos = (lax.broadcasted_iota(jnp.int32, (gt_rows, tb), 0) + j * gt_rows).astype(F32)
    g0 = jnp.where(pos == s0, 1.0, 0.0).astype(BF)
    g1 = jnp.where(pos == s1, 1.0, 0.0).astype(BF)
    xg_ref[0] = _dot(g0 + g1, h_scr[...]).astype(BF)
    lane_w = lax.broadcasted_iota(jnp.int32, (gt_rows, LANES), 1)
    wsum = (jnp.where(lane_w < 3, _dot(g0, wm_scr[...]), 0.0)
            + jnp.where((lane_w >= 3) & (lane_w < 6), _dot(g1, wm_scr[...]), 0.0))
    ws_ref[0] = jnp.broadcast_to(jnp.sum(wsum, axis=1, keepdims=True), (gt_rows, LANES))


def _moe_group(x2, sh, sc, g, wr_pad, *, tb, nt, mod_row):
    t, d = x2.shape
    nb = t // tb
    row3 = lambda b, j: (mod_row(b), 0, 0)
    return pl.pallas_call(
        _moe_group_kernel,
        grid=(nb, nt),
        in_specs=[
            pl.BlockSpec((tb, d), lambda b, j: (b, 0)),
            pl.BlockSpec((1, 1, d), row3), pl.BlockSpec((1, 1, d), row3),
            pl.BlockSpec((1, d), lambda b, j: (0, 0)),
            pl.BlockSpec((d, LANES), lambda b, j: (0, 0)),
        ],
        out_specs=[
            pl.BlockSpec((1, GROUP_TILE, d), lambda b, j: (b * nt + j, 0, 0)),
            pl.BlockSpec((1, GROUP_TILE, LANES), lambda b, j: (b * nt + j, 0, 0)),
            pl.BlockSpec((1, tb, LANES), lambda b, j: (b, 0, 0)),
            pl.BlockSpec((1, 8, LANES), lambda b, j: (b, 0, 0)),
        ],
        out_shape=[
            jax.ShapeDtypeStruct((nb * nt, GROUP_TILE, d), BF),
            jax.ShapeDtypeStruct((nb * nt, GROUP_TILE, LANES), F32),
            jax.ShapeDtypeStruct((nb, tb, LANES), F32),
            jax.ShapeDtypeStruct((nb, 8, LANES), F32),
        ],
        scratch_shapes=[pltpu.VMEM((tb, d), BF), pltpu.VMEM((8, tb), F32), pltpu.VMEM((tb, LANES), BF)],
        compiler_params=_cparams("parallel", "arbitrary"),
        name="moe_group",
    )(x2, sh, sc, g, wr_pad)


def _moe_schedule(cnt, nh):
    np_ = GROUP_PIECES
    h = (cnt + GROUP_PAD - 1) // GROUP_PAD
    nb, ne = h.shape
    tot = h.sum(0)
    pairs = (tot + np_ - 1) // np_
    cum_p = jnp.cumsum(pairs)
    start_p = cum_p - pairs
    n_used = cum_p[-1]
    n_steps = (nb * nh + ne * (np_ - 1)) // np_
    q = jnp.minimum(jnp.arange(n_steps, dtype=jnp.int32), n_used - 1)
    e = jnp.sum(q[:, None] >= cum_p[None, :], axis=1).astype(jnp.int32)
    r = q - start_p[e]
    cum_b = jnp.cumsum(h, axis=0)
    first = jnp.cumsum(h, axis=1) - h

    def piece(idx):
        idx = jnp.minimum(idx, tot[e] - 1)
        blk = jnp.sum(idx[:, None] >= cum_b.T[e], axis=1).astype(jnp.int32)
        return blk * nh + first[blk, e] + idx - (cum_b[blk, e] - h[blk, e])

    x = jnp.arange(nh, dtype=jnp.int32)[None, :]
    ex = jnp.sum(x[:, :, None] >= jnp.cumsum(h, axis=1)[:, None, :], axis=2).astype(jnp.int32)
    exc = jnp.minimum(ex, ne - 1)
    g = jnp.take_along_axis(cum_b - h, exc, axis=1) + x - jnp.take_along_axis(first, exc, axis=1)
    loc = np_ * (start_p[exc] + g // np_) + g % np_
    loc = jnp.where(ex < ne, loc, loc[:, 0:1])
    pieces = jnp.concatenate([piece(np_ * r + k) for k in range(np_)]).astype(jnp.int32)
    return (pieces, e, n_used.astype(jnp.int32).reshape(1), loc.reshape(-1).astype(jnp.int32),
            ((h.sum(1) + np_ - 1) // np_).astype(jnp.int32))


def _moe_expert_kernel(pc_ref, exp_ref, nused_ref, *refs):
    np_ = GROUP_PIECES
    x_refs, w_refs = refs[:np_], refs[np_:2 * np_]
    wg_ref, wu_ref, wd_ref, y_ref = refs[2 * np_:]

    @pl.when(pl.program_id(0) < nused_ref[0])
    def _():
        x = jnp.concatenate([r[0] for r in x_refs], axis=0)
        act = _silu(_dot(x, wg_ref[0])) * _dot(x, wu_ref[0])
        y = _dot(act.astype(BF), wd_ref[0])
        w = jnp.concatenate([r[0] for r in w_refs], axis=0)
        y_ref[0] = (y * jnp.concatenate([w] * (y.shape[1] // LANES), axis=1)).astype(BF)


def _moe_experts(pieces, step_exp, n_used, xg, ws, wg, wu, wd):
    d = xg.shape[-1]
    ff = wg.shape[2]
    np_ = GROUP_PIECES
    n_steps = step_exp.shape[0]
    xh = xg.reshape(-1, GROUP_PAD, d)
    wh = ws.reshape(-1, GROUP_PAD, LANES)
    pc3 = lambda k: (lambda i, pc, se, nu: (pc[k * n_steps + i], 0, 0))
    exp3 = lambda i, pc, se, nu: (se[i], 0, 0)
    return pl.pallas_call(
        _moe_expert_kernel,
        grid_spec=pltpu.PrefetchScalarGridSpec(
            num_scalar_prefetch=3,
            grid=(n_steps,),
            in_specs=(
                [pl.BlockSpec((1, GROUP_PAD, d), pc3(k)) for k in range(np_)]
                + [pl.BlockSpec((1, GROUP_PAD, LANES), pc3(k)) for k in range(np_)]
                + [pl.BlockSpec((1, d, ff), exp3), pl.BlockSpec((1, d, ff), exp3), pl.BlockSpec((1, ff, d), exp3)]
            ),
            out_specs=pl.BlockSpec((1, GROUP_TILE, d), lambda i, pc, se, nu: (jnp.minimum(i, nu[0] - 1), 0, 0)),
        ),
        out_shape=jax.ShapeDtypeStruct((n_steps, GROUP_TILE, d), BF),
        compiler_params=_cparams("arbitrary"),
        name="moe_experts",
    )(pieces, step_exp, n_used, *([xh] * np_), *([wh] * np_), wg, wu, wd)


def _moe_combine_kernel(nt_ref, loc_ref, x_ref, gt_ref, slot_ref, *refs):
    y_refs, o_ref, acc_scr = refs[:GROUP_PIECES], refs[GROUP_PIECES], refs[GROUP_PIECES + 1]
    b = pl.program_id(0)
    j = pl.program_id(1)
    tb = x_ref.shape[0]

    @pl.when(j == 0)
    def _():
        acc_scr[...] = jnp.zeros_like(acc_scr)

    @pl.when(j < nt_ref[b])
    def _():
        sl = slot_ref[0]
        pos = (lax.broadcasted_iota(jnp.int32, (tb, GROUP_TILE), 1) + j * GROUP_TILE).astype(F32)
        p = jnp.where((pos == sl[:, 0:1]) | (pos == sl[:, 1:2]), 1.0, 0.0).astype(BF)
        acc_scr[...] += _dot(p, jnp.concatenate([r[0] for r in y_refs], axis=0))

    @pl.when(j == pl.num_programs(1) - 1)
    def _():
        o_ref[...] = x_ref[...] + gt_ref[0] * acc_scr[...]


def _moe_combine(ntiles_b, loc, x2, gt, slots, yg, *, tb, nt, mod_row):
    t, d = x2.shape
    nb = t // tb
    np_ = GROUP_PIECES
    yh = yg.reshape(-1, GROUP_PAD, d)

    def piece3(k):
        return lambda b, j, n, lc: (lc[(b * nt + jnp.minimum(j, n[b] - 1)) * np_ + k], 0, 0)

    return pl.pallas_call(
        _moe_combine_kernel,
        grid_spec=pltpu.PrefetchScalarGridSpec(
            num_scalar_prefetch=2,
            grid=(nb, nt),
            in_specs=[
                pl.BlockSpec((tb, d), lambda b, j, n, lc: (b, 0)),
                pl.BlockSpec((1, 1, d), lambda b, j, n, lc: (mod_row(b), 0, 0)),
                pl.BlockSpec((1, tb, LANES), lambda b, j, n, lc: (b, 0, 0)),
            ] + [pl.BlockSpec((1, GROUP_PAD, d), piece3(k)) for k in range(np_)],
            out_specs=pl.BlockSpec((tb, d), lambda b, j, n, lc: (b, 0)),
            scratch_shapes=[pltpu.VMEM((tb, d), F32)],
        ),
        out_shape=jax.ShapeDtypeStruct((t, d), F32),
        compiler_params=_cparams("parallel", "arbitrary"),
        name="moe_combine",
    )(ntiles_b, loc, x2, gt, slots, *([yh] * np_))


def _moe(x2, sh, sc, gt, g, wr_pad, wg, wu, wd, *, tb, mod_row):
    nt = -(-(2 * tb + N_EXPERTS * (GROUP_PAD - 1)) // GROUP_TILE)
    xg, ws, slots, cnt = _moe_group(x2, sh, sc, g, wr_pad, tb=tb, nt=nt, mod_row=mod_row)
    counts = cnt[:, 0, :N_EXPERTS].astype(jnp.int32)
    pieces, step_exp, n_used, loc, ntiles_b = _moe_schedule(counts, nt * GROUP_PIECES)
    yg = _moe_experts(pieces, step_exp, n_used, xg, ws, wg, wu, wd)
    return _moe_combine(ntiles_b, loc, x2, gt, slots, yg, tb=tb, nt=nt, mod_row=mod_row)


def _rope_tables(seq_len):
    pos = np.arange(seq_len)
    prow = (pos // GRID_W).astype(np.float32)
    pcol = (pos % GRID_W).astype(np.float32)
    n_freq = HEAD_DIM // 4
    inv = (np.float32(ROPE_THETA) ** (-np.arange(n_freq, dtype=np.float32) / n_freq)).astype(np.float32)
    ar = (prow[:, None] * inv[None, :]).astype(np.float64)
    ac = (pcol[:, None] * inv[None, :]).astype(np.float64)
    cos = np.concatenate([np.cos(ar)] * 2 + [np.cos(ac)] * 2, axis=1)
    sin = np.concatenate([-np.sin(ar), np.sin(ar), -np.sin(ac), np.sin(ac)], axis=1)
    return (jnp.asarray(np.concatenate([cos, cos], axis=1), dtype=F32),
            jnp.asarray(np.concatenate([sin, sin], axis=1), dtype=F32))


def _head_sum_matrix():
    c = np.arange(ATTN_WIDTH)
    return jnp.asarray((c[:, None] // HEAD_DIM) == (c[None, :] // HEAD_DIM), dtype=F32).astype(BF)


def _fnet_channel_matrix():
    c = np.arange(FNET_WIDTH)
    same = (c[:, None] // FNET_GROUP_DIM) == (c[None, :] // FNET_GROUP_DIM)
    ang = 2.0 * np.pi * (((c[:, None] % FNET_GROUP_DIM) * (c[None, :] % FNET_GROUP_DIM)) % FNET_GROUP_DIM) / FNET_GROUP_DIM
    cb = np.where(same, np.cos(ang), 0.0)
    sb = np.where(same, np.sin(ang), 0.0)
    return jnp.asarray(np.concatenate([cb, -sb], axis=1), dtype=F32).astype(BF)


def _filter_features(n):
    t = np.linspace(0.0, 1.0, n)[:, None]
    w = 2.0 * np.pi * np.arange(n)[:, None] / n
    fb = np.linspace(1e-4, FILTER_BANDS - 1, FILTER_BANDS)
    z = np.concatenate([t, np.cos(fb * w), -np.sin(fb * w)], axis=-1)
    z = np.pad(z, ((0, 0), (0, 64 - z.shape[1])))
    return jnp.asarray(np.concatenate([z, np.zeros((1, z.shape[1])), z[:0:-1]], axis=0), dtype=F32)


def _decay_rates():
    d = jnp.abs(jnp.linspace(math.log(DECAY_TARGET) / SLOW_DECAY_PCT, math.log(DECAY_TARGET) / FAST_DECAY_PCT,
                             HYENA_WIDTH, dtype=F32))
    return jnp.concatenate([d] * HYENA_ORDER)[None, :]


def kernel(x, c, ctx, c_ctx, w_ada, b_ada, norm1_g, norm2_g, w_in, q_norm_g, k_norm_g, attn_sink,
           hy_conv_w, hy_conv_b, hy_filt_w1, hy_filt_b1, hy_filt_freq1, hy_filt_w2, hy_filt_b2,
           hy_filt_freq2, hy_filt_w3, hy_bias, w_proj_attn, w_proj_hyena, w_proj_fnet, w_out,
           ffn_w_gate, ffn_w_up, ffn_w_down, moe_router, moe_w_gate, moe_w_up, moe_w_down):
    b, seq, d = x.shape
    n_ctx = ctx.shape[1]
    depth = w_ada.shape[0]
    tm = 512
    tiles_per_seq = seq // tm

    cond8 = jnp.concatenate([c, c_ctx[None, :], jnp.zeros((8 - b - 1, d), F32)], axis=0)
    mods = _adaln(cond8, w_ada, b_ada)

    cos_l, sin_l = _rope_tables(seq)
    cos_c = jnp.ones((n_ctx, LANES), F32)
    sin_c = jnp.zeros((n_ctx, LANES), F32)
    gsum = _head_sum_matrix()
    mfn = _fnet_channel_matrix()
    deltas = _decay_rates()
    zfeat_l = _filter_features(seq)
    zfeat_c = _filter_features(n_ctx)

    lat_row = lambda i: i // tiles_per_seq
    ctx_row = lambda i: b
    lat_tab = lambda i: i % tiles_per_seq
    ctx_tab = lambda i: 0
    tm_c = min(tm, n_ctx)

    xs = x.reshape(b * seq, d)
    cs = ctx.reshape(b * n_ctx, d)
    for l in range(depth):
        last = l == depth - 1
        mod = lambda j: mods[l, :, j * d:(j + 1) * d].reshape(8, 1, d)
        w_in_bf = w_in[l].astype(BF)
        qg = jnp.tile(q_norm_g[l], N_HEADS)[None, :]
        kg = jnp.tile(k_norm_g[l], N_KV_HEADS)[None, :]
        g1 = norm1_g[l][None, :]
        wa, wh, wf, wo = (w_proj_attn[l].astype(BF), w_proj_hyena[l].astype(BF),
                          w_proj_fnet[l].astype(BF), w_out[l].astype(BF))
        conv_w = hy_conv_w[l].reshape(3, -1)
        conv_b = hy_conv_b[l][None, :]
        w1p = jnp.pad(hy_filt_w1[l], ((0, 64 - hy_filt_w1.shape[1]), (0, 0)))
        filt = (w1p, hy_filt_b1[l][None, :], hy_filt_freq1[l][None, :], hy_filt_w2[l],
                hy_filt_b2[l][None, :], hy_filt_freq2[l][None, :], hy_filt_w3[l], deltas)

        qkv_c, upq_c, gates_c = _phase_a(
            cs, mod(0), mod(1), g1, w_in_bf, cos_c, sin_c, qg, kg, gsum, mfn, conv_w, conv_b,
            tm=tm_c, tiles_per_seq=n_ctx // tm_c, mod_row=ctx_row, tab_row=ctx_tab)
        qkv_c = qkv_c.reshape(b, n_ctx, -1)
        upq_c = upq_c.reshape(b, n_ctx, -1)

        qkv_l, upq_l, gates_l = _phase_a(
            xs, mod(0), mod(1), g1, w_in_bf, cos_l, sin_l, qg, kg, gsum, mfn, conv_w, conv_b,
            tm=tm, tiles_per_seq=tiles_per_seq, mod_row=lat_row, tab_row=lat_tab)
        upq_l = upq_l.reshape(b, seq, -1)
        attn_l = _attention(attn_sink[l], qkv_l.reshape(b, seq, -1), qkv_c, local=True, tq=512)
        h_l, nrm_l = _hy_filter(zfeat_l, *filt, tm=512)
        hy_l = _hyena(upq_l, h_l, nrm_l, hy_bias[l])
        fn_l = _fnet(upq_l)
        xs = _merge(xs, mod(2), attn_l.reshape(b * seq, -1), hy_l, fn_l, gates_l, wa, wh, wf, wo,
                    tm=tm, mod_row=lat_row)

        if not last:
            attn_c = _attention(attn_sink[l], qkv_c, qkv_c, local=False, tq=n_ctx)
            h_c, nrm_c = _hy_filter(zfeat_c, *filt, tm=n_ctx)
            hy_c = _hyena(upq_c, h_c, nrm_c, hy_bias[l])
            fn_c = _fnet(upq_c)
            cs = _merge(cs, mod(2), attn_c.reshape(b * n_ctx, -1), hy_c, fn_c, gates_c, wa, wh, wf, wo,
                        tm=tm_c, mod_row=ctx_row)

        g2 = norm2_g[l][None, :]
        i = l // 2
        if l % 2 == 0:
            wg, wu, wd = ffn_w_gate[i].astype(BF), ffn_w_up[i].astype(BF), ffn_w_down[i].astype(BF)
            run = lambda t2, rows, tmm: _ffn(t2, mod(3), mod(4), mod(5), g2, wg, wu, wd,
                                             tm=tmm, tf=D_FF, mod_row=rows)
        else:
            wr = jnp.pad(moe_router[i], ((0, 0), (0, LANES - N_EXPERTS)))
            wg, wu, wd = moe_w_gate[i].astype(BF), moe_w_up[i].astype(BF), moe_w_down[i].astype(BF)
            run = lambda t2, rows, tmm: _moe(t2, mod(3), mod(4), mod(5), g2, wr, wg, wu, wd,
                                             tb=tmm, mod_row=rows)
        tm_ffn = 512 if l % 2 == 0 else 1024
        xs = run(xs, lambda t: t // (seq // tm_ffn), tm_ffn)
        if not last:
            cs = run(cs, ctx_row, min(tm_ffn, b * n_ctx))
    return xs.reshape(b, seq, d)
```

```python
import functools
import math

import numpy as np
import jax
import jax.numpy as jnp
from jax import lax
from jax.experimental import pallas as pl
from jax.experimental.pallas import tpu as pltpu

F32 = jnp.float32
BF = jnp.bfloat16

D_MODEL = 1024
DEPTH = 4
GRID_W = 64
HEAD_DIM = 64
N_HEADS = 8
N_KV_HEADS = 2
ATTN_WIDTH = N_HEADS * HEAD_DIM
KV_WIDTH = N_KV_HEADS * HEAD_DIM
WINDOW = 128
QBLK = 128
ROPE_THETA = 10000.0
HYENA_ORDER = 2
HYENA_WIDTH = 256
FILTER_BANDS = 16
FILTER_HIDDEN = 64
DECAY_TARGET = 1e-2
FAST_DECAY_PCT = 0.3
SLOW_DECAY_PCT = 1.5
FNET_WIDTH = 256
FNET_GROUP_DIM = 64
Q_END = ATTN_WIDTH
K_END = Q_END + KV_WIDTH
V_END = K_END + KV_WIDTH
HY_END = V_END + (HYENA_ORDER + 1) * HYENA_WIDTH
FN_END = HY_END + FNET_WIDTH
IN_WIDTH = FN_END + 3 * D_MODEL
D_FF = 2816
N_EXPERTS = 8
EPS = 1e-6
LANES = 128
NEG = -1e30
STAGE_ROWS = 16
STAGE_JOIN = 2

VMEM_LIMIT = 56 * 1024 * 1024


def _cparams(*sem):
    return pltpu.CompilerParams(dimension_semantics=sem, vmem_limit_bytes=VMEM_LIMIT)


def _dot(a, b):
    return jnp.dot(a, b, preferred_element_type=F32)


def _dot_nt(a, b):
    return lax.dot_general(a, b, (((1,), (1,)), ((), ())), preferred_element_type=F32)


def _split(a):
    hi = a.astype(BF)
    lo = (a - hi.astype(F32)).astype(BF)
    return hi, lo


def _dot3(a, b):
    ah, al = _split(a)
    bh, bl = _split(b)
    return _dot(ah, bh) + (_dot(ah, bl) + _dot(al, bh))


def _dot2(a, b_bf16):
    ah, al = _split(a)
    return _dot(ah, b_bf16) + _dot(al, b_bf16)


def _sigmoid(v):
    return 0.5 * jnp.tanh(0.5 * v) + 0.5


def _silu(v):
    return v * _sigmoid(v)


def _adaln_kernel(c_ref, w_ref, b_ref, o_ref):
    o_ref[0] = _dot3(_silu(c_ref[...]), w_ref[0]) + b_ref[0]


def _adaln(cond8, w_ada, b_ada):
    depth, d, n6 = w_ada.shape
    tn = 1024
    return pl.pallas_call(
        _adaln_kernel,
        grid=(depth, n6 // tn),
        in_specs=[
            pl.BlockSpec((8, d), lambda l, j: (0, 0)),
            pl.BlockSpec((1, d, tn), lambda l, j: (l, 0, j)),
            pl.BlockSpec((1, 1, tn), lambda l, j: (l, 0, j)),
        ],
        out_specs=pl.BlockSpec((1, 8, tn), lambda l, j: (l, 0, j)),
        out_shape=jax.ShapeDtypeStruct((depth, 8, n6), F32),
        compiler_params=_cparams("parallel", "parallel"),
        name="adaln",
    )(cond8, w_ada, b_ada.reshape(depth, 1, n6))


def _modulated_norm(x, g, sc, sh):
    ms = jnp.mean(x * x, axis=-1, keepdims=True)
    h = (x * lax.rsqrt(ms + EPS)) * g
    return h * (1.0 + sc) + sh


HALO = 16


def _phase_a_kernel(x_ref, xp_ref, xn_ref, sh_ref, sc_ref, g_ref, w_ref, cos_ref, sin_ref, qg_ref, kg_ref,
                    gsum_ref, mfn_ref, cw_ref, cb_ref, qkv_ref, upq_ref, gate_ref, *, tiles_per_seq):
    tm = x_ref.shape[0]
    norm = lambda xv: _modulated_norm(xv, g_ref[...], sc_ref[0], sh_ref[0]).astype(BF)
    hb = norm(x_ref[...])
    cos = cos_ref[...]
    sin = sin_ref[...]

    def headnorm(t, gain, gs):
        ss = _dot2(t * t, gs)
        return t * lax.rsqrt(ss * (1.0 / HEAD_DIM) + EPS) * gain

    def rope(t, cosw, sinw):
        w = t.shape[1]
        nxt = pltpu.roll(t, w - 16, axis=1)
        prv = pltpu.roll(t, 16, axis=1)
        lw = lax.broadcasted_iota(jnp.int32, t.shape, 1)
        return t * cosw + jnp.where((lw % 32) < 16, nxt, prv) * sinw

    def dup_halves(t):
        lane = lax.broadcasted_iota(jnp.int32, t.shape, 1)
        sw = pltpu.roll(t, 64, axis=1)
        lo = lane < 64
        return jnp.concatenate([jnp.where(lo, t, sw), jnp.where(lo, sw, t)], axis=1)

    pr = _dot(hb, w_ref[:, 0:FN_END])
    pq = pr[:, 0:Q_END]
    qn = headnorm(pq, qg_ref[...], gsum_ref[...])
    cos4 = jnp.concatenate([cos] * 4, axis=1)
    sin4 = jnp.concatenate([sin] * 4, axis=1)
    qkv_ref[:, 0:Q_END] = (rope(qn, cos4, sin4) * (HEAD_DIM ** -0.5)).astype(BF)

    kn = headnorm(pr[:, Q_END:K_END], kg_ref[...], gsum_ref[0:KV_WIDTH, 0:KV_WIDTH])
    qkv_ref[:, Q_END:Q_END + 2 * KV_WIDTH] = dup_halves(rope(kn, cos, sin)).astype(BF)
    qkv_ref[:, Q_END + 2 * KV_WIDTH:] = dup_halves(pr[:, K_END:V_END]).astype(BF)

    w_hy = w_ref[:, V_END:HY_END]
    u = pr[:, V_END:HY_END]
    tile = pl.program_id(0) % tiles_per_seq
    u_before = jnp.where(tile == 0, 0.0, _dot(norm(xp_ref[...]), w_hy)[HALO - 1:HALO])
    u_after = jnp.where(tile == tiles_per_seq - 1, 0.0, _dot(norm(xn_ref[...]), w_hy)[0:1])
    row = lax.broadcasted_iota(jnp.int32, (tm, 1), 0)
    prv = jnp.where(row == 0, u_before, pltpu.roll(u, 1, axis=0))
    nxt = jnp.where(row == tm - 1, u_after, pltpu.roll(u, tm - 1, axis=0))
    n_hy = HY_END - V_END
    upq_ref[:, 0:n_hy] = prv * cw_ref[0:1, :] + u * cw_ref[1:2, :] + nxt * cw_ref[2:3, :] + cb_ref[...]

    upq_ref[:, n_hy:] = _dot(pr[:, HY_END:FN_END].astype(BF), mfn_ref[...])
    gate_ref[...] = _sigmoid(_dot(hb, w_ref[:, FN_END:])).astype(BF)


def _phase_a(x2, sh, sc, g, w_in_bf, cos_t, sin_t, qg, kg, gsum, mfn, conv_w, conv_b, *, tm, tiles_per_seq,
             mod_row, tab_row):
    t, d = x2.shape
    row3 = lambda i: (mod_row(i), 0, 0)
    full = lambda i: (0, 0)
    tok = lambda i: (i, 0)
    n_halo = t // HALO
    outs = [
        ((t, ATTN_WIDTH + 4 * KV_WIDTH), BF), ((t, 3 * HYENA_WIDTH + 2 * FNET_WIDTH), F32),
        ((t, 3 * D_MODEL), BF),
    ]
    kern = functools.partial(_phase_a_kernel, tiles_per_seq=tiles_per_seq)
    return pl.pallas_call(
        kern,
        grid=(t // tm,),
        in_specs=[
            pl.BlockSpec((tm, d), tok),
            pl.BlockSpec((HALO, d), lambda i: (jnp.maximum(i * (tm // HALO) - 1, 0), 0)),
            pl.BlockSpec((HALO, d), lambda i: (jnp.minimum((i + 1) * (tm // HALO), n_halo - 1), 0)),
            pl.BlockSpec((1, 1, d), row3),
            pl.BlockSpec((1, 1, d), row3),
            pl.BlockSpec((1, d), full),
            pl.BlockSpec((d, IN_WIDTH), full, pipeline_mode=pl.Buffered(1)),
            pl.BlockSpec((tm, LANES), lambda i: (tab_row(i), 0)),
            pl.BlockSpec((tm, LANES), lambda i: (tab_row(i), 0)),
            pl.BlockSpec((1, ATTN_WIDTH), full),
            pl.BlockSpec((1, KV_WIDTH), full),
            pl.BlockSpec((ATTN_WIDTH, ATTN_WIDTH), full),
            pl.BlockSpec((FNET_WIDTH, 2 * FNET_WIDTH), full),
            pl.BlockSpec(conv_w.shape, full),
            pl.BlockSpec(conv_b.shape, full),
        ],
        out_specs=[pl.BlockSpec((tm, s[1]), tok) for s, _ in outs],
        out_shape=[jax.ShapeDtypeStruct(s, dt) for s, dt in outs],
        compiler_params=_cparams("parallel"),
        name="phase_a",
    )(x2, x2, x2, sh, sc, g, w_in_bf, cos_t, sin_t, qg, kg, gsum, mfn, conv_w, conv_b)


def _attn_kernel(sink_ref, q_ref, kd_ref, vd_ref, kc_ref, vc_ref, o_ref, *, local, seq_len):
    tq = q_ref.shape[1]
    nblk = tq // QBLK
    gq = N_HEADS // N_KV_HEADS
    rows = gq * QBLK
    lane = lax.broadcasted_iota(jnp.int32, (QBLK, LANES), 1)
    lo_half = lane < 64
    hrow = lax.broadcasted_iota(jnp.int32, (rows, 1), 0) // QBLK
    nband = 3 * QBLK
    if local:
        qk_off = (lax.broadcasted_iota(jnp.int32, (rows, nband), 0) % QBLK
                  - lax.broadcasted_iota(jnp.int32, (rows, nband), 1))
    for blk in range(nblk):
        r0 = blk * QBLK
        qb = q_ref[0, r0:r0 + QBLK, :]
        if local:
            n = pl.program_id(1) * nblk + blk
            start = pl.multiple_of(jnp.clip((n - 1) * QBLK, 0, seq_len - nband), QBLK)
            valid = jnp.abs(qk_off + (n * QBLK - start)) <= WINDOW
        for g in range(N_KV_HEADS):
            parts = []
            for hh in range(gq):
                h = gq * g + hh
                qc = qb[:, (h // 2) * LANES:(h // 2 + 1) * LANES]
                keep = lo_half if h % 2 == 0 else jnp.logical_not(lo_half)
                parts.append(jnp.where(keep, qc, jnp.zeros_like(qc)))
            q4 = jnp.concatenate(parts, axis=0)
            sk = jnp.full((rows, 1), sink_ref[gq * g + gq - 1], F32)
            for hh in range(gq - 2, -1, -1):
                sk = jnp.where(hrow == hh, sink_ref[gq * g + hh], sk)
            gl = slice(g * LANES, (g + 1) * LANES)
            s_ctx = _dot_nt(q4, kc_ref[0, :, gl])
            m = jnp.maximum(jnp.max(s_ctx, axis=1, keepdims=True), sk)
            if local:
                s_loc = _dot_nt(q4, kd_ref[0, pl.ds(start, nband), gl])
                s_loc = jnp.where(valid, s_loc, NEG)
                m = jnp.maximum(m, jnp.max(s_loc, axis=1, keepdims=True))
            p_ctx = jnp.exp(s_ctx - m)
            den = jnp.sum(p_ctx, axis=1, keepdims=True) + jnp.exp(sk - m)
            o = _dot(p_ctx.astype(BF), vc_ref[0, :, gl])
            if local:
                p_loc = jnp.exp(s_loc - m)
                den = den + jnp.sum(p_loc, axis=1, keepdims=True)
                o = o + _dot(p_loc.astype(BF), vd_ref[0, pl.ds(start, nband), gl])
            o = o / den
            for cc in range(gq // 2):
                col = (gq // 2) * g + cc
                oa = o[(2 * cc) * QBLK:(2 * cc + 1) * QBLK]
                ob = o[(2 * cc + 1) * QBLK:(2 * cc + 2) * QBLK]
                o_ref[0, r0:r0 + QBLK, col * LANES:(col + 1) * LANES] = (
                    jnp.where(lo_half, oa, ob).astype(BF))


def _attention(sink, qkv, qkv_ctx, *, local, tq):
    b, lq, _ = qkv.shape
    c = qkv_ctx.shape[1]
    kw = 2 * KV_WIDTH
    k_blk, v_blk = ATTN_WIDTH // kw, ATTN_WIDTH // kw + 1
    kern = functools.partial(_attn_kernel, local=local, seq_len=lq)
    return pl.pallas_call(
        kern,
        grid=(b, lq // tq),
        in_specs=[
            pl.BlockSpec(memory_space=pltpu.SMEM),
            pl.BlockSpec((1, tq, ATTN_WIDTH), lambda bi, i: (bi, i, 0)),
            pl.BlockSpec((1, lq, kw), lambda bi, i: (bi, 0, k_blk)),
            pl.BlockSpec((1, lq, kw), lambda bi, i: (bi, 0, v_blk)),
            pl.BlockSpec((1, c, kw), lambda bi, i: (bi, 0, k_blk)),
            pl.BlockSpec((1, c, kw), lambda bi, i: (bi, 0, v_blk)),
        ],
        out_specs=pl.BlockSpec((1, tq, ATTN_WIDTH), lambda bi, i: (bi, i, 0)),
        out_shape=jax.ShapeDtypeStruct((b, lq, ATTN_WIDTH), BF),
        compiler_params=_cparams("parallel", "parallel"),
        name="attn_local" if local else "attn_ctx",
    )(sink, qkv, qkv, qkv, qkv_ctx, qkv_ctx)


FILTER_HALO = 128


def _hy_filter_kernel(z_ref, zn_ref, w1_ref, b1_ref, f1_ref, w2_ref, b2_ref, f2_ref, w3f_ref, w3b_ref, dl_ref,
                      kf_ref, kb_ref, nrm_ref):
    i = pl.program_id(0)
    tm = z_ref.shape[0]
    rows = tm + FILTER_HALO
    z = jnp.concatenate([z_ref[...], zn_ref[...]], axis=0)
    h = jnp.sin(f1_ref[...] * (_dot3(z, w1_ref[...]) + b1_ref[...]))
    h = jnp.sin(f2_ref[...] * (_dot3(h, w2_ref[...]) + b2_ref[...]))
    tcol = jnp.where(lax.broadcasted_iota(jnp.int32, (rows, LANES - FILTER_HIDDEN), 1) == 0, z[:, 0:1], 0.0)
    hid = jnp.concatenate([h, tcol], axis=1)

    def taps(hv, w3_ref):
        return _dot3(hv, w3_ref[...]) * jnp.exp(-hv[:, FILTER_HIDDEN:FILTER_HIDDEN + 1] * dl_ref[...])

    kf = taps(hid[0:tm], w3f_ref)
    flip = jnp.where(lax.broadcasted_iota(jnp.int32, (tm, rows), 1)
                     == tm - lax.broadcasted_iota(jnp.int32, (tm, rows), 0), 1.0, 0.0).astype(BF)
    h1 = hid.astype(BF)
    r1 = hid - h1.astype(F32)
    h2 = r1.astype(BF)
    h3 = (r1 - h2.astype(F32)).astype(BF)
    hid_rev = _dot(flip, h1) + (_dot(flip, h2) + _dot(flip, h3))
    kb = taps(hid_rev, w3b_ref)
    out_row = (pl.num_programs(0) - 1 - i) * tm + lax.broadcasted_iota(jnp.int32, (tm, 1), 0)
    kb = jnp.where(out_row == 0, 0.0, kb)
    kf_ref[...] = kf
    kb_ref[...] = kb

    @pl.when(i == 0)
    def _():
        nrm_ref[...] = jnp.zeros_like(nrm_ref)

    nrm_ref[...] += jnp.sum(jnp.abs(kf), axis=0, keepdims=True) + jnp.sum(jnp.abs(kb), axis=0, keepdims=True)


def _hy_filter(zfeat, w1p, b1, f1, w2, b2, f2, w3, deltas2, *, tm):
    n = zfeat.shape[0]
    nb = n // tm
    wout = w3.shape[1] // 2
    w3p = jnp.pad(w3, ((0, LANES - w3.shape[0]), (0, 0)))
    full = lambda i: (0, 0)
    halo_blk = lambda i: (jnp.minimum((i + 1) * (tm // FILTER_HALO), n // FILTER_HALO - 1), 0)
    return pl.pallas_call(
        _hy_filter_kernel,
        grid=(nb,),
        in_specs=[
            pl.BlockSpec((tm, zfeat.shape[1]), lambda i: (i, 0)),
            pl.BlockSpec((FILTER_HALO, zfeat.shape[1]), halo_blk),
            pl.BlockSpec(w1p.shape, full), pl.BlockSpec(b1.shape, full), pl.BlockSpec(f1.shape, full),
            pl.BlockSpec(w2.shape, full), pl.BlockSpec(b2.shape, full), pl.BlockSpec(f2.shape, full),
            pl.BlockSpec((LANES, wout), lambda i: (0, 0)),
            pl.BlockSpec((LANES, wout), lambda i: (0, 1)),
            pl.BlockSpec(deltas2.shape, full),
        ],
        out_specs=[pl.BlockSpec((tm, wout), lambda i: (i, 0)), pl.BlockSpec((tm, wout), lambda i: (nb - 1 - i, 0)),
                   pl.BlockSpec((1, wout), full)],
        out_shape=[jax.ShapeDtypeStruct((n, wout), F32), jax.ShapeDtypeStruct((n, wout), F32),
                   jax.ShapeDtypeStruct((1, wout), F32)],
        compiler_params=_cparams("arbitrary"),
        name="hy_filter",
    )(zfeat, zfeat, w1p, b1, f1, w2, b2, f2, w3p, w3p, deltas2)


def _dft_mats(k_out, r_in, period, sign, scale, n_in, real_out):
    k = np.arange(k_out)[:, None]
    r = np.arange(r_in)[None, :]
    ang = 2.0 * np.pi * ((k * r) % period) / period
    fr = np.cos(ang) * scale
    fi = sign * np.sin(ang) * scale
    if real_out:
        mats = [fr, -fi]
    else:
        mats = [np.concatenate([fr, fi], 0), np.concatenate([-fi, fr], 0)]
    return jnp.asarray(np.stack(mats[:n_in], 0), dtype=F32).astype(BF)


def _twiddle(s_n, k_n, n, sign, sbk):
    s0 = lax.broadcasted_iota(jnp.int32, (s_n // sbk, k_n, LANES), 0) * sbk
    k = lax.broadcasted_iota(jnp.int32, (s_n // sbk, k_n, LANES), 1)
    ang = (s0 * k).astype(F32) * (2.0 * math.pi / n)
    ang1 = lax.broadcasted_iota(jnp.int32, (k_n, LANES), 0).astype(F32) * (2.0 * math.pi / n)
    return jnp.cos(ang), sign * jnp.sin(ang), jnp.cos(ang1), sign * jnp.sin(ang1)


def _unpack_pair(p):
    return [pltpu.unpack_elementwise(p, index=i, packed_dtype=BF, unpacked_dtype=F32) for i in (0, 1)]


def _stage_kernel(*refs, n_in, r_in, k_mid, k_out, sbk, tw, spec, second, gate, real_out,
                  transposed_out, flat, packed_in, packed_spec, packed_out):
    it = iter(refs)
    x_refs = [next(it) for _ in range(n_in)]
    g_ref = next(it)
    g2_ref = next(it) if second else None
    tw_refs = [next(it) for _ in range(4)] if tw else None
    n_spec = 1 if packed_spec else 2
    spec_refs = [next(it) for _ in range(n_spec + 1)] if spec else None
    gate_refs = [next(it) for _ in range(5)] if gate else None
    out_refs = [next(it)] if (real_out or gate or packed_out) else [next(it), next(it)]
    if not flat:
        x_refs = [r.reshape(r_in * sbk, LANES) for r in x_refs]
        if spec:
            spec_refs = [r.reshape(k_mid * sbk, LANES) for r in spec_refs[:n_spec]] + spec_refs[n_spec:]
        if gate:
            gate_refs = [r.reshape(k_out * sbk, LANES) for r in gate_refs[:4]] + gate_refs[4:]
            out_refs = [out_refs[0].reshape(2 * k_out * sbk, LANES)]
        elif not transposed_out:
            out_refs = [r.reshape(k_out * sbk, LANES) for r in out_refs]
    if spec:
        inv = 1.0 / spec_refs[n_spec][...]
    if tw:
        tr, ti = tw_refs[0][0], tw_refs[1][0]
        wr, wi = tw_refs[2][...], tw_refs[3][...]
    def joined_dots(mats_ref, cols):
        acc = None
        for xi in range(len(cols[0])):
            wide = jnp.concatenate([c[xi].astype(BF) for c in cols], axis=1)
            d = _dot(mats_ref[xi], wide)
            acc = d if acc is None else acc + d
        return [acc[:, k * LANES:(k + 1) * LANES] for k in range(len(cols))]

    join = 1 if flat else min(STAGE_JOIN, sbk)
    results = {}
    for j in range(sbk):
        if j % join == 0:
            cols = []
            for jj in range(j, j + join):
                parts = [x_ref[...] if flat else x_ref[pl.ds(jj, r_in, stride=sbk), :] for x_ref in x_refs]
                cols.append(_unpack_pair(parts[0]) if packed_in else parts)
            accs = joined_dots(g_ref, cols)
            if not real_out:
                mids = []
                for jj, acc in zip(range(j, j + join), accs):
                    yr, yi = acc[:k_mid], acc[k_mid:]
                    if spec:
                        rows = slice(None) if flat else pl.ds(jj, k_mid, stride=sbk)
                        if packed_spec:
                            sr, si = _unpack_pair(spec_refs[0][rows, :])
                        else:
                            sr, si = spec_refs[0][rows, :], spec_refs[1][rows, :]
                        sr, si = sr * inv, si * inv
                        yr, yi = yr * sr - yi * si, yr * si + yi * sr
                    mids.append([yr, yi])
                if second:
                    mids = [[acc[:k_out], acc[k_out:]] for acc in joined_dots(g2_ref, mids)]
                accs = mids
            results = dict(zip(range(j, j + join), accs))
        if real_out:
            ys = [results[j]]
        else:
            yr, yi = results[j]
            if tw:
                yr, yi = yr * tr - yi * ti, yr * ti + yi * tr
                if j + 1 < sbk:
                    tr, ti = tr * wr - ti * wi, tr * wi + ti * wr
            ys = [yr, yi]
        if gate:
            o_ref = out_refs[0]
            for part, y in enumerate(ys):
                rows = slice(None) if flat else pl.ds(j, k_out, stride=sbk)
                val = gate_refs[part][rows, :] * (y + gate_refs[4][...] * gate_refs[2 + part][rows, :])
                if flat:
                    o_ref[part] = val.astype(o_ref.dtype)
                else:
                    o_ref[pl.ds(part * k_out * sbk + j, k_out, stride=sbk), :] = val.astype(o_ref.dtype)
            continue
        if packed_out:
            ys = [pltpu.pack_elementwise(ys, packed_dtype=BF)]
        for o_ref, y in zip(out_refs, ys):
            if flat:
                o_ref[...] = y.astype(o_ref.dtype)
            elif transposed_out:
                o_ref[0, j] = y.astype(o_ref.dtype)
            else:
                o_ref[pl.ds(j, k_out, stride=sbk), :] = y.astype(o_ref.dtype)


def _fft_stage(xs, x_sel, gmat, *, r_in, s_n, k_out, n_groups, n_cblk, transposed_out, real_out,
               out_dtype=F32, g2mat=None, tw=None, spec=None, spec_sel=None, gate=None, sbk=STAGE_ROWS,
               packed_in=False, packed_out=False, name="fft_stage"):
    n_in = len(xs)
    flat = s_n == 1
    sbk = 1 if flat else min(sbk, s_n)
    cb = LANES
    in_specs, args = [], []
    for x, sel in zip(xs, x_sel):
        if flat:
            in_specs.append(pl.BlockSpec((None, r_in, cb), lambda s, g, c, sel=sel: (sel(g, c)[0], 0, sel(g, c)[1])))
            args.append(x)
        else:
            xv = x.reshape(x.shape[0], x.shape[1] // s_n, s_n, x.shape[2])
            in_specs.append(pl.BlockSpec((1, r_in, sbk, cb),
                                         lambda s, g, c, sel=sel: (sel(g, c)[0], 0, s, sel(g, c)[1])))
            args.append(xv)
    in_specs.append(pl.BlockSpec(gmat.shape, lambda s, g, c: (0, 0, 0)))
    args.append(gmat)
    k_mid = gmat.shape[1] // (1 if real_out else 2)
    if g2mat is not None:
        in_specs.append(pl.BlockSpec(g2mat.shape, lambda s, g, c: (0, 0, 0)))
        args.append(g2mat)
    if tw is not None:
        for tarr in tw[:2]:
            in_specs.append(pl.BlockSpec((1, k_out, LANES), lambda s, g, c: (s, 0, 0)))
            args.append(tarr)
        for tarr in tw[2:]:
            in_specs.append(pl.BlockSpec((k_out, LANES), lambda s, g, c: (0, 0)))
            args.append(tarr)
    if spec is not None:
        *planes, nrm = spec
        for arr in planes:
            if flat:
                in_specs.append(pl.BlockSpec((k_mid, cb), lambda s, g, c: (0, spec_sel(g, c))))
                args.append(arr)
            else:
                in_specs.append(pl.BlockSpec((1, k_mid, sbk, cb), lambda s, g, c: (0, 0, s, spec_sel(g, c))))
                args.append(arr.reshape(1, k_mid, s_n, arr.shape[-1]))
        in_specs.append(pl.BlockSpec((1, cb), lambda s, g, c: (0, spec_sel(g, c))))
        args.append(nrm)
    if gate is not None:
        (ga, gblk), (za, zblk), bias = gate
        for arr, blk in ((ga, gblk), (za, zblk)):
            for bi in (0, 1):
                if flat:
                    in_specs.append(pl.BlockSpec((None, k_out, cb), lambda s, g, c, bi=bi, blk=blk: (bi, 0, blk + c)))
                    args.append(arr)
                else:
                    in_specs.append(pl.BlockSpec((1, k_out, sbk, cb),
                                                 lambda s, g, c, bi=bi, blk=blk: (bi, 0, s, blk + c)))
                    args.append(arr.reshape(arr.shape[0], k_out, s_n, arr.shape[-1]))
        in_specs.append(pl.BlockSpec((1, cb), lambda s, g, c: (0, c)))
        args.append(bias)
    ctot = n_cblk * cb
    if gate is not None:
        n_groups = 2
        if flat:
            oshape = (2, k_out, ctot)
            ospec = pl.BlockSpec((2, k_out, cb), lambda s, g, c: (0, 0, c))
        else:
            oshape = (2, k_out, s_n, ctot)
            ospec = pl.BlockSpec((2, k_out, sbk, cb), lambda s, g, c: (0, 0, s, c))
    elif flat:
        oshape = (n_groups, k_out, ctot)
        ospec = pl.BlockSpec((None, k_out, cb), lambda s, g, c: (g, 0, c))
    elif transposed_out:
        oshape = (n_groups, s_n, k_out, ctot)
        ospec = pl.BlockSpec((1, sbk, k_out, cb), lambda s, g, c: (g, s, 0, c))
    else:
        oshape = (n_groups, k_out, s_n, ctot)
        ospec = pl.BlockSpec((1, k_out, sbk, cb), lambda s, g, c: (g, 0, s, c))
    n_out = 1 if (real_out or gate is not None or packed_out) else 2
    if packed_out:
        out_dtype = jnp.int32
    kern = functools.partial(_stage_kernel, n_in=n_in, r_in=r_in, k_mid=k_mid, k_out=k_out, sbk=sbk,
                             tw=tw is not None, spec=spec is not None, second=g2mat is not None,
                             gate=gate is not None, real_out=real_out, transposed_out=transposed_out,
                             flat=flat, packed_in=packed_in, packed_spec=spec is not None and len(spec) == 2,
                             packed_out=packed_out)
    n_grid_groups = 1 if gate is not None else n_groups
    outs = pl.pallas_call(
        kern,
        grid=(s_n // sbk, n_grid_groups, n_cblk),
        in_specs=in_specs,
        out_specs=[ospec] * n_out,
        out_shape=[jax.ShapeDtypeStruct(oshape, out_dtype)] * n_out,
        compiler_params=_cparams("parallel", "parallel", "parallel"),
        name=name,
    )(*args)
    return [o.reshape(n_groups, -1, ctot) for o in outs]


def _split_len(n):
    if n <= 1024:
        return n, 1
    s = 128
    return n // s, s


def _fft_forward(xs, x_sel, n, n_rows, *, n_groups, n_cblk, halves=False, name="fwd"):
    n1, s = _split_len(n)
    n_in = len(xs)

    def mats(k_out, r_in, period):
        if not halves:
            return _dft_mats(k_out, r_in, period, -1.0, 1.0, n_in, False)
        g = _dft_mats(k_out, 2 * r_in, period, -1.0, 1.0, 1, False)[0]
        return jnp.stack([g[:, :r_in], g[:, r_in:]], axis=0)

    if s == 1:
        g = mats(n, n_rows, n)
        return _fft_stage(xs, x_sel, g, r_in=n_rows, s_n=1, k_out=n, n_groups=n_groups, n_cblk=n_cblk,
                          transposed_out=False, real_out=False, name=name + "_direct")
    r1 = n_rows // s
    g1 = mats(n1, r1, n1)
    tw = _twiddle(s, n1, n, -1.0, STAGE_ROWS)
    (a,) = _fft_stage(xs, x_sel, g1, r_in=r1, s_n=s, k_out=n1, n_groups=n_groups, n_cblk=n_cblk,
                      transposed_out=True, real_out=False, tw=tw, packed_out=True, name=name + "_s1")
    g2 = _dft_mats(s, s, s, -1.0, 1.0, 2, False)
    return _fft_stage([a], [lambda g, c: (g, c)], g2, r_in=s, s_n=n1, k_out=s, n_groups=n_groups,
                      n_cblk=n_cblk, transposed_out=False, real_out=False, packed_in=True, packed_out=True,
                      name=name + "_s2")


def _hyena(uc, k_halves, nrm, hy_bias):
    b, n, _ = uc.shape
    w = HYENA_WIDTH
    wblk = w // LANES
    nfft = 2 * n
    n1, s = _split_len(nfft)
    ident = lambda g, c: (g, c)
    k_spec = _fft_forward([k[None] for k in k_halves], [lambda g, c: (0, c)] * 2, nfft, n, n_groups=1,
                          n_cblk=HYENA_ORDER * wblk, halves=True, name="hy_filt_fft")
    z, zblk = uc, 2 * wblk
    for o in range(HYENA_ORDER):
        sel_r = lambda g, c, zblk=zblk: (0, zblk + c)
        sel_i = lambda g, c, zblk=zblk: (1, zblk + c)
        spec = tuple(p[0] for p in k_spec) + (nrm,)
        spec_sel = lambda g, c, o=o: o * wblk + c
        gate = ((uc, o * wblk), (z, zblk), hy_bias[o:o + 1])
        common = dict(n_groups=1, n_cblk=wblk, real_out=False)
        if s == 1:
            gf = _dft_mats(nfft, n, nfft, -1.0, 1.0, 2, False)
            gi = _dft_mats(n, nfft, nfft, 1.0, 1.0 / nfft, 2, False)
            (z,) = _fft_stage([z, z], [sel_r, sel_i], gf, r_in=n, s_n=1, k_out=n, transposed_out=False,
                              g2mat=gi, spec=spec, spec_sel=spec_sel, gate=gate, name="hy_direct", **common)
        else:
            r1 = n // s
            g1 = _dft_mats(n1, r1, n1, -1.0, 1.0, 2, False)
            (a,) = _fft_stage([z, z], [sel_r, sel_i], g1, r_in=r1, s_n=s, k_out=n1, transposed_out=True,
                              tw=_twiddle(s, n1, nfft, -1.0, STAGE_ROWS), packed_out=True, name="hy_s1",
                              **common)
            g2 = _dft_mats(s, s, s, -1.0, 1.0, 2, False)
            g3 = _dft_mats(s, s, s, 1.0, 1.0, 2, False)
            (q,) = _fft_stage([a], [ident], g2, r_in=s, s_n=n1, k_out=s, transposed_out=True,
                              g2mat=g3, tw=_twiddle(n1, s, nfft, 1.0, STAGE_ROWS), spec=spec,
                              spec_sel=spec_sel, packed_in=True, packed_out=True, name="hy_mid", **common)
            g4 = _dft_mats(n // s, n1, n1, 1.0, 1.0 / nfft, 2, False)
            (z,) = _fft_stage([q], [ident], g4, r_in=n1, s_n=s, k_out=n // s, transposed_out=False,
                              gate=gate, packed_in=True, name="hy_last", **common)
        zblk = 0
    return z.reshape(b * n, w)


def _fnet(pq):
    b, n, _ = pq.shape
    w = FNET_WIDTH
    wblk = w // LANES
    first = 3 * HYENA_WIDTH // LANES
    scale = 1.0 / math.sqrt(n * FNET_GROUP_DIM)
    sel_r = lambda g, c: (g, first + c)
    sel_i = lambda g, c: (g, first + wblk + c)
    ident = lambda g, c: (g, c)
    n1, s = _split_len(n)
    if s == 1:
        g = _dft_mats(n, n, n, -1.0, scale, 2, True)
        (y,) = _fft_stage([pq, pq], [sel_r, sel_i], g, r_in=n, s_n=1, k_out=n, n_groups=b, n_cblk=wblk,
                          transposed_out=False, real_out=True, name="fnet_direct")
        return y.reshape(b * n, w)
    g1 = _dft_mats(n1, n1, n1, -1.0, 1.0, 2, False)
    tw = _twiddle(s, n1, n, -1.0, STAGE_ROWS)
    (a,) = _fft_stage([pq, pq], [sel_r, sel_i], g1, r_in=n1, s_n=s, k_out=n1, n_groups=b, n_cblk=wblk,
                      transposed_out=True, real_out=False, tw=tw, packed_out=True, name="fnet_s1")
    g2 = _dft_mats(s, s, s, -1.0, scale, 2, True)
    (y,) = _fft_stage([a], [ident], g2, r_in=s, s_n=n1, k_out=s, n_groups=b, n_cblk=wblk,
                      transposed_out=False, real_out=True, packed_in=True, name="fnet_s2")
    return y.reshape(b * n, w)


def _merge_kernel(x_ref, gt_ref, a_ref, h_ref, f_ref, gate_ref, wa_ref, wh_ref, wf_ref, wo_ref, o_ref):
    d = D_MODEL
    m = gate_ref[:, 0:d].astype(F32) * _dot(a_ref[...], wa_ref[...])
    m = m + gate_ref[:, d:2 * d].astype(F32) * _dot(h_ref[...].astype(BF), wh_ref[...])
    m = m + gate_ref[:, 2 * d:3 * d].astype(F32) * _dot(f_ref[...].astype(BF), wf_ref[...])
    y = _dot(m.astype(BF), wo_ref[...])
    o_ref[...] = x_ref[...] + gt_ref[0] * y


def _merge(x2, gt, attn_o, hy_o, fn_o, gates, wa, wh, wf, wo, *, tm, mod_row):
    t, d = x2.shape
    tok = lambda i: (i, 0)
    full = lambda i: (0, 0)
    return pl.pallas_call(
        _merge_kernel,
        grid=(t // tm,),
        in_specs=[
            pl.BlockSpec((tm, d), tok),
            pl.BlockSpec((1, 1, d), lambda i: (mod_row(i), 0, 0)),
            pl.BlockSpec((tm, ATTN_WIDTH), tok),
            pl.BlockSpec((tm, HYENA_WIDTH), tok),
            pl.BlockSpec((tm, FNET_WIDTH), tok),
            pl.BlockSpec((tm, 3 * d), tok),
            pl.BlockSpec(wa.shape, full, pipeline_mode=pl.Buffered(1)),
            pl.BlockSpec(wh.shape, full, pipeline_mode=pl.Buffered(1)),
            pl.BlockSpec(wf.shape, full, pipeline_mode=pl.Buffered(1)),
            pl.BlockSpec(wo.shape, full, pipeline_mode=pl.Buffered(1)),
        ],
        out_specs=pl.BlockSpec((tm, d), tok),
        out_shape=jax.ShapeDtypeStruct((t, d), F32),
        compiler_params=_cparams("parallel"),
        name="merge",
    )(x2, gt, attn_o, hy_o, fn_o, gates, wa, wh, wf, wo)


def _ffn_kernel(x_ref, sh_ref, sc_ref, gt_ref, g_ref, wg_ref, wu_ref, wd_ref, o_ref, h_scr, acc_scr):
    f = pl.program_id(1)

    @pl.when(f == 0)
    def _():
        h_scr[...] = _modulated_norm(x_ref[...], g_ref[...], sc_ref[0], sh_ref[0]).astype(BF)
        acc_scr[...] = jnp.zeros_like(acc_scr)

    hb = h_scr[...]
    act = _silu(_dot(hb, wg_ref[...])) * _dot(hb, wu_ref[...])
    acc_scr[...] += _dot(act.astype(BF), wd_ref[...])

    @pl.when(f == pl.num_programs(1) - 1)
    def _():
        o_ref[...] = x_ref[...] + gt_ref[0] * acc_scr[...]


def _ffn(x2, sh, sc, gt, g, wg, wu, wd, *, tm, tf, mod_row):
    t, d = x2.shape
    ff = wg.shape[1]
    row3 = lambda i, f: (mod_row(i), 0, 0)
    wmode = dict(pipeline_mode=pl.Buffered(1)) if tf == ff else {}
    return pl.pallas_call(
        _ffn_kernel,
        grid=(t // tm, ff // tf),
        in_specs=[
            pl.BlockSpec((tm, d), lambda i, f: (i, 0)),
            pl.BlockSpec((1, 1, d), row3), pl.BlockSpec((1, 1, d), row3), pl.BlockSpec((1, 1, d), row3),
            pl.BlockSpec((1, d), lambda i, f: (0, 0)),
            pl.BlockSpec((d, tf), lambda i, f: (0, f), **wmode),
            pl.BlockSpec((d, tf), lambda i, f: (0, f), **wmode),
            pl.BlockSpec((tf, d), lambda i, f: (f, 0), **wmode),
        ],
        out_specs=pl.BlockSpec((tm, d), lambda i, f: (i, 0)),
        out_shape=jax.ShapeDtypeStruct((t, d), F32),
        scratch_shapes=[pltpu.VMEM((tm, d), BF), pltpu.VMEM((tm, d), F32)],
        compiler_params=_cparams("parallel", "arbitrary"),
        name="ffn_dense",
    )(x2, sh, sc, gt, g, wg, wu, wd)


def _top2(logits):
    lane = lax.broadcasted_iota(jnp.int32, logits.shape, 1)
    lg = jnp.where(lane < N_EXPERTS, logits, -jnp.inf)
    m1 = jnp.max(lg, axis=1, keepdims=True)
    i1 = jnp.min(jnp.where(lg == m1, lane, LANES), axis=1, keepdims=True)
    lg2 = jnp.where(lane == i1, -jnp.inf, lg)
    m2 = jnp.max(lg2, axis=1, keepdims=True)
    i2 = jnp.min(jnp.where(lg2 == m2, lane, LANES), axis=1, keepdims=True)
    e2 = jnp.exp(m2 - m1)
    w1 = 1.0 / (1.0 + e2)
    return i1, i2, w1, e2 * w1


GROUP_TILE = 256
GROUP_PAD = 64
GROUP_PIECES = GROUP_TILE // GROUP_PAD
SLOT_RADIX = 64.0


def _moe_group_kernel(x_ref, sh_ref, sc_ref, g_ref, wr_ref, xg_ref, ws_ref, slot_ref, cnt_ref,
                      h_scr, rows_scr, wm_scr):
    j = pl.program_id(1)
    tb = x_ref.shape[0]
    gt_rows = xg_ref.shape[1]

    @pl.when(j == 0)
    def _():
        h = _modulated_norm(x_ref[...], g_ref[...], sc_ref[0], sh_ref[0])
        h_scr[...] = h.astype(BF)
        i1, i2, w1, w2 = _top2(_dot3(h, wr_ref[...]))
        lane = lax.broadcasted_iota(jnp.int32, (tb, LANES), 1)
        oh0 = jnp.where(lane == i1, 1.0, 0.0)
        oh1 = jnp.where(lane == i2, 1.0, 0.0)
        c0 = jnp.sum(oh0, axis=0, keepdims=True)
        cnt = c0 + jnp.sum(oh1, axis=0, keepdims=True)
        tri = jnp.where(lax.broadcasted_iota(jnp.int32, (tb, tb), 1)
                        < lax.broadcasted_iota(jnp.int32, (tb, tb), 0), 1.0, 0.0).astype(BF)
        pre0 = _dot(tri, oh0.astype(BF))
        pre1 = _dot(tri, oh1.astype(BF)) + c0
        tiles = jnp.ceil(cnt * (1.0 / GROUP_PAD))
        upper = jnp.where(lax.broadcasted_iota(jnp.int32, (LANES, LANES), 0)
                          < lax.broadcasted_iota(jnp.int32, (LANES, LANES), 1), 1.0, 0.0).astype(BF)
        off = _dot(jnp.broadcast_to(tiles, (8, LANES)).astype(BF), upper)[0:1] * float(GROUP_PAD)
        slot0 = jnp.sum(oh0 * (off + pre0), axis=1, keepdims=True)
        slot1 = jnp.sum(oh1 * (off + pre1), axis=1, keepdims=True)
        slot_ref[0] = jnp.where(lane == 0, slot0, jnp.where(lane == 1, slot1, 0.0))
        cnt_ref[0] = jnp.broadcast_to(cnt, (8, LANES))
        hi0 = jnp.floor(slot0 * (1.0 / SLOT_RADIX))
        hi1 = jnp.floor(slot1 * (1.0 / SLOT_RADIX))
        digits = jnp.where(lane == 0, hi0, jnp.where(lane == 1, slot0 - SLOT_RADIX * hi0,
                           jnp.where(lane == 2, hi1, jnp.where(lane == 3, slot1 - SLOT_RADIX * hi1, 0.0))))
        sel = jnp.where(lax.broadcasted_iota(jnp.int32, (8, LANES), 0)
                        == lax.broadcasted_iota(jnp.int32, (8, LANES), 1), 1.0, 0.0).astype(BF)
        rows_scr[...] = _dot_nt(sel, digits.astype(BF))
        w1h, w1l = _split(w1)
        w1m, w1l = _split(w1 - w1h.astype(F32))
        w2h, w2l = _split(w2)
        w2m, w2l = _split(w2 - w2h.astype(F32))
        cols = [w1h, w1m, w1l, w2h, w2m, w2l]
        wm = jnp.zeros((tb, LANES), F32)
        for li, col in enumerate(cols):
            wm = jnp.where(lane == li, col.astype(F32), wm)
        wm_scr[...] = wm.astype(BF)

    rows = rows_scr[...]
    s0 = rows[0:1] * SLOT_RADIX + rows[1:2]
    s1 = rows[2:3] * SLOT_RADIX + rows[3:4]
    pos = (lax.broadcasted_iota(jnp.int32, (gt_rows, tb), 0) + j * gt_rows).astype(F32)
    g0 = jnp.where(pos == s0, 1.0, 0.0).astype(BF)
    g1 = jnp.where(pos == s1, 1.0, 0.0).astype(BF)
    xg_ref[0] = _dot(g0 + g1, h_scr[...]).astype(BF)
    lane_w = lax.broadcasted_iota(jnp.int32, (gt_rows, LANES), 1)
    wsum = (jnp.where(lane_w < 3, _dot(g0, wm_scr[...]), 0.0)
            + jnp.where((lane_w >= 3) & (lane_w < 6), _dot(g1, wm_scr[...]), 0.0))
    ws_ref[0] = jnp.broadcast_to(jnp.sum(wsum, axis=1, keepdims=True), (gt_rows, LANES))


def _moe_group(x2, sh, sc, g, wr_pad, *, tb, nt, mod_row):
    t, d = x2.shape
    nb = t // tb
    row3 = lambda b, j: (mod_row(b), 0, 0)
    return pl.pallas_call(
        _moe_group_kernel,
        grid=(nb, nt),
        in_specs=[
            pl.BlockSpec((tb, d), lambda b, j: (b, 0)),
            pl.BlockSpec((1, 1, d), row3), pl.BlockSpec((1, 1, d), row3),
            pl.BlockSpec((1, d), lambda b, j: (0, 0)),
            pl.BlockSpec((d, LANES), lambda b, j: (0, 0)),
        ],
        out_specs=[
            pl.BlockSpec((1, GROUP_TILE, d), lambda b, j: (b * nt + j, 0, 0)),
            pl.BlockSpec((1, GROUP_TILE, LANES), lambda b, j: (b * nt + j, 0, 0)),
            pl.BlockSpec((1, tb, LANES), lambda b, j: (b, 0, 0)),
            pl.BlockSpec((1, 8, LANES), lambda b, j: (b, 0, 0)),
        ],
        out_shape=[
            jax.ShapeDtypeStruct((nb * nt, GROUP_TILE, d), BF),
            jax.ShapeDtypeStruct((nb * nt, GROUP_TILE, LANES), F32),
            jax.ShapeDtypeStruct((nb, tb, LANES), F32),
            jax.ShapeDtypeStruct((nb, 8, LANES), F32),
        ],
        scratch_shapes=[pltpu.VMEM((tb, d), BF), pltpu.VMEM((8, tb), F32), pltpu.VMEM((tb, LANES), BF)],
        compiler_params=_cparams("parallel", "arbitrary"),
        name="moe_group",
    )(x2, sh, sc, g, wr_pad)


def _moe_schedule(cnt, nh):
    np_ = GROUP_PIECES
    h = (cnt + GROUP_PAD - 1) // GROUP_PAD
    nb, ne = h.shape
    tot = h.sum(0)
    pairs = (tot + np_ - 1) // np_
    cum_p = jnp.cumsum(pairs)
    start_p = cum_p - pairs
    n_used = cum_p[-1]
    n_steps = (nb * nh + ne * (np_ - 1)) // np_
    q = jnp.minimum(jnp.arange(n_steps, dtype=jnp.int32), n_used - 1)
    e = jnp.sum(q[:, None] >= cum_p[None, :], axis=1).astype(jnp.int32)
    r = q - start_p[e]
    cum_b = jnp.cumsum(h, axis=0)
    first = jnp.cumsum(h, axis=1) - h

    def piece(idx):
        idx = jnp.minimum(idx, tot[e] - 1)
        blk = jnp.sum(idx[:, None] >= cum_b.T[e], axis=1).astype(jnp.int32)
        return blk * nh + first[blk, e] + idx - (cum_b[blk, e] - h[blk, e])

    x = jnp.arange(nh, dtype=jnp.int32)[None, :]
    ex = jnp.sum(x[:, :, None] >= jnp.cumsum(h, axis=1)[:, None, :], axis=2).astype(jnp.int32)
    exc = jnp.minimum(ex, ne - 1)
    g = jnp.take_along_axis(cum_b - h, exc, axis=1) + x - jnp.take_along_axis(first, exc, axis=1)
    loc = np_ * (start_p[exc] + g // np_) + g % np_
    loc = jnp.where(ex < ne, loc, loc[:, 0:1])
    pieces = jnp.concatenate([piece(np_ * r + k) for k in range(np_)]).astype(jnp.int32)
    return (pieces, e, n_used.astype(jnp.int32).reshape(1), loc.reshape(-1).astype(jnp.int32),
            ((h.sum(1) + np_ - 1) // np_).astype(jnp.int32))


def _moe_expert_kernel(pc_ref, exp_ref, nused_ref, *refs):
    np_ = GROUP_PIECES
    x_refs, w_refs = refs[:np_], refs[np_:2 * np_]
    wg_ref, wu_ref, wd_ref, y_ref = refs[2 * np_:]

    @pl.when(pl.program_id(0) < nused_ref[0])
    def _():
        x = jnp.concatenate([r[0] for r in x_refs], axis=0)
        act = _silu(_dot(x, wg_ref[0])) * _dot(x, wu_ref[0])
        y = _dot(act.astype(BF), wd_ref[0])
        w = jnp.concatenate([r[0] for r in w_refs], axis=0)
        y_ref[0] = (y * jnp.concatenate([w] * (y.shape[1] // LANES), axis=1)).astype(BF)


def _moe_experts(pieces, step_exp, n_used, xg, ws, wg, wu, wd):
    d = xg.shape[-1]
    ff = wg.shape[2]
    np_ = GROUP_PIECES
    n_steps = step_exp.shape[0]
    xh = xg.reshape(-1, GROUP_PAD, d)
    wh = ws.reshape(-1, GROUP_PAD, LANES)
    pc3 = lambda k: (lambda i, pc, se, nu: (pc[k * n_steps + i], 0, 0))
    exp3 = lambda i, pc, se, nu: (se[i], 0, 0)
    return pl.pallas_call(
        _moe_expert_kernel,
        grid_spec=pltpu.PrefetchScalarGridSpec(
            num_scalar_prefetch=3,
            grid=(n_steps,),
            in_specs=(
                [pl.BlockSpec((1, GROUP_PAD, d), pc3(k)) for k in range(np_)]
                + [pl.BlockSpec((1, GROUP_PAD, LANES), pc3(k)) for k in range(np_)]
                + [pl.BlockSpec((1, d, ff), exp3), pl.BlockSpec((1, d, ff), exp3), pl.BlockSpec((1, ff, d), exp3)]
            ),
            out_specs=pl.BlockSpec((1, GROUP_TILE, d), lambda i, pc, se, nu: (jnp.minimum(i, nu[0] - 1), 0, 0)),
        ),
        out_shape=jax.ShapeDtypeStruct((n_steps, GROUP_TILE, d), BF),
        compiler_params=_cparams("arbitrary"),
        name="moe_experts",
    )(pieces, step_exp, n_used, *([xh] * np_), *([wh] * np_), wg, wu, wd)


def _moe_combine_kernel(nt_ref, loc_ref, x_ref, gt_ref, slot_ref, *refs):
    y_refs, o_ref, acc_scr = refs[:GROUP_PIECES], refs[GROUP_PIECES], refs[GROUP_PIECES + 1]
    b = pl.program_id(0)
    j = pl.program_id(1)
    tb = x_ref.shape[0]

    @pl.when(j == 0)
    def _():
        acc_scr[...] = jnp.zeros_like(acc_scr)

    @pl.when(j < nt_ref[b])
    def _():
        sl = slot_ref[0]
        pos = (lax.broadcasted_iota(jnp.int32, (tb, GROUP_TILE), 1) + j * GROUP_TILE).astype(F32)
        p = jnp.where((pos == sl[:, 0:1]) | (pos == sl[:, 1:2]), 1.0, 0.0).astype(BF)
        acc_scr[...] += _dot(p, jnp.concatenate([r[0] for r in y_refs], axis=0))

    @pl.when(j == pl.num_programs(1) - 1)
    def _():
        o_ref[...] = x_ref[...] + gt_ref[0] * acc_scr[...]


def _moe_combine(ntiles_b, loc, x2, gt, slots, yg, *, tb, nt, mod_row):
    t, d = x2.shape
    nb = t // tb
    np_ = GROUP_PIECES
    yh = yg.reshape(-1, GROUP_PAD, d)

    def piece3(k):
        return lambda b, j, n, lc: (lc[(b * nt + jnp.minimum(j, n[b] - 1)) * np_ + k], 0, 0)

    return pl.pallas_call(
        _moe_combine_kernel,
        grid_spec=pltpu.PrefetchScalarGridSpec(
            num_scalar_prefetch=2,
            grid=(nb, nt),
            in_specs=[
                pl.BlockSpec((tb, d), lambda b, j, n, lc: (b, 0)),
                pl.BlockSpec((1, 1, d), lambda b, j, n, lc: (mod_row(b), 0, 0)),
                pl.BlockSpec((1, tb, LANES), lambda b, j, n, lc: (b, 0, 0)),
            ] + [pl.BlockSpec((1, GROUP_PAD, d), piece3(k)) for k in range(np_)],
            out_specs=pl.BlockSpec((tb, d), lambda b, j, n, lc: (b, 0)),
            scratch_shapes=[pltpu.VMEM((tb, d), F32)],
        ),
        out_shape=jax.ShapeDtypeStruct((t, d), F32),
        compiler_params=_cparams("parallel", "arbitrary"),
        name="moe_combine",
    )(ntiles_b, loc, x2, gt, slots, *([yh] * np_))


def _moe(x2, sh, sc, gt, g, wr_pad, wg, wu, wd, *, tb, mod_row):
    nt = -(-(2 * tb + N_EXPERTS * (GROUP_PAD - 1)) // GROUP_TILE)
    xg, ws, slots, cnt = _moe_group(x2, sh, sc, g, wr_pad, tb=tb, nt=nt, mod_row=mod_row)
    counts = cnt[:, 0, :N_EXPERTS].astype(jnp.int32)
    pieces, step_exp, n_used, loc, ntiles_b = _moe_schedule(counts, nt * GROUP_PIECES)
    yg = _moe_experts(pieces, step_exp, n_used, xg, ws, wg, wu, wd)
    return _moe_combine(ntiles_b, loc, x2, gt, slots, yg, tb=tb, nt=nt, mod_row=mod_row)


def _rope_tables(seq_len):
    pos = np.arange(seq_len)
    prow = (pos // GRID_W).astype(np.float32)
    pcol = (pos % GRID_W).astype(np.float32)
    n_freq = HEAD_DIM // 4
    inv = (np.float32(ROPE_THETA) ** (-np.arange(n_freq, dtype=np.float32) / n_freq)).astype(np.float32)
    ar = (prow[:, None] * inv[None, :]).astype(np.float64)
    ac = (pcol[:, None] * inv[None, :]).astype(np.float64)
    cos = np.concatenate([np.cos(ar)] * 2 + [np.cos(ac)] * 2, axis=1)
    sin = np.concatenate([-np.sin(ar), np.sin(ar), -np.sin(ac), np.sin(ac)], axis=1)
    return (jnp.asarray(np.concatenate([cos, cos], axis=1), dtype=F32),
            jnp.asarray(np.concatenate([sin, sin], axis=1), dtype=F32))


def _head_sum_matrix():
    c = np.arange(ATTN_WIDTH)
    return jnp.asarray((c[:, None] // HEAD_DIM) == (c[None, :] // HEAD_DIM), dtype=F32).astype(BF)


def _fnet_channel_matrix():
    c = np.arange(FNET_WIDTH)
    same = (c[:, None] // FNET_GROUP_DIM) == (c[None, :] // FNET_GROUP_DIM)
    ang = 2.0 * np.pi * (((c[:, None] % FNET_GROUP_DIM) * (c[None, :] % FNET_GROUP_DIM)) % FNET_GROUP_DIM) / FNET_GROUP_DIM
    cb = np.where(same, np.cos(ang), 0.0)
    sb = np.where(same, np.sin(ang), 0.0)
    return jnp.asarray(np.concatenate([cb, -sb], axis=1), dtype=F32).astype(BF)


def _filter_features(n):
    t = np.linspace(0.0, 1.0, n)[:, None]
    w = 2.0 * np.pi * np.arange(n)[:, None] / n
    fb = np.linspace(1e-4, FILTER_BANDS - 1, FILTER_BANDS)
    z = np.concatenate([t, np.cos(fb * w), -np.sin(fb * w)], axis=-1)
    return jnp.asarray(np.pad(z, ((0, 0), (0, 64 - z.shape[1]))), dtype=F32)


def _decay_rates():
    d = jnp.abs(jnp.linspace(math.log(DECAY_TARGET) / SLOW_DECAY_PCT, math.log(DECAY_TARGET) / FAST_DECAY_PCT,
                             HYENA_WIDTH, dtype=F32))
    return jnp.concatenate([d] * HYENA_ORDER)[None, :]


def kernel(x, c, ctx, c_ctx, w_ada, b_ada, norm1_g, norm2_g, w_in, q_norm_g, k_norm_g, attn_sink,
           hy_conv_w, hy_conv_b, hy_filt_w1, hy_filt_b1, hy_filt_freq1, hy_filt_w2, hy_filt_b2,
           hy_filt_freq2, hy_filt_w3, hy_bias, w_proj_attn, w_proj_hyena, w_proj_fnet, w_out,
           ffn_w_gate, ffn_w_up, ffn_w_down, moe_router, moe_w_gate, moe_w_up, moe_w_down):
    b, seq, d = x.shape
    n_ctx = ctx.shape[1]
    depth = w_ada.shape[0]
    tm = 512
    tiles_per_seq = seq // tm

    cond8 = jnp.concatenate([c, c_ctx[None, :], jnp.zeros((8 - b - 1, d), F32)], axis=0)
    mods = _adaln(cond8, w_ada, b_ada)

    cos_l, sin_l = _rope_tables(seq)
    cos_c = jnp.ones((n_ctx, LANES), F32)
    sin_c = jnp.zeros((n_ctx, LANES), F32)
    gsum = _head_sum_matrix()
    mfn = _fnet_channel_matrix()
    deltas = _decay_rates()
    zfeat_l = _filter_features(seq)
    zfeat_c = _filter_features(n_ctx)

    lat_row = lambda i: i // tiles_per_seq
    ctx_row = lambda i: b
    lat_tab = lambda i: i % tiles_per_seq
    ctx_tab = lambda i: 0
    tm_c = min(tm, n_ctx)

    xs = x.reshape(b * seq, d)
    cs = ctx.reshape(b * n_ctx, d)
    for l in range(depth):
        last = l == depth - 1
        mod = lambda j: mods[l, :, j * d:(j + 1) * d].reshape(8, 1, d)
        w_in_bf = w_in[l].astype(BF)
        qg = jnp.tile(q_norm_g[l], N_HEADS)[None, :]
        kg = jnp.tile(k_norm_g[l], N_KV_HEADS)[None, :]
        g1 = norm1_g[l][None, :]
        wa, wh, wf, wo = (w_proj_attn[l].astype(BF), w_proj_hyena[l].astype(BF),
                          w_proj_fnet[l].astype(BF), w_out[l].astype(BF))
        conv_w = hy_conv_w[l].reshape(3, -1)
        conv_b = hy_conv_b[l][None, :]
        w1p = jnp.pad(hy_filt_w1[l], ((0, 64 - hy_filt_w1.shape[1]), (0, 0)))
        filt = (w1p, hy_filt_b1[l][None, :], hy_filt_freq1[l][None, :], hy_filt_w2[l],
                hy_filt_b2[l][None, :], hy_filt_freq2[l][None, :], hy_filt_w3[l], deltas)

        qkv_c, upq_c, gates_c = _phase_a(
            cs, mod(0), mod(1), g1, w_in_bf, cos_c, sin_c, qg, kg, gsum, mfn, conv_w, conv_b,
            tm=tm_c, tiles_per_seq=n_ctx // tm_c, mod_row=ctx_row, tab_row=ctx_tab)
        qkv_c = qkv_c.reshape(b, n_ctx, -1)
        upq_c = upq_c.reshape(b, n_ctx, -1)

        qkv_l, upq_l, gates_l = _phase_a(
            xs, mod(0), mod(1), g1, w_in_bf, cos_l, sin_l, qg, kg, gsum, mfn, conv_w, conv_b,
            tm=tm, tiles_per_seq=tiles_per_seq, mod_row=lat_row, tab_row=lat_tab)
        upq_l = upq_l.reshape(b, seq, -1)
        attn_l = _attention(attn_sink[l], qkv_l.reshape(b, seq, -1), qkv_c, local=True, tq=512)
        kf_l, kb_l, nrm_l = _hy_filter(zfeat_l, *filt, tm=1024)
        hy_l = _hyena(upq_l, (kf_l, kb_l), nrm_l, hy_bias[l])
        fn_l = _fnet(upq_l)
        xs = _merge(xs, mod(2), attn_l.reshape(b * seq, -1), hy_l, fn_l, gates_l, wa, wh, wf, wo,
                    tm=tm, mod_row=lat_row)

        if not last:
            attn_c = _attention(attn_sink[l], qkv_c, qkv_c, local=False, tq=n_ctx)
            kf_c, kb_c, nrm_c = _hy_filter(zfeat_c, *filt, tm=n_ctx)
            hy_c = _hyena(upq_c, (kf_c, kb_c), nrm_c, hy_bias[l])
            fn_c = _fnet(upq_c)
            cs = _merge(cs, mod(2), attn_c.reshape(b * n_ctx, -1), hy_c, fn_c, gates_c, wa, wh, wf, wo,
                        tm=tm_c, mod_row=ctx_row)

        g2 = norm2_g[l][None, :]
        i = l // 2
        if l % 2 == 0:
            wg, wu, wd = ffn_w_gate[i].astype(BF), ffn_w_up[i].astype(BF), ffn_w_down[i].astype(BF)
            run = lambda t2, rows, tmm: _ffn(t2, mod(3), mod(4), mod(5), g2, wg, wu, wd,
                                             tm=tmm, tf=D_FF, mod_row=rows)
        else:
            wr = jnp.pad(moe_router[i], ((0, 0), (0, LANES - N_EXPERTS)))
            wg, wu, wd = moe_w_gate[i].astype(BF), moe_w_up[i].astype(BF), moe_w_down[i].astype(BF)
            run = lambda t2, rows, tmm: _moe(t2, mod(3), mod(4), mod(5), g2, wr, wg, wu, wd,
                                             tb=tmm, mod_row=rows)
        tm_ffn = 512 if l % 2 == 0 else 1024
        xs = run(xs, lambda t: t // (seq // tm_ffn), tm_ffn)
        if not last:
            cs = run(cs, ctx_row, min(tm_ffn, b * n_ctx))
    return xs.reshape(b, seq, d)
```

```python
import functools
import math

import numpy as np
import jax
import jax.numpy as jnp
from jax import lax
from jax.experimental import pallas as pl
from jax.experimental.pallas import tpu as pltpu

F32 = jnp.float32
BF = jnp.bfloat16

D_MODEL = 1024
DEPTH = 4
GRID_W = 64
HEAD_DIM = 64
N_HEADS = 8
N_KV_HEADS = 2
ATTN_WIDTH = N_HEADS * HEAD_DIM
KV_WIDTH = N_KV_HEADS * HEAD_DIM
WINDOW = 128
QBLK = 128
ROPE_THETA = 10000.0
HYENA_ORDER = 2
HYENA_WIDTH = 256
FILTER_BANDS = 16
FILTER_HIDDEN = 64
DECAY_TARGET = 1e-2
FAST_DECAY_PCT = 0.3
SLOW_DECAY_PCT = 1.5
FNET_WIDTH = 256
FNET_GROUP_DIM = 64
Q_END = ATTN_WIDTH
K_END = Q_END + KV_WIDTH
V_END = K_END + KV_WIDTH
HY_END = V_END + (HYENA_ORDER + 1) * HYENA_WIDTH
FN_END = HY_END + FNET_WIDTH
IN_WIDTH = FN_END + 3 * D_MODEL
D_FF = 2816
N_EXPERTS = 8
EPS = 1e-6
LANES = 128
NEG = -1e30
STAGE_ROWS = 16
STAGE_JOIN = 2

VMEM_LIMIT = 56 * 1024 * 1024


def _cparams(*sem):
    return pltpu.CompilerParams(dimension_semantics=sem, vmem_limit_bytes=VMEM_LIMIT)


def _dot(a, b):
    return jnp.dot(a, b, preferred_element_type=F32)


def _dot_nt(a, b):
    return lax.dot_general(a, b, (((1,), (1,)), ((), ())), preferred_element_type=F32)


def _split(a):
    hi = a.astype(BF)
    lo = (a - hi.astype(F32)).astype(BF)
    return hi, lo


def _dot3(a, b):
    ah, al = _split(a)
    bh, bl = _split(b)
    return _dot(ah, bh) + (_dot(ah, bl) + _dot(al, bh))


def _dot2(a, b_bf16):
    ah, al = _split(a)
    return _dot(ah, b_bf16) + _dot(al, b_bf16)


def _sigmoid(v):
    return 0.5 * jnp.tanh(0.5 * v) + 0.5


def _silu(v):
    return v * _sigmoid(v)


def _adaln_kernel(c_ref, w_ref, b_ref, o_ref):
    o_ref[0] = _dot3(_silu(c_ref[...]), w_ref[0]) + b_ref[0]


def _adaln(cond8, w_ada, b_ada):
    depth, d, n6 = w_ada.shape
    tn = 1024
    return pl.pallas_call(
        _adaln_kernel,
        grid=(depth, n6 // tn),
        in_specs=[
            pl.BlockSpec((8, d), lambda l, j: (0, 0)),
            pl.BlockSpec((1, d, tn), lambda l, j: (l, 0, j)),
            pl.BlockSpec((1, 1, tn), lambda l, j: (l, 0, j)),
        ],
        out_specs=pl.BlockSpec((1, 8, tn), lambda l, j: (l, 0, j)),
        out_shape=jax.ShapeDtypeStruct((depth, 8, n6), F32),
        compiler_params=_cparams("parallel", "parallel"),
        name="adaln",
    )(cond8, w_ada, b_ada.reshape(depth, 1, n6))


def _modulated_norm(x, g, sc, sh):
    ms = jnp.mean(x * x, axis=-1, keepdims=True)
    h = (x * lax.rsqrt(ms + EPS)) * g
    return h * (1.0 + sc) + sh


HALO = 16


def _phase_a_kernel(x_ref, xp_ref, xn_ref, sh_ref, sc_ref, g_ref, w_ref, cos_ref, sin_ref, qg_ref, kg_ref,
                    gsum_ref, mfn_ref, cw_ref, cb_ref, qkv_ref, upq_ref, gate_ref, *, tiles_per_seq):
    tm = x_ref.shape[0]
    norm = lambda xv: _modulated_norm(xv, g_ref[...], sc_ref[0], sh_ref[0]).astype(BF)
    hb = norm(x_ref[...])
    cos = cos_ref[...]
    sin = sin_ref[...]

    def headnorm(t, gain, gs):
        ss = _dot2(t * t, gs)
        return t * lax.rsqrt(ss * (1.0 / HEAD_DIM) + EPS) * gain

    def rope(t, cosw, sinw):
        w = t.shape[1]
        nxt = pltpu.roll(t, w - 16, axis=1)
        prv = pltpu.roll(t, 16, axis=1)
        lw = lax.broadcasted_iota(jnp.int32, t.shape, 1)
        return t * cosw + jnp.where((lw % 32) < 16, nxt, prv) * sinw

    def dup_halves(t):
        lane = lax.broadcasted_iota(jnp.int32, t.shape, 1)
        sw = pltpu.roll(t, 64, axis=1)
        lo = lane < 64
        return jnp.concatenate([jnp.where(lo, t, sw), jnp.where(lo, sw, t)], axis=1)

    pr = _dot(hb, w_ref[:, 0:FN_END])
    pq = pr[:, 0:Q_END]
    qn = headnorm(pq, qg_ref[...], gsum_ref[...])
    cos4 = jnp.concatenate([cos] * 4, axis=1)
    sin4 = jnp.concatenate([sin] * 4, axis=1)
    qkv_ref[:, 0:Q_END] = (rope(qn, cos4, sin4) * (HEAD_DIM ** -0.5)).astype(BF)

    kn = headnorm(pr[:, Q_END:K_END], kg_ref[...], gsum_ref[0:KV_WIDTH, 0:KV_WIDTH])
    qkv_ref[:, Q_END:Q_END + 2 * KV_WIDTH] = dup_halves(rope(kn, cos, sin)).astype(BF)
    qkv_ref[:, Q_END + 2 * KV_WIDTH:] = dup_halves(pr[:, K_END:V_END]).astype(BF)

    w_hy = w_ref[:, V_END:HY_END]
    u = pr[:, V_END:HY_END]
    tile = pl.program_id(0) % tiles_per_seq
    u_before = jnp.where(tile == 0, 0.0, _dot(norm(xp_ref[...]), w_hy)[HALO - 1:HALO])
    u_after = jnp.where(tile == tiles_per_seq - 1, 0.0, _dot(norm(xn_ref[...]), w_hy)[0:1])
    row = lax.broadcasted_iota(jnp.int32, (tm, 1), 0)
    prv = jnp.where(row == 0, u_before, pltpu.roll(u, 1, axis=0))
    nxt = jnp.where(row == tm - 1, u_after, pltpu.roll(u, tm - 1, axis=0))
    n_hy = HY_END - V_END
    upq_ref[:, 0:n_hy] = prv * cw_ref[0:1, :] + u * cw_ref[1:2, :] + nxt * cw_ref[2:3, :] + cb_ref[...]

    upq_ref[:, n_hy:] = _dot(pr[:, HY_END:FN_END].astype(BF), mfn_ref[...])
    gate_ref[...] = _sigmoid(_dot(hb, w_ref[:, FN_END:])).astype(BF)


def _phase_a(x2, sh, sc, g, w_in_bf, cos_t, sin_t, qg, kg, gsum, mfn, conv_w, conv_b, *, tm, tiles_per_seq,
             mod_row, tab_row):
    t, d = x2.shape
    row3 = lambda i: (mod_row(i), 0, 0)
    full = lambda i: (0, 0)
    tok = lambda i: (i, 0)
    n_halo = t // HALO
    outs = [
        ((t, ATTN_WIDTH + 4 * KV_WIDTH), BF), ((t, 3 * HYENA_WIDTH + 2 * FNET_WIDTH), F32),
        ((t, 3 * D_MODEL), BF),
    ]
    kern = functools.partial(_phase_a_kernel, tiles_per_seq=tiles_per_seq)
    return pl.pallas_call(
        kern,
        grid=(t // tm,),
        in_specs=[
            pl.BlockSpec((tm, d), tok),
            pl.BlockSpec((HALO, d), lambda i: (jnp.maximum(i * (tm // HALO) - 1, 0), 0)),
            pl.BlockSpec((HALO, d), lambda i: (jnp.minimum((i + 1) * (tm // HALO), n_halo - 1), 0)),
            pl.BlockSpec((1, 1, d), row3),
            pl.BlockSpec((1, 1, d), row3),
            pl.BlockSpec((1, d), full),
            pl.BlockSpec((d, IN_WIDTH), full, pipeline_mode=pl.Buffered(1)),
            pl.BlockSpec((tm, LANES), lambda i: (tab_row(i), 0)),
            pl.BlockSpec((tm, LANES), lambda i: (tab_row(i), 0)),
            pl.BlockSpec((1, ATTN_WIDTH), full),
            pl.BlockSpec((1, KV_WIDTH), full),
            pl.BlockSpec((ATTN_WIDTH, ATTN_WIDTH), full),
            pl.BlockSpec((FNET_WIDTH, 2 * FNET_WIDTH), full),
            pl.BlockSpec(conv_w.shape, full),
            pl.BlockSpec(conv_b.shape, full),
        ],
        out_specs=[pl.BlockSpec((tm, s[1]), tok) for s, _ in outs],
        out_shape=[jax.ShapeDtypeStruct(s, dt) for s, dt in outs],
        compiler_params=_cparams("parallel"),
        name="phase_a",
    )(x2, x2, x2, sh, sc, g, w_in_bf, cos_t, sin_t, qg, kg, gsum, mfn, conv_w, conv_b)


def _attn_kernel(sink_ref, q_ref, kd_ref, vd_ref, kc_ref, vc_ref, o_ref, *, local, seq_len):
    tq = q_ref.shape[1]
    nblk = tq // QBLK
    gq = N_HEADS // N_KV_HEADS
    rows = gq * QBLK
    lane = lax.broadcasted_iota(jnp.int32, (QBLK, LANES), 1)
    lo_half = lane < 64
    hrow = lax.broadcasted_iota(jnp.int32, (rows, 1), 0) // QBLK
    nband = 3 * QBLK
    if local:
        qk_off = (lax.broadcasted_iota(jnp.int32, (rows, nband), 0) % QBLK
                  - lax.broadcasted_iota(jnp.int32, (rows, nband), 1))
    for blk in range(nblk):
        r0 = blk * QBLK
        qb = q_ref[0, r0:r0 + QBLK, :]
        if local:
            n = pl.program_id(1) * nblk + blk
            start = pl.multiple_of(jnp.clip((n - 1) * QBLK, 0, seq_len - nband), QBLK)
            valid = jnp.abs(qk_off + (n * QBLK - start)) <= WINDOW
        for g in range(N_KV_HEADS):
            parts = []
            for hh in range(gq):
                h = gq * g + hh
                qc = qb[:, (h // 2) * LANES:(h // 2 + 1) * LANES]
                keep = lo_half if h % 2 == 0 else jnp.logical_not(lo_half)
                parts.append(jnp.where(keep, qc, jnp.zeros_like(qc)))
            q4 = jnp.concatenate(parts, axis=0)
            sk = jnp.full((rows, 1), sink_ref[gq * g + gq - 1], F32)
            for hh in range(gq - 2, -1, -1):
                sk = jnp.where(hrow == hh, sink_ref[gq * g + hh], sk)
            gl = slice(g * LANES, (g + 1) * LANES)
            s_ctx = _dot_nt(q4, kc_ref[0, :, gl])
            m = jnp.maximum(jnp.max(s_ctx, axis=1, keepdims=True), sk)
            if local:
                s_loc = _dot_nt(q4, kd_ref[0, pl.ds(start, nband), gl])
                s_loc = jnp.where(valid, s_loc, NEG)
                m = jnp.maximum(m, jnp.max(s_loc, axis=1, keepdims=True))
            p_ctx = jnp.exp(s_ctx - m)
            den = jnp.sum(p_ctx, axis=1, keepdims=True) + jnp.exp(sk - m)
            o = _dot(p_ctx.astype(BF), vc_ref[0, :, gl])
            if local:
                p_loc = jnp.exp(s_loc - m)
                den = den + jnp.sum(p_loc, axis=1, keepdims=True)
                o = o + _dot(p_loc.astype(BF), vd_ref[0, pl.ds(start, nband), gl])
            o = o / den
            for cc in range(gq // 2):
                col = (gq // 2) * g + cc
                oa = o[(2 * cc) * QBLK:(2 * cc + 1) * QBLK]
                ob = o[(2 * cc + 1) * QBLK:(2 * cc + 2) * QBLK]
                o_ref[0, r0:r0 + QBLK, col * LANES:(col + 1) * LANES] = (
                    jnp.where(lo_half, oa, ob).astype(BF))


def _attention(sink, qkv, qkv_ctx, *, local, tq):
    b, lq, _ = qkv.shape
    c = qkv_ctx.shape[1]
    kw = 2 * KV_WIDTH
    k_blk, v_blk = ATTN_WIDTH // kw, ATTN_WIDTH // kw + 1
    kern = functools.partial(_attn_kernel, local=local, seq_len=lq)
    return pl.pallas_call(
        kern,
        grid=(b, lq // tq),
        in_specs=[
            pl.BlockSpec(memory_space=pltpu.SMEM),
            pl.BlockSpec((1, tq, ATTN_WIDTH), lambda bi, i: (bi, i, 0)),
            pl.BlockSpec((1, lq, kw), lambda bi, i: (bi, 0, k_blk)),
            pl.BlockSpec((1, lq, kw), lambda bi, i: (bi, 0, v_blk)),
            pl.BlockSpec((1, c, kw), lambda bi, i: (bi, 0, k_blk)),
            pl.BlockSpec((1, c, kw), lambda bi, i: (bi, 0, v_blk)),
        ],
        out_specs=pl.BlockSpec((1, tq, ATTN_WIDTH), lambda bi, i: (bi, i, 0)),
        out_shape=jax.ShapeDtypeStruct((b, lq, ATTN_WIDTH), BF),
        compiler_params=_cparams("parallel", "parallel"),
        name="attn_local" if local else "attn_ctx",
    )(sink, qkv, qkv, qkv, qkv_ctx, qkv_ctx)


FILTER_HALO = 128


def _hy_filter_kernel(z_ref, zn_ref, w1_ref, b1_ref, f1_ref, w2_ref, b2_ref, f2_ref, w3f_ref, w3b_ref, dl_ref,
                      flip_ref, kf_ref, kb_ref, nrm_ref):
    i = pl.program_id(0)
    tm = z_ref.shape[0]
    rows = tm + FILTER_HALO
    z = jnp.concatenate([z_ref[...], zn_ref[...]], axis=0)
    h = jnp.sin(f1_ref[...] * (_dot3(z, w1_ref[...]) + b1_ref[...]))
    h = jnp.sin(f2_ref[...] * (_dot3(h, w2_ref[...]) + b2_ref[...]))
    tcol = jnp.where(lax.broadcasted_iota(jnp.int32, (rows, LANES - FILTER_HIDDEN), 1) == 0, z[:, 0:1], 0.0)
    hid = jnp.concatenate([h, tcol], axis=1)

    def taps(hv, w3_ref):
        return _dot3(hv, w3_ref[...]) * jnp.exp(-hv[:, FILTER_HIDDEN:FILTER_HIDDEN + 1] * dl_ref[...])

    kf = taps(hid[0:tm], w3f_ref)
    flip = flip_ref[...]
    h1 = hid.astype(BF)
    r1 = hid - h1.astype(F32)
    h2 = r1.astype(BF)
    h3 = (r1 - h2.astype(F32)).astype(BF)
    hid_rev = _dot(flip, h1) + (_dot(flip, h2) + _dot(flip, h3))
    kb = taps(hid_rev, w3b_ref)
    out_row = (pl.num_programs(0) - 1 - i) * tm + lax.broadcasted_iota(jnp.int32, (tm, 1), 0)
    kb = jnp.where(out_row == 0, 0.0, kb)
    kf_ref[...] = kf
    kb_ref[...] = kb

    @pl.when(i == 0)
    def _():
        nrm_ref[...] = jnp.zeros_like(nrm_ref)

    nrm_ref[...] += jnp.sum(jnp.abs(kf), axis=0, keepdims=True) + jnp.sum(jnp.abs(kb), axis=0, keepdims=True)


def _hy_filter(zfeat, w1p, b1, f1, w2, b2, f2, w3, deltas2, *, tm):
    n = zfeat.shape[0]
    nb = n // tm
    wout = w3.shape[1] // 2
    w3p = jnp.pad(w3, ((0, LANES - w3.shape[0]), (0, 0)))
    p = np.arange(tm)[:, None]
    flip = jnp.asarray(np.arange(tm + FILTER_HALO)[None, :] == tm - p, dtype=F32).astype(BF)
    full = lambda i: (0, 0)
    halo_blk = lambda i: (jnp.minimum((i + 1) * (tm // FILTER_HALO), n // FILTER_HALO - 1), 0)
    return pl.pallas_call(
        _hy_filter_kernel,
        grid=(nb,),
        in_specs=[
            pl.BlockSpec((tm, zfeat.shape[1]), lambda i: (i, 0)),
            pl.BlockSpec((FILTER_HALO, zfeat.shape[1]), halo_blk),
            pl.BlockSpec(w1p.shape, full), pl.BlockSpec(b1.shape, full), pl.BlockSpec(f1.shape, full),
            pl.BlockSpec(w2.shape, full), pl.BlockSpec(b2.shape, full), pl.BlockSpec(f2.shape, full),
            pl.BlockSpec((LANES, wout), lambda i: (0, 0)),
            pl.BlockSpec((LANES, wout), lambda i: (0, 1)),
            pl.BlockSpec(deltas2.shape, full),
            pl.BlockSpec(flip.shape, full, pipeline_mode=pl.Buffered(1)),
        ],
        out_specs=[pl.BlockSpec((tm, wout), lambda i: (i, 0)), pl.BlockSpec((tm, wout), lambda i: (nb - 1 - i, 0)),
                   pl.BlockSpec((1, wout), full)],
        out_shape=[jax.ShapeDtypeStruct((n, wout), F32), jax.ShapeDtypeStruct((n, wout), F32),
                   jax.ShapeDtypeStruct((1, wout), F32)],
        compiler_params=_cparams("arbitrary"),
        name="hy_filter",
    )(zfeat, zfeat, w1p, b1, f1, w2, b2, f2, w3p, w3p, deltas2, flip)


def _dft_mats(k_out, r_in, period, sign, scale, n_in, real_out):
    k = np.arange(k_out)[:, None]
    r = np.arange(r_in)[None, :]
    ang = 2.0 * np.pi * ((k * r) % period) / period
    fr = np.cos(ang) * scale
    fi = sign * np.sin(ang) * scale
    if real_out:
        mats = [fr, -fi]
    else:
        mats = [np.concatenate([fr, fi], 0), np.concatenate([-fi, fr], 0)]
    return jnp.asarray(np.stack(mats[:n_in], 0), dtype=F32).astype(BF)


def _twiddle(s_n, k_n, n, sign, sbk):
    s0 = lax.broadcasted_iota(jnp.int32, (s_n // sbk, k_n, LANES), 0) * sbk
    k = lax.broadcasted_iota(jnp.int32, (s_n // sbk, k_n, LANES), 1)
    ang = (s0 * k).astype(F32) * (2.0 * math.pi / n)
    ang1 = lax.broadcasted_iota(jnp.int32, (k_n, LANES), 0).astype(F32) * (2.0 * math.pi / n)
    return jnp.cos(ang), sign * jnp.sin(ang), jnp.cos(ang1), sign * jnp.sin(ang1)


def _unpack_pair(p):
    return [pltpu.unpack_elementwise(p, index=i, packed_dtype=BF, unpacked_dtype=F32) for i in (0, 1)]


def _stage_kernel(*refs, n_in, r_in, k_mid, k_out, sbk, tw, spec, second, gate, real_out,
                  transposed_out, flat, packed_in, packed_spec, packed_out):
    it = iter(refs)
    x_refs = [next(it) for _ in range(n_in)]
    g_ref = next(it)
    g2_ref = next(it) if second else None
    tw_refs = [next(it) for _ in range(4)] if tw else None
    n_spec = 1 if packed_spec else 2
    spec_refs = [next(it) for _ in range(n_spec + 1)] if spec else None
    gate_refs = [next(it) for _ in range(5)] if gate else None
    out_refs = [next(it)] if (real_out or gate or packed_out) else [next(it), next(it)]
    if not flat:
        x_refs = [r.reshape(r_in * sbk, LANES) for r in x_refs]
        if spec:
            spec_refs = [r.reshape(k_mid * sbk, LANES) for r in spec_refs[:n_spec]] + spec_refs[n_spec:]
        if gate:
            gate_refs = [r.reshape(k_out * sbk, LANES) for r in gate_refs[:4]] + gate_refs[4:]
            out_refs = [out_refs[0].reshape(2 * k_out * sbk, LANES)]
        elif not transposed_out:
            out_refs = [r.reshape(k_out * sbk, LANES) for r in out_refs]
    if spec:
        inv = 1.0 / spec_refs[n_spec][...]
    if tw:
        tr, ti = tw_refs[0][0], tw_refs[1][0]
        wr, wi = tw_refs[2][...], tw_refs[3][...]
    def joined_dots(mats_ref, cols):
        acc = None
        for xi in range(len(cols[0])):
            wide = jnp.concatenate([c[xi].astype(BF) for c in cols], axis=1)
            d = _dot(mats_ref[xi], wide)
            acc = d if acc is None else acc + d
        return [acc[:, k * LANES:(k + 1) * LANES] for k in range(len(cols))]

    join = 1 if flat else min(STAGE_JOIN, sbk)
    results = {}
    for j in range(sbk):
        if j % join == 0:
            cols = []
            for jj in range(j, j + join):
                parts = [x_ref[...] if flat else x_ref[pl.ds(jj, r_in, stride=sbk), :] for x_ref in x_refs]
                cols.append(_unpack_pair(parts[0]) if packed_in else parts)
            accs = joined_dots(g_ref, cols)
            if not real_out:
                mids = []
                for jj, acc in zip(range(j, j + join), accs):
                    yr, yi = acc[:k_mid], acc[k_mid:]
                    if spec:
                        rows = slice(None) if flat else pl.ds(jj, k_mid, stride=sbk)
                        if packed_spec:
                            sr, si = _unpack_pair(spec_refs[0][rows, :])
                        else:
                            sr, si = spec_refs[0][rows, :], spec_refs[1][rows, :]
                        sr, si = sr * inv, si * inv
                        yr, yi = yr * sr - yi * si, yr * si + yi * sr
                    mids.append([yr, yi])
                if second:
                    mids = [[acc[:k_out], acc[k_out:]] for acc in joined_dots(g2_ref, mids)]
                accs = mids
            results = dict(zip(range(j, j + join), accs))
        if real_out:
            ys = [results[j]]
        else:
            yr, yi = results[j]
            if tw:
                yr, yi = yr * tr - yi * ti, yr * ti + yi * tr
                if j + 1 < sbk:
                    tr, ti = tr * wr - ti * wi, tr * wi + ti * wr
            ys = [yr, yi]
        if gate:
            o_ref = out_refs[0]
            for part, y in enumerate(ys):
                rows = slice(None) if flat else pl.ds(j, k_out, stride=sbk)
                val = gate_refs[part][rows, :] * (y + gate_refs[4][...] * gate_refs[2 + part][rows, :])
                if flat:
                    o_ref[part] = val.astype(o_ref.dtype)
                else:
                    o_ref[pl.ds(part * k_out * sbk + j, k_out, stride=sbk), :] = val.astype(o_ref.dtype)
            continue
        if packed_out:
            ys = [pltpu.pack_elementwise(ys, packed_dtype=BF)]
        for o_ref, y in zip(out_refs, ys):
            if flat:
                o_ref[...] = y.astype(o_ref.dtype)
            elif transposed_out:
                o_ref[0, j] = y.astype(o_ref.dtype)
            else:
                o_ref[pl.ds(j, k_out, stride=sbk), :] = y.astype(o_ref.dtype)


def _fft_stage(xs, x_sel, gmat, *, r_in, s_n, k_out, n_groups, n_cblk, transposed_out, real_out,
               out_dtype=F32, g2mat=None, tw=None, spec=None, spec_sel=None, gate=None, sbk=STAGE_ROWS,
               packed_in=False, packed_out=False, name="fft_stage"):
    n_in = len(xs)
    flat = s_n == 1
    sbk = 1 if flat else min(sbk, s_n)
    cb = LANES
    in_specs, args = [], []
    for x, sel in zip(xs, x_sel):
        if flat:
            in_specs.append(pl.BlockSpec((None, r_in, cb), lambda s, g, c, sel=sel: (sel(g, c)[0], 0, sel(g, c)[1])))
            args.append(x)
        else:
            xv = x.reshape(x.shape[0], x.shape[1] // s_n, s_n, x.shape[2])
            in_specs.append(pl.BlockSpec((1, r_in, sbk, cb),
                                         lambda s, g, c, sel=sel: (sel(g, c)[0], 0, s, sel(g, c)[1])))
            args.append(xv)
    in_specs.append(pl.BlockSpec(gmat.shape, lambda s, g, c: (0, 0, 0)))
    args.append(gmat)
    k_mid = gmat.shape[1] // (1 if real_out else 2)
    if g2mat is not None:
        in_specs.append(pl.BlockSpec(g2mat.shape, lambda s, g, c: (0, 0, 0)))
        args.append(g2mat)
    if tw is not None:
        for tarr in tw[:2]:
            in_specs.append(pl.BlockSpec((1, k_out, LANES), lambda s, g, c: (s, 0, 0)))
            args.append(tarr)
        for tarr in tw[2:]:
            in_specs.append(pl.BlockSpec((k_out, LANES), lambda s, g, c: (0, 0)))
            args.append(tarr)
    if spec is not None:
        *planes, nrm = spec
        for arr in planes:
            if flat:
                in_specs.append(pl.BlockSpec((k_mid, cb), lambda s, g, c: (0, spec_sel(g, c))))
                args.append(arr)
            else:
                in_specs.append(pl.BlockSpec((1, k_mid, sbk, cb), lambda s, g, c: (0, 0, s, spec_sel(g, c))))
                args.append(arr.reshape(1, k_mid, s_n, arr.shape[-1]))
        in_specs.append(pl.BlockSpec((1, cb), lambda s, g, c: (0, spec_sel(g, c))))
        args.append(nrm)
    if gate is not None:
        (ga, gblk), (za, zblk), bias = gate
        for arr, blk in ((ga, gblk), (za, zblk)):
            for bi in (0, 1):
                if flat:
                    in_specs.append(pl.BlockSpec((None, k_out, cb), lambda s, g, c, bi=bi, blk=blk: (bi, 0, blk + c)))
                    args.append(arr)
                else:
                    in_specs.append(pl.BlockSpec((1, k_out, sbk, cb),
                                                 lambda s, g, c, bi=bi, blk=blk: (bi, 0, s, blk + c)))
                    args.append(arr.reshape(arr.shape[0], k_out, s_n, arr.shape[-1]))
        in_specs.append(pl.BlockSpec((1, cb), lambda s, g, c: (0, c)))
        args.append(bias)
    ctot = n_cblk * cb
    if gate is not None:
        n_groups = 2
        if flat:
            oshape = (2, k_out, ctot)
            ospec = pl.BlockSpec((2, k_out, cb), lambda s, g, c: (0, 0, c))
        else:
            oshape = (2, k_out, s_n, ctot)
            ospec = pl.BlockSpec((2, k_out, sbk, cb), lambda s, g, c: (0, 0, s, c))
    elif flat:
        oshape = (n_groups, k_out, ctot)
        ospec = pl.BlockSpec((None, k_out, cb), lambda s, g, c: (g, 0, c))
    elif transposed_out:
        oshape = (n_groups, s_n, k_out, ctot)
        ospec = pl.BlockSpec((1, sbk, k_out, cb), lambda s, g, c: (g, s, 0, c))
    else:
        oshape = (n_groups, k_out, s_n, ctot)
        ospec = pl.BlockSpec((1, k_out, sbk, cb), lambda s, g, c: (g, 0, s, c))
    n_out = 1 if (real_out or gate is not None or packed_out) else 2
    if packed_out:
        out_dtype = jnp.int32
    kern = functools.partial(_stage_kernel, n_in=n_in, r_in=r_in, k_mid=k_mid, k_out=k_out, sbk=sbk,
                             tw=tw is not None, spec=spec is not None, second=g2mat is not None,
                             gate=gate is not None, real_out=real_out, transposed_out=transposed_out,
                             flat=flat, packed_in=packed_in, packed_spec=spec is not None and len(spec) == 2,
                             packed_out=packed_out)
    n_grid_groups = 1 if gate is not None else n_groups
    outs = pl.pallas_call(
        kern,
        grid=(s_n // sbk, n_grid_groups, n_cblk),
        in_specs=in_specs,
        out_specs=[ospec] * n_out,
        out_shape=[jax.ShapeDtypeStruct(oshape, out_dtype)] * n_out,
        compiler_params=_cparams("parallel", "parallel", "parallel"),
        name=name,
    )(*args)
    return [o.reshape(n_groups, -1, ctot) for o in outs]


def _split_len(n):
    if n <= 1024:
        return n, 1
    s = 128
    return n // s, s


def _fft_forward(xs, x_sel, n, n_rows, *, n_groups, n_cblk, halves=False, name="fwd"):
    n1, s = _split_len(n)
    n_in = len(xs)

    def mats(k_out, r_in, period):
        if not halves:
            return _dft_mats(k_out, r_in, period, -1.0, 1.0, n_in, False)
        g = _dft_mats(k_out, 2 * r_in, period, -1.0, 1.0, 1, False)[0]
        return jnp.stack([g[:, :r_in], g[:, r_in:]], axis=0)

    if s == 1:
        g = mats(n, n_rows, n)
        return _fft_stage(xs, x_sel, g, r_in=n_rows, s_n=1, k_out=n, n_groups=n_groups, n_cblk=n_cblk,
                          transposed_out=False, real_out=False, name=name + "_direct")
    r1 = n_rows // s
    g1 = mats(n1, r1, n1)
    tw = _twiddle(s, n1, n, -1.0, STAGE_ROWS)
    (a,) = _fft_stage(xs, x_sel, g1, r_in=r1, s_n=s, k_out=n1, n_groups=n_groups, n_cblk=n_cblk,
                      transposed_out=True, real_out=False, tw=tw, packed_out=True, name=name + "_s1")
    g2 = _dft_mats(s, s, s, -1.0, 1.0, 2, False)
    return _fft_stage([a], [lambda g, c: (g, c)], g2, r_in=s, s_n=n1, k_out=s, n_groups=n_groups,
                      n_cblk=n_cblk, transposed_out=False, real_out=False, packed_in=True, packed_out=True,
                      name=name + "_s2")


def _hyena(uc, k_halves, nrm, hy_bias):
    b, n, _ = uc.shape
    w = HYENA_WIDTH
    wblk = w // LANES
    nfft = 2 * n
    n1, s = _split_len(nfft)
    ident = lambda g, c: (g, c)
    k_spec = _fft_forward([k[None] for k in k_halves], [lambda g, c: (0, c)] * 2, nfft, n, n_groups=1,
                          n_cblk=HYENA_ORDER * wblk, halves=True, name="hy_filt_fft")
    z, zblk = uc, 2 * wblk
    for o in range(HYENA_ORDER):
        sel_r = lambda g, c, zblk=zblk: (0, zblk + c)
        sel_i = lambda g, c, zblk=zblk: (1, zblk + c)
        spec = tuple(p[0] for p in k_spec) + (nrm,)
        spec_sel = lambda g, c, o=o: o * wblk + c
        gate = ((uc, o * wblk), (z, zblk), hy_bias[o:o + 1])
        common = dict(n_groups=1, n_cblk=wblk, real_out=False)
        if s == 1:
            gf = _dft_mats(nfft, n, nfft, -1.0, 1.0, 2, False)
            gi = _dft_mats(n, nfft, nfft, 1.0, 1.0 / nfft, 2, False)
            (z,) = _fft_stage([z, z], [sel_r, sel_i], gf, r_in=n, s_n=1, k_out=n, transposed_out=False,
                              g2mat=gi, spec=spec, spec_sel=spec_sel, gate=gate, name="hy_direct", **common)
        else:
            r1 = n // s
            g1 = _dft_mats(n1, r1, n1, -1.0, 1.0, 2, False)
            (a,) = _fft_stage([z, z], [sel_r, sel_i], g1, r_in=r1, s_n=s, k_out=n1, transposed_out=True,
                              tw=_twiddle(s, n1, nfft, -1.0, STAGE_ROWS), packed_out=True, name="hy_s1",
                              **common)
            g2 = _dft_mats(s, s, s, -1.0, 1.0, 2, False)
            g3 = _dft_mats(s, s, s, 1.0, 1.0, 2, False)
            (q,) = _fft_stage([a], [ident], g2, r_in=s, s_n=n1, k_out=s, transposed_out=True,
                              g2mat=g3, tw=_twiddle(n1, s, nfft, 1.0, STAGE_ROWS), spec=spec,
                              spec_sel=spec_sel, packed_in=True, packed_out=True, name="hy_mid", **common)
            g4 = _dft_mats(n // s, n1, n1, 1.0, 1.0 / nfft, 2, False)
            (z,) = _fft_stage([q], [ident], g4, r_in=n1, s_n=s, k_out=n // s, transposed_out=False,
                              gate=gate, packed_in=True, name="hy_last", **common)
        zblk = 0
    return z.reshape(b * n, w)


def _fnet(pq):
    b, n, _ = pq.shape
    w = FNET_WIDTH
    wblk = w // LANES
    first = 3 * HYENA_WIDTH // LANES
    scale = 1.0 / math.sqrt(n * FNET_GROUP_DIM)
    sel_r = lambda g, c: (g, first + c)
    sel_i = lambda g, c: (g, first + wblk + c)
    ident = lambda g, c: (g, c)
    n1, s = _split_len(n)
    if s == 1:
        g = _dft_mats(n, n, n, -1.0, scale, 2, True)
        (y,) = _fft_stage([pq, pq], [sel_r, sel_i], g, r_in=n, s_n=1, k_out=n, n_groups=b, n_cblk=wblk,
                          transposed_out=False, real_out=True, name="fnet_direct")
        return y.reshape(b * n, w)
    g1 = _dft_mats(n1, n1, n1, -1.0, 1.0, 2, False)
    tw = _twiddle(s, n1, n, -1.0, STAGE_ROWS)
    (a,) = _fft_stage([pq, pq], [sel_r, sel_i], g1, r_in=n1, s_n=s, k_out=n1, n_groups=b, n_cblk=wblk,
                      transposed_out=True, real_out=False, tw=tw, packed_out=True, name="fnet_s1")
    g2 = _dft_mats(s, s, s, -1.0, scale, 2, True)
    (y,) = _fft_stage([a], [ident], g2, r_in=s, s_n=n1, k_out=s, n_groups=b, n_cblk=wblk,
                      transposed_out=False, real_out=True, packed_in=True, name="fnet_s2")
    return y.reshape(b * n, w)


def _merge_kernel(x_ref, gt_ref, a_ref, h_ref, f_ref, gate_ref, wa_ref, wh_ref, wf_ref, wo_ref, o_ref):
    d = D_MODEL
    m = gate_ref[:, 0:d].astype(F32) * _dot(a_ref[...], wa_ref[...])
    m = m + gate_ref[:, d:2 * d].astype(F32) * _dot(h_ref[...].astype(BF), wh_ref[...])
    m = m + gate_ref[:, 2 * d:3 * d].astype(F32) * _dot(f_ref[...].astype(BF), wf_ref[...])
    y = _dot(m.astype(BF), wo_ref[...])
    o_ref[...] = x_ref[...] + gt_ref[0] * y


def _merge(x2, gt, attn_o, hy_o, fn_o, gates, wa, wh, wf, wo, *, tm, mod_row):
    t, d = x2.shape
    tok = lambda i: (i, 0)
    full = lambda i: (0, 0)
    return pl.pallas_call(
        _merge_kernel,
        grid=(t // tm,),
        in_specs=[
            pl.BlockSpec((tm, d), tok),
            pl.BlockSpec((1, 1, d), lambda i: (mod_row(i), 0, 0)),
            pl.BlockSpec((tm, ATTN_WIDTH), tok),
            pl.BlockSpec((tm, HYENA_WIDTH), tok),
            pl.BlockSpec((tm, FNET_WIDTH), tok),
            pl.BlockSpec((tm, 3 * d), tok),
            pl.BlockSpec(wa.shape, full, pipeline_mode=pl.Buffered(1)),
            pl.BlockSpec(wh.shape, full, pipeline_mode=pl.Buffered(1)),
            pl.BlockSpec(wf.shape, full, pipeline_mode=pl.Buffered(1)),
            pl.BlockSpec(wo.shape, full, pipeline_mode=pl.Buffered(1)),
        ],
        out_specs=pl.BlockSpec((tm, d), tok),
        out_shape=jax.ShapeDtypeStruct((t, d), F32),
        compiler_params=_cparams("parallel"),
        name="merge",
    )(x2, gt, attn_o, hy_o, fn_o, gates, wa, wh, wf, wo)


def _ffn_kernel(x_ref, sh_ref, sc_ref, gt_ref, g_ref, wg_ref, wu_ref, wd_ref, o_ref, h_scr, acc_scr):
    f = pl.program_id(1)

    @pl.when(f == 0)
    def _():
        h_scr[...] = _modulated_norm(x_ref[...], g_ref[...], sc_ref[0], sh_ref[0]).astype(BF)
        acc_scr[...] = jnp.zeros_like(acc_scr)

    hb = h_scr[...]
    act = _silu(_dot(hb, wg_ref[...])) * _dot(hb, wu_ref[...])
    acc_scr[...] += _dot(act.astype(BF), wd_ref[...])

    @pl.when(f == pl.num_programs(1) - 1)
    def _():
        o_ref[...] = x_ref[...] + gt_ref[0] * acc_scr[...]


def _ffn(x2, sh, sc, gt, g, wg, wu, wd, *, tm, tf, mod_row):
    t, d = x2.shape
    ff = wg.shape[1]
    row3 = lambda i, f: (mod_row(i), 0, 0)
    wmode = dict(pipeline_mode=pl.Buffered(1)) if tf == ff else {}
    return pl.pallas_call(
        _ffn_kernel,
        grid=(t // tm, ff // tf),
        in_specs=[
            pl.BlockSpec((tm, d), lambda i, f: (i, 0)),
            pl.BlockSpec((1, 1, d), row3), pl.BlockSpec((1, 1, d), row3), pl.BlockSpec((1, 1, d), row3),
            pl.BlockSpec((1, d), lambda i, f: (0, 0)),
            pl.BlockSpec((d, tf), lambda i, f: (0, f), **wmode),
            pl.BlockSpec((d, tf), lambda i, f: (0, f), **wmode),
            pl.BlockSpec((tf, d), lambda i, f: (f, 0), **wmode),
        ],
        out_specs=pl.BlockSpec((tm, d), lambda i, f: (i, 0)),
        out_shape=jax.ShapeDtypeStruct((t, d), F32),
        scratch_shapes=[pltpu.VMEM((tm, d), BF), pltpu.VMEM((tm, d), F32)],
        compiler_params=_cparams("parallel", "arbitrary"),
        name="ffn_dense",
    )(x2, sh, sc, gt, g, wg, wu, wd)


def _top2(logits):
    lane = lax.broadcasted_iota(jnp.int32, logits.shape, 1)
    lg = jnp.where(lane < N_EXPERTS, logits, -jnp.inf)
    m1 = jnp.max(lg, axis=1, keepdims=True)
    i1 = jnp.min(jnp.where(lg == m1, lane, LANES), axis=1, keepdims=True)
    lg2 = jnp.where(lane == i1, -jnp.inf, lg)
    m2 = jnp.max(lg2, axis=1, keepdims=True)
    i2 = jnp.min(jnp.where(lg2 == m2, lane, LANES), axis=1, keepdims=True)
    e2 = jnp.exp(m2 - m1)
    w1 = 1.0 / (1.0 + e2)
    return i1, i2, w1, e2 * w1


GROUP_TILE = 256
GROUP_PAD = 32
GROUP_PIECES = GROUP_TILE // GROUP_PAD
SLOT_RADIX = 64.0


def _moe_group_kernel(x_ref, sh_ref, sc_ref, g_ref, wr_ref, xg_ref, ws_ref, slot_ref, cnt_ref,
                      h_scr, rows_scr, wm_scr):
    j = pl.program_id(1)
    tb = x_ref.shape[0]
    gt_rows = xg_ref.shape[1]

    @pl.when(j == 0)
    def _():
        h = _modulated_norm(x_ref[...], g_ref[...], sc_ref[0], sh_ref[0])
        h_scr[...] = h.astype(BF)
        i1, i2, w1, w2 = _top2(_dot3(h, wr_ref[...]))
        lane = lax.broadcasted_iota(jnp.int32, (tb, LANES), 1)
        oh0 = jnp.where(lane == i1, 1.0, 0.0)
        oh1 = jnp.where(lane == i2, 1.0, 0.0)
        c0 = jnp.sum(oh0, axis=0, keepdims=True)
        cnt = c0 + jnp.sum(oh1, axis=0, keepdims=True)
        tri = jnp.where(lax.broadcasted_iota(jnp.int32, (tb, tb), 1)
                        < lax.broadcasted_iota(jnp.int32, (tb, tb), 0), 1.0, 0.0).astype(BF)
        pre0 = _dot(tri, oh0.astype(BF))
        pre1 = _dot(tri, oh1.astype(BF)) + c0
        tiles = jnp.ceil(cnt * (1.0 / GROUP_PAD))
        upper = jnp.where(lax.broadcasted_iota(jnp.int32, (LANES, LANES), 0)
                          < lax.broadcasted_iota(jnp.int32, (LANES, LANES), 1), 1.0, 0.0).astype(BF)
        off = _dot(jnp.broadcast_to(tiles, (8, LANES)).astype(BF), upper)[0:1] * float(GROUP_PAD)
        slot0 = jnp.sum(oh0 * (off + pre0), axis=1, keepdims=True)
        slot1 = jnp.sum(oh1 * (off + pre1), axis=1, keepdims=True)
        slot_ref[0] = jnp.where(lane == 0, slot0, jnp.where(lane == 1, slot1, 0.0))
        cnt_ref[0] = jnp.broadcast_to(cnt, (8, LANES))
        hi0 = jnp.floor(slot0 * (1.0 / SLOT_RADIX))
        hi1 = jnp.floor(slot1 * (1.0 / SLOT_RADIX))
        digits = jnp.where(lane == 0, hi0, jnp.where(lane == 1, slot0 - SLOT_RADIX * hi0,
                           jnp.where(lane == 2, hi1, jnp.where(lane == 3, slot1 - SLOT_RADIX * hi1, 0.0))))
        sel = jnp.where(lax.broadcasted_iota(jnp.int32, (8, LANES), 0)
                        == lax.broadcasted_iota(jnp.int32, (8, LANES), 1), 1.0, 0.0).astype(BF)
        rows_scr[...] = _dot_nt(sel, digits.astype(BF))
        w1h, w1l = _split(w1)
        w1m, w1l = _split(w1 - w1h.astype(F32))
        w2h, w2l = _split(w2)
        w2m, w2l = _split(w2 - w2h.astype(F32))
        cols = [w1h, w1m, w1l, w2h, w2m, w2l]
        wm = jnp.zeros((tb, LANES), F32)
        for li, col in enumerate(cols):
            wm = jnp.where(lane == li, col.astype(F32), wm)
        wm_scr[...] = wm.astype(BF)

    rows = rows_scr[...]
    s0 = rows[0:1] * SLOT_RADIX + rows[1:2]
    s1 = rows[2:3] * SLOT_RADIX + rows[3:4]
    pos = (lax.broadcasted_iota(jnp.int32, (gt_rows, tb), 0) + j * gt_rows).astype(F32)
    g0 = jnp.where(pos == s0, 1.0, 0.0).astype(BF)
    g1 = jnp.where(pos == s1, 1.0, 0.0).astype(BF)
    xg_ref[0] = _dot(g0 + g1, h_scr[...]).astype(BF)
    lane_w = lax.broadcasted_iota(jnp.int32, (gt_rows, LANES), 1)
    wsum = (jnp.where(lane_w < 3, _dot(g0, wm_scr[...]), 0.0)
            + jnp.where((lane_w >= 3) & (lane_w < 6), _dot(g1, wm_scr[...]), 0.0))
    ws_ref[0] = jnp.broadcast_to(jnp.sum(wsum, axis=1, keepdims=True), (gt_rows, LANES))


def _moe_group(x2, sh, sc, g, wr_pad, *, tb, nt, mod_row):
    t, d = x2.shape
    nb = t // tb
    row3 = lambda b, j: (mod_row(b), 0, 0)
    return pl.pallas_call(
        _moe_group_kernel,
        grid=(nb, nt),
        in_specs=[
            pl.BlockSpec((tb, d), lambda b, j: (b, 0)),
            pl.BlockSpec((1, 1, d), row3), pl.BlockSpec((1, 1, d), row3),
            pl.BlockSpec((1, d), lambda b, j: (0, 0)),
            pl.BlockSpec((d, LANES), lambda b, j: (0, 0)),
        ],
        out_specs=[
            pl.BlockSpec((1, GROUP_TILE, d), lambda b, j: (b * nt + j, 0, 0)),
            pl.BlockSpec((1, GROUP_TILE, LANES), lambda b, j: (b * nt + j, 0, 0)),
            pl.BlockSpec((1, tb, LANES), lambda b, j: (b, 0, 0)),
            pl.BlockSpec((1, 8, LANES), lambda b, j: (b, 0, 0)),
        ],
        out_shape=[
            jax.ShapeDtypeStruct((nb * nt, GROUP_TILE, d), BF),
            jax.ShapeDtypeStruct((nb * nt, GROUP_TILE, LANES), F32),
            jax.ShapeDtypeStruct((nb, tb, LANES), F32),
            jax.ShapeDtypeStruct((nb, 8, LANES), F32),
        ],
        scratch_shapes=[pltpu.VMEM((tb, d), BF), pltpu.VMEM((8, tb), F32), pltpu.VMEM((tb, LANES), BF)],
        compiler_params=_cparams("parallel", "arbitrary"),
        name="moe_group",
    )(x2, sh, sc, g, wr_pad)


def _moe_schedule(cnt, nh):
    np_ = GROUP_PIECES
    h = (cnt + GROUP_PAD - 1) // GROUP_PAD
    nb, ne = h.shape
    tot = h.sum(0)
    pairs = (tot + np_ - 1) // np_
    cum_p = jnp.cumsum(pairs)
    start_p = cum_p - pairs
    n_used = cum_p[-1]
    n_steps = (nb * nh + ne * (np_ - 1)) // np_
    q = jnp.minimum(jnp.arange(n_steps, dtype=jnp.int32), n_used - 1)
    e = jnp.sum(q[:, None] >= cum_p[None, :], axis=1).astype(jnp.int32)
    r = q - start_p[e]
    cum_b = jnp.cumsum(h, axis=0)
    first = jnp.cumsum(h, axis=1) - h

    def piece(idx):
        idx = jnp.minimum(idx, tot[e] - 1)
        blk = jnp.sum(idx[:, None] >= cum_b.T[e], axis=1).astype(jnp.int32)
        return blk * nh + first[blk, e] + idx - (cum_b[blk, e] - h[blk, e])

    x = jnp.arange(nh, dtype=jnp.int32)[None, :]
    ex = jnp.sum(x[:, :, None] >= jnp.cumsum(h, axis=1)[:, None, :], axis=2).astype(jnp.int32)
    exc = jnp.minimum(ex, ne - 1)
    g = jnp.take_along_axis(cum_b - h, exc, axis=1) + x - jnp.take_along_axis(first, exc, axis=1)
    loc = np_ * (start_p[exc] + g // np_) + g % np_
    loc = jnp.where(ex < ne, loc, loc[:, 0:1])
    pieces = jnp.concatenate([piece(np_ * r + k) for k in range(np_)]).astype(jnp.int32)
    return (pieces, e, n_used.astype(jnp.int32).reshape(1), loc.reshape(-1).astype(jnp.int32),
            ((h.sum(1) + np_ - 1) // np_).astype(jnp.int32))


def _moe_expert_kernel(pc_ref, exp_ref, nused_ref, *refs):
    np_ = GROUP_PIECES
    x_refs, w_refs = refs[:np_], refs[np_:2 * np_]
    wg_ref, wu_ref, wd_ref, y_ref = refs[2 * np_:]

    @pl.when(pl.program_id(0) < nused_ref[0])
    def _():
        x = jnp.concatenate([r[0] for r in x_refs], axis=0)
        act = _silu(_dot(x, wg_ref[0])) * _dot(x, wu_ref[0])
        y = _dot(act.astype(BF), wd_ref[0])
        w = jnp.concatenate([r[0] for r in w_refs], axis=0)
        y_ref[0] = (y * jnp.concatenate([w] * (y.shape[1] // LANES), axis=1)).astype(BF)


def _moe_experts(pieces, step_exp, n_used, xg, ws, wg, wu, wd):
    d = xg.shape[-1]
    ff = wg.shape[2]
    np_ = GROUP_PIECES
    n_steps = step_exp.shape[0]
    xh = xg.reshape(-1, GROUP_PAD, d)
    wh = ws.reshape(-1, GROUP_PAD, LANES)
    pc3 = lambda k: (lambda i, pc, se, nu: (pc[k * n_steps + i], 0, 0))
    exp3 = lambda i, pc, se, nu: (se[i], 0, 0)
    return pl.pallas_call(
        _moe_expert_kernel,
        grid_spec=pltpu.PrefetchScalarGridSpec(
            num_scalar_prefetch=3,
            grid=(n_steps,),
            in_specs=(
                [pl.BlockSpec((1, GROUP_PAD, d), pc3(k)) for k in range(np_)]
                + [pl.BlockSpec((1, GROUP_PAD, LANES), pc3(k)) for k in range(np_)]
                + [pl.BlockSpec((1, d, ff), exp3), pl.BlockSpec((1, d, ff), exp3), pl.BlockSpec((1, ff, d), exp3)]
            ),
            out_specs=pl.BlockSpec((1, GROUP_TILE, d), lambda i, pc, se, nu: (jnp.minimum(i, nu[0] - 1), 0, 0)),
        ),
        out_shape=jax.ShapeDtypeStruct((n_steps, GROUP_TILE, d), BF),
        compiler_params=_cparams("arbitrary"),
        name="moe_experts",
    )(pieces, step_exp, n_used, *([xh] * np_), *([wh] * np_), wg, wu, wd)


def _moe_combine_kernel(nt_ref, loc_ref, x_ref, gt_ref, slot_ref, *refs):
    y_refs, o_ref, acc_scr = refs[:GROUP_PIECES], refs[GROUP_PIECES], refs[GROUP_PIECES + 1]
    b = pl.program_id(0)
    j = pl.program_id(1)
    tb = x_ref.shape[0]

    @pl.when(j == 0)
    def _():
        acc_scr[...] = jnp.zeros_like(acc_scr)

    @pl.when(j < nt_ref[b])
    def _():
        sl = slot_ref[0]
        pos = (lax.broadcasted_iota(jnp.int32, (tb, GROUP_TILE), 1) + j * GROUP_TILE).astype(F32)
        p = jnp.where((pos == sl[:, 0:1]) | (pos == sl[:, 1:2]), 1.0, 0.0).astype(BF)
        acc_scr[...] += _dot(p, jnp.concatenate([r[0] for r in y_refs], axis=0))

    @pl.when(j == pl.num_programs(1) - 1)
    def _():
        o_ref[...] = x_ref[...] + gt_ref[0] * acc_scr[...]


def _moe_combine(ntiles_b, loc, x2, gt, slots, yg, *, tb, nt, mod_row):
    t, d = x2.shape
    nb = t // tb
    np_ = GROUP_PIECES
    yh = yg.reshape(-1, GROUP_PAD, d)

    def piece3(k):
        return lambda b, j, n, lc: (lc[(b * nt + jnp.minimum(j, n[b] - 1)) * np_ + k], 0, 0)

    return pl.pallas_call(
        _moe_combine_kernel,
        grid_spec=pltpu.PrefetchScalarGridSpec(
            num_scalar_prefetch=2,
            grid=(nb, nt),
            in_specs=[
                pl.BlockSpec((tb, d), lambda b, j, n, lc: (b, 0)),
                pl.BlockSpec((1, 1, d), lambda b, j, n, lc: (mod_row(b), 0, 0)),
                pl.BlockSpec((1, tb, LANES), lambda b, j, n, lc: (b, 0, 0)),
            ] + [pl.BlockSpec((1, GROUP_PAD, d), piece3(k)) for k in range(np_)],
            out_specs=pl.BlockSpec((tb, d), lambda b, j, n, lc: (b, 0)),
            scratch_shapes=[pltpu.VMEM((tb, d), F32)],
        ),
        out_shape=jax.ShapeDtypeStruct((t, d), F32),
        compiler_params=_cparams("parallel", "arbitrary"),
        name="moe_combine",
    )(ntiles_b, loc, x2, gt, slots, *([yh] * np_))


def _moe(x2, sh, sc, gt, g, wr_pad, wg, wu, wd, *, tb, mod_row):
    nt = -(-(2 * tb + N_EXPERTS * (GROUP_PAD - 1)) // GROUP_TILE)
    xg, ws, slots, cnt = _moe_group(x2, sh, sc, g, wr_pad, tb=tb, nt=nt, mod_row=mod_row)
    counts = cnt[:, 0, :N_EXPERTS].astype(jnp.int32)
    pieces, step_exp, n_used, loc, ntiles_b = _moe_schedule(counts, nt * GROUP_PIECES)
    yg = _moe_experts(pieces, step_exp, n_used, xg, ws, wg, wu, wd)
    return _moe_combine(ntiles_b, loc, x2, gt, slots, yg, tb=tb, nt=nt, mod_row=mod_row)


def _rope_tables(seq_len):
    pos = np.arange(seq_len)
    prow = (pos // GRID_W).astype(np.float32)
    pcol = (pos % GRID_W).astype(np.float32)
    n_freq = HEAD_DIM // 4
    inv = (np.float32(ROPE_THETA) ** (-np.arange(n_freq, dtype=np.float32) / n_freq)).astype(np.float32)
    ar = (prow[:, None] * inv[None, :]).astype(np.float64)
    ac = (pcol[:, None] * inv[None, :]).astype(np.float64)
    cos = np.concatenate([np.cos(ar)] * 2 + [np.cos(ac)] * 2, axis=1)
    sin = np.concatenate([-np.sin(ar), np.sin(ar), -np.sin(ac), np.sin(ac)], axis=1)
    return (jnp.asarray(np.concatenate([cos, cos], axis=1), dtype=F32),
            jnp.asarray(np.concatenate([sin, sin], axis=1), dtype=F32))


def _head_sum_matrix():
    c = np.arange(ATTN_WIDTH)
    return jnp.asarray((c[:, None] // HEAD_DIM) == (c[None, :] // HEAD_DIM), dtype=F32).astype(BF)


def _fnet_channel_matrix():
    c = np.arange(FNET_WIDTH)
    same = (c[:, None] // FNET_GROUP_DIM) == (c[None, :] // FNET_GROUP_DIM)
    ang = 2.0 * np.pi * (((c[:, None] % FNET_GROUP_DIM) * (c[None, :] % FNET_GROUP_DIM)) % FNET_GROUP_DIM) / FNET_GROUP_DIM
    cb = np.where(same, np.cos(ang), 0.0)
    sb = np.where(same, np.sin(ang), 0.0)
    return jnp.asarray(np.concatenate([cb, -sb], axis=1), dtype=F32).astype(BF)


def _filter_features(n):
    t = np.linspace(0.0, 1.0, n)[:, None]
    w = 2.0 * np.pi * np.arange(n)[:, None] / n
    fb = np.linspace(1e-4, FILTER_BANDS - 1, FILTER_BANDS)
    z = np.concatenate([t, np.cos(fb * w), -np.sin(fb * w)], axis=-1)
    return jnp.asarray(np.pad(z, ((0, 0), (0, 64 - z.shape[1]))), dtype=F32)


def _decay_rates():
    d = jnp.abs(jnp.linspace(math.log(DECAY_TARGET) / SLOW_DECAY_PCT, math.log(DECAY_TARGET) / FAST_DECAY_PCT,
                             HYENA_WIDTH, dtype=F32))
    return jnp.concatenate([d] * HYENA_ORDER)[None, :]


def kernel(x, c, ctx, c_ctx, w_ada, b_ada, norm1_g, norm2_g, w_in, q_norm_g, k_norm_g, attn_sink,
           hy_conv_w, hy_conv_b, hy_filt_w1, hy_filt_b1, hy_filt_freq1, hy_filt_w2, hy_filt_b2,
           hy_filt_freq2, hy_filt_w3, hy_bias, w_proj_attn, w_proj_hyena, w_proj_fnet, w_out,
           ffn_w_gate, ffn_w_up, ffn_w_down, moe_router, moe_w_gate, moe_w_up, moe_w_down):
    b, seq, d = x.shape
    n_ctx = ctx.shape[1]
    depth = w_ada.shape[0]
    tm = 512
    tiles_per_seq = seq // tm

    cond8 = jnp.concatenate([c, c_ctx[None, :], jnp.zeros((8 - b - 1, d), F32)], axis=0)
    mods = _adaln(cond8, w_ada, b_ada)

    cos_l, sin_l = _rope_tables(seq)
    cos_c = jnp.ones((n_ctx, LANES), F32)
    sin_c = jnp.zeros((n_ctx, LANES), F32)
    gsum = _head_sum_matrix()
    mfn = _fnet_channel_matrix()
    deltas = _decay_rates()
    zfeat_l = _filter_features(seq)
    zfeat_c = _filter_features(n_ctx)

    lat_row = lambda i: i // tiles_per_seq
    ctx_row = lambda i: b
    lat_tab = lambda i: i % tiles_per_seq
    ctx_tab = lambda i: 0
    tm_c = min(tm, n_ctx)

    xs = x.reshape(b * seq, d)
    cs = ctx.reshape(b * n_ctx, d)
    for l in range(depth):
        last = l == depth - 1
        mod = lambda j: mods[l, :, j * d:(j + 1) * d].reshape(8, 1, d)
        w_in_bf = w_in[l].astype(BF)
        qg = jnp.tile(q_norm_g[l], N_HEADS)[None, :]
        kg = jnp.tile(k_norm_g[l], N_KV_HEADS)[None, :]
        g1 = norm1_g[l][None, :]
        wa, wh, wf, wo = (w_proj_attn[l].astype(BF), w_proj_hyena[l].astype(BF),
                          w_proj_fnet[l].astype(BF), w_out[l].astype(BF))
        conv_w = hy_conv_w[l].reshape(3, -1)
        conv_b = hy_conv_b[l][None, :]
        w1p = jnp.pad(hy_filt_w1[l], ((0, 64 - hy_filt_w1.shape[1]), (0, 0)))
        filt = (w1p, hy_filt_b1[l][None, :], hy_filt_freq1[l][None, :], hy_filt_w2[l],
                hy_filt_b2[l][None, :], hy_filt_freq2[l][None, :], hy_filt_w3[l], deltas)

        qkv_c, upq_c, gates_c = _phase_a(
            cs, mod(0), mod(1), g1, w_in_bf, cos_c, sin_c, qg, kg, gsum, mfn, conv_w, conv_b,
            tm=tm_c, tiles_per_seq=n_ctx // tm_c, mod_row=ctx_row, tab_row=ctx_tab)
        qkv_c = qkv_c.reshape(b, n_ctx, -1)
        upq_c = upq_c.reshape(b, n_ctx, -1)

        qkv_l, upq_l, gates_l = _phase_a(
            xs, mod(0), mod(1), g1, w_in_bf, cos_l, sin_l, qg, kg, gsum, mfn, conv_w, conv_b,
            tm=tm, tiles_per_seq=tiles_per_seq, mod_row=lat_row, tab_row=lat_tab)
        upq_l = upq_l.reshape(b, seq, -1)
        attn_l = _attention(attn_sink[l], qkv_l.reshape(b, seq, -1), qkv_c, local=True, tq=512)
        kf_l, kb_l, nrm_l = _hy_filter(zfeat_l, *filt, tm=1024)
        hy_l = _hyena(upq_l, (kf_l, kb_l), nrm_l, hy_bias[l])
        fn_l = _fnet(upq_l)
        xs = _merge(xs, mod(2), attn_l.reshape(b * seq, -1), hy_l, fn_l, gates_l, wa, wh, wf, wo,
                    tm=tm, mod_row=lat_row)

        if not last:
            attn_c = _attention(attn_sink[l], qkv_c, qkv_c, local=False, tq=n_ctx)
            kf_c, kb_c, nrm_c = _hy_filter(zfeat_c, *filt, tm=n_ctx)
            hy_c = _hyena(upq_c, (kf_c, kb_c), nrm_c, hy_bias[l])
            fn_c = _fnet(upq_c)
            cs = _merge(cs, mod(2), attn_c.reshape(b * n_ctx, -1), hy_c, fn_c, gates_c, wa, wh, wf, wo,
                        tm=tm_c, mod_row=ctx_row)

        g2 = norm2_g[l][None, :]
        i = l // 2
        if l % 2 == 0:
            wg, wu, wd = ffn_w_gate[i].astype(BF), ffn_w_up[i].astype(BF), ffn_w_down[i].astype(BF)
            run = lambda t2, rows, tmm: _ffn(t2, mod(3), mod(4), mod(5), g2, wg, wu, wd,
                                             tm=tmm, tf=D_FF, mod_row=rows)
        else:
            wr = jnp.pad(moe_router[i], ((0, 0), (0, LANES - N_EXPERTS)))
            wg, wu, wd = moe_w_gate[i].astype(BF), moe_w_up[i].astype(BF), moe_w_down[i].astype(BF)
            run = lambda t2, rows, tmm: _moe(t2, mod(3), mod(4), mod(5), g2, wr, wg, wu, wd,
                                             tb=tmm, mod_row=rows)
        tm_ffn = 512 if l % 2 == 0 else 1024
        xs = run(xs, lambda t: t // (seq // tm_ffn), tm_ffn)
        if not last:
            cs = run(cs, ctx_row, min(tm_ffn, b * n_ctx))
    return xs.reshape(b, seq, d)
```

```python
import functools
import math

import numpy as np
import jax
import jax.numpy as jnp
from jax import lax
from jax.experimental import pallas as pl
from jax.experimental.pallas import tpu as pltpu

F32 = jnp.float32
BF = jnp.bfloat16

D_MODEL = 1024
DEPTH = 4
GRID_W = 64
HEAD_DIM = 64
N_HEADS = 8
N_KV_HEADS = 2
ATTN_WIDTH = N_HEADS * HEAD_DIM
KV_WIDTH = N_KV_HEADS * HEAD_DIM
WINDOW = 128
QBLK = 128
ROPE_THETA = 10000.0
HYENA_ORDER = 2
HYENA_WIDTH = 256
FILTER_BANDS = 16
FILTER_HIDDEN = 64
DECAY_TARGET = 1e-2
FAST_DECAY_PCT = 0.3
SLOW_DECAY_PCT = 1.5
FNET_WIDTH = 256
FNET_GROUP_DIM = 64
Q_END = ATTN_WIDTH
K_END = Q_END + KV_WIDTH
V_END = K_END + KV_WIDTH
HY_END = V_END + (HYENA_ORDER + 1) * HYENA_WIDTH
FN_END = HY_END + FNET_WIDTH
IN_WIDTH = FN_END + 3 * D_MODEL
D_FF = 2816
N_EXPERTS = 8
EPS = 1e-6
LANES = 128
NEG = -1e30
STAGE_ROWS = 8
STAGE_JOIN = 2

VMEM_LIMIT = 56 * 1024 * 1024


def _cparams(*sem):
    return pltpu.CompilerParams(dimension_semantics=sem, vmem_limit_bytes=VMEM_LIMIT)


def _dot(a, b):
    return jnp.dot(a, b, preferred_element_type=F32)


def _dot_nt(a, b):
    return lax.dot_general(a, b, (((1,), (1,)), ((), ())), preferred_element_type=F32)


def _split(a):
    hi = a.astype(BF)
    lo = (a - hi.astype(F32)).astype(BF)
    return hi, lo


def _dot3(a, b):
    ah, al = _split(a)
    bh, bl = _split(b)
    return _dot(ah, bh) + (_dot(ah, bl) + _dot(al, bh))


def _dot2(a, b_bf16):
    ah, al = _split(a)
    return _dot(ah, b_bf16) + _dot(al, b_bf16)


def _sigmoid(v):
    return 0.5 * jnp.tanh(0.5 * v) + 0.5


def _silu(v):
    return v * _sigmoid(v)


def _adaln_kernel(c_ref, w_ref, b_ref, o_ref):
    o_ref[0] = _dot3(_silu(c_ref[...]), w_ref[0]) + b_ref[0]


def _adaln(cond8, w_ada, b_ada):
    depth, d, n6 = w_ada.shape
    tn = 1024
    return pl.pallas_call(
        _adaln_kernel,
        grid=(depth, n6 // tn),
        in_specs=[
            pl.BlockSpec((8, d), lambda l, j: (0, 0)),
            pl.BlockSpec((1, d, tn), lambda l, j: (l, 0, j)),
            pl.BlockSpec((1, 1, tn), lambda l, j: (l, 0, j)),
        ],
        out_specs=pl.BlockSpec((1, 8, tn), lambda l, j: (l, 0, j)),
        out_shape=jax.ShapeDtypeStruct((depth, 8, n6), F32),
        compiler_params=_cparams("parallel", "parallel"),
        name="adaln",
    )(cond8, w_ada, b_ada.reshape(depth, 1, n6))


def _modulated_norm(x, g, sc, sh):
    ms = jnp.mean(x * x, axis=-1, keepdims=True)
    h = (x * lax.rsqrt(ms + EPS)) * g
    return h * (1.0 + sc) + sh


HALO = 16


def _phase_a_kernel(x_ref, xp_ref, xn_ref, sh_ref, sc_ref, g_ref, w_ref, cos_ref, sin_ref, qg_ref, kg_ref,
                    gsum_ref, mfn_ref, cw_ref, cb_ref, qkv_ref, upq_ref, gate_ref, *, tiles_per_seq):
    tm = x_ref.shape[0]
    norm = lambda xv: _modulated_norm(xv, g_ref[...], sc_ref[0], sh_ref[0]).astype(BF)
    hb = norm(x_ref[...])
    cos = cos_ref[...]
    sin = sin_ref[...]

    def headnorm(t, gain, gs):
        ss = _dot2(t * t, gs)
        return t * lax.rsqrt(ss * (1.0 / HEAD_DIM) + EPS) * gain

    def rope(t, cosw, sinw):
        w = t.shape[1]
        nxt = pltpu.roll(t, w - 16, axis=1)
        prv = pltpu.roll(t, 16, axis=1)
        lw = lax.broadcasted_iota(jnp.int32, t.shape, 1)
        return t * cosw + jnp.where((lw % 32) < 16, nxt, prv) * sinw

    def dup_halves(t):
        lane = lax.broadcasted_iota(jnp.int32, t.shape, 1)
        sw = pltpu.roll(t, 64, axis=1)
        lo = lane < 64
        return jnp.concatenate([jnp.where(lo, t, sw), jnp.where(lo, sw, t)], axis=1)

    pr = _dot(hb, w_ref[:, 0:FN_END])
    pq = pr[:, 0:Q_END]
    qn = headnorm(pq, qg_ref[...], gsum_ref[...])
    cos4 = jnp.concatenate([cos] * 4, axis=1)
    sin4 = jnp.concatenate([sin] * 4, axis=1)
    qkv_ref[:, 0:Q_END] = (rope(qn, cos4, sin4) * (HEAD_DIM ** -0.5)).astype(BF)

    kn = headnorm(pr[:, Q_END:K_END], kg_ref[...], gsum_ref[0:KV_WIDTH, 0:KV_WIDTH])
    qkv_ref[:, Q_END:Q_END + 2 * KV_WIDTH] = dup_halves(rope(kn, cos, sin)).astype(BF)
    qkv_ref[:, Q_END + 2 * KV_WIDTH:] = dup_halves(pr[:, K_END:V_END]).astype(BF)

    w_hy = w_ref[:, V_END:HY_END]
    u = pr[:, V_END:HY_END]
    tile = pl.program_id(0) % tiles_per_seq
    u_before = jnp.where(tile == 0, 0.0, _dot(norm(xp_ref[...]), w_hy)[HALO - 1:HALO])
    u_after = jnp.where(tile == tiles_per_seq - 1, 0.0, _dot(norm(xn_ref[...]), w_hy)[0:1])
    row = lax.broadcasted_iota(jnp.int32, (tm, 1), 0)
    prv = jnp.where(row == 0, u_before, pltpu.roll(u, 1, axis=0))
    nxt = jnp.where(row == tm - 1, u_after, pltpu.roll(u, tm - 1, axis=0))
    n_hy = HY_END - V_END
    upq_ref[:, 0:n_hy] = prv * cw_ref[0:1, :] + u * cw_ref[1:2, :] + nxt * cw_ref[2:3, :] + cb_ref[...]

    upq_ref[:, n_hy:] = _dot(pr[:, HY_END:FN_END].astype(BF), mfn_ref[...])
    gate_ref[...] = _sigmoid(_dot(hb, w_ref[:, FN_END:])).astype(BF)


def _phase_a(x2, sh, sc, g, w_in_bf, cos_t, sin_t, qg, kg, gsum, mfn, conv_w, conv_b, *, tm, tiles_per_seq,
             mod_row, tab_row):
    t, d = x2.shape
    row3 = lambda i: (mod_row(i), 0, 0)
    full = lambda i: (0, 0)
    tok = lambda i: (i, 0)
    n_halo = t // HALO
    outs = [
        ((t, ATTN_WIDTH + 4 * KV_WIDTH), BF), ((t, 3 * HYENA_WIDTH + 2 * FNET_WIDTH), F32),
        ((t, 3 * D_MODEL), BF),
    ]
    kern = functools.partial(_phase_a_kernel, tiles_per_seq=tiles_per_seq)
    return pl.pallas_call(
        kern,
        grid=(t // tm,),
        in_specs=[
            pl.BlockSpec((tm, d), tok),
            pl.BlockSpec((HALO, d), lambda i: (jnp.maximum(i * (tm // HALO) - 1, 0), 0)),
            pl.BlockSpec((HALO, d), lambda i: (jnp.minimum((i + 1) * (tm // HALO), n_halo - 1), 0)),
            pl.BlockSpec((1, 1, d), row3),
            pl.BlockSpec((1, 1, d), row3),
            pl.BlockSpec((1, d), full),
            pl.BlockSpec((d, IN_WIDTH), full, pipeline_mode=pl.Buffered(1)),
            pl.BlockSpec((tm, LANES), lambda i: (tab_row(i), 0)),
            pl.BlockSpec((tm, LANES), lambda i: (tab_row(i), 0)),
            pl.BlockSpec((1, ATTN_WIDTH), full),
            pl.BlockSpec((1, KV_WIDTH), full),
            pl.BlockSpec((ATTN_WIDTH, ATTN_WIDTH), full),
            pl.BlockSpec((FNET_WIDTH, 2 * FNET_WIDTH), full),
            pl.BlockSpec(conv_w.shape, full),
            pl.BlockSpec(conv_b.shape, full),
        ],
        out_specs=[pl.BlockSpec((tm, s[1]), tok) for s, _ in outs],
        out_shape=[jax.ShapeDtypeStruct(s, dt) for s, dt in outs],
        compiler_params=_cparams("parallel"),
        name="phase_a",
    )(x2, x2, x2, sh, sc, g, w_in_bf, cos_t, sin_t, qg, kg, gsum, mfn, conv_w, conv_b)


def _attn_kernel(sink_ref, q_ref, kd_ref, vd_ref, kc_ref, vc_ref, o_ref, *, local, seq_len):
    tq = q_ref.shape[1]
    nblk = tq // QBLK
    gq = N_HEADS // N_KV_HEADS
    rows = gq * QBLK
    lane = lax.broadcasted_iota(jnp.int32, (QBLK, LANES), 1)
    lo_half = lane < 64
    hrow = lax.broadcasted_iota(jnp.int32, (rows, 1), 0) // QBLK
    nband = 3 * QBLK
    if local:
        qk_off = (lax.broadcasted_iota(jnp.int32, (rows, nband), 0) % QBLK
                  - lax.broadcasted_iota(jnp.int32, (rows, nband), 1))
    for blk in range(nblk):
        r0 = blk * QBLK
        qb = q_ref[0, r0:r0 + QBLK, :]
        if local:
            n = pl.program_id(1) * nblk + blk
            start = pl.multiple_of(jnp.clip((n - 1) * QBLK, 0, seq_len - nband), QBLK)
            valid = jnp.abs(qk_off + (n * QBLK - start)) <= WINDOW
        for g in range(N_KV_HEADS):
            parts = []
            for hh in range(gq):
                h = gq * g + hh
                qc = qb[:, (h // 2) * LANES:(h // 2 + 1) * LANES]
                keep = lo_half if h % 2 == 0 else jnp.logical_not(lo_half)
                parts.append(jnp.where(keep, qc, jnp.zeros_like(qc)))
            q4 = jnp.concatenate(parts, axis=0)
            sk = jnp.full((rows, 1), sink_ref[gq * g + gq - 1], F32)
            for hh in range(gq - 2, -1, -1):
                sk = jnp.where(hrow == hh, sink_ref[gq * g + hh], sk)
            gl = slice(g * LANES, (g + 1) * LANES)
            s_ctx = _dot_nt(q4, kc_ref[0, :, gl])
            m = jnp.maximum(jnp.max(s_ctx, axis=1, keepdims=True), sk)
            if local:
                s_loc = _dot_nt(q4, kd_ref[0, pl.ds(start, nband), gl])
                s_loc = jnp.where(valid, s_loc, NEG)
                m = jnp.maximum(m, jnp.max(s_loc, axis=1, keepdims=True))
            p_ctx = jnp.exp(s_ctx - m)
            den = jnp.sum(p_ctx, axis=1, keepdims=True) + jnp.exp(sk - m)
            o = _dot(p_ctx.astype(BF), vc_ref[0, :, gl])
            if local:
                p_loc = jnp.exp(s_loc - m)
                den = den + jnp.sum(p_loc, axis=1, keepdims=True)
                o = o + _dot(p_loc.astype(BF), vd_ref[0, pl.ds(start, nband), gl])
            o = o / den
            for cc in range(gq // 2):
                col = (gq // 2) * g + cc
                oa = o[(2 * cc) * QBLK:(2 * cc + 1) * QBLK]
                ob = o[(2 * cc + 1) * QBLK:(2 * cc + 2) * QBLK]
                o_ref[0, r0:r0 + QBLK, col * LANES:(col + 1) * LANES] = (
                    jnp.where(lo_half, oa, ob).astype(BF))


def _attention(sink, qkv, qkv_ctx, *, local, tq):
    b, lq, _ = qkv.shape
    c = qkv_ctx.shape[1]
    kw = 2 * KV_WIDTH
    k_blk, v_blk = ATTN_WIDTH // kw, ATTN_WIDTH // kw + 1
    kern = functools.partial(_attn_kernel, local=local, seq_len=lq)
    return pl.pallas_call(
        kern,
        grid=(b, lq // tq),
        in_specs=[
            pl.BlockSpec(memory_space=pltpu.SMEM),
            pl.BlockSpec((1, tq, ATTN_WIDTH), lambda bi, i: (bi, i, 0)),
            pl.BlockSpec((1, lq, kw), lambda bi, i: (bi, 0, k_blk)),
            pl.BlockSpec((1, lq, kw), lambda bi, i: (bi, 0, v_blk)),
            pl.BlockSpec((1, c, kw), lambda bi, i: (bi, 0, k_blk)),
            pl.BlockSpec((1, c, kw), lambda bi, i: (bi, 0, v_blk)),
        ],
        out_specs=pl.BlockSpec((1, tq, ATTN_WIDTH), lambda bi, i: (bi, i, 0)),
        out_shape=jax.ShapeDtypeStruct((b, lq, ATTN_WIDTH), BF),
        compiler_params=_cparams("parallel", "parallel"),
        name="attn_local" if local else "attn_ctx",
    )(sink, qkv, qkv, qkv, qkv_ctx, qkv_ctx)


FILTER_HALO = 128


def _hy_filter_kernel(z_ref, zn_ref, w1_ref, b1_ref, f1_ref, w2_ref, b2_ref, f2_ref, w3f_ref, w3b_ref, dl_ref,
                      flip_ref, kf_ref, kb_ref, nrm_ref):
    i = pl.program_id(0)
    tm = z_ref.shape[0]
    rows = tm + FILTER_HALO
    z = jnp.concatenate([z_ref[...], zn_ref[...]], axis=0)
    h = jnp.sin(f1_ref[...] * (_dot3(z, w1_ref[...]) + b1_ref[...]))
    h = jnp.sin(f2_ref[...] * (_dot3(h, w2_ref[...]) + b2_ref[...]))
    tcol = jnp.where(lax.broadcasted_iota(jnp.int32, (rows, LANES - FILTER_HIDDEN), 1) == 0, z[:, 0:1], 0.0)
    hid = jnp.concatenate([h, tcol], axis=1)

    def taps(hv, w3_ref):
        return _dot3(hv, w3_ref[...]) * jnp.exp(-hv[:, FILTER_HIDDEN:FILTER_HIDDEN + 1] * dl_ref[...])

    kf = taps(hid[0:tm], w3f_ref)
    flip = flip_ref[...]
    h1 = hid.astype(BF)
    r1 = hid - h1.astype(F32)
    h2 = r1.astype(BF)
    h3 = (r1 - h2.astype(F32)).astype(BF)
    hid_rev = _dot(flip, h1) + (_dot(flip, h2) + _dot(flip, h3))
    kb = taps(hid_rev, w3b_ref)
    out_row = (pl.num_programs(0) - 1 - i) * tm + lax.broadcasted_iota(jnp.int32, (tm, 1), 0)
    kb = jnp.where(out_row == 0, 0.0, kb)
    kf_ref[...] = kf
    kb_ref[...] = kb

    @pl.when(i == 0)
    def _():
        nrm_ref[...] = jnp.zeros_like(nrm_ref)

    nrm_ref[...] += jnp.sum(jnp.abs(kf), axis=0, keepdims=True) + jnp.sum(jnp.abs(kb), axis=0, keepdims=True)


def _hy_filter(zfeat, w1p, b1, f1, w2, b2, f2, w3, deltas2, *, tm):
    n = zfeat.shape[0]
    nb = n // tm
    wout = w3.shape[1] // 2
    w3p = jnp.pad(w3, ((0, LANES - w3.shape[0]), (0, 0)))
    p = np.arange(tm)[:, None]
    flip = jnp.asarray(np.arange(tm + FILTER_HALO)[None, :] == tm - p, dtype=F32).astype(BF)
    full = lambda i: (0, 0)
    halo_blk = lambda i: (jnp.minimum((i + 1) * (tm // FILTER_HALO), n // FILTER_HALO - 1), 0)
    return pl.pallas_call(
        _hy_filter_kernel,
        grid=(nb,),
        in_specs=[
            pl.BlockSpec((tm, zfeat.shape[1]), lambda i: (i, 0)),
            pl.BlockSpec((FILTER_HALO, zfeat.shape[1]), halo_blk),
            pl.BlockSpec(w1p.shape, full), pl.BlockSpec(b1.shape, full), pl.BlockSpec(f1.shape, full),
            pl.BlockSpec(w2.shape, full), pl.BlockSpec(b2.shape, full), pl.BlockSpec(f2.shape, full),
            pl.BlockSpec((LANES, wout), lambda i: (0, 0)),
            pl.BlockSpec((LANES, wout), lambda i: (0, 1)),
            pl.BlockSpec(deltas2.shape, full),
            pl.BlockSpec(flip.shape, full, pipeline_mode=pl.Buffered(1)),
        ],
        out_specs=[pl.BlockSpec((tm, wout), lambda i: (i, 0)), pl.BlockSpec((tm, wout), lambda i: (nb - 1 - i, 0)),
                   pl.BlockSpec((1, wout), full)],
        out_shape=[jax.ShapeDtypeStruct((n, wout), F32), jax.ShapeDtypeStruct((n, wout), F32),
                   jax.ShapeDtypeStruct((1, wout), F32)],
        compiler_params=_cparams("arbitrary"),
        name="hy_filter",
    )(zfeat, zfeat, w1p, b1, f1, w2, b2, f2, w3p, w3p, deltas2, flip)


def _dft_mats(k_out, r_in, period, sign, scale, n_in, real_out):
    k = np.arange(k_out)[:, None]
    r = np.arange(r_in)[None, :]
    ang = 2.0 * np.pi * ((k * r) % period) / period
    fr = np.cos(ang) * scale
    fi = sign * np.sin(ang) * scale
    if real_out:
        mats = [fr, -fi]
    else:
        mats = [np.concatenate([fr, fi], 0), np.concatenate([-fi, fr], 0)]
    return jnp.asarray(np.stack(mats[:n_in], 0), dtype=F32).astype(BF)


def _twiddle(s_n, k_n, n, sign, sbk):
    s0 = lax.broadcasted_iota(jnp.int32, (s_n // sbk, k_n, LANES), 0) * sbk
    k = lax.broadcasted_iota(jnp.int32, (s_n // sbk, k_n, LANES), 1)
    ang = (s0 * k).astype(F32) * (2.0 * math.pi / n)
    ang1 = lax.broadcasted_iota(jnp.int32, (k_n, LANES), 0).astype(F32) * (2.0 * math.pi / n)
    return jnp.cos(ang), sign * jnp.sin(ang), jnp.cos(ang1), sign * jnp.sin(ang1)


def _unpack_pair(p):
    return [pltpu.unpack_elementwise(p, index=i, packed_dtype=BF, unpacked_dtype=F32) for i in (0, 1)]


def _stage_kernel(*refs, n_in, r_in, k_mid, k_out, sbk, tw, spec, second, gate, real_out,
                  transposed_out, flat, packed_in, packed_spec, packed_out):
    it = iter(refs)
    x_refs = [next(it) for _ in range(n_in)]
    g_ref = next(it)
    g2_ref = next(it) if second else None
    tw_refs = [next(it) for _ in range(4)] if tw else None
    n_spec = 1 if packed_spec else 2
    spec_refs = [next(it) for _ in range(n_spec + 1)] if spec else None
    gate_refs = [next(it) for _ in range(5)] if gate else None
    out_refs = [next(it)] if (real_out or gate or packed_out) else [next(it), next(it)]
    if not flat:
        x_refs = [r.reshape(r_in * sbk, LANES) for r in x_refs]
        if spec:
            spec_refs = [r.reshape(k_mid * sbk, LANES) for r in spec_refs[:n_spec]] + spec_refs[n_spec:]
        if gate:
            gate_refs = [r.reshape(k_out * sbk, LANES) for r in gate_refs[:4]] + gate_refs[4:]
            out_refs = [out_refs[0].reshape(2 * k_out * sbk, LANES)]
        elif not transposed_out:
            out_refs = [r.reshape(k_out * sbk, LANES) for r in out_refs]
    if spec:
        inv = 1.0 / spec_refs[n_spec][...]
    if tw:
        tr, ti = tw_refs[0][0], tw_refs[1][0]
        wr, wi = tw_refs[2][...], tw_refs[3][...]
    def joined_dots(mats_ref, cols):
        acc = None
        for xi in range(len(cols[0])):
            wide = jnp.concatenate([c[xi].astype(BF) for c in cols], axis=1)
            d = _dot(mats_ref[xi], wide)
            acc = d if acc is None else acc + d
        return [acc[:, k * LANES:(k + 1) * LANES] for k in range(len(cols))]

    join = 1 if flat else min(STAGE_JOIN, sbk)
    results = {}
    for j in range(sbk):
        if j % join == 0:
            cols = []
            for jj in range(j, j + join):
                parts = [x_ref[...] if flat else x_ref[pl.ds(jj, r_in, stride=sbk), :] for x_ref in x_refs]
                cols.append(_unpack_pair(parts[0]) if packed_in else parts)
            accs = joined_dots(g_ref, cols)
            if not real_out:
                mids = []
                for jj, acc in zip(range(j, j + join), accs):
                    yr, yi = acc[:k_mid], acc[k_mid:]
                    if spec:
                        rows = slice(None) if flat else pl.ds(jj, k_mid, stride=sbk)
                        if packed_spec:
                            sr, si = _unpack_pair(spec_refs[0][rows, :])
                        else:
                            sr, si = spec_refs[0][rows, :], spec_refs[1][rows, :]
                        sr, si = sr * inv, si * inv
                        yr, yi = yr * sr - yi * si, yr * si + yi * sr
                    mids.append([yr, yi])
                if second:
                    mids = [[acc[:k_out], acc[k_out:]] for acc in joined_dots(g2_ref, mids)]
                accs = mids
            results = dict(zip(range(j, j + join), accs))
        if real_out:
            ys = [results[j]]
        else:
            yr, yi = results[j]
            if tw:
                yr, yi = yr * tr - yi * ti, yr * ti + yi * tr
                if j + 1 < sbk:
                    tr, ti = tr * wr - ti * wi, tr * wi + ti * wr
            ys = [yr, yi]
        if gate:
            o_ref = out_refs[0]
            for part, y in enumerate(ys):
                rows = slice(None) if flat else pl.ds(j, k_out, stride=sbk)
                val = gate_refs[part][rows, :] * (y + gate_refs[4][...] * gate_refs[2 + part][rows, :])
                if flat:
                    o_ref[part] = val.astype(o_ref.dtype)
                else:
                    o_ref[pl.ds(part * k_out * sbk + j, k_out, stride=sbk), :] = val.astype(o_ref.dtype)
            continue
        if packed_out:
            ys = [pltpu.pack_elementwise(ys, packed_dtype=BF)]
        for o_ref, y in zip(out_refs, ys):
            if flat:
                o_ref[...] = y.astype(o_ref.dtype)
            elif transposed_out:
                o_ref[0, j] = y.astype(o_ref.dtype)
            else:
                o_ref[pl.ds(j, k_out, stride=sbk), :] = y.astype(o_ref.dtype)


def _fft_stage(xs, x_sel, gmat, *, r_in, s_n, k_out, n_groups, n_cblk, transposed_out, real_out,
               out_dtype=F32, g2mat=None, tw=None, spec=None, spec_sel=None, gate=None, sbk=STAGE_ROWS,
               packed_in=False, packed_out=False, name="fft_stage"):
    n_in = len(xs)
    flat = s_n == 1
    sbk = 1 if flat else min(sbk, s_n)
    cb = LANES
    in_specs, args = [], []
    for x, sel in zip(xs, x_sel):
        if flat:
            in_specs.append(pl.BlockSpec((None, r_in, cb), lambda s, g, c, sel=sel: (sel(g, c)[0], 0, sel(g, c)[1])))
            args.append(x)
        else:
            xv = x.reshape(x.shape[0], x.shape[1] // s_n, s_n, x.shape[2])
            in_specs.append(pl.BlockSpec((1, r_in, sbk, cb),
                                         lambda s, g, c, sel=sel: (sel(g, c)[0], 0, s, sel(g, c)[1])))
            args.append(xv)
    in_specs.append(pl.BlockSpec(gmat.shape, lambda s, g, c: (0, 0, 0)))
    args.append(gmat)
    k_mid = gmat.shape[1] // (1 if real_out else 2)
    if g2mat is not None:
        in_specs.append(pl.BlockSpec(g2mat.shape, lambda s, g, c: (0, 0, 0)))
        args.append(g2mat)
    if tw is not None:
        for tarr in tw[:2]:
            in_specs.append(pl.BlockSpec((1, k_out, LANES), lambda s, g, c: (s, 0, 0)))
            args.append(tarr)
        for tarr in tw[2:]:
            in_specs.append(pl.BlockSpec((k_out, LANES), lambda s, g, c: (0, 0)))
            args.append(tarr)
    if spec is not None:
        *planes, nrm = spec
        for arr in planes:
            if flat:
                in_specs.append(pl.BlockSpec((k_mid, cb), lambda s, g, c: (0, spec_sel(g, c))))
                args.append(arr)
            else:
                in_specs.append(pl.BlockSpec((1, k_mid, sbk, cb), lambda s, g, c: (0, 0, s, spec_sel(g, c))))
                args.append(arr.reshape(1, k_mid, s_n, arr.shape[-1]))
        in_specs.append(pl.BlockSpec((1, cb), lambda s, g, c: (0, spec_sel(g, c))))
        args.append(nrm)
    if gate is not None:
        (ga, gblk), (za, zblk), bias = gate
        for arr, blk in ((ga, gblk), (za, zblk)):
            for bi in (0, 1):
                if flat:
                    in_specs.append(pl.BlockSpec((None, k_out, cb), lambda s, g, c, bi=bi, blk=blk: (bi, 0, blk + c)))
                    args.append(arr)
                else:
                    in_specs.append(pl.BlockSpec((1, k_out, sbk, cb),
                                                 lambda s, g, c, bi=bi, blk=blk: (bi, 0, s, blk + c)))
                    args.append(arr.reshape(arr.shape[0], k_out, s_n, arr.shape[-1]))
        in_specs.append(pl.BlockSpec((1, cb), lambda s, g, c: (0, c)))
        args.append(bias)
    ctot = n_cblk * cb
    if gate is not None:
        n_groups = 2
        if flat:
            oshape = (2, k_out, ctot)
            ospec = pl.BlockSpec((2, k_out, cb), lambda s, g, c: (0, 0, c))
        else:
            oshape = (2, k_out, s_n, ctot)
            ospec = pl.BlockSpec((2, k_out, sbk, cb), lambda s, g, c: (0, 0, s, c))
    elif flat:
        oshape = (n_groups, k_out, ctot)
        ospec = pl.BlockSpec((None, k_out, cb), lambda s, g, c: (g, 0, c))
    elif transposed_out:
        oshape = (n_groups, s_n, k_out, ctot)
        ospec = pl.BlockSpec((1, sbk, k_out, cb), lambda s, g, c: (g, s, 0, c))
    else:
        oshape = (n_groups, k_out, s_n, ctot)
        ospec = pl.BlockSpec((1, k_out, sbk, cb), lambda s, g, c: (g, 0, s, c))
    n_out = 1 if (real_out or gate is not None or packed_out) else 2
    if packed_out:
        out_dtype = jnp.int32
    kern = functools.partial(_stage_kernel, n_in=n_in, r_in=r_in, k_mid=k_mid, k_out=k_out, sbk=sbk,
                             tw=tw is not None, spec=spec is not None, second=g2mat is not None,
                             gate=gate is not None, real_out=real_out, transposed_out=transposed_out,
                             flat=flat, packed_in=packed_in, packed_spec=spec is not None and len(spec) == 2,
                             packed_out=packed_out)
    n_grid_groups = 1 if gate is not None else n_groups
    outs = pl.pallas_call(
        kern,
        grid=(s_n // sbk, n_grid_groups, n_cblk),
        in_specs=in_specs,
        out_specs=[ospec] * n_out,
        out_shape=[jax.ShapeDtypeStruct(oshape, out_dtype)] * n_out,
        compiler_params=_cparams("parallel", "parallel", "parallel"),
        name=name,
    )(*args)
    return [o.reshape(n_groups, -1, ctot) for o in outs]


def _split_len(n):
    if n <= 1024:
        return n, 1
    s = 128
    return n // s, s


def _fft_forward(xs, x_sel, n, n_rows, *, n_groups, n_cblk, halves=False, name="fwd"):
    n1, s = _split_len(n)
    n_in = len(xs)

    def mats(k_out, r_in, period):
        if not halves:
            return _dft_mats(k_out, r_in, period, -1.0, 1.0, n_in, False)
        g = _dft_mats(k_out, 2 * r_in, period, -1.0, 1.0, 1, False)[0]
        return jnp.stack([g[:, :r_in], g[:, r_in:]], axis=0)

    if s == 1:
        g = mats(n, n_rows, n)
        return _fft_stage(xs, x_sel, g, r_in=n_rows, s_n=1, k_out=n, n_groups=n_groups, n_cblk=n_cblk,
                          transposed_out=False, real_out=False, name=name + "_direct")
    r1 = n_rows // s
    g1 = mats(n1, r1, n1)
    tw = _twiddle(s, n1, n, -1.0, STAGE_ROWS)
    (a,) = _fft_stage(xs, x_sel, g1, r_in=r1, s_n=s, k_out=n1, n_groups=n_groups, n_cblk=n_cblk,
                      transposed_out=True, real_out=False, tw=tw, packed_out=True, name=name + "_s1")
    g2 = _dft_mats(s, s, s, -1.0, 1.0, 2, False)
    return _fft_stage([a], [lambda g, c: (g, c)], g2, r_in=s, s_n=n1, k_out=s, n_groups=n_groups,
                      n_cblk=n_cblk, transposed_out=False, real_out=False, packed_in=True, packed_out=True,
                      name=name + "_s2")


def _hyena(uc, k_halves, nrm, hy_bias):
    b, n, _ = uc.shape
    w = HYENA_WIDTH
    wblk = w // LANES
    nfft = 2 * n
    n1, s = _split_len(nfft)
    ident = lambda g, c: (g, c)
    k_spec = _fft_forward([k[None] for k in k_halves], [lambda g, c: (0, c)] * 2, nfft, n, n_groups=1,
                          n_cblk=HYENA_ORDER * wblk, halves=True, name="hy_filt_fft")
    z, zblk = uc, 2 * wblk
    for o in range(HYENA_ORDER):
        sel_r = lambda g, c, zblk=zblk: (0, zblk + c)
        sel_i = lambda g, c, zblk=zblk: (1, zblk + c)
        spec = tuple(p[0] for p in k_spec) + (nrm,)
        spec_sel = lambda g, c, o=o: o * wblk + c
        gate = ((uc, o * wblk), (z, zblk), hy_bias[o:o + 1])
        common = dict(n_groups=1, n_cblk=wblk, real_out=False)
        if s == 1:
            gf = _dft_mats(nfft, n, nfft, -1.0, 1.0, 2, False)
            gi = _dft_mats(n, nfft, nfft, 1.0, 1.0 / nfft, 2, False)
            (z,) = _fft_stage([z, z], [sel_r, sel_i], gf, r_in=n, s_n=1, k_out=n, transposed_out=False,
                              g2mat=gi, spec=spec, spec_sel=spec_sel, gate=gate, name="hy_direct", **common)
        else:
            r1 = n // s
            g1 = _dft_mats(n1, r1, n1, -1.0, 1.0, 2, False)
            (a,) = _fft_stage([z, z], [sel_r, sel_i], g1, r_in=r1, s_n=s, k_out=n1, transposed_out=True,
                              tw=_twiddle(s, n1, nfft, -1.0, STAGE_ROWS), packed_out=True, name="hy_s1",
                              **common)
            g2 = _dft_mats(s, s, s, -1.0, 1.0, 2, False)
            g3 = _dft_mats(s, s, s, 1.0, 1.0, 2, False)
            (q,) = _fft_stage([a], [ident], g2, r_in=s, s_n=n1, k_out=s, transposed_out=True,
                              g2mat=g3, tw=_twiddle(n1, s, nfft, 1.0, STAGE_ROWS), spec=spec,
                              spec_sel=spec_sel, packed_in=True, packed_out=True, name="hy_mid", **common)
            g4 = _dft_mats(n // s, n1, n1, 1.0, 1.0 / nfft, 2, False)
            (z,) = _fft_stage([q], [ident], g4, r_in=n1, s_n=s, k_out=n // s, transposed_out=False,
                              gate=gate, packed_in=True, name="hy_last", **common)
        zblk = 0
    return z.reshape(b * n, w)


def _fnet(pq):
    b, n, _ = pq.shape
    w = FNET_WIDTH
    wblk = w // LANES
    first = 3 * HYENA_WIDTH // LANES
    scale = 1.0 / math.sqrt(n * FNET_GROUP_DIM)
    sel_r = lambda g, c: (g, first + c)
    sel_i = lambda g, c: (g, first + wblk + c)
    ident = lambda g, c: (g, c)
    n1, s = _split_len(n)
    if s == 1:
        g = _dft_mats(n, n, n, -1.0, scale, 2, True)
        (y,) = _fft_stage([pq, pq], [sel_r, sel_i], g, r_in=n, s_n=1, k_out=n, n_groups=b, n_cblk=wblk,
                          transposed_out=False, real_out=True, name="fnet_direct")
        return y.reshape(b * n, w)
    g1 = _dft_mats(n1, n1, n1, -1.0, 1.0, 2, False)
    tw = _twiddle(s, n1, n, -1.0, STAGE_ROWS)
    (a,) = _fft_stage([pq, pq], [sel_r, sel_i], g1, r_in=n1, s_n=s, k_out=n1, n_groups=b, n_cblk=wblk,
                      transposed_out=True, real_out=False, tw=tw, packed_out=True, name="fnet_s1")
    g2 = _dft_mats(s, s, s, -1.0, scale, 2, True)
    (y,) = _fft_stage([a], [ident], g2, r_in=s, s_n=n1, k_out=s, n_groups=b, n_cblk=wblk,
                      transposed_out=False, real_out=True, packed_in=True, name="fnet_s2")
    return y.reshape(b * n, w)


def _merge_kernel(x_ref, gt_ref, a_ref, h_ref, f_ref, gate_ref, wa_ref, wh_ref, wf_ref, wo_ref, o_ref):
    d = D_MODEL
    m = gate_ref[:, 0:d].astype(F32) * _dot(a_ref[...], wa_ref[...])
    m = m + gate_ref[:, d:2 * d].astype(F32) * _dot(h_ref[...].astype(BF), wh_ref[...])
    m = m + gate_ref[:, 2 * d:3 * d].astype(F32) * _dot(f_ref[...].astype(BF), wf_ref[...])
    y = _dot(m.astype(BF), wo_ref[...])
    o_ref[...] = x_ref[...] + gt_ref[0] * y


def _merge(x2, gt, attn_o, hy_o, fn_o, gates, wa, wh, wf, wo, *, tm, mod_row):
    t, d = x2.shape
    tok = lambda i: (i, 0)
    full = lambda i: (0, 0)
    return pl.pallas_call(
        _merge_kernel,
        grid=(t // tm,),
        in_specs=[
            pl.BlockSpec((tm, d), tok),
            pl.BlockSpec((1, 1, d), lambda i: (mod_row(i), 0, 0)),
            pl.BlockSpec((tm, ATTN_WIDTH), tok),
            pl.BlockSpec((tm, HYENA_WIDTH), tok),
            pl.BlockSpec((tm, FNET_WIDTH), tok),
            pl.BlockSpec((tm, 3 * d), tok),
            pl.BlockSpec(wa.shape, full, pipeline_mode=pl.Buffered(1)),
            pl.BlockSpec(wh.shape, full, pipeline_mode=pl.Buffered(1)),
            pl.BlockSpec(wf.shape, full, pipeline_mode=pl.Buffered(1)),
            pl.BlockSpec(wo.shape, full, pipeline_mode=pl.Buffered(1)),
        ],
        out_specs=pl.BlockSpec((tm, d), tok),
        out_shape=jax.ShapeDtypeStruct((t, d), F32),
        compiler_params=_cparams("parallel"),
        name="merge",
    )(x2, gt, attn_o, hy_o, fn_o, gates, wa, wh, wf, wo)


def _ffn_kernel(x_ref, sh_ref, sc_ref, gt_ref, g_ref, wg_ref, wu_ref, wd_ref, o_ref, h_scr, acc_scr):
    f = pl.program_id(1)

    @pl.when(f == 0)
    def _():
        h_scr[...] = _modulated_norm(x_ref[...], g_ref[...], sc_ref[0], sh_ref[0]).astype(BF)
        acc_scr[...] = jnp.zeros_like(acc_scr)

    hb = h_scr[...]
    act = _silu(_dot(hb, wg_ref[...])) * _dot(hb, wu_ref[...])
    acc_scr[...] += _dot(act.astype(BF), wd_ref[...])

    @pl.when(f == pl.num_programs(1) - 1)
    def _():
        o_ref[...] = x_ref[...] + gt_ref[0] * acc_scr[...]


def _ffn(x2, sh, sc, gt, g, wg, wu, wd, *, tm, tf, mod_row):
    t, d = x2.shape
    ff = wg.shape[1]
    row3 = lambda i, f: (mod_row(i), 0, 0)
    wmode = dict(pipeline_mode=pl.Buffered(1)) if tf == ff else {}
    return pl.pallas_call(
        _ffn_kernel,
        grid=(t // tm, ff // tf),
        in_specs=[
            pl.BlockSpec((tm, d), lambda i, f: (i, 0)),
            pl.BlockSpec((1, 1, d), row3), pl.BlockSpec((1, 1, d), row3), pl.BlockSpec((1, 1, d), row3),
            pl.BlockSpec((1, d), lambda i, f: (0, 0)),
            pl.BlockSpec((d, tf), lambda i, f: (0, f), **wmode),
            pl.BlockSpec((d, tf), lambda i, f: (0, f), **wmode),
            pl.BlockSpec((tf, d), lambda i, f: (f, 0), **wmode),
        ],
        out_specs=pl.BlockSpec((tm, d), lambda i, f: (i, 0)),
        out_shape=jax.ShapeDtypeStruct((t, d), F32),
        scratch_shapes=[pltpu.VMEM((tm, d), BF), pltpu.VMEM((tm, d), F32)],
        compiler_params=_cparams("parallel", "arbitrary"),
        name="ffn_dense",
    )(x2, sh, sc, gt, g, wg, wu, wd)


def _top2(logits):
    lane = lax.broadcasted_iota(jnp.int32, logits.shape, 1)
    lg = jnp.where(lane < N_EXPERTS, logits, -jnp.inf)
    m1 = jnp.max(lg, axis=1, keepdims=True)
    i1 = jnp.min(jnp.where(lg == m1, lane, LANES), axis=1, keepdims=True)
    lg2 = jnp.where(lane == i1, -jnp.inf, lg)
    m2 = jnp.max(lg2, axis=1, keepdims=True)
    i2 = jnp.min(jnp.where(lg2 == m2, lane, LANES), axis=1, keepdims=True)
    e2 = jnp.exp(m2 - m1)
    w1 = 1.0 / (1.0 + e2)
    return i1, i2, w1, e2 * w1


GROUP_TILE = 256
GROUP_PAD = 32
GROUP_PIECES = GROUP_TILE // GROUP_PAD
SLOT_RADIX = 64.0


def _moe_group_kernel(x_ref, sh_ref, sc_ref, g_ref, wr_ref, tri_ref, xg_ref, ws_ref, slot_ref, cnt_ref,
                      h_scr, rows_scr, wm_scr):
    j = pl.program_id(1)
    tb = x_ref.shape[0]
    gt_rows = xg_ref.shape[1]

    @pl.when(j == 0)
    def _():
        h = _modulated_norm(x_ref[...], g_ref[...], sc_ref[0], sh_ref[0])
        h_scr[...] = h.astype(BF)
        i1, i2, w1, w2 = _top2(_dot3(h, wr_ref[...]))
        lane = lax.broadcasted_iota(jnp.int32, (tb, LANES), 1)
        oh0 = jnp.where(lane == i1, 1.0, 0.0)
        oh1 = jnp.where(lane == i2, 1.0, 0.0)
        c0 = jnp.sum(oh0, axis=0, keepdims=True)
        cnt = c0 + jnp.sum(oh1, axis=0, keepdims=True)
        tri = tri_ref[...]
        pre0 = _dot(tri, oh0.astype(BF))
        pre1 = _dot(tri, oh1.astype(BF)) + c0
        tiles = jnp.ceil(cnt * (1.0 / GROUP_PAD))
        upper = jnp.where(lax.broadcasted_iota(jnp.int32, (LANES, LANES), 0)
                          < lax.broadcasted_iota(jnp.int32, (LANES, LANES), 1), 1.0, 0.0).astype(BF)
        off = _dot(jnp.broadcast_to(tiles, (8, LANES)).astype(BF), upper)[0:1] * float(GROUP_PAD)
        slot0 = jnp.sum(oh0 * (off + pre0), axis=1, keepdims=True)
        slot1 = jnp.sum(oh1 * (off + pre1), axis=1, keepdims=True)
        slot_ref[0] = jnp.where(lane == 0, slot0, jnp.where(lane == 1, slot1, 0.0))
        cnt_ref[0] = jnp.broadcast_to(cnt, (8, LANES))
        hi0 = jnp.floor(slot0 * (1.0 / SLOT_RADIX))
        hi1 = jnp.floor(slot1 * (1.0 / SLOT_RADIX))
        digits = jnp.where(lane == 0, hi0, jnp.where(lane == 1, slot0 - SLOT_RADIX * hi0,
                           jnp.where(lane == 2, hi1, jnp.where(lane == 3, slot1 - SLOT_RADIX * hi1, 0.0))))
        sel = jnp.where(lax.broadcasted_iota(jnp.int32, (8, LANES), 0)
                        == lax.broadcasted_iota(jnp.int32, (8, LANES), 1), 1.0, 0.0).astype(BF)
        rows_scr[...] = _dot_nt(sel, digits.astype(BF))
        w1h, w1l = _split(w1)
        w1m, w1l = _split(w1 - w1h.astype(F32))
        w2h, w2l = _split(w2)
        w2m, w2l = _split(w2 - w2h.astype(F32))
        cols = [w1h, w1m, w1l, w2h, w2m, w2l]
        wm = jnp.zeros((tb, LANES), F32)
        for li, col in enumerate(cols):
            wm = jnp.where(lane == li, col.astype(F32), wm)
        wm_scr[...] = wm.astype(BF)

    rows = rows_scr[...]
    s0 = rows[0:1] * SLOT_RADIX + rows[1:2]
    s1 = rows[2:3] * SLOT_RADIX + rows[3:4]
    pos = (lax.broadcasted_iota(jnp.int32, (gt_rows, tb), 0) + j * gt_rows).astype(F32)
    g0 = jnp.where(pos == s0, 1.0, 0.0).astype(BF)
    g1 = jnp.where(pos == s1, 1.0, 0.0).astype(BF)
    xg_ref[0] = _dot(g0 + g1, h_scr[...]).astype(BF)
    lane_w = lax.broadcasted_iota(jnp.int32, (gt_rows, LANES), 1)
    wsum = (jnp.where(lane_w < 3, _dot(g0, wm_scr[...]), 0.0)
            + jnp.where((lane_w >= 3) & (lane_w < 6), _dot(g1, wm_scr[...]), 0.0))
    ws_ref[0] = jnp.broadcast_to(jnp.sum(wsum, axis=1, keepdims=True), (gt_rows, LANES))


def _moe_group(x2, sh, sc, g, wr_pad, *, tb, nt, mod_row):
    t, d = x2.shape
    nb = t // tb
    row3 = lambda b, j: (mod_row(b), 0, 0)
    tri = jnp.asarray(np.tril(np.ones((tb, tb), np.float32), -1)).astype(BF)
    return pl.pallas_call(
        _moe_group_kernel,
        grid=(nb, nt),
        in_specs=[
            pl.BlockSpec((tb, d), lambda b, j: (b, 0)),
            pl.BlockSpec((1, 1, d), row3), pl.BlockSpec((1, 1, d), row3),
            pl.BlockSpec((1, d), lambda b, j: (0, 0)),
            pl.BlockSpec((d, LANES), lambda b, j: (0, 0)),
            pl.BlockSpec((tb, tb), lambda b, j: (0, 0), pipeline_mode=pl.Buffered(1)),
        ],
        out_specs=[
            pl.BlockSpec((1, GROUP_TILE, d), lambda b, j: (b * nt + j, 0, 0)),
            pl.BlockSpec((1, GROUP_TILE, LANES), lambda b, j: (b * nt + j, 0, 0)),
            pl.BlockSpec((1, tb, LANES), lambda b, j: (b, 0, 0)),
            pl.BlockSpec((1, 8, LANES), lambda b, j: (b, 0, 0)),
        ],
        out_shape=[
            jax.ShapeDtypeStruct((nb * nt, GROUP_TILE, d), BF),
            jax.ShapeDtypeStruct((nb * nt, GROUP_TILE, LANES), F32),
            jax.ShapeDtypeStruct((nb, tb, LANES), F32),
            jax.ShapeDtypeStruct((nb, 8, LANES), F32),
        ],
        scratch_shapes=[pltpu.VMEM((tb, d), BF), pltpu.VMEM((8, tb), F32), pltpu.VMEM((tb, LANES), BF)],
        compiler_params=_cparams("parallel", "arbitrary"),
        name="moe_group",
    )(x2, sh, sc, g, wr_pad, tri)


def _moe_schedule(cnt, nh):
    np_ = GROUP_PIECES
    h = (cnt + GROUP_PAD - 1) // GROUP_PAD
    nb, ne = h.shape
    tot = h.sum(0)
    pairs = (tot + np_ - 1) // np_
    cum_p = jnp.cumsum(pairs)
    start_p = cum_p - pairs
    n_used = cum_p[-1]
    n_steps = (nb * nh + ne * (np_ - 1)) // np_
    q = jnp.minimum(jnp.arange(n_steps, dtype=jnp.int32), n_used - 1)
    e = jnp.sum(q[:, None] >= cum_p[None, :], axis=1).astype(jnp.int32)
    r = q - start_p[e]
    cum_b = jnp.cumsum(h, axis=0)
    first = jnp.cumsum(h, axis=1) - h

    def piece(idx):
        idx = jnp.minimum(idx, tot[e] - 1)
        blk = jnp.sum(idx[:, None] >= cum_b.T[e], axis=1).astype(jnp.int32)
        return blk * nh + first[blk, e] + idx - (cum_b[blk, e] - h[blk, e])

    x = jnp.arange(nh, dtype=jnp.int32)[None, :]
    ex = jnp.sum(x[:, :, None] >= jnp.cumsum(h, axis=1)[:, None, :], axis=2).astype(jnp.int32)
    exc = jnp.minimum(ex, ne - 1)
    g = jnp.take_along_axis(cum_b - h, exc, axis=1) + x - jnp.take_along_axis(first, exc, axis=1)
    loc = np_ * (start_p[exc] + g // np_) + g % np_
    loc = jnp.where(ex < ne, loc, loc[:, 0:1])
    pieces = jnp.concatenate([piece(np_ * r + k) for k in range(np_)]).astype(jnp.int32)
    return (pieces, e, n_used.astype(jnp.int32).reshape(1), loc.reshape(-1).astype(jnp.int32),
            ((h.sum(1) + np_ - 1) // np_).astype(jnp.int32))


def _moe_expert_kernel(pc_ref, exp_ref, nused_ref, *refs):
    np_ = GROUP_PIECES
    x_refs, w_refs = refs[:np_], refs[np_:2 * np_]
    wg_ref, wu_ref, wd_ref, y_ref = refs[2 * np_:]

    @pl.when(pl.program_id(0) < nused_ref[0])
    def _():
        x = jnp.concatenate([r[0] for r in x_refs], axis=0)
        act = _silu(_dot(x, wg_ref[0])) * _dot(x, wu_ref[0])
        y = _dot(act.astype(BF), wd_ref[0])
        w = jnp.concatenate([r[0] for r in w_refs], axis=0)
        y_ref[0] = (y * jnp.concatenate([w] * (y.shape[1] // LANES), axis=1)).astype(BF)


def _moe_experts(pieces, step_exp, n_used, xg, ws, wg, wu, wd):
    d = xg.shape[-1]
    ff = wg.shape[2]
    np_ = GROUP_PIECES
    n_steps = step_exp.shape[0]
    xh = xg.reshape(-1, GROUP_PAD, d)
    wh = ws.reshape(-1, GROUP_PAD, LANES)
    pc3 = lambda k: (lambda i, pc, se, nu: (pc[k * n_steps + i], 0, 0))
    exp3 = lambda i, pc, se, nu: (se[i], 0, 0)
    return pl.pallas_call(
        _moe_expert_kernel,
        grid_spec=pltpu.PrefetchScalarGridSpec(
            num_scalar_prefetch=3,
            grid=(n_steps,),
            in_specs=(
                [pl.BlockSpec((1, GROUP_PAD, d), pc3(k)) for k in range(np_)]
                + [pl.BlockSpec((1, GROUP_PAD, LANES), pc3(k)) for k in range(np_)]
                + [pl.BlockSpec((1, d, ff), exp3), pl.BlockSpec((1, d, ff), exp3), pl.BlockSpec((1, ff, d), exp3)]
            ),
            out_specs=pl.BlockSpec((1, GROUP_TILE, d), lambda i, pc, se, nu: (jnp.minimum(i, nu[0] - 1), 0, 0)),
        ),
        out_shape=jax.ShapeDtypeStruct((n_steps, GROUP_TILE, d), BF),
        compiler_params=_cparams("arbitrary"),
        name="moe_experts",
    )(pieces, step_exp, n_used, *([xh] * np_), *([wh] * np_), wg, wu, wd)


def _moe_combine_kernel(nt_ref, loc_ref, x_ref, gt_ref, slot_ref, *refs):
    y_refs, o_ref, acc_scr = refs[:GROUP_PIECES], refs[GROUP_PIECES], refs[GROUP_PIECES + 1]
    b = pl.program_id(0)
    j = pl.program_id(1)
    tb = x_ref.shape[0]

    @pl.when(j == 0)
    def _():
        acc_scr[...] = jnp.zeros_like(acc_scr)

    @pl.when(j < nt_ref[b])
    def _():
        sl = slot_ref[0]
        pos = (lax.broadcasted_iota(jnp.int32, (tb, GROUP_TILE), 1) + j * GROUP_TILE).astype(F32)
        p = jnp.where((pos == sl[:, 0:1]) | (pos == sl[:, 1:2]), 1.0, 0.0).astype(BF)
        acc_scr[...] += _dot(p, jnp.concatenate([r[0] for r in y_refs], axis=0))

    @pl.when(j == pl.num_programs(1) - 1)
    def _():
        o_ref[...] = x_ref[...] + gt_ref[0] * acc_scr[...]


def _moe_combine(ntiles_b, loc, x2, gt, slots, yg, *, tb, nt, mod_row):
    t, d = x2.shape
    nb = t // tb
    np_ = GROUP_PIECES
    yh = yg.reshape(-1, GROUP_PAD, d)

    def piece3(k):
        return lambda b, j, n, lc: (lc[(b * nt + jnp.minimum(j, n[b] - 1)) * np_ + k], 0, 0)

    return pl.pallas_call(
        _moe_combine_kernel,
        grid_spec=pltpu.PrefetchScalarGridSpec(
            num_scalar_prefetch=2,
            grid=(nb, nt),
            in_specs=[
                pl.BlockSpec((tb, d), lambda b, j, n, lc: (b, 0)),
                pl.BlockSpec((1, 1, d), lambda b, j, n, lc: (mod_row(b), 0, 0)),
                pl.BlockSpec((1, tb, LANES), lambda b, j, n, lc: (b, 0, 0)),
            ] + [pl.BlockSpec((1, GROUP_PAD, d), piece3(k)) for k in range(np_)],
            out_specs=pl.BlockSpec((tb, d), lambda b, j, n, lc: (b, 0)),
            scratch_shapes=[pltpu.VMEM((tb, d), F32)],
        ),
        out_shape=jax.ShapeDtypeStruct((t, d), F32),
        compiler_params=_cparams("parallel", "arbitrary"),
        name="moe_combine",
    )(ntiles_b, loc, x2, gt, slots, *([yh] * np_))


def _moe(x2, sh, sc, gt, g, wr_pad, wg, wu, wd, *, tb, mod_row):
    nt = -(-(2 * tb + N_EXPERTS * (GROUP_PAD - 1)) // GROUP_TILE)
    xg, ws, slots, cnt = _moe_group(x2, sh, sc, g, wr_pad, tb=tb, nt=nt, mod_row=mod_row)
    counts = cnt[:, 0, :N_EXPERTS].astype(jnp.int32)
    pieces, step_exp, n_used, loc, ntiles_b = _moe_schedule(counts, nt * GROUP_PIECES)
    yg = _moe_experts(pieces, step_exp, n_used, xg, ws, wg, wu, wd)
    return _moe_combine(ntiles_b, loc, x2, gt, slots, yg, tb=tb, nt=nt, mod_row=mod_row)


def _rope_tables(seq_len):
    pos = np.arange(seq_len)
    prow = (pos // GRID_W).astype(np.float32)
    pcol = (pos % GRID_W).astype(np.float32)
    n_freq = HEAD_DIM // 4
    inv = (np.float32(ROPE_THETA) ** (-np.arange(n_freq, dtype=np.float32) / n_freq)).astype(np.float32)
    ar = (prow[:, None] * inv[None, :]).astype(np.float64)
    ac = (pcol[:, None] * inv[None, :]).astype(np.float64)
    cos = np.concatenate([np.cos(ar)] * 2 + [np.cos(ac)] * 2, axis=1)
    sin = np.concatenate([-np.sin(ar), np.sin(ar), -np.sin(ac), np.sin(ac)], axis=1)
    return (jnp.asarray(np.concatenate([cos, cos], axis=1), dtype=F32),
            jnp.asarray(np.concatenate([sin, sin], axis=1), dtype=F32))


def _head_sum_matrix():
    c = np.arange(ATTN_WIDTH)
    return jnp.asarray((c[:, None] // HEAD_DIM) == (c[None, :] // HEAD_DIM), dtype=F32).astype(BF)


def _fnet_channel_matrix():
    c = np.arange(FNET_WIDTH)
    same = (c[:, None] // FNET_GROUP_DIM) == (c[None, :] // FNET_GROUP_DIM)
    ang = 2.0 * np.pi * (((c[:, None] % FNET_GROUP_DIM) * (c[None, :] % FNET_GROUP_DIM)) % FNET_GROUP_DIM) / FNET_GROUP_DIM
    cb = np.where(same, np.cos(ang), 0.0)
    sb = np.where(same, np.sin(ang), 0.0)
    return jnp.asarray(np.concatenate([cb, -sb], axis=1), dtype=F32).astype(BF)


def _filter_features(n):
    t = np.linspace(0.0, 1.0, n)[:, None]
    w = 2.0 * np.pi * np.arange(n)[:, None] / n
    fb = np.linspace(1e-4, FILTER_BANDS - 1, FILTER_BANDS)
    z = np.concatenate([t, np.cos(fb * w), -np.sin(fb * w)], axis=-1)
    return jnp.asarray(np.pad(z, ((0, 0), (0, 64 - z.shape[1]))), dtype=F32)


def _decay_rates():
    d = jnp.abs(jnp.linspace(math.log(DECAY_TARGET) / SLOW_DECAY_PCT, math.log(DECAY_TARGET) / FAST_DECAY_PCT,
                             HYENA_WIDTH, dtype=F32))
    return jnp.concatenate([d] * HYENA_ORDER)[None, :]


def kernel(x, c, ctx, c_ctx, w_ada, b_ada, norm1_g, norm2_g, w_in, q_norm_g, k_norm_g, attn_sink,
           hy_conv_w, hy_conv_b, hy_filt_w1, hy_filt_b1, hy_filt_freq1, hy_filt_w2, hy_filt_b2,
           hy_filt_freq2, hy_filt_w3, hy_bias, w_proj_attn, w_proj_hyena, w_proj_fnet, w_out,
           ffn_w_gate, ffn_w_up, ffn_w_down, moe_router, moe_w_gate, moe_w_up, moe_w_down):
    b, seq, d = x.shape
    n_ctx = ctx.shape[1]
    depth = w_ada.shape[0]
    tm = 512
    tiles_per_seq = seq // tm

    cond8 = jnp.concatenate([c, c_ctx[None, :], jnp.zeros((8 - b - 1, d), F32)], axis=0)
    mods = _adaln(cond8, w_ada, b_ada)

    cos_l, sin_l = _rope_tables(seq)
    cos_c = jnp.ones((n_ctx, LANES), F32)
    sin_c = jnp.zeros((n_ctx, LANES), F32)
    gsum = _head_sum_matrix()
    mfn = _fnet_channel_matrix()
    deltas = _decay_rates()
    zfeat_l = _filter_features(seq)
    zfeat_c = _filter_features(n_ctx)

    lat_row = lambda i: i // tiles_per_seq
    ctx_row = lambda i: b
    lat_tab = lambda i: i % tiles_per_seq
    ctx_tab = lambda i: 0
    tm_c = min(tm, n_ctx)

    xs = x.reshape(b * seq, d)
    cs = ctx.reshape(b * n_ctx, d)
    for l in range(depth):
        last = l == depth - 1
        mod = lambda j: mods[l, :, j * d:(j + 1) * d].reshape(8, 1, d)
        w_in_bf = w_in[l].astype(BF)
        qg = jnp.tile(q_norm_g[l], N_HEADS)[None, :]
        kg = jnp.tile(k_norm_g[l], N_KV_HEADS)[None, :]
        g1 = norm1_g[l][None, :]
        wa, wh, wf, wo = (w_proj_attn[l].astype(BF), w_proj_hyena[l].astype(BF),
                          w_proj_fnet[l].astype(BF), w_out[l].astype(BF))
        conv_w = hy_conv_w[l].reshape(3, -1)
        conv_b = hy_conv_b[l][None, :]
        w1p = jnp.pad(hy_filt_w1[l], ((0, 64 - hy_filt_w1.shape[1]), (0, 0)))
        filt = (w1p, hy_filt_b1[l][None, :], hy_filt_freq1[l][None, :], hy_filt_w2[l],
                hy_filt_b2[l][None, :], hy_filt_freq2[l][None, :], hy_filt_w3[l], deltas)

        qkv_c, upq_c, gates_c = _phase_a(
            cs, mod(0), mod(1), g1, w_in_bf, cos_c, sin_c, qg, kg, gsum, mfn, conv_w, conv_b,
            tm=tm_c, tiles_per_seq=n_ctx // tm_c, mod_row=ctx_row, tab_row=ctx_tab)
        qkv_c = qkv_c.reshape(b, n_ctx, -1)
        upq_c = upq_c.reshape(b, n_ctx, -1)

        qkv_l, upq_l, gates_l = _phase_a(
            xs, mod(0), mod(1), g1, w_in_bf, cos_l, sin_l, qg, kg, gsum, mfn, conv_w, conv_b,
            tm=tm, tiles_per_seq=tiles_per_seq, mod_row=lat_row, tab_row=lat_tab)
        upq_l = upq_l.reshape(b, seq, -1)
        attn_l = _attention(attn_sink[l], qkv_l.reshape(b, seq, -1), qkv_c, local=True, tq=512)
        kf_l, kb_l, nrm_l = _hy_filter(zfeat_l, *filt, tm=1024)
        hy_l = _hyena(upq_l, (kf_l, kb_l), nrm_l, hy_bias[l])
        fn_l = _fnet(upq_l)
        xs = _merge(xs, mod(2), attn_l.reshape(b * seq, -1), hy_l, fn_l, gates_l, wa, wh, wf, wo,
                    tm=tm, mod_row=lat_row)

        if not last:
            attn_c = _attention(attn_sink[l], qkv_c, qkv_c, local=False, tq=n_ctx)
            kf_c, kb_c, nrm_c = _hy_filter(zfeat_c, *filt, tm=n_ctx)
            hy_c = _hyena(upq_c, (kf_c, kb_c), nrm_c, hy_bias[l])
            fn_c = _fnet(upq_c)
            cs = _merge(cs, mod(2), attn_c.reshape(b * n_ctx, -1), hy_c, fn_c, gates_c, wa, wh, wf, wo,
                        tm=tm_c, mod_row=ctx_row)

        g2 = norm2_g[l][None, :]
        i = l // 2
        if l % 2 == 0:
            wg, wu, wd = ffn_w_gate[i].astype(BF), ffn_w_up[i].astype(BF), ffn_w_down[i].astype(BF)
            run = lambda t2, rows, tmm: _ffn(t2, mod(3), mod(4), mod(5), g2, wg, wu, wd,
                                             tm=tmm, tf=D_FF, mod_row=rows)
        else:
            wr = jnp.pad(moe_router[i], ((0, 0), (0, LANES - N_EXPERTS)))
            wg, wu, wd = moe_w_gate[i].astype(BF), moe_w_up[i].astype(BF), moe_w_down[i].astype(BF)
            run = lambda t2, rows, tmm: _moe(t2, mod(3), mod(4), mod(5), g2, wr, wg, wu, wd,
                                             tb=tmm, mod_row=rows)
        tm_ffn = 512 if l % 2 == 0 else 1024
        xs = run(xs, lambda t: t // (seq // tm_ffn), tm_ffn)
        if not last:
            cs = run(cs, ctx_row, min(tm_ffn, b * n_ctx))
    return xs.reshape(b, seq, d)
```

```python
import functools
import math

import numpy as np
import jax
import jax.numpy as jnp
from jax import lax
from jax.experimental import pallas as pl
from jax.experimental.pallas import tpu as pltpu

F32 = jnp.float32
BF = jnp.bfloat16

D_MODEL = 1024
DEPTH = 4
GRID_W = 64
HEAD_DIM = 64
N_HEADS = 8
N_KV_HEADS = 2
ATTN_WIDTH = N_HEADS * HEAD_DIM
KV_WIDTH = N_KV_HEADS * HEAD_DIM
WINDOW = 128
QBLK = 128
ROPE_THETA = 10000.0
HYENA_ORDER = 2
HYENA_WIDTH = 256
FILTER_BANDS = 16
FILTER_HIDDEN = 64
DECAY_TARGET = 1e-2
FAST_DECAY_PCT = 0.3
SLOW_DECAY_PCT = 1.5
FNET_WIDTH = 256
FNET_GROUP_DIM = 64
Q_END = ATTN_WIDTH
K_END = Q_END + KV_WIDTH
V_END = K_END + KV_WIDTH
HY_END = V_END + (HYENA_ORDER + 1) * HYENA_WIDTH
FN_END = HY_END + FNET_WIDTH
IN_WIDTH = FN_END + 3 * D_MODEL
D_FF = 2816
N_EXPERTS = 8
EPS = 1e-6
LANES = 128
NEG = -1e30
STAGE_ROWS = 16
STAGE_JOIN = 2

VMEM_LIMIT = 56 * 1024 * 1024


def _cparams(*sem):
    return pltpu.CompilerParams(dimension_semantics=sem, vmem_limit_bytes=VMEM_LIMIT)


def _dot(a, b):
    return jnp.dot(a, b, preferred_element_type=F32)


def _dot_nt(a, b):
    return lax.dot_general(a, b, (((1,), (1,)), ((), ())), preferred_element_type=F32)


def _split(a):
    hi = a.astype(BF)
    lo = (a - hi.astype(F32)).astype(BF)
    return hi, lo


def _dot3(a, b):
    ah, al = _split(a)
    bh, bl = _split(b)
    return _dot(ah, bh) + (_dot(ah, bl) + _dot(al, bh))


def _dot2(a, b_bf16):
    ah, al = _split(a)
    return _dot(ah, b_bf16) + _dot(al, b_bf16)


def _sigmoid(v):
    return 0.5 * jnp.tanh(0.5 * v) + 0.5


def _silu(v):
    return v * _sigmoid(v)


def _cast_kernel(w_ref, o_ref):
    o_ref[...] = w_ref[...].astype(BF)


def _layer_bf16(w, layer, rows):
    a, b = w.shape[-2:]
    inner = int(np.prod(w.shape[1:-2], dtype=np.int64))
    w3 = w.reshape(-1, a, b)
    out = pl.pallas_call(
        _cast_kernel,
        grid=(inner, a // rows),
        in_specs=[pl.BlockSpec((1, rows, b), lambda m, i: (layer * inner + m, i, 0))],
        out_specs=pl.BlockSpec((1, rows, b), lambda m, i: (m, i, 0)),
        out_shape=jax.ShapeDtypeStruct((inner, a, b), BF),
        compiler_params=_cparams("parallel", "parallel"),
        name="cast_bf16",
    )(w3)
    return out.reshape(w.shape[1:])


def _adaln_kernel(c_ref, w_ref, b_ref, o_ref):
    o_ref[0] = _dot3(_silu(c_ref[...]), w_ref[0]) + b_ref[0]


def _adaln(cond8, w_ada, b_ada):
    depth, d, n6 = w_ada.shape
    tn = 1024
    return pl.pallas_call(
        _adaln_kernel,
        grid=(depth, n6 // tn),
        in_specs=[
            pl.BlockSpec((8, d), lambda l, j: (0, 0)),
            pl.BlockSpec((1, d, tn), lambda l, j: (l, 0, j)),
            pl.BlockSpec((1, 1, tn), lambda l, j: (l, 0, j)),
        ],
        out_specs=pl.BlockSpec((1, 8, tn), lambda l, j: (l, 0, j)),
        out_shape=jax.ShapeDtypeStruct((depth, 8, n6), F32),
        compiler_params=_cparams("parallel", "parallel"),
        name="adaln",
    )(cond8, w_ada, b_ada.reshape(depth, 1, n6))


def _modulated_norm(x, g, sc, sh):
    ms = jnp.mean(x * x, axis=-1, keepdims=True)
    h = (x * lax.rsqrt(ms + EPS)) * g
    return h * (1.0 + sc) + sh


HALO = 16


def _phase_a_kernel(x_ref, xp_ref, xn_ref, sh_ref, sc_ref, g_ref, w_ref, cos_ref, sin_ref, qg_ref, kg_ref,
                    gsum_ref, mfn_ref, cw_ref, cb_ref, qkv_ref, upq_ref, gate_ref, *, tiles_per_seq):
    tm = x_ref.shape[0]
    norm = lambda xv: _modulated_norm(xv, g_ref[...], sc_ref[0], sh_ref[0]).astype(BF)
    hb = norm(x_ref[...])
    cos = cos_ref[...]
    sin = sin_ref[...]

    def headnorm(t, gain, gs):
        ss = _dot2(t * t, gs)
        return t * lax.rsqrt(ss * (1.0 / HEAD_DIM) + EPS) * gain

    def rope(t, cosw, sinw):
        w = t.shape[1]
        nxt = pltpu.roll(t, w - 16, axis=1)
        prv = pltpu.roll(t, 16, axis=1)
        lw = lax.broadcasted_iota(jnp.int32, t.shape, 1)
        return t * cosw + jnp.where((lw % 32) < 16, nxt, prv) * sinw

    def dup_halves(t):
        lane = lax.broadcasted_iota(jnp.int32, t.shape, 1)
        sw = pltpu.roll(t, 64, axis=1)
        lo = lane < 64
        return jnp.concatenate([jnp.where(lo, t, sw), jnp.where(lo, sw, t)], axis=1)

    pr = _dot(hb, w_ref[:, 0:FN_END])
    pq = pr[:, 0:Q_END]
    qn = headnorm(pq, qg_ref[...], gsum_ref[...])
    cos4 = jnp.concatenate([cos] * 4, axis=1)
    sin4 = jnp.concatenate([sin] * 4, axis=1)
    qkv_ref[:, 0:Q_END] = (rope(qn, cos4, sin4) * (HEAD_DIM ** -0.5)).astype(BF)

    kn = headnorm(pr[:, Q_END:K_END], kg_ref[...], gsum_ref[0:KV_WIDTH, 0:KV_WIDTH])
    qkv_ref[:, Q_END:Q_END + 2 * KV_WIDTH] = dup_halves(rope(kn, cos, sin)).astype(BF)
    qkv_ref[:, Q_END + 2 * KV_WIDTH:] = dup_halves(pr[:, K_END:V_END]).astype(BF)

    w_hy = w_ref[:, V_END:HY_END]
    u = pr[:, V_END:HY_END]
    tile = pl.program_id(0) % tiles_per_seq
    u_before = jnp.where(tile == 0, 0.0, _dot(norm(xp_ref[...]), w_hy)[HALO - 1:HALO])
    u_after = jnp.where(tile == tiles_per_seq - 1, 0.0, _dot(norm(xn_ref[...]), w_hy)[0:1])
    row = lax.broadcasted_iota(jnp.int32, (tm, 1), 0)
    prv = jnp.where(row == 0, u_before, pltpu.roll(u, 1, axis=0))
    nxt = jnp.where(row == tm - 1, u_after, pltpu.roll(u, tm - 1, axis=0))
    n_hy = HY_END - V_END
    upq_ref[:, 0:n_hy] = prv * cw_ref[0:1, :] + u * cw_ref[1:2, :] + nxt * cw_ref[2:3, :] + cb_ref[...]

    upq_ref[:, n_hy:] = _dot(pr[:, HY_END:FN_END].astype(BF), mfn_ref[...])
    gate_ref[...] = _sigmoid(_dot(hb, w_ref[:, FN_END:])).astype(BF)


def _phase_a(x2, sh, sc, g, w_in_bf, cos_t, sin_t, qg, kg, gsum, mfn, conv_w, conv_b, *, tm, tiles_per_seq,
             mod_row, tab_row):
    t, d = x2.shape
    row3 = lambda i: (mod_row(i), 0, 0)
    full = lambda i: (0, 0)
    tok = lambda i: (i, 0)
    n_halo = t // HALO
    outs = [
        ((t, ATTN_WIDTH + 4 * KV_WIDTH), BF), ((t, 3 * HYENA_WIDTH + 2 * FNET_WIDTH), F32),
        ((t, 3 * D_MODEL), BF),
    ]
    kern = functools.partial(_phase_a_kernel, tiles_per_seq=tiles_per_seq)
    return pl.pallas_call(
        kern,
        grid=(t // tm,),
        in_specs=[
            pl.BlockSpec((tm, d), tok),
            pl.BlockSpec((HALO, d), lambda i: (jnp.maximum(i * (tm // HALO) - 1, 0), 0)),
            pl.BlockSpec((HALO, d), lambda i: (jnp.minimum((i + 1) * (tm // HALO), n_halo - 1), 0)),
            pl.BlockSpec((1, 1, d), row3),
            pl.BlockSpec((1, 1, d), row3),
            pl.BlockSpec((1, d), full),
            pl.BlockSpec((d, IN_WIDTH), full, pipeline_mode=pl.Buffered(1)),
            pl.BlockSpec((tm, LANES), lambda i: (tab_row(i), 0)),
            pl.BlockSpec((tm, LANES), lambda i: (tab_row(i), 0)),
            pl.BlockSpec((1, ATTN_WIDTH), full),
            pl.BlockSpec((1, KV_WIDTH), full),
            pl.BlockSpec((ATTN_WIDTH, ATTN_WIDTH), full),
            pl.BlockSpec((FNET_WIDTH, 2 * FNET_WIDTH), full),
            pl.BlockSpec(conv_w.shape, full),
            pl.BlockSpec(conv_b.shape, full),
        ],
        out_specs=[pl.BlockSpec((tm, s[1]), tok) for s, _ in outs],
        out_shape=[jax.ShapeDtypeStruct(s, dt) for s, dt in outs],
        compiler_params=_cparams("parallel"),
        name="phase_a",
    )(x2, x2, x2, sh, sc, g, w_in_bf, cos_t, sin_t, qg, kg, gsum, mfn, conv_w, conv_b)


def _attn_kernel(sink_ref, q_ref, kd_ref, vd_ref, kc_ref, vc_ref, o_ref, *, local, seq_len):
    tq = q_ref.shape[1]
    nblk = tq // QBLK
    gq = N_HEADS // N_KV_HEADS
    rows = gq * QBLK
    lane = lax.broadcasted_iota(jnp.int32, (QBLK, LANES), 1)
    lo_half = lane < 64
    hrow = lax.broadcasted_iota(jnp.int32, (rows, 1), 0) // QBLK
    nband = 3 * QBLK
    if local:
        qk_off = (lax.broadcasted_iota(jnp.int32, (rows, nband), 0) % QBLK
                  - lax.broadcasted_iota(jnp.int32, (rows, nband), 1))
    for blk in range(nblk):
        r0 = blk * QBLK
        qb = q_ref[0, r0:r0 + QBLK, :]
        if local:
            n = pl.program_id(1) * nblk + blk
            start = pl.multiple_of(jnp.clip((n - 1) * QBLK, 0, seq_len - nband), QBLK)
            valid = jnp.abs(qk_off + (n * QBLK - start)) <= WINDOW
        for g in range(N_KV_HEADS):
            parts = []
            for hh in range(gq):
                h = gq * g + hh
                qc = qb[:, (h // 2) * LANES:(h // 2 + 1) * LANES]
                keep = lo_half if h % 2 == 0 else jnp.logical_not(lo_half)
                parts.append(jnp.where(keep, qc, jnp.zeros_like(qc)))
            q4 = jnp.concatenate(parts, axis=0)
            sk = jnp.full((rows, 1), sink_ref[gq * g + gq - 1], F32)
            for hh in range(gq - 2, -1, -1):
                sk = jnp.where(hrow == hh, sink_ref[gq * g + hh], sk)
            gl = slice(g * LANES, (g + 1) * LANES)
            s_ctx = _dot_nt(q4, kc_ref[0, :, gl])
            m = jnp.maximum(jnp.max(s_ctx, axis=1, keepdims=True), sk)
            if local:
                s_loc = _dot_nt(q4, kd_ref[0, pl.ds(start, nband), gl])
                s_loc = jnp.where(valid, s_loc, NEG)
                m = jnp.maximum(m, jnp.max(s_loc, axis=1, keepdims=True))
            p_ctx = jnp.exp(s_ctx - m)
            den = jnp.sum(p_ctx, axis=1, keepdims=True) + jnp.exp(sk - m)
            o = _dot(p_ctx.astype(BF), vc_ref[0, :, gl])
            if local:
                p_loc = jnp.exp(s_loc - m)
                den = den + jnp.sum(p_loc, axis=1, keepdims=True)
                o = o + _dot(p_loc.astype(BF), vd_ref[0, pl.ds(start, nband), gl])
            o = o / den
            for cc in range(gq // 2):
                col = (gq // 2) * g + cc
                oa = o[(2 * cc) * QBLK:(2 * cc + 1) * QBLK]
                ob = o[(2 * cc + 1) * QBLK:(2 * cc + 2) * QBLK]
                o_ref[0, r0:r0 + QBLK, col * LANES:(col + 1) * LANES] = (
                    jnp.where(lo_half, oa, ob).astype(BF))


def _attention(sink, qkv, qkv_ctx, *, local, tq):
    b, lq, _ = qkv.shape
    c = qkv_ctx.shape[1]
    kw = 2 * KV_WIDTH
    k_blk, v_blk = ATTN_WIDTH // kw, ATTN_WIDTH // kw + 1
    kern = functools.partial(_attn_kernel, local=local, seq_len=lq)
    return pl.pallas_call(
        kern,
        grid=(b, lq // tq),
        in_specs=[
            pl.BlockSpec(memory_space=pltpu.SMEM),
            pl.BlockSpec((1, tq, ATTN_WIDTH), lambda bi, i: (bi, i, 0)),
            pl.BlockSpec((1, lq, kw), lambda bi, i: (bi, 0, k_blk)),
            pl.BlockSpec((1, lq, kw), lambda bi, i: (bi, 0, v_blk)),
            pl.BlockSpec((1, c, kw), lambda bi, i: (bi, 0, k_blk)),
            pl.BlockSpec((1, c, kw), lambda bi, i: (bi, 0, v_blk)),
        ],
        out_specs=pl.BlockSpec((1, tq, ATTN_WIDTH), lambda bi, i: (bi, i, 0)),
        out_shape=jax.ShapeDtypeStruct((b, lq, ATTN_WIDTH), BF),
        compiler_params=_cparams("parallel", "parallel"),
        name="attn_local" if local else "attn_ctx",
    )(sink, qkv, qkv, qkv, qkv_ctx, qkv_ctx)


FILTER_HALO = 128


def _hy_filter_kernel(z_ref, zn_ref, w1_ref, b1_ref, f1_ref, w2_ref, b2_ref, f2_ref, w3f_ref, w3b_ref, dl_ref,
                      flip_ref, kf_ref, kb_ref, nrm_ref):
    i = pl.program_id(0)
    tm = z_ref.shape[0]
    rows = tm + FILTER_HALO
    z = jnp.concatenate([z_ref[...], zn_ref[...]], axis=0)
    h = jnp.sin(f1_ref[...] * (_dot3(z, w1_ref[...]) + b1_ref[...]))
    h = jnp.sin(f2_ref[...] * (_dot3(h, w2_ref[...]) + b2_ref[...]))
    tcol = jnp.where(lax.broadcasted_iota(jnp.int32, (rows, LANES - FILTER_HIDDEN), 1) == 0, z[:, 0:1], 0.0)
    hid = jnp.concatenate([h, tcol], axis=1)

    def taps(hv, w3_ref):
        return _dot3(hv, w3_ref[...]) * jnp.exp(-hv[:, FILTER_HIDDEN:FILTER_HIDDEN + 1] * dl_ref[...])

    kf = taps(hid[0:tm], w3f_ref)
    flip = flip_ref[...]
    h1 = hid.astype(BF)
    r1 = hid - h1.astype(F32)
    h2 = r1.astype(BF)
    h3 = (r1 - h2.astype(F32)).astype(BF)
    hid_rev = _dot(flip, h1) + (_dot(flip, h2) + _dot(flip, h3))
    kb = taps(hid_rev, w3b_ref)
    out_row = (pl.num_programs(0) - 1 - i) * tm + lax.broadcasted_iota(jnp.int32, (tm, 1), 0)
    kb = jnp.where(out_row == 0, 0.0, kb)
    kf_ref[...] = kf
    kb_ref[...] = kb

    @pl.when(i == 0)
    def _():
        nrm_ref[...] = jnp.zeros_like(nrm_ref)

    nrm_ref[...] += jnp.sum(jnp.abs(kf), axis=0, keepdims=True) + jnp.sum(jnp.abs(kb), axis=0, keepdims=True)


def _hy_filter(zfeat, w1p, b1, f1, w2, b2, f2, w3, deltas2, *, tm):
    n = zfeat.shape[0]
    nb = n // tm
    wout = w3.shape[1] // 2
    w3p = jnp.pad(w3, ((0, LANES - w3.shape[0]), (0, 0)))
    p = np.arange(tm)[:, None]
    flip = jnp.asarray(np.arange(tm + FILTER_HALO)[None, :] == tm - p, dtype=F32).astype(BF)
    full = lambda i: (0, 0)
    halo_blk = lambda i: (jnp.minimum((i + 1) * (tm // FILTER_HALO), n // FILTER_HALO - 1), 0)
    return pl.pallas_call(
        _hy_filter_kernel,
        grid=(nb,),
        in_specs=[
            pl.BlockSpec((tm, zfeat.shape[1]), lambda i: (i, 0)),
            pl.BlockSpec((FILTER_HALO, zfeat.shape[1]), halo_blk),
            pl.BlockSpec(w1p.shape, full), pl.BlockSpec(b1.shape, full), pl.BlockSpec(f1.shape, full),
            pl.BlockSpec(w2.shape, full), pl.BlockSpec(b2.shape, full), pl.BlockSpec(f2.shape, full),
            pl.BlockSpec((LANES, wout), lambda i: (0, 0)),
            pl.BlockSpec((LANES, wout), lambda i: (0, 1)),
            pl.BlockSpec(deltas2.shape, full),
            pl.BlockSpec(flip.shape, full, pipeline_mode=pl.Buffered(1)),
        ],
        out_specs=[pl.BlockSpec((tm, wout), lambda i: (i, 0)), pl.BlockSpec((tm, wout), lambda i: (nb - 1 - i, 0)),
                   pl.BlockSpec((1, wout), full)],
        out_shape=[jax.ShapeDtypeStruct((n, wout), F32), jax.ShapeDtypeStruct((n, wout), F32),
                   jax.ShapeDtypeStruct((1, wout), F32)],
        compiler_params=_cparams("arbitrary"),
        name="hy_filter",
    )(zfeat, zfeat, w1p, b1, f1, w2, b2, f2, w3p, w3p, deltas2, flip)


def _dft_mats(k_out, r_in, period, sign, scale, n_in, real_out):
    k = np.arange(k_out)[:, None]
    r = np.arange(r_in)[None, :]
    ang = 2.0 * np.pi * ((k * r) % period) / period
    fr = np.cos(ang) * scale
    fi = sign * np.sin(ang) * scale
    if real_out:
        mats = [fr, -fi]
    else:
        mats = [np.concatenate([fr, fi], 0), np.concatenate([-fi, fr], 0)]
    return jnp.asarray(np.stack(mats[:n_in], 0), dtype=F32).astype(BF)


def _twiddle(s_n, k_n, n, sign, sbk):
    s0 = lax.broadcasted_iota(jnp.int32, (s_n // sbk, k_n, LANES), 0) * sbk
    k = lax.broadcasted_iota(jnp.int32, (s_n // sbk, k_n, LANES), 1)
    ang = (s0 * k).astype(F32) * (2.0 * math.pi / n)
    ang1 = lax.broadcasted_iota(jnp.int32, (k_n, LANES), 0).astype(F32) * (2.0 * math.pi / n)
    return jnp.cos(ang), sign * jnp.sin(ang), jnp.cos(ang1), sign * jnp.sin(ang1)


def _unpack_pair(p):
    return [pltpu.unpack_elementwise(p, index=i, packed_dtype=BF, unpacked_dtype=F32) for i in (0, 1)]


def _stage_kernel(*refs, n_in, r_in, k_mid, k_out, sbk, tw, spec, second, gate, real_out,
                  transposed_out, flat, packed_in, packed_spec, packed_out):
    it = iter(refs)
    x_refs = [next(it) for _ in range(n_in)]
    g_ref = next(it)
    g2_ref = next(it) if second else None
    tw_refs = [next(it) for _ in range(4)] if tw else None
    n_spec = 1 if packed_spec else 2
    spec_refs = [next(it) for _ in range(n_spec + 1)] if spec else None
    gate_refs = [next(it) for _ in range(5)] if gate else None
    out_refs = [next(it)] if (real_out or gate or packed_out) else [next(it), next(it)]
    if not flat:
        x_refs = [r.reshape(r_in * sbk, LANES) for r in x_refs]
        if spec:
            spec_refs = [r.reshape(k_mid * sbk, LANES) for r in spec_refs[:n_spec]] + spec_refs[n_spec:]
        if gate:
            gate_refs = [r.reshape(k_out * sbk, LANES) for r in gate_refs[:4]] + gate_refs[4:]
            out_refs = [out_refs[0].reshape(2 * k_out * sbk, LANES)]
        elif not transposed_out:
            out_refs = [r.reshape(k_out * sbk, LANES) for r in out_refs]
    if spec:
        inv = 1.0 / spec_refs[n_spec][...]
    if tw:
        tr, ti = tw_refs[0][0], tw_refs[1][0]
        wr, wi = tw_refs[2][...], tw_refs[3][...]
    def joined_dots(mats_ref, cols):
        acc = None
        for xi in range(len(cols[0])):
            wide = jnp.concatenate([c[xi].astype(BF) for c in cols], axis=1)
            d = _dot(mats_ref[xi], wide)
            acc = d if acc is None else acc + d
        return [acc[:, k * LANES:(k + 1) * LANES] for k in range(len(cols))]

    join = 1 if flat else min(STAGE_JOIN, sbk)
    results = {}
    for j in range(sbk):
        if j % join == 0:
            cols = []
            for jj in range(j, j + join):
                parts = [x_ref[...] if flat else x_ref[pl.ds(jj, r_in, stride=sbk), :] for x_ref in x_refs]
                cols.append(_unpack_pair(parts[0]) if packed_in else parts)
            accs = joined_dots(g_ref, cols)
            if not real_out:
                mids = []
                for jj, acc in zip(range(j, j + join), accs):
                    yr, yi = acc[:k_mid], acc[k_mid:]
                    if spec:
                        rows = slice(None) if flat else pl.ds(jj, k_mid, stride=sbk)
                        if packed_spec:
                            sr, si = _unpack_pair(spec_refs[0][rows, :])
                        else:
                            sr, si = spec_refs[0][rows, :], spec_refs[1][rows, :]
                        sr, si = sr * inv, si * inv
                        yr, yi = yr * sr - yi * si, yr * si + yi * sr
                    mids.append([yr, yi])
                if second:
                    mids = [[acc[:k_out], acc[k_out:]] for acc in joined_dots(g2_ref, mids)]
                accs = mids
            results = dict(zip(range(j, j + join), accs))
        if real_out:
            ys = [results[j]]
        else:
            yr, yi = results[j]
            if tw:
                yr, yi = yr * tr - yi * ti, yr * ti + yi * tr
                if j + 1 < sbk:
                    tr, ti = tr * wr - ti * wi, tr * wi + ti * wr
            ys = [yr, yi]
        if gate:
            o_ref = out_refs[0]
            for part, y in enumerate(ys):
                rows = slice(None) if flat else pl.ds(j, k_out, stride=sbk)
                val = gate_refs[part][rows, :] * (y + gate_refs[4][...] * gate_refs[2 + part][rows, :])
                if flat:
                    o_ref[part] = val.astype(o_ref.dtype)
                else:
                    o_ref[pl.ds(part * k_out * sbk + j, k_out, stride=sbk), :] = val.astype(o_ref.dtype)
            continue
        if packed_out:
            ys = [pltpu.pack_elementwise(ys, packed_dtype=BF)]
        for o_ref, y in zip(out_refs, ys):
            if flat:
                o_ref[...] = y.astype(o_ref.dtype)
            elif transposed_out:
                o_ref[0, j] = y.astype(o_ref.dtype)
            else:
                o_ref[pl.ds(j, k_out, stride=sbk), :] = y.astype(o_ref.dtype)


def _fft_stage(xs, x_sel, gmat, *, r_in, s_n, k_out, n_groups, n_cblk, transposed_out, real_out,
               out_dtype=F32, g2mat=None, tw=None, spec=None, spec_sel=None, gate=None, sbk=STAGE_ROWS,
               packed_in=False, packed_out=False, name="fft_stage"):
    n_in = len(xs)
    flat = s_n == 1
    sbk = 1 if flat else min(sbk, s_n)
    cb = LANES
    in_specs, args = [], []
    for x, sel in zip(xs, x_sel):
        if flat:
            in_specs.append(pl.BlockSpec((None, r_in, cb), lambda s, g, c, sel=sel: (sel(g, c)[0], 0, sel(g, c)[1])))
            args.append(x)
        else:
            xv = x.reshape(x.shape[0], x.shape[1] // s_n, s_n, x.shape[2])
            in_specs.append(pl.BlockSpec((1, r_in, sbk, cb),
                                         lambda s, g, c, sel=sel: (sel(g, c)[0], 0, s, sel(g, c)[1])))
            args.append(xv)
    in_specs.append(pl.BlockSpec(gmat.shape, lambda s, g, c: (0, 0, 0)))
    args.append(gmat)
    k_mid = gmat.shape[1] // (1 if real_out else 2)
    if g2mat is not None:
        in_specs.append(pl.BlockSpec(g2mat.shape, lambda s, g, c: (0, 0, 0)))
        args.append(g2mat)
    if tw is not None:
        for tarr in tw[:2]:
            in_specs.append(pl.BlockSpec((1, k_out, LANES), lambda s, g, c: (s, 0, 0)))
            args.append(tarr)
        for tarr in tw[2:]:
            in_specs.append(pl.BlockSpec((k_out, LANES), lambda s, g, c: (0, 0)))
            args.append(tarr)
    if spec is not None:
        *planes, nrm = spec
        for arr in planes:
            if flat:
                in_specs.append(pl.BlockSpec((k_mid, cb), lambda s, g, c: (0, spec_sel(g, c))))
                args.append(arr)
            else:
                in_specs.append(pl.BlockSpec((1, k_mid, sbk, cb), lambda s, g, c: (0, 0, s, spec_sel(g, c))))
                args.append(arr.reshape(1, k_mid, s_n, arr.shape[-1]))
        in_specs.append(pl.BlockSpec((1, cb), lambda s, g, c: (0, spec_sel(g, c))))
        args.append(nrm)
    if gate is not None:
        (ga, gblk), (za, zblk), bias = gate
        for arr, blk in ((ga, gblk), (za, zblk)):
            for bi in (0, 1):
                if flat:
                    in_specs.append(pl.BlockSpec((None, k_out, cb), lambda s, g, c, bi=bi, blk=blk: (bi, 0, blk + c)))
                    args.append(arr)
                else:
                    in_specs.append(pl.BlockSpec((1, k_out, sbk, cb),
                                                 lambda s, g, c, bi=bi, blk=blk: (bi, 0, s, blk + c)))
                    args.append(arr.reshape(arr.shape[0], k_out, s_n, arr.shape[-1]))
        in_specs.append(pl.BlockSpec((1, cb), lambda s, g, c: (0, c)))
        args.append(bias)
    ctot = n_cblk * cb
    if gate is not None:
        n_groups = 2
        if flat:
            oshape = (2, k_out, ctot)
            ospec = pl.BlockSpec((2, k_out, cb), lambda s, g, c: (0, 0, c))
        else:
            oshape = (2, k_out, s_n, ctot)
            ospec = pl.BlockSpec((2, k_out, sbk, cb), lambda s, g, c: (0, 0, s, c))
    elif flat:
        oshape = (n_groups, k_out, ctot)
        ospec = pl.BlockSpec((None, k_out, cb), lambda s, g, c: (g, 0, c))
    elif transposed_out:
        oshape = (n_groups, s_n, k_out, ctot)
        ospec = pl.BlockSpec((1, sbk, k_out, cb), lambda s, g, c: (g, s, 0, c))
    else:
        oshape = (n_groups, k_out, s_n, ctot)
        ospec = pl.BlockSpec((1, k_out, sbk, cb), lambda s, g, c: (g, 0, s, c))
    n_out = 1 if (real_out or gate is not None or packed_out) else 2
    if packed_out:
        out_dtype = jnp.int32
    kern = functools.partial(_stage_kernel, n_in=n_in, r_in=r_in, k_mid=k_mid, k_out=k_out, sbk=sbk,
                             tw=tw is not None, spec=spec is not None, second=g2mat is not None,
                             gate=gate is not None, real_out=real_out, transposed_out=transposed_out,
                             flat=flat, packed_in=packed_in, packed_spec=spec is not None and len(spec) == 2,
                             packed_out=packed_out)
    n_grid_groups = 1 if gate is not None else n_groups
    outs = pl.pallas_call(
        kern,
        grid=(s_n // sbk, n_grid_groups, n_cblk),
        in_specs=in_specs,
        out_specs=[ospec] * n_out,
        out_shape=[jax.ShapeDtypeStruct(oshape, out_dtype)] * n_out,
        compiler_params=_cparams("parallel", "parallel", "parallel"),
        name=name,
    )(*args)
    return [o.reshape(n_groups, -1, ctot) for o in outs]


def _split_len(n):
    if n <= 1024:
        return n, 1
    s = 128
    return n // s, s


def _fft_forward(xs, x_sel, n, n_rows, *, n_groups, n_cblk, halves=False, name="fwd"):
    n1, s = _split_len(n)
    n_in = len(xs)

    def mats(k_out, r_in, period):
        if not halves:
            return _dft_mats(k_out, r_in, period, -1.0, 1.0, n_in, False)
        g = _dft_mats(k_out, 2 * r_in, period, -1.0, 1.0, 1, False)[0]
        return jnp.stack([g[:, :r_in], g[:, r_in:]], axis=0)

    if s == 1:
        g = mats(n, n_rows, n)
        return _fft_stage(xs, x_sel, g, r_in=n_rows, s_n=1, k_out=n, n_groups=n_groups, n_cblk=n_cblk,
                          transposed_out=False, real_out=False, name=name + "_direct")
    r1 = n_rows // s
    g1 = mats(n1, r1, n1)
    tw = _twiddle(s, n1, n, -1.0, STAGE_ROWS)
    (a,) = _fft_stage(xs, x_sel, g1, r_in=r1, s_n=s, k_out=n1, n_groups=n_groups, n_cblk=n_cblk,
                      transposed_out=True, real_out=False, tw=tw, packed_out=True, name=name + "_s1")
    g2 = _dft_mats(s, s, s, -1.0, 1.0, 2, False)
    return _fft_stage([a], [lambda g, c: (g, c)], g2, r_in=s, s_n=n1, k_out=s, n_groups=n_groups,
                      n_cblk=n_cblk, transposed_out=False, real_out=False, packed_in=True, packed_out=True,
                      name=name + "_s2")


def _hyena(uc, k_halves, nrm, hy_bias):
    b, n, _ = uc.shape
    w = HYENA_WIDTH
    wblk = w // LANES
    nfft = 2 * n
    n1, s = _split_len(nfft)
    ident = lambda g, c: (g, c)
    k_spec = _fft_forward([k[None] for k in k_halves], [lambda g, c: (0, c)] * 2, nfft, n, n_groups=1,
                          n_cblk=HYENA_ORDER * wblk, halves=True, name="hy_filt_fft")
    z, zblk = uc, 2 * wblk
    for o in range(HYENA_ORDER):
        sel_r = lambda g, c, zblk=zblk: (0, zblk + c)
        sel_i = lambda g, c, zblk=zblk: (1, zblk + c)
        spec = tuple(p[0] for p in k_spec) + (nrm,)
        spec_sel = lambda g, c, o=o: o * wblk + c
        gate = ((uc, o * wblk), (z, zblk), hy_bias[o:o + 1])
        common = dict(n_groups=1, n_cblk=wblk, real_out=False)
        if s == 1:
            gf = _dft_mats(nfft, n, nfft, -1.0, 1.0, 2, False)
            gi = _dft_mats(n, nfft, nfft, 1.0, 1.0 / nfft, 2, False)
            (z,) = _fft_stage([z, z], [sel_r, sel_i], gf, r_in=n, s_n=1, k_out=n, transposed_out=False,
                              g2mat=gi, spec=spec, spec_sel=spec_sel, gate=gate, name="hy_direct", **common)
        else:
            r1 = n // s
            g1 = _dft_mats(n1, r1, n1, -1.0, 1.0, 2, False)
            (a,) = _fft_stage([z, z], [sel_r, sel_i], g1, r_in=r1, s_n=s, k_out=n1, transposed_out=True,
                              tw=_twiddle(s, n1, nfft, -1.0, STAGE_ROWS), packed_out=True, name="hy_s1",
                              **common)
            g2 = _dft_mats(s, s, s, -1.0, 1.0, 2, False)
            g3 = _dft_mats(s, s, s, 1.0, 1.0, 2, False)
            (q,) = _fft_stage([a], [ident], g2, r_in=s, s_n=n1, k_out=s, transposed_out=True,
                              g2mat=g3, tw=_twiddle(n1, s, nfft, 1.0, STAGE_ROWS), spec=spec,
                              spec_sel=spec_sel, packed_in=True, packed_out=True, name="hy_mid", **common)
            g4 = _dft_mats(n // s, n1, n1, 1.0, 1.0 / nfft, 2, False)
            (z,) = _fft_stage([q], [ident], g4, r_in=n1, s_n=s, k_out=n // s, transposed_out=False,
                              gate=gate, packed_in=True, name="hy_last", **common)
        zblk = 0
    return z.reshape(b * n, w)


def _fnet(pq):
    b, n, _ = pq.shape
    w = FNET_WIDTH
    wblk = w // LANES
    first = 3 * HYENA_WIDTH // LANES
    scale = 1.0 / math.sqrt(n * FNET_GROUP_DIM)
    sel_r = lambda g, c: (g, first + c)
    sel_i = lambda g, c: (g, first + wblk + c)
    ident = lambda g, c: (g, c)
    n1, s = _split_len(n)
    if s == 1:
        g = _dft_mats(n, n, n, -1.0, scale, 2, True)
        (y,) = _fft_stage([pq, pq], [sel_r, sel_i], g, r_in=n, s_n=1, k_out=n, n_groups=b, n_cblk=wblk,
                          transposed_out=False, real_out=True, name="fnet_direct")
        return y.reshape(b * n, w)
    g1 = _dft_mats(n1, n1, n1, -1.0, 1.0, 2, False)
    tw = _twiddle(s, n1, n, -1.0, STAGE_ROWS)
    (a,) = _fft_stage([pq, pq], [sel_r, sel_i], g1, r_in=n1, s_n=s, k_out=n1, n_groups=b, n_cblk=wblk,
                      transposed_out=True, real_out=False, tw=tw, packed_out=True, name="fnet_s1")
    g2 = _dft_mats(s, s, s, -1.0, scale, 2, True)
    (y,) = _fft_stage([a], [ident], g2, r_in=s, s_n=n1, k_out=s, n_groups=b, n_cblk=wblk,
                      transposed_out=False, real_out=True, packed_in=True, name="fnet_s2")
    return y.reshape(b * n, w)


def _merge_kernel(x_ref, gt_ref, a_ref, h_ref, f_ref, gate_ref, wa_ref, wh_ref, wf_ref, wo_ref, o_ref):
    d = D_MODEL
    m = gate_ref[:, 0:d].astype(F32) * _dot(a_ref[...], wa_ref[...])
    m = m + gate_ref[:, d:2 * d].astype(F32) * _dot(h_ref[...].astype(BF), wh_ref[...])
    m = m + gate_ref[:, 2 * d:3 * d].astype(F32) * _dot(f_ref[...].astype(BF), wf_ref[...])
    y = _dot(m.astype(BF), wo_ref[...])
    o_ref[...] = x_ref[...] + gt_ref[0] * y


def _merge(x2, gt, attn_o, hy_o, fn_o, gates, wa, wh, wf, wo, *, tm, mod_row):
    t, d = x2.shape
    tok = lambda i: (i, 0)
    full = lambda i: (0, 0)
    return pl.pallas_call(
        _merge_kernel,
        grid=(t // tm,),
        in_specs=[
            pl.BlockSpec((tm, d), tok),
            pl.BlockSpec((1, 1, d), lambda i: (mod_row(i), 0, 0)),
            pl.BlockSpec((tm, ATTN_WIDTH), tok),
            pl.BlockSpec((tm, HYENA_WIDTH), tok),
            pl.BlockSpec((tm, FNET_WIDTH), tok),
            pl.BlockSpec((tm, 3 * d), tok),
            pl.BlockSpec(wa.shape, full, pipeline_mode=pl.Buffered(1)),
            pl.BlockSpec(wh.shape, full, pipeline_mode=pl.Buffered(1)),
            pl.BlockSpec(wf.shape, full, pipeline_mode=pl.Buffered(1)),
            pl.BlockSpec(wo.shape, full, pipeline_mode=pl.Buffered(1)),
        ],
        out_specs=pl.BlockSpec((tm, d), tok),
        out_shape=jax.ShapeDtypeStruct((t, d), F32),
        compiler_params=_cparams("parallel"),
        name="merge",
    )(x2, gt, attn_o, hy_o, fn_o, gates, wa, wh, wf, wo)


def _ffn_kernel(x_ref, sh_ref, sc_ref, gt_ref, g_ref, wg_ref, wu_ref, wd_ref, o_ref, h_scr, acc_scr):
    f = pl.program_id(1)

    @pl.when(f == 0)
    def _():
        h_scr[...] = _modulated_norm(x_ref[...], g_ref[...], sc_ref[0], sh_ref[0]).astype(BF)
        acc_scr[...] = jnp.zeros_like(acc_scr)

    hb = h_scr[...]
    act = _silu(_dot(hb, wg_ref[...])) * _dot(hb, wu_ref[...])
    acc_scr[...] += _dot(act.astype(BF), wd_ref[...])

    @pl.when(f == pl.num_programs(1) - 1)
    def _():
        o_ref[...] = x_ref[...] + gt_ref[0] * acc_scr[...]


def _ffn(x2, sh, sc, gt, g, wg, wu, wd, *, tm, tf, mod_row):
    t, d = x2.shape
    ff = wg.shape[1]
    row3 = lambda i, f: (mod_row(i), 0, 0)
    wmode = dict(pipeline_mode=pl.Buffered(1)) if tf == ff else {}
    return pl.pallas_call(
        _ffn_kernel,
        grid=(t // tm, ff // tf),
        in_specs=[
            pl.BlockSpec((tm, d), lambda i, f: (i, 0)),
            pl.BlockSpec((1, 1, d), row3), pl.BlockSpec((1, 1, d), row3), pl.BlockSpec((1, 1, d), row3),
            pl.BlockSpec((1, d), lambda i, f: (0, 0)),
            pl.BlockSpec((d, tf), lambda i, f: (0, f), **wmode),
            pl.BlockSpec((d, tf), lambda i, f: (0, f), **wmode),
            pl.BlockSpec((tf, d), lambda i, f: (f, 0), **wmode),
        ],
        out_specs=pl.BlockSpec((tm, d), lambda i, f: (i, 0)),
        out_shape=jax.ShapeDtypeStruct((t, d), F32),
        scratch_shapes=[pltpu.VMEM((tm, d), BF), pltpu.VMEM((tm, d), F32)],
        compiler_params=_cparams("parallel", "arbitrary"),
        name="ffn_dense",
    )(x2, sh, sc, gt, g, wg, wu, wd)


def _top2(logits):
    lane = lax.broadcasted_iota(jnp.int32, logits.shape, 1)
    lg = jnp.where(lane < N_EXPERTS, logits, -jnp.inf)
    m1 = jnp.max(lg, axis=1, keepdims=True)
    i1 = jnp.min(jnp.where(lg == m1, lane, LANES), axis=1, keepdims=True)
    lg2 = jnp.where(lane == i1, -jnp.inf, lg)
    m2 = jnp.max(lg2, axis=1, keepdims=True)
    i2 = jnp.min(jnp.where(lg2 == m2, lane, LANES), axis=1, keepdims=True)
    e2 = jnp.exp(m2 - m1)
    w1 = 1.0 / (1.0 + e2)
    return i1, i2, w1, e2 * w1


GROUP_TILE = 256
GROUP_PAD = 32
GROUP_PIECES = GROUP_TILE // GROUP_PAD
SLOT_RADIX = 64.0


def _moe_group_kernel(x_ref, sh_ref, sc_ref, g_ref, wr_ref, tri_ref, xg_ref, ws_ref, slot_ref, cnt_ref,
                      h_scr, rows_scr, wm_scr):
    j = pl.program_id(1)
    tb = x_ref.shape[0]
    gt_rows = xg_ref.shape[1]

    @pl.when(j == 0)
    def _():
        h = _modulated_norm(x_ref[...], g_ref[...], sc_ref[0], sh_ref[0])
        h_scr[...] = h.astype(BF)
        i1, i2, w1, w2 = _top2(_dot3(h, wr_ref[...]))
        lane = lax.broadcasted_iota(jnp.int32, (tb, LANES), 1)
        oh0 = jnp.where(lane == i1, 1.0, 0.0)
        oh1 = jnp.where(lane == i2, 1.0, 0.0)
        c0 = jnp.sum(oh0, axis=0, keepdims=True)
        cnt = c0 + jnp.sum(oh1, axis=0, keepdims=True)
        tri = tri_ref[...]
        pre0 = _dot(tri, oh0.astype(BF))
        pre1 = _dot(tri, oh1.astype(BF)) + c0
        tiles = jnp.ceil(cnt * (1.0 / GROUP_PAD))
        upper = jnp.where(lax.broadcasted_iota(jnp.int32, (LANES, LANES), 0)
                          < lax.broadcasted_iota(jnp.int32, (LANES, LANES), 1), 1.0, 0.0).astype(BF)
        off = _dot(jnp.broadcast_to(tiles, (8, LANES)).astype(BF), upper)[0:1] * float(GROUP_PAD)
        slot0 = jnp.sum(oh0 * (off + pre0), axis=1, keepdims=True)
        slot1 = jnp.sum(oh1 * (off + pre1), axis=1, keepdims=True)
        slot_ref[0] = jnp.where(lane == 0, slot0, jnp.where(lane == 1, slot1, 0.0))
        cnt_ref[0] = jnp.broadcast_to(cnt, (8, LANES))
        hi0 = jnp.floor(slot0 * (1.0 / SLOT_RADIX))
        hi1 = jnp.floor(slot1 * (1.0 / SLOT_RADIX))
        digits = jnp.where(lane == 0, hi0, jnp.where(lane == 1, slot0 - SLOT_RADIX * hi0,
                           jnp.where(lane == 2, hi1, jnp.where(lane == 3, slot1 - SLOT_RADIX * hi1, 0.0))))
        sel = jnp.where(lax.broadcasted_iota(jnp.int32, (8, LANES), 0)
                        == lax.broadcasted_iota(jnp.int32, (8, LANES), 1), 1.0, 0.0).astype(BF)
        rows_scr[...] = _dot_nt(sel, digits.astype(BF))
        w1h, w1l = _split(w1)
        w1m, w1l = _split(w1 - w1h.astype(F32))
        w2h, w2l = _split(w2)
        w2m, w2l = _split(w2 - w2h.astype(F32))
        cols = [w1h, w1m, w1l, w2h, w2m, w2l]
        wm = jnp.zeros((tb, LANES), F32)
        for li, col in enumerate(cols):
            wm = jnp.where(lane == li, col.astype(F32), wm)
        wm_scr[...] = wm.astype(BF)

    rows = rows_scr[...]
    s0 = rows[0:1] * SLOT_RADIX + rows[1:2]
    s1 = rows[2:3] * SLOT_RADIX + rows[3:4]
    pos = (lax.broadcasted_iota(jnp.int32, (gt_rows, tb), 0) + j * gt_rows).astype(F32)
    g0 = jnp.where(pos == s0, 1.0, 0.0).astype(BF)
    g1 = jnp.where(pos == s1, 1.0, 0.0).astype(BF)
    xg_ref[0] = _dot(g0 + g1, h_scr[...]).astype(BF)
    lane_w = lax.broadcasted_iota(jnp.int32, (gt_rows, LANES), 1)
    wsum = (jnp.where(lane_w < 3, _dot(g0, wm_scr[...]), 0.0)
            + jnp.where((lane_w >= 3) & (lane_w < 6), _dot(g1, wm_scr[...]), 0.0))
    ws_ref[0] = jnp.broadcast_to(jnp.sum(wsum, axis=1, keepdims=True), (gt_rows, LANES))


def _moe_group(x2, sh, sc, g, wr_pad, *, tb, nt, mod_row):
    t, d = x2.shape
    nb = t // tb
    row3 = lambda b, j: (mod_row(b), 0, 0)
    tri = jnp.asarray(np.tril(np.ones((tb, tb), np.float32), -1)).astype(BF)
    return pl.pallas_call(
        _moe_group_kernel,
        grid=(nb, nt),
        in_specs=[
            pl.BlockSpec((tb, d), lambda b, j: (b, 0)),
            pl.BlockSpec((1, 1, d), row3), pl.BlockSpec((1, 1, d), row3),
            pl.BlockSpec((1, d), lambda b, j: (0, 0)),
            pl.BlockSpec((d, LANES), lambda b, j: (0, 0)),
            pl.BlockSpec((tb, tb), lambda b, j: (0, 0), pipeline_mode=pl.Buffered(1)),
        ],
        out_specs=[
            pl.BlockSpec((1, GROUP_TILE, d), lambda b, j: (b * nt + j, 0, 0)),
            pl.BlockSpec((1, GROUP_TILE, LANES), lambda b, j: (b * nt + j, 0, 0)),
            pl.BlockSpec((1, tb, LANES), lambda b, j: (b, 0, 0)),
            pl.BlockSpec((1, 8, LANES), lambda b, j: (b, 0, 0)),
        ],
        out_shape=[
            jax.ShapeDtypeStruct((nb * nt, GROUP_TILE, d), BF),
            jax.ShapeDtypeStruct((nb * nt, GROUP_TILE, LANES), F32),
            jax.ShapeDtypeStruct((nb, tb, LANES), F32),
            jax.ShapeDtypeStruct((nb, 8, LANES), F32),
        ],
        scratch_shapes=[pltpu.VMEM((tb, d), BF), pltpu.VMEM((8, tb), F32), pltpu.VMEM((tb, LANES), BF)],
        compiler_params=_cparams("parallel", "arbitrary"),
        name="moe_group",
    )(x2, sh, sc, g, wr_pad, tri)


def _moe_schedule(cnt, nh):
    np_ = GROUP_PIECES
    h = (cnt + GROUP_PAD - 1) // GROUP_PAD
    nb, ne = h.shape
    tot = h.sum(0)
    pairs = (tot + np_ - 1) // np_
    cum_p = jnp.cumsum(pairs)
    start_p = cum_p - pairs
    n_used = cum_p[-1]
    n_steps = (nb * nh + ne * (np_ - 1)) // np_
    q = jnp.minimum(jnp.arange(n_steps, dtype=jnp.int32), n_used - 1)
    e = jnp.sum(q[:, None] >= cum_p[None, :], axis=1).astype(jnp.int32)
    r = q - start_p[e]
    cum_b = jnp.cumsum(h, axis=0)
    first = jnp.cumsum(h, axis=1) - h

    def piece(idx):
        idx = jnp.minimum(idx, tot[e] - 1)
        blk = jnp.sum(idx[:, None] >= cum_b.T[e], axis=1).astype(jnp.int32)
        return blk * nh + first[blk, e] + idx - (cum_b[blk, e] - h[blk, e])

    x = jnp.arange(nh, dtype=jnp.int32)[None, :]
    ex = jnp.sum(x[:, :, None] >= jnp.cumsum(h, axis=1)[:, None, :], axis=2).astype(jnp.int32)
    exc = jnp.minimum(ex, ne - 1)
    g = jnp.take_along_axis(cum_b - h, exc, axis=1) + x - jnp.take_along_axis(first, exc, axis=1)
    loc = np_ * (start_p[exc] + g // np_) + g % np_
    loc = jnp.where(ex < ne, loc, loc[:, 0:1])
    pieces = jnp.concatenate([piece(np_ * r + k) for k in range(np_)]).astype(jnp.int32)
    return (pieces, e, n_used.astype(jnp.int32).reshape(1), loc.reshape(-1).astype(jnp.int32),
            ((h.sum(1) + np_ - 1) // np_).astype(jnp.int32))


def _moe_expert_kernel(pc_ref, exp_ref, nused_ref, *refs):
    np_ = GROUP_PIECES
    x_refs, w_refs = refs[:np_], refs[np_:2 * np_]
    wg_ref, wu_ref, wd_ref, y_ref = refs[2 * np_:]

    @pl.when(pl.program_id(0) < nused_ref[0])
    def _():
        x = jnp.concatenate([r[0] for r in x_refs], axis=0)
        act = _silu(_dot(x, wg_ref[0])) * _dot(x, wu_ref[0])
        y = _dot(act.astype(BF), wd_ref[0])
        w = jnp.concatenate([r[0] for r in w_refs], axis=0)
        y_ref[0] = (y * jnp.concatenate([w] * (y.shape[1] // LANES), axis=1)).astype(BF)


def _moe_experts(pieces, step_exp, n_used, xg, ws, wg, wu, wd):
    d = xg.shape[-1]
    ff = wg.shape[2]
    np_ = GROUP_PIECES
    n_steps = step_exp.shape[0]
    xh = xg.reshape(-1, GROUP_PAD, d)
    wh = ws.reshape(-1, GROUP_PAD, LANES)
    pc3 = lambda k: (lambda i, pc, se, nu: (pc[k * n_steps + i], 0, 0))
    exp3 = lambda i, pc, se, nu: (se[i], 0, 0)
    return pl.pallas_call(
        _moe_expert_kernel,
        grid_spec=pltpu.PrefetchScalarGridSpec(
            num_scalar_prefetch=3,
            grid=(n_steps,),
            in_specs=(
                [pl.BlockSpec((1, GROUP_PAD, d), pc3(k)) for k in range(np_)]
                + [pl.BlockSpec((1, GROUP_PAD, LANES), pc3(k)) for k in range(np_)]
                + [pl.BlockSpec((1, d, ff), exp3), pl.BlockSpec((1, d, ff), exp3), pl.BlockSpec((1, ff, d), exp3)]
            ),
            out_specs=pl.BlockSpec((1, GROUP_TILE, d), lambda i, pc, se, nu: (jnp.minimum(i, nu[0] - 1), 0, 0)),
        ),
        out_shape=jax.ShapeDtypeStruct((n_steps, GROUP_TILE, d), BF),
        compiler_params=_cparams("arbitrary"),
        name="moe_experts",
    )(pieces, step_exp, n_used, *([xh] * np_), *([wh] * np_), wg, wu, wd)


def _moe_combine_kernel(nt_ref, loc_ref, x_ref, gt_ref, slot_ref, *refs):
    y_refs, o_ref, acc_scr = refs[:GROUP_PIECES], refs[GROUP_PIECES], refs[GROUP_PIECES + 1]
    b = pl.program_id(0)
    j = pl.program_id(1)
    tb = x_ref.shape[0]

    @pl.when(j == 0)
    def _():
        acc_scr[...] = jnp.zeros_like(acc_scr)

    @pl.when(j < nt_ref[b])
    def _():
        sl = slot_ref[0]
        pos = (lax.broadcasted_iota(jnp.int32, (tb, GROUP_TILE), 1) + j * GROUP_TILE).astype(F32)
        p = jnp.where((pos == sl[:, 0:1]) | (pos == sl[:, 1:2]), 1.0, 0.0).astype(BF)
        acc_scr[...] += _dot(p, jnp.concatenate([r[0] for r in y_refs], axis=0))

    @pl.when(j == pl.num_programs(1) - 1)
    def _():
        o_ref[...] = x_ref[...] + gt_ref[0] * acc_scr[...]


def _moe_combine(ntiles_b, loc, x2, gt, slots, yg, *, tb, nt, mod_row):
    t, d = x2.shape
    nb = t // tb
    np_ = GROUP_PIECES
    yh = yg.reshape(-1, GROUP_PAD, d)

    def piece3(k):
        return lambda b, j, n, lc: (lc[(b * nt + jnp.minimum(j, n[b] - 1)) * np_ + k], 0, 0)

    return pl.pallas_call(
        _moe_combine_kernel,
        grid_spec=pltpu.PrefetchScalarGridSpec(
            num_scalar_prefetch=2,
            grid=(nb, nt),
            in_specs=[
                pl.BlockSpec((tb, d), lambda b, j, n, lc: (b, 0)),
                pl.BlockSpec((1, 1, d), lambda b, j, n, lc: (mod_row(b), 0, 0)),
                pl.BlockSpec((1, tb, LANES), lambda b, j, n, lc: (b, 0, 0)),
            ] + [pl.BlockSpec((1, GROUP_PAD, d), piece3(k)) for k in range(np_)],
            out_specs=pl.BlockSpec((tb, d), lambda b, j, n, lc: (b, 0)),
            scratch_shapes=[pltpu.VMEM((tb, d), F32)],
        ),
        out_shape=jax.ShapeDtypeStruct((t, d), F32),
        compiler_params=_cparams("parallel", "arbitrary"),
        name="moe_combine",
    )(ntiles_b, loc, x2, gt, slots, *([yh] * np_))


def _moe(x2, sh, sc, gt, g, wr_pad, wg, wu, wd, *, tb, mod_row):
    nt = -(-(2 * tb + N_EXPERTS * (GROUP_PAD - 1)) // GROUP_TILE)
    xg, ws, slots, cnt = _moe_group(x2, sh, sc, g, wr_pad, tb=tb, nt=nt, mod_row=mod_row)
    counts = cnt[:, 0, :N_EXPERTS].astype(jnp.int32)
    pieces, step_exp, n_used, loc, ntiles_b = _moe_schedule(counts, nt * GROUP_PIECES)
    yg = _moe_experts(pieces, step_exp, n_used, xg, ws, wg, wu, wd)
    return _moe_combine(ntiles_b, loc, x2, gt, slots, yg, tb=tb, nt=nt, mod_row=mod_row)


def _rope_tables(seq_len):
    pos = np.arange(seq_len)
    prow = (pos // GRID_W).astype(np.float32)
    pcol = (pos % GRID_W).astype(np.float32)
    n_freq = HEAD_DIM // 4
    inv = (np.float32(ROPE_THETA) ** (-np.arange(n_freq, dtype=np.float32) / n_freq)).astype(np.float32)
    ar = (prow[:, None] * inv[None, :]).astype(np.float64)
    ac = (pcol[:, None] * inv[None, :]).astype(np.float64)
    cos = np.concatenate([np.cos(ar)] * 2 + [np.cos(ac)] * 2, axis=1)
    sin = np.concatenate([-np.sin(ar), np.sin(ar), -np.sin(ac), np.sin(ac)], axis=1)
    return (jnp.asarray(np.concatenate([cos, cos], axis=1), dtype=F32),
            jnp.asarray(np.concatenate([sin, sin], axis=1), dtype=F32))


def _head_sum_matrix():
    c = np.arange(ATTN_WIDTH)
    return jnp.asarray((c[:, None] // HEAD_DIM) == (c[None, :] // HEAD_DIM), dtype=F32).astype(BF)


def _fnet_channel_matrix():
    c = np.arange(FNET_WIDTH)
    same = (c[:, None] // FNET_GROUP_DIM) == (c[None, :] // FNET_GROUP_DIM)
    ang = 2.0 * np.pi * (((c[:, None] % FNET_GROUP_DIM) * (c[None, :] % FNET_GROUP_DIM)) % FNET_GROUP_DIM) / FNET_GROUP_DIM
    cb = np.where(same, np.cos(ang), 0.0)
    sb = np.where(same, np.sin(ang), 0.0)
    return jnp.asarray(np.concatenate([cb, -sb], axis=1), dtype=F32).astype(BF)


def _filter_features(n):
    t = np.linspace(0.0, 1.0, n)[:, None]
    w = 2.0 * np.pi * np.arange(n)[:, None] / n
    fb = np.linspace(1e-4, FILTER_BANDS - 1, FILTER_BANDS)
    z = np.concatenate([t, np.cos(fb * w), -np.sin(fb * w)], axis=-1)
    return jnp.asarray(np.pad(z, ((0, 0), (0, 64 - z.shape[1]))), dtype=F32)


def _decay_rates():
    d = jnp.abs(jnp.linspace(math.log(DECAY_TARGET) / SLOW_DECAY_PCT, math.log(DECAY_TARGET) / FAST_DECAY_PCT,
                             HYENA_WIDTH, dtype=F32))
    return jnp.concatenate([d] * HYENA_ORDER)[None, :]


def kernel(x, c, ctx, c_ctx, w_ada, b_ada, norm1_g, norm2_g, w_in, q_norm_g, k_norm_g, attn_sink,
           hy_conv_w, hy_conv_b, hy_filt_w1, hy_filt_b1, hy_filt_freq1, hy_filt_w2, hy_filt_b2,
           hy_filt_freq2, hy_filt_w3, hy_bias, w_proj_attn, w_proj_hyena, w_proj_fnet, w_out,
           ffn_w_gate, ffn_w_up, ffn_w_down, moe_router, moe_w_gate, moe_w_up, moe_w_down):
    b, seq, d = x.shape
    n_ctx = ctx.shape[1]
    depth = w_ada.shape[0]
    tm = 512
    tiles_per_seq = seq // tm

    cond8 = jnp.concatenate([c, c_ctx[None, :], jnp.zeros((8 - b - 1, d), F32)], axis=0)
    mods = _adaln(cond8, w_ada, b_ada)

    cos_l, sin_l = _rope_tables(seq)
    cos_c = jnp.ones((n_ctx, LANES), F32)
    sin_c = jnp.zeros((n_ctx, LANES), F32)
    gsum = _head_sum_matrix()
    mfn = _fnet_channel_matrix()
    deltas = _decay_rates()
    zfeat_l = _filter_features(seq)
    zfeat_c = _filter_features(n_ctx)

    lat_row = lambda i: i // tiles_per_seq
    ctx_row = lambda i: b
    lat_tab = lambda i: i % tiles_per_seq
    ctx_tab = lambda i: 0
    tm_c = min(tm, n_ctx)

    xs = x.reshape(b * seq, d)
    cs = ctx.reshape(b * n_ctx, d)
    for l in range(depth):
        last = l == depth - 1
        mod = lambda j: mods[l, :, j * d:(j + 1) * d].reshape(8, 1, d)
        w_in_bf = _layer_bf16(w_in, l, 256)
        qg = jnp.tile(q_norm_g[l], N_HEADS)[None, :]
        kg = jnp.tile(k_norm_g[l], N_KV_HEADS)[None, :]
        g1 = norm1_g[l][None, :]
        wa, wh, wf, wo = (w_proj_attn[l].astype(BF), w_proj_hyena[l].astype(BF),
                          w_proj_fnet[l].astype(BF), w_out[l].astype(BF))
        conv_w = hy_conv_w[l].reshape(3, -1)
        conv_b = hy_conv_b[l][None, :]
        w1p = jnp.pad(hy_filt_w1[l], ((0, 64 - hy_filt_w1.shape[1]), (0, 0)))
        filt = (w1p, hy_filt_b1[l][None, :], hy_filt_freq1[l][None, :], hy_filt_w2[l],
                hy_filt_b2[l][None, :], hy_filt_freq2[l][None, :], hy_filt_w3[l], deltas)

        qkv_c, upq_c, gates_c = _phase_a(
            cs, mod(0), mod(1), g1, w_in_bf, cos_c, sin_c, qg, kg, gsum, mfn, conv_w, conv_b,
            tm=tm_c, tiles_per_seq=n_ctx // tm_c, mod_row=ctx_row, tab_row=ctx_tab)
        qkv_c = qkv_c.reshape(b, n_ctx, -1)
        upq_c = upq_c.reshape(b, n_ctx, -1)

        qkv_l, upq_l, gates_l = _phase_a(
            xs, mod(0), mod(1), g1, w_in_bf, cos_l, sin_l, qg, kg, gsum, mfn, conv_w, conv_b,
            tm=tm, tiles_per_seq=tiles_per_seq, mod_row=lat_row, tab_row=lat_tab)
        upq_l = upq_l.reshape(b, seq, -1)
        attn_l = _attention(attn_sink[l], qkv_l.reshape(b, seq, -1), qkv_c, local=True, tq=512)
        kf_l, kb_l, nrm_l = _hy_filter(zfeat_l, *filt, tm=1024)
        hy_l = _hyena(upq_l, (kf_l, kb_l), nrm_l, hy_bias[l])
        fn_l = _fnet(upq_l)
        xs = _merge(xs, mod(2), attn_l.reshape(b * seq, -1), hy_l, fn_l, gates_l, wa, wh, wf, wo,
                    tm=tm, mod_row=lat_row)

        if not last:
            attn_c = _attention(attn_sink[l], qkv_c, qkv_c, local=False, tq=n_ctx)
            kf_c, kb_c, nrm_c = _hy_filter(zfeat_c, *filt, tm=n_ctx)
            hy_c = _hyena(upq_c, (kf_c, kb_c), nrm_c, hy_bias[l])
            fn_c = _fnet(upq_c)
            cs = _merge(cs, mod(2), attn_c.reshape(b * n_ctx, -1), hy_c, fn_c, gates_c, wa, wh, wf, wo,
                        tm=tm_c, mod_row=ctx_row)

        g2 = norm2_g[l][None, :]
        i = l // 2
        if l % 2 == 0:
            wg, wu, wd = (_layer_bf16(ffn_w_gate, i, 512), _layer_bf16(ffn_w_up, i, 512),
                          _layer_bf16(ffn_w_down, i, D_FF // 4))
            run = lambda t2, rows, tmm: _ffn(t2, mod(3), mod(4), mod(5), g2, wg, wu, wd,
                                             tm=tmm, tf=D_FF, mod_row=rows)
        else:
            wr = jnp.pad(moe_router[i], ((0, 0), (0, LANES - N_EXPERTS)))
            wg, wu, wd = (_layer_bf16(moe_w_gate, i, 512), _layer_bf16(moe_w_up, i, 512),
                          _layer_bf16(moe_w_down, i, D_FF // 4))
            run = lambda t2, rows, tmm: _moe(t2, mod(3), mod(4), mod(5), g2, wr, wg, wu, wd,
                                             tb=tmm, mod_row=rows)
        tm_ffn = 512 if l % 2 == 0 else 1024
        xs = run(xs, lambda t: t // (seq // tm_ffn), tm_ffn)
        if not last:
            cs = run(cs, ctx_row, min(tm_ffn, b * n_ctx))
    return xs.reshape(b, seq, d)
```

```python
import functools
import math

import numpy as np
import jax
import jax.numpy as jnp
from jax import lax
from jax.experimental import pallas as pl
from jax.experimental.pallas import tpu as pltpu

F32 = jnp.float32
BF = jnp.bfloat16

D_MODEL = 1024
DEPTH = 4
GRID_W = 64
HEAD_DIM = 64
N_HEADS = 8
N_KV_HEADS = 2
ATTN_WIDTH = N_HEADS * HEAD_DIM
KV_WIDTH = N_KV_HEADS * HEAD_DIM
WINDOW = 128
QBLK = 128
ROPE_THETA = 10000.0
HYENA_ORDER = 2
HYENA_WIDTH = 256
FILTER_BANDS = 16
FILTER_HIDDEN = 64
DECAY_TARGET = 1e-2
FAST_DECAY_PCT = 0.3
SLOW_DECAY_PCT = 1.5
FNET_WIDTH = 256
FNET_GROUP_DIM = 64
Q_END = ATTN_WIDTH
K_END = Q_END + KV_WIDTH
V_END = K_END + KV_WIDTH
HY_END = V_END + (HYENA_ORDER + 1) * HYENA_WIDTH
FN_END = HY_END + FNET_WIDTH
IN_WIDTH = FN_END + 3 * D_MODEL
D_FF = 2816
N_EXPERTS = 8
EPS = 1e-6
LANES = 128
NEG = -1e30
STAGE_ROWS = 16
STAGE_JOIN = 2

VMEM_LIMIT = 56 * 1024 * 1024


def _cparams(*sem):
    return pltpu.CompilerParams(dimension_semantics=sem, vmem_limit_bytes=VMEM_LIMIT)


def _dot(a, b):
    return jnp.dot(a, b, preferred_element_type=F32)


def _dot_nt(a, b):
    return lax.dot_general(a, b, (((1,), (1,)), ((), ())), preferred_element_type=F32)


def _split(a):
    hi = a.astype(BF)
    lo = (a - hi.astype(F32)).astype(BF)
    return hi, lo


def _dot3(a, b):
    ah, al = _split(a)
    bh, bl = _split(b)
    return _dot(ah, bh) + (_dot(ah, bl) + _dot(al, bh))


def _dot2(a, b_bf16):
    ah, al = _split(a)
    return _dot(ah, b_bf16) + _dot(al, b_bf16)


def _sigmoid(v):
    return 0.5 * jnp.tanh(0.5 * v) + 0.5


def _silu(v):
    return v * _sigmoid(v)


def _cast_kernel(w_ref, o_ref):
    o_ref[...] = w_ref[...].astype(BF)


def _layer_bf16(w, layer, rows):
    a, b = w.shape[-2:]
    inner = int(np.prod(w.shape[1:-2], dtype=np.int64))
    w3 = w.reshape(-1, a, b)
    out = pl.pallas_call(
        _cast_kernel,
        grid=(inner, a // rows),
        in_specs=[pl.BlockSpec((1, rows, b), lambda m, i: (layer * inner + m, i, 0))],
        out_specs=pl.BlockSpec((1, rows, b), lambda m, i: (m, i, 0)),
        out_shape=jax.ShapeDtypeStruct((inner, a, b), BF),
        compiler_params=_cparams("parallel", "parallel"),
        name="cast_bf16",
    )(w3)
    return out.reshape(w.shape[1:])


def _adaln_kernel(c_ref, w_ref, b_ref, o_ref):
    o_ref[0] = _dot3(_silu(c_ref[...]), w_ref[0]) + b_ref[0]


def _adaln(cond8, w_ada, b_ada):
    depth, d, n6 = w_ada.shape
    tn = 1024
    return pl.pallas_call(
        _adaln_kernel,
        grid=(depth, n6 // tn),
        in_specs=[
            pl.BlockSpec((8, d), lambda l, j: (0, 0)),
            pl.BlockSpec((1, d, tn), lambda l, j: (l, 0, j)),
            pl.BlockSpec((1, 1, tn), lambda l, j: (l, 0, j)),
        ],
        out_specs=pl.BlockSpec((1, 8, tn), lambda l, j: (l, 0, j)),
        out_shape=jax.ShapeDtypeStruct((depth, 8, n6), F32),
        compiler_params=_cparams("parallel", "parallel"),
        name="adaln",
    )(cond8, w_ada, b_ada.reshape(depth, 1, n6))


def _modulated_norm(x, g, sc, sh):
    ms = jnp.mean(x * x, axis=-1, keepdims=True)
    h = (x * lax.rsqrt(ms + EPS)) * g
    return h * (1.0 + sc) + sh


HALO = 16


def _phase_a_kernel(x_ref, xp_ref, xn_ref, sh_ref, sc_ref, g_ref, w_ref, cos_ref, sin_ref, qg_ref, kg_ref,
                    gsum_ref, mfn_ref, cw_ref, cb_ref, qkv_ref, upq_ref, gate_ref, *, tiles_per_seq):
    tm = x_ref.shape[0]
    norm = lambda xv: _modulated_norm(xv, g_ref[...], sc_ref[0], sh_ref[0]).astype(BF)
    hb = norm(x_ref[...])
    cos = cos_ref[...]
    sin = sin_ref[...]

    def headnorm(t, gain, gs):
        ss = _dot2(t * t, gs)
        return t * lax.rsqrt(ss * (1.0 / HEAD_DIM) + EPS) * gain

    def rope(t, cosw, sinw):
        w = t.shape[1]
        nxt = pltpu.roll(t, w - 16, axis=1)
        prv = pltpu.roll(t, 16, axis=1)
        lw = lax.broadcasted_iota(jnp.int32, t.shape, 1)
        return t * cosw + jnp.where((lw % 32) < 16, nxt, prv) * sinw

    def dup_halves(t):
        lane = lax.broadcasted_iota(jnp.int32, t.shape, 1)
        sw = pltpu.roll(t, 64, axis=1)
        lo = lane < 64
        return jnp.concatenate([jnp.where(lo, t, sw), jnp.where(lo, sw, t)], axis=1)

    pr = _dot(hb, w_ref[:, 0:FN_END])
    pq = pr[:, 0:Q_END]
    qn = headnorm(pq, qg_ref[...], gsum_ref[...])
    cos4 = jnp.concatenate([cos] * 4, axis=1)
    sin4 = jnp.concatenate([sin] * 4, axis=1)
    qkv_ref[:, 0:Q_END] = (rope(qn, cos4, sin4) * (HEAD_DIM ** -0.5)).astype(BF)

    kn = headnorm(pr[:, Q_END:K_END], kg_ref[...], gsum_ref[0:KV_WIDTH, 0:KV_WIDTH])
    qkv_ref[:, Q_END:Q_END + 2 * KV_WIDTH] = dup_halves(rope(kn, cos, sin)).astype(BF)
    qkv_ref[:, Q_END + 2 * KV_WIDTH:] = dup_halves(pr[:, K_END:V_END]).astype(BF)

    w_hy = w_ref[:, V_END:HY_END]
    u = pr[:, V_END:HY_END]
    tile = pl.program_id(0) % tiles_per_seq
    u_before = jnp.where(tile == 0, 0.0, _dot(norm(xp_ref[...]), w_hy)[HALO - 1:HALO])
    u_after = jnp.where(tile == tiles_per_seq - 1, 0.0, _dot(norm(xn_ref[...]), w_hy)[0:1])
    row = lax.broadcasted_iota(jnp.int32, (tm, 1), 0)
    prv = jnp.where(row == 0, u_before, pltpu.roll(u, 1, axis=0))
    nxt = jnp.where(row == tm - 1, u_after, pltpu.roll(u, tm - 1, axis=0))
    n_hy = HY_END - V_END
    upq_ref[:, 0:n_hy] = prv * cw_ref[0:1, :] + u * cw_ref[1:2, :] + nxt * cw_ref[2:3, :] + cb_ref[...]

    upq_ref[:, n_hy:] = _dot(pr[:, HY_END:FN_END].astype(BF), mfn_ref[...])
    gate_ref[...] = _sigmoid(_dot(hb, w_ref[:, FN_END:])).astype(BF)


def _phase_a(x2, sh, sc, g, w_in_bf, cos_t, sin_t, qg, kg, gsum, mfn, conv_w, conv_b, *, tm, tiles_per_seq,
             mod_row, tab_row):
    t, d = x2.shape
    row3 = lambda i: (mod_row(i), 0, 0)
    full = lambda i: (0, 0)
    tok = lambda i: (i, 0)
    n_halo = t // HALO
    outs = [
        ((t, ATTN_WIDTH + 4 * KV_WIDTH), BF), ((t, 3 * HYENA_WIDTH + 2 * FNET_WIDTH), F32),
        ((t, 3 * D_MODEL), BF),
    ]
    kern = functools.partial(_phase_a_kernel, tiles_per_seq=tiles_per_seq)
    return pl.pallas_call(
        kern,
        grid=(t // tm,),
        in_specs=[
            pl.BlockSpec((tm, d), tok),
            pl.BlockSpec((HALO, d), lambda i: (jnp.maximum(i * (tm // HALO) - 1, 0), 0)),
            pl.BlockSpec((HALO, d), lambda i: (jnp.minimum((i + 1) * (tm // HALO), n_halo - 1), 0)),
            pl.BlockSpec((1, 1, d), row3),
            pl.BlockSpec((1, 1, d), row3),
            pl.BlockSpec((1, d), full),
            pl.BlockSpec((d, IN_WIDTH), full, pipeline_mode=pl.Buffered(1)),
            pl.BlockSpec((tm, LANES), lambda i: (tab_row(i), 0)),
            pl.BlockSpec((tm, LANES), lambda i: (tab_row(i), 0)),
            pl.BlockSpec((1, ATTN_WIDTH), full),
            pl.BlockSpec((1, KV_WIDTH), full),
            pl.BlockSpec((ATTN_WIDTH, ATTN_WIDTH), full),
            pl.BlockSpec((FNET_WIDTH, 2 * FNET_WIDTH), full),
            pl.BlockSpec(conv_w.shape, full),
            pl.BlockSpec(conv_b.shape, full),
        ],
        out_specs=[pl.BlockSpec((tm, s[1]), tok) for s, _ in outs],
        out_shape=[jax.ShapeDtypeStruct(s, dt) for s, dt in outs],
        compiler_params=_cparams("parallel"),
        name="phase_a",
    )(x2, x2, x2, sh, sc, g, w_in_bf, cos_t, sin_t, qg, kg, gsum, mfn, conv_w, conv_b)


def _attn_kernel(sink_ref, q_ref, kd_ref, vd_ref, kc_ref, vc_ref, o_ref, *, local, seq_len):
    tq = q_ref.shape[1]
    nblk = tq // QBLK
    gq = N_HEADS // N_KV_HEADS
    rows = gq * QBLK
    lane = lax.broadcasted_iota(jnp.int32, (QBLK, LANES), 1)
    lo_half = lane < 64
    hrow = lax.broadcasted_iota(jnp.int32, (rows, 1), 0) // QBLK
    nband = 3 * QBLK
    if local:
        qk_off = (lax.broadcasted_iota(jnp.int32, (rows, nband), 0) % QBLK
                  - lax.broadcasted_iota(jnp.int32, (rows, nband), 1))
    for blk in range(nblk):
        r0 = blk * QBLK
        qb = q_ref[0, r0:r0 + QBLK, :]
        if local:
            n = pl.program_id(1) * nblk + blk
            start = pl.multiple_of(jnp.clip((n - 1) * QBLK, 0, seq_len - nband), QBLK)
            valid = jnp.abs(qk_off + (n * QBLK - start)) <= WINDOW
        for g in range(N_KV_HEADS):
            parts = []
            for hh in range(gq):
                h = gq * g + hh
                qc = qb[:, (h // 2) * LANES:(h // 2 + 1) * LANES]
                keep = lo_half if h % 2 == 0 else jnp.logical_not(lo_half)
                parts.append(jnp.where(keep, qc, jnp.zeros_like(qc)))
            q4 = jnp.concatenate(parts, axis=0)
            sk = jnp.full((rows, 1), sink_ref[gq * g + gq - 1], F32)
            for hh in range(gq - 2, -1, -1):
                sk = jnp.where(hrow == hh, sink_ref[gq * g + hh], sk)
            gl = slice(g * LANES, (g + 1) * LANES)
            s_ctx = _dot_nt(q4, kc_ref[0, :, gl])
            m = jnp.maximum(jnp.max(s_ctx, axis=1, keepdims=True), sk)
            if local:
                s_loc = _dot_nt(q4, kd_ref[0, pl.ds(start, nband), gl])
                s_loc = jnp.where(valid, s_loc, NEG)
                m = jnp.maximum(m, jnp.max(s_loc, axis=1, keepdims=True))
            p_ctx = jnp.exp(s_ctx - m)
            den = jnp.sum(p_ctx, axis=1, keepdims=True) + jnp.exp(sk - m)
            o = _dot(p_ctx.astype(BF), vc_ref[0, :, gl])
            if local:
                p_loc = jnp.exp(s_loc - m)
                den = den + jnp.sum(p_loc, axis=1, keepdims=True)
                o = o + _dot(p_loc.astype(BF), vd_ref[0, pl.ds(start, nband), gl])
            o = o / den
            for cc in range(gq // 2):
                col = (gq // 2) * g + cc
                oa = o[(2 * cc) * QBLK:(2 * cc + 1) * QBLK]
                ob = o[(2 * cc + 1) * QBLK:(2 * cc + 2) * QBLK]
                o_ref[0, r0:r0 + QBLK, col * LANES:(col + 1) * LANES] = (
                    jnp.where(lo_half, oa, ob).astype(BF))


def _attention(sink, qkv, qkv_ctx, *, local, tq):
    b, lq, _ = qkv.shape
    c = qkv_ctx.shape[1]
    kw = 2 * KV_WIDTH
    k_blk, v_blk = ATTN_WIDTH // kw, ATTN_WIDTH // kw + 1
    kern = functools.partial(_attn_kernel, local=local, seq_len=lq)
    return pl.pallas_call(
        kern,
        grid=(b, lq // tq),
        in_specs=[
            pl.BlockSpec(memory_space=pltpu.SMEM),
            pl.BlockSpec((1, tq, ATTN_WIDTH), lambda bi, i: (bi, i, 0)),
            pl.BlockSpec((1, lq, kw), lambda bi, i: (bi, 0, k_blk)),
            pl.BlockSpec((1, lq, kw), lambda bi, i: (bi, 0, v_blk)),
            pl.BlockSpec((1, c, kw), lambda bi, i: (bi, 0, k_blk)),
            pl.BlockSpec((1, c, kw), lambda bi, i: (bi, 0, v_blk)),
        ],
        out_specs=pl.BlockSpec((1, tq, ATTN_WIDTH), lambda bi, i: (bi, i, 0)),
        out_shape=jax.ShapeDtypeStruct((b, lq, ATTN_WIDTH), BF),
        compiler_params=_cparams("parallel", "parallel"),
        name="attn_local" if local else "attn_ctx",
    )(sink, qkv, qkv, qkv, qkv_ctx, qkv_ctx)


FILTER_HALO = 128


def _hy_filter_kernel(z_ref, zn_ref, w1_ref, b1_ref, f1_ref, w2_ref, b2_ref, f2_ref, w3f_ref, w3b_ref, dl_ref,
                      flip_ref, kf_ref, kb_ref, nrm_ref):
    i = pl.program_id(0)
    tm = z_ref.shape[0]
    rows = tm + FILTER_HALO
    z = jnp.concatenate([z_ref[...], zn_ref[...]], axis=0)
    h = jnp.sin(f1_ref[...] * (_dot3(z, w1_ref[...]) + b1_ref[...]))
    h = jnp.sin(f2_ref[...] * (_dot3(h, w2_ref[...]) + b2_ref[...]))
    tcol = jnp.where(lax.broadcasted_iota(jnp.int32, (rows, LANES - FILTER_HIDDEN), 1) == 0, z[:, 0:1], 0.0)
    hid = jnp.concatenate([h, tcol], axis=1)

    def taps(hv, w3_ref):
        return _dot3(hv, w3_ref[...]) * jnp.exp(-hv[:, FILTER_HIDDEN:FILTER_HIDDEN + 1] * dl_ref[...])

    kf = taps(hid[0:tm], w3f_ref)
    flip = flip_ref[...]
    h1 = hid.astype(BF)
    r1 = hid - h1.astype(F32)
    h2 = r1.astype(BF)
    h3 = (r1 - h2.astype(F32)).astype(BF)
    hid_rev = _dot(flip, h1) + (_dot(flip, h2) + _dot(flip, h3))
    kb = taps(hid_rev, w3b_ref)
    out_row = (pl.num_programs(0) - 1 - i) * tm + lax.broadcasted_iota(jnp.int32, (tm, 1), 0)
    kb = jnp.where(out_row == 0, 0.0, kb)
    kf_ref[...] = kf
    kb_ref[...] = kb

    @pl.when(i == 0)
    def _():
        nrm_ref[...] = jnp.zeros_like(nrm_ref)

    nrm_ref[...] += jnp.sum(jnp.abs(kf), axis=0, keepdims=True) + jnp.sum(jnp.abs(kb), axis=0, keepdims=True)


def _hy_filter(zfeat, w1p, b1, f1, w2, b2, f2, w3, deltas2, *, tm):
    n = zfeat.shape[0]
    nb = n // tm
    wout = w3.shape[1] // 2
    w3p = jnp.pad(w3, ((0, LANES - w3.shape[0]), (0, 0)))
    p = np.arange(tm)[:, None]
    flip = jnp.asarray(np.arange(tm + FILTER_HALO)[None, :] == tm - p, dtype=F32).astype(BF)
    full = lambda i: (0, 0)
    halo_blk = lambda i: (jnp.minimum((i + 1) * (tm // FILTER_HALO), n // FILTER_HALO - 1), 0)
    return pl.pallas_call(
        _hy_filter_kernel,
        grid=(nb,),
        in_specs=[
            pl.BlockSpec((tm, zfeat.shape[1]), lambda i: (i, 0)),
            pl.BlockSpec((FILTER_HALO, zfeat.shape[1]), halo_blk),
            pl.BlockSpec(w1p.shape, full), pl.BlockSpec(b1.shape, full), pl.BlockSpec(f1.shape, full),
            pl.BlockSpec(w2.shape, full), pl.BlockSpec(b2.shape, full), pl.BlockSpec(f2.shape, full),
            pl.BlockSpec((LANES, wout), lambda i: (0, 0)),
            pl.BlockSpec((LANES, wout), lambda i: (0, 1)),
            pl.BlockSpec(deltas2.shape, full),
            pl.BlockSpec(flip.shape, full, pipeline_mode=pl.Buffered(1)),
        ],
        out_specs=[pl.BlockSpec((tm, wout), lambda i: (i, 0)), pl.BlockSpec((tm, wout), lambda i: (nb - 1 - i, 0)),
                   pl.BlockSpec((1, wout), full)],
        out_shape=[jax.ShapeDtypeStruct((n, wout), F32), jax.ShapeDtypeStruct((n, wout), F32),
                   jax.ShapeDtypeStruct((1, wout), F32)],
        compiler_params=_cparams("arbitrary"),
        name="hy_filter",
    )(zfeat, zfeat, w1p, b1, f1, w2, b2, f2, w3p, w3p, deltas2, flip)


def _dft_mats(k_out, r_in, period, sign, scale, n_in, real_out):
    k = np.arange(k_out)[:, None]
    r = np.arange(r_in)[None, :]
    ang = 2.0 * np.pi * ((k * r) % period) / period
    fr = np.cos(ang) * scale
    fi = sign * np.sin(ang) * scale
    if real_out:
        mats = [fr, -fi]
    else:
        mats = [np.concatenate([fr, fi], 0), np.concatenate([-fi, fr], 0)]
    return jnp.asarray(np.stack(mats[:n_in], 0), dtype=F32).astype(BF)


def _twiddle(s_n, k_n, n, sign, sbk):
    s0 = lax.broadcasted_iota(jnp.int32, (s_n // sbk, k_n, LANES), 0) * sbk
    k = lax.broadcasted_iota(jnp.int32, (s_n // sbk, k_n, LANES), 1)
    ang = (s0 * k).astype(F32) * (2.0 * math.pi / n)
    ang1 = lax.broadcasted_iota(jnp.int32, (k_n, LANES), 0).astype(F32) * (2.0 * math.pi / n)
    return jnp.cos(ang), sign * jnp.sin(ang), jnp.cos(ang1), sign * jnp.sin(ang1)


def _unpack_pair(p):
    return [pltpu.unpack_elementwise(p, index=i, packed_dtype=BF, unpacked_dtype=F32) for i in (0, 1)]


def _stage_kernel(*refs, n_in, r_in, k_mid, k_out, sbk, tw, spec, second, gate, real_out,
                  transposed_out, flat, packed_in, packed_spec, packed_out):
    it = iter(refs)
    x_refs = [next(it) for _ in range(n_in)]
    g_ref = next(it)
    g2_ref = next(it) if second else None
    tw_refs = [next(it) for _ in range(4)] if tw else None
    n_spec = 1 if packed_spec else 2
    spec_refs = [next(it) for _ in range(n_spec + 1)] if spec else None
    gate_refs = [next(it) for _ in range(5)] if gate else None
    out_refs = [next(it)] if (real_out or gate or packed_out) else [next(it), next(it)]
    if not flat:
        x_refs = [r.reshape(r_in * sbk, LANES) for r in x_refs]
        if spec:
            spec_refs = [r.reshape(k_mid * sbk, LANES) for r in spec_refs[:n_spec]] + spec_refs[n_spec:]
        if gate:
            gate_refs = [r.reshape(k_out * sbk, LANES) for r in gate_refs[:4]] + gate_refs[4:]
            out_refs = [out_refs[0].reshape(2 * k_out * sbk, LANES)]
        elif not transposed_out:
            out_refs = [r.reshape(k_out * sbk, LANES) for r in out_refs]
    if spec:
        inv = 1.0 / spec_refs[n_spec][...]
    if tw:
        tr, ti = tw_refs[0][0], tw_refs[1][0]
        wr, wi = tw_refs[2][...], tw_refs[3][...]
    def joined_dots(mats_ref, cols):
        acc = None
        for xi in range(len(cols[0])):
            wide = jnp.concatenate([c[xi].astype(BF) for c in cols], axis=1)
            d = _dot(mats_ref[xi], wide)
            acc = d if acc is None else acc + d
        return [acc[:, k * LANES:(k + 1) * LANES] for k in range(len(cols))]

    join = 1 if flat else min(STAGE_JOIN, sbk)
    results = {}
    for j in range(sbk):
        if j % join == 0:
            cols = []
            for jj in range(j, j + join):
                parts = [x_ref[...] if flat else x_ref[pl.ds(jj, r_in, stride=sbk), :] for x_ref in x_refs]
                cols.append(_unpack_pair(parts[0]) if packed_in else parts)
            accs = joined_dots(g_ref, cols)
            if not real_out:
                mids = []
                for jj, acc in zip(range(j, j + join), accs):
                    yr, yi = acc[:k_mid], acc[k_mid:]
                    if spec:
                        rows = slice(None) if flat else pl.ds(jj, k_mid, stride=sbk)
                        if packed_spec:
                            sr, si = _unpack_pair(spec_refs[0][rows, :])
                        else:
                            sr, si = spec_refs[0][rows, :], spec_refs[1][rows, :]
                        sr, si = sr * inv, si * inv
                        yr, yi = yr * sr - yi * si, yr * si + yi * sr
                    mids.append([yr, yi])
                if second:
                    mids = [[acc[:k_out], acc[k_out:]] for acc in joined_dots(g2_ref, mids)]
                accs = mids
            results = dict(zip(range(j, j + join), accs))
        if real_out:
            ys = [results[j]]
        else:
            yr, yi = results[j]
            if tw:
                yr, yi = yr * tr - yi * ti, yr * ti + yi * tr
                if j + 1 < sbk:
                    tr, ti = tr * wr - ti * wi, tr * wi + ti * wr
            ys = [yr, yi]
        if gate:
            o_ref = out_refs[0]
            for part, y in enumerate(ys):
                rows = slice(None) if flat else pl.ds(j, k_out, stride=sbk)
                val = gate_refs[part][rows, :] * (y + gate_refs[4][...] * gate_refs[2 + part][rows, :])
                if flat:
                    o_ref[part] = val.astype(o_ref.dtype)
                else:
                    o_ref[pl.ds(part * k_out * sbk + j, k_out, stride=sbk), :] = val.astype(o_ref.dtype)
            continue
        if packed_out:
            ys = [pltpu.pack_elementwise(ys, packed_dtype=BF)]
        for o_ref, y in zip(out_refs, ys):
            if flat:
                o_ref[...] = y.astype(o_ref.dtype)
            elif transposed_out:
                o_ref[0, j] = y.astype(o_ref.dtype)
            else:
                o_ref[pl.ds(j, k_out, stride=sbk), :] = y.astype(o_ref.dtype)


def _fft_stage(xs, x_sel, gmat, *, r_in, s_n, k_out, n_groups, n_cblk, transposed_out, real_out,
               out_dtype=F32, g2mat=None, tw=None, spec=None, spec_sel=None, gate=None, sbk=STAGE_ROWS,
               packed_in=False, packed_out=False, name="fft_stage"):
    n_in = len(xs)
    flat = s_n == 1
    sbk = 1 if flat else min(sbk, s_n)
    cb = LANES
    in_specs, args = [], []
    for x, sel in zip(xs, x_sel):
        if flat:
            in_specs.append(pl.BlockSpec((None, r_in, cb), lambda s, g, c, sel=sel: (sel(g, c)[0], 0, sel(g, c)[1])))
            args.append(x)
        else:
            xv = x.reshape(x.shape[0], x.shape[1] // s_n, s_n, x.shape[2])
            in_specs.append(pl.BlockSpec((1, r_in, sbk, cb),
                                         lambda s, g, c, sel=sel: (sel(g, c)[0], 0, s, sel(g, c)[1])))
            args.append(xv)
    in_specs.append(pl.BlockSpec(gmat.shape, lambda s, g, c: (0, 0, 0)))
    args.append(gmat)
    k_mid = gmat.shape[1] // (1 if real_out else 2)
    if g2mat is not None:
        in_specs.append(pl.BlockSpec(g2mat.shape, lambda s, g, c: (0, 0, 0)))
        args.append(g2mat)
    if tw is not None:
        for tarr in tw[:2]:
            in_specs.append(pl.BlockSpec((1, k_out, LANES), lambda s, g, c: (s, 0, 0)))
            args.append(tarr)
        for tarr in tw[2:]:
            in_specs.append(pl.BlockSpec((k_out, LANES), lambda s, g, c: (0, 0)))
            args.append(tarr)
    if spec is not None:
        *planes, nrm = spec
        for arr in planes:
            if flat:
                in_specs.append(pl.BlockSpec((k_mid, cb), lambda s, g, c: (0, spec_sel(g, c))))
                args.append(arr)
            else:
                in_specs.append(pl.BlockSpec((1, k_mid, sbk, cb), lambda s, g, c: (0, 0, s, spec_sel(g, c))))
                args.append(arr.reshape(1, k_mid, s_n, arr.shape[-1]))
        in_specs.append(pl.BlockSpec((1, cb), lambda s, g, c: (0, spec_sel(g, c))))
        args.append(nrm)
    if gate is not None:
        (ga, gblk), (za, zblk), bias = gate
        for arr, blk in ((ga, gblk), (za, zblk)):
            for bi in (0, 1):
                if flat:
                    in_specs.append(pl.BlockSpec((None, k_out, cb), lambda s, g, c, bi=bi, blk=blk: (bi, 0, blk + c)))
                    args.append(arr)
                else:
                    in_specs.append(pl.BlockSpec((1, k_out, sbk, cb),
                                                 lambda s, g, c, bi=bi, blk=blk: (bi, 0, s, blk + c)))
                    args.append(arr.reshape(arr.shape[0], k_out, s_n, arr.shape[-1]))
        in_specs.append(pl.BlockSpec((1, cb), lambda s, g, c: (0, c)))
        args.append(bias)
    ctot = n_cblk * cb
    if gate is not None:
        n_groups = 2
        if flat:
            oshape = (2, k_out, ctot)
            ospec = pl.BlockSpec((2, k_out, cb), lambda s, g, c: (0, 0, c))
        else:
            oshape = (2, k_out, s_n, ctot)
            ospec = pl.BlockSpec((2, k_out, sbk, cb), lambda s, g, c: (0, 0, s, c))
    elif flat:
        oshape = (n_groups, k_out, ctot)
        ospec = pl.BlockSpec((None, k_out, cb), lambda s, g, c: (g, 0, c))
    elif transposed_out:
        oshape = (n_groups, s_n, k_out, ctot)
        ospec = pl.BlockSpec((1, sbk, k_out, cb), lambda s, g, c: (g, s, 0, c))
    else:
        oshape = (n_groups, k_out, s_n, ctot)
        ospec = pl.BlockSpec((1, k_out, sbk, cb), lambda s, g, c: (g, 0, s, c))
    n_out = 1 if (real_out or gate is not None or packed_out) else 2
    if packed_out:
        out_dtype = jnp.int32
    kern = functools.partial(_stage_kernel, n_in=n_in, r_in=r_in, k_mid=k_mid, k_out=k_out, sbk=sbk,
                             tw=tw is not None, spec=spec is not None, second=g2mat is not None,
                             gate=gate is not None, real_out=real_out, transposed_out=transposed_out,
                             flat=flat, packed_in=packed_in, packed_spec=spec is not None and len(spec) == 2,
                             packed_out=packed_out)
    n_grid_groups = 1 if gate is not None else n_groups
    outs = pl.pallas_call(
        kern,
        grid=(s_n // sbk, n_grid_groups, n_cblk),
        in_specs=in_specs,
        out_specs=[ospec] * n_out,
        out_shape=[jax.ShapeDtypeStruct(oshape, out_dtype)] * n_out,
        compiler_params=_cparams("parallel", "parallel", "parallel"),
        name=name,
    )(*args)
    return [o.reshape(n_groups, -1, ctot) for o in outs]


def _split_len(n):
    if n <= 1024:
        return n, 1
    s = 128
    return n // s, s


def _fft_forward(xs, x_sel, n, n_rows, *, n_groups, n_cblk, halves=False, name="fwd"):
    n1, s = _split_len(n)
    n_in = len(xs)

    def mats(k_out, r_in, period):
        if not halves:
            return _dft_mats(k_out, r_in, period, -1.0, 1.0, n_in, False)
        g = _dft_mats(k_out, 2 * r_in, period, -1.0, 1.0, 1, False)[0]
        return jnp.stack([g[:, :r_in], g[:, r_in:]], axis=0)

    if s == 1:
        g = mats(n, n_rows, n)
        return _fft_stage(xs, x_sel, g, r_in=n_rows, s_n=1, k_out=n, n_groups=n_groups, n_cblk=n_cblk,
                          transposed_out=False, real_out=False, name=name + "_direct")
    r1 = n_rows // s
    g1 = mats(n1, r1, n1)
    tw = _twiddle(s, n1, n, -1.0, STAGE_ROWS)
    (a,) = _fft_stage(xs, x_sel, g1, r_in=r1, s_n=s, k_out=n1, n_groups=n_groups, n_cblk=n_cblk,
                      transposed_out=True, real_out=False, tw=tw, packed_out=True, name=name + "_s1")
    g2 = _dft_mats(s, s, s, -1.0, 1.0, 2, False)
    return _fft_stage([a], [lambda g, c: (g, c)], g2, r_in=s, s_n=n1, k_out=s, n_groups=n_groups,
                      n_cblk=n_cblk, transposed_out=False, real_out=False, packed_in=True, packed_out=True,
                      name=name + "_s2")


def _hyena(uc, k_halves, nrm, hy_bias):
    b, n, _ = uc.shape
    w = HYENA_WIDTH
    wblk = w // LANES
    nfft = 2 * n
    n1, s = _split_len(nfft)
    ident = lambda g, c: (g, c)
    k_spec = _fft_forward([k[None] for k in k_halves], [lambda g, c: (0, c)] * 2, nfft, n, n_groups=1,
                          n_cblk=HYENA_ORDER * wblk, halves=True, name="hy_filt_fft")
    z, zblk = uc, 2 * wblk
    for o in range(HYENA_ORDER):
        sel_r = lambda g, c, zblk=zblk: (0, zblk + c)
        sel_i = lambda g, c, zblk=zblk: (1, zblk + c)
        spec = tuple(p[0] for p in k_spec) + (nrm,)
        spec_sel = lambda g, c, o=o: o * wblk + c
        gate = ((uc, o * wblk), (z, zblk), hy_bias[o:o + 1])
        common = dict(n_groups=1, n_cblk=wblk, real_out=False)
        if s == 1:
            gf = _dft_mats(nfft, n, nfft, -1.0, 1.0, 2, False)
            gi = _dft_mats(n, nfft, nfft, 1.0, 1.0 / nfft, 2, False)
            (z,) = _fft_stage([z, z], [sel_r, sel_i], gf, r_in=n, s_n=1, k_out=n, transposed_out=False,
                              g2mat=gi, spec=spec, spec_sel=spec_sel, gate=gate, name="hy_direct", **common)
        else:
            r1 = n // s
            g1 = _dft_mats(n1, r1, n1, -1.0, 1.0, 2, False)
            (a,) = _fft_stage([z, z], [sel_r, sel_i], g1, r_in=r1, s_n=s, k_out=n1, transposed_out=True,
                              tw=_twiddle(s, n1, nfft, -1.0, STAGE_ROWS), packed_out=True, name="hy_s1",
                              **common)
            g2 = _dft_mats(s, s, s, -1.0, 1.0, 2, False)
            g3 = _dft_mats(s, s, s, 1.0, 1.0, 2, False)
            (q,) = _fft_stage([a], [ident], g2, r_in=s, s_n=n1, k_out=s, transposed_out=True,
                              g2mat=g3, tw=_twiddle(n1, s, nfft, 1.0, STAGE_ROWS), spec=spec,
                              spec_sel=spec_sel, packed_in=True, packed_out=True, name="hy_mid", **common)
            g4 = _dft_mats(n // s, n1, n1, 1.0, 1.0 / nfft, 2, False)
            (z,) = _fft_stage([q], [ident], g4, r_in=n1, s_n=s, k_out=n // s, transposed_out=False,
                              gate=gate, packed_in=True, name="hy_last", **common)
        zblk = 0
    return z.reshape(b * n, w)


def _fnet(pq):
    b, n, _ = pq.shape
    w = FNET_WIDTH
    wblk = w // LANES
    first = 3 * HYENA_WIDTH // LANES
    scale = 1.0 / math.sqrt(n * FNET_GROUP_DIM)
    sel_r = lambda g, c: (g, first + c)
    sel_i = lambda g, c: (g, first + wblk + c)
    ident = lambda g, c: (g, c)
    n1, s = _split_len(n)
    if s == 1:
        g = _dft_mats(n, n, n, -1.0, scale, 2, True)
        (y,) = _fft_stage([pq, pq], [sel_r, sel_i], g, r_in=n, s_n=1, k_out=n, n_groups=b, n_cblk=wblk,
                          transposed_out=False, real_out=True, name="fnet_direct")
        return y.reshape(b * n, w)
    g1 = _dft_mats(n1, n1, n1, -1.0, 1.0, 2, False)
    tw = _twiddle(s, n1, n, -1.0, STAGE_ROWS)
    (a,) = _fft_stage([pq, pq], [sel_r, sel_i], g1, r_in=n1, s_n=s, k_out=n1, n_groups=b, n_cblk=wblk,
                      transposed_out=True, real_out=False, tw=tw, packed_out=True, name="fnet_s1")
    g2 = _dft_mats(s, s, s, -1.0, scale, 2, True)
    (y,) = _fft_stage([a], [ident], g2, r_in=s, s_n=n1, k_out=s, n_groups=b, n_cblk=wblk,
                      transposed_out=False, real_out=True, packed_in=True, name="fnet_s2")
    return y.reshape(b * n, w)


def _merge_kernel(x_ref, gt_ref, a_ref, h_ref, f_ref, gate_ref, wa_ref, wh_ref, wf_ref, wo_ref, o_ref):
    d = D_MODEL
    m = gate_ref[:, 0:d].astype(F32) * _dot(a_ref[...], wa_ref[...])
    m = m + gate_ref[:, d:2 * d].astype(F32) * _dot(h_ref[...].astype(BF), wh_ref[...])
    m = m + gate_ref[:, 2 * d:3 * d].astype(F32) * _dot(f_ref[...].astype(BF), wf_ref[...])
    y = _dot(m.astype(BF), wo_ref[...])
    o_ref[...] = x_ref[...] + gt_ref[0] * y


def _merge(x2, gt, attn_o, hy_o, fn_o, gates, wa, wh, wf, wo, *, tm, mod_row):
    t, d = x2.shape
    tok = lambda i: (i, 0)
    full = lambda i: (0, 0)
    return pl.pallas_call(
        _merge_kernel,
        grid=(t // tm,),
        in_specs=[
            pl.BlockSpec((tm, d), tok),
            pl.BlockSpec((1, 1, d), lambda i: (mod_row(i), 0, 0)),
            pl.BlockSpec((tm, ATTN_WIDTH), tok),
            pl.BlockSpec((tm, HYENA_WIDTH), tok),
            pl.BlockSpec((tm, FNET_WIDTH), tok),
            pl.BlockSpec((tm, 3 * d), tok),
            pl.BlockSpec(wa.shape, full, pipeline_mode=pl.Buffered(1)),
            pl.BlockSpec(wh.shape, full, pipeline_mode=pl.Buffered(1)),
            pl.BlockSpec(wf.shape, full, pipeline_mode=pl.Buffered(1)),
            pl.BlockSpec(wo.shape, full, pipeline_mode=pl.Buffered(1)),
        ],
        out_specs=pl.BlockSpec((tm, d), tok),
        out_shape=jax.ShapeDtypeStruct((t, d), F32),
        compiler_params=_cparams("parallel"),
        name="merge",
    )(x2, gt, attn_o, hy_o, fn_o, gates, wa, wh, wf, wo)


def _ffn_kernel(x_ref, sh_ref, sc_ref, gt_ref, g_ref, wg_ref, wu_ref, wd_ref, o_ref, h_scr, acc_scr):
    f = pl.program_id(1)

    @pl.when(f == 0)
    def _():
        h_scr[...] = _modulated_norm(x_ref[...], g_ref[...], sc_ref[0], sh_ref[0]).astype(BF)
        acc_scr[...] = jnp.zeros_like(acc_scr)

    hb = h_scr[...]
    act = _silu(_dot(hb, wg_ref[...])) * _dot(hb, wu_ref[...])
    acc_scr[...] += _dot(act.astype(BF), wd_ref[...])

    @pl.when(f == pl.num_programs(1) - 1)
    def _():
        o_ref[...] = x_ref[...] + gt_ref[0] * acc_scr[...]


def _ffn(x2, sh, sc, gt, g, wg, wu, wd, *, tm, tf, mod_row):
    t, d = x2.shape
    ff = wg.shape[1]
    row3 = lambda i, f: (mod_row(i), 0, 0)
    wmode = dict(pipeline_mode=pl.Buffered(1)) if tf == ff else {}
    return pl.pallas_call(
        _ffn_kernel,
        grid=(t // tm, ff // tf),
        in_specs=[
            pl.BlockSpec((tm, d), lambda i, f: (i, 0)),
            pl.BlockSpec((1, 1, d), row3), pl.BlockSpec((1, 1, d), row3), pl.BlockSpec((1, 1, d), row3),
            pl.BlockSpec((1, d), lambda i, f: (0, 0)),
            pl.BlockSpec((d, tf), lambda i, f: (0, f), **wmode),
            pl.BlockSpec((d, tf), lambda i, f: (0, f), **wmode),
            pl.BlockSpec((tf, d), lambda i, f: (f, 0), **wmode),
        ],
        out_specs=pl.BlockSpec((tm, d), lambda i, f: (i, 0)),
        out_shape=jax.ShapeDtypeStruct((t, d), F32),
        scratch_shapes=[pltpu.VMEM((tm, d), BF), pltpu.VMEM((tm, d), F32)],
        compiler_params=_cparams("parallel", "arbitrary"),
        name="ffn_dense",
    )(x2, sh, sc, gt, g, wg, wu, wd)


def _top2(logits):
    lane = lax.broadcasted_iota(jnp.int32, logits.shape, 1)
    lg = jnp.where(lane < N_EXPERTS, logits, -jnp.inf)
    m1 = jnp.max(lg, axis=1, keepdims=True)
    i1 = jnp.min(jnp.where(lg == m1, lane, LANES), axis=1, keepdims=True)
    lg2 = jnp.where(lane == i1, -jnp.inf, lg)
    m2 = jnp.max(lg2, axis=1, keepdims=True)
    i2 = jnp.min(jnp.where(lg2 == m2, lane, LANES), axis=1, keepdims=True)
    e2 = jnp.exp(m2 - m1)
    w1 = 1.0 / (1.0 + e2)
    return i1, i2, w1, e2 * w1


GROUP_TILE = 256
GROUP_PAD = 32
GROUP_PIECES = GROUP_TILE // GROUP_PAD
SLOT_RADIX = 64.0


def _moe_group_kernel(x_ref, sh_ref, sc_ref, g_ref, wr_ref, tri_ref, xg_ref, ws_ref, slot_ref, cnt_ref,
                      h_scr, rows_scr, wm_scr):
    j = pl.program_id(1)
    tb = x_ref.shape[0]
    gt_rows = xg_ref.shape[1]

    @pl.when(j == 0)
    def _():
        h = _modulated_norm(x_ref[...], g_ref[...], sc_ref[0], sh_ref[0])
        h_scr[...] = h.astype(BF)
        i1, i2, w1, w2 = _top2(_dot3(h, wr_ref[...]))
        lane = lax.broadcasted_iota(jnp.int32, (tb, LANES), 1)
        oh0 = jnp.where(lane == i1, 1.0, 0.0)
        oh1 = jnp.where(lane == i2, 1.0, 0.0)
        c0 = jnp.sum(oh0, axis=0, keepdims=True)
        cnt = c0 + jnp.sum(oh1, axis=0, keepdims=True)
        tri = tri_ref[...]
        pre0 = _dot(tri, oh0.astype(BF))
        pre1 = _dot(tri, oh1.astype(BF)) + c0
        tiles = jnp.ceil(cnt * (1.0 / GROUP_PAD))
        upper = jnp.where(lax.broadcasted_iota(jnp.int32, (LANES, LANES), 0)
                          < lax.broadcasted_iota(jnp.int32, (LANES, LANES), 1), 1.0, 0.0).astype(BF)
        off = _dot(jnp.broadcast_to(tiles, (8, LANES)).astype(BF), upper)[0:1] * float(GROUP_PAD)
        slot0 = jnp.sum(oh0 * (off + pre0), axis=1, keepdims=True)
        slot1 = jnp.sum(oh1 * (off + pre1), axis=1, keepdims=True)
        slot_ref[0] = jnp.where(lane == 0, slot0, jnp.where(lane == 1, slot1, 0.0))
        cnt_ref[0] = jnp.broadcast_to(cnt, (8, LANES))
        hi0 = jnp.floor(slot0 * (1.0 / SLOT_RADIX))
        hi1 = jnp.floor(slot1 * (1.0 / SLOT_RADIX))
        digits = jnp.where(lane == 0, hi0, jnp.where(lane == 1, slot0 - SLOT_RADIX * hi0,
                           jnp.where(lane == 2, hi1, jnp.where(lane == 3, slot1 - SLOT_RADIX * hi1, 0.0))))
        sel = jnp.where(lax.broadcasted_iota(jnp.int32, (8, LANES), 0)
                        == lax.broadcasted_iota(jnp.int32, (8, LANES), 1), 1.0, 0.0).astype(BF)
        rows_scr[...] = _dot_nt(sel, digits.astype(BF))
        w1h, w1l = _split(w1)
        w1m, w1l = _split(w1 - w1h.astype(F32))
        w2h, w2l = _split(w2)
        w2m, w2l = _split(w2 - w2h.astype(F32))
        cols = [w1h, w1m, w1l, w2h, w2m, w2l]
        wm = jnp.zeros((tb, LANES), F32)
        for li, col in enumerate(cols):
            wm = jnp.where(lane == li, col.astype(F32), wm)
        wm_scr[...] = wm.astype(BF)

    rows = rows_scr[...]
    s0 = rows[0:1] * SLOT_RADIX + rows[1:2]
    s1 = rows[2:3] * SLOT_RADIX + rows[3:4]
    pos = (lax.broadcasted_iota(jnp.int32, (gt_rows, tb), 0) + j * gt_rows).astype(F32)
    g0 = jnp.where(pos == s0, 1.0, 0.0).astype(BF)
    g1 = jnp.where(pos == s1, 1.0, 0.0).astype(BF)
    xg_ref[0] = _dot(g0 + g1, h_scr[...]).astype(BF)
    lane_w = lax.broadcasted_iota(jnp.int32, (gt_rows, LANES), 1)
    wsum = (jnp.where(lane_w < 3, _dot(g0, wm_scr[...]), 0.0)
            + jnp.where((lane_w >= 3) & (lane_w < 6), _dot(g1, wm_scr[...]), 0.0))
    ws_ref[0] = jnp.broadcast_to(jnp.sum(wsum, axis=1, keepdims=True), (gt_rows, LANES))


def _moe_group(x2, sh, sc, g, wr_pad, *, tb, nt, mod_row):
    t, d = x2.shape
    nb = t // tb
    row3 = lambda b, j: (mod_row(b), 0, 0)
    tri = jnp.asarray(np.tril(np.ones((tb, tb), np.float32), -1)).astype(BF)
    return pl.pallas_call(
        _moe_group_kernel,
        grid=(nb, nt),
        in_specs=[
            pl.BlockSpec((tb, d), lambda b, j: (b, 0)),
            pl.BlockSpec((1, 1, d), row3), pl.BlockSpec((1, 1, d), row3),
            pl.BlockSpec((1, d), lambda b, j: (0, 0)),
            pl.BlockSpec((d, LANES), lambda b, j: (0, 0)),
            pl.BlockSpec((tb, tb), lambda b, j: (0, 0), pipeline_mode=pl.Buffered(1)),
        ],
        out_specs=[
            pl.BlockSpec((1, GROUP_TILE, d), lambda b, j: (b * nt + j, 0, 0)),
            pl.BlockSpec((1, GROUP_TILE, LANES), lambda b, j: (b * nt + j, 0, 0)),
            pl.BlockSpec((1, tb, LANES), lambda b, j: (b, 0, 0)),
            pl.BlockSpec((1, 8, LANES), lambda b, j: (b, 0, 0)),
        ],
        out_shape=[
            jax.ShapeDtypeStruct((nb * nt, GROUP_TILE, d), BF),
            jax.ShapeDtypeStruct((nb * nt, GROUP_TILE, LANES), F32),
            jax.ShapeDtypeStruct((nb, tb, LANES), F32),
            jax.ShapeDtypeStruct((nb, 8, LANES), F32),
        ],
        scratch_shapes=[pltpu.VMEM((tb, d), BF), pltpu.VMEM((8, tb), F32), pltpu.VMEM((tb, LANES), BF)],
        compiler_params=_cparams("parallel", "arbitrary"),
        name="moe_group",
    )(x2, sh, sc, g, wr_pad, tri)


def _moe_schedule(cnt, nh):
    np_ = GROUP_PIECES
    h = (cnt + GROUP_PAD - 1) // GROUP_PAD
    nb, ne = h.shape
    tot = h.sum(0)
    pairs = (tot + np_ - 1) // np_
    cum_p = jnp.cumsum(pairs)
    start_p = cum_p - pairs
    n_used = cum_p[-1]
    n_steps = (nb * nh + ne * (np_ - 1)) // np_
    q = jnp.minimum(jnp.arange(n_steps, dtype=jnp.int32), n_used - 1)
    e = jnp.sum(q[:, None] >= cum_p[None, :], axis=1).astype(jnp.int32)
    r = q - start_p[e]
    cum_b = jnp.cumsum(h, axis=0)
    first = jnp.cumsum(h, axis=1) - h

    idx = jnp.minimum(np_ * r[:, None] + jnp.arange(np_, dtype=jnp.int32)[None, :], tot[e][:, None] - 1)
    blk = jnp.sum(idx[:, :, None] >= cum_b.T[e][:, None, :], axis=2).astype(jnp.int32)
    in_blk = (first - (cum_b - h))[blk, e[:, None]]
    pieces = (blk * nh + in_blk + idx).T.reshape(-1).astype(jnp.int32)

    x = jnp.arange(nh, dtype=jnp.int32)[None, :]
    ex = jnp.sum(x[:, :, None] >= jnp.cumsum(h, axis=1)[:, None, :], axis=2).astype(jnp.int32)
    exc = jnp.minimum(ex, ne - 1)
    g = jnp.take_along_axis(cum_b - h - first, exc, axis=1) + x
    loc = np_ * (start_p[exc] + g // np_) + g % np_
    loc = jnp.where(ex < ne, loc, loc[:, 0:1])
    return (pieces, e, n_used.astype(jnp.int32).reshape(1), loc.reshape(-1).astype(jnp.int32),
            ((h.sum(1) + np_ - 1) // np_).astype(jnp.int32))


def _moe_expert_kernel(pc_ref, exp_ref, nused_ref, *refs):
    np_ = GROUP_PIECES
    x_refs, w_refs = refs[:np_], refs[np_:2 * np_]
    wg_ref, wu_ref, wd_ref, y_ref = refs[2 * np_:]

    @pl.when(pl.program_id(0) < nused_ref[0])
    def _():
        x = jnp.concatenate([r[0] for r in x_refs], axis=0)
        act = _silu(_dot(x, wg_ref[0])) * _dot(x, wu_ref[0])
        y = _dot(act.astype(BF), wd_ref[0])
        w = jnp.concatenate([r[0] for r in w_refs], axis=0)
        y_ref[0] = (y * jnp.concatenate([w] * (y.shape[1] // LANES), axis=1)).astype(BF)


def _moe_experts(pieces, step_exp, n_used, xg, ws, wg, wu, wd):
    d = xg.shape[-1]
    ff = wg.shape[2]
    np_ = GROUP_PIECES
    n_steps = step_exp.shape[0]
    xh = xg.reshape(-1, GROUP_PAD, d)
    wh = ws.reshape(-1, GROUP_PAD, LANES)
    pc3 = lambda k: (lambda i, pc, se, nu: (pc[k * n_steps + i], 0, 0))
    exp3 = lambda i, pc, se, nu: (se[i], 0, 0)
    return pl.pallas_call(
        _moe_expert_kernel,
        grid_spec=pltpu.PrefetchScalarGridSpec(
            num_scalar_prefetch=3,
            grid=(n_steps,),
            in_specs=(
                [pl.BlockSpec((1, GROUP_PAD, d), pc3(k)) for k in range(np_)]
                + [pl.BlockSpec((1, GROUP_PAD, LANES), pc3(k)) for k in range(np_)]
                + [pl.BlockSpec((1, d, ff), exp3), pl.BlockSpec((1, d, ff), exp3), pl.BlockSpec((1, ff, d), exp3)]
            ),
            out_specs=pl.BlockSpec((1, GROUP_TILE, d), lambda i, pc, se, nu: (jnp.minimum(i, nu[0] - 1), 0, 0)),
        ),
        out_shape=jax.ShapeDtypeStruct((n_steps, GROUP_TILE, d), BF),
        compiler_params=_cparams("arbitrary"),
        name="moe_experts",
    )(pieces, step_exp, n_used, *([xh] * np_), *([wh] * np_), wg, wu, wd)


def _moe_combine_kernel(nt_ref, loc_ref, x_ref, gt_ref, slot_ref, *refs):
    y_refs, o_ref, acc_scr = refs[:GROUP_PIECES], refs[GROUP_PIECES], refs[GROUP_PIECES + 1]
    b = pl.program_id(0)
    j = pl.program_id(1)
    tb = x_ref.shape[0]

    @pl.when(j == 0)
    def _():
        acc_scr[...] = jnp.zeros_like(acc_scr)

    @pl.when(j < nt_ref[b])
    def _():
        sl = slot_ref[0]
        pos = (lax.broadcasted_iota(jnp.int32, (tb, GROUP_TILE), 1) + j * GROUP_TILE).astype(F32)
        p = jnp.where((pos == sl[:, 0:1]) | (pos == sl[:, 1:2]), 1.0, 0.0).astype(BF)
        acc_scr[...] += _dot(p, jnp.concatenate([r[0] for r in y_refs], axis=0))

    @pl.when(j == pl.num_programs(1) - 1)
    def _():
        o_ref[...] = x_ref[...] + gt_ref[0] * acc_scr[...]


def _moe_combine(ntiles_b, loc, x2, gt, slots, yg, *, tb, nt, mod_row):
    t, d = x2.shape
    nb = t // tb
    np_ = GROUP_PIECES
    yh = yg.reshape(-1, GROUP_PAD, d)

    def piece3(k):
        return lambda b, j, n, lc: (lc[(b * nt + jnp.minimum(j, n[b] - 1)) * np_ + k], 0, 0)

    return pl.pallas_call(
        _moe_combine_kernel,
        grid_spec=pltpu.PrefetchScalarGridSpec(
            num_scalar_prefetch=2,
            grid=(nb, nt),
            in_specs=[
                pl.BlockSpec((tb, d), lambda b, j, n, lc: (b, 0)),
                pl.BlockSpec((1, 1, d), lambda b, j, n, lc: (mod_row(b), 0, 0)),
                pl.BlockSpec((1, tb, LANES), lambda b, j, n, lc: (b, 0, 0)),
            ] + [pl.BlockSpec((1, GROUP_PAD, d), piece3(k)) for k in range(np_)],
            out_specs=pl.BlockSpec((tb, d), lambda b, j, n, lc: (b, 0)),
            scratch_shapes=[pltpu.VMEM((tb, d), F32)],
        ),
        out_shape=jax.ShapeDtypeStruct((t, d), F32),
        compiler_params=_cparams("parallel", "arbitrary"),
        name="moe_combine",
    )(ntiles_b, loc, x2, gt, slots, *([yh] * np_))


def _moe(x2, sh, sc, gt, g, wr_pad, wg, wu, wd, *, tb, mod_row):
    nt = -(-(2 * tb + N_EXPERTS * (GROUP_PAD - 1)) // GROUP_TILE)
    xg, ws, slots, cnt = _moe_group(x2, sh, sc, g, wr_pad, tb=tb, nt=nt, mod_row=mod_row)
    counts = cnt[:, 0, :N_EXPERTS].astype(jnp.int32)
    pieces, step_exp, n_used, loc, ntiles_b = _moe_schedule(counts, nt * GROUP_PIECES)
    yg = _moe_experts(pieces, step_exp, n_used, xg, ws, wg, wu, wd)
    return _moe_combine(ntiles_b, loc, x2, gt, slots, yg, tb=tb, nt=nt, mod_row=mod_row)


def _rope_tables(seq_len):
    pos = np.arange(seq_len)
    prow = (pos // GRID_W).astype(np.float32)
    pcol = (pos % GRID_W).astype(np.float32)
    n_freq = HEAD_DIM // 4
    inv = (np.float32(ROPE_THETA) ** (-np.arange(n_freq, dtype=np.float32) / n_freq)).astype(np.float32)
    ar = (prow[:, None] * inv[None, :]).astype(np.float64)
    ac = (pcol[:, None] * inv[None, :]).astype(np.float64)
    cos = np.concatenate([np.cos(ar)] * 2 + [np.cos(ac)] * 2, axis=1)
    sin = np.concatenate([-np.sin(ar), np.sin(ar), -np.sin(ac), np.sin(ac)], axis=1)
    return (jnp.asarray(np.concatenate([cos, cos], axis=1), dtype=F32),
            jnp.asarray(np.concatenate([sin, sin], axis=1), dtype=F32))


def _head_sum_matrix():
    c = np.arange(ATTN_WIDTH)
    return jnp.asarray((c[:, None] // HEAD_DIM) == (c[None, :] // HEAD_DIM), dtype=F32).astype(BF)


def _fnet_channel_matrix():
    c = np.arange(FNET_WIDTH)
    same = (c[:, None] // FNET_GROUP_DIM) == (c[None, :] // FNET_GROUP_DIM)
    ang = 2.0 * np.pi * (((c[:, None] % FNET_GROUP_DIM) * (c[None, :] % FNET_GROUP_DIM)) % FNET_GROUP_DIM) / FNET_GROUP_DIM
    cb = np.where(same, np.cos(ang), 0.0)
    sb = np.where(same, np.sin(ang), 0.0)
    return jnp.asarray(np.concatenate([cb, -sb], axis=1), dtype=F32).astype(BF)


def _filter_features(n):
    t = np.linspace(0.0, 1.0, n)[:, None]
    w = 2.0 * np.pi * np.arange(n)[:, None] / n
    fb = np.linspace(1e-4, FILTER_BANDS - 1, FILTER_BANDS)
    z = np.concatenate([t, np.cos(fb * w), -np.sin(fb * w)], axis=-1)
    return jnp.asarray(np.pad(z, ((0, 0), (0, 64 - z.shape[1]))), dtype=F32)


def _decay_rates():
    d = jnp.abs(jnp.linspace(math.log(DECAY_TARGET) / SLOW_DECAY_PCT, math.log(DECAY_TARGET) / FAST_DECAY_PCT,
                             HYENA_WIDTH, dtype=F32))
    return jnp.concatenate([d] * HYENA_ORDER)[None, :]


def kernel(x, c, ctx, c_ctx, w_ada, b_ada, norm1_g, norm2_g, w_in, q_norm_g, k_norm_g, attn_sink,
           hy_conv_w, hy_conv_b, hy_filt_w1, hy_filt_b1, hy_filt_freq1, hy_filt_w2, hy_filt_b2,
           hy_filt_freq2, hy_filt_w3, hy_bias, w_proj_attn, w_proj_hyena, w_proj_fnet, w_out,
           ffn_w_gate, ffn_w_up, ffn_w_down, moe_router, moe_w_gate, moe_w_up, moe_w_down):
    b, seq, d = x.shape
    n_ctx = ctx.shape[1]
    depth = w_ada.shape[0]
    tm = 512
    tiles_per_seq = seq // tm

    cond8 = jnp.concatenate([c, c_ctx[None, :], jnp.zeros((8 - b - 1, d), F32)], axis=0)
    mods = _adaln(cond8, w_ada, b_ada)

    cos_l, sin_l = _rope_tables(seq)
    cos_c = jnp.ones((n_ctx, LANES), F32)
    sin_c = jnp.zeros((n_ctx, LANES), F32)
    gsum = _head_sum_matrix()
    mfn = _fnet_channel_matrix()
    deltas = _decay_rates()
    zfeat_l = _filter_features(seq)
    zfeat_c = _filter_features(n_ctx)

    lat_row = lambda i: i // tiles_per_seq
    ctx_row = lambda i: b
    lat_tab = lambda i: i % tiles_per_seq
    ctx_tab = lambda i: 0
    tm_c = min(tm, n_ctx)

    xs = x.reshape(b * seq, d)
    cs = ctx.reshape(b * n_ctx, d)
    for l in range(depth):
        last = l == depth - 1
        mod = lambda j: mods[l, :, j * d:(j + 1) * d].reshape(8, 1, d)
        w_in_bf = _layer_bf16(w_in, l, 256)
        qg = jnp.tile(q_norm_g[l], N_HEADS)[None, :]
        kg = jnp.tile(k_norm_g[l], N_KV_HEADS)[None, :]
        g1 = norm1_g[l][None, :]
        wa, wh, wf, wo = (w_proj_attn[l].astype(BF), w_proj_hyena[l].astype(BF),
                          w_proj_fnet[l].astype(BF), w_out[l].astype(BF))
        conv_w = hy_conv_w[l].reshape(3, -1)
        conv_b = hy_conv_b[l][None, :]
        w1p = jnp.pad(hy_filt_w1[l], ((0, 64 - hy_filt_w1.shape[1]), (0, 0)))
        filt = (w1p, hy_filt_b1[l][None, :], hy_filt_freq1[l][None, :], hy_filt_w2[l],
                hy_filt_b2[l][None, :], hy_filt_freq2[l][None, :], hy_filt_w3[l], deltas)

        qkv_c, upq_c, gates_c = _phase_a(
            cs, mod(0), mod(1), g1, w_in_bf, cos_c, sin_c, qg, kg, gsum, mfn, conv_w, conv_b,
            tm=tm_c, tiles_per_seq=n_ctx // tm_c, mod_row=ctx_row, tab_row=ctx_tab)
        qkv_c = qkv_c.reshape(b, n_ctx, -1)
        upq_c = upq_c.reshape(b, n_ctx, -1)

        qkv_l, upq_l, gates_l = _phase_a(
            xs, mod(0), mod(1), g1, w_in_bf, cos_l, sin_l, qg, kg, gsum, mfn, conv_w, conv_b,
            tm=tm, tiles_per_seq=tiles_per_seq, mod_row=lat_row, tab_row=lat_tab)
        upq_l = upq_l.reshape(b, seq, -1)
        attn_l = _attention(attn_sink[l], qkv_l.reshape(b, seq, -1), qkv_c, local=True, tq=512)
        kf_l, kb_l, nrm_l = _hy_filter(zfeat_l, *filt, tm=1024)
        hy_l = _hyena(upq_l, (kf_l, kb_l), nrm_l, hy_bias[l])
        fn_l = _fnet(upq_l)
        xs = _merge(xs, mod(2), attn_l.reshape(b * seq, -1), hy_l, fn_l, gates_l, wa, wh, wf, wo,
                    tm=tm, mod_row=lat_row)

        if not last:
            attn_c = _attention(attn_sink[l], qkv_c, qkv_c, local=False, tq=n_ctx)
            kf_c, kb_c, nrm_c = _hy_filter(zfeat_c, *filt, tm=n_ctx)
            hy_c = _hyena(upq_c, (kf_c, kb_c), nrm_c, hy_bias[l])
            fn_c = _fnet(upq_c)
            cs = _merge(cs, mod(2), attn_c.reshape(b * n_ctx, -1), hy_c, fn_c, gates_c, wa, wh, wf, wo,
                        tm=tm_c, mod_row=ctx_row)

        g2 = norm2_g[l][None, :]
        i = l // 2
        if l % 2 == 0:
            wg, wu, wd = (_layer_bf16(ffn_w_gate, i, 512), _layer_bf16(ffn_w_up, i, 512),
                          _layer_bf16(ffn_w_down, i, D_FF // 4))
            run = lambda t2, rows, tmm: _ffn(t2, mod(3), mod(4), mod(5), g2, wg, wu, wd,
                                             tm=tmm, tf=D_FF, mod_row=rows)
        else:
            wr = jnp.pad(moe_router[i], ((0, 0), (0, LANES - N_EXPERTS)))
            wg, wu, wd = (_layer_bf16(moe_w_gate, i, 512), _layer_bf16(moe_w_up, i, 512),
                          _layer_bf16(moe_w_down, i, D_FF // 4))
            run = lambda t2, rows, tmm: _moe(t2, mod(3), mod(4), mod(5), g2, wr, wg, wu, wd,
                                             tb=tmm, mod_row=rows)
        tm_ffn = 512 if l % 2 == 0 else 1024
        xs = run(xs, lambda t: t // (seq // tm_ffn), tm_ffn)
        if not last:
            cs = run(cs, ctx_row, min(tm_ffn, b * n_ctx))
    return xs.reshape(b, seq, d)
```

```python
import functools
import math

import numpy as np
import jax
import jax.numpy as jnp
from jax import lax
from jax.experimental import pallas as pl
from jax.experimental.pallas import tpu as pltpu

F32 = jnp.float32
BF = jnp.bfloat16

D_MODEL = 1024
GRID_W = 64
HEAD_DIM = 64
N_HEADS = 8
N_KV_HEADS = 2
ATTN_WIDTH = N_HEADS * HEAD_DIM
KV_WIDTH = N_KV_HEADS * HEAD_DIM
WINDOW = 128
QBLK = 128
ROPE_THETA = 10000.0
HYENA_ORDER = 2
HYENA_WIDTH = 256
FILTER_BANDS = 16
FILTER_HIDDEN = 64
DECAY_TARGET = 1e-2
FAST_DECAY_PCT = 0.3
SLOW_DECAY_PCT = 1.5
FNET_WIDTH = 256
FNET_GROUP_DIM = 64
Q_END = ATTN_WIDTH
K_END = Q_END + KV_WIDTH
V_END = K_END + KV_WIDTH
HY_END = V_END + (HYENA_ORDER + 1) * HYENA_WIDTH
FN_END = HY_END + FNET_WIDTH
IN_WIDTH = FN_END + 3 * D_MODEL
D_FF = 2816
N_EXPERTS = 8
EPS = 1e-6
LANES = 128
NEG = -1e30
STAGE_ROWS = 16
STAGE_JOIN = 2

VMEM_LIMIT = 56 * 1024 * 1024


def _cparams(*sem):
    return pltpu.CompilerParams(dimension_semantics=sem, vmem_limit_bytes=VMEM_LIMIT)


def _dot(a, b):
    return jnp.dot(a, b, preferred_element_type=F32)


def _dot_nt(a, b):
    return lax.dot_general(a, b, (((1,), (1,)), ((), ())), preferred_element_type=F32)


def _split(a):
    hi = a.astype(BF)
    lo = (a - hi.astype(F32)).astype(BF)
    return hi, lo


def _dot3(a, b):
    ah, al = _split(a)
    bh, bl = _split(b)
    return _dot(ah, bh) + (_dot(ah, bl) + _dot(al, bh))


def _dot2(a, b_bf16):
    ah, al = _split(a)
    return _dot(ah, b_bf16) + _dot(al, b_bf16)


def _sigmoid(v):
    return 0.5 * jnp.tanh(0.5 * v) + 0.5


def _silu(v):
    return v * _sigmoid(v)


def _cast_kernel(w_ref, o_ref):
    o_ref[...] = w_ref[...].astype(BF)


def _layer_bf16(w, layer, rows):
    a, b = w.shape[-2:]
    inner = int(np.prod(w.shape[1:-2], dtype=np.int64))
    w3 = w.reshape(-1, a, b)
    out = pl.pallas_call(
        _cast_kernel,
        grid=(inner, a // rows),
        in_specs=[pl.BlockSpec((1, rows, b), lambda m, i: (layer * inner + m, i, 0))],
        out_specs=pl.BlockSpec((1, rows, b), lambda m, i: (m, i, 0)),
        out_shape=jax.ShapeDtypeStruct((inner, a, b), BF),
        compiler_params=_cparams("parallel", "parallel"),
        name="cast_bf16",
    )(w3)
    return out.reshape(w.shape[1:])


def _adaln_kernel(c_ref, w_ref, b_ref, o_ref):
    o_ref[0] = _dot3(_silu(c_ref[...]), w_ref[0]) + b_ref[0]


def _adaln(cond8, w_ada, b_ada):
    depth, d, n6 = w_ada.shape
    tn = 1024
    return pl.pallas_call(
        _adaln_kernel,
        grid=(depth, n6 // tn),
        in_specs=[
            pl.BlockSpec((8, d), lambda l, j: (0, 0)),
            pl.BlockSpec((1, d, tn), lambda l, j: (l, 0, j)),
            pl.BlockSpec((1, 1, tn), lambda l, j: (l, 0, j)),
        ],
        out_specs=pl.BlockSpec((1, 8, tn), lambda l, j: (l, 0, j)),
        out_shape=jax.ShapeDtypeStruct((depth, 8, n6), F32),
        compiler_params=_cparams("parallel", "parallel"),
        name="adaln",
    )(cond8, w_ada, b_ada.reshape(depth, 1, n6))


def _modulated_norm(x, g, sc, sh):
    ms = jnp.mean(x * x, axis=-1, keepdims=True)
    h = (x * lax.rsqrt(ms + EPS)) * g
    return h * (1.0 + sc) + sh


HALO = 16


def _phase_a_kernel(x_ref, xp_ref, xn_ref, sh_ref, sc_ref, g_ref, w_ref, cos_ref, sin_ref, qg_ref, kg_ref,
                    gsum_ref, mfn_ref, cw_ref, cb_ref, qkv_ref, upq_ref, gate_ref, *, tiles_per_seq):
    tm = x_ref.shape[0]
    norm = lambda xv: _modulated_norm(xv, g_ref[...], sc_ref[0], sh_ref[0]).astype(BF)
    hb = norm(x_ref[...])
    cos = cos_ref[...]
    sin = sin_ref[...]

    def headnorm(t, gain, gs):
        ss = _dot2(t * t, gs)
        return t * lax.rsqrt(ss * (1.0 / HEAD_DIM) + EPS) * gain

    def rope(t, cosw, sinw):
        w = t.shape[1]
        nxt = pltpu.roll(t, w - 16, axis=1)
        prv = pltpu.roll(t, 16, axis=1)
        lw = lax.broadcasted_iota(jnp.int32, t.shape, 1)
        return t * cosw + jnp.where((lw % 32) < 16, nxt, prv) * sinw

    def dup_halves(t):
        lane = lax.broadcasted_iota(jnp.int32, t.shape, 1)
        sw = pltpu.roll(t, 64, axis=1)
        lo = lane < 64
        return jnp.concatenate([jnp.where(lo, t, sw), jnp.where(lo, sw, t)], axis=1)

    pr = _dot(hb, w_ref[:, 0:FN_END])
    pq = pr[:, 0:Q_END]
    qn = headnorm(pq, qg_ref[...], gsum_ref[...])
    cos4 = jnp.concatenate([cos] * 4, axis=1)
    sin4 = jnp.concatenate([sin] * 4, axis=1)
    qkv_ref[:, 0:Q_END] = (rope(qn, cos4, sin4) * (HEAD_DIM ** -0.5)).astype(BF)

    kn = headnorm(pr[:, Q_END:K_END], kg_ref[...], gsum_ref[0:KV_WIDTH, 0:KV_WIDTH])
    qkv_ref[:, Q_END:Q_END + 2 * KV_WIDTH] = dup_halves(rope(kn, cos, sin)).astype(BF)
    qkv_ref[:, Q_END + 2 * KV_WIDTH:] = dup_halves(pr[:, K_END:V_END]).astype(BF)

    w_hy = w_ref[:, V_END:HY_END]
    u = pr[:, V_END:HY_END]
    tile = pl.program_id(0) % tiles_per_seq
    u_before = jnp.where(tile == 0, 0.0, _dot(norm(xp_ref[...]), w_hy)[HALO - 1:HALO])
    u_after = jnp.where(tile == tiles_per_seq - 1, 0.0, _dot(norm(xn_ref[...]), w_hy)[0:1])
    row = lax.broadcasted_iota(jnp.int32, (tm, 1), 0)
    prv = jnp.where(row == 0, u_before, pltpu.roll(u, 1, axis=0))
    nxt = jnp.where(row == tm - 1, u_after, pltpu.roll(u, tm - 1, axis=0))
    n_hy = HY_END - V_END
    upq_ref[:, 0:n_hy] = prv * cw_ref[0:1, :] + u * cw_ref[1:2, :] + nxt * cw_ref[2:3, :] + cb_ref[...]

    upq_ref[:, n_hy:] = _dot(pr[:, HY_END:FN_END].astype(BF), mfn_ref[...])
    gate_ref[...] = _sigmoid(_dot(hb, w_ref[:, FN_END:])).astype(BF)


def _phase_a(x2, sh, sc, g, w_in_bf, cos_t, sin_t, qg, kg, gsum, mfn, conv_w, conv_b, *, tm, tiles_per_seq,
             mod_row, tab_row):
    t, d = x2.shape
    row3 = lambda i: (mod_row(i), 0, 0)
    full = lambda i: (0, 0)
    tok = lambda i: (i, 0)
    n_halo = t // HALO
    outs = [
        ((t, ATTN_WIDTH + 4 * KV_WIDTH), BF), ((t, 3 * HYENA_WIDTH + 2 * FNET_WIDTH), F32),
        ((t, 3 * D_MODEL), BF),
    ]
    kern = functools.partial(_phase_a_kernel, tiles_per_seq=tiles_per_seq)
    return pl.pallas_call(
        kern,
        grid=(t // tm,),
        in_specs=[
            pl.BlockSpec((tm, d), tok),
            pl.BlockSpec((HALO, d), lambda i: (jnp.maximum(i * (tm // HALO) - 1, 0), 0)),
            pl.BlockSpec((HALO, d), lambda i: (jnp.minimum((i + 1) * (tm // HALO), n_halo - 1), 0)),
            pl.BlockSpec((1, 1, d), row3),
            pl.BlockSpec((1, 1, d), row3),
            pl.BlockSpec((1, d), full),
            pl.BlockSpec((d, IN_WIDTH), full, pipeline_mode=pl.Buffered(1)),
            pl.BlockSpec((tm, LANES), lambda i: (tab_row(i), 0)),
            pl.BlockSpec((tm, LANES), lambda i: (tab_row(i), 0)),
            pl.BlockSpec((1, ATTN_WIDTH), full),
            pl.BlockSpec((1, KV_WIDTH), full),
            pl.BlockSpec((ATTN_WIDTH, ATTN_WIDTH), full),
            pl.BlockSpec((FNET_WIDTH, 2 * FNET_WIDTH), full),
            pl.BlockSpec(conv_w.shape, full),
            pl.BlockSpec(conv_b.shape, full),
        ],
        out_specs=[pl.BlockSpec((tm, s[1]), tok) for s, _ in outs],
        out_shape=[jax.ShapeDtypeStruct(s, dt) for s, dt in outs],
        compiler_params=_cparams("parallel"),
        name="phase_a",
    )(x2, x2, x2, sh, sc, g, w_in_bf, cos_t, sin_t, qg, kg, gsum, mfn, conv_w, conv_b)


def _attn_kernel(sink_ref, q_ref, kd_ref, vd_ref, kc_ref, vc_ref, o_ref, *, local, seq_len):
    tq = q_ref.shape[1]
    nblk = tq // QBLK
    gq = N_HEADS // N_KV_HEADS
    rows = gq * QBLK
    lane = lax.broadcasted_iota(jnp.int32, (QBLK, LANES), 1)
    lo_half = lane < 64
    hrow = lax.broadcasted_iota(jnp.int32, (rows, 1), 0) // QBLK
    nband = 3 * QBLK
    if local:
        qk_off = (lax.broadcasted_iota(jnp.int32, (rows, nband), 0) % QBLK
                  - lax.broadcasted_iota(jnp.int32, (rows, nband), 1))
    for blk in range(nblk):
        r0 = blk * QBLK
        qb = q_ref[0, r0:r0 + QBLK, :]
        if local:
            n = pl.program_id(1) * nblk + blk
            start = pl.multiple_of(jnp.clip((n - 1) * QBLK, 0, seq_len - nband), QBLK)
            valid = jnp.abs(qk_off + (n * QBLK - start)) <= WINDOW
        for g in range(N_KV_HEADS):
            parts = []
            for hh in range(gq):
                h = gq * g + hh
                qc = qb[:, (h // 2) * LANES:(h // 2 + 1) * LANES]
                keep = lo_half if h % 2 == 0 else jnp.logical_not(lo_half)
                parts.append(jnp.where(keep, qc, jnp.zeros_like(qc)))
            q4 = jnp.concatenate(parts, axis=0)
            sk = jnp.full((rows, 1), sink_ref[gq * g + gq - 1], F32)
            for hh in range(gq - 2, -1, -1):
                sk = jnp.where(hrow == hh, sink_ref[gq * g + hh], sk)
            gl = slice(g * LANES, (g + 1) * LANES)
            s_ctx = _dot_nt(q4, kc_ref[0, :, gl])
            m = jnp.maximum(jnp.max(s_ctx, axis=1, keepdims=True), sk)
            if local:
                s_loc = _dot_nt(q4, kd_ref[0, pl.ds(start, nband), gl])
                s_loc = jnp.where(valid, s_loc, NEG)
                m = jnp.maximum(m, jnp.max(s_loc, axis=1, keepdims=True))
            p_ctx = jnp.exp(s_ctx - m)
            den = jnp.sum(p_ctx, axis=1, keepdims=True) + jnp.exp(sk - m)
            o = _dot(p_ctx.astype(BF), vc_ref[0, :, gl])
            if local:
                p_loc = jnp.exp(s_loc - m)
                den = den + jnp.sum(p_loc, axis=1, keepdims=True)
                o = o + _dot(p_loc.astype(BF), vd_ref[0, pl.ds(start, nband), gl])
            o = o / den
            for cc in range(gq // 2):
                col = (gq // 2) * g + cc
                oa = o[(2 * cc) * QBLK:(2 * cc + 1) * QBLK]
                ob = o[(2 * cc + 1) * QBLK:(2 * cc + 2) * QBLK]
                o_ref[0, r0:r0 + QBLK, col * LANES:(col + 1) * LANES] = (
                    jnp.where(lo_half, oa, ob).astype(BF))


def _attention(sink, qkv, qkv_ctx, *, local, tq):
    b, lq, _ = qkv.shape
    c = qkv_ctx.shape[1]
    kw = 2 * KV_WIDTH
    k_blk, v_blk = ATTN_WIDTH // kw, ATTN_WIDTH // kw + 1
    kern = functools.partial(_attn_kernel, local=local, seq_len=lq)
    return pl.pallas_call(
        kern,
        grid=(b, lq // tq),
        in_specs=[
            pl.BlockSpec(memory_space=pltpu.SMEM),
            pl.BlockSpec((1, tq, ATTN_WIDTH), lambda bi, i: (bi, i, 0)),
            pl.BlockSpec((1, lq, kw), lambda bi, i: (bi, 0, k_blk)),
            pl.BlockSpec((1, lq, kw), lambda bi, i: (bi, 0, v_blk)),
            pl.BlockSpec((1, c, kw), lambda bi, i: (bi, 0, k_blk)),
            pl.BlockSpec((1, c, kw), lambda bi, i: (bi, 0, v_blk)),
        ],
        out_specs=pl.BlockSpec((1, tq, ATTN_WIDTH), lambda bi, i: (bi, i, 0)),
        out_shape=jax.ShapeDtypeStruct((b, lq, ATTN_WIDTH), BF),
        compiler_params=_cparams("parallel", "parallel"),
        name="attn_local" if local else "attn_ctx",
    )(sink, qkv, qkv, qkv, qkv_ctx, qkv_ctx)


FILTER_HALO = 128


def _hy_filter_kernel(z_ref, zn_ref, w1_ref, b1_ref, f1_ref, w2_ref, b2_ref, f2_ref, w3f_ref, w3b_ref, dl_ref,
                      flip_ref, kf_ref, kb_ref, nrm_ref):
    i = pl.program_id(0)
    tm = z_ref.shape[0]
    rows = tm + FILTER_HALO
    z = jnp.concatenate([z_ref[...], zn_ref[...]], axis=0)
    h = jnp.sin(f1_ref[...] * (_dot3(z, w1_ref[...]) + b1_ref[...]))
    h = jnp.sin(f2_ref[...] * (_dot3(h, w2_ref[...]) + b2_ref[...]))
    tcol = jnp.where(lax.broadcasted_iota(jnp.int32, (rows, LANES - FILTER_HIDDEN), 1) == 0, z[:, 0:1], 0.0)
    hid = jnp.concatenate([h, tcol], axis=1)

    def taps(hv, w3_ref):
        return _dot3(hv, w3_ref[...]) * jnp.exp(-hv[:, FILTER_HIDDEN:FILTER_HIDDEN + 1] * dl_ref[...])

    kf = taps(hid[0:tm], w3f_ref)
    flip = flip_ref[...]
    h1 = hid.astype(BF)
    r1 = hid - h1.astype(F32)
    h2 = r1.astype(BF)
    h3 = (r1 - h2.astype(F32)).astype(BF)
    hid_rev = _dot(flip, h1) + (_dot(flip, h2) + _dot(flip, h3))
    kb = taps(hid_rev, w3b_ref)
    out_row = (pl.num_programs(0) - 1 - i) * tm + lax.broadcasted_iota(jnp.int32, (tm, 1), 0)
    kb = jnp.where(out_row == 0, 0.0, kb)
    kf_ref[...] = kf
    kb_ref[...] = kb

    @pl.when(i == 0)
    def _():
        nrm_ref[...] = jnp.zeros_like(nrm_ref)

    nrm_ref[...] += jnp.sum(jnp.abs(kf), axis=0, keepdims=True) + jnp.sum(jnp.abs(kb), axis=0, keepdims=True)


def _hy_filter(zfeat, w1p, b1, f1, w2, b2, f2, w3, deltas2, *, tm):
    n = zfeat.shape[0]
    nb = n // tm
    wout = w3.shape[1] // 2
    w3p = jnp.pad(w3, ((0, LANES - w3.shape[0]), (0, 0)))
    p = np.arange(tm)[:, None]
    flip = jnp.asarray(np.arange(tm + FILTER_HALO)[None, :] == tm - p, dtype=F32).astype(BF)
    full = lambda i: (0, 0)
    halo_blk = lambda i: (jnp.minimum((i + 1) * (tm // FILTER_HALO), n // FILTER_HALO - 1), 0)
    return pl.pallas_call(
        _hy_filter_kernel,
        grid=(nb,),
        in_specs=[
            pl.BlockSpec((tm, zfeat.shape[1]), lambda i: (i, 0)),
            pl.BlockSpec((FILTER_HALO, zfeat.shape[1]), halo_blk),
            pl.BlockSpec(w1p.shape, full), pl.BlockSpec(b1.shape, full), pl.BlockSpec(f1.shape, full),
            pl.BlockSpec(w2.shape, full), pl.BlockSpec(b2.shape, full), pl.BlockSpec(f2.shape, full),
            pl.BlockSpec((LANES, wout), lambda i: (0, 0)),
            pl.BlockSpec((LANES, wout), lambda i: (0, 1)),
            pl.BlockSpec(deltas2.shape, full),
            pl.BlockSpec(flip.shape, full, pipeline_mode=pl.Buffered(1)),
        ],
        out_specs=[pl.BlockSpec((tm, wout), lambda i: (i, 0)), pl.BlockSpec((tm, wout), lambda i: (nb - 1 - i, 0)),
                   pl.BlockSpec((1, wout), full)],
        out_shape=[jax.ShapeDtypeStruct((n, wout), F32), jax.ShapeDtypeStruct((n, wout), F32),
                   jax.ShapeDtypeStruct((1, wout), F32)],
        compiler_params=_cparams("arbitrary"),
        name="hy_filter",
    )(zfeat, zfeat, w1p, b1, f1, w2, b2, f2, w3p, w3p, deltas2, flip)


def _dft_mats(k_out, r_in, period, sign, scale, n_in, real_out):
    k = np.arange(k_out)[:, None]
    r = np.arange(r_in)[None, :]
    ang = 2.0 * np.pi * ((k * r) % period) / period
    fr = np.cos(ang) * scale
    fi = sign * np.sin(ang) * scale
    if real_out:
        mats = [fr, -fi]
    else:
        mats = [np.concatenate([fr, fi], 0), np.concatenate([-fi, fr], 0)]
    return jnp.asarray(np.stack(mats[:n_in], 0), dtype=F32).astype(BF)


def _twiddle(s_n, k_n, n, sign, sbk):
    s0 = lax.broadcasted_iota(jnp.int32, (s_n // sbk, k_n, LANES), 0) * sbk
    k = lax.broadcasted_iota(jnp.int32, (s_n // sbk, k_n, LANES), 1)
    ang = (s0 * k).astype(F32) * (2.0 * math.pi / n)
    ang1 = lax.broadcasted_iota(jnp.int32, (k_n, LANES), 0).astype(F32) * (2.0 * math.pi / n)
    return jnp.cos(ang), sign * jnp.sin(ang), jnp.cos(ang1), sign * jnp.sin(ang1)


def _unpack_pair(p):
    return [pltpu.unpack_elementwise(p, index=i, packed_dtype=BF, unpacked_dtype=F32) for i in (0, 1)]


def _stage_kernel(*refs, n_in, r_in, k_mid, k_out, sbk, tw, spec, second, gate, real_out,
                  transposed_out, flat, packed_in, packed_spec, packed_out):
    it = iter(refs)
    x_refs = [next(it) for _ in range(n_in)]
    g_ref = next(it)
    g2_ref = next(it) if second else None
    tw_refs = [next(it) for _ in range(4)] if tw else None
    n_spec = 1 if packed_spec else 2
    spec_refs = [next(it) for _ in range(n_spec + 1)] if spec else None
    gate_refs = [next(it) for _ in range(5)] if gate else None
    out_refs = [next(it)] if (real_out or gate or packed_out) else [next(it), next(it)]
    if not flat:
        x_refs = [r.reshape(r_in * sbk, LANES) for r in x_refs]
        if spec:
            spec_refs = [r.reshape(k_mid * sbk, LANES) for r in spec_refs[:n_spec]] + spec_refs[n_spec:]
        if gate:
            gate_refs = [r.reshape(k_out * sbk, LANES) for r in gate_refs[:4]] + gate_refs[4:]
            out_refs = [out_refs[0].reshape(2 * k_out * sbk, LANES)]
        elif not transposed_out:
            out_refs = [r.reshape(k_out * sbk, LANES) for r in out_refs]
    if spec:
        inv = 1.0 / spec_refs[n_spec][...]
    if tw:
        tr, ti = tw_refs[0][0], tw_refs[1][0]
        wr, wi = tw_refs[2][...], tw_refs[3][...]
    def joined_dots(mats_ref, cols):
        acc = None
        for xi in range(len(cols[0])):
            wide = jnp.concatenate([c[xi].astype(BF) for c in cols], axis=1)
            d = _dot(mats_ref[xi], wide)
            acc = d if acc is None else acc + d
        return [acc[:, k * LANES:(k + 1) * LANES] for k in range(len(cols))]

    join = 1 if flat else min(STAGE_JOIN, sbk)
    results = {}
    for j in range(sbk):
        if j % join == 0:
            cols = []
            for jj in range(j, j + join):
                parts = [x_ref[...] if flat else x_ref[pl.ds(jj, r_in, stride=sbk), :] for x_ref in x_refs]
                cols.append(_unpack_pair(parts[0]) if packed_in else parts)
            accs = joined_dots(g_ref, cols)
            if not real_out:
                mids = []
                for jj, acc in zip(range(j, j + join), accs):
                    yr, yi = acc[:k_mid], acc[k_mid:]
                    if spec:
                        rows = slice(None) if flat else pl.ds(jj, k_mid, stride=sbk)
                        if packed_spec:
                            sr, si = _unpack_pair(spec_refs[0][rows, :])
                        else:
                            sr, si = spec_refs[0][rows, :], spec_refs[1][rows, :]
                        sr, si = sr * inv, si * inv
                        yr, yi = yr * sr - yi * si, yr * si + yi * sr
                    mids.append([yr, yi])
                if second:
                    mids = [[acc[:k_out], acc[k_out:]] for acc in joined_dots(g2_ref, mids)]
                accs = mids
            results = dict(zip(range(j, j + join), accs))
        if real_out:
            ys = [results[j]]
        else:
            yr, yi = results[j]
            if tw:
                yr, yi = yr * tr - yi * ti, yr * ti + yi * tr
                if j + 1 < sbk:
                    tr, ti = tr * wr - ti * wi, tr * wi + ti * wr
            ys = [yr, yi]
        if gate:
            o_ref = out_refs[0]
            for part, y in enumerate(ys):
                rows = slice(None) if flat else pl.ds(j, k_out, stride=sbk)
                val = gate_refs[part][rows, :] * (y + gate_refs[4][...] * gate_refs[2 + part][rows, :])
                if flat:
                    o_ref[part] = val.astype(o_ref.dtype)
                else:
                    o_ref[pl.ds(part * k_out * sbk + j, k_out, stride=sbk), :] = val.astype(o_ref.dtype)
            continue
        if packed_out:
            ys = [pltpu.pack_elementwise(ys, packed_dtype=BF)]
        for o_ref, y in zip(out_refs, ys):
            if flat:
                o_ref[...] = y.astype(o_ref.dtype)
            elif transposed_out:
                o_ref[0, j] = y.astype(o_ref.dtype)
            else:
                o_ref[pl.ds(j, k_out, stride=sbk), :] = y.astype(o_ref.dtype)


def _fft_stage(xs, x_sel, gmat, *, r_in, s_n, k_out, n_groups, n_cblk, transposed_out, real_out,
               out_dtype=F32, g2mat=None, tw=None, spec=None, spec_sel=None, gate=None, sbk=STAGE_ROWS,
               packed_in=False, packed_out=False, name="fft_stage"):
    n_in = len(xs)
    flat = s_n == 1
    sbk = 1 if flat else min(sbk, s_n)
    cb = LANES
    in_specs, args = [], []
    for x, sel in zip(xs, x_sel):
        if flat:
            in_specs.append(pl.BlockSpec((None, r_in, cb), lambda s, g, c, sel=sel: (sel(g, c)[0], 0, sel(g, c)[1])))
            args.append(x)
        else:
            xv = x.reshape(x.shape[0], x.shape[1] // s_n, s_n, x.shape[2])
            in_specs.append(pl.BlockSpec((1, r_in, sbk, cb),
                                         lambda s, g, c, sel=sel: (sel(g, c)[0], 0, s, sel(g, c)[1])))
            args.append(xv)
    in_specs.append(pl.BlockSpec(gmat.shape, lambda s, g, c: (0, 0, 0)))
    args.append(gmat)
    k_mid = gmat.shape[1] // (1 if real_out else 2)
    if g2mat is not None:
        in_specs.append(pl.BlockSpec(g2mat.shape, lambda s, g, c: (0, 0, 0)))
        args.append(g2mat)
    if tw is not None:
        for tarr in tw[:2]:
            in_specs.append(pl.BlockSpec((1, k_out, LANES), lambda s, g, c: (s, 0, 0)))
            args.append(tarr)
        for tarr in tw[2:]:
            in_specs.append(pl.BlockSpec((k_out, LANES), lambda s, g, c: (0, 0)))
            args.append(tarr)
    if spec is not None:
        *planes, nrm = spec
        for arr in planes:
            if flat:
                in_specs.append(pl.BlockSpec((k_mid, cb), lambda s, g, c: (0, spec_sel(g, c))))
                args.append(arr)
            else:
                in_specs.append(pl.BlockSpec((1, k_mid, sbk, cb), lambda s, g, c: (0, 0, s, spec_sel(g, c))))
                args.append(arr.reshape(1, k_mid, s_n, arr.shape[-1]))
        in_specs.append(pl.BlockSpec((1, cb), lambda s, g, c: (0, spec_sel(g, c))))
        args.append(nrm)
    if gate is not None:
        (ga, gblk), (za, zblk), bias = gate
        for arr, blk in ((ga, gblk), (za, zblk)):
            for bi in (0, 1):
                if flat:
                    in_specs.append(pl.BlockSpec((None, k_out, cb), lambda s, g, c, bi=bi, blk=blk: (bi, 0, blk + c)))
                    args.append(arr)
                else:
                    in_specs.append(pl.BlockSpec((1, k_out, sbk, cb),
                                                 lambda s, g, c, bi=bi, blk=blk: (bi, 0, s, blk + c)))
                    args.append(arr.reshape(arr.shape[0], k_out, s_n, arr.shape[-1]))
        in_specs.append(pl.BlockSpec((1, cb), lambda s, g, c: (0, c)))
        args.append(bias)
    ctot = n_cblk * cb
    if gate is not None:
        n_groups = 2
        if flat:
            oshape = (2, k_out, ctot)
            ospec = pl.BlockSpec((2, k_out, cb), lambda s, g, c: (0, 0, c))
        else:
            oshape = (2, k_out, s_n, ctot)
            ospec = pl.BlockSpec((2, k_out, sbk, cb), lambda s, g, c: (0, 0, s, c))
    elif flat:
        oshape = (n_groups, k_out, ctot)
        ospec = pl.BlockSpec((None, k_out, cb), lambda s, g, c: (g, 0, c))
    elif transposed_out:
        oshape = (n_groups, s_n, k_out, ctot)
        ospec = pl.BlockSpec((1, sbk, k_out, cb), lambda s, g, c: (g, s, 0, c))
    else:
        oshape = (n_groups, k_out, s_n, ctot)
        ospec = pl.BlockSpec((1, k_out, sbk, cb), lambda s, g, c: (g, 0, s, c))
    n_out = 1 if (real_out or gate is not None or packed_out) else 2
    if packed_out:
        out_dtype = jnp.int32
    kern = functools.partial(_stage_kernel, n_in=n_in, r_in=r_in, k_mid=k_mid, k_out=k_out, sbk=sbk,
                             tw=tw is not None, spec=spec is not None, second=g2mat is not None,
                             gate=gate is not None, real_out=real_out, transposed_out=transposed_out,
                             flat=flat, packed_in=packed_in, packed_spec=spec is not None and len(spec) == 2,
                             packed_out=packed_out)
    n_grid_groups = 1 if gate is not None else n_groups
    outs = pl.pallas_call(
        kern,
        grid=(s_n // sbk, n_grid_groups, n_cblk),
        in_specs=in_specs,
        out_specs=[ospec] * n_out,
        out_shape=[jax.ShapeDtypeStruct(oshape, out_dtype)] * n_out,
        compiler_params=_cparams("parallel", "parallel", "parallel"),
        name=name,
    )(*args)
    return [o.reshape(n_groups, -1, ctot) for o in outs]


def _split_len(n):
    if n <= 1024:
        return n, 1
    s = 128
    return n // s, s


def _fft_forward(xs, x_sel, n, n_rows, *, n_groups, n_cblk, halves=False, name="fwd"):
    n1, s = _split_len(n)
    n_in = len(xs)

    def mats(k_out, r_in, period):
        if not halves:
            return _dft_mats(k_out, r_in, period, -1.0, 1.0, n_in, False)
        g = _dft_mats(k_out, 2 * r_in, period, -1.0, 1.0, 1, False)[0]
        return jnp.stack([g[:, :r_in], g[:, r_in:]], axis=0)

    if s == 1:
        g = mats(n, n_rows, n)
        return _fft_stage(xs, x_sel, g, r_in=n_rows, s_n=1, k_out=n, n_groups=n_groups, n_cblk=n_cblk,
                          transposed_out=False, real_out=False, name=name + "_direct")
    r1 = n_rows // s
    g1 = mats(n1, r1, n1)
    tw = _twiddle(s, n1, n, -1.0, STAGE_ROWS)
    (a,) = _fft_stage(xs, x_sel, g1, r_in=r1, s_n=s, k_out=n1, n_groups=n_groups, n_cblk=n_cblk,
                      transposed_out=True, real_out=False, tw=tw, packed_out=True, name=name + "_s1")
    g2 = _dft_mats(s, s, s, -1.0, 1.0, 2, False)
    return _fft_stage([a], [lambda g, c: (g, c)], g2, r_in=s, s_n=n1, k_out=s, n_groups=n_groups,
                      n_cblk=n_cblk, transposed_out=False, real_out=False, packed_in=True, packed_out=True,
                      name=name + "_s2")


def _hyena(uc, k_halves, nrm, hy_bias):
    b, n, _ = uc.shape
    w = HYENA_WIDTH
    wblk = w // LANES
    nfft = 2 * n
    n1, s = _split_len(nfft)
    ident = lambda g, c: (g, c)
    k_spec = _fft_forward([k[None] for k in k_halves], [lambda g, c: (0, c)] * 2, nfft, n, n_groups=1,
                          n_cblk=HYENA_ORDER * wblk, halves=True, name="hy_filt_fft")
    z, zblk = uc, 2 * wblk
    for o in range(HYENA_ORDER):
        sel_r = lambda g, c, zblk=zblk: (0, zblk + c)
        sel_i = lambda g, c, zblk=zblk: (1, zblk + c)
        spec = tuple(p[0] for p in k_spec) + (nrm,)
        spec_sel = lambda g, c, o=o: o * wblk + c
        gate = ((uc, o * wblk), (z, zblk), hy_bias[o:o + 1])
        common = dict(n_groups=1, n_cblk=wblk, real_out=False)
        if s == 1:
            gf = _dft_mats(nfft, n, nfft, -1.0, 1.0, 2, False)
            gi = _dft_mats(n, nfft, nfft, 1.0, 1.0 / nfft, 2, False)
            (z,) = _fft_stage([z, z], [sel_r, sel_i], gf, r_in=n, s_n=1, k_out=n, transposed_out=False,
                              g2mat=gi, spec=spec, spec_sel=spec_sel, gate=gate, name="hy_direct", **common)
        else:
            r1 = n // s
            g1 = _dft_mats(n1, r1, n1, -1.0, 1.0, 2, False)
            (a,) = _fft_stage([z, z], [sel_r, sel_i], g1, r_in=r1, s_n=s, k_out=n1, transposed_out=True,
                              tw=_twiddle(s, n1, nfft, -1.0, STAGE_ROWS), packed_out=True, name="hy_s1",
                              **common)
            g2 = _dft_mats(s, s, s, -1.0, 1.0, 2, False)
            g3 = _dft_mats(s, s, s, 1.0, 1.0, 2, False)
            (q,) = _fft_stage([a], [ident], g2, r_in=s, s_n=n1, k_out=s, transposed_out=True,
                              g2mat=g3, tw=_twiddle(n1, s, nfft, 1.0, STAGE_ROWS), spec=spec,
                              spec_sel=spec_sel, packed_in=True, packed_out=True, name="hy_mid", **common)
            g4 = _dft_mats(n // s, n1, n1, 1.0, 1.0 / nfft, 2, False)
            (z,) = _fft_stage([q], [ident], g4, r_in=n1, s_n=s, k_out=n // s, transposed_out=False,
                              gate=gate, packed_in=True, name="hy_last", **common)
        zblk = 0
    return z.reshape(b * n, w)


def _fnet(pq):
    b, n, _ = pq.shape
    w = FNET_WIDTH
    wblk = w // LANES
    first = 3 * HYENA_WIDTH // LANES
    scale = 1.0 / math.sqrt(n * FNET_GROUP_DIM)
    sel_r = lambda g, c: (g, first + c)
    sel_i = lambda g, c: (g, first + wblk + c)
    ident = lambda g, c: (g, c)
    n1, s = _split_len(n)
    if s == 1:
        g = _dft_mats(n, n, n, -1.0, scale, 2, True)
        (y,) = _fft_stage([pq, pq], [sel_r, sel_i], g, r_in=n, s_n=1, k_out=n, n_groups=b, n_cblk=wblk,
                          transposed_out=False, real_out=True, name="fnet_direct")
        return y.reshape(b * n, w)
    g1 = _dft_mats(n1, n1, n1, -1.0, 1.0, 2, False)
    tw = _twiddle(s, n1, n, -1.0, STAGE_ROWS)
    (a,) = _fft_stage([pq, pq], [sel_r, sel_i], g1, r_in=n1, s_n=s, k_out=n1, n_groups=b, n_cblk=wblk,
                      transposed_out=True, real_out=False, tw=tw, packed_out=True, name="fnet_s1")
    g2 = _dft_mats(s, s, s, -1.0, scale, 2, True)
    (y,) = _fft_stage([a], [ident], g2, r_in=s, s_n=n1, k_out=s, n_groups=b, n_cblk=wblk,
                      transposed_out=False, real_out=True, packed_in=True, name="fnet_s2")
    return y.reshape(b * n, w)


def _merge_kernel(x_ref, gt_ref, a_ref, h_ref, f_ref, gate_ref, wa_ref, wh_ref, wf_ref, wo_ref, o_ref):
    d = D_MODEL
    m = gate_ref[:, 0:d].astype(F32) * _dot(a_ref[...], wa_ref[...])
    m = m + gate_ref[:, d:2 * d].astype(F32) * _dot(h_ref[...].astype(BF), wh_ref[...])
    m = m + gate_ref[:, 2 * d:3 * d].astype(F32) * _dot(f_ref[...].astype(BF), wf_ref[...])
    y = _dot(m.astype(BF), wo_ref[...])
    o_ref[...] = x_ref[...] + gt_ref[0] * y


def _merge(x2, gt, attn_o, hy_o, fn_o, gates, wa, wh, wf, wo, *, tm, mod_row):
    t, d = x2.shape
    tok = lambda i: (i, 0)
    full = lambda i: (0, 0)
    return pl.pallas_call(
        _merge_kernel,
        grid=(t // tm,),
        in_specs=[
            pl.BlockSpec((tm, d), tok),
            pl.BlockSpec((1, 1, d), lambda i: (mod_row(i), 0, 0)),
            pl.BlockSpec((tm, ATTN_WIDTH), tok),
            pl.BlockSpec((tm, HYENA_WIDTH), tok),
            pl.BlockSpec((tm, FNET_WIDTH), tok),
            pl.BlockSpec((tm, 3 * d), tok),
            pl.BlockSpec(wa.shape, full, pipeline_mode=pl.Buffered(1)),
            pl.BlockSpec(wh.shape, full, pipeline_mode=pl.Buffered(1)),
            pl.BlockSpec(wf.shape, full, pipeline_mode=pl.Buffered(1)),
            pl.BlockSpec(wo.shape, full, pipeline_mode=pl.Buffered(1)),
        ],
        out_specs=pl.BlockSpec((tm, d), tok),
        out_shape=jax.ShapeDtypeStruct((t, d), F32),
        compiler_params=_cparams("parallel"),
        name="merge",
    )(x2, gt, attn_o, hy_o, fn_o, gates, wa, wh, wf, wo)


def _ffn_kernel(x_ref, sh_ref, sc_ref, gt_ref, g_ref, wg_ref, wu_ref, wd_ref, o_ref, h_scr, acc_scr):
    f = pl.program_id(1)

    @pl.when(f == 0)
    def _():
        h_scr[...] = _modulated_norm(x_ref[...], g_ref[...], sc_ref[0], sh_ref[0]).astype(BF)
        acc_scr[...] = jnp.zeros_like(acc_scr)

    hb = h_scr[...]
    act = _silu(_dot(hb, wg_ref[...])) * _dot(hb, wu_ref[...])
    acc_scr[...] += _dot(act.astype(BF), wd_ref[...])

    @pl.when(f == pl.num_programs(1) - 1)
    def _():
        o_ref[...] = x_ref[...] + gt_ref[0] * acc_scr[...]


def _ffn(x2, sh, sc, gt, g, wg, wu, wd, *, tm, tf, mod_row):
    t, d = x2.shape
    ff = wg.shape[1]
    row3 = lambda i, f: (mod_row(i), 0, 0)
    wmode = dict(pipeline_mode=pl.Buffered(1)) if tf == ff else {}
    return pl.pallas_call(
        _ffn_kernel,
        grid=(t // tm, ff // tf),
        in_specs=[
            pl.BlockSpec((tm, d), lambda i, f: (i, 0)),
            pl.BlockSpec((1, 1, d), row3), pl.BlockSpec((1, 1, d), row3), pl.BlockSpec((1, 1, d), row3),
            pl.BlockSpec((1, d), lambda i, f: (0, 0)),
            pl.BlockSpec((d, tf), lambda i, f: (0, f), **wmode),
            pl.BlockSpec((d, tf), lambda i, f: (0, f), **wmode),
            pl.BlockSpec((tf, d), lambda i, f: (f, 0), **wmode),
        ],
        out_specs=pl.BlockSpec((tm, d), lambda i, f: (i, 0)),
        out_shape=jax.ShapeDtypeStruct((t, d), F32),
        scratch_shapes=[pltpu.VMEM((tm, d), BF), pltpu.VMEM((tm, d), F32)],
        compiler_params=_cparams("parallel", "arbitrary"),
        name="ffn_dense",
    )(x2, sh, sc, gt, g, wg, wu, wd)


def _top2(logits):
    lane = lax.broadcasted_iota(jnp.int32, logits.shape, 1)
    lg = jnp.where(lane < N_EXPERTS, logits, -jnp.inf)
    m1 = jnp.max(lg, axis=1, keepdims=True)
    i1 = jnp.min(jnp.where(lg == m1, lane, LANES), axis=1, keepdims=True)
    lg2 = jnp.where(lane == i1, -jnp.inf, lg)
    m2 = jnp.max(lg2, axis=1, keepdims=True)
    i2 = jnp.min(jnp.where(lg2 == m2, lane, LANES), axis=1, keepdims=True)
    e2 = jnp.exp(m2 - m1)
    w1 = 1.0 / (1.0 + e2)
    return i1, i2, w1, e2 * w1


GROUP_TILE = 256
GROUP_PAD = 32
GROUP_PIECES = GROUP_TILE // GROUP_PAD
SLOT_RADIX = 64.0


def _moe_group_kernel(x_ref, sh_ref, sc_ref, g_ref, wr_ref, tri_ref, xg_ref, ws_ref, slot_ref, cnt_ref,
                      h_scr, rows_scr, wm_scr):
    j = pl.program_id(1)
    tb = x_ref.shape[0]
    gt_rows = xg_ref.shape[1]

    @pl.when(j == 0)
    def _():
        h = _modulated_norm(x_ref[...], g_ref[...], sc_ref[0], sh_ref[0])
        h_scr[...] = h.astype(BF)
        i1, i2, w1, w2 = _top2(_dot3(h, wr_ref[...]))
        lane = lax.broadcasted_iota(jnp.int32, (tb, LANES), 1)
        oh0 = jnp.where(lane == i1, 1.0, 0.0)
        oh1 = jnp.where(lane == i2, 1.0, 0.0)
        c0 = jnp.sum(oh0, axis=0, keepdims=True)
        cnt = c0 + jnp.sum(oh1, axis=0, keepdims=True)
        tri = tri_ref[...]
        pre0 = _dot(tri, oh0.astype(BF))
        pre1 = _dot(tri, oh1.astype(BF)) + c0
        tiles = jnp.ceil(cnt * (1.0 / GROUP_PAD))
        upper = jnp.where(lax.broadcasted_iota(jnp.int32, (LANES, LANES), 0)
                          < lax.broadcasted_iota(jnp.int32, (LANES, LANES), 1), 1.0, 0.0).astype(BF)
        off = _dot(jnp.broadcast_to(tiles, (8, LANES)).astype(BF), upper)[0:1] * float(GROUP_PAD)
        slot0 = jnp.sum(oh0 * (off + pre0), axis=1, keepdims=True)
        slot1 = jnp.sum(oh1 * (off + pre1), axis=1, keepdims=True)
        slot_ref[0] = jnp.where(lane == 0, slot0, jnp.where(lane == 1, slot1, 0.0))
        cnt_ref[0] = jnp.broadcast_to(cnt, (8, LANES))
        hi0 = jnp.floor(slot0 * (1.0 / SLOT_RADIX))
        hi1 = jnp.floor(slot1 * (1.0 / SLOT_RADIX))
        digits = jnp.where(lane == 0, hi0, jnp.where(lane == 1, slot0 - SLOT_RADIX * hi0,
                           jnp.where(lane == 2, hi1, jnp.where(lane == 3, slot1 - SLOT_RADIX * hi1, 0.0))))
        sel = jnp.where(lax.broadcasted_iota(jnp.int32, (8, LANES), 0)
                        == lax.broadcasted_iota(jnp.int32, (8, LANES), 1), 1.0, 0.0).astype(BF)
        rows_scr[...] = _dot_nt(sel, digits.astype(BF))
        w1h, w1l = _split(w1)
        w1m, w1l = _split(w1 - w1h.astype(F32))
        w2h, w2l = _split(w2)
        w2m, w2l = _split(w2 - w2h.astype(F32))
        cols = [w1h, w1m, w1l, w2h, w2m, w2l]
        wm = jnp.zeros((tb, LANES), F32)
        for li, col in enumerate(cols):
            wm = jnp.where(lane == li, col.astype(F32), wm)
        wm_scr[...] = wm.astype(BF)

    rows = rows_scr[...]
    s0 = rows[0:1] * SLOT_RADIX + rows[1:2]
    s1 = rows[2:3] * SLOT_RADIX + rows[3:4]
    pos = (lax.broadcasted_iota(jnp.int32, (gt_rows, tb), 0) + j * gt_rows).astype(F32)
    g0 = jnp.where(pos == s0, 1.0, 0.0).astype(BF)
    g1 = jnp.where(pos == s1, 1.0, 0.0).astype(BF)
    xg_ref[0] = _dot(g0 + g1, h_scr[...]).astype(BF)
    lane_w = lax.broadcasted_iota(jnp.int32, (gt_rows, LANES), 1)
    wsum = (jnp.where(lane_w < 3, _dot(g0, wm_scr[...]), 0.0)
            + jnp.where((lane_w >= 3) & (lane_w < 6), _dot(g1, wm_scr[...]), 0.0))
    ws_ref[0] = jnp.broadcast_to(jnp.sum(wsum, axis=1, keepdims=True), (gt_rows, LANES))


def _moe_group(x2, sh, sc, g, wr_pad, *, tb, nt, mod_row):
    t, d = x2.shape
    nb = t // tb
    row3 = lambda b, j: (mod_row(b), 0, 0)
    tri = jnp.asarray(np.tril(np.ones((tb, tb), np.float32), -1)).astype(BF)
    return pl.pallas_call(
        _moe_group_kernel,
        grid=(nb, nt),
        in_specs=[
            pl.BlockSpec((tb, d), lambda b, j: (b, 0)),
            pl.BlockSpec((1, 1, d), row3), pl.BlockSpec((1, 1, d), row3),
            pl.BlockSpec((1, d), lambda b, j: (0, 0)),
            pl.BlockSpec((d, LANES), lambda b, j: (0, 0)),
            pl.BlockSpec((tb, tb), lambda b, j: (0, 0), pipeline_mode=pl.Buffered(1)),
        ],
        out_specs=[
            pl.BlockSpec((1, GROUP_TILE, d), lambda b, j: (b * nt + j, 0, 0)),
            pl.BlockSpec((1, GROUP_TILE, LANES), lambda b, j: (b * nt + j, 0, 0)),
            pl.BlockSpec((1, tb, LANES), lambda b, j: (b, 0, 0)),
            pl.BlockSpec((1, 8, LANES), lambda b, j: (b, 0, 0)),
        ],
        out_shape=[
            jax.ShapeDtypeStruct((nb * nt, GROUP_TILE, d), BF),
            jax.ShapeDtypeStruct((nb * nt, GROUP_TILE, LANES), F32),
            jax.ShapeDtypeStruct((nb, tb, LANES), F32),
            jax.ShapeDtypeStruct((nb, 8, LANES), F32),
        ],
        scratch_shapes=[pltpu.VMEM((tb, d), BF), pltpu.VMEM((8, tb), F32), pltpu.VMEM((tb, LANES), BF)],
        compiler_params=_cparams("parallel", "arbitrary"),
        name="moe_group",
    )(x2, sh, sc, g, wr_pad, tri)


def _moe_schedule(cnt, nh):
    np_ = GROUP_PIECES
    h = (cnt + GROUP_PAD - 1) // GROUP_PAD
    nb, ne = h.shape
    tot = h.sum(0)
    pairs = (tot + np_ - 1) // np_
    cum_p = jnp.cumsum(pairs)
    start_p = cum_p - pairs
    n_used = cum_p[-1]
    n_steps = (nb * nh + ne * (np_ - 1)) // np_
    q = jnp.minimum(jnp.arange(n_steps, dtype=jnp.int32), n_used - 1)
    e = jnp.sum(q[:, None] >= cum_p[None, :], axis=1).astype(jnp.int32)
    r = q - start_p[e]
    cum_b = jnp.cumsum(h, axis=0)
    first = jnp.cumsum(h, axis=1) - h

    idx = jnp.minimum(np_ * r[:, None] + jnp.arange(np_, dtype=jnp.int32)[None, :], tot[e][:, None] - 1)
    blk = jnp.sum(idx[:, :, None] >= cum_b.T[e][:, None, :], axis=2).astype(jnp.int32)
    in_blk = (first - (cum_b - h))[blk, e[:, None]]
    pieces = (blk * nh + in_blk + idx).T.reshape(-1).astype(jnp.int32)

    x = jnp.arange(nh, dtype=jnp.int32)[None, :]
    ex = jnp.sum(x[:, :, None] >= jnp.cumsum(h, axis=1)[:, None, :], axis=2).astype(jnp.int32)
    exc = jnp.minimum(ex, ne - 1)
    g = jnp.take_along_axis(cum_b - h - first, exc, axis=1) + x
    loc = np_ * (start_p[exc] + g // np_) + g % np_
    loc = jnp.where(ex < ne, loc, loc[:, 0:1])
    return (pieces, e, n_used.astype(jnp.int32).reshape(1), loc.reshape(-1).astype(jnp.int32),
            ((h.sum(1) + np_ - 1) // np_).astype(jnp.int32))


def _moe_expert_kernel(pc_ref, exp_ref, nused_ref, *refs):
    np_ = GROUP_PIECES
    x_refs, w_refs = refs[:np_], refs[np_:2 * np_]
    wg_ref, wu_ref, wd_ref, y_ref = refs[2 * np_:]

    @pl.when(pl.program_id(0) < nused_ref[0])
    def _():
        x = jnp.concatenate([r[0] for r in x_refs], axis=0)
        act = _silu(_dot(x, wg_ref[0])) * _dot(x, wu_ref[0])
        y = _dot(act.astype(BF), wd_ref[0])
        w = jnp.concatenate([r[0] for r in w_refs], axis=0)
        y_ref[0] = (y * jnp.concatenate([w] * (y.shape[1] // LANES), axis=1)).astype(BF)


def _moe_experts(pieces, step_exp, n_used, xg, ws, wg, wu, wd):
    d = xg.shape[-1]
    ff = wg.shape[2]
    np_ = GROUP_PIECES
    n_steps = step_exp.shape[0]
    xh = xg.reshape(-1, GROUP_PAD, d)
    wh = ws.reshape(-1, GROUP_PAD, LANES)
    pc3 = lambda k: (lambda i, pc, se, nu: (pc[k * n_steps + i], 0, 0))
    exp3 = lambda i, pc, se, nu: (se[i], 0, 0)
    return pl.pallas_call(
        _moe_expert_kernel,
        grid_spec=pltpu.PrefetchScalarGridSpec(
            num_scalar_prefetch=3,
            grid=(n_steps,),
            in_specs=(
                [pl.BlockSpec((1, GROUP_PAD, d), pc3(k)) for k in range(np_)]
                + [pl.BlockSpec((1, GROUP_PAD, LANES), pc3(k)) for k in range(np_)]
                + [pl.BlockSpec((1, d, ff), exp3), pl.BlockSpec((1, d, ff), exp3), pl.BlockSpec((1, ff, d), exp3)]
            ),
            out_specs=pl.BlockSpec((1, GROUP_TILE, d), lambda i, pc, se, nu: (jnp.minimum(i, nu[0] - 1), 0, 0)),
        ),
        out_shape=jax.ShapeDtypeStruct((n_steps, GROUP_TILE, d), BF),
        compiler_params=_cparams("arbitrary"),
        name="moe_experts",
    )(pieces, step_exp, n_used, *([xh] * np_), *([wh] * np_), wg, wu, wd)


def _moe_combine_kernel(nt_ref, loc_ref, x_ref, gt_ref, slot_ref, *refs):
    y_refs, o_ref, acc_scr = refs[:GROUP_PIECES], refs[GROUP_PIECES], refs[GROUP_PIECES + 1]
    b = pl.program_id(0)
    j = pl.program_id(1)
    tb = x_ref.shape[0]

    @pl.when(j == 0)
    def _():
        acc_scr[...] = jnp.zeros_like(acc_scr)

    @pl.when(j < nt_ref[b])
    def _():
        sl = slot_ref[0]
        pos = (lax.broadcasted_iota(jnp.int32, (tb, GROUP_TILE), 1) + j * GROUP_TILE).astype(F32)
        p = jnp.where((pos == sl[:, 0:1]) | (pos == sl[:, 1:2]), 1.0, 0.0).astype(BF)
        acc_scr[...] += _dot(p, jnp.concatenate([r[0] for r in y_refs], axis=0))

    @pl.when(j == pl.num_programs(1) - 1)
    def _():
        o_ref[...] = x_ref[...] + gt_ref[0] * acc_scr[...]


def _moe_combine(ntiles_b, loc, x2, gt, slots, yg, *, tb, nt, mod_row):
    t, d = x2.shape
    nb = t // tb
    np_ = GROUP_PIECES
    yh = yg.reshape(-1, GROUP_PAD, d)

    def piece3(k):
        return lambda b, j, n, lc: (lc[(b * nt + jnp.minimum(j, n[b] - 1)) * np_ + k], 0, 0)

    return pl.pallas_call(
        _moe_combine_kernel,
        grid_spec=pltpu.PrefetchScalarGridSpec(
            num_scalar_prefetch=2,
            grid=(nb, nt),
            in_specs=[
                pl.BlockSpec((tb, d), lambda b, j, n, lc: (b, 0)),
                pl.BlockSpec((1, 1, d), lambda b, j, n, lc: (mod_row(b), 0, 0)),
                pl.BlockSpec((1, tb, LANES), lambda b, j, n, lc: (b, 0, 0)),
            ] + [pl.BlockSpec((1, GROUP_PAD, d), piece3(k)) for k in range(np_)],
            out_specs=pl.BlockSpec((tb, d), lambda b, j, n, lc: (b, 0)),
            scratch_shapes=[pltpu.VMEM((tb, d), F32)],
        ),
        out_shape=jax.ShapeDtypeStruct((t, d), F32),
        compiler_params=_cparams("parallel", "arbitrary"),
        name="moe_combine",
    )(ntiles_b, loc, x2, gt, slots, *([yh] * np_))


def _moe(x2, sh, sc, gt, g, wr_pad, wg, wu, wd, *, tb, mod_row):
    nt = -(-(2 * tb + N_EXPERTS * (GROUP_PAD - 1)) // GROUP_TILE)
    xg, ws, slots, cnt = _moe_group(x2, sh, sc, g, wr_pad, tb=tb, nt=nt, mod_row=mod_row)
    counts = cnt[:, 0, :N_EXPERTS].astype(jnp.int32)
    pieces, step_exp, n_used, loc, ntiles_b = _moe_schedule(counts, nt * GROUP_PIECES)
    yg = _moe_experts(pieces, step_exp, n_used, xg, ws, wg, wu, wd)
    return _moe_combine(ntiles_b, loc, x2, gt, slots, yg, tb=tb, nt=nt, mod_row=mod_row)


def _rope_tables(seq_len):
    pos = np.arange(seq_len)
    prow = (pos // GRID_W).astype(np.float32)
    pcol = (pos % GRID_W).astype(np.float32)
    n_freq = HEAD_DIM // 4
    inv = (np.float32(ROPE_THETA) ** (-np.arange(n_freq, dtype=np.float32) / n_freq)).astype(np.float32)
    ar = (prow[:, None] * inv[None, :]).astype(np.float64)
    ac = (pcol[:, None] * inv[None, :]).astype(np.float64)
    cos = np.concatenate([np.cos(ar)] * 2 + [np.cos(ac)] * 2, axis=1)
    sin = np.concatenate([-np.sin(ar), np.sin(ar), -np.sin(ac), np.sin(ac)], axis=1)
    return (jnp.asarray(np.concatenate([cos, cos], axis=1), dtype=F32),
            jnp.asarray(np.concatenate([sin, sin], axis=1), dtype=F32))


def _head_sum_matrix():
    c = np.arange(ATTN_WIDTH)
    return jnp.asarray((c[:, None] // HEAD_DIM) == (c[None, :] // HEAD_DIM), dtype=F32).astype(BF)


def _fnet_channel_matrix():
    c = np.arange(FNET_WIDTH)
    same = (c[:, None] // FNET_GROUP_DIM) == (c[None, :] // FNET_GROUP_DIM)
    ang = 2.0 * np.pi * (((c[:, None] % FNET_GROUP_DIM) * (c[None, :] % FNET_GROUP_DIM)) % FNET_GROUP_DIM) / FNET_GROUP_DIM
    cb = np.where(same, np.cos(ang), 0.0)
    sb = np.where(same, np.sin(ang), 0.0)
    return jnp.asarray(np.concatenate([cb, -sb], axis=1), dtype=F32).astype(BF)


def _filter_features(n):
    t = np.linspace(0.0, 1.0, n)[:, None]
    w = 2.0 * np.pi * np.arange(n)[:, None] / n
    fb = np.linspace(1e-4, FILTER_BANDS - 1, FILTER_BANDS)
    z = np.concatenate([t, np.cos(fb * w), -np.sin(fb * w)], axis=-1)
    return jnp.asarray(np.pad(z, ((0, 0), (0, 64 - z.shape[1]))), dtype=F32)


def _decay_rates():
    d = jnp.abs(jnp.linspace(math.log(DECAY_TARGET) / SLOW_DECAY_PCT, math.log(DECAY_TARGET) / FAST_DECAY_PCT,
                             HYENA_WIDTH, dtype=F32))
    return jnp.concatenate([d] * HYENA_ORDER)[None, :]


def kernel(x, c, ctx, c_ctx, w_ada, b_ada, norm1_g, norm2_g, w_in, q_norm_g, k_norm_g, attn_sink,
           hy_conv_w, hy_conv_b, hy_filt_w1, hy_filt_b1, hy_filt_freq1, hy_filt_w2, hy_filt_b2,
           hy_filt_freq2, hy_filt_w3, hy_bias, w_proj_attn, w_proj_hyena, w_proj_fnet, w_out,
           ffn_w_gate, ffn_w_up, ffn_w_down, moe_router, moe_w_gate, moe_w_up, moe_w_down):
    b, seq, d = x.shape
    n_ctx = ctx.shape[1]
    depth = w_ada.shape[0]
    tm = 512
    tiles_per_seq = seq // tm

    cond8 = jnp.concatenate([c, c_ctx[None, :], jnp.zeros((8 - b - 1, d), F32)], axis=0)
    mods = _adaln(cond8, w_ada, b_ada)

    cos_l, sin_l = _rope_tables(seq)
    cos_c = jnp.ones((n_ctx, LANES), F32)
    sin_c = jnp.zeros((n_ctx, LANES), F32)
    gsum = _head_sum_matrix()
    mfn = _fnet_channel_matrix()
    deltas = _decay_rates()
    zfeat_l = _filter_features(seq)
    zfeat_c = _filter_features(n_ctx)

    lat_row = lambda i: i // tiles_per_seq
    ctx_row = lambda i: b
    lat_tab = lambda i: i % tiles_per_seq
    ctx_tab = lambda i: 0
    tm_c = min(tm, n_ctx)

    xs = x.reshape(b * seq, d)
    cs = ctx.reshape(b * n_ctx, d)
    for l in range(depth):
        last = l == depth - 1
        mod = lambda j: mods[l, :, j * d:(j + 1) * d].reshape(8, 1, d)
        w_in_bf = _layer_bf16(w_in, l, 256)
        qg = jnp.tile(q_norm_g[l], N_HEADS)[None, :]
        kg = jnp.tile(k_norm_g[l], N_KV_HEADS)[None, :]
        g1 = norm1_g[l][None, :]
        wa, wh, wf, wo = (w_proj_attn[l].astype(BF), w_proj_hyena[l].astype(BF),
                          w_proj_fnet[l].astype(BF), w_out[l].astype(BF))
        conv_w = hy_conv_w[l].reshape(3, -1)
        conv_b = hy_conv_b[l][None, :]
        w1p = jnp.pad(hy_filt_w1[l], ((0, 64 - hy_filt_w1.shape[1]), (0, 0)))
        filt = (w1p, hy_filt_b1[l][None, :], hy_filt_freq1[l][None, :], hy_filt_w2[l],
                hy_filt_b2[l][None, :], hy_filt_freq2[l][None, :], hy_filt_w3[l], deltas)

        qkv_c, upq_c, gates_c = _phase_a(
            cs, mod(0), mod(1), g1, w_in_bf, cos_c, sin_c, qg, kg, gsum, mfn, conv_w, conv_b,
            tm=tm_c, tiles_per_seq=n_ctx // tm_c, mod_row=ctx_row, tab_row=ctx_tab)
        qkv_c = qkv_c.reshape(b, n_ctx, -1)
        upq_c = upq_c.reshape(b, n_ctx, -1)

        qkv_l, upq_l, gates_l = _phase_a(
            xs, mod(0), mod(1), g1, w_in_bf, cos_l, sin_l, qg, kg, gsum, mfn, conv_w, conv_b,
            tm=tm, tiles_per_seq=tiles_per_seq, mod_row=lat_row, tab_row=lat_tab)
        upq_l = upq_l.reshape(b, seq, -1)
        attn_l = _attention(attn_sink[l], qkv_l.reshape(b, seq, -1), qkv_c, local=True, tq=512)
        kf_l, kb_l, nrm_l = _hy_filter(zfeat_l, *filt, tm=1024)
        hy_l = _hyena(upq_l, (kf_l, kb_l), nrm_l, hy_bias[l])
        fn_l = _fnet(upq_l)
        tm_merge = 2 * tm
        xs = _merge(xs, mod(2), attn_l.reshape(b * seq, -1), hy_l, fn_l, gates_l, wa, wh, wf, wo,
                    tm=tm_merge, mod_row=lambda t: t // (seq // tm_merge))

        if not last:
            attn_c = _attention(attn_sink[l], qkv_c, qkv_c, local=False, tq=n_ctx)
            kf_c, kb_c, nrm_c = _hy_filter(zfeat_c, *filt, tm=n_ctx)
            hy_c = _hyena(upq_c, (kf_c, kb_c), nrm_c, hy_bias[l])
            fn_c = _fnet(upq_c)
            cs = _merge(cs, mod(2), attn_c.reshape(b * n_ctx, -1), hy_c, fn_c, gates_c, wa, wh, wf, wo,
                        tm=tm_c, mod_row=ctx_row)

        g2 = norm2_g[l][None, :]
        i = l // 2
        if l % 2 == 0:
            wg, wu, wd = (_layer_bf16(ffn_w_gate, i, 512), _layer_bf16(ffn_w_up, i, 512),
                          _layer_bf16(ffn_w_down, i, D_FF // 4))
            run = lambda t2, rows, tmm: _ffn(t2, mod(3), mod(4), mod(5), g2, wg, wu, wd,
                                             tm=tmm, tf=D_FF, mod_row=rows)
        else:
            wr = jnp.pad(moe_router[i], ((0, 0), (0, LANES - N_EXPERTS)))
            wg, wu, wd = (_layer_bf16(moe_w_gate, i, 512), _layer_bf16(moe_w_up, i, 512),
                          _layer_bf16(moe_w_down, i, D_FF // 4))
            run = lambda t2, rows, tmm: _moe(t2, mod(3), mod(4), mod(5), g2, wr, wg, wu, wd,
                                             tb=tmm, mod_row=rows)
        tm_ffn = 512 if l % 2 == 0 else 1024
        xs = run(xs, lambda t: t // (seq // tm_ffn), tm_ffn)
        if not last:
            cs = run(cs, ctx_row, min(tm_ffn, b * n_ctx))
    return xs.reshape(b, seq, d)
```

```python
import functools
import math

import numpy as np
import jax
import jax.numpy as jnp
from jax import lax
from jax.experimental import pallas as pl
from jax.experimental.pallas import tpu as pltpu

F32 = jnp.float32
BF = jnp.bfloat16

D_MODEL = 1024
GRID_W = 64
HEAD_DIM = 64
N_HEADS = 8
N_KV_HEADS = 2
ATTN_WIDTH = N_HEADS * HEAD_DIM
KV_WIDTH = N_KV_HEADS * HEAD_DIM
WINDOW = 128
QBLK = 128
ROPE_THETA = 10000.0
HYENA_ORDER = 2
HYENA_WIDTH = 256
FILTER_BANDS = 16
FILTER_HIDDEN = 64
DECAY_TARGET = 1e-2
FAST_DECAY_PCT = 0.3
SLOW_DECAY_PCT = 1.5
FNET_WIDTH = 256
FNET_GROUP_DIM = 64
Q_END = ATTN_WIDTH
K_END = Q_END + KV_WIDTH
V_END = K_END + KV_WIDTH
HY_END = V_END + (HYENA_ORDER + 1) * HYENA_WIDTH
FN_END = HY_END + FNET_WIDTH
IN_WIDTH = FN_END + 3 * D_MODEL
D_FF = 2816
N_EXPERTS = 8
EPS = 1e-6
LANES = 128
NEG = -1e30
STAGE_ROWS = 16
STAGE_JOIN = 2

VMEM_LIMIT = 56 * 1024 * 1024


def _cparams(*sem):
    return pltpu.CompilerParams(dimension_semantics=sem, vmem_limit_bytes=VMEM_LIMIT)


def _dot(a, b):
    return jnp.dot(a, b, preferred_element_type=F32)


def _dot_nt(a, b):
    return lax.dot_general(a, b, (((1,), (1,)), ((), ())), preferred_element_type=F32)


def _split(a):
    hi = a.astype(BF)
    lo = (a - hi.astype(F32)).astype(BF)
    return hi, lo


def _dot3(a, b):
    ah, al = _split(a)
    bh, bl = _split(b)
    return _dot(ah, bh) + (_dot(ah, bl) + _dot(al, bh))


def _dot2(a, b_bf16):
    ah, al = _split(a)
    return _dot(ah, b_bf16) + _dot(al, b_bf16)


def _sigmoid(v):
    return 0.5 * jnp.tanh(0.5 * v) + 0.5


def _silu(v):
    return v * _sigmoid(v)


def _cast_kernel(w_ref, o_ref):
    o_ref[...] = w_ref[...].astype(BF)


def _layer_bf16(w, layer, rows):
    a, b = w.shape[-2:]
    inner = int(np.prod(w.shape[1:-2], dtype=np.int64))
    w3 = w.reshape(-1, a, b)
    out = pl.pallas_call(
        _cast_kernel,
        grid=(inner, a // rows),
        in_specs=[pl.BlockSpec((1, rows, b), lambda m, i: (layer * inner + m, i, 0))],
        out_specs=pl.BlockSpec((1, rows, b), lambda m, i: (m, i, 0)),
        out_shape=jax.ShapeDtypeStruct((inner, a, b), BF),
        compiler_params=_cparams("parallel", "parallel"),
        name="cast_bf16",
    )(w3)
    return out.reshape(w.shape[1:])


def _adaln_kernel(c_ref, w_ref, b_ref, o_ref):
    o_ref[0] = _dot3(_silu(c_ref[...]), w_ref[0]) + b_ref[0]


def _adaln(cond8, w_ada, b_ada):
    depth, d, n6 = w_ada.shape
    tn = 1024
    return pl.pallas_call(
        _adaln_kernel,
        grid=(depth, n6 // tn),
        in_specs=[
            pl.BlockSpec((8, d), lambda l, j: (0, 0)),
            pl.BlockSpec((1, d, tn), lambda l, j: (l, 0, j)),
            pl.BlockSpec((1, 1, tn), lambda l, j: (l, 0, j)),
        ],
        out_specs=pl.BlockSpec((1, 8, tn), lambda l, j: (l, 0, j)),
        out_shape=jax.ShapeDtypeStruct((depth, 8, n6), F32),
        compiler_params=_cparams("parallel", "parallel"),
        name="adaln",
    )(cond8, w_ada, b_ada.reshape(depth, 1, n6))


def _modulated_norm(x, g, sc, sh):
    ms = jnp.mean(x * x, axis=-1, keepdims=True)
    h = (x * lax.rsqrt(ms + EPS)) * g
    return h * (1.0 + sc) + sh


HALO = 16


def _phase_a_kernel(x_ref, xp_ref, xn_ref, sh_ref, sc_ref, g_ref, w_ref, cos_ref, sin_ref, qg_ref, kg_ref,
                    gsum_ref, mfn_ref, cw_ref, cb_ref, qkv_ref, upq_ref, gate_ref, *, tiles_per_seq):
    tm = x_ref.shape[0]
    norm = lambda xv: _modulated_norm(xv, g_ref[...], sc_ref[0], sh_ref[0]).astype(BF)
    hb = norm(x_ref[...])
    cos = cos_ref[...]
    sin = sin_ref[...]

    def headnorm(t, gain, gs):
        ss = _dot2(t * t, gs)
        return t * lax.rsqrt(ss * (1.0 / HEAD_DIM) + EPS) * gain

    def rope(t, cosw, sinw):
        w = t.shape[1]
        nxt = pltpu.roll(t, w - 16, axis=1)
        prv = pltpu.roll(t, 16, axis=1)
        lw = lax.broadcasted_iota(jnp.int32, t.shape, 1)
        return t * cosw + jnp.where((lw % 32) < 16, nxt, prv) * sinw

    def dup_halves(t):
        lane = lax.broadcasted_iota(jnp.int32, t.shape, 1)
        sw = pltpu.roll(t, 64, axis=1)
        lo = lane < 64
        return jnp.concatenate([jnp.where(lo, t, sw), jnp.where(lo, sw, t)], axis=1)

    pr = _dot(hb, w_ref[:, 0:FN_END])
    pq = pr[:, 0:Q_END]
    qn = headnorm(pq, qg_ref[...], gsum_ref[...])
    cos4 = jnp.concatenate([cos] * 4, axis=1)
    sin4 = jnp.concatenate([sin] * 4, axis=1)
    qkv_ref[:, 0:Q_END] = (rope(qn, cos4, sin4) * (HEAD_DIM ** -0.5)).astype(BF)

    kn = headnorm(pr[:, Q_END:K_END], kg_ref[...], gsum_ref[0:KV_WIDTH, 0:KV_WIDTH])
    qkv_ref[:, Q_END:Q_END + 2 * KV_WIDTH] = dup_halves(rope(kn, cos, sin)).astype(BF)
    qkv_ref[:, Q_END + 2 * KV_WIDTH:] = dup_halves(pr[:, K_END:V_END]).astype(BF)

    w_hy = w_ref[:, V_END:HY_END]
    u = pr[:, V_END:HY_END]
    tile = pl.program_id(0) % tiles_per_seq
    u_before = jnp.where(tile == 0, 0.0, _dot(norm(xp_ref[...]), w_hy)[HALO - 1:HALO])
    u_after = jnp.where(tile == tiles_per_seq - 1, 0.0, _dot(norm(xn_ref[...]), w_hy)[0:1])
    row = lax.broadcasted_iota(jnp.int32, (tm, 1), 0)
    prv = jnp.where(row == 0, u_before, pltpu.roll(u, 1, axis=0))
    nxt = jnp.where(row == tm - 1, u_after, pltpu.roll(u, tm - 1, axis=0))
    n_hy = HY_END - V_END
    upq_ref[:, 0:n_hy] = prv * cw_ref[0:1, :] + u * cw_ref[1:2, :] + nxt * cw_ref[2:3, :] + cb_ref[...]

    upq_ref[:, n_hy:] = _dot(pr[:, HY_END:FN_END].astype(BF), mfn_ref[...])
    gate_ref[...] = _sigmoid(_dot(hb, w_ref[:, FN_END:])).astype(BF)


def _phase_a(x2, sh, sc, g, w_in_bf, cos_t, sin_t, qg, kg, gsum, mfn, conv_w, conv_b, *, tm, tiles_per_seq,
             mod_row, tab_row):
    t, d = x2.shape
    row3 = lambda i: (mod_row(i), 0, 0)
    full = lambda i: (0, 0)
    tok = lambda i: (i, 0)
    n_halo = t // HALO
    outs = [
        ((t, ATTN_WIDTH + 4 * KV_WIDTH), BF), ((t, 3 * HYENA_WIDTH + 2 * FNET_WIDTH), F32),
        ((t, 3 * D_MODEL), BF),
    ]
    kern = functools.partial(_phase_a_kernel, tiles_per_seq=tiles_per_seq)
    return pl.pallas_call(
        kern,
        grid=(t // tm,),
        in_specs=[
            pl.BlockSpec((tm, d), tok),
            pl.BlockSpec((HALO, d), lambda i: (jnp.maximum(i * (tm // HALO) - 1, 0), 0)),
            pl.BlockSpec((HALO, d), lambda i: (jnp.minimum((i + 1) * (tm // HALO), n_halo - 1), 0)),
            pl.BlockSpec((1, 1, d), row3),
            pl.BlockSpec((1, 1, d), row3),
            pl.BlockSpec((1, d), full),
            pl.BlockSpec((d, IN_WIDTH), full, pipeline_mode=pl.Buffered(1)),
            pl.BlockSpec((tm, LANES), lambda i: (tab_row(i), 0)),
            pl.BlockSpec((tm, LANES), lambda i: (tab_row(i), 0)),
            pl.BlockSpec((1, ATTN_WIDTH), full),
            pl.BlockSpec((1, KV_WIDTH), full),
            pl.BlockSpec((ATTN_WIDTH, ATTN_WIDTH), full),
            pl.BlockSpec((FNET_WIDTH, 2 * FNET_WIDTH), full),
            pl.BlockSpec(conv_w.shape, full),
            pl.BlockSpec(conv_b.shape, full),
        ],
        out_specs=[pl.BlockSpec((tm, s[1]), tok) for s, _ in outs],
        out_shape=[jax.ShapeDtypeStruct(s, dt) for s, dt in outs],
        compiler_params=_cparams("parallel"),
        name="phase_a",
    )(x2, x2, x2, sh, sc, g, w_in_bf, cos_t, sin_t, qg, kg, gsum, mfn, conv_w, conv_b)


def _attn_kernel(sink_ref, q_ref, kd_ref, vd_ref, kc_ref, vc_ref, o_ref, *, local, seq_len):
    tq = q_ref.shape[1]
    nblk = tq // QBLK
    gq = N_HEADS // N_KV_HEADS
    rows = gq * QBLK
    lane = lax.broadcasted_iota(jnp.int32, (QBLK, LANES), 1)
    lo_half = lane < 64
    hrow = lax.broadcasted_iota(jnp.int32, (rows, 1), 0) // QBLK
    nband = 3 * QBLK
    if local:
        qk_off = (lax.broadcasted_iota(jnp.int32, (rows, nband), 0) % QBLK
                  - lax.broadcasted_iota(jnp.int32, (rows, nband), 1))
    for blk in range(nblk):
        r0 = blk * QBLK
        qb = q_ref[0, r0:r0 + QBLK, :]
        if local:
            n = pl.program_id(1) * nblk + blk
            start = pl.multiple_of(jnp.clip((n - 1) * QBLK, 0, seq_len - nband), QBLK)
            valid = jnp.abs(qk_off + (n * QBLK - start)) <= WINDOW
        for g in range(N_KV_HEADS):
            parts = []
            for hh in range(gq):
                h = gq * g + hh
                qc = qb[:, (h // 2) * LANES:(h // 2 + 1) * LANES]
                keep = lo_half if h % 2 == 0 else jnp.logical_not(lo_half)
                parts.append(jnp.where(keep, qc, jnp.zeros_like(qc)))
            q4 = jnp.concatenate(parts, axis=0)
            sk = jnp.full((rows, 1), sink_ref[gq * g + gq - 1], F32)
            for hh in range(gq - 2, -1, -1):
                sk = jnp.where(hrow == hh, sink_ref[gq * g + hh], sk)
            gl = slice(g * LANES, (g + 1) * LANES)
            s_ctx = _dot_nt(q4, kc_ref[0, :, gl])
            m = jnp.maximum(jnp.max(s_ctx, axis=1, keepdims=True), sk)
            if local:
                s_loc = _dot_nt(q4, kd_ref[0, pl.ds(start, nband), gl])
                s_loc = jnp.where(valid, s_loc, NEG)
                m = jnp.maximum(m, jnp.max(s_loc, axis=1, keepdims=True))
            p_ctx = jnp.exp(s_ctx - m)
            den = jnp.sum(p_ctx, axis=1, keepdims=True) + jnp.exp(sk - m)
            o = _dot(p_ctx.astype(BF), vc_ref[0, :, gl])
            if local:
                p_loc = jnp.exp(s_loc - m)
                den = den + jnp.sum(p_loc, axis=1, keepdims=True)
                o = o + _dot(p_loc.astype(BF), vd_ref[0, pl.ds(start, nband), gl])
            o = o / den
            for cc in range(gq // 2):
                col = (gq // 2) * g + cc
                oa = o[(2 * cc) * QBLK:(2 * cc + 1) * QBLK]
                ob = o[(2 * cc + 1) * QBLK:(2 * cc + 2) * QBLK]
                o_ref[0, r0:r0 + QBLK, col * LANES:(col + 1) * LANES] = (
                    jnp.where(lo_half, oa, ob).astype(BF))


def _attention(sink, qkv, qkv_ctx, *, local, tq):
    b, lq, _ = qkv.shape
    c = qkv_ctx.shape[1]
    kw = 2 * KV_WIDTH
    k_blk, v_blk = ATTN_WIDTH // kw, ATTN_WIDTH // kw + 1
    kern = functools.partial(_attn_kernel, local=local, seq_len=lq)
    return pl.pallas_call(
        kern,
        grid=(b, lq // tq),
        in_specs=[
            pl.BlockSpec(memory_space=pltpu.SMEM),
            pl.BlockSpec((1, tq, ATTN_WIDTH), lambda bi, i: (bi, i, 0)),
            pl.BlockSpec((1, lq, kw), lambda bi, i: (bi, 0, k_blk)),
            pl.BlockSpec((1, lq, kw), lambda bi, i: (bi, 0, v_blk)),
            pl.BlockSpec((1, c, kw), lambda bi, i: (bi, 0, k_blk)),
            pl.BlockSpec((1, c, kw), lambda bi, i: (bi, 0, v_blk)),
        ],
        out_specs=pl.BlockSpec((1, tq, ATTN_WIDTH), lambda bi, i: (bi, i, 0)),
        out_shape=jax.ShapeDtypeStruct((b, lq, ATTN_WIDTH), BF),
        compiler_params=_cparams("parallel", "parallel"),
        name="attn_local" if local else "attn_ctx",
    )(sink, qkv, qkv, qkv, qkv_ctx, qkv_ctx)


FILTER_HALO = 128


def _hy_filter_kernel(z_ref, zn_ref, w1_ref, b1_ref, f1_ref, w2_ref, b2_ref, f2_ref, w3f_ref, w3b_ref, dl_ref,
                      flip_ref, kf_ref, kb_ref, nrm_ref):
    i = pl.program_id(0)
    tm = z_ref.shape[0]
    rows = tm + FILTER_HALO
    z = jnp.concatenate([z_ref[...], zn_ref[...]], axis=0)
    h = jnp.sin(f1_ref[...] * (_dot3(z, w1_ref[...]) + b1_ref[...]))
    h = jnp.sin(f2_ref[...] * (_dot3(h, w2_ref[...]) + b2_ref[...]))
    tcol = jnp.where(lax.broadcasted_iota(jnp.int32, (rows, LANES - FILTER_HIDDEN), 1) == 0, z[:, 0:1], 0.0)
    hid = jnp.concatenate([h, tcol], axis=1)

    def taps(hv, w3_ref):
        return _dot3(hv, w3_ref[...]) * jnp.exp(-hv[:, FILTER_HIDDEN:FILTER_HIDDEN + 1] * dl_ref[...])

    kf = taps(hid[0:tm], w3f_ref)
    flip = flip_ref[...]
    h1 = hid.astype(BF)
    r1 = hid - h1.astype(F32)
    h2 = r1.astype(BF)
    h3 = (r1 - h2.astype(F32)).astype(BF)
    hid_rev = _dot(flip, h1) + (_dot(flip, h2) + _dot(flip, h3))
    kb = taps(hid_rev, w3b_ref)
    out_row = (pl.num_programs(0) - 1 - i) * tm + lax.broadcasted_iota(jnp.int32, (tm, 1), 0)
    kb = jnp.where(out_row == 0, 0.0, kb)
    kf_ref[...] = kf
    kb_ref[...] = kb

    @pl.when(i == 0)
    def _():
        nrm_ref[...] = jnp.zeros_like(nrm_ref)

    nrm_ref[...] += jnp.sum(jnp.abs(kf), axis=0, keepdims=True) + jnp.sum(jnp.abs(kb), axis=0, keepdims=True)


def _hy_filter(zfeat, w1p, b1, f1, w2, b2, f2, w3, deltas2, *, tm):
    n = zfeat.shape[0]
    nb = n // tm
    wout = w3.shape[1] // 2
    w3p = jnp.pad(w3, ((0, LANES - w3.shape[0]), (0, 0)))
    p = np.arange(tm)[:, None]
    flip = jnp.asarray(np.arange(tm + FILTER_HALO)[None, :] == tm - p, dtype=F32).astype(BF)
    full = lambda i: (0, 0)
    halo_blk = lambda i: (jnp.minimum((i + 1) * (tm // FILTER_HALO), n // FILTER_HALO - 1), 0)
    return pl.pallas_call(
        _hy_filter_kernel,
        grid=(nb,),
        in_specs=[
            pl.BlockSpec((tm, zfeat.shape[1]), lambda i: (i, 0)),
            pl.BlockSpec((FILTER_HALO, zfeat.shape[1]), halo_blk),
            pl.BlockSpec(w1p.shape, full), pl.BlockSpec(b1.shape, full), pl.BlockSpec(f1.shape, full),
            pl.BlockSpec(w2.shape, full), pl.BlockSpec(b2.shape, full), pl.BlockSpec(f2.shape, full),
            pl.BlockSpec((LANES, wout), lambda i: (0, 0)),
            pl.BlockSpec((LANES, wout), lambda i: (0, 1)),
            pl.BlockSpec(deltas2.shape, full),
            pl.BlockSpec(flip.shape, full, pipeline_mode=pl.Buffered(1)),
        ],
        out_specs=[pl.BlockSpec((tm, wout), lambda i: (i, 0)), pl.BlockSpec((tm, wout), lambda i: (nb - 1 - i, 0)),
                   pl.BlockSpec((1, wout), full)],
        out_shape=[jax.ShapeDtypeStruct((n, wout), F32), jax.ShapeDtypeStruct((n, wout), F32),
                   jax.ShapeDtypeStruct((1, wout), F32)],
        compiler_params=_cparams("arbitrary"),
        name="hy_filter",
    )(zfeat, zfeat, w1p, b1, f1, w2, b2, f2, w3p, w3p, deltas2, flip)


def _dft_mats(k_out, r_in, period, sign, scale, n_in, real_out):
    k = np.arange(k_out)[:, None]
    r = np.arange(r_in)[None, :]
    ang = 2.0 * np.pi * ((k * r) % period) / period
    fr = np.cos(ang) * scale
    fi = sign * np.sin(ang) * scale
    if real_out:
        mats = [fr, -fi]
    else:
        mats = [np.concatenate([fr, fi], 0), np.concatenate([-fi, fr], 0)]
    return jnp.asarray(np.stack(mats[:n_in], 0), dtype=F32).astype(BF)


def _twiddle(s_n, k_n, n, sign, sbk):
    s0 = lax.broadcasted_iota(jnp.int32, (s_n // sbk, k_n, LANES), 0) * sbk
    k = lax.broadcasted_iota(jnp.int32, (s_n // sbk, k_n, LANES), 1)
    ang = (s0 * k).astype(F32) * (2.0 * math.pi / n)
    ang1 = lax.broadcasted_iota(jnp.int32, (k_n, LANES), 0).astype(F32) * (2.0 * math.pi / n)
    return jnp.cos(ang), sign * jnp.sin(ang), jnp.cos(ang1), sign * jnp.sin(ang1)


def _unpack_pair(p):
    return [pltpu.unpack_elementwise(p, index=i, packed_dtype=BF, unpacked_dtype=F32) for i in (0, 1)]


def _stage_kernel(*refs, n_in, r_in, k_mid, k_out, sbk, tw, spec, second, gate, real_out,
                  transposed_out, flat, packed_in, packed_spec, packed_out):
    it = iter(refs)
    x_refs = [next(it) for _ in range(n_in)]
    g_ref = next(it)
    g2_ref = next(it) if second else None
    tw_refs = [next(it) for _ in range(4)] if tw else None
    n_spec = 1 if packed_spec else 2
    spec_refs = [next(it) for _ in range(n_spec + 1)] if spec else None
    gate_refs = [next(it) for _ in range(5)] if gate else None
    out_refs = [next(it)] if (real_out or gate or packed_out) else [next(it), next(it)]
    if not flat:
        x_refs = [r.reshape(r_in * sbk, LANES) for r in x_refs]
        if spec:
            spec_refs = [r.reshape(k_mid * sbk, LANES) for r in spec_refs[:n_spec]] + spec_refs[n_spec:]
        if gate:
            gate_refs = [r.reshape(k_out * sbk, LANES) for r in gate_refs[:4]] + gate_refs[4:]
            out_refs = [out_refs[0].reshape(2 * k_out * sbk, LANES)]
        elif not transposed_out:
            out_refs = [r.reshape(k_out * sbk, LANES) for r in out_refs]
    if spec:
        inv = 1.0 / spec_refs[n_spec][...]
    if tw:
        tr, ti = tw_refs[0][0], tw_refs[1][0]
        wr, wi = tw_refs[2][...], tw_refs[3][...]
    def joined_dots(mats_ref, cols):
        acc = None
        for xi in range(len(cols[0])):
            wide = jnp.concatenate([c[xi].astype(BF) for c in cols], axis=1)
            d = _dot(mats_ref[xi], wide)
            acc = d if acc is None else acc + d
        return [acc[:, k * LANES:(k + 1) * LANES] for k in range(len(cols))]

    join = 1 if flat else min(STAGE_JOIN, sbk)
    results = {}
    for j in range(sbk):
        if j % join == 0:
            cols = []
            for jj in range(j, j + join):
                parts = [x_ref[...] if flat else x_ref[pl.ds(jj, r_in, stride=sbk), :] for x_ref in x_refs]
                cols.append(_unpack_pair(parts[0]) if packed_in else parts)
            accs = joined_dots(g_ref, cols)
            if not real_out:
                mids = []
                for jj, acc in zip(range(j, j + join), accs):
                    yr, yi = acc[:k_mid], acc[k_mid:]
                    if spec:
                        rows = slice(None) if flat else pl.ds(jj, k_mid, stride=sbk)
                        if packed_spec:
                            sr, si = _unpack_pair(spec_refs[0][rows, :])
                        else:
                            sr, si = spec_refs[0][rows, :], spec_refs[1][rows, :]
                        sr, si = sr * inv, si * inv
                        yr, yi = yr * sr - yi * si, yr * si + yi * sr
                    mids.append([yr, yi])
                if second:
                    mids = [[acc[:k_out], acc[k_out:]] for acc in joined_dots(g2_ref, mids)]
                accs = mids
            results = dict(zip(range(j, j + join), accs))
        if real_out:
            ys = [results[j]]
        else:
            yr, yi = results[j]
            if tw:
                yr, yi = yr * tr - yi * ti, yr * ti + yi * tr
                if j + 1 < sbk:
                    tr, ti = tr * wr - ti * wi, tr * wi + ti * wr
            ys = [yr, yi]
        if gate:
            o_ref = out_refs[0]
            for part, y in enumerate(ys):
                rows = slice(None) if flat else pl.ds(j, k_out, stride=sbk)
                val = gate_refs[part][rows, :] * (y + gate_refs[4][...] * gate_refs[2 + part][rows, :])
                if flat:
                    o_ref[part] = val.astype(o_ref.dtype)
                else:
                    o_ref[pl.ds(part * k_out * sbk + j, k_out, stride=sbk), :] = val.astype(o_ref.dtype)
            continue
        if packed_out:
            ys = [pltpu.pack_elementwise(ys, packed_dtype=BF)]
        for o_ref, y in zip(out_refs, ys):
            if flat:
                o_ref[...] = y.astype(o_ref.dtype)
            elif transposed_out:
                o_ref[0, j] = y.astype(o_ref.dtype)
            else:
                o_ref[pl.ds(j, k_out, stride=sbk), :] = y.astype(o_ref.dtype)


def _fft_stage(xs, x_sel, gmat, *, r_in, s_n, k_out, n_groups, n_cblk, transposed_out, real_out,
               out_dtype=F32, g2mat=None, tw=None, spec=None, spec_sel=None, gate=None, sbk=STAGE_ROWS,
               packed_in=False, packed_out=False, name="fft_stage"):
    n_in = len(xs)
    flat = s_n == 1
    sbk = 1 if flat else min(sbk, s_n)
    cb = LANES
    in_specs, args = [], []
    for x, sel in zip(xs, x_sel):
        if flat:
            in_specs.append(pl.BlockSpec((None, r_in, cb), lambda s, g, c, sel=sel: (sel(g, c)[0], 0, sel(g, c)[1])))
            args.append(x)
        else:
            xv = x.reshape(x.shape[0], x.shape[1] // s_n, s_n, x.shape[2])
            in_specs.append(pl.BlockSpec((1, r_in, sbk, cb),
                                         lambda s, g, c, sel=sel: (sel(g, c)[0], 0, s, sel(g, c)[1])))
            args.append(xv)
    in_specs.append(pl.BlockSpec(gmat.shape, lambda s, g, c: (0, 0, 0)))
    args.append(gmat)
    k_mid = gmat.shape[1] // (1 if real_out else 2)
    if g2mat is not None:
        in_specs.append(pl.BlockSpec(g2mat.shape, lambda s, g, c: (0, 0, 0)))
        args.append(g2mat)
    if tw is not None:
        for tarr in tw[:2]:
            in_specs.append(pl.BlockSpec((1, k_out, LANES), lambda s, g, c: (s, 0, 0)))
            args.append(tarr)
        for tarr in tw[2:]:
            in_specs.append(pl.BlockSpec((k_out, LANES), lambda s, g, c: (0, 0)))
            args.append(tarr)
    if spec is not None:
        *planes, nrm = spec
        for arr in planes:
            if flat:
                in_specs.append(pl.BlockSpec((k_mid, cb), lambda s, g, c: (0, spec_sel(g, c))))
                args.append(arr)
            else:
                in_specs.append(pl.BlockSpec((1, k_mid, sbk, cb), lambda s, g, c: (0, 0, s, spec_sel(g, c))))
                args.append(arr.reshape(1, k_mid, s_n, arr.shape[-1]))
        in_specs.append(pl.BlockSpec((1, cb), lambda s, g, c: (0, spec_sel(g, c))))
        args.append(nrm)
    if gate is not None:
        (ga, gblk), (za, zblk), bias = gate
        for arr, blk in ((ga, gblk), (za, zblk)):
            for bi in (0, 1):
                if flat:
                    in_specs.append(pl.BlockSpec((None, k_out, cb), lambda s, g, c, bi=bi, blk=blk: (bi, 0, blk + c)))
                    args.append(arr)
                else:
                    in_specs.append(pl.BlockSpec((1, k_out, sbk, cb),
                                                 lambda s, g, c, bi=bi, blk=blk: (bi, 0, s, blk + c)))
                    args.append(arr.reshape(arr.shape[0], k_out, s_n, arr.shape[-1]))
        in_specs.append(pl.BlockSpec((1, cb), lambda s, g, c: (0, c)))
        args.append(bias)
    ctot = n_cblk * cb
    if gate is not None:
        n_groups = 2
        if flat:
            oshape = (2, k_out, ctot)
            ospec = pl.BlockSpec((2, k_out, cb), lambda s, g, c: (0, 0, c))
        else:
            oshape = (2, k_out, s_n, ctot)
            ospec = pl.BlockSpec((2, k_out, sbk, cb), lambda s, g, c: (0, 0, s, c))
    elif flat:
        oshape = (n_groups, k_out, ctot)
        ospec = pl.BlockSpec((None, k_out, cb), lambda s, g, c: (g, 0, c))
    elif transposed_out:
        oshape = (n_groups, s_n, k_out, ctot)
        ospec = pl.BlockSpec((1, sbk, k_out, cb), lambda s, g, c: (g, s, 0, c))
    else:
        oshape = (n_groups, k_out, s_n, ctot)
        ospec = pl.BlockSpec((1, k_out, sbk, cb), lambda s, g, c: (g, 0, s, c))
    n_out = 1 if (real_out or gate is not None or packed_out) else 2
    if packed_out:
        out_dtype = jnp.int32
    kern = functools.partial(_stage_kernel, n_in=n_in, r_in=r_in, k_mid=k_mid, k_out=k_out, sbk=sbk,
                             tw=tw is not None, spec=spec is not None, second=g2mat is not None,
                             gate=gate is not None, real_out=real_out, transposed_out=transposed_out,
                             flat=flat, packed_in=packed_in, packed_spec=spec is not None and len(spec) == 2,
                             packed_out=packed_out)
    n_grid_groups = 1 if gate is not None else n_groups
    outs = pl.pallas_call(
        kern,
        grid=(s_n // sbk, n_grid_groups, n_cblk),
        in_specs=in_specs,
        out_specs=[ospec] * n_out,
        out_shape=[jax.ShapeDtypeStruct(oshape, out_dtype)] * n_out,
        compiler_params=_cparams("parallel", "parallel", "parallel"),
        name=name,
    )(*args)
    return [o.reshape(n_groups, -1, ctot) for o in outs]


def _split_len(n):
    if n <= 1024:
        return n, 1
    s = 128
    return n // s, s


def _fft_forward(xs, x_sel, n, n_rows, *, n_groups, n_cblk, halves=False, name="fwd"):
    n1, s = _split_len(n)
    n_in = len(xs)

    def mats(k_out, r_in, period):
        if not halves:
            return _dft_mats(k_out, r_in, period, -1.0, 1.0, n_in, False)
        g = _dft_mats(k_out, 2 * r_in, period, -1.0, 1.0, 1, False)[0]
        return jnp.stack([g[:, :r_in], g[:, r_in:]], axis=0)

    if s == 1:
        g = mats(n, n_rows, n)
        return _fft_stage(xs, x_sel, g, r_in=n_rows, s_n=1, k_out=n, n_groups=n_groups, n_cblk=n_cblk,
                          transposed_out=False, real_out=False, name=name + "_direct")
    r1 = n_rows // s
    g1 = mats(n1, r1, n1)
    tw = _twiddle(s, n1, n, -1.0, STAGE_ROWS)
    (a,) = _fft_stage(xs, x_sel, g1, r_in=r1, s_n=s, k_out=n1, n_groups=n_groups, n_cblk=n_cblk,
                      transposed_out=True, real_out=False, tw=tw, packed_out=True, name=name + "_s1")
    g2 = _dft_mats(s, s, s, -1.0, 1.0, 2, False)
    return _fft_stage([a], [lambda g, c: (g, c)], g2, r_in=s, s_n=n1, k_out=s, n_groups=n_groups,
                      n_cblk=n_cblk, transposed_out=False, real_out=False, packed_in=True, packed_out=True,
                      name=name + "_s2")


def _hyena(uc, k_halves, nrm, hy_bias):
    b, n, _ = uc.shape
    w = HYENA_WIDTH
    wblk = w // LANES
    nfft = 2 * n
    n1, s = _split_len(nfft)
    ident = lambda g, c: (g, c)
    k_spec = _fft_forward([k[None] for k in k_halves], [lambda g, c: (0, c)] * 2, nfft, n, n_groups=1,
                          n_cblk=HYENA_ORDER * wblk, halves=True, name="hy_filt_fft")
    z, zblk = uc, 2 * wblk
    for o in range(HYENA_ORDER):
        sel_r = lambda g, c, zblk=zblk: (0, zblk + c)
        sel_i = lambda g, c, zblk=zblk: (1, zblk + c)
        spec = tuple(p[0] for p in k_spec) + (nrm,)
        spec_sel = lambda g, c, o=o: o * wblk + c
        gate = ((uc, o * wblk), (z, zblk), hy_bias[o:o + 1])
        common = dict(n_groups=1, n_cblk=wblk, real_out=False)
        if s == 1:
            gf = _dft_mats(nfft, n, nfft, -1.0, 1.0, 2, False)
            gi = _dft_mats(n, nfft, nfft, 1.0, 1.0 / nfft, 2, False)
            (z,) = _fft_stage([z, z], [sel_r, sel_i], gf, r_in=n, s_n=1, k_out=n, transposed_out=False,
                              g2mat=gi, spec=spec, spec_sel=spec_sel, gate=gate, name="hy_direct", **common)
        else:
            r1 = n // s
            g1 = _dft_mats(n1, r1, n1, -1.0, 1.0, 2, False)
            (a,) = _fft_stage([z, z], [sel_r, sel_i], g1, r_in=r1, s_n=s, k_out=n1, transposed_out=True,
                              tw=_twiddle(s, n1, nfft, -1.0, STAGE_ROWS), packed_out=True, name="hy_s1",
                              **common)
            g2 = _dft_mats(s, s, s, -1.0, 1.0, 2, False)
            g3 = _dft_mats(s, s, s, 1.0, 1.0, 2, False)
            (q,) = _fft_stage([a], [ident], g2, r_in=s, s_n=n1, k_out=s, transposed_out=True,
                              g2mat=g3, tw=_twiddle(n1, s, nfft, 1.0, STAGE_ROWS), spec=spec,
                              spec_sel=spec_sel, packed_in=True, packed_out=True, name="hy_mid", **common)
            g4 = _dft_mats(n // s, n1, n1, 1.0, 1.0 / nfft, 2, False)
            (z,) = _fft_stage([q], [ident], g4, r_in=n1, s_n=s, k_out=n // s, transposed_out=False,
                              gate=gate, packed_in=True, name="hy_last", **common)
        zblk = 0
    return z.reshape(b * n, w)


def _fnet(pq):
    b, n, _ = pq.shape
    w = FNET_WIDTH
    wblk = w // LANES
    first = 3 * HYENA_WIDTH // LANES
    scale = 1.0 / math.sqrt(n * FNET_GROUP_DIM)
    sel_r = lambda g, c: (g, first + c)
    sel_i = lambda g, c: (g, first + wblk + c)
    ident = lambda g, c: (g, c)
    n1, s = _split_len(n)
    if s == 1:
        g = _dft_mats(n, n, n, -1.0, scale, 2, True)
        (y,) = _fft_stage([pq, pq], [sel_r, sel_i], g, r_in=n, s_n=1, k_out=n, n_groups=b, n_cblk=wblk,
                          transposed_out=False, real_out=True, name="fnet_direct")
        return y.reshape(b * n, w)
    g1 = _dft_mats(n1, n1, n1, -1.0, 1.0, 2, False)
    tw = _twiddle(s, n1, n, -1.0, STAGE_ROWS)
    (a,) = _fft_stage([pq, pq], [sel_r, sel_i], g1, r_in=n1, s_n=s, k_out=n1, n_groups=b, n_cblk=wblk,
                      transposed_out=True, real_out=False, tw=tw, packed_out=True, name="fnet_s1")
    g2 = _dft_mats(s, s, s, -1.0, scale, 2, True)
    (y,) = _fft_stage([a], [ident], g2, r_in=s, s_n=n1, k_out=s, n_groups=b, n_cblk=wblk,
                      transposed_out=False, real_out=True, packed_in=True, name="fnet_s2")
    return y.reshape(b * n, w)


def _merge_kernel(x_ref, gt_ref, a_ref, h_ref, f_ref, gate_ref, wa_ref, wh_ref, wf_ref, wo_ref, o_ref):
    d = D_MODEL
    m = gate_ref[:, 0:d].astype(F32) * _dot(a_ref[...], wa_ref[...])
    m = m + gate_ref[:, d:2 * d].astype(F32) * _dot(h_ref[...].astype(BF), wh_ref[...])
    m = m + gate_ref[:, 2 * d:3 * d].astype(F32) * _dot(f_ref[...].astype(BF), wf_ref[...])
    y = _dot(m.astype(BF), wo_ref[...])
    o_ref[...] = x_ref[...] + gt_ref[0] * y


def _merge(x2, gt, attn_o, hy_o, fn_o, gates, wa, wh, wf, wo, *, tm, mod_row):
    t, d = x2.shape
    tok = lambda i: (i, 0)
    full = lambda i: (0, 0)
    return pl.pallas_call(
        _merge_kernel,
        grid=(t // tm,),
        in_specs=[
            pl.BlockSpec((tm, d), tok),
            pl.BlockSpec((1, 1, d), lambda i: (mod_row(i), 0, 0)),
            pl.BlockSpec((tm, ATTN_WIDTH), tok),
            pl.BlockSpec((tm, HYENA_WIDTH), tok),
            pl.BlockSpec((tm, FNET_WIDTH), tok),
            pl.BlockSpec((tm, 3 * d), tok),
            pl.BlockSpec(wa.shape, full, pipeline_mode=pl.Buffered(1)),
            pl.BlockSpec(wh.shape, full, pipeline_mode=pl.Buffered(1)),
            pl.BlockSpec(wf.shape, full, pipeline_mode=pl.Buffered(1)),
            pl.BlockSpec(wo.shape, full, pipeline_mode=pl.Buffered(1)),
        ],
        out_specs=pl.BlockSpec((tm, d), tok),
        out_shape=jax.ShapeDtypeStruct((t, d), F32),
        compiler_params=_cparams("parallel"),
        name="merge",
    )(x2, gt, attn_o, hy_o, fn_o, gates, wa, wh, wf, wo)


def _ffn_kernel(x_ref, sh_ref, sc_ref, gt_ref, g_ref, wg_ref, wu_ref, wd_ref, o_ref, h_scr, acc_scr):
    f = pl.program_id(1)

    @pl.when(f == 0)
    def _():
        h_scr[...] = _modulated_norm(x_ref[...], g_ref[...], sc_ref[0], sh_ref[0]).astype(BF)
        acc_scr[...] = jnp.zeros_like(acc_scr)

    hb = h_scr[...]
    act = _silu(_dot(hb, wg_ref[...])) * _dot(hb, wu_ref[...])
    acc_scr[...] += _dot(act.astype(BF), wd_ref[...])

    @pl.when(f == pl.num_programs(1) - 1)
    def _():
        o_ref[...] = x_ref[...] + gt_ref[0] * acc_scr[...]


def _ffn(x2, sh, sc, gt, g, wg, wu, wd, *, tm, tf, mod_row):
    t, d = x2.shape
    ff = wg.shape[1]
    row3 = lambda i, f: (mod_row(i), 0, 0)
    wmode = dict(pipeline_mode=pl.Buffered(1)) if tf == ff else {}
    return pl.pallas_call(
        _ffn_kernel,
        grid=(t // tm, ff // tf),
        in_specs=[
            pl.BlockSpec((tm, d), lambda i, f: (i, 0)),
            pl.BlockSpec((1, 1, d), row3), pl.BlockSpec((1, 1, d), row3), pl.BlockSpec((1, 1, d), row3),
            pl.BlockSpec((1, d), lambda i, f: (0, 0)),
            pl.BlockSpec((d, tf), lambda i, f: (0, f), **wmode),
            pl.BlockSpec((d, tf), lambda i, f: (0, f), **wmode),
            pl.BlockSpec((tf, d), lambda i, f: (f, 0), **wmode),
        ],
        out_specs=pl.BlockSpec((tm, d), lambda i, f: (i, 0)),
        out_shape=jax.ShapeDtypeStruct((t, d), F32),
        scratch_shapes=[pltpu.VMEM((tm, d), BF), pltpu.VMEM((tm, d), F32)],
        compiler_params=_cparams("parallel", "arbitrary"),
        name="ffn_dense",
    )(x2, sh, sc, gt, g, wg, wu, wd)


def _top2(logits):
    lane = lax.broadcasted_iota(jnp.int32, logits.shape, 1)
    lg = jnp.where(lane < N_EXPERTS, logits, -jnp.inf)
    m1 = jnp.max(lg, axis=1, keepdims=True)
    i1 = jnp.min(jnp.where(lg == m1, lane, LANES), axis=1, keepdims=True)
    lg2 = jnp.where(lane == i1, -jnp.inf, lg)
    m2 = jnp.max(lg2, axis=1, keepdims=True)
    i2 = jnp.min(jnp.where(lg2 == m2, lane, LANES), axis=1, keepdims=True)
    e2 = jnp.exp(m2 - m1)
    w1 = 1.0 / (1.0 + e2)
    return i1, i2, w1, e2 * w1


GROUP_TILE = 256
GROUP_PAD = 32
GROUP_PIECES = GROUP_TILE // GROUP_PAD
SLOT_RADIX = 64.0


def _moe_group_kernel(x_ref, sh_ref, sc_ref, g_ref, wr_ref, tri_ref, xg_ref, ws_ref, slot_ref, cnt_ref,
                      h_scr, rows_scr, wm_scr):
    j = pl.program_id(1)
    tb = x_ref.shape[0]
    gt_rows = xg_ref.shape[1]

    @pl.when(j == 0)
    def _():
        h = _modulated_norm(x_ref[...], g_ref[...], sc_ref[0], sh_ref[0])
        h_scr[...] = h.astype(BF)
        i1, i2, w1, w2 = _top2(_dot3(h, wr_ref[...]))
        lane = lax.broadcasted_iota(jnp.int32, (tb, LANES), 1)
        oh0 = jnp.where(lane == i1, 1.0, 0.0)
        oh1 = jnp.where(lane == i2, 1.0, 0.0)
        c0 = jnp.sum(oh0, axis=0, keepdims=True)
        cnt = c0 + jnp.sum(oh1, axis=0, keepdims=True)
        tri = tri_ref[...]
        pre0 = _dot(tri, oh0.astype(BF))
        pre1 = _dot(tri, oh1.astype(BF)) + c0
        tiles = jnp.ceil(cnt * (1.0 / GROUP_PAD))
        upper = jnp.where(lax.broadcasted_iota(jnp.int32, (LANES, LANES), 0)
                          < lax.broadcasted_iota(jnp.int32, (LANES, LANES), 1), 1.0, 0.0).astype(BF)
        off = _dot(jnp.broadcast_to(tiles, (8, LANES)).astype(BF), upper)[0:1] * float(GROUP_PAD)
        slot0 = jnp.sum(oh0 * (off + pre0), axis=1, keepdims=True)
        slot1 = jnp.sum(oh1 * (off + pre1), axis=1, keepdims=True)
        slot_ref[0] = jnp.where(lane == 0, slot0, jnp.where(lane == 1, slot1, 0.0))
        cnt_ref[0] = jnp.broadcast_to(cnt, (8, LANES))
        hi0 = jnp.floor(slot0 * (1.0 / SLOT_RADIX))
        hi1 = jnp.floor(slot1 * (1.0 / SLOT_RADIX))
        digits = jnp.where(lane == 0, hi0, jnp.where(lane == 1, slot0 - SLOT_RADIX * hi0,
                           jnp.where(lane == 2, hi1, jnp.where(lane == 3, slot1 - SLOT_RADIX * hi1, 0.0))))
        sel = jnp.where(lax.broadcasted_iota(jnp.int32, (8, LANES), 0)
                        == lax.broadcasted_iota(jnp.int32, (8, LANES), 1), 1.0, 0.0).astype(BF)
        rows_scr[...] = _dot_nt(sel, digits.astype(BF))
        w1h, w1l = _split(w1)
        w1m, w1l = _split(w1 - w1h.astype(F32))
        w2h, w2l = _split(w2)
        w2m, w2l = _split(w2 - w2h.astype(F32))
        cols = [w1h, w1m, w1l, w2h, w2m, w2l]
        wm = jnp.zeros((tb, LANES), F32)
        for li, col in enumerate(cols):
            wm = jnp.where(lane == li, col.astype(F32), wm)
        wm_scr[...] = wm.astype(BF)

    rows = rows_scr[...]
    s0 = rows[0:1] * SLOT_RADIX + rows[1:2]
    s1 = rows[2:3] * SLOT_RADIX + rows[3:4]
    pos = (lax.broadcasted_iota(jnp.int32, (gt_rows, tb), 0) + j * gt_rows).astype(F32)
    g0 = jnp.where(pos == s0, 1.0, 0.0).astype(BF)
    g1 = jnp.where(pos == s1, 1.0, 0.0).astype(BF)
    xg_ref[0] = _dot(g0 + g1, h_scr[...]).astype(BF)
    lane_w = lax.broadcasted_iota(jnp.int32, (gt_rows, LANES), 1)
    wsum = (jnp.where(lane_w < 3, _dot(g0, wm_scr[...]), 0.0)
            + jnp.where((lane_w >= 3) & (lane_w < 6), _dot(g1, wm_scr[...]), 0.0))
    ws_ref[0] = jnp.broadcast_to(jnp.sum(wsum, axis=1, keepdims=True), (gt_rows, LANES))


def _moe_group(x2, sh, sc, g, wr_pad, *, tb, nt, mod_row):
    t, d = x2.shape
    nb = t // tb
    row3 = lambda b, j: (mod_row(b), 0, 0)
    tri = jnp.asarray(np.tril(np.ones((tb, tb), np.float32), -1)).astype(BF)
    return pl.pallas_call(
        _moe_group_kernel,
        grid=(nb, nt),
        in_specs=[
            pl.BlockSpec((tb, d), lambda b, j: (b, 0)),
            pl.BlockSpec((1, 1, d), row3), pl.BlockSpec((1, 1, d), row3),
            pl.BlockSpec((1, d), lambda b, j: (0, 0)),
            pl.BlockSpec((d, LANES), lambda b, j: (0, 0)),
            pl.BlockSpec((tb, tb), lambda b, j: (0, 0), pipeline_mode=pl.Buffered(1)),
        ],
        out_specs=[
            pl.BlockSpec((1, GROUP_TILE, d), lambda b, j: (b * nt + j, 0, 0)),
            pl.BlockSpec((1, GROUP_TILE, LANES), lambda b, j: (b * nt + j, 0, 0)),
            pl.BlockSpec((1, tb, LANES), lambda b, j: (b, 0, 0)),
            pl.BlockSpec((1, 8, LANES), lambda b, j: (b, 0, 0)),
        ],
        out_shape=[
            jax.ShapeDtypeStruct((nb * nt, GROUP_TILE, d), BF),
            jax.ShapeDtypeStruct((nb * nt, GROUP_TILE, LANES), F32),
            jax.ShapeDtypeStruct((nb, tb, LANES), F32),
            jax.ShapeDtypeStruct((nb, 8, LANES), F32),
        ],
        scratch_shapes=[pltpu.VMEM((tb, d), BF), pltpu.VMEM((8, tb), F32), pltpu.VMEM((tb, LANES), BF)],
        compiler_params=_cparams("parallel", "arbitrary"),
        name="moe_group",
    )(x2, sh, sc, g, wr_pad, tri)


def _moe_schedule(cnt, nh):
    np_ = GROUP_PIECES
    h = (cnt + GROUP_PAD - 1) // GROUP_PAD
    nb, ne = h.shape
    tot = h.sum(0)
    pairs = (tot + np_ - 1) // np_
    cum_p = jnp.cumsum(pairs)
    start_p = cum_p - pairs
    n_used = cum_p[-1]
    n_steps = (nb * nh + ne * (np_ - 1)) // np_
    q = jnp.minimum(jnp.arange(n_steps, dtype=jnp.int32), n_used - 1)
    e = jnp.sum(q[:, None] >= cum_p[None, :], axis=1).astype(jnp.int32)
    r = q - start_p[e]
    cum_b = jnp.cumsum(h, axis=0)
    first = jnp.cumsum(h, axis=1) - h

    idx = jnp.minimum(np_ * r[:, None] + jnp.arange(np_, dtype=jnp.int32)[None, :], tot[e][:, None] - 1)
    blk = jnp.sum(idx[:, :, None] >= cum_b.T[e][:, None, :], axis=2).astype(jnp.int32)
    in_blk = (first - (cum_b - h))[blk, e[:, None]]
    pieces = (blk * nh + in_blk + idx).T.reshape(-1).astype(jnp.int32)

    x = jnp.arange(nh, dtype=jnp.int32)[None, :]
    ex = jnp.sum(x[:, :, None] >= jnp.cumsum(h, axis=1)[:, None, :], axis=2).astype(jnp.int32)
    exc = jnp.minimum(ex, ne - 1)
    g = jnp.take_along_axis(cum_b - h - first, exc, axis=1) + x
    loc = np_ * (start_p[exc] + g // np_) + g % np_
    loc = jnp.where(ex < ne, loc, loc[:, 0:1])
    return (pieces, e, n_used.astype(jnp.int32).reshape(1), loc.reshape(-1).astype(jnp.int32),
            ((h.sum(1) + np_ - 1) // np_).astype(jnp.int32))


def _moe_expert_kernel(pc_ref, exp_ref, nused_ref, *refs):
    np_ = GROUP_PIECES
    x_refs, w_refs = refs[:np_], refs[np_:2 * np_]
    wg_ref, wu_ref, wd_ref, y_ref = refs[2 * np_:]

    @pl.when(pl.program_id(0) < nused_ref[0])
    def _():
        x = jnp.concatenate([r[0] for r in x_refs], axis=0)
        act = _silu(_dot(x, wg_ref[0])) * _dot(x, wu_ref[0])
        y = _dot(act.astype(BF), wd_ref[0])
        w = jnp.concatenate([r[0] for r in w_refs], axis=0)
        y_ref[0] = (y * jnp.concatenate([w] * (y.shape[1] // LANES), axis=1)).astype(BF)


def _moe_experts(pieces, step_exp, n_used, xg, ws, wg, wu, wd):
    d = xg.shape[-1]
    ff = wg.shape[2]
    np_ = GROUP_PIECES
    n_steps = step_exp.shape[0]
    xh = xg.reshape(-1, GROUP_PAD, d)
    wh = ws.reshape(-1, GROUP_PAD, LANES)
    pc3 = lambda k: (lambda i, pc, se, nu: (pc[k * n_steps + i], 0, 0))
    exp3 = lambda i, pc, se, nu: (se[i], 0, 0)
    return pl.pallas_call(
        _moe_expert_kernel,
        grid_spec=pltpu.PrefetchScalarGridSpec(
            num_scalar_prefetch=3,
            grid=(n_steps,),
            in_specs=(
                [pl.BlockSpec((1, GROUP_PAD, d), pc3(k)) for k in range(np_)]
                + [pl.BlockSpec((1, GROUP_PAD, LANES), pc3(k)) for k in range(np_)]
                + [pl.BlockSpec((1, d, ff), exp3), pl.BlockSpec((1, d, ff), exp3), pl.BlockSpec((1, ff, d), exp3)]
            ),
            out_specs=pl.BlockSpec((1, GROUP_TILE, d), lambda i, pc, se, nu: (jnp.minimum(i, nu[0] - 1), 0, 0)),
        ),
        out_shape=jax.ShapeDtypeStruct((n_steps, GROUP_TILE, d), BF),
        compiler_params=_cparams("arbitrary"),
        name="moe_experts",
    )(pieces, step_exp, n_used, *([xh] * np_), *([wh] * np_), wg, wu, wd)


def _moe_combine_kernel(nt_ref, loc_ref, x_ref, gt_ref, slot_ref, *refs):
    y_refs, o_ref, acc_scr = refs[:GROUP_PIECES], refs[GROUP_PIECES], refs[GROUP_PIECES + 1]
    b = pl.program_id(0)
    j = pl.program_id(1)
    tb = x_ref.shape[0]

    @pl.when(j == 0)
    def _():
        acc_scr[...] = jnp.zeros_like(acc_scr)

    @pl.when(j < nt_ref[b])
    def _():
        sl = slot_ref[0]
        pos = (lax.broadcasted_iota(jnp.int32, (tb, GROUP_TILE), 1) + j * GROUP_TILE).astype(F32)
        p = jnp.where((pos == sl[:, 0:1]) | (pos == sl[:, 1:2]), 1.0, 0.0).astype(BF)
        acc_scr[...] += _dot(p, jnp.concatenate([r[0] for r in y_refs], axis=0))

    @pl.when(j == pl.num_programs(1) - 1)
    def _():
        o_ref[...] = x_ref[...] + gt_ref[0] * acc_scr[...]


def _moe_combine(ntiles_b, loc, x2, gt, slots, yg, *, tb, nt, mod_row):
    t, d = x2.shape
    nb = t // tb
    np_ = GROUP_PIECES
    yh = yg.reshape(-1, GROUP_PAD, d)

    def piece3(k):
        return lambda b, j, n, lc: (lc[(b * nt + jnp.minimum(j, n[b] - 1)) * np_ + k], 0, 0)

    return pl.pallas_call(
        _moe_combine_kernel,
        grid_spec=pltpu.PrefetchScalarGridSpec(
            num_scalar_prefetch=2,
            grid=(nb, nt),
            in_specs=[
                pl.BlockSpec((tb, d), lambda b, j, n, lc: (b, 0)),
                pl.BlockSpec((1, 1, d), lambda b, j, n, lc: (mod_row(b), 0, 0)),
                pl.BlockSpec((1, tb, LANES), lambda b, j, n, lc: (b, 0, 0)),
            ] + [pl.BlockSpec((1, GROUP_PAD, d), piece3(k)) for k in range(np_)],
            out_specs=pl.BlockSpec((tb, d), lambda b, j, n, lc: (b, 0)),
            scratch_shapes=[pltpu.VMEM((tb, d), F32)],
        ),
        out_shape=jax.ShapeDtypeStruct((t, d), F32),
        compiler_params=_cparams("parallel", "arbitrary"),
        name="moe_combine",
    )(ntiles_b, loc, x2, gt, slots, *([yh] * np_))


def _moe(x2, sh, sc, gt, g, wr_pad, wg, wu, wd, *, tb, mod_row):
    nt = -(-(2 * tb + N_EXPERTS * (GROUP_PAD - 1)) // GROUP_TILE)
    xg, ws, slots, cnt = _moe_group(x2, sh, sc, g, wr_pad, tb=tb, nt=nt, mod_row=mod_row)
    counts = cnt[:, 0, :N_EXPERTS].astype(jnp.int32)
    pieces, step_exp, n_used, loc, ntiles_b = _moe_schedule(counts, nt * GROUP_PIECES)
    yg = _moe_experts(pieces, step_exp, n_used, xg, ws, wg, wu, wd)
    return _moe_combine(ntiles_b, loc, x2, gt, slots, yg, tb=tb, nt=nt, mod_row=mod_row)


def _rope_tables(seq_len):
    pos = np.arange(seq_len)
    prow = (pos // GRID_W).astype(np.float32)
    pcol = (pos % GRID_W).astype(np.float32)
    n_freq = HEAD_DIM // 4
    inv = (np.float32(ROPE_THETA) ** (-np.arange(n_freq, dtype=np.float32) / n_freq)).astype(np.float32)
    ar = (prow[:, None] * inv[None, :]).astype(np.float64)
    ac = (pcol[:, None] * inv[None, :]).astype(np.float64)
    cos = np.concatenate([np.cos(ar)] * 2 + [np.cos(ac)] * 2, axis=1)
    sin = np.concatenate([-np.sin(ar), np.sin(ar), -np.sin(ac), np.sin(ac)], axis=1)
    return (jnp.asarray(np.concatenate([cos, cos], axis=1), dtype=F32),
            jnp.asarray(np.concatenate([sin, sin], axis=1), dtype=F32))


def _head_sum_matrix():
    c = np.arange(ATTN_WIDTH)
    return jnp.asarray((c[:, None] // HEAD_DIM) == (c[None, :] // HEAD_DIM), dtype=F32).astype(BF)


def _fnet_channel_matrix():
    c = np.arange(FNET_WIDTH)
    same = (c[:, None] // FNET_GROUP_DIM) == (c[None, :] // FNET_GROUP_DIM)
    ang = 2.0 * np.pi * (((c[:, None] % FNET_GROUP_DIM) * (c[None, :] % FNET_GROUP_DIM)) % FNET_GROUP_DIM) / FNET_GROUP_DIM
    cb = np.where(same, np.cos(ang), 0.0)
    sb = np.where(same, np.sin(ang), 0.0)
    return jnp.asarray(np.concatenate([cb, -sb], axis=1), dtype=F32).astype(BF)


def _filter_features(n):
    t = np.linspace(0.0, 1.0, n)[:, None]
    w = 2.0 * np.pi * np.arange(n)[:, None] / n
    fb = np.linspace(1e-4, FILTER_BANDS - 1, FILTER_BANDS)
    z = np.concatenate([t, np.cos(fb * w), -np.sin(fb * w)], axis=-1)
    return jnp.asarray(np.pad(z, ((0, 0), (0, 64 - z.shape[1]))), dtype=F32)


def _decay_rates():
    d = jnp.abs(jnp.linspace(math.log(DECAY_TARGET) / SLOW_DECAY_PCT, math.log(DECAY_TARGET) / FAST_DECAY_PCT,
                             HYENA_WIDTH, dtype=F32))
    return jnp.concatenate([d] * HYENA_ORDER)[None, :]


def kernel(x, c, ctx, c_ctx, w_ada, b_ada, norm1_g, norm2_g, w_in, q_norm_g, k_norm_g, attn_sink,
           hy_conv_w, hy_conv_b, hy_filt_w1, hy_filt_b1, hy_filt_freq1, hy_filt_w2, hy_filt_b2,
           hy_filt_freq2, hy_filt_w3, hy_bias, w_proj_attn, w_proj_hyena, w_proj_fnet, w_out,
           ffn_w_gate, ffn_w_up, ffn_w_down, moe_router, moe_w_gate, moe_w_up, moe_w_down):
    b, seq, d = x.shape
    n_ctx = ctx.shape[1]
    depth = w_ada.shape[0]
    tm = 512
    tiles_per_seq = seq // tm

    cond8 = jnp.concatenate([c, c_ctx[None, :], jnp.zeros((8 - b - 1, d), F32)], axis=0)
    mods = _adaln(cond8, w_ada, b_ada)

    cos_l, sin_l = _rope_tables(seq)
    cos_c = jnp.ones((n_ctx, LANES), F32)
    sin_c = jnp.zeros((n_ctx, LANES), F32)
    gsum = _head_sum_matrix()
    mfn = _fnet_channel_matrix()
    deltas = _decay_rates()
    zfeat_l = _filter_features(seq)
    zfeat_c = _filter_features(n_ctx)

    lat_row = lambda i: i // tiles_per_seq
    ctx_row = lambda i: b
    lat_tab = lambda i: i % tiles_per_seq
    ctx_tab = lambda i: 0
    tm_c = min(tm, n_ctx)

    xs = x.reshape(b * seq, d)
    cs = ctx.reshape(b * n_ctx, d)
    for l in range(depth):
        last = l == depth - 1
        mod = lambda j: mods[l, :, j * d:(j + 1) * d].reshape(8, 1, d)
        w_in_bf = _layer_bf16(w_in, l, 256)
        qg = jnp.tile(q_norm_g[l], N_HEADS)[None, :]
        kg = jnp.tile(k_norm_g[l], N_KV_HEADS)[None, :]
        g1 = norm1_g[l][None, :]
        wa, wh, wf, wo = (w_proj_attn[l].astype(BF), w_proj_hyena[l].astype(BF),
                          w_proj_fnet[l].astype(BF), w_out[l].astype(BF))
        conv_w = hy_conv_w[l].reshape(3, -1)
        conv_b = hy_conv_b[l][None, :]
        w1p = jnp.pad(hy_filt_w1[l], ((0, 64 - hy_filt_w1.shape[1]), (0, 0)))
        filt = (w1p, hy_filt_b1[l][None, :], hy_filt_freq1[l][None, :], hy_filt_w2[l],
                hy_filt_b2[l][None, :], hy_filt_freq2[l][None, :], hy_filt_w3[l], deltas)

        qkv_c, upq_c, gates_c = _phase_a(
            cs, mod(0), mod(1), g1, w_in_bf, cos_c, sin_c, qg, kg, gsum, mfn, conv_w, conv_b,
            tm=tm_c, tiles_per_seq=n_ctx // tm_c, mod_row=ctx_row, tab_row=ctx_tab)
        qkv_c = qkv_c.reshape(b, n_ctx, -1)
        upq_c = upq_c.reshape(b, n_ctx, -1)

        qkv_l, upq_l, gates_l = _phase_a(
            xs, mod(0), mod(1), g1, w_in_bf, cos_l, sin_l, qg, kg, gsum, mfn, conv_w, conv_b,
            tm=tm, tiles_per_seq=tiles_per_seq, mod_row=lat_row, tab_row=lat_tab)
        upq_l = upq_l.reshape(b, seq, -1)
        attn_l = _attention(attn_sink[l], qkv_l.reshape(b, seq, -1), qkv_c, local=True, tq=512)
        kf_l, kb_l, nrm_l = _hy_filter(zfeat_l, *filt, tm=1024)
        hy_l = _hyena(upq_l, (kf_l, kb_l), nrm_l, hy_bias[l])
        fn_l = _fnet(upq_l)
        tm_merge = 2 * tm
        xs = _merge(xs, mod(2), attn_l.reshape(b * seq, -1), hy_l, fn_l, gates_l, wa, wh, wf, wo,
                    tm=tm_merge, mod_row=lambda t: t // (seq // tm_merge))

        if not last:
            attn_c = _attention(attn_sink[l], qkv_c, qkv_c, local=False, tq=n_ctx)
            kf_c, kb_c, nrm_c = _hy_filter(zfeat_c, *filt, tm=n_ctx)
            hy_c = _hyena(upq_c, (kf_c, kb_c), nrm_c, hy_bias[l])
            fn_c = _fnet(upq_c)
            cs = _merge(cs, mod(2), attn_c.reshape(b * n_ctx, -1), hy_c, fn_c, gates_c, wa, wh, wf, wo,
                        tm=tm_c, mod_row=ctx_row)

        g2 = norm2_g[l][None, :]
        i = l // 2
        if l % 2 == 0:
            wg, wu, wd = (_layer_bf16(ffn_w_gate, i, 512), _layer_bf16(ffn_w_up, i, 512),
                          _layer_bf16(ffn_w_down, i, D_FF // 4))
            run = lambda t2, rows, tmm: _ffn(t2, mod(3), mod(4), mod(5), g2, wg, wu, wd,
                                             tm=tmm, tf=D_FF, mod_row=rows)
        else:
            wr = jnp.pad(moe_router[i], ((0, 0), (0, LANES - N_EXPERTS)))
            wg, wu, wd = (_layer_bf16(moe_w_gate, i, 512), _layer_bf16(moe_w_up, i, 512),
                          _layer_bf16(moe_w_down, i, D_FF // 4))
            run = lambda t2, rows, tmm: _moe(t2, mod(3), mod(4), mod(5), g2, wr, wg, wu, wd,
                                             tb=tmm, mod_row=rows)
        tm_ffn = 512
        xs = run(xs, lambda t: t // (seq // tm_ffn), tm_ffn)
        if not last:
            cs = run(cs, ctx_row, min(tm_ffn, b * n_ctx))
    return xs.reshape(b, seq, d)
```

```python
import functools
import math

import numpy as np
import jax
import jax.numpy as jnp
from jax import lax
from jax.experimental import pallas as pl
from jax.experimental.pallas import tpu as pltpu

F32 = jnp.float32
BF = jnp.bfloat16

D_MODEL = 1024
GRID_W = 64
HEAD_DIM = 64
N_HEADS = 8
N_KV_HEADS = 2
ATTN_WIDTH = N_HEADS * HEAD_DIM
KV_WIDTH = N_KV_HEADS * HEAD_DIM
WINDOW = 128
QBLK = 128
ROPE_THETA = 10000.0
HYENA_ORDER = 2
HYENA_WIDTH = 256
FILTER_BANDS = 16
FILTER_HIDDEN = 64
DECAY_TARGET = 1e-2
FAST_DECAY_PCT = 0.3
SLOW_DECAY_PCT = 1.5
FNET_WIDTH = 256
FNET_GROUP_DIM = 64
Q_END = ATTN_WIDTH
K_END = Q_END + KV_WIDTH
V_END = K_END + KV_WIDTH
HY_END = V_END + (HYENA_ORDER + 1) * HYENA_WIDTH
FN_END = HY_END + FNET_WIDTH
IN_WIDTH = FN_END + 3 * D_MODEL
D_FF = 2816
N_EXPERTS = 8
EPS = 1e-6
LANES = 128
NEG = -1e30
STAGE_ROWS = 16
STAGE_JOIN = 2

VMEM_LIMIT = 56 * 1024 * 1024


def _cparams(*sem):
    return pltpu.CompilerParams(dimension_semantics=sem, vmem_limit_bytes=VMEM_LIMIT)


def _dot(a, b):
    return jnp.dot(a, b, preferred_element_type=F32)


def _dot_nt(a, b):
    return lax.dot_general(a, b, (((1,), (1,)), ((), ())), preferred_element_type=F32)


def _split(a):
    hi = a.astype(BF)
    lo = (a - hi.astype(F32)).astype(BF)
    return hi, lo


def _dot3(a, b):
    ah, al = _split(a)
    bh, bl = _split(b)
    return _dot(ah, bh) + (_dot(ah, bl) + _dot(al, bh))


def _dot2(a, b_bf16):
    ah, al = _split(a)
    return _dot(ah, b_bf16) + _dot(al, b_bf16)


def _sigmoid(v):
    return 0.5 * jnp.tanh(0.5 * v) + 0.5


def _silu(v):
    return v * _sigmoid(v)


def _cast_kernel(w_ref, o_ref):
    o_ref[...] = w_ref[...].astype(BF)


def _layer_bf16(w, layer, rows):
    a, b = w.shape[-2:]
    inner = int(np.prod(w.shape[1:-2], dtype=np.int64))
    w3 = w.reshape(-1, a, b)
    out = pl.pallas_call(
        _cast_kernel,
        grid=(inner, a // rows),
        in_specs=[pl.BlockSpec((1, rows, b), lambda m, i: (layer * inner + m, i, 0))],
        out_specs=pl.BlockSpec((1, rows, b), lambda m, i: (m, i, 0)),
        out_shape=jax.ShapeDtypeStruct((inner, a, b), BF),
        compiler_params=_cparams("parallel", "parallel"),
        name="cast_bf16",
    )(w3)
    return out.reshape(w.shape[1:])


def _adaln_kernel(c_ref, w_ref, b_ref, o_ref):
    o_ref[0] = _dot3(_silu(c_ref[...]), w_ref[0]) + b_ref[0]


def _adaln(cond8, w_ada, b_ada):
    depth, d, n6 = w_ada.shape
    tn = 1024
    return pl.pallas_call(
        _adaln_kernel,
        grid=(depth, n6 // tn),
        in_specs=[
            pl.BlockSpec((8, d), lambda l, j: (0, 0)),
            pl.BlockSpec((1, d, tn), lambda l, j: (l, 0, j)),
            pl.BlockSpec((1, 1, tn), lambda l, j: (l, 0, j)),
        ],
        out_specs=pl.BlockSpec((1, 8, tn), lambda l, j: (l, 0, j)),
        out_shape=jax.ShapeDtypeStruct((depth, 8, n6), F32),
        compiler_params=_cparams("parallel", "parallel"),
        name="adaln",
    )(cond8, w_ada, b_ada.reshape(depth, 1, n6))


def _modulated_norm(x, g, sc, sh):
    ms = jnp.mean(x * x, axis=-1, keepdims=True)
    h = (x * lax.rsqrt(ms + EPS)) * g
    return h * (1.0 + sc) + sh


HALO = 16


def _phase_a_kernel(x_ref, xp_ref, xn_ref, sh_ref, sc_ref, g_ref, w_ref, cos_ref, sin_ref, qg_ref, kg_ref,
                    gsum_ref, mfn_ref, cw_ref, cb_ref, qkv_ref, upq_ref, gate_ref, *, tiles_per_seq):
    tm = x_ref.shape[0]
    norm = lambda xv: _modulated_norm(xv, g_ref[...], sc_ref[0], sh_ref[0]).astype(BF)
    hb = norm(x_ref[...])
    cos = cos_ref[...]
    sin = sin_ref[...]

    def headnorm(t, gain, gs):
        ss = _dot2(t * t, gs)
        return t * lax.rsqrt(ss * (1.0 / HEAD_DIM) + EPS) * gain

    def rope(t, cosw, sinw):
        w = t.shape[1]
        nxt = pltpu.roll(t, w - 16, axis=1)
        prv = pltpu.roll(t, 16, axis=1)
        lw = lax.broadcasted_iota(jnp.int32, t.shape, 1)
        return t * cosw + jnp.where((lw % 32) < 16, nxt, prv) * sinw

    def dup_halves(t):
        lane = lax.broadcasted_iota(jnp.int32, t.shape, 1)
        sw = pltpu.roll(t, 64, axis=1)
        lo = lane < 64
        return jnp.concatenate([jnp.where(lo, t, sw), jnp.where(lo, sw, t)], axis=1)

    pr = _dot(hb, w_ref[:, 0:FN_END])
    pq = pr[:, 0:Q_END]
    qn = headnorm(pq, qg_ref[...], gsum_ref[...])
    cos4 = jnp.concatenate([cos] * 4, axis=1)
    sin4 = jnp.concatenate([sin] * 4, axis=1)
    qkv_ref[:, 0:Q_END] = (rope(qn, cos4, sin4) * (HEAD_DIM ** -0.5)).astype(BF)

    kn = headnorm(pr[:, Q_END:K_END], kg_ref[...], gsum_ref[0:KV_WIDTH, 0:KV_WIDTH])
    qkv_ref[:, Q_END:Q_END + 2 * KV_WIDTH] = dup_halves(rope(kn, cos, sin)).astype(BF)
    qkv_ref[:, Q_END + 2 * KV_WIDTH:] = dup_halves(pr[:, K_END:V_END]).astype(BF)

    w_hy = w_ref[:, V_END:HY_END]
    u = pr[:, V_END:HY_END]
    tile = pl.program_id(0) % tiles_per_seq
    u_before = jnp.where(tile == 0, 0.0, _dot(norm(xp_ref[...]), w_hy)[HALO - 1:HALO])
    u_after = jnp.where(tile == tiles_per_seq - 1, 0.0, _dot(norm(xn_ref[...]), w_hy)[0:1])
    row = lax.broadcasted_iota(jnp.int32, (tm, 1), 0)
    prv = jnp.where(row == 0, u_before, pltpu.roll(u, 1, axis=0))
    nxt = jnp.where(row == tm - 1, u_after, pltpu.roll(u, tm - 1, axis=0))
    n_hy = HY_END - V_END
    upq_ref[:, 0:n_hy] = prv * cw_ref[0:1, :] + u * cw_ref[1:2, :] + nxt * cw_ref[2:3, :] + cb_ref[...]

    upq_ref[:, n_hy:] = _dot(pr[:, HY_END:FN_END].astype(BF), mfn_ref[...])
    gate_ref[...] = _sigmoid(_dot(hb, w_ref[:, FN_END:])).astype(BF)


def _phase_a(x2, sh, sc, g, w_in_bf, cos_t, sin_t, qg, kg, gsum, mfn, conv_w, conv_b, *, tm, tiles_per_seq,
             mod_row, tab_row):
    t, d = x2.shape
    row3 = lambda i: (mod_row(i), 0, 0)
    full = lambda i: (0, 0)
    tok = lambda i: (i, 0)
    n_halo = t // HALO
    outs = [
        ((t, ATTN_WIDTH + 4 * KV_WIDTH), BF), ((t, 3 * HYENA_WIDTH + 2 * FNET_WIDTH), F32),
        ((t, 3 * D_MODEL), BF),
    ]
    kern = functools.partial(_phase_a_kernel, tiles_per_seq=tiles_per_seq)
    return pl.pallas_call(
        kern,
        grid=(t // tm,),
        in_specs=[
            pl.BlockSpec((tm, d), tok),
            pl.BlockSpec((HALO, d), lambda i: (jnp.maximum(i * (tm // HALO) - 1, 0), 0)),
            pl.BlockSpec((HALO, d), lambda i: (jnp.minimum((i + 1) * (tm // HALO), n_halo - 1), 0)),
            pl.BlockSpec((1, 1, d), row3),
            pl.BlockSpec((1, 1, d), row3),
            pl.BlockSpec((1, d), full),
            pl.BlockSpec((d, IN_WIDTH), full, pipeline_mode=pl.Buffered(1)),
            pl.BlockSpec((tm, LANES), lambda i: (tab_row(i), 0)),
            pl.BlockSpec((tm, LANES), lambda i: (tab_row(i), 0)),
            pl.BlockSpec((1, ATTN_WIDTH), full),
            pl.BlockSpec((1, KV_WIDTH), full),
            pl.BlockSpec((ATTN_WIDTH, ATTN_WIDTH), full),
            pl.BlockSpec((FNET_WIDTH, 2 * FNET_WIDTH), full),
            pl.BlockSpec(conv_w.shape, full),
            pl.BlockSpec(conv_b.shape, full),
        ],
        out_specs=[pl.BlockSpec((tm, s[1]), tok) for s, _ in outs],
        out_shape=[jax.ShapeDtypeStruct(s, dt) for s, dt in outs],
        compiler_params=_cparams("parallel"),
        name="phase_a",
    )(x2, x2, x2, sh, sc, g, w_in_bf, cos_t, sin_t, qg, kg, gsum, mfn, conv_w, conv_b)


def _attn_kernel(sink_ref, q_ref, kd_ref, vd_ref, kc_ref, vc_ref, o_ref, *, local, seq_len):
    tq = q_ref.shape[1]
    nblk = tq // QBLK
    gq = N_HEADS // N_KV_HEADS
    rows = gq * QBLK
    lane = lax.broadcasted_iota(jnp.int32, (QBLK, LANES), 1)
    lo_half = lane < 64
    hrow = lax.broadcasted_iota(jnp.int32, (rows, 1), 0) // QBLK
    nband = 3 * QBLK
    if local:
        qk_off = (lax.broadcasted_iota(jnp.int32, (rows, nband), 0) % QBLK
                  - lax.broadcasted_iota(jnp.int32, (rows, nband), 1))
    for blk in range(nblk):
        r0 = blk * QBLK
        qb = q_ref[0, r0:r0 + QBLK, :]
        if local:
            n = pl.program_id(1) * nblk + blk
            start = pl.multiple_of(jnp.clip((n - 1) * QBLK, 0, seq_len - nband), QBLK)
            valid = jnp.abs(qk_off + (n * QBLK - start)) <= WINDOW
        for g in range(N_KV_HEADS):
            parts = []
            for hh in range(gq):
                h = gq * g + hh
                qc = qb[:, (h // 2) * LANES:(h // 2 + 1) * LANES]
                keep = lo_half if h % 2 == 0 else jnp.logical_not(lo_half)
                parts.append(jnp.where(keep, qc, jnp.zeros_like(qc)))
            q4 = jnp.concatenate(parts, axis=0)
            sk = jnp.full((rows, 1), sink_ref[gq * g + gq - 1], F32)
            for hh in range(gq - 2, -1, -1):
                sk = jnp.where(hrow == hh, sink_ref[gq * g + hh], sk)
            gl = slice(g * LANES, (g + 1) * LANES)
            s_ctx = _dot_nt(q4, kc_ref[0, :, gl])
            m = jnp.maximum(jnp.max(s_ctx, axis=1, keepdims=True), sk)
            if local:
                s_loc = _dot_nt(q4, kd_ref[0, pl.ds(start, nband), gl])
                s_loc = jnp.where(valid, s_loc, NEG)
                m = jnp.maximum(m, jnp.max(s_loc, axis=1, keepdims=True))
            p_ctx = jnp.exp(s_ctx - m)
            den = jnp.sum(p_ctx, axis=1, keepdims=True) + jnp.exp(sk - m)
            o = _dot(p_ctx.astype(BF), vc_ref[0, :, gl])
            if local:
                p_loc = jnp.exp(s_loc - m)
                den = den + jnp.sum(p_loc, axis=1, keepdims=True)
                o = o + _dot(p_loc.astype(BF), vd_ref[0, pl.ds(start, nband), gl])
            o = o / den
            for cc in range(gq // 2):
                col = (gq // 2) * g + cc
                oa = o[(2 * cc) * QBLK:(2 * cc + 1) * QBLK]
                ob = o[(2 * cc + 1) * QBLK:(2 * cc + 2) * QBLK]
                o_ref[0, r0:r0 + QBLK, col * LANES:(col + 1) * LANES] = (
                    jnp.where(lo_half, oa, ob).astype(BF))


def _attention(sink, qkv, qkv_ctx, *, local, tq):
    b, lq, _ = qkv.shape
    c = qkv_ctx.shape[1]
    kw = 2 * KV_WIDTH
    k_blk, v_blk = ATTN_WIDTH // kw, ATTN_WIDTH // kw + 1
    kern = functools.partial(_attn_kernel, local=local, seq_len=lq)
    return pl.pallas_call(
        kern,
        grid=(b, lq // tq),
        in_specs=[
            pl.BlockSpec(memory_space=pltpu.SMEM),
            pl.BlockSpec((1, tq, ATTN_WIDTH), lambda bi, i: (bi, i, 0)),
            pl.BlockSpec((1, lq, kw), lambda bi, i: (bi, 0, k_blk)),
            pl.BlockSpec((1, lq, kw), lambda bi, i: (bi, 0, v_blk)),
            pl.BlockSpec((1, c, kw), lambda bi, i: (bi, 0, k_blk)),
            pl.BlockSpec((1, c, kw), lambda bi, i: (bi, 0, v_blk)),
        ],
        out_specs=pl.BlockSpec((1, tq, ATTN_WIDTH), lambda bi, i: (bi, i, 0)),
        out_shape=jax.ShapeDtypeStruct((b, lq, ATTN_WIDTH), BF),
        compiler_params=_cparams("parallel", "parallel"),
        name="attn_local" if local else "attn_ctx",
    )(sink, qkv, qkv, qkv, qkv_ctx, qkv_ctx)


FILTER_HALO = 128


def _hy_filter_kernel(z_ref, zn_ref, w1_ref, b1_ref, f1_ref, w2_ref, b2_ref, f2_ref, w3f_ref, w3b_ref, dl_ref,
                      flip_ref, kf_ref, kb_ref, nrm_ref):
    i = pl.program_id(0)
    tm = z_ref.shape[0]
    rows = tm + FILTER_HALO
    z = jnp.concatenate([z_ref[...], zn_ref[...]], axis=0)
    h = jnp.sin(f1_ref[...] * (_dot3(z, w1_ref[...]) + b1_ref[...]))
    h = jnp.sin(f2_ref[...] * (_dot3(h, w2_ref[...]) + b2_ref[...]))
    tcol = jnp.where(lax.broadcasted_iota(jnp.int32, (rows, LANES - FILTER_HIDDEN), 1) == 0, z[:, 0:1], 0.0)
    hid = jnp.concatenate([h, tcol], axis=1)

    def taps(hv, w3_ref):
        return _dot3(hv, w3_ref[...]) * jnp.exp(-hv[:, FILTER_HIDDEN:FILTER_HIDDEN + 1] * dl_ref[...])

    kf = taps(hid[0:tm], w3f_ref)
    flip = flip_ref[...]
    h1 = hid.astype(BF)
    r1 = hid - h1.astype(F32)
    h2 = r1.astype(BF)
    h3 = (r1 - h2.astype(F32)).astype(BF)
    hid_rev = _dot(flip, h1) + (_dot(flip, h2) + _dot(flip, h3))
    kb = taps(hid_rev, w3b_ref)
    out_row = (pl.num_programs(0) - 1 - i) * tm + lax.broadcasted_iota(jnp.int32, (tm, 1), 0)
    kb = jnp.where(out_row == 0, 0.0, kb)
    kf_ref[...] = kf
    kb_ref[...] = kb

    @pl.when(i == 0)
    def _():
        nrm_ref[...] = jnp.zeros_like(nrm_ref)

    nrm_ref[...] += jnp.sum(jnp.abs(kf), axis=0, keepdims=True) + jnp.sum(jnp.abs(kb), axis=0, keepdims=True)


def _hy_filter(zfeat, w1p, b1, f1, w2, b2, f2, w3, deltas2, *, tm):
    n = zfeat.shape[0]
    nb = n // tm
    wout = w3.shape[1] // 2
    w3p = jnp.pad(w3, ((0, LANES - w3.shape[0]), (0, 0)))
    p = np.arange(tm)[:, None]
    flip = jnp.asarray(np.arange(tm + FILTER_HALO)[None, :] == tm - p, dtype=F32).astype(BF)
    full = lambda i: (0, 0)
    halo_blk = lambda i: (jnp.minimum((i + 1) * (tm // FILTER_HALO), n // FILTER_HALO - 1), 0)
    return pl.pallas_call(
        _hy_filter_kernel,
        grid=(nb,),
        in_specs=[
            pl.BlockSpec((tm, zfeat.shape[1]), lambda i: (i, 0)),
            pl.BlockSpec((FILTER_HALO, zfeat.shape[1]), halo_blk),
            pl.BlockSpec(w1p.shape, full), pl.BlockSpec(b1.shape, full), pl.BlockSpec(f1.shape, full),
            pl.BlockSpec(w2.shape, full), pl.BlockSpec(b2.shape, full), pl.BlockSpec(f2.shape, full),
            pl.BlockSpec((LANES, wout), lambda i: (0, 0)),
            pl.BlockSpec((LANES, wout), lambda i: (0, 1)),
            pl.BlockSpec(deltas2.shape, full),
            pl.BlockSpec(flip.shape, full, pipeline_mode=pl.Buffered(1)),
        ],
        out_specs=[pl.BlockSpec((tm, wout), lambda i: (i, 0)), pl.BlockSpec((tm, wout), lambda i: (nb - 1 - i, 0)),
                   pl.BlockSpec((1, wout), full)],
        out_shape=[jax.ShapeDtypeStruct((n, wout), F32), jax.ShapeDtypeStruct((n, wout), F32),
                   jax.ShapeDtypeStruct((1, wout), F32)],
        compiler_params=_cparams("arbitrary"),
        name="hy_filter",
    )(zfeat, zfeat, w1p, b1, f1, w2, b2, f2, w3p, w3p, deltas2, flip)


def _dft_mats(k_out, r_in, period, sign, scale, n_in, real_out):
    k = np.arange(k_out)[:, None]
    r = np.arange(r_in)[None, :]
    ang = 2.0 * np.pi * ((k * r) % period) / period
    fr = np.cos(ang) * scale
    fi = sign * np.sin(ang) * scale
    if real_out:
        mats = [fr, -fi]
    else:
        mats = [np.concatenate([fr, fi], 0), np.concatenate([-fi, fr], 0)]
    return jnp.asarray(np.stack(mats[:n_in], 0), dtype=F32).astype(BF)


def _twiddle(s_n, k_n, n, sign, sbk):
    s0 = lax.broadcasted_iota(jnp.int32, (s_n // sbk, k_n, LANES), 0) * sbk
    k = lax.broadcasted_iota(jnp.int32, (s_n // sbk, k_n, LANES), 1)
    ang = (s0 * k).astype(F32) * (2.0 * math.pi / n)
    ang1 = lax.broadcasted_iota(jnp.int32, (k_n, LANES), 0).astype(F32) * (2.0 * math.pi / n)
    return jnp.cos(ang), sign * jnp.sin(ang), jnp.cos(ang1), sign * jnp.sin(ang1)


def _unpack_pair(p):
    return [pltpu.unpack_elementwise(p, index=i, packed_dtype=BF, unpacked_dtype=F32) for i in (0, 1)]


def _stage_kernel(*refs, n_in, r_in, k_mid, k_out, sbk, tw, spec, second, gate, real_out,
                  transposed_out, flat, packed_in, packed_spec, packed_out):
    it = iter(refs)
    x_refs = [next(it) for _ in range(n_in)]
    g_ref = next(it)
    g2_ref = next(it) if second else None
    tw_refs = [next(it) for _ in range(4)] if tw else None
    n_spec = 1 if packed_spec else 2
    spec_refs = [next(it) for _ in range(n_spec + 1)] if spec else None
    gate_refs = [next(it) for _ in range(5)] if gate else None
    out_refs = [next(it)] if (real_out or gate or packed_out) else [next(it), next(it)]
    if not flat:
        x_refs = [r.reshape(r_in * sbk, LANES) for r in x_refs]
        if spec:
            spec_refs = [r.reshape(k_mid * sbk, LANES) for r in spec_refs[:n_spec]] + spec_refs[n_spec:]
        if gate:
            gate_refs = [r.reshape(k_out * sbk, LANES) for r in gate_refs[:4]] + gate_refs[4:]
            out_refs = [out_refs[0].reshape(2 * k_out * sbk, LANES)]
        elif not transposed_out:
            out_refs = [r.reshape(k_out * sbk, LANES) for r in out_refs]
    if spec:
        inv = 1.0 / spec_refs[n_spec][...]
    if tw:
        tr, ti = tw_refs[0][0], tw_refs[1][0]
        wr, wi = tw_refs[2][...], tw_refs[3][...]
    def joined_dots(mats_ref, cols):
        acc = None
        for xi in range(len(cols[0])):
            wide = jnp.concatenate([c[xi].astype(BF) for c in cols], axis=1)
            d = _dot(mats_ref[xi], wide)
            acc = d if acc is None else acc + d
        return [acc[:, k * LANES:(k + 1) * LANES] for k in range(len(cols))]

    join = 1 if flat else min(STAGE_JOIN, sbk)
    results = {}
    for j in range(sbk):
        if j % join == 0:
            cols = []
            for jj in range(j, j + join):
                parts = [x_ref[...] if flat else x_ref[pl.ds(jj, r_in, stride=sbk), :] for x_ref in x_refs]
                cols.append(_unpack_pair(parts[0]) if packed_in else parts)
            accs = joined_dots(g_ref, cols)
            if not real_out:
                mids = []
                for jj, acc in zip(range(j, j + join), accs):
                    yr, yi = acc[:k_mid], acc[k_mid:]
                    if spec:
                        rows = slice(None) if flat else pl.ds(jj, k_mid, stride=sbk)
                        if packed_spec:
                            sr, si = _unpack_pair(spec_refs[0][rows, :])
                        else:
                            sr, si = spec_refs[0][rows, :], spec_refs[1][rows, :]
                        sr, si = sr * inv, si * inv
                        yr, yi = yr * sr - yi * si, yr * si + yi * sr
                    mids.append([yr, yi])
                if second:
                    mids = [[acc[:k_out], acc[k_out:]] for acc in joined_dots(g2_ref, mids)]
                accs = mids
            results = dict(zip(range(j, j + join), accs))
        if real_out:
            ys = [results[j]]
        else:
            yr, yi = results[j]
            if tw:
                yr, yi = yr * tr - yi * ti, yr * ti + yi * tr
                if j + 1 < sbk:
                    tr, ti = tr * wr - ti * wi, tr * wi + ti * wr
            ys = [yr, yi]
        if gate:
            o_ref = out_refs[0]
            for part, y in enumerate(ys):
                rows = slice(None) if flat else pl.ds(j, k_out, stride=sbk)
                val = gate_refs[part][rows, :] * (y + gate_refs[4][...] * gate_refs[2 + part][rows, :])
                if flat:
                    o_ref[part] = val.astype(o_ref.dtype)
                else:
                    o_ref[pl.ds(part * k_out * sbk + j, k_out, stride=sbk), :] = val.astype(o_ref.dtype)
            continue
        if packed_out:
            ys = [pltpu.pack_elementwise(ys, packed_dtype=BF)]
        for o_ref, y in zip(out_refs, ys):
            if flat:
                o_ref[...] = y.astype(o_ref.dtype)
            elif transposed_out:
                o_ref[0, j] = y.astype(o_ref.dtype)
            else:
                o_ref[pl.ds(j, k_out, stride=sbk), :] = y.astype(o_ref.dtype)


def _fft_stage(xs, x_sel, gmat, *, r_in, s_n, k_out, n_groups, n_cblk, transposed_out, real_out,
               out_dtype=F32, g2mat=None, tw=None, spec=None, spec_sel=None, gate=None, sbk=STAGE_ROWS,
               packed_in=False, packed_out=False, name="fft_stage"):
    n_in = len(xs)
    flat = s_n == 1
    sbk = 1 if flat else min(sbk, s_n)
    cb = LANES
    in_specs, args = [], []
    for x, sel in zip(xs, x_sel):
        if flat:
            in_specs.append(pl.BlockSpec((None, r_in, cb), lambda s, g, c, sel=sel: (sel(g, c)[0], 0, sel(g, c)[1])))
            args.append(x)
        else:
            xv = x.reshape(x.shape[0], x.shape[1] // s_n, s_n, x.shape[2])
            in_specs.append(pl.BlockSpec((1, r_in, sbk, cb),
                                         lambda s, g, c, sel=sel: (sel(g, c)[0], 0, s, sel(g, c)[1])))
            args.append(xv)
    in_specs.append(pl.BlockSpec(gmat.shape, lambda s, g, c: (0, 0, 0)))
    args.append(gmat)
    k_mid = gmat.shape[1] // (1 if real_out else 2)
    if g2mat is not None:
        in_specs.append(pl.BlockSpec(g2mat.shape, lambda s, g, c: (0, 0, 0)))
        args.append(g2mat)
    if tw is not None:
        for tarr in tw[:2]:
            in_specs.append(pl.BlockSpec((1, k_out, LANES), lambda s, g, c: (s, 0, 0)))
            args.append(tarr)
        for tarr in tw[2:]:
            in_specs.append(pl.BlockSpec((k_out, LANES), lambda s, g, c: (0, 0)))
            args.append(tarr)
    if spec is not None:
        *planes, nrm = spec
        for arr in planes:
            if flat:
                in_specs.append(pl.BlockSpec((k_mid, cb), lambda s, g, c: (0, spec_sel(g, c))))
                args.append(arr)
            else:
                in_specs.append(pl.BlockSpec((1, k_mid, sbk, cb), lambda s, g, c: (0, 0, s, spec_sel(g, c))))
                args.append(arr.reshape(1, k_mid, s_n, arr.shape[-1]))
        in_specs.append(pl.BlockSpec((1, cb), lambda s, g, c: (0, spec_sel(g, c))))
        args.append(nrm)
    if gate is not None:
        (ga, gblk), (za, zblk), bias = gate
        for arr, blk in ((ga, gblk), (za, zblk)):
            for bi in (0, 1):
                if flat:
                    in_specs.append(pl.BlockSpec((None, k_out, cb), lambda s, g, c, bi=bi, blk=blk: (bi, 0, blk + c)))
                    args.append(arr)
                else:
                    in_specs.append(pl.BlockSpec((1, k_out, sbk, cb),
                                                 lambda s, g, c, bi=bi, blk=blk: (bi, 0, s, blk + c)))
                    args.append(arr.reshape(arr.shape[0], k_out, s_n, arr.shape[-1]))
        in_specs.append(pl.BlockSpec((1, cb), lambda s, g, c: (0, c)))
        args.append(bias)
    ctot = n_cblk * cb
    if gate is not None:
        n_groups = 2
        if flat:
            oshape = (2, k_out, ctot)
            ospec = pl.BlockSpec((2, k_out, cb), lambda s, g, c: (0, 0, c))
        else:
            oshape = (2, k_out, s_n, ctot)
            ospec = pl.BlockSpec((2, k_out, sbk, cb), lambda s, g, c: (0, 0, s, c))
    elif flat:
        oshape = (n_groups, k_out, ctot)
        ospec = pl.BlockSpec((None, k_out, cb), lambda s, g, c: (g, 0, c))
    elif transposed_out:
        oshape = (n_groups, s_n, k_out, ctot)
        ospec = pl.BlockSpec((1, sbk, k_out, cb), lambda s, g, c: (g, s, 0, c))
    else:
        oshape = (n_groups, k_out, s_n, ctot)
        ospec = pl.BlockSpec((1, k_out, sbk, cb), lambda s, g, c: (g, 0, s, c))
    n_out = 1 if (real_out or gate is not None or packed_out) else 2
    if packed_out:
        out_dtype = jnp.int32
    kern = functools.partial(_stage_kernel, n_in=n_in, r_in=r_in, k_mid=k_mid, k_out=k_out, sbk=sbk,
                             tw=tw is not None, spec=spec is not None, second=g2mat is not None,
                             gate=gate is not None, real_out=real_out, transposed_out=transposed_out,
                             flat=flat, packed_in=packed_in, packed_spec=spec is not None and len(spec) == 2,
                             packed_out=packed_out)
    n_grid_groups = 1 if gate is not None else n_groups
    outs = pl.pallas_call(
        kern,
        grid=(s_n // sbk, n_grid_groups, n_cblk),
        in_specs=in_specs,
        out_specs=[ospec] * n_out,
        out_shape=[jax.ShapeDtypeStruct(oshape, out_dtype)] * n_out,
        compiler_params=_cparams("parallel", "parallel", "parallel"),
        name=name,
    )(*args)
    return [o.reshape(n_groups, -1, ctot) for o in outs]


def _split_len(n):
    if n <= 1024:
        return n, 1
    s = 128
    return n // s, s


def _fft_forward(xs, x_sel, n, n_rows, *, n_groups, n_cblk, halves=False, name="fwd"):
    n1, s = _split_len(n)
    n_in = len(xs)

    def mats(k_out, r_in, period):
        if not halves:
            return _dft_mats(k_out, r_in, period, -1.0, 1.0, n_in, False)
        g = _dft_mats(k_out, 2 * r_in, period, -1.0, 1.0, 1, False)[0]
        return jnp.stack([g[:, :r_in], g[:, r_in:]], axis=0)

    if s == 1:
        g = mats(n, n_rows, n)
        return _fft_stage(xs, x_sel, g, r_in=n_rows, s_n=1, k_out=n, n_groups=n_groups, n_cblk=n_cblk,
                          transposed_out=False, real_out=False, name=name + "_direct")
    r1 = n_rows // s
    g1 = mats(n1, r1, n1)
    tw = _twiddle(s, n1, n, -1.0, STAGE_ROWS)
    (a,) = _fft_stage(xs, x_sel, g1, r_in=r1, s_n=s, k_out=n1, n_groups=n_groups, n_cblk=n_cblk,
                      transposed_out=True, real_out=False, tw=tw, packed_out=True, name=name + "_s1")
    g2 = _dft_mats(s, s, s, -1.0, 1.0, 2, False)
    return _fft_stage([a], [lambda g, c: (g, c)], g2, r_in=s, s_n=n1, k_out=s, n_groups=n_groups,
                      n_cblk=n_cblk, transposed_out=False, real_out=False, packed_in=True, packed_out=True,
                      name=name + "_s2")


def _hyena(uc, k_halves, nrm, hy_bias):
    b, n, _ = uc.shape
    w = HYENA_WIDTH
    wblk = w // LANES
    nfft = 2 * n
    n1, s = _split_len(nfft)
    ident = lambda g, c: (g, c)
    k_spec = _fft_forward([k[None] for k in k_halves], [lambda g, c: (0, c)] * 2, nfft, n, n_groups=1,
                          n_cblk=HYENA_ORDER * wblk, halves=True, name="hy_filt_fft")
    z, zblk = uc, 2 * wblk
    for o in range(HYENA_ORDER):
        sel_r = lambda g, c, zblk=zblk: (0, zblk + c)
        sel_i = lambda g, c, zblk=zblk: (1, zblk + c)
        spec = tuple(p[0] for p in k_spec) + (nrm,)
        spec_sel = lambda g, c, o=o: o * wblk + c
        gate = ((uc, o * wblk), (z, zblk), hy_bias[o:o + 1])
        common = dict(n_groups=1, n_cblk=wblk, real_out=False)
        if s == 1:
            gf = _dft_mats(nfft, n, nfft, -1.0, 1.0, 2, False)
            gi = _dft_mats(n, nfft, nfft, 1.0, 1.0 / nfft, 2, False)
            (z,) = _fft_stage([z, z], [sel_r, sel_i], gf, r_in=n, s_n=1, k_out=n, transposed_out=False,
                              g2mat=gi, spec=spec, spec_sel=spec_sel, gate=gate, name="hy_direct", **common)
        else:
            r1 = n // s
            g1 = _dft_mats(n1, r1, n1, -1.0, 1.0, 2, False)
            (a,) = _fft_stage([z, z], [sel_r, sel_i], g1, r_in=r1, s_n=s, k_out=n1, transposed_out=True,
                              tw=_twiddle(s, n1, nfft, -1.0, STAGE_ROWS), packed_out=True, name="hy_s1",
                              **common)
            g2 = _dft_mats(s, s, s, -1.0, 1.0, 2, False)
            g3 = _dft_mats(s, s, s, 1.0, 1.0, 2, False)
            (q,) = _fft_stage([a], [ident], g2, r_in=s, s_n=n1, k_out=s, transposed_out=True,
                              g2mat=g3, tw=_twiddle(n1, s, nfft, 1.0, STAGE_ROWS), spec=spec,
                              spec_sel=spec_sel, packed_in=True, packed_out=True, name="hy_mid", **common)
            g4 = _dft_mats(n // s, n1, n1, 1.0, 1.0 / nfft, 2, False)
            (z,) = _fft_stage([q], [ident], g4, r_in=n1, s_n=s, k_out=n // s, transposed_out=False,
                              gate=gate, packed_in=True, name="hy_last", **common)
        zblk = 0
    return z.reshape(b * n, w)


def _fnet(pq):
    b, n, _ = pq.shape
    w = FNET_WIDTH
    wblk = w // LANES
    first = 3 * HYENA_WIDTH // LANES
    scale = 1.0 / math.sqrt(n * FNET_GROUP_DIM)
    sel_r = lambda g, c: (g, first + c)
    sel_i = lambda g, c: (g, first + wblk + c)
    ident = lambda g, c: (g, c)
    n1, s = _split_len(n)
    if s == 1:
        g = _dft_mats(n, n, n, -1.0, scale, 2, True)
        (y,) = _fft_stage([pq, pq], [sel_r, sel_i], g, r_in=n, s_n=1, k_out=n, n_groups=b, n_cblk=wblk,
                          transposed_out=False, real_out=True, name="fnet_direct")
        return y.reshape(b * n, w)
    g1 = _dft_mats(n1, n1, n1, -1.0, 1.0, 2, False)
    tw = _twiddle(s, n1, n, -1.0, STAGE_ROWS)
    (a,) = _fft_stage([pq, pq], [sel_r, sel_i], g1, r_in=n1, s_n=s, k_out=n1, n_groups=b, n_cblk=wblk,
                      transposed_out=True, real_out=False, tw=tw, packed_out=True, name="fnet_s1")
    g2 = _dft_mats(s, s, s, -1.0, scale, 2, True)
    (y,) = _fft_stage([a], [ident], g2, r_in=s, s_n=n1, k_out=s, n_groups=b, n_cblk=wblk,
                      transposed_out=False, real_out=True, packed_in=True, name="fnet_s2")
    return y.reshape(b * n, w)


def _merge_kernel(x_ref, gt_ref, a_ref, h_ref, f_ref, gate_ref, wa_ref, wh_ref, wf_ref, wo_ref, o_ref):
    d = D_MODEL
    m = gate_ref[:, 0:d].astype(F32) * _dot(a_ref[...], wa_ref[...])
    m = m + gate_ref[:, d:2 * d].astype(F32) * _dot(h_ref[...].astype(BF), wh_ref[...])
    m = m + gate_ref[:, 2 * d:3 * d].astype(F32) * _dot(f_ref[...].astype(BF), wf_ref[...])
    y = _dot(m.astype(BF), wo_ref[...])
    o_ref[...] = x_ref[...] + gt_ref[0] * y


def _merge(x2, gt, attn_o, hy_o, fn_o, gates, wa, wh, wf, wo, *, tm, mod_row):
    t, d = x2.shape
    tok = lambda i: (i, 0)
    full = lambda i: (0, 0)
    return pl.pallas_call(
        _merge_kernel,
        grid=(t // tm,),
        in_specs=[
            pl.BlockSpec((tm, d), tok),
            pl.BlockSpec((1, 1, d), lambda i: (mod_row(i), 0, 0)),
            pl.BlockSpec((tm, ATTN_WIDTH), tok),
            pl.BlockSpec((tm, HYENA_WIDTH), tok),
            pl.BlockSpec((tm, FNET_WIDTH), tok),
            pl.BlockSpec((tm, 3 * d), tok),
            pl.BlockSpec(wa.shape, full, pipeline_mode=pl.Buffered(1)),
            pl.BlockSpec(wh.shape, full, pipeline_mode=pl.Buffered(1)),
            pl.BlockSpec(wf.shape, full, pipeline_mode=pl.Buffered(1)),
            pl.BlockSpec(wo.shape, full, pipeline_mode=pl.Buffered(1)),
        ],
        out_specs=pl.BlockSpec((tm, d), tok),
        out_shape=jax.ShapeDtypeStruct((t, d), F32),
        compiler_params=_cparams("parallel"),
        name="merge",
    )(x2, gt, attn_o, hy_o, fn_o, gates, wa, wh, wf, wo)


def _ffn_kernel(x_ref, sh_ref, sc_ref, gt_ref, g_ref, wg_ref, wu_ref, wd_ref, o_ref, h_scr, acc_scr):
    f = pl.program_id(1)

    @pl.when(f == 0)
    def _():
        h_scr[...] = _modulated_norm(x_ref[...], g_ref[...], sc_ref[0], sh_ref[0]).astype(BF)
        acc_scr[...] = jnp.zeros_like(acc_scr)

    hb = h_scr[...]
    act = _silu(_dot(hb, wg_ref[...])) * _dot(hb, wu_ref[...])
    acc_scr[...] += _dot(act.astype(BF), wd_ref[...])

    @pl.when(f == pl.num_programs(1) - 1)
    def _():
        o_ref[...] = x_ref[...] + gt_ref[0] * acc_scr[...]


def _ffn(x2, sh, sc, gt, g, wg, wu, wd, *, tm, tf, mod_row):
    t, d = x2.shape
    ff = wg.shape[1]
    row3 = lambda i, f: (mod_row(i), 0, 0)
    wmode = dict(pipeline_mode=pl.Buffered(1)) if tf == ff else {}
    return pl.pallas_call(
        _ffn_kernel,
        grid=(t // tm, ff // tf),
        in_specs=[
            pl.BlockSpec((tm, d), lambda i, f: (i, 0)),
            pl.BlockSpec((1, 1, d), row3), pl.BlockSpec((1, 1, d), row3), pl.BlockSpec((1, 1, d), row3),
            pl.BlockSpec((1, d), lambda i, f: (0, 0)),
            pl.BlockSpec((d, tf), lambda i, f: (0, f), **wmode),
            pl.BlockSpec((d, tf), lambda i, f: (0, f), **wmode),
            pl.BlockSpec((tf, d), lambda i, f: (f, 0), **wmode),
        ],
        out_specs=pl.BlockSpec((tm, d), lambda i, f: (i, 0)),
        out_shape=jax.ShapeDtypeStruct((t, d), F32),
        scratch_shapes=[pltpu.VMEM((tm, d), BF), pltpu.VMEM((tm, d), F32)],
        compiler_params=_cparams("parallel", "arbitrary"),
        name="ffn_dense",
    )(x2, sh, sc, gt, g, wg, wu, wd)


def _top2(logits):
    lane = lax.broadcasted_iota(jnp.int32, logits.shape, 1)
    lg = jnp.where(lane < N_EXPERTS, logits, -jnp.inf)
    m1 = jnp.max(lg, axis=1, keepdims=True)
    i1 = jnp.min(jnp.where(lg == m1, lane, LANES), axis=1, keepdims=True)
    lg2 = jnp.where(lane == i1, -jnp.inf, lg)
    m2 = jnp.max(lg2, axis=1, keepdims=True)
    i2 = jnp.min(jnp.where(lg2 == m2, lane, LANES), axis=1, keepdims=True)
    e2 = jnp.exp(m2 - m1)
    w1 = 1.0 / (1.0 + e2)
    return i1, i2, w1, e2 * w1


GROUP_TILE = 256
GROUP_PAD = 16
GROUP_PIECES = GROUP_TILE // GROUP_PAD
SLOT_RADIX = 64.0


def _moe_group_kernel(x_ref, sh_ref, sc_ref, g_ref, wr_ref, tri_ref, xg_ref, ws_ref, slot_ref, cnt_ref,
                      h_scr, rows_scr, wm_scr):
    j = pl.program_id(1)
    tb = x_ref.shape[0]
    gt_rows = xg_ref.shape[1]

    @pl.when(j == 0)
    def _():
        h = _modulated_norm(x_ref[...], g_ref[...], sc_ref[0], sh_ref[0])
        h_scr[...] = h.astype(BF)
        i1, i2, w1, w2 = _top2(_dot3(h, wr_ref[...]))
        lane = lax.broadcasted_iota(jnp.int32, (tb, LANES), 1)
        oh0 = jnp.where(lane == i1, 1.0, 0.0)
        oh1 = jnp.where(lane == i2, 1.0, 0.0)
        c0 = jnp.sum(oh0, axis=0, keepdims=True)
        cnt = c0 + jnp.sum(oh1, axis=0, keepdims=True)
        tri = tri_ref[...]
        pre0 = _dot(tri, oh0.astype(BF))
        pre1 = _dot(tri, oh1.astype(BF)) + c0
        tiles = jnp.ceil(cnt * (1.0 / GROUP_PAD))
        upper = jnp.where(lax.broadcasted_iota(jnp.int32, (LANES, LANES), 0)
                          < lax.broadcasted_iota(jnp.int32, (LANES, LANES), 1), 1.0, 0.0).astype(BF)
        off = _dot(jnp.broadcast_to(tiles, (8, LANES)).astype(BF), upper)[0:1] * float(GROUP_PAD)
        slot0 = jnp.sum(oh0 * (off + pre0), axis=1, keepdims=True)
        slot1 = jnp.sum(oh1 * (off + pre1), axis=1, keepdims=True)
        slot_ref[0] = jnp.where(lane == 0, slot0, jnp.where(lane == 1, slot1, 0.0))
        cnt_ref[0] = jnp.broadcast_to(cnt, (8, LANES))
        hi0 = jnp.floor(slot0 * (1.0 / SLOT_RADIX))
        hi1 = jnp.floor(slot1 * (1.0 / SLOT_RADIX))
        digits = jnp.where(lane == 0, hi0, jnp.where(lane == 1, slot0 - SLOT_RADIX * hi0,
                           jnp.where(lane == 2, hi1, jnp.where(lane == 3, slot1 - SLOT_RADIX * hi1, 0.0))))
        sel = jnp.where(lax.broadcasted_iota(jnp.int32, (8, LANES), 0)
                        == lax.broadcasted_iota(jnp.int32, (8, LANES), 1), 1.0, 0.0).astype(BF)
        rows_scr[...] = _dot_nt(sel, digits.astype(BF))
        w1h, w1l = _split(w1)
        w1m, w1l = _split(w1 - w1h.astype(F32))
        w2h, w2l = _split(w2)
        w2m, w2l = _split(w2 - w2h.astype(F32))
        cols = [w1h, w1m, w1l, w2h, w2m, w2l]
        wm = jnp.zeros((tb, LANES), F32)
        for li, col in enumerate(cols):
            wm = jnp.where(lane == li, col.astype(F32), wm)
        wm_scr[...] = wm.astype(BF)

    rows = rows_scr[...]
    s0 = rows[0:1] * SLOT_RADIX + rows[1:2]
    s1 = rows[2:3] * SLOT_RADIX + rows[3:4]
    pos = (lax.broadcasted_iota(jnp.int32, (gt_rows, tb), 0) + j * gt_rows).astype(F32)
    g0 = jnp.where(pos == s0, 1.0, 0.0).astype(BF)
    g1 = jnp.where(pos == s1, 1.0, 0.0).astype(BF)
    xg_ref[0] = _dot(g0 + g1, h_scr[...]).astype(BF)
    lane_w = lax.broadcasted_iota(jnp.int32, (gt_rows, LANES), 1)
    wsum = (jnp.where(lane_w < 3, _dot(g0, wm_scr[...]), 0.0)
            + jnp.where((lane_w >= 3) & (lane_w < 6), _dot(g1, wm_scr[...]), 0.0))
    ws_ref[0] = jnp.broadcast_to(jnp.sum(wsum, axis=1, keepdims=True), (gt_rows, LANES))


def _moe_group(x2, sh, sc, g, wr_pad, *, tb, nt, mod_row):
    t, d = x2.shape
    nb = t // tb
    row3 = lambda b, j: (mod_row(b), 0, 0)
    tri = jnp.asarray(np.tril(np.ones((tb, tb), np.float32), -1)).astype(BF)
    return pl.pallas_call(
        _moe_group_kernel,
        grid=(nb, nt),
        in_specs=[
            pl.BlockSpec((tb, d), lambda b, j: (b, 0)),
            pl.BlockSpec((1, 1, d), row3), pl.BlockSpec((1, 1, d), row3),
            pl.BlockSpec((1, d), lambda b, j: (0, 0)),
            pl.BlockSpec((d, LANES), lambda b, j: (0, 0)),
            pl.BlockSpec((tb, tb), lambda b, j: (0, 0), pipeline_mode=pl.Buffered(1)),
        ],
        out_specs=[
            pl.BlockSpec((1, GROUP_TILE, d), lambda b, j: (b * nt + j, 0, 0)),
            pl.BlockSpec((1, GROUP_TILE, LANES), lambda b, j: (b * nt + j, 0, 0)),
            pl.BlockSpec((1, tb, LANES), lambda b, j: (b, 0, 0)),
            pl.BlockSpec((1, 8, LANES), lambda b, j: (b, 0, 0)),
        ],
        out_shape=[
            jax.ShapeDtypeStruct((nb * nt, GROUP_TILE, d), BF),
            jax.ShapeDtypeStruct((nb * nt, GROUP_TILE, LANES), F32),
            jax.ShapeDtypeStruct((nb, tb, LANES), F32),
            jax.ShapeDtypeStruct((nb, 8, LANES), F32),
        ],
        scratch_shapes=[pltpu.VMEM((tb, d), BF), pltpu.VMEM((8, tb), F32), pltpu.VMEM((tb, LANES), BF)],
        compiler_params=_cparams("parallel", "arbitrary"),
        name="moe_group",
    )(x2, sh, sc, g, wr_pad, tri)


def _moe_schedule(cnt, nh):
    np_ = GROUP_PIECES
    h = (cnt + GROUP_PAD - 1) // GROUP_PAD
    nb, ne = h.shape
    tot = h.sum(0)
    pairs = (tot + np_ - 1) // np_
    cum_p = jnp.cumsum(pairs)
    start_p = cum_p - pairs
    n_used = cum_p[-1]
    n_steps = (nb * nh + ne * (np_ - 1)) // np_
    q = jnp.minimum(jnp.arange(n_steps, dtype=jnp.int32), n_used - 1)
    e = jnp.sum(q[:, None] >= cum_p[None, :], axis=1).astype(jnp.int32)
    r = q - start_p[e]
    cum_b = jnp.cumsum(h, axis=0)
    first = jnp.cumsum(h, axis=1) - h

    idx = jnp.minimum(np_ * r[:, None] + jnp.arange(np_, dtype=jnp.int32)[None, :], tot[e][:, None] - 1)
    blk = jnp.sum(idx[:, :, None] >= cum_b.T[e][:, None, :], axis=2).astype(jnp.int32)
    in_blk = (first - (cum_b - h))[blk, e[:, None]]
    pieces = (blk * nh + in_blk + idx).T.reshape(-1).astype(jnp.int32)

    x = jnp.arange(nh, dtype=jnp.int32)[None, :]
    ex = jnp.sum(x[:, :, None] >= jnp.cumsum(h, axis=1)[:, None, :], axis=2).astype(jnp.int32)
    exc = jnp.minimum(ex, ne - 1)
    g = jnp.take_along_axis(cum_b - h - first, exc, axis=1) + x
    loc = np_ * (start_p[exc] + g // np_) + g % np_
    loc = jnp.where(ex < ne, loc, loc[:, 0:1])
    return (pieces, e, n_used.astype(jnp.int32).reshape(1), loc.reshape(-1).astype(jnp.int32),
            ((h.sum(1) + np_ - 1) // np_).astype(jnp.int32))


def _moe_expert_kernel(pc_ref, exp_ref, nused_ref, *refs):
    np_ = GROUP_PIECES
    x_refs, w_refs = refs[:np_], refs[np_:2 * np_]
    wg_ref, wu_ref, wd_ref, y_ref = refs[2 * np_:]

    @pl.when(pl.program_id(0) < nused_ref[0])
    def _():
        x = jnp.concatenate([r[0] for r in x_refs], axis=0)
        act = _silu(_dot(x, wg_ref[0])) * _dot(x, wu_ref[0])
        y = _dot(act.astype(BF), wd_ref[0])
        w = jnp.concatenate([r[0] for r in w_refs], axis=0)
        y_ref[0] = (y * jnp.concatenate([w] * (y.shape[1] // LANES), axis=1)).astype(BF)


def _moe_experts(pieces, step_exp, n_used, xg, ws, wg, wu, wd):
    d = xg.shape[-1]
    ff = wg.shape[2]
    np_ = GROUP_PIECES
    n_steps = step_exp.shape[0]
    xh = xg.reshape(-1, GROUP_PAD, d)
    wh = ws.reshape(-1, GROUP_PAD, LANES)
    pc3 = lambda k: (lambda i, pc, se, nu: (pc[k * n_steps + i], 0, 0))
    exp3 = lambda i, pc, se, nu: (se[i], 0, 0)
    return pl.pallas_call(
        _moe_expert_kernel,
        grid_spec=pltpu.PrefetchScalarGridSpec(
            num_scalar_prefetch=3,
            grid=(n_steps,),
            in_specs=(
                [pl.BlockSpec((1, GROUP_PAD, d), pc3(k)) for k in range(np_)]
                + [pl.BlockSpec((1, GROUP_PAD, LANES), pc3(k)) for k in range(np_)]
                + [pl.BlockSpec((1, d, ff), exp3), pl.BlockSpec((1, d, ff), exp3), pl.BlockSpec((1, ff, d), exp3)]
            ),
            out_specs=pl.BlockSpec((1, GROUP_TILE, d), lambda i, pc, se, nu: (jnp.minimum(i, nu[0] - 1), 0, 0)),
        ),
        out_shape=jax.ShapeDtypeStruct((n_steps, GROUP_TILE, d), BF),
        compiler_params=_cparams("arbitrary"),
        name="moe_experts",
    )(pieces, step_exp, n_used, *([xh] * np_), *([wh] * np_), wg, wu, wd)


def _moe_combine_kernel(nt_ref, loc_ref, x_ref, gt_ref, slot_ref, *refs):
    y_refs, o_ref, acc_scr = refs[:GROUP_PIECES], refs[GROUP_PIECES], refs[GROUP_PIECES + 1]
    b = pl.program_id(0)
    j = pl.program_id(1)
    tb = x_ref.shape[0]

    @pl.when(j == 0)
    def _():
        acc_scr[...] = jnp.zeros_like(acc_scr)

    @pl.when(j < nt_ref[b])
    def _():
        sl = slot_ref[0]
        pos = (lax.broadcasted_iota(jnp.int32, (tb, GROUP_TILE), 1) + j * GROUP_TILE).astype(F32)
        p = jnp.where((pos == sl[:, 0:1]) | (pos == sl[:, 1:2]), 1.0, 0.0).astype(BF)
        acc_scr[...] += _dot(p, jnp.concatenate([r[0] for r in y_refs], axis=0))

    @pl.when(j == pl.num_programs(1) - 1)
    def _():
        o_ref[...] = x_ref[...] + gt_ref[0] * acc_scr[...]


def _moe_combine(ntiles_b, loc, x2, gt, slots, yg, *, tb, nt, mod_row):
    t, d = x2.shape
    nb = t // tb
    np_ = GROUP_PIECES
    yh = yg.reshape(-1, GROUP_PAD, d)

    def piece3(k):
        return lambda b, j, n, lc: (lc[(b * nt + jnp.minimum(j, n[b] - 1)) * np_ + k], 0, 0)

    return pl.pallas_call(
        _moe_combine_kernel,
        grid_spec=pltpu.PrefetchScalarGridSpec(
            num_scalar_prefetch=2,
            grid=(nb, nt),
            in_specs=[
                pl.BlockSpec((tb, d), lambda b, j, n, lc: (b, 0)),
                pl.BlockSpec((1, 1, d), lambda b, j, n, lc: (mod_row(b), 0, 0)),
                pl.BlockSpec((1, tb, LANES), lambda b, j, n, lc: (b, 0, 0)),
            ] + [pl.BlockSpec((1, GROUP_PAD, d), piece3(k)) for k in range(np_)],
            out_specs=pl.BlockSpec((tb, d), lambda b, j, n, lc: (b, 0)),
            scratch_shapes=[pltpu.VMEM((tb, d), F32)],
        ),
        out_shape=jax.ShapeDtypeStruct((t, d), F32),
        compiler_params=_cparams("parallel", "arbitrary"),
        name="moe_combine",
    )(ntiles_b, loc, x2, gt, slots, *([yh] * np_))


def _moe(x2, sh, sc, gt, g, wr_pad, wg, wu, wd, *, tb, mod_row):
    nt = -(-(2 * tb + N_EXPERTS * (GROUP_PAD - 1)) // GROUP_TILE)
    xg, ws, slots, cnt = _moe_group(x2, sh, sc, g, wr_pad, tb=tb, nt=nt, mod_row=mod_row)
    counts = cnt[:, 0, :N_EXPERTS].astype(jnp.int32)
    pieces, step_exp, n_used, loc, ntiles_b = _moe_schedule(counts, nt * GROUP_PIECES)
    yg = _moe_experts(pieces, step_exp, n_used, xg, ws, wg, wu, wd)
    return _moe_combine(ntiles_b, loc, x2, gt, slots, yg, tb=tb, nt=nt, mod_row=mod_row)


def _rope_tables(seq_len):
    pos = np.arange(seq_len)
    prow = (pos // GRID_W).astype(np.float32)
    pcol = (pos % GRID_W).astype(np.float32)
    n_freq = HEAD_DIM // 4
    inv = (np.float32(ROPE_THETA) ** (-np.arange(n_freq, dtype=np.float32) / n_freq)).astype(np.float32)
    ar = (prow[:, None] * inv[None, :]).astype(np.float64)
    ac = (pcol[:, None] * inv[None, :]).astype(np.float64)
    cos = np.concatenate([np.cos(ar)] * 2 + [np.cos(ac)] * 2, axis=1)
    sin = np.concatenate([-np.sin(ar), np.sin(ar), -np.sin(ac), np.sin(ac)], axis=1)
    return (jnp.asarray(np.concatenate([cos, cos], axis=1), dtype=F32),
            jnp.asarray(np.concatenate([sin, sin], axis=1), dtype=F32))


def _head_sum_matrix():
    c = np.arange(ATTN_WIDTH)
    return jnp.asarray((c[:, None] // HEAD_DIM) == (c[None, :] // HEAD_DIM), dtype=F32).astype(BF)


def _fnet_channel_matrix():
    c = np.arange(FNET_WIDTH)
    same = (c[:, None] // FNET_GROUP_DIM) == (c[None, :] // FNET_GROUP_DIM)
    ang = 2.0 * np.pi * (((c[:, None] % FNET_GROUP_DIM) * (c[None, :] % FNET_GROUP_DIM)) % FNET_GROUP_DIM) / FNET_GROUP_DIM
    cb = np.where(same, np.cos(ang), 0.0)
    sb = np.where(same, np.sin(ang), 0.0)
    return jnp.asarray(np.concatenate([cb, -sb], axis=1), dtype=F32).astype(BF)


def _filter_features(n):
    t = np.linspace(0.0, 1.0, n)[:, None]
    w = 2.0 * np.pi * np.arange(n)[:, None] / n
    fb = np.linspace(1e-4, FILTER_BANDS - 1, FILTER_BANDS)
    z = np.concatenate([t, np.cos(fb * w), -np.sin(fb * w)], axis=-1)
    return jnp.asarray(np.pad(z, ((0, 0), (0, 64 - z.shape[1]))), dtype=F32)


def _decay_rates():
    d = jnp.abs(jnp.linspace(math.log(DECAY_TARGET) / SLOW_DECAY_PCT, math.log(DECAY_TARGET) / FAST_DECAY_PCT,
                             HYENA_WIDTH, dtype=F32))
    return jnp.concatenate([d] * HYENA_ORDER)[None, :]


def kernel(x, c, ctx, c_ctx, w_ada, b_ada, norm1_g, norm2_g, w_in, q_norm_g, k_norm_g, attn_sink,
           hy_conv_w, hy_conv_b, hy_filt_w1, hy_filt_b1, hy_filt_freq1, hy_filt_w2, hy_filt_b2,
           hy_filt_freq2, hy_filt_w3, hy_bias, w_proj_attn, w_proj_hyena, w_proj_fnet, w_out,
           ffn_w_gate, ffn_w_up, ffn_w_down, moe_router, moe_w_gate, moe_w_up, moe_w_down):
    b, seq, d = x.shape
    n_ctx = ctx.shape[1]
    depth = w_ada.shape[0]
    tm = 512
    tiles_per_seq = seq // tm

    cond8 = jnp.concatenate([c, c_ctx[None, :], jnp.zeros((8 - b - 1, d), F32)], axis=0)
    mods = _adaln(cond8, w_ada, b_ada)

    cos_l, sin_l = _rope_tables(seq)
    cos_c = jnp.ones((n_ctx, LANES), F32)
    sin_c = jnp.zeros((n_ctx, LANES), F32)
    gsum = _head_sum_matrix()
    mfn = _fnet_channel_matrix()
    deltas = _decay_rates()
    zfeat_l = _filter_features(seq)
    zfeat_c = _filter_features(n_ctx)

    lat_row = lambda i: i // tiles_per_seq
    ctx_row = lambda i: b
    lat_tab = lambda i: i % tiles_per_seq
    ctx_tab = lambda i: 0
    tm_c = min(tm, n_ctx)

    xs = x.reshape(b * seq, d)
    cs = ctx.reshape(b * n_ctx, d)
    for l in range(depth):
        last = l == depth - 1
        mod = lambda j: mods[l, :, j * d:(j + 1) * d].reshape(8, 1, d)
        w_in_bf = _layer_bf16(w_in, l, 256)
        qg = jnp.tile(q_norm_g[l], N_HEADS)[None, :]
        kg = jnp.tile(k_norm_g[l], N_KV_HEADS)[None, :]
        g1 = norm1_g[l][None, :]
        wa, wh, wf, wo = (w_proj_attn[l].astype(BF), w_proj_hyena[l].astype(BF),
                          w_proj_fnet[l].astype(BF), w_out[l].astype(BF))
        conv_w = hy_conv_w[l].reshape(3, -1)
        conv_b = hy_conv_b[l][None, :]
        w1p = jnp.pad(hy_filt_w1[l], ((0, 64 - hy_filt_w1.shape[1]), (0, 0)))
        filt = (w1p, hy_filt_b1[l][None, :], hy_filt_freq1[l][None, :], hy_filt_w2[l],
                hy_filt_b2[l][None, :], hy_filt_freq2[l][None, :], hy_filt_w3[l], deltas)

        qkv_c, upq_c, gates_c = _phase_a(
            cs, mod(0), mod(1), g1, w_in_bf, cos_c, sin_c, qg, kg, gsum, mfn, conv_w, conv_b,
            tm=tm_c, tiles_per_seq=n_ctx // tm_c, mod_row=ctx_row, tab_row=ctx_tab)
        qkv_c = qkv_c.reshape(b, n_ctx, -1)
        upq_c = upq_c.reshape(b, n_ctx, -1)

        qkv_l, upq_l, gates_l = _phase_a(
            xs, mod(0), mod(1), g1, w_in_bf, cos_l, sin_l, qg, kg, gsum, mfn, conv_w, conv_b,
            tm=tm, tiles_per_seq=tiles_per_seq, mod_row=lat_row, tab_row=lat_tab)
        upq_l = upq_l.reshape(b, seq, -1)
        attn_l = _attention(attn_sink[l], qkv_l.reshape(b, seq, -1), qkv_c, local=True, tq=512)
        kf_l, kb_l, nrm_l = _hy_filter(zfeat_l, *filt, tm=1024)
        hy_l = _hyena(upq_l, (kf_l, kb_l), nrm_l, hy_bias[l])
        fn_l = _fnet(upq_l)
        tm_merge = 2 * tm
        xs = _merge(xs, mod(2), attn_l.reshape(b * seq, -1), hy_l, fn_l, gates_l, wa, wh, wf, wo,
                    tm=tm_merge, mod_row=lambda t: t // (seq // tm_merge))

        if not last:
            attn_c = _attention(attn_sink[l], qkv_c, qkv_c, local=False, tq=n_ctx)
            kf_c, kb_c, nrm_c = _hy_filter(zfeat_c, *filt, tm=n_ctx)
            hy_c = _hyena(upq_c, (kf_c, kb_c), nrm_c, hy_bias[l])
            fn_c = _fnet(upq_c)
            cs = _merge(cs, mod(2), attn_c.reshape(b * n_ctx, -1), hy_c, fn_c, gates_c, wa, wh, wf, wo,
                        tm=tm_c, mod_row=ctx_row)

        g2 = norm2_g[l][None, :]
        i = l // 2
        if l % 2 == 0:
            wg, wu, wd = (_layer_bf16(ffn_w_gate, i, 512), _layer_bf16(ffn_w_up, i, 512),
                          _layer_bf16(ffn_w_down, i, D_FF // 4))
            run = lambda t2, rows, tmm: _ffn(t2, mod(3), mod(4), mod(5), g2, wg, wu, wd,
                                             tm=tmm, tf=D_FF, mod_row=rows)
        else:
            wr = jnp.pad(moe_router[i], ((0, 0), (0, LANES - N_EXPERTS)))
            wg, wu, wd = (_layer_bf16(moe_w_gate, i, 512), _layer_bf16(moe_w_up, i, 512),
                          _layer_bf16(moe_w_down, i, D_FF // 4))
            run = lambda t2, rows, tmm: _moe(t2, mod(3), mod(4), mod(5), g2, wr, wg, wu, wd,
                                             tb=tmm, mod_row=rows)
        tm_ffn = 512
        xs = run(xs, lambda t: t // (seq // tm_ffn), tm_ffn)
        if not last:
            cs = run(cs, ctx_row, min(tm_ffn, b * n_ctx))
    return xs.reshape(b, seq, d)
```
